```python
import jax
import jax.numpy as jnp
from jax import lax
import numpy as np

D_MODEL = 1024
BATCH = 4
SEQ = 4096
DEPTH = 2
DEC_BATCH = 8
DEC_SEQ = 32
PAST_LEN = 1024

CHUNK = 64
Q_BLOCK = 128
HEAD_DIM = 64
ROPE_THETA = 10000.0
NORM_EPS = 1e-6
A_HEADS = 6
IDX_HEADS = 4
IDX_DIM = 64
TOPK_MAX = 256
B_HEADS = 6
FORGET_BIAS_INIT = 3.0
POOL_WINDOWS = (2, 4, 8, 16)
POOL_GROUPS = 4
POOL_GROUP_DIM = 64
POOL_WIDTH = POOL_GROUPS * POOL_GROUP_DIM
POOL_HIST = max(POOL_WINDOWS) - 1
D_FF = 2816
N_EXPERTS = 8
TOP_K_EXPERTS = 2
D_FF_EXPERT = 1408
N_DENSE = (DEPTH + 1) // 2
N_MOE = DEPTH // 2
IN_SIZES = (A_HEADS * HEAD_DIM, HEAD_DIM, HEAD_DIM, IDX_HEADS * IDX_DIM, IDX_HEADS, IDX_DIM,
            B_HEADS * HEAD_DIM, B_HEADS * HEAD_DIM, B_HEADS * HEAD_DIM, B_HEADS, POOL_WIDTH, 3 * D_MODEL)
IN_SPLITS = tuple(int(v) for v in np.cumsum(IN_SIZES)[:-1])
N_IN = int(sum(IN_SIZES))

kernel_name = 'hybrid_streaming_encoder_step'


def rms_norm(x, g):
    xf = x.astype(jnp.float32)
    y = xf * lax.rsqrt(jnp.mean(xf * xf, axis=-1, keepdims=True) + NORM_EPS)
    return (y * g.astype(jnp.float32)).astype(x.dtype)


def rope(x, pos):
    half = x.shape[-1] // 2
    inv = ROPE_THETA ** (-jnp.arange(half, dtype=jnp.float32) / half)
    ang = pos.astype(jnp.float32)[:, None] * inv[None, :]
    cos = jnp.cos(ang)[None, :, None, :]
    sin = jnp.sin(ang)[None, :, None, :]
    x1 = x[..., :half].astype(jnp.float32)
    x2 = x[..., half:].astype(jnp.float32)
    return jnp.concatenate([x1 * cos - x2 * sin, x2 * cos + x1 * sin], axis=-1).astype(x.dtype)


def sweep_queries(fn, qs, pos_q):
    n_q = pos_q.shape[0]
    if n_q <= Q_BLOCK or n_q % Q_BLOCK:
        return fn(qs, pos_q)
    nb = n_q // Q_BLOCK
    qs_b = tuple(jnp.moveaxis(a.reshape(a.shape[0], nb, Q_BLOCK, *a.shape[2:]), 1, 0) for a in qs)
    out = lax.map(lambda args: fn(args[0], args[1]), (qs_b, pos_q.reshape(nb, Q_BLOCK)))
    out = jnp.moveaxis(out, 0, 1)
    return out.reshape(out.shape[0], n_q, *out.shape[3:])


def dsa_attention(q, qi, wi, pos_q, k, v, ki, pos_k, n_sel):
    scale = HEAD_DIM ** -0.5
    ki32 = ki.astype(jnp.float32)

    def block(qs, pq):
        qb, qib, wib = qs
        rel = jax.nn.relu(jnp.einsum('bqhd,bsd->bqhs', qib.astype(jnp.float32), ki32))
        score = jnp.einsum('bqh,bqhs->bqs', wib.astype(jnp.float32), rel)
        admissible = (pos_k[None, :] // CHUNK) <= (pq[:, None] // CHUNK)
        score = jnp.where(admissible[None], score, -jnp.inf)
        top_val, top_idx = lax.top_k(score, n_sel)
        kg = jax.vmap(lambda a, i: a[i])(k, top_idx)
        vg = jax.vmap(lambda a, i: a[i])(v, top_idx)
        logits = jnp.einsum('bqhd,bqkd->bqhk', qb, kg).astype(jnp.float32) * scale
        logits = jnp.where(jnp.isfinite(top_val)[:, :, None, :], logits, -jnp.inf)
        p = jax.nn.softmax(logits, axis=-1)
        return jnp.einsum('bqhk,bqkd->bqhd', p.astype(v.dtype), vg)

    return sweep_queries(block, (q, qi, wi), pos_q)


def fox_attention(q, cum_q, pos_q, k, v, cum_k, pos_k):
    scale = HEAD_DIM ** -0.5
    cum_k_t = jnp.transpose(cum_k, (0, 2, 1))

    def block(qs, pq):
        qb, cqb = qs
        logits = jnp.einsum('bqhd,bkhd->bhqk', qb, k).astype(jnp.float32) * scale
        decay = jnp.transpose(cqb, (0, 2, 1))[..., :, None] - cum_k_t[..., None, :]
        causal = pos_k[None, :] <= pq[:, None]
        logits = jnp.where(causal[None, None], logits + decay, -jnp.inf)
        p = jax.nn.softmax(logits, axis=-1)
        return jnp.einsum('bhqk,bkhd->bqhd', p.astype(v.dtype), v)

    return sweep_queries(block, (q, cum_q), pos_q)


def pool_mixer(u, hist, start_pos, pool_w, pool_scale):
    bsz, t_new, _ = u.shape
    full = jnp.concatenate([hist, u], axis=1)
    cs = jnp.pad(jnp.cumsum(full.astype(jnp.float32), axis=1), ((0, 0), (1, 0), (0, 0)))
    pos = start_pos + jnp.arange(t_new)
    end = POOL_HIST + 1
    outs = []
    for g, w in enumerate(POOL_WINDOWS):
        sl = slice(g * POOL_GROUP_DIM, (g + 1) * POOL_GROUP_DIM)
        s = cs[:, end:end + t_new, sl] - cs[:, end - w:end - w + t_new, sl]
        cnt = jnp.minimum(pos + 1, w).astype(jnp.float32)[None, :, None]
        outs.append(s / cnt)
    pooled = jnp.concatenate(outs, axis=-1)
    z = (pooled - u.astype(jnp.float32)).astype(u.dtype)
    z = z.reshape(bsz, t_new, POOL_GROUPS, POOL_GROUP_DIM)
    z = jnp.einsum('btgc,gcd->btgd', z, pool_w).reshape(bsz, t_new, POOL_WIDTH)
    return z * pool_scale, full[:, -POOL_HIST:]


def swiglu(h, wg, wu, wd):
    return (jax.nn.silu(h @ wg) * (h @ wu)) @ wd


def moe_ffn(h, router_w, router_b, wg, wu, wd):
    bsz, t_new, d = h.shape
    tok = h.reshape(-1, d)
    logits = (tok @ router_w).astype(jnp.float32) + router_b.astype(jnp.float32)
    top_val, top_idx = lax.top_k(logits, TOP_K_EXPERTS)
    wts = jax.nn.softmax(top_val, axis=-1)
    gate = jnp.sum(jax.nn.one_hot(top_idx, N_EXPERTS, dtype=jnp.float32) * wts[..., None], axis=1)
    out = jnp.zeros_like(tok)
    for e in range(N_EXPERTS):
        out = out + gate[:, e:e + 1].astype(tok.dtype) * swiglu(tok, wg[e], wu[e], wd[e])
    return out.reshape(bsz, t_new, d)


def trunk_layer(x, c, past, lw, layer):
    f32 = jnp.float32
    bsz, t_new, _ = x.shape
    mod = jnp.einsum('bd,de->be', jax.nn.silu(c), lw['ada_w']) + lw['ada_b']
    sh1, sc1, g1, sh2, sc2, g2 = [m[:, None, :] for m in jnp.split(mod, 6, axis=-1)]
    h = rms_norm(x, lw['norm_mix_g']) * (1 + sc1) + sh1
    z = jnp.einsum('btd,dn->btn', h, lw['w_in'])
    aq, ak, av, iq, iw, ik, bq, bk, bv, bf, cu, gl = jnp.split(z, IN_SPLITS, axis=-1)
    n_past = 0 if past is None else past[0].shape[1]
    pos = n_past + jnp.arange(t_new)
    aq = rope(aq.reshape(bsz, t_new, A_HEADS, HEAD_DIM), pos)
    ak = rope(ak[:, :, None, :], pos)[:, :, 0]
    iq = rope(iq.reshape(bsz, t_new, IDX_HEADS, IDX_DIM), pos)
    ik = rope(ik[:, :, None, :], pos)[:, :, 0]
    new_a = jnp.stack([ak, av, ik], axis=2)
    bq = bq.reshape(bsz, t_new, B_HEADS, HEAD_DIM)
    new_b = jnp.stack([bk.reshape(bsz, t_new, B_HEADS, HEAD_DIM),
                       bv.reshape(bsz, t_new, B_HEADS, HEAD_DIM)], axis=2)
    new_logf = jax.nn.log_sigmoid(bf.astype(f32) + lw['b_forget'].astype(f32))
    if past is None:
        a_all, b_all, logf_all = new_a, new_b, new_logf
        hist = jnp.zeros((bsz, POOL_HIST, POOL_WIDTH), cu.dtype)
    else:
        pa, pb, pl, pc = past
        a_all = jnp.concatenate([pa.astype(new_a.dtype), new_a], axis=1)
        b_all = jnp.concatenate([pb.astype(new_b.dtype), new_b], axis=1)
        logf_all = jnp.concatenate([pl.astype(f32), new_logf], axis=1)
        hist = pc.astype(cu.dtype)
    n_keys = a_all.shape[1]
    pos_k = jnp.arange(n_keys)
    n_sel = min(TOPK_MAX, n_keys // 4)
    oa = dsa_attention(aq, iq, iw, pos, a_all[:, :, 0], a_all[:, :, 1], a_all[:, :, 2], pos_k, n_sel)
    cum_logf = jnp.cumsum(logf_all, axis=1)
    ob = fox_attention(bq, cum_logf[:, n_past:], pos, b_all[:, :, 0], b_all[:, :, 1], cum_logf, pos_k)
    oc, new_pool = pool_mixer(cu, hist, n_past, lw['pool_w'], lw['pool_scale'])
    gates = jax.nn.sigmoid(gl.astype(f32)).astype(x.dtype).reshape(bsz, t_new, 3, D_MODEL)
    merged = (gates[:, :, 0] * (oa.reshape(bsz, t_new, A_HEADS * HEAD_DIM) @ lw['w_br_a'])
              + gates[:, :, 1] * (ob.reshape(bsz, t_new, B_HEADS * HEAD_DIM) @ lw['w_br_b'])
              + gates[:, :, 2] * (oc @ lw['w_br_c']))
    x = x + g1 * (merged @ lw['w_out'])
    h = rms_norm(x, lw['norm_ffn_g']) * (1 + sc2) + sh2
    if layer % 2 == 0:
        f = swiglu(h, lw['ffn_w_gate'], lw['ffn_w_up'], lw['ffn_w_down'])
    else:
        f = moe_ffn(h, lw['moe_router_w'], lw['moe_router_b'], lw['moe_w_gate'], lw['moe_w_up'], lw['moe_w_down'])
    x = x + g2 * f
    return x, (new_a, new_b, new_logf, new_pool)


def setup_inputs(seed: int = 0) -> dict:
    key = jax.random.key(seed)
    ks = jax.random.split(key, 64)
    D = D_MODEL

    def nrm(i, shape, scale):
        return scale * jax.random.normal(ks[i], shape, jnp.float32)

    return {
        'x_prompt': nrm(0, (BATCH, SEQ, D), 1.0),
        'x_sample': nrm(1, (DEC_BATCH, DEC_SEQ, D), 1.0),
        'cache_a_kvi': nrm(2, (DEPTH, DEC_BATCH, PAST_LEN, 3, HEAD_DIM), 1.0),
        'cache_b_kv': nrm(3, (DEPTH, DEC_BATCH, PAST_LEN, 2, B_HEADS, HEAD_DIM), 1.0),
        'cache_b_logf': jax.nn.log_sigmoid(FORGET_BIAS_INIT + nrm(4, (DEPTH, DEC_BATCH, PAST_LEN, B_HEADS), 1.0)),
        'state_c_pool': nrm(5, (DEPTH, DEC_BATCH, POOL_HIST, POOL_WIDTH), 1.0),
        'c_prompt': nrm(6, (BATCH, D), 1.0),
        'c_sample': nrm(7, (DEC_BATCH, D), 1.0),
        'ada_w': nrm(8, (DEPTH, D, 6 * D), 0.3 * D ** -0.5),
        'ada_b': nrm(9, (DEPTH, 6 * D), 0.02),
        'norm_mix_g': 1.0 + nrm(10, (DEPTH, D), 0.05),
        'w_in': nrm(11, (DEPTH, D, N_IN), D ** -0.5),
        'b_forget': FORGET_BIAS_INIT + nrm(12, (DEPTH, B_HEADS), 0.1),
        'pool_w': nrm(13, (DEPTH, POOL_GROUPS, POOL_GROUP_DIM, POOL_GROUP_DIM), POOL_GROUP_DIM ** -0.5),
        'pool_scale': 1.0 + nrm(14, (DEPTH, POOL_WIDTH), 0.1),
        'w_br_a': nrm(15, (DEPTH, A_HEADS * HEAD_DIM, D), (A_HEADS * HEAD_DIM) ** -0.5),
        'w_br_b': nrm(16, (DEPTH, B_HEADS * HEAD_DIM, D), (B_HEADS * HEAD_DIM) ** -0.5),
        'w_br_c': nrm(17, (DEPTH, POOL_WIDTH, D), POOL_WIDTH ** -0.5),
        'w_out': nrm(18, (DEPTH, D, D), D ** -0.5),
        'norm_ffn_g': 1.0 + nrm(19, (DEPTH, D), 0.05),
        'ffn_w_gate': nrm(20, (N_DENSE, D, D_FF), D ** -0.5),
        'ffn_w_up': nrm(21, (N_DENSE, D, D_FF), D ** -0.5),
        'ffn_w_down': nrm(22, (N_DENSE, D_FF, D), D_FF ** -0.5),
        'moe_router_w': nrm(23, (N_MOE, D, N_EXPERTS), D ** -0.5),
        'moe_router_b': nrm(24, (N_MOE, N_EXPERTS), 0.01),
        'moe_w_gate': nrm(25, (N_MOE, N_EXPERTS, D, D_FF_EXPERT), D ** -0.5),
        'moe_w_up': nrm(26, (N_MOE, N_EXPERTS, D, D_FF_EXPERT), D ** -0.5),
        'moe_w_down': nrm(27, (N_MOE, N_EXPERTS, D_FF_EXPERT, D), D_FF_EXPERT ** -0.5),
        'final_norm_g': 1.0 + nrm(28, (D,), 0.05),
    }


def reference(x_prompt, x_sample, cache_a_kvi, cache_b_kv, cache_b_logf, state_c_pool, c_prompt, c_sample,
              ada_w, ada_b, norm_mix_g, w_in, b_forget, pool_w, pool_scale, w_br_a, w_br_b, w_br_c, w_out,
              norm_ffn_g, ffn_w_gate, ffn_w_up, ffn_w_down, moe_router_w, moe_router_b, moe_w_gate,
              moe_w_up, moe_w_down, final_norm_g):
    xp, xs = x_prompt, x_sample
    pa, pb, pl, pc = [], [], [], []
    sa, sb, sl, sc = [], [], [], []
    for layer in range(DEPTH):
        lw = dict(ada_w=ada_w[layer], ada_b=ada_b[layer], norm_mix_g=norm_mix_g[layer], w_in=w_in[layer],
                  b_forget=b_forget[layer], pool_w=pool_w[layer], pool_scale=pool_scale[layer],
                  w_br_a=w_br_a[layer], w_br_b=w_br_b[layer], w_br_c=w_br_c[layer], w_out=w_out[layer],
                  norm_ffn_g=norm_ffn_g[layer])
        j = layer // 2
        if layer % 2 == 0:
            lw.update(ffn_w_gate=ffn_w_gate[j], ffn_w_up=ffn_w_up[j], ffn_w_down=ffn_w_down[j])
        else:
            lw.update(moe_router_w=moe_router_w[j], moe_router_b=moe_router_b[j], moe_w_gate=moe_w_gate[j],
                      moe_w_up=moe_w_up[j], moe_w_down=moe_w_down[j])
        xp, (na, nb, nl, npool) = trunk_layer(xp, c_prompt, None, lw, layer)
        pa.append(na); pb.append(nb); pl.append(nl); pc.append(npool)
        past = (cache_a_kvi[layer], cache_b_kv[layer], cache_b_logf[layer], state_c_pool[layer])
        xs, (na, nb, nl, npool) = trunk_layer(xs, c_sample, past, lw, layer)
        sa.append(na); sb.append(nb); sl.append(nl); sc.append(npool)
    y_prompt = rms_norm(xp, final_norm_g)
    y_sample = rms_norm(xs, final_norm_g)
    return (y_prompt, y_sample,
            jnp.stack(pa), jnp.stack(pb), jnp.stack(pl), jnp.stack(pc),
            jnp.stack(sa), jnp.stack(sb), jnp.stack(sl), jnp.stack(sc))
```

```python
import functools

import jax
import jax.numpy as jnp
import numpy as np
from jax import lax
from jax.experimental import pallas as pl
from jax.experimental.pallas import tpu as pltpu

F32 = jnp.float32
BF16 = jnp.bfloat16
I32 = jnp.int32

D_MODEL = 1024
CHUNK = 64
HEAD_DIM = 64
HALF = HEAD_DIM // 2
ROPE_THETA = 10000.0
NORM_EPS = 1e-6
A_HEADS = 6
IDX_HEADS = 4
TOPK_MAX = 256
B_HEADS = 6
POOL_WINDOWS = (2, 4, 8, 16)
POOL_GROUP_DIM = 64
POOL_WIDTH = 256
POOL_HIST = 15
N_EXPERTS = 8
LANES = 128
QK_SCALE = HEAD_DIM ** -0.5
VMEM_LIMIT = 56 * 1024 * 1024
NEG_INF = float("-inf")
INT_MIN = -2 ** 31

C_AQ, C_IQ, C_A, C_KVI, C_BQK, C_BV, C_CU, C_SM, C_GATE, C_END = (
    0, 768, 1280, 1792, 2816, 3584, 4736, 4992, 5120, 8192)


def _cparams(sem):
    return pltpu.CompilerParams(dimension_semantics=sem, vmem_limit_bytes=VMEM_LIMIT)


def _const_spec(shape):
    nd = len(shape)
    return pl.BlockSpec(shape, lambda *_: (0,) * nd, pipeline_mode=pl.Buffered(1))


def _nt_dot(a, b):
    return lax.dot_general(a, b, (((1,), (1,)), ((), ())), preferred_element_type=F32)


def _lane_iota(shape):
    return lax.broadcasted_iota(I32, shape, len(shape) - 1)


def _ada_kernel(c_ref, w_ref, b_ref, o_ref):
    c = c_ref[...]
    s = c * jax.nn.sigmoid(c)
    o_ref[...] = jnp.dot(s, w_ref[...], preferred_element_type=F32,
                         precision=lax.Precision.HIGHEST) + b_ref[...]


def _ada(c_all, ada_w, ada_b):
    depth, d, n = ada_w.shape
    rows = c_all.shape[0]
    tn = 1536
    return pl.pallas_call(
        _ada_kernel,
        grid=(depth, n // tn),
        in_specs=[pl.BlockSpec((rows, d), lambda l, j: (0, 0)),
                  pl.BlockSpec((None, d, tn), lambda l, j: (l, 0, j)),
                  pl.BlockSpec((None, 1, tn), lambda l, j: (l, 0, j))],
        out_specs=pl.BlockSpec((None, rows, tn), lambda l, j: (l, 0, j)),
        out_shape=jax.ShapeDtypeStruct((depth, rows, n), F32),
        compiler_params=_cparams(("arbitrary", "arbitrary")),
        name="ada_mod",
    )(c_all, ada_w, ada_b.reshape(depth, 1, n))


def _in_kernel(x_ref, sc_ref, sh_ref, g_ref, w_ref, bf_ref, cos_ref, sin_ref,
               aq_ref, iq_ref, na_ref, kvi_ref, nb_ref, bqk_ref, bv1_ref, cu_ref, sm_ref, gate_ref):
    x = x_ref[...]
    ms = jnp.mean(x * x, axis=-1, keepdims=True)
    y = x * lax.rsqrt(ms + NORM_EPS) * g_ref[...]
    h = (y * (1.0 + sc_ref[...]) + sh_ref[...]).astype(BF16)
    tm = x.shape[0]

    def mm(a, b):
        return jnp.dot(h, w_ref[:, a:b], preferred_element_type=F32)

    cos = cos_ref[...]
    sin = sin_ref[...]
    ones = jnp.ones_like(cos)
    zeros = jnp.zeros_like(cos)
    lane = _lane_iota((tm, LANES))
    low = lane < HEAD_DIM

    z = mm(C_AQ, C_IQ)
    cos3 = jnp.concatenate([cos] * 3, axis=1)
    sin3 = jnp.concatenate([sin] * 3, axis=1)
    aq_ref[...] = ((z[:, :384] * cos3 + z[:, 384:] * sin3) * QK_SCALE).astype(BF16)

    z = mm(C_IQ, C_A)
    cos2 = jnp.concatenate([cos] * 2, axis=1)
    sin2 = jnp.concatenate([sin] * 2, axis=1)
    iq_ref[...] = (z[:, :256] * cos2 + z[:, 256:] * sin2).astype(BF16)

    z = mm(C_A, C_KVI)
    cos_a = jnp.concatenate([jnp.where(low, cos, ones), jnp.where(low, cos, zeros)], axis=1)
    sin_a = jnp.concatenate([jnp.where(low, sin, zeros), jnp.where(low, sin, zeros)], axis=1)
    na = z[:, :256] * cos_a + z[:, 256:] * sin_a
    na_ref[...] = na[:, :192]

    z = mm(C_KVI, C_BQK)
    kk = z[:, 0:128] * cos + z[:, 512:640] * sin
    v1 = jnp.where(low, z[:, 128:256], ones)
    v2 = jnp.where(low, ones, z[:, 256:384])
    ikk = z[:, 384:512] * cos + z[:, 896:1024] * sin
    kvi_ref[...] = jnp.concatenate([kk, v1, v2, ikk], axis=1).astype(BF16)

    z = mm(C_BQK, C_BV)
    nb_ref[:, :384] = z[:, 384:]
    bqk_ref[...] = jnp.concatenate([z[:, :384] * QK_SCALE, z[:, 384:]], axis=1).astype(BF16)

    z = mm(C_BV, C_CU)
    nb_ref[:, 384:] = z[:, :384]
    lane3 = _lane_iota((tm, 768)) % 256
    pad = (lane3 >= HEAD_DIM) & (lane3 < 3 * HEAD_DIM)
    bv1_ref[...] = jnp.where(pad, 1.0, z[:, 384:]).astype(BF16)

    cu_ref[...] = mm(C_CU, C_SM)

    z = mm(C_SM, C_GATE)
    t = z + bf_ref[...]
    logf = jnp.minimum(t, 0.0) - jnp.log1p(jnp.exp(-jnp.abs(t)))
    sm_ref[...] = jnp.where(lane < B_HEADS, logf, z)

    for c in range(3):
        gate_ref[:, c * D_MODEL:(c + 1) * D_MODEL] = jax.nn.sigmoid(
            mm(C_GATE + c * D_MODEL, C_GATE + (c + 1) * D_MODEL))


def _swap_halves_cols(w, n_heads):
    d = w.shape[0]
    w4 = w.reshape(d, n_heads, 2, HALF)
    return jnp.concatenate([w4[:, :, 1], w4[:, :, 0]], axis=-1).reshape(d, n_heads * HEAD_DIM)


def _in_weights(w_in_l, b_forget_l):
    d = w_in_l.shape[0]
    sizes = (384, 64, 64, 256, 4, 64, 384, 384, 384, 6, 256, 3072)
    offs = np.concatenate([[0], np.cumsum(sizes)])
    aq, ak, av, iq, iw, ik, bq, bk, bv, bfw, cu, gl = [w_in_l[:, offs[i]:offs[i + 1]] for i in range(12)]
    z64 = jnp.zeros((d, 64), F32)
    aks, iks = _swap_halves_cols(ak, 1), _swap_halves_cols(ik, 1)
    bv6 = bv.reshape(d, 3, 2, HEAD_DIM)
    z3 = jnp.zeros((d, 3, HEAD_DIM), F32)
    bv_pad = jnp.stack([bv6[:, :, 0], z3, z3, bv6[:, :, 1]], axis=2).reshape(d, 768)
    small = jnp.concatenate([bfw, jnp.zeros((d, 2), F32), iw, jnp.zeros((d, LANES - 12), F32)], axis=1)
    cols = [aq, _swap_halves_cols(aq, A_HEADS),
            iq, _swap_halves_cols(iq, IDX_HEADS),
            ak, av, ik, z64, aks, z64, iks, z64,
            ak, ak, av, z64, z64, av, ik, ik, aks, aks, z64, z64, z64, z64, iks, iks,
            bq, bk,
            bv, bv_pad,
            cu, small, gl]
    w = jnp.concatenate(cols, axis=1).astype(BF16)
    assert w.shape[1] == C_END, w.shape
    bias = jnp.concatenate([b_forget_l, jnp.zeros((LANES - B_HEADS,), F32)]).reshape(1, LANES)
    return w, bias


def _in_proj(x, sc, sh, g, w, bias, cos, sin, tm):
    bsz, t, d = x.shape
    mrows = sc.shape[1]
    mblk = 1 if mrows == 1 else tm
    mod_spec = pl.BlockSpec((None, mblk, d), (lambda b, i: (b, 0, 0)) if mrows == 1 else (lambda b, i: (b, i, 0)))
    widths = (384, 256, 192, 512, 768, 768, 768, 256, LANES, 3 * D_MODEL)
    dtypes = (BF16, BF16, F32, BF16, F32, BF16, BF16, F32, F32, F32)
    return pl.pallas_call(
        _in_kernel,
        grid=(bsz, t // tm),
        in_specs=[pl.BlockSpec((None, tm, d), lambda b, i: (b, i, 0)), mod_spec, mod_spec,
                  _const_spec((1, d)), _const_spec(w.shape), _const_spec((1, LANES)),
                  pl.BlockSpec((tm, LANES), lambda b, i: (i, 0)),
                  pl.BlockSpec((tm, LANES), lambda b, i: (i, 0))],
        out_specs=[pl.BlockSpec((None, tm, n), lambda b, i: (b, i, 0)) for n in widths],
        out_shape=[jax.ShapeDtypeStruct((bsz, t, n), dt) for n, dt in zip(widths, dtypes)],
        compiler_params=_cparams(("parallel", "parallel")),
        name="in_proj",
    )(x, sc, sh, g.reshape(1, d), w, bias, cos, sin)


def _cum_kernel(x_ref, o_ref, carry_ref):
    @pl.when(pl.program_id(1) == 0)
    def _():
        carry_ref[...] = jnp.zeros_like(carry_ref)

    tc = x_ref.shape[0]
    r = lax.broadcasted_iota(I32, (tc, tc), 0)
    c = lax.broadcasted_iota(I32, (tc, tc), 1)
    tri = jnp.where(c <= r, 1.0, 0.0).astype(F32)
    cum = jnp.dot(tri, x_ref[...], preferred_element_type=F32,
                  precision=lax.Precision.HIGHEST) + carry_ref[0:1, :]
    o_ref[...] = cum
    carry_ref[...] = jnp.broadcast_to(cum[tc - 1:tc, :], carry_ref.shape)


def _cumsum_rows(x, tc):
    bsz, t, n = x.shape
    return pl.pallas_call(
        _cum_kernel,
        grid=(bsz, t // tc),
        in_specs=[pl.BlockSpec((None, tc, n), lambda b, i: (b, i, 0))],
        out_specs=pl.BlockSpec((None, tc, n), lambda b, i: (b, i, 0)),
        out_shape=jax.ShapeDtypeStruct((bsz, t, n), F32),
        scratch_shapes=[pltpu.VMEM((8, n), F32)],
        compiler_params=_cparams(("parallel", "arbitrary")),
        name="logf_cumsum",
    )(x)


def _dsa_kernel(aq_ref, iq_ref, sm_ref, kvi_ref, o_ref, key_ref, *, tq, tk, n_keys, q_pos0, n_sel):
    i = pl.program_id(1)
    pos_first = q_pos0 + i * tq
    last_chunk = (pos_first + tq - 1) // CHUNK
    n_adm = jnp.minimum((last_chunk + 1) * CHUNK, n_keys)
    nkb = (n_adm + tk - 1) // tk

    lane_q = _lane_iota((tq, LANES))
    low_q = lane_q < HEAD_DIM
    row_pos = pos_first + lax.broadcasted_iota(I32, (tq, tk), 0)
    row_chunk = row_pos // CHUNK

    iq = iq_ref[...]
    zq = jnp.zeros((tq, LANES), BF16)
    q4 = jnp.concatenate(
        [jnp.where(low_q if hh == 0 else ~low_q, iq[:, p * LANES:(p + 1) * LANES], zq)
         for p in range(2) for hh in range(2)], axis=0)
    sm = sm_ref[...]
    wcols = [sm[:, 8 + hd:9 + hd] for hd in range(IDX_HEADS)]

    def score_body(kb, carry):
        ikk = kvi_ref[pl.ds(pl.multiple_of(kb * tk, tk), tk), 384:512]
        s4 = _nt_dot(q4, ikk)
        score = wcols[0] * jnp.maximum(s4[0:tq], 0.0)
        for hd in range(1, IDX_HEADS):
            score = score + wcols[hd] * jnp.maximum(s4[hd * tq:(hd + 1) * tq], 0.0)
        score = score + 0.0
        s_idx = kb * tk + _lane_iota((tq, tk))
        adm = ((s_idx // CHUNK) <= row_chunk) & (s_idx < n_keys)
        score = jnp.where(adm, score, NEG_INF)
        bits = lax.bitcast_convert_type(score, I32)
        key_ref[kb] = jnp.where(bits < 0, bits ^ 0x7FFFFFFF, bits)
        return carry

    lax.fori_loop(0, nkb, score_body, 0)

    def count_ge(cand, strict):
        def body(kb, acc):
            blk = key_ref[kb]
            hit = (blk > cand) if strict else (blk >= cand)
            return acc + jnp.where(hit, 1.0, 0.0)
        acc = lax.fori_loop(0, nkb, body, jnp.zeros((tq, tk), F32))
        return jnp.sum(acc, axis=1, keepdims=True)

    def bit_body(b, t_b):
        bit = lax.shift_left(jnp.int32(1), 31 - b)
        cand_b = t_b | bit
        cnt = count_ge(cand_b ^ INT_MIN, False)
        return jnp.where(cnt >= n_sel, cand_b, t_b)

    t_b = lax.fori_loop(0, 32, bit_body, jnp.zeros((tq, 1), I32))
    thr = t_b ^ INT_MIN
    need = n_sel - count_ge(thr, True)

    aq = aq_ref[...]
    q6 = jnp.concatenate(
        [jnp.where(low_q if hh == 0 else ~low_q, aq[:, p * LANES:(p + 1) * LANES], zq)
         for p in range(3) for hh in range(2)], axis=0)
    r_i = lax.broadcasted_iota(I32, (tk, tk), 0)
    c_i = lax.broadcasted_iota(I32, (tk, tk), 1)
    upper = jnp.where(r_i <= c_i, 1.0, 0.0).astype(BF16)
    key_ninf = jnp.int32(-2 ** 31 + 0x7FFFFF)
    key_pinf = jnp.int32(0x7F800000)

    def attn_body(kb, carry):
        eq_seen, ms, accs = carry
        blk = key_ref[kb]
        finite = (blk > key_ninf) & (blk < key_pinf)
        eq = blk == thr
        eq_f = jnp.where(eq, 1.0, 0.0)
        pref = jnp.dot(eq_f.astype(BF16), upper, preferred_element_type=F32) + eq_seen
        sel = ((blk > thr) | (eq & (pref <= need))) & finite
        bias = jnp.where(sel, 0.0, NEG_INF)
        off = pl.multiple_of(kb * tk, tk)
        kk = kvi_ref[pl.ds(off, tk), 0:128]
        logits = _nt_dot(q6, kk)
        new_ms, new_accs = [], []
        for hd in range(A_HEADS):
            vv = kvi_ref[pl.ds(off, tk), (128 if hd % 2 == 0 else 256):(256 if hd % 2 == 0 else 384)]
            lg = logits[hd * tq:(hd + 1) * tq] + bias
            m_old = ms[hd]
            m_new = jnp.maximum(m_old, jnp.max(lg, axis=1, keepdims=True))
            m_safe = jnp.where(m_new == NEG_INF, 0.0, m_new)
            p = jnp.exp(lg - m_safe)
            alpha = jnp.exp(m_old - m_safe)
            new_accs.append(alpha * accs[hd] + jnp.dot(p.astype(BF16), vv, preferred_element_type=F32))
            new_ms.append(m_new)
        eq_seen = eq_seen + jnp.sum(eq_f, axis=1, keepdims=True)
        return eq_seen, tuple(new_ms), tuple(new_accs)

    init = (jnp.zeros((tq, 1), F32),
            tuple(jnp.full((tq, 1), NEG_INF, F32) for _ in range(A_HEADS)),
            tuple(jnp.zeros((tq, LANES), F32) for _ in range(A_HEADS)))
    _, _, accs = lax.fori_loop(0, nkb, attn_body, init)

    outs = []
    for p in range(3):
        ev, od = accs[2 * p], accs[2 * p + 1]
        o_ev = ev / ev[:, HEAD_DIM:HEAD_DIM + 1]
        o_od = od / od[:, 0:1]
        outs.append(jnp.where(low_q, o_ev, o_od))
    o_ref[...] = jnp.concatenate(outs, axis=1).astype(o_ref.dtype)


def _dsa(aq, iq, sm, kvi, *, tq, tk, n_keys, q_pos0, n_sel):
    bsz, t_q, _ = aq.shape
    lpad = kvi.shape[1]
    kern = functools.partial(_dsa_kernel, tq=tq, tk=tk, n_keys=n_keys, q_pos0=q_pos0, n_sel=n_sel)
    return pl.pallas_call(
        kern,
        grid=(bsz, t_q // tq),
        in_specs=[pl.BlockSpec((None, tq, 384), lambda b, i: (b, i, 0)),
                  pl.BlockSpec((None, tq, 256), lambda b, i: (b, i, 0)),
                  pl.BlockSpec((None, tq, LANES), lambda b, i: (b, i, 0)),
                  pl.BlockSpec((None, lpad, 512), lambda b, i: (b, 0, 0))],
        out_specs=pl.BlockSpec((None, tq, 384), lambda b, i: (b, i, 0)),
        out_shape=jax.ShapeDtypeStruct((bsz, t_q, 384), BF16),
        scratch_shapes=[pltpu.VMEM((lpad // tk, tq, tk), I32)],
        compiler_params=_cparams(("parallel", "arbitrary")),
        name="dsa_attention",
    )(aq, iq, sm, kvi)


def _fox_kernel(q_ref, k_ref, v_ref, cq_ref, ck_ref, o_ref, *, tq, tk, n_keys, q_pos0):
    i = pl.program_id(2)
    pos_first = q_pos0 + i * tq
    nkb = (jnp.minimum(pos_first + tq, n_keys) + tk - 1) // tk
    q = q_ref[...]
    lane_q = _lane_iota((tq, LANES))
    low_q = lane_q < HEAD_DIM
    zq = jnp.zeros_like(q)
    row_pos = pos_first + lax.broadcasted_iota(I32, (tq, tk), 0)
    col = _lane_iota((tq, tk))
    outs = []
    for hh in range(2):
        qm = jnp.where(low_q if hh == 0 else ~low_q, q, zq)
        cq = cq_ref[:, hh:hh + 1]

        def body(kb, carry, qm=qm, cq=cq, hh=hh):
            m_old, acc = carry
            off = pl.multiple_of(kb * tk, tk)
            kblk = k_ref[pl.ds(off, tk), :]
            vblk = v_ref[pl.ds(off, tk), hh * LANES:(hh + 1) * LANES]
            ck = ck_ref[kb][hh:hh + 1, :]
            lg = _nt_dot(qm, kblk) + (cq - ck)
            lg = jnp.where(kb * tk + col <= row_pos, lg, NEG_INF)
            m_new = jnp.maximum(m_old, jnp.max(lg, axis=1, keepdims=True))
            m_safe = jnp.where(m_new == NEG_INF, 0.0, m_new)
            p = jnp.exp(lg - m_safe)
            alpha = jnp.exp(m_old - m_safe)
            acc = alpha * acc + jnp.dot(p.astype(BF16), vblk, preferred_element_type=F32)
            return m_new, acc

        _, acc = lax.fori_loop(0, nkb, body, (jnp.full((tq, 1), NEG_INF, F32), jnp.zeros((tq, LANES), F32)))
        outs.append(acc / (acc[:, HEAD_DIM:HEAD_DIM + 1] if hh == 0 else acc[:, 0:1]))
    o_ref[...] = jnp.where(low_q, outs[0], outs[1]).astype(o_ref.dtype)


def _fox(q, k, v1, cq, ck, *, tq, tk, n_keys, q_pos0, k_blk0):
    bsz, t_q, _ = q.shape
    lpad = k.shape[1]
    kern = functools.partial(_fox_kernel, tq=tq, tk=tk, n_keys=n_keys, q_pos0=q_pos0)
    return pl.pallas_call(
        kern,
        grid=(bsz, 3, t_q // tq),
        in_specs=[pl.BlockSpec((None, tq, LANES), lambda b, p, i: (b, i, p)),
                  pl.BlockSpec((None, lpad, LANES), lambda b, p, i: (b, 0, p + k_blk0)),
                  pl.BlockSpec((None, lpad, 2 * LANES), lambda b, p, i: (b, 0, p)),
                  pl.BlockSpec((None, None, tq, 2), lambda b, p, i: (b, p, i, 0)),
                  pl.BlockSpec((None, None, lpad // tk, 2, tk), lambda b, p, i: (b, p, 0, 0, 0))],
        out_specs=pl.BlockSpec((None, tq, LANES), lambda b, p, i: (b, i, p)),
        out_shape=jax.ShapeDtypeStruct((bsz, t_q, 384), BF16),
        compiler_params=_cparams(("parallel", "parallel", "arbitrary")),
        name="fox_attention",
    )(q, k, v1, cq, ck)


def _pool_kernel(cur_ref, prev_ref, hist_ref, w_ref, s_ref, o_ref, ext, *, tc, start_pos):
    i = pl.program_id(1)
    cur = cur_ref[...]
    ext[0:16, :] = jnp.where(i == 0, hist_ref[...], prev_ref[tc - 16:, :])
    ext[16:, :] = cur
    pos = start_pos + i * tc + lax.broadcasted_iota(I32, (tc, POOL_WIDTH), 0)
    lane = _lane_iota((tc, POOL_WIDTH))
    run = cur
    pooled = jnp.zeros_like(cur)
    k = 1
    for g, w in enumerate(POOL_WINDOWS):
        while k < w:
            run = run + ext[16 - k:16 - k + tc, :]
            k += 1
        cnt = jnp.minimum(pos + 1, w).astype(F32)
        in_group = (lane >= g * POOL_GROUP_DIM) & (lane < (g + 1) * POOL_GROUP_DIM)
        pooled = jnp.where(in_group, run / cnt, pooled)
    z = (pooled - cur).astype(BF16)
    o_ref[...] = (jnp.dot(z, w_ref[...], preferred_element_type=F32) * s_ref[...]).astype(o_ref.dtype)


def _pool(cu, hist16, w_bd, scale, *, tc, start_pos):
    bsz, t, n = cu.shape
    kern = functools.partial(_pool_kernel, tc=tc, start_pos=start_pos)
    return pl.pallas_call(
        kern,
        grid=(bsz, t // tc),
        in_specs=[pl.BlockSpec((None, tc, n), lambda b, i: (b, i, 0)),
                  pl.BlockSpec((None, tc, n), lambda b, i: (b, jnp.maximum(i - 1, 0), 0)),
                  pl.BlockSpec((None, 16, n), lambda b, i: (b, 0, 0)),
                  _const_spec((n, n)), _const_spec((1, n))],
        out_specs=pl.BlockSpec((None, tc, n), lambda b, i: (b, i, 0)),
        out_shape=jax.ShapeDtypeStruct((bsz, t, n), BF16),
        scratch_shapes=[pltpu.VMEM((16 + tc, n), F32)],
        compiler_params=_cparams(("parallel", "arbitrary")),
        name="pool_mixer",
    )(cu, cu, hist16, w_bd, scale.reshape(1, n))


def _route(logits):
    lane = _lane_iota(logits.shape).astype(F32)
    lg = jnp.where(lane < N_EXPERTS, logits, NEG_INF)
    m1 = jnp.max(lg, axis=1, keepdims=True)
    i1 = jnp.min(jnp.where(lg == m1, lane, float(LANES)), axis=1, keepdims=True)
    hot1 = lane == i1
    lg2 = jnp.where(hot1, NEG_INF, lg)
    m2 = jnp.max(lg2, axis=1, keepdims=True)
    i2 = jnp.min(jnp.where(lg2 == m2, lane, float(LANES)), axis=1, keepdims=True)
    hot2 = lane == i2
    e2 = jnp.exp(m2 - m1)
    den = 1.0 + e2
    return jnp.where(hot1, 1.0 / den, 0.0) + jnp.where(hot2, e2 / den, 0.0)


def _merge_kernel(x_ref, oa_ref, ob_ref, oc_ref, gate_ref, g1_ref, sc2_ref, sh2_ref, g_ref,
                  wa_ref, wb_ref, wc_ref, wo_ref, *rest, moe):
    if moe:
        rw_ref, rb_ref, xo_ref, h_ref, gw_ref = rest
    else:
        xo_ref, h_ref = rest
    d = D_MODEL
    merged = (gate_ref[:, 0:d] * jnp.dot(oa_ref[...], wa_ref[...], preferred_element_type=F32)
              + gate_ref[:, d:2 * d] * jnp.dot(ob_ref[...], wb_ref[...], preferred_element_type=F32)
              + gate_ref[:, 2 * d:3 * d] * jnp.dot(oc_ref[...], wc_ref[...], preferred_element_type=F32))
    x = x_ref[...] + g1_ref[...] * jnp.dot(merged.astype(BF16), wo_ref[...], preferred_element_type=F32)
    xo_ref[...] = x
    ms = jnp.mean(x * x, axis=-1, keepdims=True)
    y = x * lax.rsqrt(ms + NORM_EPS) * g_ref[...]
    h = y * (1.0 + sc2_ref[...]) + sh2_ref[...]
    h_ref[...] = h.astype(BF16)
    if moe:
        logits = jnp.dot(h, rw_ref[...], preferred_element_type=F32,
                         precision=lax.Precision.HIGHEST) + rb_ref[...]
        gw_ref[...] = _route(logits)


def _merge(x, oa, ob, oc, gates, g1, sc2, sh2, g, wa, wb, wc, wo, router, tm):
    bsz, t, d = x.shape
    mrows = g1.shape[1]
    mblk = 1 if mrows == 1 else tm
    mod_spec = pl.BlockSpec((None, mblk, d), (lambda b, i: (b, 0, 0)) if mrows == 1 else (lambda b, i: (b, i, 0)))

    def tok(n):
        return pl.BlockSpec((None, tm, n), lambda b, i: (b, i, 0))

    in_specs = [tok(d), tok(384), tok(384), tok(256), tok(3 * d), mod_spec, mod_spec, mod_spec,
                _const_spec((1, d)), _const_spec(wa.shape), _const_spec(wb.shape), _const_spec(wc.shape),
                _const_spec(wo.shape)]
    args = [x, oa, ob, oc, gates, g1, sc2, sh2, g.reshape(1, d), wa, wb, wc, wo]
    out_specs = [tok(d), tok(d)]
    out_shape = [jax.ShapeDtypeStruct((bsz, t, d), F32), jax.ShapeDtypeStruct((bsz, t, d), BF16)]
    if router is not None:
        rw, rb = router
        in_specs += [_const_spec(rw.shape), _const_spec(rb.shape)]
        args += [rw, rb]
        out_specs.append(tok(LANES))
        out_shape.append(jax.ShapeDtypeStruct((bsz, t, LANES), F32))
    return pl.pallas_call(
        functools.partial(_merge_kernel, moe=router is not None),
        grid=(bsz, t // tm),
        in_specs=in_specs, out_specs=out_specs, out_shape=out_shape,
        compiler_params=_cparams(("parallel", "parallel")),
        name="merge_out",
    )(*args)


def _final_norm(x, gain):
    ms = jnp.mean(x * x, axis=-1, keepdims=True)
    return x * lax.rsqrt(ms + NORM_EPS) * gain


def _ffn_kernel(x_ref, h_ref, g2_ref, wg_ref, wu_ref, wd_ref, *rest, n_chunks, final):
    if final:
        fg_ref, o_ref = rest
    else:
        (o_ref,) = rest
    h = h_ref[...]
    tf = wg_ref.shape[1] // n_chunks
    acc = jnp.zeros(x_ref.shape, F32)
    for c in range(n_chunks):
        gt = jnp.dot(h, wg_ref[:, c * tf:(c + 1) * tf], preferred_element_type=F32)
        up = jnp.dot(h, wu_ref[:, c * tf:(c + 1) * tf], preferred_element_type=F32)
        act = (gt * jax.nn.sigmoid(gt) * up).astype(BF16)
        acc = acc + jnp.dot(act, wd_ref[c * tf:(c + 1) * tf, :], preferred_element_type=F32)
    x = x_ref[...] + g2_ref[...] * acc
    o_ref[...] = _final_norm(x, fg_ref[...]) if final else x


def _ffn(x, h, g2, wg, wu, wd, final_g, tm):
    bsz, t, d = x.shape
    mrows = g2.shape[1]
    mblk = 1 if mrows == 1 else tm
    mod_spec = pl.BlockSpec((None, mblk, d), (lambda b, i: (b, 0, 0)) if mrows == 1 else (lambda b, i: (b, i, 0)))
    tok = pl.BlockSpec((None, tm, d), lambda b, i: (b, i, 0))
    in_specs = [tok, tok, mod_spec, _const_spec(wg.shape), _const_spec(wu.shape), _const_spec(wd.shape)]
    args = [x, h, g2, wg, wu, wd]
    if final_g is not None:
        in_specs.append(_const_spec((1, d)))
        args.append(final_g.reshape(1, d))
    return pl.pallas_call(
        functools.partial(_ffn_kernel, n_chunks=2, final=final_g is not None),
        grid=(bsz, t // tm),
        in_specs=in_specs, out_specs=tok,
        out_shape=jax.ShapeDtypeStruct((bsz, t, d), F32),
        compiler_params=_cparams(("parallel", "parallel")),
        name="ffn_dense",
    )(*args)


def _moe_kernel(x_ref, h_ref, g2_ref, gw_ref, wg_ref, wu_ref, wd_ref, *rest, final):
    if final:
        fg_ref, o_ref, acc_ref = rest
    else:
        o_ref, acc_ref = rest
    e = pl.program_id(2)

    @pl.when(e == 0)
    def _():
        acc_ref[...] = jnp.zeros_like(acc_ref)

    h = h_ref[...]
    gt = jnp.dot(h, wg_ref[...], preferred_element_type=F32)
    up = jnp.dot(h, wu_ref[...], preferred_element_type=F32)
    act = (gt * jax.nn.sigmoid(gt) * up).astype(BF16)
    y = jnp.dot(act, wd_ref[...], preferred_element_type=F32)
    gw = gw_ref[...]
    ge = jnp.sum(jnp.where(_lane_iota(gw.shape) == e, gw, 0.0), axis=1, keepdims=True)
    acc_ref[...] += ge * y

    @pl.when(e == pl.num_programs(2) - 1)
    def _():
        x = x_ref[...] + g2_ref[...] * acc_ref[...]
        o_ref[...] = _final_norm(x, fg_ref[...]) if final else x


def _moe(x, h, g2, gw, wg, wu, wd, final_g, tm):
    bsz, t, d = x.shape
    n_e, _, dff = wg.shape
    mrows = g2.shape[1]
    mblk = 1 if mrows == 1 else tm
    mod_spec = pl.BlockSpec((None, mblk, d), (lambda b, i, e: (b, 0, 0)) if mrows == 1 else (lambda b, i, e: (b, i, 0)))
    tok = pl.BlockSpec((None, tm, d), lambda b, i, e: (b, i, 0))
    in_specs = [tok, tok, mod_spec, pl.BlockSpec((None, tm, LANES), lambda b, i, e: (b, i, 0)),
                pl.BlockSpec((None, d, dff), lambda b, i, e: (e, 0, 0)),
                pl.BlockSpec((None, d, dff), lambda b, i, e: (e, 0, 0)),
                pl.BlockSpec((None, dff, d), lambda b, i, e: (e, 0, 0))]
    args = [x, h, g2, gw, wg, wu, wd]
    if final_g is not None:
        in_specs.append(pl.BlockSpec((1, d), lambda b, i, e: (0, 0)))
        args.append(final_g.reshape(1, d))
    return pl.pallas_call(
        functools.partial(_moe_kernel, final=final_g is not None),
        grid=(bsz, t // tm, n_e),
        in_specs=in_specs, out_specs=tok,
        out_shape=jax.ShapeDtypeStruct((bsz, t, d), F32),
        scratch_shapes=[pltpu.VMEM((tm, d), F32)],
        compiler_params=_cparams(("parallel", "parallel", "arbitrary")),
        name="moe_dense",
    )(*args)


def _rope_tables(pos):
    inv = ROPE_THETA ** (-jnp.arange(HALF, dtype=F32) / HALF)
    ang = pos.astype(F32)[:, None] * inv[None, :]
    cos, sin = jnp.cos(ang), jnp.sin(ang)
    return jnp.tile(cos, (1, 4)), jnp.tile(jnp.concatenate([-sin, sin], axis=1), (1, 2))


def _pad_rows(a, n):
    return jnp.pad(a, ((0, 0), (0, n - a.shape[1])) + ((0, 0),) * (a.ndim - 2))


def _pick_tile(n, pref):
    t = min(n, pref)
    while n % t:
        t //= 2
    return t


def _mixers(inp, past, n_past, lw):
    aq, iq, na, kvi, nb, bqk, bv1, cu, sm = inp
    bsz, t, _ = aq.shape
    n_keys = n_past + t
    n_sel = min(TOPK_MAX, n_keys // 4)
    ones = jnp.ones((bsz, n_past, HEAD_DIM), BF16)
    if past is None:
        kvi_all, bk_all, bv1_all, logf_all, k_blk0 = kvi, bqk, bv1, sm, 3
        hist16 = jnp.zeros((bsz, 16, POOL_WIDTH), F32)
    else:
        k_blk0 = 0
        pa, pb, plf, pc = past
        pk, pv, pik = (pa[:, :, j].astype(BF16) for j in range(3))
        kvi_all = jnp.concatenate(
            [jnp.concatenate([pk, pk, pv, ones, ones, pv, pik, pik], axis=-1), kvi], axis=1)
        bk_all = jnp.concatenate([pb[:, :, 0].reshape(bsz, n_past, 384).astype(BF16), bqk[:, :, 384:]], axis=1)
        pv6 = pb[:, :, 1].astype(BF16)
        pv1 = jnp.concatenate(
            [jnp.concatenate([pv6[:, :, 2 * p], ones, ones, pv6[:, :, 2 * p + 1]], axis=-1) for p in range(3)], axis=-1)
        bv1_all = jnp.concatenate([pv1, bv1], axis=1)
        logf_all = jnp.concatenate([jnp.pad(plf, ((0, 0), (0, 0), (0, LANES - B_HEADS))), sm], axis=1)
        hist16 = jnp.pad(pc, ((0, 0), (1, 0), (0, 0)))

    tk = 128
    lpad = -(-n_keys // tk) * tk
    kvi_all, bk_all, bv1_all, logf_all = (_pad_rows(a, lpad) for a in (kvi_all, bk_all, bv1_all, logf_all))

    tq = _pick_tile(t, 128)
    oa = _dsa(aq, iq, sm, kvi_all, tq=tq, tk=tk, n_keys=n_keys, q_pos0=n_past, n_sel=n_sel)

    cum = _cumsum_rows(logf_all, tk)[:, :, :B_HEADS]
    tqb = _pick_tile(t, 256)
    tkb = _pick_tile(lpad, 256)
    cq = cum[:, n_past:n_past + t].reshape(bsz, t, 3, 2).transpose(0, 2, 1, 3)
    ck = cum.reshape(bsz, lpad // tkb, tkb, 3, 2).transpose(0, 3, 1, 4, 2)
    ob = _fox(bqk, bk_all, bv1_all, cq, ck, tq=tqb, tk=tkb, n_keys=n_keys, q_pos0=n_past, k_blk0=k_blk0)

    oc = _pool(cu, hist16, lw["pool_bd"], lw["pool_scale"], tc=_pick_tile(t, 256), start_pos=n_past)
    return oa, ob, oc


def _layer(x, mod, past, n_past, pos_tab, lw, layer, final_g, per_token):
    bsz, t, d = x.shape
    sh1, sc1, g1, sh2, sc2, g2 = mod
    if per_token:
        xt = x.reshape(1, bsz * t, d)
        sh1, sc1, g1, sh2, sc2, g2 = (jnp.broadcast_to(m, (bsz, t, d)).reshape(1, bsz * t, d) for m in mod)
        cos, sin = (jnp.tile(a, (bsz, 1)) for a in pos_tab)
    else:
        xt = x
        cos, sin = pos_tab
    tm = _pick_tile(xt.shape[1], 256)
    outs = _in_proj(xt, sc1, sh1, lw["norm_mix_g"], lw["w_in"], lw["bf_bias"], cos, sin, tm)
    outs = [o.reshape(bsz, t, o.shape[-1]) for o in outs]
    aq, iq, na, kvi, nb, bqk, bv1, cu, sm, gates = outs
    oa, ob, oc = _mixers((aq, iq, na, kvi, nb, bqk, bv1, cu, sm), past, n_past, lw)

    def flat(a):
        return a.reshape(xt.shape[0], xt.shape[1], a.shape[-1])

    router = (lw["router_w"], lw["router_b"]) if layer % 2 else None
    res = _merge(xt, flat(oa), flat(ob), flat(oc), flat(gates), g1, sc2, sh2, lw["norm_ffn_g"],
                 lw["w_br_a"], lw["w_br_b"], lw["w_br_c"], lw["w_out"], router, tm)
    tmf = _pick_tile(xt.shape[1], 512)
    if layer % 2 == 0:
        x_mid, h2 = res
        x_new = _ffn(x_mid, h2, g2, lw["ffn_wg"], lw["ffn_wu"], lw["ffn_wd"], final_g, tmf)
    else:
        x_mid, h2, gw = res
        x_new = _moe(x_mid, h2, g2, gw, lw["moe_wg"], lw["moe_wu"], lw["moe_wd"], final_g, tmf)
    new_a = na.reshape(bsz, t, 3, HEAD_DIM)
    new_b = nb.reshape(bsz, t, 2, B_HEADS, HEAD_DIM)
    new_logf = sm[:, :, :B_HEADS]
    new_pool = cu[:, t - POOL_HIST:, :]
    return x_new.reshape(bsz, t, d), (new_a, new_b, new_logf, new_pool)


def kernel(x_prompt, x_sample, cache_a_kvi, cache_b_kv, cache_b_logf, state_c_pool, c_prompt, c_sample,
           ada_w, ada_b, norm_mix_g, w_in, b_forget, pool_w, pool_scale, w_br_a, w_br_b, w_br_c, w_out,
           norm_ffn_g, ffn_w_gate, ffn_w_up, ffn_w_down, moe_router_w, moe_router_b, moe_w_gate,
           moe_w_up, moe_w_down, final_norm_g):
    depth = ada_w.shape[0]
    bp, tp, d = x_prompt.shape
    bs, ts, _ = x_sample.shape
    n_past = cache_a_kvi.shape[2]

    rows = -(-(bp + bs) // 8) * 8
    c_all = jnp.pad(jnp.concatenate([c_prompt, c_sample], axis=0), ((0, rows - bp - bs), (0, 0)))
    mod_all = _ada(c_all, ada_w, ada_b)

    tab_p = _rope_tables(jnp.arange(tp))
    tab_s = _rope_tables(n_past + jnp.arange(ts))

    xp, xs = x_prompt, x_sample
    outs_p, outs_s = [], []
    for layer in range(depth):
        j = layer // 2
        w_l, bias_l = _in_weights(w_in[layer], b_forget[layer])
        pw = pool_w[layer]
        pool_bd = jnp.zeros((POOL_WIDTH, POOL_WIDTH), F32)
        for g in range(len(POOL_WINDOWS)):
            sl = slice(g * POOL_GROUP_DIM, (g + 1) * POOL_GROUP_DIM)
            pool_bd = pool_bd.at[sl, sl].set(pw[g])
        lw = dict(w_in=w_l, bf_bias=bias_l, norm_mix_g=norm_mix_g[layer], norm_ffn_g=norm_ffn_g[layer],
                  pool_bd=pool_bd.astype(BF16), pool_scale=pool_scale[layer],
                  w_br_a=w_br_a[layer].astype(BF16), w_br_b=w_br_b[layer].astype(BF16),
                  w_br_c=w_br_c[layer].astype(BF16), w_out=w_out[layer].astype(BF16))
        if layer % 2 == 0:
            lw.update(ffn_wg=ffn_w_gate[j].astype(BF16), ffn_wu=ffn_w_up[j].astype(BF16),
                      ffn_wd=ffn_w_down[j].astype(BF16))
        else:
            lw.update(router_w=jnp.pad(moe_router_w[j], ((0, 0), (0, LANES - N_EXPERTS))),
                      router_b=jnp.pad(moe_router_b[j], (0, LANES - N_EXPERTS)).reshape(1, LANES),
                      moe_wg=moe_w_gate[j].astype(BF16), moe_wu=moe_w_up[j].astype(BF16),
                      moe_wd=moe_w_down[j].astype(BF16))
        final_g = final_norm_g if layer == depth - 1 else None
        mod_p = [m[:, None, :] for m in jnp.split(mod_all[layer, :bp], 6, axis=-1)]
        mod_s = [m[:, None, :] for m in jnp.split(mod_all[layer, bp:bp + bs], 6, axis=-1)]
        xp, new_p = _layer(xp, mod_p, None, 0, tab_p, lw, layer, final_g, per_token=False)
        past = (cache_a_kvi[layer], cache_b_kv[layer], cache_b_logf[layer], state_c_pool[layer])
        xs, new_s = _layer(xs, mod_s, past, n_past, tab_s, lw, layer, final_g, per_token=True)
        outs_p.append(new_p)
        outs_s.append(new_s)

    def stack(outs, k):
        return jnp.stack([o[k] for o in outs])

    return (xp, xs,
            stack(outs_p, 0), stack(outs_p, 1), stack(outs_p, 2), stack(outs_p, 3),
            stack(outs_s, 0), stack(outs_s, 1), stack(outs_s, 2), stack(outs_s, 3))
```

```python
import functools

import jax
import jax.numpy as jnp
import numpy as np
from jax import lax
from jax.experimental import pallas as pl
from jax.experimental.pallas import tpu as pltpu

F32 = jnp.float32
BF16 = jnp.bfloat16
I32 = jnp.int32

D_MODEL = 1024
CHUNK = 64
HEAD_DIM = 64
HALF = HEAD_DIM // 2
ROPE_THETA = 10000.0
NORM_EPS = 1e-6
A_HEADS = 6
IDX_HEADS = 4
TOPK_MAX = 256
B_HEADS = 6
POOL_WINDOWS = (2, 4, 8, 16)
POOL_GROUP_DIM = 64
POOL_WIDTH = 256
POOL_HIST = 15
N_EXPERTS = 8
LANES = 128
LOG2E = 1.4426950408889634
QK_SCALE = HEAD_DIM ** -0.5 * LOG2E
KV_BLOCK = 256
VMEM_LIMIT = 56 * 1024 * 1024
NEG_INF = float("-inf")
INT_MIN = -2 ** 31
KEY_NEG_INF = -2 ** 31 + 0x7FFFFF
KEY_POS_INF = 0x7F800000

C_AQ, C_IQ, C_A, C_KVI, C_BQK, C_BV, C_CU, C_SM, C_GATE, C_END = (
    0, 768, 1280, 1792, 2816, 3584, 4736, 4992, 5120, 8192)


def _cparams(sem):
    return pltpu.CompilerParams(dimension_semantics=sem, vmem_limit_bytes=VMEM_LIMIT)


def _const_spec(shape):
    nd = len(shape)
    return pl.BlockSpec(shape, lambda *_: (0,) * nd, pipeline_mode=pl.Buffered(1))


def _lane_iota(shape):
    return lax.broadcasted_iota(I32, shape, len(shape) - 1)


def _ada_kernel(c_ref, w_ref, b_ref, o_ref):
    c = c_ref[...]
    s = c * jax.nn.sigmoid(c)
    o_ref[...] = jnp.dot(s, w_ref[...], preferred_element_type=F32,
                         precision=lax.Precision.HIGHEST) + b_ref[...]


def _ada(c_all, ada_w, ada_b):
    depth, d, n = ada_w.shape
    rows = c_all.shape[0]
    tn = 1536
    return pl.pallas_call(
        _ada_kernel,
        grid=(depth, n // tn),
        in_specs=[pl.BlockSpec((rows, d), lambda l, j: (0, 0)),
                  pl.BlockSpec((None, d, tn), lambda l, j: (l, 0, j)),
                  pl.BlockSpec((None, 1, tn), lambda l, j: (l, 0, j))],
        out_specs=pl.BlockSpec((None, rows, tn), lambda l, j: (l, 0, j)),
        out_shape=jax.ShapeDtypeStruct((depth, rows, n), F32),
        compiler_params=_cparams(("arbitrary", "arbitrary")),
        name="ada_mod",
    )(c_all, ada_w, ada_b.reshape(depth, 1, n))


def _in_kernel(x_ref, sc_ref, sh_ref, g_ref, w_ref, bf_ref, cos_ref, sin_ref,
               aq_ref, iq_ref, na_ref, v12_ref, kkt_ref, ikt_ref, nb_ref, bq_ref, bkt_ref, bv1_ref,
               cu_ref, sm_ref, smt_ref, gate_ref):
    x = x_ref[...]
    ms = jnp.mean(x * x, axis=-1, keepdims=True)
    y = x * lax.rsqrt(ms + NORM_EPS) * g_ref[...]
    h = (y * (1.0 + sc_ref[...]) + sh_ref[...]).astype(BF16)
    tm = x.shape[0]

    def mm(a, b):
        return jnp.dot(h, w_ref[:, a:b], preferred_element_type=F32)

    cos = cos_ref[...]
    sin = sin_ref[...]
    ones = jnp.ones_like(cos)
    zeros = jnp.zeros_like(cos)
    lane = _lane_iota((tm, LANES))
    low = lane < HEAD_DIM

    z = mm(C_AQ, C_IQ)
    cos3 = jnp.concatenate([cos] * 3, axis=1)
    sin3 = jnp.concatenate([sin] * 3, axis=1)
    aq_ref[...] = ((z[:, :384] * cos3 + z[:, 384:] * sin3) * QK_SCALE).astype(BF16)

    z = mm(C_IQ, C_A)
    cos2 = jnp.concatenate([cos] * 2, axis=1)
    sin2 = jnp.concatenate([sin] * 2, axis=1)
    iq_ref[...] = (z[:, :256] * cos2 + z[:, 256:] * sin2).astype(BF16)

    z = mm(C_A, C_KVI)
    cos_a = jnp.concatenate([jnp.where(low, cos, ones), jnp.where(low, cos, zeros)], axis=1)
    sin_a = jnp.concatenate([jnp.where(low, sin, zeros), jnp.where(low, sin, zeros)], axis=1)
    na = z[:, :256] * cos_a + z[:, 256:] * sin_a
    na_ref[...] = na[:, :192]

    z = mm(C_KVI, C_BQK)
    kk = z[:, 0:128] * cos + z[:, 512:640] * sin
    v1 = jnp.where(low, z[:, 128:256], ones)
    v2 = jnp.where(low, ones, z[:, 256:384])
    ikk = z[:, 384:512] * cos + z[:, 896:1024] * sin
    v12_ref[...] = jnp.concatenate([v1, v2], axis=1).astype(BF16)
    kkt_ref[...] = kk.T.astype(BF16)
    ikt_ref[...] = ikk.T.astype(BF16)

    z = mm(C_BQK, C_BV)
    nb_ref[:, :384] = z[:, 384:]
    bq_ref[...] = (z[:, :384] * QK_SCALE).astype(BF16)
    for p in range(3):
        bkt_ref[p] = z[:, 384 + p * LANES:384 + (p + 1) * LANES].T.astype(BF16)

    z = mm(C_BV, C_CU)
    nb_ref[:, 384:] = z[:, :384]
    lane3 = _lane_iota((tm, 768)) % 256
    pad = (lane3 >= HEAD_DIM) & (lane3 < 3 * HEAD_DIM)
    bv1_ref[...] = jnp.where(pad, 1.0, z[:, 384:]).astype(BF16)

    cu_ref[...] = mm(C_CU, C_SM)

    z = mm(C_SM, C_GATE)
    t = z + bf_ref[...]
    logf = jnp.minimum(t, 0.0) - jnp.log1p(jnp.exp(-jnp.abs(t)))
    sm = jnp.where(lane < B_HEADS, logf, z)
    sm_ref[...] = sm
    smt_ref[...] = sm.T[0:8, :]

    for c in range(3):
        gate_ref[:, c * D_MODEL:(c + 1) * D_MODEL] = jax.nn.sigmoid(
            mm(C_GATE + c * D_MODEL, C_GATE + (c + 1) * D_MODEL))


def _swap_halves_cols(w, n_heads):
    d = w.shape[0]
    w4 = w.reshape(d, n_heads, 2, HALF)
    return jnp.concatenate([w4[:, :, 1], w4[:, :, 0]], axis=-1).reshape(d, n_heads * HEAD_DIM)


def _in_weights(w_in_l, b_forget_l):
    d = w_in_l.shape[0]
    sizes = (384, 64, 64, 256, 4, 64, 384, 384, 384, 6, 256, 3072)
    offs = np.concatenate([[0], np.cumsum(sizes)])
    aq, ak, av, iq, iw, ik, bq, bk, bv, bfw, cu, gl = [w_in_l[:, offs[i]:offs[i + 1]] for i in range(12)]
    z64 = jnp.zeros((d, 64), F32)
    aks, iks = _swap_halves_cols(ak, 1), _swap_halves_cols(ik, 1)
    bv6 = bv.reshape(d, 3, 2, HEAD_DIM)
    z3 = jnp.zeros((d, 3, HEAD_DIM), F32)
    bv_pad = jnp.stack([bv6[:, :, 0], z3, z3, bv6[:, :, 1]], axis=2).reshape(d, 768)
    small = jnp.concatenate([bfw, jnp.zeros((d, 2), F32), iw, jnp.zeros((d, LANES - 12), F32)], axis=1)
    cols = [aq, _swap_halves_cols(aq, A_HEADS),
            iq, _swap_halves_cols(iq, IDX_HEADS),
            ak, av, ik, z64, aks, z64, iks, z64,
            ak, ak, av, z64, z64, av, ik, ik, aks, aks, z64, z64, z64, z64, iks, iks,
            bq, bk,
            bv, bv_pad,
            cu, small, gl]
    w = jnp.concatenate(cols, axis=1).astype(BF16)
    assert w.shape[1] == C_END, w.shape
    bias = jnp.concatenate([b_forget_l, jnp.zeros((LANES - B_HEADS,), F32)]).reshape(1, LANES)
    return w, bias


def _in_proj(x, sc, sh, g, w, bias, cos, sin, tm):
    bsz, t, d = x.shape
    mrows = sc.shape[1]
    mblk = 1 if mrows == 1 else tm
    mod_spec = pl.BlockSpec((None, mblk, d), (lambda b, i: (b, 0, 0)) if mrows == 1 else (lambda b, i: (b, i, 0)))
    nblk = t // tm

    def rows(n, dt):
        return pl.BlockSpec((None, tm, n), lambda b, i: (b, i, 0)), jax.ShapeDtypeStruct((bsz, t, n), dt)

    def cols(dt):
        return (pl.BlockSpec((None, None, LANES, tm), lambda b, i: (b, i, 0, 0)),
                jax.ShapeDtypeStruct((bsz, nblk, LANES, tm), dt))

    outs = [rows(384, BF16), rows(256, BF16), rows(192, F32), rows(256, BF16), cols(BF16), cols(BF16),
            rows(768, F32), rows(384, BF16),
            (pl.BlockSpec((None, 3, None, LANES, tm), lambda b, i: (b, 0, i, 0, 0)),
             jax.ShapeDtypeStruct((bsz, 3, nblk, LANES, tm), BF16)),
            rows(768, BF16), rows(256, F32), rows(LANES, F32),
            (pl.BlockSpec((None, 8, tm), lambda b, i: (b, 0, i)), jax.ShapeDtypeStruct((bsz, 8, t), F32)),
            rows(3 * D_MODEL, F32)]
    return pl.pallas_call(
        _in_kernel,
        grid=(bsz, nblk),
        in_specs=[pl.BlockSpec((None, tm, d), lambda b, i: (b, i, 0)), mod_spec, mod_spec,
                  _const_spec((1, d)), _const_spec(w.shape), _const_spec((1, LANES)),
                  pl.BlockSpec((tm, LANES), lambda b, i: (i, 0)),
                  pl.BlockSpec((tm, LANES), lambda b, i: (i, 0))],
        out_specs=[o[0] for o in outs],
        out_shape=[o[1] for o in outs],
        compiler_params=_cparams(("parallel", "parallel")),
        name="in_proj",
    )(x, sc, sh, g.reshape(1, d), w, bias, cos, sin)


def _cum_kernel(x_ref, o_ref, carry_ref):
    @pl.when(pl.program_id(1) == 0)
    def _():
        carry_ref[...] = jnp.zeros_like(carry_ref)

    tc = x_ref.shape[1]
    r = lax.broadcasted_iota(I32, (tc, tc), 0)
    c = lax.broadcasted_iota(I32, (tc, tc), 1)
    tri = jnp.where(r <= c, 1.0, 0.0).astype(F32)
    cum = jnp.dot(x_ref[...], tri, preferred_element_type=F32,
                  precision=lax.Precision.HIGHEST) + carry_ref[:, 0:1]
    o_ref[...] = cum * LOG2E
    carry_ref[...] = jnp.broadcast_to(cum[:, tc - 1:tc], carry_ref.shape)


def _cumsum_lanes(x, tc):
    bsz, n, t = x.shape
    return pl.pallas_call(
        _cum_kernel,
        grid=(bsz, t // tc),
        in_specs=[pl.BlockSpec((None, n, tc), lambda b, i: (b, 0, i))],
        out_specs=pl.BlockSpec((None, None, n, tc), lambda b, i: (b, i, 0, 0)),
        out_shape=jax.ShapeDtypeStruct((bsz, t // tc, n, tc), F32),
        scratch_shapes=[pltpu.VMEM((n, LANES), F32)],
        compiler_params=_cparams(("parallel", "arbitrary")),
        name="logf_cumsum",
    )(x)


def _head_rows(x, n_pairs, low):
    zero = jnp.zeros((x.shape[0], LANES), x.dtype)
    return jnp.concatenate(
        [jnp.where(low if hh == 0 else ~low, x[:, p * LANES:(p + 1) * LANES], zero)
         for p in range(n_pairs) for hh in range(2)], axis=0)


def _dsa_kernel(aq_ref, iq_ref, sm_ref, ikt_ref, kkt_ref, v12_ref, o_ref, key_ref,
                *, tq, tk, n_keys, q_pos0, n_sel):
    i = pl.program_id(1)
    pos_first = q_pos0 + i * tq
    last_chunk = (pos_first + tq - 1) // CHUNK
    n_adm = jnp.minimum((last_chunk + 1) * CHUNK, n_keys)
    nkb = (n_adm + tk - 1) // tk
    n_pairs = (nkb + 1) // 2

    low_q = _lane_iota((tq, LANES)) < HEAD_DIM
    row_pos = pos_first + lax.broadcasted_iota(I32, (tq, 1), 0)
    row_lim = jnp.minimum((row_pos // CHUNK + 1) * CHUNK, n_keys)
    col = _lane_iota((tq, tk))

    q4 = _head_rows(iq_ref[...], 2, low_q)
    sm = sm_ref[...]
    wb = [jnp.broadcast_to(sm[:, 8 + hd:9 + hd], (tq, tk)) for hd in range(IDX_HEADS)]

    def score_body(kb, carry):
        s4 = jnp.dot(q4, ikt_ref[kb], preferred_element_type=F32)
        score = wb[0] * jnp.maximum(s4[0:tq], 0.0)
        for hd in range(1, IDX_HEADS):
            score = score + wb[hd] * jnp.maximum(s4[hd * tq:(hd + 1) * tq], 0.0)
        score = score + 0.0
        score = jnp.where(col < row_lim - kb * tk, score, NEG_INF)
        bits = lax.bitcast_convert_type(score, I32)
        key_ref[kb] = jnp.where(bits < 0, bits ^ 0x7FFFFFFF, bits)
        return carry

    lax.fori_loop(0, nkb, score_body, 0)

    @pl.when(nkb % 2 == 1)
    def _():
        key_ref[nkb] = jnp.full((tq, tk), INT_MIN, I32)

    def count(cand, strict):
        def hits(blk):
            h = (blk > cand) if strict else (blk >= cand)
            h = jnp.where(h, 1.0, 0.0)
            return h[:, :LANES] + h[:, LANES:]

        def body(j, acc):
            return acc + hits(key_ref[2 * j]) + hits(key_ref[2 * j + 1])

        acc = lax.fori_loop(0, n_pairs, body, jnp.zeros((tq, LANES), F32))
        return jnp.sum(acc, axis=1, keepdims=True)

    def bit_body(b, t_b):
        bit = lax.shift_left(jnp.int32(1), 31 - b)
        cand_b = t_b | bit
        cnt = count(cand_b ^ INT_MIN, False)
        return jnp.where(cnt >= n_sel, cand_b, t_b)

    t_b = lax.fori_loop(0, 32, bit_body, jnp.zeros((tq, 1), I32))
    thr = t_b ^ INT_MIN
    need = n_sel - count(thr, True)
    thr_lo = jnp.maximum(thr, KEY_NEG_INF + 1)
    n_ge = count(thr, False)
    n_nonfinite = count(jnp.full((tq, 1), KEY_POS_INF, I32), False)
    partial_ties = jnp.where((thr > KEY_NEG_INF) & (n_ge > n_sel), 1.0, 0.0)
    general = jnp.max(partial_ties + n_nonfinite) > 0.0

    q6 = _head_rows(aq_ref[...], 3, low_q)
    r_i = lax.broadcasted_iota(I32, (tk, tk), 0)
    c_i = lax.broadcasted_iota(I32, (tk, tk), 1)

    def attend(with_ties):
        upper = jnp.where(r_i <= c_i, 1.0, 0.0).astype(BF16)

        def body(kb, carry):
            eq_seen, ms, accs = carry
            blk = key_ref[kb]
            if with_ties:
                eq = blk == thr
                eq_f = jnp.where(eq, 1.0, 0.0)
                pref = jnp.dot(eq_f.astype(BF16), upper, preferred_element_type=F32) + eq_seen
                slack = jnp.where(blk >= thr_lo, need - jnp.where(eq, pref, 0.0), -1.0)
                bias = jnp.where(slack >= 0.0, jnp.where(blk < KEY_POS_INF, 0.0, NEG_INF), NEG_INF)
                eq_seen = eq_seen + jnp.sum(eq_f, axis=1, keepdims=True)
            else:
                bias = jnp.where(blk >= thr_lo, 0.0, NEG_INF)
            logits = jnp.dot(q6, kkt_ref[kb], preferred_element_type=F32)
            vblk = v12_ref[pl.ds(pl.multiple_of(kb * tk, tk), tk), :]
            new_ms, new_accs = [], []
            for hd in range(A_HEADS):
                lg = logits[hd * tq:(hd + 1) * tq] + bias
                m_old = ms[hd]
                m_new = jnp.maximum(m_old, jnp.max(lg, axis=1, keepdims=True))
                m_safe = jnp.where(m_new == NEG_INF, 0.0, m_new)
                p = jnp.exp2(lg - m_safe)
                alpha = jnp.exp2(m_old - m_safe)
                vv = vblk[:, (hd % 2) * LANES:(hd % 2 + 1) * LANES]
                new_accs.append(alpha * accs[hd] + jnp.dot(p.astype(BF16), vv, preferred_element_type=F32))
                new_ms.append(m_new)
            return eq_seen, tuple(new_ms), tuple(new_accs)

        init = (jnp.zeros((tq, 1), F32),
                tuple(jnp.full((tq, 1), NEG_INF, F32) for _ in range(A_HEADS)),
                tuple(jnp.zeros((tq, LANES), F32) for _ in range(A_HEADS)))
        return lax.fori_loop(0, nkb, body, init)[2]

    accs = lax.cond(general, lambda: attend(True), lambda: attend(False))

    outs = []
    for p in range(3):
        ev, od = accs[2 * p], accs[2 * p + 1]
        o_ev = ev / ev[:, HEAD_DIM:HEAD_DIM + 1]
        o_od = od / od[:, 0:1]
        outs.append(jnp.where(low_q, o_ev, o_od))
    o_ref[...] = jnp.concatenate(outs, axis=1).astype(o_ref.dtype)


def _dsa(aq, iq, sm, ikt, kkt, v12, *, tq, n_keys, q_pos0, n_sel):
    bsz, t_q, _ = aq.shape
    _, nblk, _, tk = kkt.shape
    kern = functools.partial(_dsa_kernel, tq=tq, tk=tk, n_keys=n_keys, q_pos0=q_pos0, n_sel=n_sel)
    kt_spec = pl.BlockSpec((None, nblk, LANES, tk), lambda b, i: (b, 0, 0, 0))
    return pl.pallas_call(
        kern,
        grid=(bsz, t_q // tq),
        in_specs=[pl.BlockSpec((None, tq, 384), lambda b, i: (b, i, 0)),
                  pl.BlockSpec((None, tq, 256), lambda b, i: (b, i, 0)),
                  pl.BlockSpec((None, tq, LANES), lambda b, i: (b, i, 0)),
                  kt_spec, kt_spec,
                  pl.BlockSpec((None, nblk * tk, 256), lambda b, i: (b, 0, 0))],
        out_specs=pl.BlockSpec((None, tq, 384), lambda b, i: (b, i, 0)),
        out_shape=jax.ShapeDtypeStruct((bsz, t_q, 384), BF16),
        scratch_shapes=[pltpu.VMEM((2 * ((nblk + 1) // 2), tq, tk), I32)],
        compiler_params=_cparams(("parallel", "arbitrary")),
        name="dsa_attention",
    )(aq, iq, sm, ikt, kkt, v12)


def _fox_kernel(q_ref, kt_ref, v_ref, ck_ref, o_ref, *, tq, tk, n_keys, q_pos0):
    pair = pl.program_id(1)
    i = pl.program_id(2)
    pos_first = q_pos0 + i * tq
    n_full = pos_first // tk
    nkb = (jnp.minimum(pos_first + tq, n_keys) + tk - 1) // tk
    low_q = _lane_iota((tq, LANES)) < HEAD_DIM
    q = q_ref[...]
    zq = jnp.zeros_like(q)
    qs = (jnp.where(low_q, q, zq), jnp.where(low_q, zq, q))
    row_pos = pos_first + lax.broadcasted_iota(I32, (tq, 1), 0)
    col = _lane_iota((tq, tk))

    def block(kb, state, masked):
        kt = kt_ref[kb]
        vb = v_ref[pl.ds(pl.multiple_of(kb * tk, tk), tk), :]
        new = []
        for hh in range(2):
            m_old, acc = state[hh]
            ck = ck_ref[kb, pl.ds(2 * pair + hh, 1), :]
            lg = jnp.dot(qs[hh], kt, preferred_element_type=F32) - ck
            if masked:
                lg = jnp.where(col <= row_pos - kb * tk, lg, NEG_INF)
            m_new = jnp.maximum(m_old, jnp.max(lg, axis=1, keepdims=True))
            m_safe = jnp.where(m_new == NEG_INF, 0.0, m_new) if masked else m_new
            p = jnp.exp2(lg - m_safe)
            alpha = jnp.exp2(m_old - m_safe)
            acc = alpha * acc + jnp.dot(p.astype(BF16), vb[:, hh * LANES:(hh + 1) * LANES],
                                        preferred_element_type=F32)
            new.append((m_new, acc))
        return tuple(new)

    init = tuple((jnp.full((tq, 1), NEG_INF, F32), jnp.zeros((tq, LANES), F32)) for _ in range(2))
    state = lax.fori_loop(0, n_full // 2,
                          lambda j, st: block(2 * j + 1, block(2 * j, st, False), False), init)
    state = lax.fori_loop(2 * (n_full // 2), nkb, lambda kb, st: block(kb, st, True), state)
    acc_ev, acc_od = state[0][1], state[1][1]
    o_ev = acc_ev / acc_ev[:, HEAD_DIM:HEAD_DIM + 1]
    o_od = acc_od / acc_od[:, 0:1]
    o_ref[...] = jnp.where(low_q, o_ev, o_od).astype(o_ref.dtype)


def _fox(q, kt, v1, ck, *, tq, n_keys, q_pos0):
    bsz, t_q, _ = q.shape
    _, _, nblk, _, tk = kt.shape
    kern = functools.partial(_fox_kernel, tq=tq, tk=tk, n_keys=n_keys, q_pos0=q_pos0)
    return pl.pallas_call(
        kern,
        grid=(bsz, 3, t_q // tq),
        in_specs=[pl.BlockSpec((None, tq, LANES), lambda b, p, i: (b, i, p)),
                  pl.BlockSpec((None, None, nblk, LANES, tk), lambda b, p, i: (b, p, 0, 0, 0)),
                  pl.BlockSpec((None, nblk * tk, 2 * LANES), lambda b, p, i: (b, 0, p)),
                  pl.BlockSpec((None, nblk, 8, tk), lambda b, p, i: (b, 0, 0, 0))],
        out_specs=pl.BlockSpec((None, tq, LANES), lambda b, p, i: (b, i, p)),
        out_shape=jax.ShapeDtypeStruct((bsz, t_q, 384), BF16),
        compiler_params=_cparams(("parallel", "parallel", "arbitrary")),
        name="fox_attention",
    )(q, kt, v1, ck)


def _pool_kernel(cur_ref, prev_ref, hist_ref, w_ref, s_ref, o_ref, ext, *, tc, start_pos):
    i = pl.program_id(1)
    cur = cur_ref[...]
    ext[0:16, :] = jnp.where(i == 0, hist_ref[...], prev_ref[tc - 16:, :])
    ext[16:, :] = cur
    pos = start_pos + i * tc + lax.broadcasted_iota(I32, (tc, POOL_WIDTH), 0)
    lane = _lane_iota((tc, POOL_WIDTH))
    run = cur
    pooled = jnp.zeros_like(cur)
    k = 1
    for g, w in enumerate(POOL_WINDOWS):
        while k < w:
            run = run + ext[16 - k:16 - k + tc, :]
            k += 1
        cnt = jnp.minimum(pos + 1, w).astype(F32)
        in_group = (lane >= g * POOL_GROUP_DIM) & (lane < (g + 1) * POOL_GROUP_DIM)
        pooled = jnp.where(in_group, run / cnt, pooled)
    z = (pooled - cur).astype(BF16)
    o_ref[...] = (jnp.dot(z, w_ref[...], preferred_element_type=F32) * s_ref[...]).astype(o_ref.dtype)


def _pool(cu, hist16, w_bd, scale, *, tc, start_pos):
    bsz, t, n = cu.shape
    kern = functools.partial(_pool_kernel, tc=tc, start_pos=start_pos)
    return pl.pallas_call(
        kern,
        grid=(bsz, t // tc),
        in_specs=[pl.BlockSpec((None, tc, n), lambda b, i: (b, i, 0)),
                  pl.BlockSpec((None, tc, n), lambda b, i: (b, jnp.maximum(i - 1, 0), 0)),
                  pl.BlockSpec((None, 16, n), lambda b, i: (b, 0, 0)),
                  _const_spec((n, n)), _const_spec((1, n))],
        out_specs=pl.BlockSpec((None, tc, n), lambda b, i: (b, i, 0)),
        out_shape=jax.ShapeDtypeStruct((bsz, t, n), BF16),
        scratch_shapes=[pltpu.VMEM((16 + tc, n), F32)],
        compiler_params=_cparams(("parallel", "arbitrary")),
        name="pool_mixer",
    )(cu, cu, hist16, w_bd, scale.reshape(1, n))


def _route(logits):
    lane = _lane_iota(logits.shape).astype(F32)
    lg = jnp.where(lane < N_EXPERTS, logits, NEG_INF)
    m1 = jnp.max(lg, axis=1, keepdims=True)
    i1 = jnp.min(jnp.where(lg == m1, lane, float(LANES)), axis=1, keepdims=True)
    hot1 = lane == i1
    lg2 = jnp.where(hot1, NEG_INF, lg)
    m2 = jnp.max(lg2, axis=1, keepdims=True)
    i2 = jnp.min(jnp.where(lg2 == m2, lane, float(LANES)), axis=1, keepdims=True)
    hot2 = lane == i2
    e2 = jnp.exp(m2 - m1)
    den = 1.0 + e2
    return jnp.where(hot1, 1.0 / den, 0.0) + jnp.where(hot2, e2 / den, 0.0)


def _merge_kernel(x_ref, oa_ref, ob_ref, oc_ref, gate_ref, g1_ref, sc2_ref, sh2_ref, g_ref,
                  wa_ref, wb_ref, wc_ref, wo_ref, *rest, moe):
    if moe:
        rw_ref, rb_ref, xo_ref, h_ref, gw_ref = rest
    else:
        xo_ref, h_ref = rest
    d = D_MODEL
    merged = (gate_ref[:, 0:d] * jnp.dot(oa_ref[...], wa_ref[...], preferred_element_type=F32)
              + gate_ref[:, d:2 * d] * jnp.dot(ob_ref[...], wb_ref[...], preferred_element_type=F32)
              + gate_ref[:, 2 * d:3 * d] * jnp.dot(oc_ref[...], wc_ref[...], preferred_element_type=F32))
    x = x_ref[...] + g1_ref[...] * jnp.dot(merged.astype(BF16), wo_ref[...], preferred_element_type=F32)
    xo_ref[...] = x
    ms = jnp.mean(x * x, axis=-1, keepdims=True)
    y = x * lax.rsqrt(ms + NORM_EPS) * g_ref[...]
    h = y * (1.0 + sc2_ref[...]) + sh2_ref[...]
    h_ref[...] = h.astype(BF16)
    if moe:
        logits = jnp.dot(h, rw_ref[...], preferred_element_type=F32,
                         precision=lax.Precision.HIGHEST) + rb_ref[...]
        gw_ref[...] = _route(logits)


def _merge(x, oa, ob, oc, gates, g1, sc2, sh2, g, wa, wb, wc, wo, router, tm):
    bsz, t, d = x.shape
    mrows = g1.shape[1]
    mblk = 1 if mrows == 1 else tm
    mod_spec = pl.BlockSpec((None, mblk, d), (lambda b, i: (b, 0, 0)) if mrows == 1 else (lambda b, i: (b, i, 0)))

    def tok(n):
        return pl.BlockSpec((None, tm, n), lambda b, i: (b, i, 0))

    in_specs = [tok(d), tok(384), tok(384), tok(256), tok(3 * d), mod_spec, mod_spec, mod_spec,
                _const_spec((1, d)), _const_spec(wa.shape), _const_spec(wb.shape), _const_spec(wc.shape),
                _const_spec(wo.shape)]
    args = [x, oa, ob, oc, gates, g1, sc2, sh2, g.reshape(1, d), wa, wb, wc, wo]
    out_specs = [tok(d), tok(d)]
    out_shape = [jax.ShapeDtypeStruct((bsz, t, d), F32), jax.ShapeDtypeStruct((bsz, t, d), BF16)]
    if router is not None:
        rw, rb = router
        in_specs += [_const_spec(rw.shape), _const_spec(rb.shape)]
        args += [rw, rb]
        out_specs.append(tok(LANES))
        out_shape.append(jax.ShapeDtypeStruct((bsz, t, LANES), F32))
    return pl.pallas_call(
        functools.partial(_merge_kernel, moe=router is not None),
        grid=(bsz, t // tm),
        in_specs=in_specs, out_specs=out_specs, out_shape=out_shape,
        compiler_params=_cparams(("parallel", "parallel")),
        name="merge_out",
    )(*args)


def _final_norm(x, gain):
    ms = jnp.mean(x * x, axis=-1, keepdims=True)
    return x * lax.rsqrt(ms + NORM_EPS) * gain


def _ffn_kernel(x_ref, h_ref, g2_ref, wg_ref, wu_ref, wd_ref, *rest, n_chunks, final):
    if final:
        fg_ref, o_ref = rest
    else:
        (o_ref,) = rest
    h = h_ref[...]
    tf = wg_ref.shape[1] // n_chunks
    acc = jnp.zeros(x_ref.shape, F32)
    for c in range(n_chunks):
        gt = jnp.dot(h, wg_ref[:, c * tf:(c + 1) * tf], preferred_element_type=F32)
        up = jnp.dot(h, wu_ref[:, c * tf:(c + 1) * tf], preferred_element_type=F32)
        act = (gt * jax.nn.sigmoid(gt) * up).astype(BF16)
        acc = acc + jnp.dot(act, wd_ref[c * tf:(c + 1) * tf, :], preferred_element_type=F32)
    x = x_ref[...] + g2_ref[...] * acc
    o_ref[...] = _final_norm(x, fg_ref[...]) if final else x


def _ffn(x, h, g2, wg, wu, wd, final_g, tm):
    bsz, t, d = x.shape
    mrows = g2.shape[1]
    mblk = 1 if mrows == 1 else tm
    mod_spec = pl.BlockSpec((None, mblk, d), (lambda b, i: (b, 0, 0)) if mrows == 1 else (lambda b, i: (b, i, 0)))
    tok = pl.BlockSpec((None, tm, d), lambda b, i: (b, i, 0))
    in_specs = [tok, tok, mod_spec, _const_spec(wg.shape), _const_spec(wu.shape), _const_spec(wd.shape)]
    args = [x, h, g2, wg, wu, wd]
    if final_g is not None:
        in_specs.append(_const_spec((1, d)))
        args.append(final_g.reshape(1, d))
    return pl.pallas_call(
        functools.partial(_ffn_kernel, n_chunks=2, final=final_g is not None),
        grid=(bsz, t // tm),
        in_specs=in_specs, out_specs=tok,
        out_shape=jax.ShapeDtypeStruct((bsz, t, d), F32),
        compiler_params=_cparams(("parallel", "parallel")),
        name="ffn_dense",
    )(*args)


def _moe_kernel(x_ref, h_ref, g2_ref, gw_ref, wg_ref, wu_ref, wd_ref, *rest, final):
    if final:
        fg_ref, o_ref, acc_ref = rest
    else:
        o_ref, acc_ref = rest
    e = pl.program_id(2)

    @pl.when(e == 0)
    def _():
        acc_ref[...] = jnp.zeros_like(acc_ref)

    h = h_ref[...]
    gt = jnp.dot(h, wg_ref[...], preferred_element_type=F32)
    up = jnp.dot(h, wu_ref[...], preferred_element_type=F32)
    act = (gt * jax.nn.sigmoid(gt) * up).astype(BF16)
    y = jnp.dot(act, wd_ref[...], preferred_element_type=F32)
    gw = gw_ref[...]
    ge = jnp.sum(jnp.where(_lane_iota(gw.shape) == e, gw, 0.0), axis=1, keepdims=True)
    acc_ref[...] += ge * y

    @pl.when(e == pl.num_programs(2) - 1)
    def _():
        x = x_ref[...] + g2_ref[...] * acc_ref[...]
        o_ref[...] = _final_norm(x, fg_ref[...]) if final else x


def _moe(x, h, g2, gw, wg, wu, wd, final_g, tm):
    bsz, t, d = x.shape
    n_e, _, dff = wg.shape
    mrows = g2.shape[1]
    mblk = 1 if mrows == 1 else tm
    mod_spec = pl.BlockSpec((None, mblk, d), (lambda b, i, e: (b, 0, 0)) if mrows == 1 else (lambda b, i, e: (b, i, 0)))
    tok = pl.BlockSpec((None, tm, d), lambda b, i, e: (b, i, 0))
    in_specs = [tok, tok, mod_spec, pl.BlockSpec((None, tm, LANES), lambda b, i, e: (b, i, 0)),
                pl.BlockSpec((None, d, dff), lambda b, i, e: (e, 0, 0)),
                pl.BlockSpec((None, d, dff), lambda b, i, e: (e, 0, 0)),
                pl.BlockSpec((None, dff, d), lambda b, i, e: (e, 0, 0))]
    args = [x, h, g2, gw, wg, wu, wd]
    if final_g is not None:
        in_specs.append(pl.BlockSpec((1, d), lambda b, i, e: (0, 0)))
        args.append(final_g.reshape(1, d))
    return pl.pallas_call(
        functools.partial(_moe_kernel, final=final_g is not None),
        grid=(bsz, t // tm, n_e),
        in_specs=in_specs, out_specs=tok,
        out_shape=jax.ShapeDtypeStruct((bsz, t, d), F32),
        scratch_shapes=[pltpu.VMEM((tm, d), F32)],
        compiler_params=_cparams(("parallel", "parallel", "arbitrary")),
        name="moe_dense",
    )(*args)


def _rope_tables(pos):
    inv = ROPE_THETA ** (-jnp.arange(HALF, dtype=F32) / HALF)
    ang = pos.astype(F32)[:, None] * inv[None, :]
    cos, sin = jnp.cos(ang), jnp.sin(ang)
    return jnp.tile(cos, (1, 4)), jnp.tile(jnp.concatenate([-sin, sin], axis=1), (1, 2))


def _pick_tile(n, pref):
    t = min(n, pref)
    while n % t:
        t //= 2
    return t


def _key_blocks(past_t, new_t, bsz, t, lpad):
    lead = new_t.shape[:-2]
    f = new_t.shape[-2]
    new_b = jnp.moveaxis(new_t.reshape(*lead, f, bsz, t), -2, 0)
    full = jnp.concatenate([past_t, new_b], axis=-1)
    full = jnp.pad(full, [(0, 0)] * (full.ndim - 1) + [(0, lpad - full.shape[-1])])
    full = full.reshape(bsz, *lead, f, lpad // KV_BLOCK, KV_BLOCK)
    return jnp.moveaxis(full, -2, -3)


def _mixers(inp, past, n_past, lw, bsz, t):
    aq, iq, v12, kkt, ikt, bq, bkt, bv1, cu, sm, smt = inp
    n_keys = n_past + t
    n_sel = min(TOPK_MAX, n_keys // 4)
    lpad = -(-n_keys // KV_BLOCK) * KV_BLOCK
    if past is None:
        v12_all, kkt_all, ikt_all, bkt_all, bv1_all, logft_all = v12, kkt, ikt, bkt, bv1, smt
        hist16 = jnp.zeros((bsz, 16, POOL_WIDTH), F32)
    else:
        pa, pb, plf, pc = past
        ones = jnp.ones((bsz, n_past, HEAD_DIM), BF16)
        pk, pv, pik = (pa[:, :, j].astype(BF16) for j in range(3))

        def pad_rows(a):
            return jnp.pad(a, ((0, 0), (0, lpad - a.shape[1]), (0, 0)))

        v12_all = pad_rows(jnp.concatenate(
            [jnp.concatenate([pv, ones, ones, pv], axis=-1), v12.reshape(bsz, t, 256)], axis=1))
        pv6 = pb[:, :, 1].astype(BF16)
        pv1 = jnp.concatenate(
            [jnp.concatenate([pv6[:, :, 2 * p], ones, ones, pv6[:, :, 2 * p + 1]], axis=-1) for p in range(3)], axis=-1)
        bv1_all = pad_rows(jnp.concatenate([pv1, bv1.reshape(bsz, t, 768)], axis=1))
        kkt_all = _key_blocks(jnp.swapaxes(jnp.concatenate([pk, pk], axis=-1), 1, 2), kkt[0, 0], bsz, t, lpad)
        ikt_all = _key_blocks(jnp.swapaxes(jnp.concatenate([pik, pik], axis=-1), 1, 2), ikt[0, 0], bsz, t, lpad)
        pbk = pb[:, :, 0].astype(BF16).reshape(bsz, n_past, 3, LANES)
        bkt_all = _key_blocks(jnp.transpose(pbk, (0, 2, 3, 1)), bkt[0, :, 0], bsz, t, lpad)
        plt = jnp.pad(jnp.swapaxes(plf, 1, 2), ((0, 0), (0, 8 - B_HEADS), (0, 0)))
        new_lt = jnp.moveaxis(smt[0].reshape(8, bsz, t), 1, 0)
        logft_all = jnp.concatenate([plt, new_lt], axis=-1)
        logft_all = jnp.pad(logft_all, ((0, 0), (0, 0), (0, lpad - n_keys)))
        hist16 = jnp.pad(pc, ((0, 0), (1, 0), (0, 0)))

    aq, iq, sm, bq, cu = (a.reshape(bsz, t, a.shape[-1]) for a in (aq, iq, sm, bq, cu))
    oa = _dsa(aq, iq, sm, ikt_all, kkt_all, v12_all, tq=_pick_tile(t, 128), n_keys=n_keys, q_pos0=n_past,
              n_sel=n_sel)
    ck = _cumsum_lanes(logft_all, KV_BLOCK)
    ob = _fox(bq, bkt_all, bv1_all, ck, tq=_pick_tile(t, 256), n_keys=n_keys, q_pos0=n_past)
    oc = _pool(cu, hist16, lw["pool_bd"], lw["pool_scale"], tc=_pick_tile(t, 256), start_pos=n_past)
    return oa, ob, oc


def _layer(x, mod, past, n_past, pos_tab, lw, layer, final_g, per_token):
    bsz, t, d = x.shape
    sh1, sc1, g1, sh2, sc2, g2 = mod
    if per_token:
        xt = x.reshape(1, bsz * t, d)
        sh1, sc1, g1, sh2, sc2, g2 = (jnp.broadcast_to(m, (bsz, t, d)).reshape(1, bsz * t, d) for m in mod)
        cos, sin = (jnp.tile(a, (bsz, 1)) for a in pos_tab)
    else:
        xt = x
        cos, sin = pos_tab
    tm = KV_BLOCK
    (aq, iq, na, v12, kkt, ikt, nb, bq, bkt, bv1, cu, sm, smt, gates) = _in_proj(
        xt, sc1, sh1, lw["norm_mix_g"], lw["w_in"], lw["bf_bias"], cos, sin, tm)
    oa, ob, oc = _mixers((aq, iq, v12, kkt, ikt, bq, bkt, bv1, cu, sm, smt), past, n_past, lw, bsz, t)

    def flat(a):
        return a.reshape(xt.shape[0], xt.shape[1], a.shape[-1])

    router = (lw["router_w"], lw["router_b"]) if layer % 2 else None
    res = _merge(xt, flat(oa), flat(ob), flat(oc), gates, g1, sc2, sh2, lw["norm_ffn_g"],
                 lw["w_br_a"], lw["w_br_b"], lw["w_br_c"], lw["w_out"], router, tm)
    tmf = _pick_tile(xt.shape[1], 512)
    if layer % 2 == 0:
        x_mid, h2 = res
        x_new = _ffn(x_mid, h2, g2, lw["ffn_wg"], lw["ffn_wu"], lw["ffn_wd"], final_g, tmf)
    else:
        x_mid, h2, gw = res
        x_new = _moe(x_mid, h2, g2, gw, lw["moe_wg"], lw["moe_wu"], lw["moe_wd"], final_g, tmf)
    new_a = na.reshape(bsz, t, 3, HEAD_DIM)
    new_b = nb.reshape(bsz, t, 2, B_HEADS, HEAD_DIM)
    new_logf = sm.reshape(bsz, t, LANES)[:, :, :B_HEADS]
    new_pool = cu.reshape(bsz, t, POOL_WIDTH)[:, t - POOL_HIST:, :]
    return x_new.reshape(bsz, t, d), (new_a, new_b, new_logf, new_pool)


def kernel(x_prompt, x_sample, cache_a_kvi, cache_b_kv, cache_b_logf, state_c_pool, c_prompt, c_sample,
           ada_w, ada_b, norm_mix_g, w_in, b_forget, pool_w, pool_scale, w_br_a, w_br_b, w_br_c, w_out,
           norm_ffn_g, ffn_w_gate, ffn_w_up, ffn_w_down, moe_router_w, moe_router_b, moe_w_gate,
           moe_w_up, moe_w_down, final_norm_g):
    depth = ada_w.shape[0]
    bp, tp, d = x_prompt.shape
    bs, ts, _ = x_sample.shape
    n_past = cache_a_kvi.shape[2]
    assert tp % KV_BLOCK == 0 and (bs * ts) % KV_BLOCK == 0

    rows = -(-(bp + bs) // 8) * 8
    c_all = jnp.pad(jnp.concatenate([c_prompt, c_sample], axis=0), ((0, rows - bp - bs), (0, 0)))
    mod_all = _ada(c_all, ada_w, ada_b)

    tab_p = _rope_tables(jnp.arange(tp))
    tab_s = _rope_tables(n_past + jnp.arange(ts))

    xp, xs = x_prompt, x_sample
    outs_p, outs_s = [], []
    for layer in range(depth):
        j = layer // 2
        w_l, bias_l = _in_weights(w_in[layer], b_forget[layer])
        pw = pool_w[layer]
        pool_bd = jnp.zeros((POOL_WIDTH, POOL_WIDTH), F32)
        for g in range(len(POOL_WINDOWS)):
            sl = slice(g * POOL_GROUP_DIM, (g + 1) * POOL_GROUP_DIM)
            pool_bd = pool_bd.at[sl, sl].set(pw[g])
        lw = dict(w_in=w_l, bf_bias=bias_l, norm_mix_g=norm_mix_g[layer], norm_ffn_g=norm_ffn_g[layer],
                  pool_bd=pool_bd.astype(BF16), pool_scale=pool_scale[layer],
                  w_br_a=w_br_a[layer].astype(BF16), w_br_b=w_br_b[layer].astype(BF16),
                  w_br_c=w_br_c[layer].astype(BF16), w_out=w_out[layer].astype(BF16))
        if layer % 2 == 0:
            lw.update(ffn_wg=ffn_w_gate[j].astype(BF16), ffn_wu=ffn_w_up[j].astype(BF16),
                      ffn_wd=ffn_w_down[j].astype(BF16))
        else:
            lw.update(router_w=jnp.pad(moe_router_w[j], ((0, 0), (0, LANES - N_EXPERTS))),
                      router_b=jnp.pad(moe_router_b[j], (0, LANES - N_EXPERTS)).reshape(1, LANES),
                      moe_wg=moe_w_gate[j].astype(BF16), moe_wu=moe_w_up[j].astype(BF16),
                      moe_wd=moe_w_down[j].astype(BF16))
        final_g = final_norm_g if layer == depth - 1 else None
        mod_p = [m[:, None, :] for m in jnp.split(mod_all[layer, :bp], 6, axis=-1)]
        mod_s = [m[:, None, :] for m in jnp.split(mod_all[layer, bp:bp + bs], 6, axis=-1)]
        xp, new_p = _layer(xp, mod_p, None, 0, tab_p, lw, layer, final_g, per_token=False)
        past = (cache_a_kvi[layer], cache_b_kv[layer], cache_b_logf[layer], state_c_pool[layer])
        xs, new_s = _layer(xs, mod_s, past, n_past, tab_s, lw, layer, final_g, per_token=True)
        outs_p.append(new_p)
        outs_s.append(new_s)

    def stack(outs, k):
        return jnp.stack([o[k] for o in outs])

    return (xp, xs,
            stack(outs_p, 0), stack(outs_p, 1), stack(outs_p, 2), stack(outs_p, 3),
            stack(outs_s, 0), stack(outs_s, 1), stack(outs_s, 2), stack(outs_s, 3))
```

```python
import functools

import jax
import jax.numpy as jnp
import numpy as np
from jax import lax
from jax.experimental import pallas as pl
from jax.experimental.pallas import tpu as pltpu

F32 = jnp.float32
BF16 = jnp.bfloat16
I32 = jnp.int32

D_MODEL = 1024
CHUNK = 64
HEAD_DIM = 64
HALF = HEAD_DIM // 2
ROPE_THETA = 10000.0
NORM_EPS = 1e-6
A_HEADS = 6
IDX_HEADS = 4
TOPK_MAX = 256
B_HEADS = 6
POOL_WINDOWS = (2, 4, 8, 16)
POOL_GROUP_DIM = 64
POOL_WIDTH = 256
POOL_HIST = 15
N_EXPERTS = 8
LANES = 128
SUBLANES = 8
LOG2E = 1.4426950408889634
QK_SCALE = HEAD_DIM ** -0.5 * LOG2E
KV_BLOCK = 256
V_ROWS = HEAD_DIM + 16
VMEM_LIMIT = 56 * 1024 * 1024
NEG_INF = float("-inf")
INT_MIN = -2 ** 31
KEY_NEG_INF = -2 ** 31 + 0x7FFFFF
KEY_POS_INF = 0x7F800000

C_AQ, C_IQ, C_A, C_B, C_CU, C_SM, C_GATE, C_END = 0, 384, 640, 896, 2048, 2304, 2432, 5504


def _cparams(sem):
    return pltpu.CompilerParams(dimension_semantics=sem, vmem_limit_bytes=VMEM_LIMIT)


def _const_spec(shape):
    nd = len(shape)
    return pl.BlockSpec(shape, lambda *_: (0,) * nd, pipeline_mode=pl.Buffered(1))


def _lane_iota(shape):
    return lax.broadcasted_iota(I32, shape, len(shape) - 1)


def _row_iota(shape):
    return lax.broadcasted_iota(I32, shape, len(shape) - 2)


def _ada_kernel(c_ref, w_ref, b_ref, o_ref):
    c = c_ref[...]
    s = c * jax.nn.sigmoid(c)
    o_ref[...] = jnp.dot(s, w_ref[...], preferred_element_type=F32,
                         precision=lax.Precision.HIGHEST) + b_ref[...]


def _ada(c_all, ada_w, ada_b):
    depth, d, n = ada_w.shape
    rows = c_all.shape[0]
    tn = 1536
    return pl.pallas_call(
        _ada_kernel,
        grid=(depth, n // tn),
        in_specs=[pl.BlockSpec((rows, d), lambda l, j: (0, 0)),
                  pl.BlockSpec((None, d, tn), lambda l, j: (l, 0, j)),
                  pl.BlockSpec((None, 1, tn), lambda l, j: (l, 0, j))],
        out_specs=pl.BlockSpec((None, rows, tn), lambda l, j: (l, 0, j)),
        out_shape=jax.ShapeDtypeStruct((depth, rows, n), F32),
        compiler_params=_cparams(("arbitrary", "arbitrary")),
        name="ada_mod",
    )(c_all, ada_w, ada_b.reshape(depth, 1, n))


def _in_kernel(x_ref, sc_ref, sh_ref, g_ref, w_ref, bf_ref, cos_ref, sin_ref,
               aqt_ref, iqt_ref, na_ref, kik_ref, avt_ref, nb_ref, bqt_ref, bk_ref, bvt_ref,
               cu_ref, sm_ref, smt_ref, gate_ref):
    x = x_ref[...]
    ms = jnp.mean(x * x, axis=-1, keepdims=True)
    y = x * lax.rsqrt(ms + NORM_EPS) * g_ref[...]
    h = (y * (1.0 + sc_ref[...]) + sh_ref[...]).astype(BF16)
    tm = x.shape[0]

    def mm(a, b):
        return jnp.dot(h, w_ref[:, a:b], preferred_element_type=F32)

    cos = cos_ref[...]
    sin = sin_ref[...]
    lane = _lane_iota((tm, LANES))
    low = lane < HEAD_DIM
    first_half = (lane & HALF) == 0

    def rope(z):
        swapped = jnp.where(first_half, pltpu.roll(z, LANES - HALF, 1), pltpu.roll(z, HALF, 1))
        return z * cos + swapped * sin

    zeros64 = jnp.zeros((HEAD_DIM, tm), BF16)
    ones_rows = jnp.where(_row_iota((V_ROWS - HEAD_DIM, tm)) == 0, 1.0, 0.0).astype(BF16)

    def put_heads(ref, zt, p, slot_even, slot_odd):
        for hh, slot in ((0, slot_even), (1, slot_odd)):
            base = (2 * p + hh) * LANES
            ref[base + slot * HEAD_DIM:base + (slot + 1) * HEAD_DIM, :] = zt[hh * HEAD_DIM:(hh + 1) * HEAD_DIM]
            ref[base + (1 - slot) * HEAD_DIM:base + (2 - slot) * HEAD_DIM, :] = zeros64

    z = mm(C_AQ, C_IQ)
    for p in range(3):
        zt = (rope(z[:, p * LANES:(p + 1) * LANES]) * QK_SCALE).T.astype(BF16)
        put_heads(aqt_ref, zt, p, 0, 0)
    z = mm(C_IQ, C_A)
    for p in range(2):
        zt = rope(z[:, p * LANES:(p + 1) * LANES]).T.astype(BF16)
        put_heads(iqt_ref, zt, p, 1, 1)

    z = mm(C_A, C_B)
    kv = z[:, :LANES]
    r0 = jnp.where(low, rope(kv), kv)
    r1 = rope(z[:, LANES:])
    na_ref[...] = jnp.concatenate([r0, r1], axis=1)[:, :3 * HEAD_DIM]
    kik_ref[...] = jnp.where(low, r0, pltpu.roll(r1, HEAD_DIM, 1)).astype(BF16)
    avt_ref[0:HEAD_DIM, :] = r0.T[HEAD_DIM:, :].astype(BF16)
    avt_ref[HEAD_DIM:, :] = ones_rows

    z = mm(C_B, C_CU)
    nb_ref[...] = z[:, 384:]
    bk_ref[...] = z[:, 384:768].astype(BF16)
    for p in range(3):
        zt = (z[:, p * LANES:(p + 1) * LANES] * QK_SCALE).T.astype(BF16)
        put_heads(bqt_ref, zt, p, 0, 1)
        vt = z[:, 768 + p * LANES:768 + (p + 1) * LANES].T.astype(BF16)
        for hh in range(2):
            bvt_ref[2 * p + hh, 0:HEAD_DIM, :] = vt[hh * HEAD_DIM:(hh + 1) * HEAD_DIM]
            bvt_ref[2 * p + hh, HEAD_DIM:, :] = ones_rows

    cu_ref[...] = mm(C_CU, C_SM)

    z = mm(C_SM, C_GATE)
    t = z + bf_ref[...]
    logf = jnp.minimum(t, 0.0) - jnp.log1p(jnp.exp(-jnp.abs(t)))
    sm = jnp.where(lane < B_HEADS, logf, z)
    sm_ref[...] = sm
    smt_ref[...] = sm.T[0:16, :]

    for c in range(3):
        gate_ref[:, c * D_MODEL:(c + 1) * D_MODEL] = jax.nn.sigmoid(
            mm(C_GATE + c * D_MODEL, C_GATE + (c + 1) * D_MODEL))


def _in_weights(w_in_l, b_forget_l):
    d = w_in_l.shape[0]
    sizes = (384, 64, 64, 256, 4, 64, 384, 384, 384, 6, 256, 3072)
    o = np.concatenate([[0], np.cumsum(sizes)])
    cols = [w_in_l[:, o[0]:o[1]],
            w_in_l[:, o[3]:o[4]],
            w_in_l[:, o[1]:o[3]], w_in_l[:, o[5]:o[6]], jnp.zeros((d, 64), F32),
            w_in_l[:, o[6]:o[9]],
            w_in_l[:, o[10]:o[11]],
            w_in_l[:, o[9]:o[10]], jnp.zeros((d, 2), F32), w_in_l[:, o[4]:o[5]], jnp.zeros((d, LANES - 12), F32),
            w_in_l[:, o[11]:o[12]]]
    w = jnp.concatenate(cols, axis=1).astype(BF16)
    assert w.shape[1] == C_END, w.shape
    bias = jnp.concatenate([b_forget_l, jnp.zeros((LANES - B_HEADS,), F32)]).reshape(1, LANES)
    return w, bias


def _in_proj(x, sc, sh, g, w, bias, cos, sin, tm):
    bsz, t, d = x.shape
    mrows = sc.shape[1]
    mblk = 1 if mrows == 1 else tm
    mod_spec = pl.BlockSpec((None, mblk, d), (lambda b, i: (b, 0, 0)) if mrows == 1 else (lambda b, i: (b, i, 0)))
    nblk = t // tm

    def rows(n, dt):
        return pl.BlockSpec((None, tm, n), lambda b, i: (b, i, 0)), jax.ShapeDtypeStruct((bsz, t, n), dt)

    def cols(n, dt):
        return pl.BlockSpec((None, n, tm), lambda b, i: (b, 0, i)), jax.ShapeDtypeStruct((bsz, n, t), dt)

    outs = [cols(A_HEADS * LANES, BF16), cols(IDX_HEADS * LANES, BF16), rows(192, F32), rows(LANES, BF16),
            (pl.BlockSpec((None, None, V_ROWS, tm), lambda b, i: (b, i, 0, 0)),
             jax.ShapeDtypeStruct((bsz, nblk, V_ROWS, tm), BF16)),
            rows(768, F32), cols(B_HEADS * LANES, BF16), rows(384, BF16),
            (pl.BlockSpec((None, B_HEADS, None, V_ROWS, tm), lambda b, i: (b, 0, i, 0, 0)),
             jax.ShapeDtypeStruct((bsz, B_HEADS, nblk, V_ROWS, tm), BF16)),
            rows(256, F32), rows(LANES, F32), cols(16, F32), rows(3 * D_MODEL, F32)]
    return pl.pallas_call(
        _in_kernel,
        grid=(bsz, nblk),
        in_specs=[pl.BlockSpec((None, tm, d), lambda b, i: (b, i, 0)), mod_spec, mod_spec,
                  _const_spec((1, d)), _const_spec(w.shape), _const_spec((1, LANES)),
                  pl.BlockSpec((tm, LANES), lambda b, i: (i, 0)),
                  pl.BlockSpec((tm, LANES), lambda b, i: (i, 0))],
        out_specs=[o[0] for o in outs],
        out_shape=[o[1] for o in outs],
        compiler_params=_cparams(("parallel", "parallel")),
        name="in_proj",
    )(x, sc, sh, g.reshape(1, d), w, bias, cos, sin)


def _cum_kernel(x_ref, o_ref, carry_ref):
    @pl.when(pl.program_id(1) == 0)
    def _():
        carry_ref[...] = jnp.zeros_like(carry_ref)

    tc = x_ref.shape[0]
    n = B_HEADS * LANES
    spread = jnp.where(_lane_iota((LANES, n)) // LANES == _row_iota((LANES, n)), 1.0, 0.0)
    tri = jnp.where(_lane_iota((tc, tc)) <= _row_iota((tc, tc)), 1.0, 0.0)
    wide = jnp.dot(x_ref[...], spread, preferred_element_type=F32, precision=lax.Precision.HIGHEST)
    cum = jnp.dot(tri, wide, preferred_element_type=F32, precision=lax.Precision.HIGHEST) + carry_ref[0:1, :]
    for hd in range(B_HEADS):
        o_ref[hd] = cum[:, hd * LANES:(hd + 1) * LANES] * LOG2E
    carry_ref[...] = jnp.broadcast_to(cum[tc - 1:tc, :], carry_ref.shape)


def _cum_logf(x, tc):
    bsz, t, n = x.shape
    return pl.pallas_call(
        _cum_kernel,
        grid=(bsz, t // tc),
        in_specs=[pl.BlockSpec((None, tc, n), lambda b, i: (b, i, 0))],
        out_specs=pl.BlockSpec((None, B_HEADS, tc, LANES), lambda b, i: (b, 0, i, 0)),
        out_shape=jax.ShapeDtypeStruct((bsz, B_HEADS, t, LANES), F32),
        scratch_shapes=[pltpu.VMEM((SUBLANES, B_HEADS * LANES), F32)],
        compiler_params=_cparams(("parallel", "arbitrary")),
        name="logf_cumsum",
    )(x)


def _sum_keys(x):
    part = x.reshape(x.shape[0] // SUBLANES, SUBLANES, x.shape[1]).sum(axis=0)
    return jnp.sum(part, axis=0, keepdims=True)


def _max_keys(x):
    part = x.reshape(x.shape[0] // SUBLANES, SUBLANES, x.shape[1]).max(axis=0)
    return jnp.max(part, axis=0, keepdims=True)


def _dsa_kernel(aqt_ref, iqt_ref, smt_ref, kik_ref, avt_ref, o_ref, key_ref,
                *, tq, tk, n_keys, q_pos0, n_sel):
    i = pl.program_id(1)
    pos_first = q_pos0 + i * tq
    last_chunk = (pos_first + tq - 1) // CHUNK
    n_adm = jnp.minimum((last_chunk + 1) * CHUNK, n_keys)
    nkb = (n_adm + tk - 1) // tk
    n_pairs = (nkb + 1) // 2

    q_pos = pos_first + _lane_iota((1, tq))
    q_lim = jnp.minimum((q_pos // CHUNK + 1) * CHUNK, n_keys)
    key_row = _row_iota((tk, tq))

    def keys(kb):
        return kik_ref[pl.ds(pl.multiple_of(kb * tk, tk), tk), :]

    iq4 = jnp.concatenate([iqt_ref[hd * LANES:(hd + 1) * LANES, :] for hd in range(IDX_HEADS)], axis=1)
    smt = smt_ref[...]
    w_rows = [smt[8 + hd:9 + hd, :] for hd in range(IDX_HEADS)]

    def score_body(kb, carry):
        s4 = jnp.dot(keys(kb), iq4, preferred_element_type=F32)
        score = w_rows[0] * jnp.maximum(s4[:, 0:tq], 0.0)
        for hd in range(1, IDX_HEADS):
            score = score + w_rows[hd] * jnp.maximum(s4[:, hd * tq:(hd + 1) * tq], 0.0)
        score = score + 0.0
        score = jnp.where(key_row < q_lim - kb * tk, score, NEG_INF)
        bits = lax.bitcast_convert_type(score, I32)
        key_ref[kb] = jnp.where(bits < 0, bits ^ 0x7FFFFFFF, bits)
        return carry

    lax.fori_loop(0, nkb, score_body, 0)

    @pl.when(nkb % 2 == 1)
    def _():
        key_ref[nkb] = jnp.full((tk, tq), INT_MIN, I32)

    def count(cand, strict):
        def hits(blk):
            h = (blk > cand) if strict else (blk >= cand)
            h = jnp.where(h, 1.0, 0.0)
            return h.reshape(tk // SUBLANES, SUBLANES, tq).sum(axis=0)

        def body(j, acc):
            return acc + hits(key_ref[2 * j]) + hits(key_ref[2 * j + 1])

        acc = lax.fori_loop(0, n_pairs, body, jnp.zeros((SUBLANES, tq), F32))
        return jnp.sum(acc, axis=0, keepdims=True)

    def bit_body(b, t_b):
        bit = lax.shift_left(jnp.int32(1), 31 - b)
        cand_b = t_b | bit
        cnt = count(cand_b ^ INT_MIN, False)
        return jnp.where(cnt >= n_sel, cand_b, t_b)

    t_b = lax.fori_loop(0, 32, bit_body, jnp.zeros((1, tq), I32))
    thr = t_b ^ INT_MIN
    need = n_sel - count(thr, True)
    thr_lo = jnp.maximum(thr, KEY_NEG_INF + 1)
    n_ge = count(thr, False)
    n_nonfinite = count(jnp.full((1, tq), KEY_POS_INF, I32), False)
    partial_ties = jnp.where((thr > KEY_NEG_INF) & (n_ge > n_sel), 1.0, 0.0)
    general = jnp.max(partial_ties + n_nonfinite) > 0.0

    aq6 = jnp.concatenate([aqt_ref[hd * LANES:(hd + 1) * LANES, :] for hd in range(A_HEADS)], axis=1)

    def attend(with_ties):
        lower = jnp.where(_lane_iota((tk, tk)) <= _row_iota((tk, tk)), 1.0, 0.0).astype(BF16)

        def body(kb, carry):
            eq_seen, ms, accs = carry
            blk = key_ref[kb]
            if with_ties:
                eq = blk == thr
                eq_f = jnp.where(eq, 1.0, 0.0)
                pref = jnp.dot(lower, eq_f.astype(BF16), preferred_element_type=F32) + eq_seen
                slack = jnp.where(blk >= thr_lo, need - jnp.where(eq, pref, 0.0), -1.0)
                bias = jnp.where(slack >= 0.0, jnp.where(blk < KEY_POS_INF, 0.0, NEG_INF), NEG_INF)
                eq_seen = eq_seen + _sum_keys(eq_f)
            else:
                bias = jnp.where(blk >= thr_lo, 0.0, NEG_INF)
            logits = jnp.dot(keys(kb), aq6, preferred_element_type=F32)
            vts = (avt_ref[2 * kb], avt_ref[2 * kb + 1])
            new_ms, new_accs = [], []
            for p in range(A_HEADS // 2):
                ps, alphas = [], []
                for hd in (2 * p, 2 * p + 1):
                    lg = logits[:, hd * tq:(hd + 1) * tq] + bias
                    m_old = ms[hd]
                    m_new = jnp.maximum(m_old, _max_keys(lg))
                    m_safe = jnp.where(m_new == NEG_INF, 0.0, m_new)
                    ps.append(jnp.exp2(lg - m_safe).astype(BF16))
                    alphas.append(jnp.exp2(m_old - m_safe))
                    new_ms.append(m_new)
                p2 = jnp.concatenate(ps, axis=1)
                pv = (jnp.dot(vts[0], p2[:tk // 2], preferred_element_type=F32)
                      + jnp.dot(vts[1], p2[tk // 2:], preferred_element_type=F32))
                new_accs.append(jnp.concatenate(alphas, axis=1) * accs[p] + pv)
            return eq_seen, tuple(new_ms), tuple(new_accs)

        init = (jnp.zeros((1, tq), F32),
                tuple(jnp.full((1, tq), NEG_INF, F32) for _ in range(A_HEADS)),
                tuple(jnp.zeros((V_ROWS, 2 * tq), F32) for _ in range(A_HEADS // 2)))
        return lax.fori_loop(0, nkb, body, init)[2]

    accs = lax.cond(general, lambda: attend(True), lambda: attend(False))

    outs = []
    for p in range(A_HEADS // 2):
        o2 = accs[p][0:HEAD_DIM] / accs[p][HEAD_DIM:HEAD_DIM + 1]
        outs += [o2[:, 0:tq], o2[:, tq:2 * tq]]
    o_ref[...] = jnp.concatenate(outs, axis=0).T.astype(o_ref.dtype)


def _dsa(aqt, iqt, smt, kik, avt, *, tq, n_keys, q_pos0, n_sel):
    bsz, _, t_q = aqt.shape
    _, nblk, _, tkv = avt.shape
    assert nblk % 2 == 0
    tk = 2 * tkv
    kern = functools.partial(_dsa_kernel, tq=tq, tk=tk, n_keys=n_keys, q_pos0=q_pos0, n_sel=n_sel)
    return pl.pallas_call(
        kern,
        grid=(bsz, t_q // tq),
        in_specs=[pl.BlockSpec((None, A_HEADS * LANES, tq), lambda b, i: (b, 0, i)),
                  pl.BlockSpec((None, IDX_HEADS * LANES, tq), lambda b, i: (b, 0, i)),
                  pl.BlockSpec((None, 16, tq), lambda b, i: (b, 0, i)),
                  pl.BlockSpec((None, nblk * tkv, LANES), lambda b, i: (b, 0, 0)),
                  pl.BlockSpec((None, nblk, V_ROWS, tkv), lambda b, i: (b, 0, 0, 0))],
        out_specs=pl.BlockSpec((None, tq, 384), lambda b, i: (b, i, 0)),
        out_shape=jax.ShapeDtypeStruct((bsz, t_q, 384), BF16),
        scratch_shapes=[pltpu.VMEM((2 * ((nblk // 2 + 1) // 2), tk, tq), I32)],
        compiler_params=_cparams(("parallel", "arbitrary")),
        name="dsa_attention",
    )(aqt, iqt, smt, kik, avt)


def _fox_kernel(qt_ref, k_ref, vt_ref, ck_ref, o_ref, *, tq, tk, n_keys, q_pos0):
    i = pl.program_id(2)
    pos_first = q_pos0 + i * tq
    n_full = pos_first // tk
    nkb = (jnp.minimum(pos_first + tq, n_keys) + tk - 1) // tk
    q_pos = pos_first + _lane_iota((1, tq))
    key_row = _row_iota((tk, tq))
    qts = (qt_ref[0:LANES, :], qt_ref[LANES:2 * LANES, :])

    def step(kbs, state, masked):
        logits = []
        for hh in range(2):
            for kb in kbs:
                rows = pl.ds(pl.multiple_of(kb * tk, tk), tk)
                ck = ck_ref[hh, rows, :]
                lg = (jnp.dot(k_ref[rows, :], qts[hh], preferred_element_type=F32)
                      - jnp.concatenate([ck] * (tq // LANES), axis=1))
                if masked:
                    lg = jnp.where(key_row <= q_pos - kb * tk, lg, NEG_INF)
                logits.append(lg)
        new = []
        for hh in range(2):
            m_old, acc = state[hh]
            lgs = logits[hh * len(kbs):(hh + 1) * len(kbs)]
            m_new = m_old
            for lg in lgs:
                m_new = jnp.maximum(m_new, _max_keys(lg))
            m_safe = jnp.where(m_new == NEG_INF, 0.0, m_new) if masked else m_new
            acc = jnp.exp2(m_old - m_safe) * acc
            for kb, lg in zip(kbs, lgs):
                p = jnp.exp2(lg - m_safe).astype(BF16)
                acc = acc + jnp.dot(vt_ref[hh, kb], p, preferred_element_type=F32)
            new.append((m_new, acc))
        return tuple(new)

    init = tuple((jnp.full((1, tq), NEG_INF, F32), jnp.zeros((V_ROWS, tq), F32)) for _ in range(2))
    state = lax.fori_loop(0, n_full // 2, lambda j, st: step((2 * j, 2 * j + 1), st, False), init)
    state = lax.fori_loop(2 * (n_full // 2), nkb, lambda kb, st: step((kb,), st, True), state)
    outs = [acc[0:HEAD_DIM] / acc[HEAD_DIM:HEAD_DIM + 1] for _, acc in state]
    o_ref[...] = jnp.concatenate(outs, axis=0).T.astype(o_ref.dtype)


def _fox(qt, k, vt, ck, *, tq, n_keys, q_pos0):
    bsz, _, t_q = qt.shape
    _, _, nblk, _, tk = vt.shape
    lpad = nblk * tk
    kern = functools.partial(_fox_kernel, tq=tq, tk=tk, n_keys=n_keys, q_pos0=q_pos0)
    return pl.pallas_call(
        kern,
        grid=(bsz, 3, t_q // tq),
        in_specs=[pl.BlockSpec((None, 2 * LANES, tq), lambda b, p, i: (b, p, i)),
                  pl.BlockSpec((None, lpad, LANES), lambda b, p, i: (b, 0, p)),
                  pl.BlockSpec((None, 2, nblk, V_ROWS, tk), lambda b, p, i: (b, p, 0, 0, 0)),
                  pl.BlockSpec((None, 2, lpad, LANES), lambda b, p, i: (b, p, 0, 0))],
        out_specs=pl.BlockSpec((None, tq, LANES), lambda b, p, i: (b, i, p)),
        out_shape=jax.ShapeDtypeStruct((bsz, t_q, 384), BF16),
        compiler_params=_cparams(("parallel", "parallel", "arbitrary")),
        name="fox_attention",
    )(qt, k, vt, ck)


def _pool_kernel(cur_ref, prev_ref, hist_ref, w_ref, s_ref, o_ref, ext, *, tc, start_pos):
    i = pl.program_id(1)
    cur = cur_ref[...]
    ext[0:16, :] = jnp.where(i == 0, hist_ref[...], prev_ref[tc - 16:, :])
    ext[16:, :] = cur
    pos = start_pos + i * tc + lax.broadcasted_iota(I32, (tc, POOL_WIDTH), 0)
    lane = _lane_iota((tc, POOL_WIDTH))
    run = cur
    pooled = jnp.zeros_like(cur)
    k = 1
    for g, w in enumerate(POOL_WINDOWS):
        while k < w:
            run = run + ext[16 - k:16 - k + tc, :]
            k += 1
        cnt = jnp.minimum(pos + 1, w).astype(F32)
        in_group = (lane >= g * POOL_GROUP_DIM) & (lane < (g + 1) * POOL_GROUP_DIM)
        pooled = jnp.where(in_group, run / cnt, pooled)
    z = (pooled - cur).astype(BF16)
    o_ref[...] = (jnp.dot(z, w_ref[...], preferred_element_type=F32) * s_ref[...]).astype(o_ref.dtype)


def _pool(cu, hist16, w_bd, scale, *, tc, start_pos):
    bsz, t, n = cu.shape
    kern = functools.partial(_pool_kernel, tc=tc, start_pos=start_pos)
    return pl.pallas_call(
        kern,
        grid=(bsz, t // tc),
        in_specs=[pl.BlockSpec((None, tc, n), lambda b, i: (b, i, 0)),
                  pl.BlockSpec((None, tc, n), lambda b, i: (b, jnp.maximum(i - 1, 0), 0)),
                  pl.BlockSpec((None, 16, n), lambda b, i: (b, 0, 0)),
                  _const_spec((n, n)), _const_spec((1, n))],
        out_specs=pl.BlockSpec((None, tc, n), lambda b, i: (b, i, 0)),
        out_shape=jax.ShapeDtypeStruct((bsz, t, n), BF16),
        scratch_shapes=[pltpu.VMEM((16 + tc, n), F32)],
        compiler_params=_cparams(("parallel", "arbitrary")),
        name="pool_mixer",
    )(cu, cu, hist16, w_bd, scale.reshape(1, n))


def _route(logits):
    lane = _lane_iota(logits.shape).astype(F32)
    lg = jnp.where(lane < N_EXPERTS, logits, NEG_INF)
    m1 = jnp.max(lg, axis=1, keepdims=True)
    i1 = jnp.min(jnp.where(lg == m1, lane, float(LANES)), axis=1, keepdims=True)
    hot1 = lane == i1
    lg2 = jnp.where(hot1, NEG_INF, lg)
    m2 = jnp.max(lg2, axis=1, keepdims=True)
    i2 = jnp.min(jnp.where(lg2 == m2, lane, float(LANES)), axis=1, keepdims=True)
    hot2 = lane == i2
    e2 = jnp.exp(m2 - m1)
    den = 1.0 + e2
    return jnp.where(hot1, 1.0 / den, 0.0) + jnp.where(hot2, e2 / den, 0.0)


def _merge_kernel(x_ref, oa_ref, ob_ref, oc_ref, gate_ref, g1_ref, sc2_ref, sh2_ref, g_ref,
                  wa_ref, wb_ref, wc_ref, wo_ref, *rest, moe):
    if moe:
        rw_ref, rb_ref, xo_ref, h_ref, gw_ref = rest
    else:
        xo_ref, h_ref = rest
    d = D_MODEL
    merged = (gate_ref[:, 0:d] * jnp.dot(oa_ref[...], wa_ref[...], preferred_element_type=F32)
              + gate_ref[:, d:2 * d] * jnp.dot(ob_ref[...], wb_ref[...], preferred_element_type=F32)
              + gate_ref[:, 2 * d:3 * d] * jnp.dot(oc_ref[...], wc_ref[...], preferred_element_type=F32))
    x = x_ref[...] + g1_ref[...] * jnp.dot(merged.astype(BF16), wo_ref[...], preferred_element_type=F32)
    xo_ref[...] = x
    ms = jnp.mean(x * x, axis=-1, keepdims=True)
    y = x * lax.rsqrt(ms + NORM_EPS) * g_ref[...]
    h = y * (1.0 + sc2_ref[...]) + sh2_ref[...]
    h_ref[...] = h.astype(BF16)
    if moe:
        logits = jnp.dot(h, rw_ref[...], preferred_element_type=F32,
                         precision=lax.Precision.HIGHEST) + rb_ref[...]
        gw_ref[...] = _route(logits)


def _merge(x, oa, ob, oc, gates, g1, sc2, sh2, g, wa, wb, wc, wo, router, tm):
    bsz, t, d = x.shape
    mrows = g1.shape[1]
    mblk = 1 if mrows == 1 else tm
    mod_spec = pl.BlockSpec((None, mblk, d), (lambda b, i: (b, 0, 0)) if mrows == 1 else (lambda b, i: (b, i, 0)))

    def tok(n):
        return pl.BlockSpec((None, tm, n), lambda b, i: (b, i, 0))

    in_specs = [tok(d), tok(384), tok(384), tok(256), tok(3 * d), mod_spec, mod_spec, mod_spec,
                _const_spec((1, d)), _const_spec(wa.shape), _const_spec(wb.shape), _const_spec(wc.shape),
                _const_spec(wo.shape)]
    args = [x, oa, ob, oc, gates, g1, sc2, sh2, g.reshape(1, d), wa, wb, wc, wo]
    out_specs = [tok(d), tok(d)]
    out_shape = [jax.ShapeDtypeStruct((bsz, t, d), F32), jax.ShapeDtypeStruct((bsz, t, d), BF16)]
    if router is not None:
        rw, rb = router
        in_specs += [_const_spec(rw.shape), _const_spec(rb.shape)]
        args += [rw, rb]
        out_specs.append(tok(LANES))
        out_shape.append(jax.ShapeDtypeStruct((bsz, t, LANES), F32))
    return pl.pallas_call(
        functools.partial(_merge_kernel, moe=router is not None),
        grid=(bsz, t // tm),
        in_specs=in_specs, out_specs=out_specs, out_shape=out_shape,
        compiler_params=_cparams(("parallel", "parallel")),
        name="merge_out",
    )(*args)


def _final_norm(x, gain):
    ms = jnp.mean(x * x, axis=-1, keepdims=True)
    return x * lax.rsqrt(ms + NORM_EPS) * gain


def _ffn_kernel(x_ref, h_ref, g2_ref, wg_ref, wu_ref, wd_ref, *rest, n_chunks, final):
    if final:
        fg_ref, o_ref = rest
    else:
        (o_ref,) = rest
    h = h_ref[...]
    tf = wg_ref.shape[1] // n_chunks
    acc = jnp.zeros(x_ref.shape, F32)
    for c in range(n_chunks):
        gt = jnp.dot(h, wg_ref[:, c * tf:(c + 1) * tf], preferred_element_type=F32)
        up = jnp.dot(h, wu_ref[:, c * tf:(c + 1) * tf], preferred_element_type=F32)
        act = (gt * jax.nn.sigmoid(gt) * up).astype(BF16)
        acc = acc + jnp.dot(act, wd_ref[c * tf:(c + 1) * tf, :], preferred_element_type=F32)
    x = x_ref[...] + g2_ref[...] * acc
    o_ref[...] = _final_norm(x, fg_ref[...]) if final else x


def _ffn(x, h, g2, wg, wu, wd, final_g, tm):
    bsz, t, d = x.shape
    mrows = g2.shape[1]
    mblk = 1 if mrows == 1 else tm
    mod_spec = pl.BlockSpec((None, mblk, d), (lambda b, i: (b, 0, 0)) if mrows == 1 else (lambda b, i: (b, i, 0)))
    tok = pl.BlockSpec((None, tm, d), lambda b, i: (b, i, 0))
    in_specs = [tok, tok, mod_spec, _const_spec(wg.shape), _const_spec(wu.shape), _const_spec(wd.shape)]
    args = [x, h, g2, wg, wu, wd]
    if final_g is not None:
        in_specs.append(_const_spec((1, d)))
        args.append(final_g.reshape(1, d))
    return pl.pallas_call(
        functools.partial(_ffn_kernel, n_chunks=2, final=final_g is not None),
        grid=(bsz, t // tm),
        in_specs=in_specs, out_specs=tok,
        out_shape=jax.ShapeDtypeStruct((bsz, t, d), F32),
        compiler_params=_cparams(("parallel", "parallel")),
        name="ffn_dense",
    )(*args)


def _moe_kernel(x_ref, h_ref, g2_ref, gw_ref, wg_ref, wu_ref, wd_ref, *rest, final):
    if final:
        fg_ref, o_ref, acc_ref = rest
    else:
        o_ref, acc_ref = rest
    e = pl.program_id(2)

    @pl.when(e == 0)
    def _():
        acc_ref[...] = jnp.zeros_like(acc_ref)

    h = h_ref[...]
    gt = jnp.dot(h, wg_ref[...], preferred_element_type=F32)
    up = jnp.dot(h, wu_ref[...], preferred_element_type=F32)
    act = (gt * jax.nn.sigmoid(gt) * up).astype(BF16)
    y = jnp.dot(act, wd_ref[...], preferred_element_type=F32)
    gw = gw_ref[...]
    ge = jnp.sum(jnp.where(_lane_iota(gw.shape) == e, gw, 0.0), axis=1, keepdims=True)
    acc_ref[...] += ge * y

    @pl.when(e == pl.num_programs(2) - 1)
    def _():
        x = x_ref[...] + g2_ref[...] * acc_ref[...]
        o_ref[...] = _final_norm(x, fg_ref[...]) if final else x


def _moe(x, h, g2, gw, wg, wu, wd, final_g, tm):
    bsz, t, d = x.shape
    n_e, _, dff = wg.shape
    mrows = g2.shape[1]
    mblk = 1 if mrows == 1 else tm
    mod_spec = pl.BlockSpec((None, mblk, d), (lambda b, i, e: (b, 0, 0)) if mrows == 1 else (lambda b, i, e: (b, i, 0)))
    tok = pl.BlockSpec((None, tm, d), lambda b, i, e: (b, i, 0))
    in_specs = [tok, tok, mod_spec, pl.BlockSpec((None, tm, LANES), lambda b, i, e: (b, i, 0)),
                pl.BlockSpec((None, d, dff), lambda b, i, e: (e, 0, 0)),
                pl.BlockSpec((None, d, dff), lambda b, i, e: (e, 0, 0)),
                pl.BlockSpec((None, dff, d), lambda b, i, e: (e, 0, 0))]
    args = [x, h, g2, gw, wg, wu, wd]
    if final_g is not None:
        in_specs.append(pl.BlockSpec((1, d), lambda b, i, e: (0, 0)))
        args.append(final_g.reshape(1, d))
    return pl.pallas_call(
        functools.partial(_moe_kernel, final=final_g is not None),
        grid=(bsz, t // tm, n_e),
        in_specs=in_specs, out_specs=tok,
        out_shape=jax.ShapeDtypeStruct((bsz, t, d), F32),
        scratch_shapes=[pltpu.VMEM((tm, d), F32)],
        compiler_params=_cparams(("parallel", "parallel", "arbitrary")),
        name="moe_dense",
    )(*args)


def _rope_tables(pos):
    inv = ROPE_THETA ** (-jnp.arange(HALF, dtype=F32) / HALF)
    ang = pos.astype(F32)[:, None] * inv[None, :]
    cos, sin = jnp.cos(ang), jnp.sin(ang)
    return jnp.tile(cos, (1, 4)), jnp.tile(jnp.concatenate([-sin, sin], axis=1), (1, 2))


def _pick_tile(n, pref):
    t = min(n, pref)
    while n % t:
        t //= 2
    return t


def _per_seq_cols(a, bsz, t, width):
    f = a.shape[1]
    a = jnp.moveaxis(a[0].reshape(f, bsz, t), 1, 0)
    return jnp.pad(a, ((0, 0), (0, 0), (0, width - t)))


def _value_blocks(past_vt, new_vt, bsz, t, lpad):
    lead = new_vt.shape[:-2]
    new_b = jnp.moveaxis(new_vt.reshape(*lead, V_ROWS, bsz, t), -2, 0)
    full = jnp.concatenate([past_vt, new_b], axis=-1)
    full = jnp.pad(full, [(0, 0)] * (full.ndim - 1) + [(0, lpad - full.shape[-1])])
    full = full.reshape(bsz, *lead, V_ROWS, lpad // KV_BLOCK, KV_BLOCK)
    return jnp.moveaxis(full, -2, -3)


def _with_ones_rows(vt):
    ones = jnp.ones(vt.shape[:-2] + (1, vt.shape[-1]), vt.dtype)
    zeros = jnp.zeros(vt.shape[:-2] + (V_ROWS - HEAD_DIM - 1, vt.shape[-1]), vt.dtype)
    return jnp.concatenate([vt, ones, zeros], axis=-2)


def _mixers(inp, past, n_past, lw, bsz, t):
    aqt, iqt, kik, avt, bqt, bk, bvt, cu, sm, smt = inp
    n_keys = n_past + t
    n_sel = min(TOPK_MAX, n_keys // 4)
    lpad = -(-n_keys // (2 * KV_BLOCK)) * 2 * KV_BLOCK
    if past is None:
        kik_all, avt_all, bk_all, bvt_all, logf_all = kik, avt, bk, bvt, sm
        hist16 = jnp.zeros((bsz, 16, POOL_WIDTH), F32)
        tq_a, tq_b, t_pad = _pick_tile(t, 128), _pick_tile(t, 256), t
    else:
        pa, pb, plf, pc = past
        pk, pv, pik = (pa[:, :, j].astype(BF16) for j in range(3))

        def join_rows(p, new):
            full = jnp.concatenate([p, new.reshape(bsz, t, new.shape[-1])], axis=1)
            return jnp.pad(full, ((0, 0), (0, lpad - n_keys), (0, 0)))

        kik_all = join_rows(jnp.concatenate([pk, pik], axis=-1), kik)
        avt_all = _value_blocks(_with_ones_rows(jnp.swapaxes(pv, 1, 2)), avt[0, 0], bsz, t, lpad)
        bk_all = join_rows(pb[:, :, 0].astype(BF16).reshape(bsz, n_past, 384), bk)
        pvt = jnp.transpose(pb[:, :, 1].astype(BF16), (0, 2, 3, 1))
        bvt_all = _value_blocks(_with_ones_rows(pvt), bvt[0, :, 0], bsz, t, lpad)
        logf_all = join_rows(jnp.pad(plf, ((0, 0), (0, 0), (0, LANES - B_HEADS))), sm)
        hist16 = jnp.pad(pc, ((0, 0), (1, 0), (0, 0)))
        tq_a = tq_b = t_pad = LANES
        aqt, iqt, bqt, smt = (_per_seq_cols(a, bsz, t, t_pad) for a in (aqt, iqt, bqt, smt))

    oa = _dsa(aqt, iqt, smt, kik_all, avt_all, tq=tq_a, n_keys=n_keys, q_pos0=n_past, n_sel=n_sel)
    ck = _cum_logf(logf_all, KV_BLOCK)
    ob = _fox(bqt, bk_all, bvt_all, ck, tq=tq_b, n_keys=n_keys, q_pos0=n_past)
    cu = cu.reshape(bsz, t, POOL_WIDTH)
    oc = _pool(cu, hist16, lw["pool_bd"], lw["pool_scale"], tc=_pick_tile(t, 256), start_pos=n_past)
    return oa[:, :t], ob[:, :t], oc


def _layer(x, mod, past, n_past, pos_tab, lw, layer, final_g, per_token):
    bsz, t, d = x.shape
    sh1, sc1, g1, sh2, sc2, g2 = mod
    if per_token:
        xt = x.reshape(1, bsz * t, d)
        sh1, sc1, g1, sh2, sc2, g2 = (jnp.broadcast_to(m, (bsz, t, d)).reshape(1, bsz * t, d) for m in mod)
        cos, sin = (jnp.tile(a, (bsz, 1)) for a in pos_tab)
    else:
        xt = x
        cos, sin = pos_tab
    tm = KV_BLOCK
    (aqt, iqt, na, kik, avt, nb, bqt, bk, bvt, cu, sm, smt, gates) = _in_proj(
        xt, sc1, sh1, lw["norm_mix_g"], lw["w_in"], lw["bf_bias"], cos, sin, tm)
    oa, ob, oc = _mixers((aqt, iqt, kik, avt, bqt, bk, bvt, cu, sm, smt), past, n_past, lw, bsz, t)

    def flat(a):
        return a.reshape(xt.shape[0], xt.shape[1], a.shape[-1])

    router = (lw["router_w"], lw["router_b"]) if layer % 2 else None
    res = _merge(xt, flat(oa), flat(ob), flat(oc), gates, g1, sc2, sh2, lw["norm_ffn_g"],
                 lw["w_br_a"], lw["w_br_b"], lw["w_br_c"], lw["w_out"], router, tm)
    tmf = _pick_tile(xt.shape[1], 512)
    if layer % 2 == 0:
        x_mid, h2 = res
        x_new = _ffn(x_mid, h2, g2, lw["ffn_wg"], lw["ffn_wu"], lw["ffn_wd"], final_g, tmf)
    else:
        x_mid, h2, gw = res
        x_new = _moe(x_mid, h2, g2, gw, lw["moe_wg"], lw["moe_wu"], lw["moe_wd"], final_g, tmf)
    new_a = na.reshape(bsz, t, 3, HEAD_DIM)
    new_b = nb.reshape(bsz, t, 2, B_HEADS, HEAD_DIM)
    new_logf = sm.reshape(bsz, t, LANES)[:, :, :B_HEADS]
    new_pool = cu.reshape(bsz, t, POOL_WIDTH)[:, t - POOL_HIST:, :]
    return x_new.reshape(bsz, t, d), (new_a, new_b, new_logf, new_pool)


def kernel(x_prompt, x_sample, cache_a_kvi, cache_b_kv, cache_b_logf, state_c_pool, c_prompt, c_sample,
           ada_w, ada_b, norm_mix_g, w_in, b_forget, pool_w, pool_scale, w_br_a, w_br_b, w_br_c, w_out,
           norm_ffn_g, ffn_w_gate, ffn_w_up, ffn_w_down, moe_router_w, moe_router_b, moe_w_gate,
           moe_w_up, moe_w_down, final_norm_g):
    depth = ada_w.shape[0]
    bp, tp, d = x_prompt.shape
    bs, ts, _ = x_sample.shape
    n_past = cache_a_kvi.shape[2]
    assert tp % KV_BLOCK == 0 and (bs * ts) % KV_BLOCK == 0 and ts <= LANES

    rows = -(-(bp + bs) // 8) * 8
    c_all = jnp.pad(jnp.concatenate([c_prompt, c_sample], axis=0), ((0, rows - bp - bs), (0, 0)))
    mod_all = _ada(c_all, ada_w, ada_b)

    tab_p = _rope_tables(jnp.arange(tp))
    tab_s = _rope_tables(n_past + jnp.arange(ts))

    xp, xs = x_prompt, x_sample
    outs_p, outs_s = [], []
    for layer in range(depth):
        j = layer // 2
        w_l, bias_l = _in_weights(w_in[layer], b_forget[layer])
        pw = pool_w[layer]
        pool_bd = jnp.zeros((POOL_WIDTH, POOL_WIDTH), F32)
        for g in range(len(POOL_WINDOWS)):
            sl = slice(g * POOL_GROUP_DIM, (g + 1) * POOL_GROUP_DIM)
            pool_bd = pool_bd.at[sl, sl].set(pw[g])
        lw = dict(w_in=w_l, bf_bias=bias_l, norm_mix_g=norm_mix_g[layer], norm_ffn_g=norm_ffn_g[layer],
                  pool_bd=pool_bd.astype(BF16), pool_scale=pool_scale[layer],
                  w_br_a=w_br_a[layer].astype(BF16), w_br_b=w_br_b[layer].astype(BF16),
                  w_br_c=w_br_c[layer].astype(BF16), w_out=w_out[layer].astype(BF16))
        if layer % 2 == 0:
            lw.update(ffn_wg=ffn_w_gate[j].astype(BF16), ffn_wu=ffn_w_up[j].astype(BF16),
                      ffn_wd=ffn_w_down[j].astype(BF16))
        else:
            lw.update(router_w=jnp.pad(moe_router_w[j], ((0, 0), (0, LANES - N_EXPERTS))),
                      router_b=jnp.pad(moe_router_b[j], (0, LANES - N_EXPERTS)).reshape(1, LANES),
                      moe_wg=moe_w_gate[j].astype(BF16), moe_wu=moe_w_up[j].astype(BF16),
                      moe_wd=moe_w_down[j].astype(BF16))
        final_g = final_norm_g if layer == depth - 1 else None
        mod_p = [m[:, None, :] for m in jnp.split(mod_all[layer, :bp], 6, axis=-1)]
        mod_s = [m[:, None, :] for m in jnp.split(mod_all[layer, bp:bp + bs], 6, axis=-1)]
        xp, new_p = _layer(xp, mod_p, None, 0, tab_p, lw, layer, final_g, per_token=False)
        past = (cache_a_kvi[layer], cache_b_kv[layer], cache_b_logf[layer], state_c_pool[layer])
        xs, new_s = _layer(xs, mod_s, past, n_past, tab_s, lw, layer, final_g, per_token=True)
        outs_p.append(new_p)
        outs_s.append(new_s)

    def stack(outs, k):
        return jnp.stack([o[k] for o in outs])

    return (xp, xs,
            stack(outs_p, 0), stack(outs_p, 1), stack(outs_p, 2), stack(outs_p, 3),
            stack(outs_s, 0), stack(outs_s, 1), stack(outs_s, 2), stack(outs_s, 3))
```

```python
import functools

import jax
import jax.numpy as jnp
import numpy as np
from jax import lax
from jax.experimental import pallas as pl
from jax.experimental.pallas import tpu as pltpu

F32 = jnp.float32
BF16 = jnp.bfloat16
I32 = jnp.int32

D_MODEL = 1024
CHUNK = 64
HEAD_DIM = 64
HALF = HEAD_DIM // 2
ROPE_THETA = 10000.0
NORM_EPS = 1e-6
A_HEADS = 6
IDX_HEADS = 4
TOPK_MAX = 256
B_HEADS = 6
POOL_WINDOWS = (2, 4, 8, 16)
POOL_GROUP_DIM = 64
POOL_WIDTH = 256
POOL_HIST = 15
N_EXPERTS = 8
LANES = 128
SUBLANES = 8
LOG2E = 1.4426950408889634
QK_SCALE = HEAD_DIM ** -0.5 * LOG2E
KV_BLOCK = 256
V_ROWS = HEAD_DIM + 16
VMEM_LIMIT = 56 * 1024 * 1024
NEG_INF = float("-inf")
KEY_DT = jnp.bfloat16

C_AQ, C_IQ, C_A, C_B, C_CU, C_SM, C_GATE, C_END = 0, 384, 640, 896, 2048, 2304, 2432, 5504


def _cparams(sem):
    return pltpu.CompilerParams(dimension_semantics=sem, vmem_limit_bytes=VMEM_LIMIT)


def _const_spec(shape):
    nd = len(shape)
    return pl.BlockSpec(shape, lambda *_: (0,) * nd, pipeline_mode=pl.Buffered(1))


def _lane_iota(shape):
    return lax.broadcasted_iota(I32, shape, len(shape) - 1)


def _row_iota(shape):
    return lax.broadcasted_iota(I32, shape, len(shape) - 2)


def _ada_kernel(c_ref, w_ref, b_ref, o_ref):
    c = c_ref[...]
    s = c * jax.nn.sigmoid(c)
    o_ref[...] = jnp.dot(s, w_ref[...], preferred_element_type=F32,
                         precision=lax.Precision.HIGHEST) + b_ref[...]


def _ada(c_all, ada_w, ada_b):
    depth, d, n = ada_w.shape
    rows = c_all.shape[0]
    tn = 1536
    return pl.pallas_call(
        _ada_kernel,
        grid=(depth, n // tn),
        in_specs=[pl.BlockSpec((rows, d), lambda l, j: (0, 0)),
                  pl.BlockSpec((None, d, tn), lambda l, j: (l, 0, j)),
                  pl.BlockSpec((None, 1, tn), lambda l, j: (l, 0, j))],
        out_specs=pl.BlockSpec((None, rows, tn), lambda l, j: (l, 0, j)),
        out_shape=jax.ShapeDtypeStruct((depth, rows, n), F32),
        compiler_params=_cparams(("arbitrary", "arbitrary")),
        name="ada_mod",
    )(c_all, ada_w, ada_b.reshape(depth, 1, n))


def _in_kernel(x_ref, sc_ref, sh_ref, g_ref, w_ref, bf_ref, cos_ref, sin_ref,
               aqt_ref, iqt_ref, na_ref, kik_ref, avt_ref, nb_ref, bqt_ref, bk_ref, bvt_ref,
               cu_ref, sm_ref, smt_ref, gate_ref):
    x = x_ref[...]
    ms = jnp.mean(x * x, axis=-1, keepdims=True)
    y = x * lax.rsqrt(ms + NORM_EPS) * g_ref[...]
    h = (y * (1.0 + sc_ref[...]) + sh_ref[...]).astype(BF16)
    tm = x.shape[0]

    def mm(a, b):
        return jnp.dot(h, w_ref[:, a:b], preferred_element_type=F32)

    cos = cos_ref[...]
    sin = sin_ref[...]
    lane = _lane_iota((tm, LANES))
    low = lane < HEAD_DIM
    first_half = (lane & HALF) == 0

    def rope(z):
        swapped = jnp.where(first_half, pltpu.roll(z, LANES - HALF, 1), pltpu.roll(z, HALF, 1))
        return z * cos + swapped * sin

    zeros64 = jnp.zeros((HEAD_DIM, tm), BF16)
    ones_rows = jnp.where(_row_iota((V_ROWS - HEAD_DIM, tm)) == 0, 1.0, 0.0).astype(BF16)

    def put_heads(ref, zt, p, slot_even, slot_odd):
        for hh, slot in ((0, slot_even), (1, slot_odd)):
            base = (2 * p + hh) * LANES
            ref[base + slot * HEAD_DIM:base + (slot + 1) * HEAD_DIM, :] = zt[hh * HEAD_DIM:(hh + 1) * HEAD_DIM]
            ref[base + (1 - slot) * HEAD_DIM:base + (2 - slot) * HEAD_DIM, :] = zeros64

    z = mm(C_AQ, C_IQ)
    for p in range(3):
        zt = (rope(z[:, p * LANES:(p + 1) * LANES]) * QK_SCALE).T.astype(BF16)
        put_heads(aqt_ref, zt, p, 0, 0)
    z = mm(C_IQ, C_A)
    for p in range(2):
        zt = rope(z[:, p * LANES:(p + 1) * LANES]).T.astype(BF16)
        put_heads(iqt_ref, zt, p, 1, 1)

    z = mm(C_A, C_B)
    kv = z[:, :LANES]
    r0 = jnp.where(low, rope(kv), kv)
    r1 = rope(z[:, LANES:])
    na_ref[...] = jnp.concatenate([r0, r1], axis=1)[:, :3 * HEAD_DIM]
    kik_ref[...] = jnp.where(low, r0, pltpu.roll(r1, HEAD_DIM, 1)).astype(BF16)
    avt_ref[0:HEAD_DIM, :] = r0.T[HEAD_DIM:, :].astype(BF16)
    avt_ref[HEAD_DIM:, :] = ones_rows

    z = mm(C_B, C_CU)
    nb_ref[...] = z[:, 384:]
    bk_ref[...] = z[:, 384:768].astype(BF16)
    for p in range(3):
        zt = (z[:, p * LANES:(p + 1) * LANES] * QK_SCALE).T.astype(BF16)
        put_heads(bqt_ref, zt, p, 0, 1)
        vt = z[:, 768 + p * LANES:768 + (p + 1) * LANES].T.astype(BF16)
        for hh in range(2):
            bvt_ref[2 * p + hh, 0:HEAD_DIM, :] = vt[hh * HEAD_DIM:(hh + 1) * HEAD_DIM]
            bvt_ref[2 * p + hh, HEAD_DIM:, :] = ones_rows

    cu_ref[...] = mm(C_CU, C_SM)

    z = mm(C_SM, C_GATE)
    t = z + bf_ref[...]
    logf = jnp.minimum(t, 0.0) - jnp.log1p(jnp.exp(-jnp.abs(t)))
    sm = jnp.where(lane < B_HEADS, logf, z)
    sm_ref[...] = sm
    smt_ref[...] = sm.T[0:16, :]

    for c in range(3):
        gate_ref[:, c * D_MODEL:(c + 1) * D_MODEL] = jax.nn.sigmoid(
            mm(C_GATE + c * D_MODEL, C_GATE + (c + 1) * D_MODEL))


def _in_weights(w_in_l, b_forget_l):
    d = w_in_l.shape[0]
    sizes = (384, 64, 64, 256, 4, 64, 384, 384, 384, 6, 256, 3072)
    o = np.concatenate([[0], np.cumsum(sizes)])
    cols = [w_in_l[:, o[0]:o[1]],
            w_in_l[:, o[3]:o[4]],
            w_in_l[:, o[1]:o[3]], w_in_l[:, o[5]:o[6]], jnp.zeros((d, 64), F32),
            w_in_l[:, o[6]:o[9]],
            w_in_l[:, o[10]:o[11]],
            w_in_l[:, o[9]:o[10]], jnp.zeros((d, 2), F32), w_in_l[:, o[4]:o[5]], jnp.zeros((d, LANES - 12), F32),
            w_in_l[:, o[11]:o[12]]]
    w = jnp.concatenate(cols, axis=1).astype(BF16)
    assert w.shape[1] == C_END, w.shape
    bias = jnp.concatenate([b_forget_l, jnp.zeros((LANES - B_HEADS,), F32)]).reshape(1, LANES)
    return w, bias


def _in_proj(x, sc, sh, g, w, bias, cos, sin, tm):
    bsz, t, d = x.shape
    mrows = sc.shape[1]
    mblk = 1 if mrows == 1 else tm
    mod_spec = pl.BlockSpec((None, mblk, d), (lambda b, i: (b, 0, 0)) if mrows == 1 else (lambda b, i: (b, i, 0)))
    nblk = t // tm

    def rows(n, dt):
        return pl.BlockSpec((None, tm, n), lambda b, i: (b, i, 0)), jax.ShapeDtypeStruct((bsz, t, n), dt)

    def cols(n, dt):
        return pl.BlockSpec((None, n, tm), lambda b, i: (b, 0, i)), jax.ShapeDtypeStruct((bsz, n, t), dt)

    outs = [cols(A_HEADS * LANES, BF16), cols(IDX_HEADS * LANES, BF16), rows(192, F32), rows(LANES, BF16),
            (pl.BlockSpec((None, None, V_ROWS, tm), lambda b, i: (b, i, 0, 0)),
             jax.ShapeDtypeStruct((bsz, nblk, V_ROWS, tm), BF16)),
            rows(768, F32), cols(B_HEADS * LANES, BF16), rows(384, BF16),
            (pl.BlockSpec((None, B_HEADS, None, V_ROWS, tm), lambda b, i: (b, 0, i, 0, 0)),
             jax.ShapeDtypeStruct((bsz, B_HEADS, nblk, V_ROWS, tm), BF16)),
            rows(256, F32), rows(LANES, F32), cols(16, F32), rows(3 * D_MODEL, F32)]
    return pl.pallas_call(
        _in_kernel,
        grid=(bsz, nblk),
        in_specs=[pl.BlockSpec((None, tm, d), lambda b, i: (b, i, 0)), mod_spec, mod_spec,
                  _const_spec((1, d)), _const_spec(w.shape), _const_spec((1, LANES)),
                  pl.BlockSpec((tm, LANES), lambda b, i: (i, 0)),
                  pl.BlockSpec((tm, LANES), lambda b, i: (i, 0))],
        out_specs=[o[0] for o in outs],
        out_shape=[o[1] for o in outs],
        compiler_params=_cparams(("parallel", "parallel")),
        name="in_proj",
    )(x, sc, sh, g.reshape(1, d), w, bias, cos, sin)


def _cum_kernel(x_ref, o_ref, carry_ref):
    @pl.when(pl.program_id(1) == 0)
    def _():
        carry_ref[...] = jnp.zeros_like(carry_ref)

    tc = x_ref.shape[0]
    tri = jnp.where(_lane_iota((tc, tc)) <= _row_iota((tc, tc)), 1.0, 0.0)
    cum = jnp.dot(tri, x_ref[...], preferred_element_type=F32, precision=lax.Precision.HIGHEST) + carry_ref[0:1, :]
    for hd in range(B_HEADS):
        o_ref[hd] = jnp.broadcast_to(cum[:, hd:hd + 1] * LOG2E, (tc, LANES))
    carry_ref[...] = jnp.broadcast_to(cum[tc - 1:tc, :], carry_ref.shape)


def _cum_logf(x, tc):
    bsz, t, n = x.shape
    return pl.pallas_call(
        _cum_kernel,
        grid=(bsz, t // tc),
        in_specs=[pl.BlockSpec((None, tc, n), lambda b, i: (b, i, 0))],
        out_specs=pl.BlockSpec((None, B_HEADS, tc, LANES), lambda b, i: (b, 0, i, 0)),
        out_shape=jax.ShapeDtypeStruct((bsz, B_HEADS, t, LANES), F32),
        scratch_shapes=[pltpu.VMEM((SUBLANES, LANES), F32)],
        compiler_params=_cparams(("parallel", "arbitrary")),
        name="logf_cumsum",
    )(x)


def _sum_keys(x):
    part = x.reshape(x.shape[0] // SUBLANES, SUBLANES, x.shape[1]).sum(axis=0)
    return jnp.sum(part, axis=0, keepdims=True)


def _max_keys(x):
    part = x.reshape(x.shape[0] // SUBLANES, SUBLANES, x.shape[1]).max(axis=0)
    return jnp.max(part, axis=0, keepdims=True)


def _dsa_kernel(aqt_ref, iqt_ref, smt_ref, kik_ref, avt_ref, o_ref, key_ref,
                *, tq, tk, n_keys, q_pos0, n_sel):
    i = pl.program_id(1)
    pos_first = q_pos0 + i * tq
    last_chunk = (pos_first + tq - 1) // CHUNK
    n_adm = jnp.minimum((last_chunk + 1) * CHUNK, n_keys)
    nkb = (n_adm + tk - 1) // tk
    n_pairs = (nkb + 1) // 2

    q_pos = pos_first + _lane_iota((1, tq))
    q_lim = jnp.minimum((q_pos // CHUNK + 1) * CHUNK, n_keys)
    key_row = _row_iota((tk, tq))

    def keys(kb):
        return kik_ref[pl.ds(pl.multiple_of(kb * tk, tk), tk), :]

    iq4 = jnp.concatenate([iqt_ref[hd * LANES:(hd + 1) * LANES, :] for hd in range(IDX_HEADS)], axis=1)
    smt = smt_ref[...]
    w_rows = [smt[8 + hd:9 + hd, :] for hd in range(IDX_HEADS)]

    def score_body(kb, carry):
        s4 = jnp.dot(keys(kb), iq4, preferred_element_type=F32)
        score = w_rows[0] * jnp.maximum(s4[:, 0:tq], 0.0)
        for hd in range(1, IDX_HEADS):
            score = score + w_rows[hd] * jnp.maximum(s4[:, hd * tq:(hd + 1) * tq], 0.0)
        score = jnp.where(key_row < q_lim - kb * tk, score, NEG_INF)
        key_ref[kb] = score.astype(KEY_DT)
        return carry

    lax.fori_loop(0, nkb, score_body, 0)

    @pl.when(nkb % 2 == 1)
    def _():
        key_ref[nkb] = jnp.full((tk, tq), NEG_INF, KEY_DT)

    one, zero = jnp.ones((), KEY_DT), jnp.zeros((), KEY_DT)
    packed_rows = 2 * SUBLANES

    def count(cand, strict):
        def hits(blk):
            h = jnp.where((blk > cand) if strict else (blk >= cand), one, zero)
            parts = [h[r * packed_rows:(r + 1) * packed_rows] for r in range(tk // packed_rows)]
            while len(parts) > 1:
                parts = [a + b for a, b in zip(parts[::2], parts[1::2])]
            return parts[0].astype(F32)

        def body(j, acc):
            return acc + hits(key_ref[2 * j]) + hits(key_ref[2 * j + 1])

        acc = lax.fori_loop(0, n_pairs, body, jnp.zeros((packed_rows, tq), F32))
        return jnp.sum(acc, axis=0, keepdims=True)

    def pattern_value(u):
        bits = jnp.where(u >= 0x8000, u & 0x7FFF, (~u) & 0xFFFF)
        return lax.bitcast_convert_type(lax.shift_left(bits, 16), F32).astype(KEY_DT)

    def bit_body(b, u):
        cand_u = u | lax.shift_left(jnp.int32(1), 15 - b)
        cnt = count(pattern_value(cand_u), False)
        return jnp.where(cnt >= n_sel, cand_u, u)

    u_thr = lax.fori_loop(0, 16, bit_body, jnp.zeros((1, tq), I32))
    thr_k = pattern_value(jnp.maximum(u_thr, 0x007F))
    need = n_sel - count(thr_k, True)
    thr = thr_k.astype(F32)

    aq6 = jnp.concatenate([aqt_ref[hd * LANES:(hd + 1) * LANES, :] for hd in range(A_HEADS)], axis=1)

    def attend():
        half = tk // 2
        lower = jnp.where(_lane_iota((half, half)) <= _row_iota((half, half)), 1.0, 0.0).astype(BF16)

        def body(kb, carry):
            eq_seen, ms, accs = carry
            blk = key_ref[kb].astype(F32)
            eq = blk == thr
            eq_f = jnp.where(eq, 1.0, 0.0)
            prefs = []
            for e in (eq_f[:half], eq_f[half:]):
                prefs.append(jnp.dot(lower, e.astype(BF16), preferred_element_type=F32) + eq_seen)
                eq_seen = eq_seen + _sum_keys(e)
            pref = jnp.concatenate(prefs, axis=0)
            slack = jnp.where(blk >= thr, need - jnp.where(eq, pref, 0.0), -1.0)
            bias = jnp.where(slack >= 0.0, jnp.where(jnp.abs(blk) < jnp.inf, 0.0, NEG_INF), NEG_INF)
            logits = jnp.dot(keys(kb), aq6, preferred_element_type=F32)
            vts = (avt_ref[2 * kb], avt_ref[2 * kb + 1])
            new_ms, new_accs = [], []
            for p in range(A_HEADS // 2):
                ps, alphas = [], []
                for hd in (2 * p, 2 * p + 1):
                    lg = logits[:, hd * tq:(hd + 1) * tq] + bias
                    m_old = ms[hd]
                    m_new = jnp.maximum(m_old, _max_keys(lg))
                    m_safe = jnp.where(m_new == NEG_INF, 0.0, m_new)
                    ps.append(jnp.exp2(lg - m_safe).astype(BF16))
                    alphas.append(jnp.exp2(m_old - m_safe))
                    new_ms.append(m_new)
                p2 = jnp.concatenate(ps, axis=1)
                pv = (jnp.dot(vts[0], p2[:tk // 2], preferred_element_type=F32)
                      + jnp.dot(vts[1], p2[tk // 2:], preferred_element_type=F32))
                new_accs.append(jnp.concatenate(alphas, axis=1) * accs[p] + pv)
            return eq_seen, tuple(new_ms), tuple(new_accs)

        init = (jnp.zeros((1, tq), F32),
                tuple(jnp.full((1, tq), NEG_INF, F32) for _ in range(A_HEADS)),
                tuple(jnp.zeros((V_ROWS, 2 * tq), F32) for _ in range(A_HEADS // 2)))
        return lax.fori_loop(0, nkb, body, init)[2]

    accs = attend()

    outs = []
    for p in range(A_HEADS // 2):
        o2 = accs[p][0:HEAD_DIM] / accs[p][HEAD_DIM:HEAD_DIM + 1]
        outs += [o2[:, 0:tq], o2[:, tq:2 * tq]]
    o_ref[...] = jnp.concatenate(outs, axis=0).T.astype(o_ref.dtype)


def _dsa(aqt, iqt, smt, kik, avt, *, tq, n_keys, q_pos0, n_sel):
    bsz, _, t_q = aqt.shape
    _, nblk, _, tkv = avt.shape
    assert nblk % 2 == 0
    tk = 2 * tkv
    kern = functools.partial(_dsa_kernel, tq=tq, tk=tk, n_keys=n_keys, q_pos0=q_pos0, n_sel=n_sel)
    return pl.pallas_call(
        kern,
        grid=(bsz, t_q // tq),
        in_specs=[pl.BlockSpec((None, A_HEADS * LANES, tq), lambda b, i: (b, 0, i)),
                  pl.BlockSpec((None, IDX_HEADS * LANES, tq), lambda b, i: (b, 0, i)),
                  pl.BlockSpec((None, 16, tq), lambda b, i: (b, 0, i)),
                  pl.BlockSpec((None, nblk * tkv, LANES), lambda b, i: (b, 0, 0)),
                  pl.BlockSpec((None, nblk, V_ROWS, tkv), lambda b, i: (b, 0, 0, 0))],
        out_specs=pl.BlockSpec((None, tq, 384), lambda b, i: (b, i, 0)),
        out_shape=jax.ShapeDtypeStruct((bsz, t_q, 384), BF16),
        scratch_shapes=[pltpu.VMEM((2 * ((nblk // 2 + 1) // 2), tk, tq), KEY_DT)],
        compiler_params=_cparams(("parallel", "arbitrary")),
        name="dsa_attention",
    )(aqt, iqt, smt, kik, avt)


def _fox_kernel(qt_ref, k_ref, vt_ref, ck_ref, o_ref, *, tq, tk, n_keys, q_pos0):
    i = pl.program_id(2)
    pos_first = q_pos0 + i * tq
    n_full = pos_first // tk
    nkb = (jnp.minimum(pos_first + tq, n_keys) + tk - 1) // tk
    q_pos = pos_first + _lane_iota((1, tq))
    key_row = _row_iota((tk, tq))
    qts = (qt_ref[0:LANES, :], qt_ref[LANES:2 * LANES, :])

    def step(kbs, state, masked):
        logits = []
        for hh in range(2):
            for kb in kbs:
                rows = pl.ds(pl.multiple_of(kb * tk, tk), tk)
                ck = ck_ref[hh, rows, :]
                lg = (jnp.dot(k_ref[rows, :], qts[hh], preferred_element_type=F32)
                      - jnp.concatenate([ck] * (tq // LANES), axis=1))
                if masked:
                    lg = jnp.where(key_row <= q_pos - kb * tk, lg, NEG_INF)
                logits.append(lg)
        new = []
        for hh in range(2):
            m_old, acc = state[hh]
            lgs = logits[hh * len(kbs):(hh + 1) * len(kbs)]
            m_new = m_old
            for lg in lgs:
                m_new = jnp.maximum(m_new, _max_keys(lg))
            m_safe = jnp.where(m_new == NEG_INF, 0.0, m_new) if masked else m_new
            acc = jnp.exp2(m_old - m_safe) * acc
            for kb, lg in zip(kbs, lgs):
                p = jnp.exp2(lg - m_safe).astype(BF16)
                acc = acc + jnp.dot(vt_ref[hh, kb], p, preferred_element_type=F32)
            new.append((m_new, acc))
        return tuple(new)

    init = tuple((jnp.full((1, tq), NEG_INF, F32), jnp.zeros((V_ROWS, tq), F32)) for _ in range(2))
    state = lax.fori_loop(0, n_full // 2, lambda j, st: step((2 * j, 2 * j + 1), st, False), init)
    state = lax.fori_loop(2 * (n_full // 2), nkb, lambda kb, st: step((kb,), st, True), state)
    outs = [acc[0:HEAD_DIM] / acc[HEAD_DIM:HEAD_DIM + 1] for _, acc in state]
    o_ref[...] = jnp.concatenate(outs, axis=0).T.astype(o_ref.dtype)


def _fox(qt, k, vt, ck, *, tq, n_keys, q_pos0):
    bsz, _, t_q = qt.shape
    _, _, nblk, _, tk = vt.shape
    lpad = nblk * tk
    kern = functools.partial(_fox_kernel, tq=tq, tk=tk, n_keys=n_keys, q_pos0=q_pos0)
    return pl.pallas_call(
        kern,
        grid=(bsz, 3, t_q // tq),
        in_specs=[pl.BlockSpec((None, 2 * LANES, tq), lambda b, p, i: (b, p, i)),
                  pl.BlockSpec((None, lpad, LANES), lambda b, p, i: (b, 0, p)),
                  pl.BlockSpec((None, 2, nblk, V_ROWS, tk), lambda b, p, i: (b, p, 0, 0, 0)),
                  pl.BlockSpec((None, 2, lpad, LANES), lambda b, p, i: (b, p, 0, 0))],
        out_specs=pl.BlockSpec((None, tq, LANES), lambda b, p, i: (b, i, p)),
        out_shape=jax.ShapeDtypeStruct((bsz, t_q, 384), BF16),
        compiler_params=_cparams(("parallel", "parallel", "arbitrary")),
        name="fox_attention",
    )(qt, k, vt, ck)


def _pool_kernel(cur_ref, prev_ref, hist_ref, w_ref, s_ref, o_ref, ext, *, tc, start_pos):
    i = pl.program_id(1)
    cur = cur_ref[...]
    ext[0:16, :] = jnp.where(i == 0, hist_ref[...], prev_ref[tc - 16:, :])
    ext[16:, :] = cur
    pos = start_pos + i * tc + lax.broadcasted_iota(I32, (tc, POOL_WIDTH), 0)
    lane = _lane_iota((tc, POOL_WIDTH))
    run = cur
    pooled = jnp.zeros_like(cur)
    k = 1
    for g, w in enumerate(POOL_WINDOWS):
        while k < w:
            run = run + ext[16 - k:16 - k + tc, :]
            k += 1
        cnt = jnp.minimum(pos + 1, w).astype(F32)
        in_group = (lane >= g * POOL_GROUP_DIM) & (lane < (g + 1) * POOL_GROUP_DIM)
        pooled = jnp.where(in_group, run / cnt, pooled)
    z = (pooled - cur).astype(BF16)
    o_ref[...] = (jnp.dot(z, w_ref[...], preferred_element_type=F32) * s_ref[...]).astype(o_ref.dtype)


def _pool(cu, hist16, w_bd, scale, *, tc, start_pos):
    bsz, t, n = cu.shape
    kern = functools.partial(_pool_kernel, tc=tc, start_pos=start_pos)
    return pl.pallas_call(
        kern,
        grid=(bsz, t // tc),
        in_specs=[pl.BlockSpec((None, tc, n), lambda b, i: (b, i, 0)),
                  pl.BlockSpec((None, tc, n), lambda b, i: (b, jnp.maximum(i - 1, 0), 0)),
                  pl.BlockSpec((None, 16, n), lambda b, i: (b, 0, 0)),
                  _const_spec((n, n)), _const_spec((1, n))],
        out_specs=pl.BlockSpec((None, tc, n), lambda b, i: (b, i, 0)),
        out_shape=jax.ShapeDtypeStruct((bsz, t, n), BF16),
        scratch_shapes=[pltpu.VMEM((16 + tc, n), F32)],
        compiler_params=_cparams(("parallel", "arbitrary")),
        name="pool_mixer",
    )(cu, cu, hist16, w_bd, scale.reshape(1, n))


def _route(logits):
    lane = _lane_iota(logits.shape).astype(F32)
    lg = jnp.where(lane < N_EXPERTS, logits, NEG_INF)
    m1 = jnp.max(lg, axis=1, keepdims=True)
    i1 = jnp.min(jnp.where(lg == m1, lane, float(LANES)), axis=1, keepdims=True)
    hot1 = lane == i1
    lg2 = jnp.where(hot1, NEG_INF, lg)
    m2 = jnp.max(lg2, axis=1, keepdims=True)
    i2 = jnp.min(jnp.where(lg2 == m2, lane, float(LANES)), axis=1, keepdims=True)
    hot2 = lane == i2
    e2 = jnp.exp(m2 - m1)
    den = 1.0 + e2
    return jnp.where(hot1, 1.0 / den, 0.0) + jnp.where(hot2, e2 / den, 0.0)


def _merge_kernel(x_ref, oa_ref, ob_ref, oc_ref, gate_ref, g1_ref, sc2_ref, sh2_ref, g_ref,
                  wa_ref, wb_ref, wc_ref, wo_ref, *rest, moe):
    if moe:
        rw_ref, rb_ref, xo_ref, h_ref, gw_ref = rest
    else:
        xo_ref, h_ref = rest
    d = D_MODEL
    merged = (gate_ref[:, 0:d] * jnp.dot(oa_ref[...], wa_ref[...], preferred_element_type=F32)
              + gate_ref[:, d:2 * d] * jnp.dot(ob_ref[...], wb_ref[...], preferred_element_type=F32)
              + gate_ref[:, 2 * d:3 * d] * jnp.dot(oc_ref[...], wc_ref[...], preferred_element_type=F32))
    x = x_ref[...] + g1_ref[...] * jnp.dot(merged.astype(BF16), wo_ref[...], preferred_element_type=F32)
    xo_ref[...] = x
    ms = jnp.mean(x * x, axis=-1, keepdims=True)
    y = x * lax.rsqrt(ms + NORM_EPS) * g_ref[...]
    h = y * (1.0 + sc2_ref[...]) + sh2_ref[...]
    h_ref[...] = h.astype(BF16)
    if moe:
        logits = jnp.dot(h, rw_ref[...], preferred_element_type=F32,
                         precision=lax.Precision.HIGHEST) + rb_ref[...]
        gw_ref[...] = _route(logits)


def _merge(x, oa, ob, oc, gates, g1, sc2, sh2, g, wa, wb, wc, wo, router, tm):
    bsz, t, d = x.shape
    mrows = g1.shape[1]
    mblk = 1 if mrows == 1 else tm
    mod_spec = pl.BlockSpec((None, mblk, d), (lambda b, i: (b, 0, 0)) if mrows == 1 else (lambda b, i: (b, i, 0)))

    def tok(n):
        return pl.BlockSpec((None, tm, n), lambda b, i: (b, i, 0))

    in_specs = [tok(d), tok(384), tok(384), tok(256), tok(3 * d), mod_spec, mod_spec, mod_spec,
                _const_spec((1, d)), _const_spec(wa.shape), _const_spec(wb.shape), _const_spec(wc.shape),
                _const_spec(wo.shape)]
    args = [x, oa, ob, oc, gates, g1, sc2, sh2, g.reshape(1, d), wa, wb, wc, wo]
    out_specs = [tok(d), tok(d)]
    out_shape = [jax.ShapeDtypeStruct((bsz, t, d), F32), jax.ShapeDtypeStruct((bsz, t, d), BF16)]
    if router is not None:
        rw, rb = router
        in_specs += [_const_spec(rw.shape), _const_spec(rb.shape)]
        args += [rw, rb]
        out_specs.append(tok(LANES))
        out_shape.append(jax.ShapeDtypeStruct((bsz, t, LANES), F32))
    return pl.pallas_call(
        functools.partial(_merge_kernel, moe=router is not None),
        grid=(bsz, t // tm),
        in_specs=in_specs, out_specs=out_specs, out_shape=out_shape,
        compiler_params=_cparams(("parallel", "parallel")),
        name="merge_out",
    )(*args)


def _final_norm(x, gain):
    ms = jnp.mean(x * x, axis=-1, keepdims=True)
    return x * lax.rsqrt(ms + NORM_EPS) * gain


def _ffn_kernel(x_ref, h_ref, g2_ref, wg_ref, wu_ref, wd_ref, *rest, n_chunks, final):
    if final:
        fg_ref, o_ref = rest
    else:
        (o_ref,) = rest
    h = h_ref[...]
    tf = wg_ref.shape[1] // n_chunks
    acc = jnp.zeros(x_ref.shape, F32)
    for c in range(n_chunks):
        gt = jnp.dot(h, wg_ref[:, c * tf:(c + 1) * tf], preferred_element_type=F32)
        up = jnp.dot(h, wu_ref[:, c * tf:(c + 1) * tf], preferred_element_type=F32)
        act = (gt * jax.nn.sigmoid(gt) * up).astype(BF16)
        acc = acc + jnp.dot(act, wd_ref[c * tf:(c + 1) * tf, :], preferred_element_type=F32)
    x = x_ref[...] + g2_ref[...] * acc
    o_ref[...] = _final_norm(x, fg_ref[...]) if final else x


def _ffn(x, h, g2, wg, wu, wd, final_g, tm):
    bsz, t, d = x.shape
    mrows = g2.shape[1]
    mblk = 1 if mrows == 1 else tm
    mod_spec = pl.BlockSpec((None, mblk, d), (lambda b, i: (b, 0, 0)) if mrows == 1 else (lambda b, i: (b, i, 0)))
    tok = pl.BlockSpec((None, tm, d), lambda b, i: (b, i, 0))
    in_specs = [tok, tok, mod_spec, _const_spec(wg.shape), _const_spec(wu.shape), _const_spec(wd.shape)]
    args = [x, h, g2, wg, wu, wd]
    if final_g is not None:
        in_specs.append(_const_spec((1, d)))
        args.append(final_g.reshape(1, d))
    return pl.pallas_call(
        functools.partial(_ffn_kernel, n_chunks=2, final=final_g is not None),
        grid=(bsz, t // tm),
        in_specs=in_specs, out_specs=tok,
        out_shape=jax.ShapeDtypeStruct((bsz, t, d), F32),
        compiler_params=_cparams(("parallel", "parallel")),
        name="ffn_dense",
    )(*args)


def _moe_kernel(x_ref, h_ref, g2_ref, gw_ref, wg_ref, wu_ref, wd_ref, *rest, final):
    if final:
        fg_ref, o_ref, acc_ref = rest
    else:
        o_ref, acc_ref = rest
    e = pl.program_id(2)

    @pl.when(e == 0)
    def _():
        acc_ref[...] = jnp.zeros_like(acc_ref)

    h = h_ref[...]
    gt = jnp.dot(h, wg_ref[...], preferred_element_type=F32)
    up = jnp.dot(h, wu_ref[...], preferred_element_type=F32)
    act = (gt * jax.nn.sigmoid(gt) * up).astype(BF16)
    y = jnp.dot(act, wd_ref[...], preferred_element_type=F32)
    gw = gw_ref[...]
    ge = jnp.sum(jnp.where(_lane_iota(gw.shape) == e, gw, 0.0), axis=1, keepdims=True)
    acc_ref[...] += ge * y

    @pl.when(e == pl.num_programs(2) - 1)
    def _():
        x = x_ref[...] + g2_ref[...] * acc_ref[...]
        o_ref[...] = _final_norm(x, fg_ref[...]) if final else x


def _moe(x, h, g2, gw, wg, wu, wd, final_g, tm):
    bsz, t, d = x.shape
    n_e, _, dff = wg.shape
    mrows = g2.shape[1]
    mblk = 1 if mrows == 1 else tm
    mod_spec = pl.BlockSpec((None, mblk, d), (lambda b, i, e: (b, 0, 0)) if mrows == 1 else (lambda b, i, e: (b, i, 0)))
    tok = pl.BlockSpec((None, tm, d), lambda b, i, e: (b, i, 0))
    in_specs = [tok, tok, mod_spec, pl.BlockSpec((None, tm, LANES), lambda b, i, e: (b, i, 0)),
                pl.BlockSpec((None, d, dff), lambda b, i, e: (e, 0, 0)),
                pl.BlockSpec((None, d, dff), lambda b, i, e: (e, 0, 0)),
                pl.BlockSpec((None, dff, d), lambda b, i, e: (e, 0, 0))]
    args = [x, h, g2, gw, wg, wu, wd]
    if final_g is not None:
        in_specs.append(pl.BlockSpec((1, d), lambda b, i, e: (0, 0)))
        args.append(final_g.reshape(1, d))
    return pl.pallas_call(
        functools.partial(_moe_kernel, final=final_g is not None),
        grid=(bsz, t // tm, n_e),
        in_specs=in_specs, out_specs=tok,
        out_shape=jax.ShapeDtypeStruct((bsz, t, d), F32),
        scratch_shapes=[pltpu.VMEM((tm, d), F32)],
        compiler_params=_cparams(("parallel", "parallel", "arbitrary")),
        name="moe_dense",
    )(*args)


def _rope_tables(pos):
    inv = ROPE_THETA ** (-jnp.arange(HALF, dtype=F32) / HALF)
    ang = pos.astype(F32)[:, None] * inv[None, :]
    cos, sin = jnp.cos(ang), jnp.sin(ang)
    return jnp.tile(cos, (1, 4)), jnp.tile(jnp.concatenate([-sin, sin], axis=1), (1, 2))


def _pick_tile(n, pref):
    t = min(n, pref)
    while n % t:
        t //= 2
    return t


def _per_seq_cols(a, bsz, t, width):
    f = a.shape[1]
    a = jnp.moveaxis(a[0].reshape(f, bsz, t), 1, 0)
    return jnp.pad(a, ((0, 0), (0, 0), (0, width - t)))


def _value_blocks(past_vt, new_vt, bsz, t, lpad):
    lead = new_vt.shape[:-2]
    new_b = jnp.moveaxis(new_vt.reshape(*lead, V_ROWS, bsz, t), -2, 0)
    full = jnp.concatenate([past_vt, new_b], axis=-1)
    full = jnp.pad(full, [(0, 0)] * (full.ndim - 1) + [(0, lpad - full.shape[-1])])
    full = full.reshape(bsz, *lead, V_ROWS, lpad // KV_BLOCK, KV_BLOCK)
    return jnp.moveaxis(full, -2, -3)


def _with_ones_rows(vt):
    ones = jnp.ones(vt.shape[:-2] + (1, vt.shape[-1]), vt.dtype)
    zeros = jnp.zeros(vt.shape[:-2] + (V_ROWS - HEAD_DIM - 1, vt.shape[-1]), vt.dtype)
    return jnp.concatenate([vt, ones, zeros], axis=-2)


def _mixers(inp, past, n_past, lw, bsz, t):
    aqt, iqt, kik, avt, bqt, bk, bvt, cu, sm, smt = inp
    n_keys = n_past + t
    n_sel = min(TOPK_MAX, n_keys // 4)
    lpad = -(-n_keys // (2 * KV_BLOCK)) * 2 * KV_BLOCK
    if past is None:
        kik_all, avt_all, bk_all, bvt_all, logf_all = kik, avt, bk, bvt, sm
        hist16 = jnp.zeros((bsz, 16, POOL_WIDTH), F32)
        tq_a, tq_b, t_pad = _pick_tile(t, 256), _pick_tile(t, 256), t
    else:
        pa, pb, plf, pc = past
        pk, pv, pik = (pa[:, :, j].astype(BF16) for j in range(3))

        def join_rows(p, new):
            full = jnp.concatenate([p, new.reshape(bsz, t, new.shape[-1])], axis=1)
            return jnp.pad(full, ((0, 0), (0, lpad - n_keys), (0, 0)))

        kik_all = join_rows(jnp.concatenate([pk, pik], axis=-1), kik)
        avt_all = _value_blocks(_with_ones_rows(jnp.swapaxes(pv, 1, 2)), avt[0, 0], bsz, t, lpad)
        bk_all = join_rows(pb[:, :, 0].astype(BF16).reshape(bsz, n_past, 384), bk)
        pvt = jnp.transpose(pb[:, :, 1].astype(BF16), (0, 2, 3, 1))
        bvt_all = _value_blocks(_with_ones_rows(pvt), bvt[0, :, 0], bsz, t, lpad)
        logf_all = join_rows(jnp.pad(plf, ((0, 0), (0, 0), (0, LANES - B_HEADS))), sm)
        hist16 = jnp.pad(pc, ((0, 0), (1, 0), (0, 0)))
        tq_a = tq_b = t_pad = LANES
        aqt, iqt, bqt, smt = (_per_seq_cols(a, bsz, t, t_pad) for a in (aqt, iqt, bqt, smt))

    oa = _dsa(aqt, iqt, smt, kik_all, avt_all, tq=tq_a, n_keys=n_keys, q_pos0=n_past, n_sel=n_sel)
    ck = _cum_logf(logf_all, KV_BLOCK)
    ob = _fox(bqt, bk_all, bvt_all, ck, tq=tq_b, n_keys=n_keys, q_pos0=n_past)
    cu = cu.reshape(bsz, t, POOL_WIDTH)
    oc = _pool(cu, hist16, lw["pool_bd"], lw["pool_scale"], tc=_pick_tile(t, 256), start_pos=n_past)
    return oa[:, :t], ob[:, :t], oc


def _layer(x, mod, past, n_past, pos_tab, lw, layer, final_g, per_token):
    bsz, t, d = x.shape
    sh1, sc1, g1, sh2, sc2, g2 = mod
    if per_token:
        xt = x.reshape(1, bsz * t, d)
        sh1, sc1, g1, sh2, sc2, g2 = (jnp.broadcast_to(m, (bsz, t, d)).reshape(1, bsz * t, d) for m in mod)
        cos, sin = (jnp.tile(a, (bsz, 1)) for a in pos_tab)
    else:
        xt = x
        cos, sin = pos_tab
    tm = KV_BLOCK
    (aqt, iqt, na, kik, avt, nb, bqt, bk, bvt, cu, sm, smt, gates) = _in_proj(
        xt, sc1, sh1, lw["norm_mix_g"], lw["w_in"], lw["bf_bias"], cos, sin, tm)
    oa, ob, oc = _mixers((aqt, iqt, kik, avt, bqt, bk, bvt, cu, sm, smt), past, n_past, lw, bsz, t)

    def flat(a):
        return a.reshape(xt.shape[0], xt.shape[1], a.shape[-1])

    router = (lw["router_w"], lw["router_b"]) if layer % 2 else None
    res = _merge(xt, flat(oa), flat(ob), flat(oc), gates, g1, sc2, sh2, lw["norm_ffn_g"],
                 lw["w_br_a"], lw["w_br_b"], lw["w_br_c"], lw["w_out"], router, tm)
    tmf = _pick_tile(xt.shape[1], 512)
    if layer % 2 == 0:
        x_mid, h2 = res
        x_new = _ffn(x_mid, h2, g2, lw["ffn_wg"], lw["ffn_wu"], lw["ffn_wd"], final_g, tmf)
    else:
        x_mid, h2, gw = res
        x_new = _moe(x_mid, h2, g2, gw, lw["moe_wg"], lw["moe_wu"], lw["moe_wd"], final_g, tmf)
    new_a = na.reshape(bsz, t, 3, HEAD_DIM)
    new_b = nb.reshape(bsz, t, 2, B_HEADS, HEAD_DIM)
    new_logf = sm.reshape(bsz, t, LANES)[:, :, :B_HEADS]
    new_pool = cu.reshape(bsz, t, POOL_WIDTH)[:, t - POOL_HIST:, :]
    return x_new.reshape(bsz, t, d), (new_a, new_b, new_logf, new_pool)


def kernel(x_prompt, x_sample, cache_a_kvi, cache_b_kv, cache_b_logf, state_c_pool, c_prompt, c_sample,
           ada_w, ada_b, norm_mix_g, w_in, b_forget, pool_w, pool_scale, w_br_a, w_br_b, w_br_c, w_out,
           norm_ffn_g, ffn_w_gate, ffn_w_up, ffn_w_down, moe_router_w, moe_router_b, moe_w_gate,
           moe_w_up, moe_w_down, final_norm_g):
    depth = ada_w.shape[0]
    bp, tp, d = x_prompt.shape
    bs, ts, _ = x_sample.shape
    n_past = cache_a_kvi.shape[2]
    assert tp % KV_BLOCK == 0 and (bs * ts) % KV_BLOCK == 0 and ts <= LANES

    rows = -(-(bp + bs) // 8) * 8
    c_all = jnp.pad(jnp.concatenate([c_prompt, c_sample], axis=0), ((0, rows - bp - bs), (0, 0)))
    mod_all = _ada(c_all, ada_w, ada_b)

    tab_p = _rope_tables(jnp.arange(tp))
    tab_s = _rope_tables(n_past + jnp.arange(ts))

    xp, xs = x_prompt, x_sample
    outs_p, outs_s = [], []
    for layer in range(depth):
        j = layer // 2
        w_l, bias_l = _in_weights(w_in[layer], b_forget[layer])
        pw = pool_w[layer]
        pool_bd = jnp.zeros((POOL_WIDTH, POOL_WIDTH), F32)
        for g in range(len(POOL_WINDOWS)):
            sl = slice(g * POOL_GROUP_DIM, (g + 1) * POOL_GROUP_DIM)
            pool_bd = pool_bd.at[sl, sl].set(pw[g])
        lw = dict(w_in=w_l, bf_bias=bias_l, norm_mix_g=norm_mix_g[layer], norm_ffn_g=norm_ffn_g[layer],
                  pool_bd=pool_bd.astype(BF16), pool_scale=pool_scale[layer],
                  w_br_a=w_br_a[layer].astype(BF16), w_br_b=w_br_b[layer].astype(BF16),
                  w_br_c=w_br_c[layer].astype(BF16), w_out=w_out[layer].astype(BF16))
        if layer % 2 == 0:
            lw.update(ffn_wg=ffn_w_gate[j].astype(BF16), ffn_wu=ffn_w_up[j].astype(BF16),
                      ffn_wd=ffn_w_down[j].astype(BF16))
        else:
            lw.update(router_w=jnp.pad(moe_router_w[j], ((0, 0), (0, LANES - N_EXPERTS))),
                      router_b=jnp.pad(moe_router_b[j], (0, LANES - N_EXPERTS)).reshape(1, LANES),
                      moe_wg=moe_w_gate[j].astype(BF16), moe_wu=moe_w_up[j].astype(BF16),
                      moe_wd=moe_w_down[j].astype(BF16))
        final_g = final_norm_g if layer == depth - 1 else None
        mod_p = [m[:, None, :] for m in jnp.split(mod_all[layer, :bp], 6, axis=-1)]
        mod_s = [m[:, None, :] for m in jnp.split(mod_all[layer, bp:bp + bs], 6, axis=-1)]
        xp, new_p = _layer(xp, mod_p, None, 0, tab_p, lw, layer, final_g, per_token=False)
        past = (cache_a_kvi[layer], cache_b_kv[layer], cache_b_logf[layer], state_c_pool[layer])
        xs, new_s = _layer(xs, mod_s, past, n_past, tab_s, lw, layer, final_g, per_token=True)
        outs_p.append(new_p)
        outs_s.append(new_s)

    def stack(outs, k):
        return jnp.stack([o[k] for o in outs])

    return (xp, xs,
            stack(outs_p, 0), stack(outs_p, 1), stack(outs_p, 2), stack(outs_p, 3),
            stack(outs_s, 0), stack(outs_s, 1), stack(outs_s, 2), stack(outs_s, 3))
```

```python
import functools

import jax
import jax.numpy as jnp
import numpy as np
from jax import lax
from jax.experimental import pallas as pl
from jax.experimental.pallas import tpu as pltpu

F32 = jnp.float32
BF16 = jnp.bfloat16
I32 = jnp.int32

D_MODEL = 1024
CHUNK = 64
HEAD_DIM = 64
HALF = HEAD_DIM // 2
ROPE_THETA = 10000.0
NORM_EPS = 1e-6
A_HEADS = 6
IDX_HEADS = 4
TOPK_MAX = 256
B_HEADS = 6
POOL_WINDOWS = (2, 4, 8, 16)
POOL_GROUP_DIM = 64
POOL_WIDTH = 256
POOL_HIST = 15
N_EXPERTS = 8
LANES = 128
SUBLANES = 8
LOG2E = 1.4426950408889634
QK_SCALE = HEAD_DIM ** -0.5 * LOG2E
KV_BLOCK = 256
V_ROWS = HEAD_DIM + 16
VMEM_LIMIT = 56 * 1024 * 1024
NEG_INF = float("-inf")
KEY_DT = jnp.bfloat16

C_AQ, C_IQ, C_A, C_B, C_CU, C_SM, C_GATE, C_END = 0, 384, 640, 896, 2048, 2304, 2432, 5504


def _cparams(sem):
    return pltpu.CompilerParams(dimension_semantics=sem, vmem_limit_bytes=VMEM_LIMIT)


def _const_spec(shape):
    nd = len(shape)
    return pl.BlockSpec(shape, lambda *_: (0,) * nd, pipeline_mode=pl.Buffered(1))


def _lane_iota(shape):
    return lax.broadcasted_iota(I32, shape, len(shape) - 1)


def _row_iota(shape):
    return lax.broadcasted_iota(I32, shape, len(shape) - 2)


def _ada_kernel(c_ref, w_ref, b_ref, o_ref):
    c = c_ref[...]
    s = c * jax.nn.sigmoid(c)
    o_ref[...] = jnp.dot(s, w_ref[...], preferred_element_type=F32,
                         precision=lax.Precision.HIGHEST) + b_ref[...]


def _ada(c_all, ada_w, ada_b):
    depth, d, n = ada_w.shape
    rows = c_all.shape[0]
    tn = 1536
    return pl.pallas_call(
        _ada_kernel,
        grid=(depth, n // tn),
        in_specs=[pl.BlockSpec((rows, d), lambda l, j: (0, 0)),
                  pl.BlockSpec((None, d, tn), lambda l, j: (l, 0, j)),
                  pl.BlockSpec((None, 1, tn), lambda l, j: (l, 0, j))],
        out_specs=pl.BlockSpec((None, rows, tn), lambda l, j: (l, 0, j)),
        out_shape=jax.ShapeDtypeStruct((depth, rows, n), F32),
        compiler_params=_cparams(("arbitrary", "arbitrary")),
        name="ada_mod",
    )(c_all, ada_w, ada_b.reshape(depth, 1, n))


def _in_kernel(x_ref, sc_ref, sh_ref, g_ref, w_ref, bf_ref, cos_ref, sin_ref,
               aqt_ref, iqt_ref, na_ref, kik_ref, avt_ref, nb_ref, bqt_ref, bk_ref, bvt_ref,
               cu_ref, sm_ref, smt_ref, gate_ref):
    x = x_ref[...]
    ms = jnp.mean(x * x, axis=-1, keepdims=True)
    y = x * lax.rsqrt(ms + NORM_EPS) * g_ref[...]
    h = (y * (1.0 + sc_ref[...]) + sh_ref[...]).astype(BF16)
    tm = x.shape[0]

    def mm(a, b):
        return jnp.dot(h, w_ref[:, a:b], preferred_element_type=F32)

    cos = cos_ref[...]
    sin = sin_ref[...]
    lane = _lane_iota((tm, LANES))
    low = lane < HEAD_DIM
    first_half = (lane & HALF) == 0

    def rope(z):
        swapped = jnp.where(first_half, pltpu.roll(z, LANES - HALF, 1), pltpu.roll(z, HALF, 1))
        return z * cos + swapped * sin

    zeros64 = jnp.zeros((HEAD_DIM, tm), BF16)
    ones_rows = jnp.where(_row_iota((V_ROWS - HEAD_DIM, tm)) == 0, 1.0, 0.0).astype(BF16)

    def put_heads(ref, zt, p, slot_even, slot_odd):
        for hh, slot in ((0, slot_even), (1, slot_odd)):
            base = (2 * p + hh) * LANES
            ref[base + slot * HEAD_DIM:base + (slot + 1) * HEAD_DIM, :] = zt[hh * HEAD_DIM:(hh + 1) * HEAD_DIM]
            ref[base + (1 - slot) * HEAD_DIM:base + (2 - slot) * HEAD_DIM, :] = zeros64

    z = mm(C_AQ, C_IQ)
    for p in range(3):
        zt = (rope(z[:, p * LANES:(p + 1) * LANES]) * QK_SCALE).T.astype(BF16)
        put_heads(aqt_ref, zt, p, 0, 0)
    z = mm(C_IQ, C_A)
    for p in range(2):
        zt = rope(z[:, p * LANES:(p + 1) * LANES]).T.astype(BF16)
        put_heads(iqt_ref, zt, p, 1, 1)

    z = mm(C_A, C_B)
    kv = z[:, :LANES]
    r0 = jnp.where(low, rope(kv), kv)
    r1 = rope(z[:, LANES:])
    na_ref[...] = jnp.concatenate([r0, r1], axis=1)[:, :3 * HEAD_DIM]
    kik_ref[...] = jnp.where(low, r0, pltpu.roll(r1, HEAD_DIM, 1)).astype(BF16)
    avt_ref[0:HEAD_DIM, :] = r0.T[HEAD_DIM:, :].astype(BF16)
    avt_ref[HEAD_DIM:, :] = ones_rows

    z = mm(C_B, C_CU)
    nb_ref[...] = z[:, 384:]
    bk_ref[...] = z[:, 384:768].astype(BF16)
    for p in range(3):
        zt = (z[:, p * LANES:(p + 1) * LANES] * QK_SCALE).T.astype(BF16)
        put_heads(bqt_ref, zt, p, 0, 1)
        vt = z[:, 768 + p * LANES:768 + (p + 1) * LANES].T.astype(BF16)
        for hh in range(2):
            bvt_ref[2 * p + hh, 0:HEAD_DIM, :] = vt[hh * HEAD_DIM:(hh + 1) * HEAD_DIM]
            bvt_ref[2 * p + hh, HEAD_DIM:, :] = ones_rows

    cu_ref[...] = mm(C_CU, C_SM)

    z = mm(C_SM, C_GATE)
    t = z + bf_ref[...]
    logf = jnp.minimum(t, 0.0) - jnp.log1p(jnp.exp(-jnp.abs(t)))
    sm = jnp.where(lane < B_HEADS, logf, z)
    sm_ref[...] = sm
    smt_ref[...] = sm.T[0:16, :]

    for c in range(3):
        gate_ref[:, c * D_MODEL:(c + 1) * D_MODEL] = jax.nn.sigmoid(
            mm(C_GATE + c * D_MODEL, C_GATE + (c + 1) * D_MODEL))


def _in_weights(w_in_l, b_forget_l):
    d = w_in_l.shape[0]
    sizes = (384, 64, 64, 256, 4, 64, 384, 384, 384, 6, 256, 3072)
    o = np.concatenate([[0], np.cumsum(sizes)])
    w_in_l = w_in_l.astype(BF16)
    cols = [w_in_l[:, o[0]:o[1]],
            w_in_l[:, o[3]:o[4]],
            w_in_l[:, o[1]:o[3]], w_in_l[:, o[5]:o[6]], jnp.zeros((d, 64), BF16),
            w_in_l[:, o[6]:o[9]],
            w_in_l[:, o[10]:o[11]],
            w_in_l[:, o[9]:o[10]], jnp.zeros((d, 2), BF16), w_in_l[:, o[4]:o[5]], jnp.zeros((d, LANES - 12), BF16),
            w_in_l[:, o[11]:o[12]]]
    w = jnp.concatenate(cols, axis=1)
    assert w.shape[1] == C_END, w.shape
    bias = jnp.concatenate([b_forget_l, jnp.zeros((LANES - B_HEADS,), F32)]).reshape(1, LANES)
    return w, bias


def _in_proj(x, sc, sh, g, w, bias, cos, sin, tm):
    bsz, t, d = x.shape
    mrows = sc.shape[1]
    mblk = 1 if mrows == 1 else tm
    mod_spec = pl.BlockSpec((None, mblk, d), (lambda b, i: (b, 0, 0)) if mrows == 1 else (lambda b, i: (b, i, 0)))
    nblk = t // tm

    def rows(n, dt):
        return pl.BlockSpec((None, tm, n), lambda b, i: (b, i, 0)), jax.ShapeDtypeStruct((bsz, t, n), dt)

    def cols(n, dt):
        return pl.BlockSpec((None, n, tm), lambda b, i: (b, 0, i)), jax.ShapeDtypeStruct((bsz, n, t), dt)

    outs = [cols(A_HEADS * LANES, BF16), cols(IDX_HEADS * LANES, BF16), rows(192, F32), rows(LANES, BF16),
            (pl.BlockSpec((None, None, V_ROWS, tm), lambda b, i: (b, i, 0, 0)),
             jax.ShapeDtypeStruct((bsz, nblk, V_ROWS, tm), BF16)),
            rows(768, F32), cols(B_HEADS * LANES, BF16), rows(384, BF16),
            (pl.BlockSpec((None, B_HEADS, None, V_ROWS, tm), lambda b, i: (b, 0, i, 0, 0)),
             jax.ShapeDtypeStruct((bsz, B_HEADS, nblk, V_ROWS, tm), BF16)),
            rows(256, F32), rows(LANES, F32), cols(16, F32), rows(3 * D_MODEL, F32)]
    return pl.pallas_call(
        _in_kernel,
        grid=(bsz, nblk),
        in_specs=[pl.BlockSpec((None, tm, d), lambda b, i: (b, i, 0)), mod_spec, mod_spec,
                  _const_spec((1, d)), _const_spec(w.shape), _const_spec((1, LANES)),
                  pl.BlockSpec((tm, LANES), lambda b, i: (i, 0)),
                  pl.BlockSpec((tm, LANES), lambda b, i: (i, 0))],
        out_specs=[o[0] for o in outs],
        out_shape=[o[1] for o in outs],
        compiler_params=_cparams(("parallel", "parallel")),
        name="in_proj",
    )(x, sc, sh, g.reshape(1, d), w, bias, cos, sin)


def _cum_kernel(x_ref, o_ref, carry_ref):
    @pl.when(pl.program_id(1) == 0)
    def _():
        carry_ref[...] = jnp.zeros_like(carry_ref)

    tc = x_ref.shape[0]
    tri = jnp.where(_lane_iota((tc, tc)) <= _row_iota((tc, tc)), 1.0, 0.0)
    cum = jnp.dot(tri, x_ref[...], preferred_element_type=F32, precision=lax.Precision.HIGHEST) + carry_ref[0:1, :]
    for hd in range(B_HEADS):
        o_ref[hd] = jnp.broadcast_to(cum[:, hd:hd + 1] * LOG2E, (tc, LANES))
    carry_ref[...] = jnp.broadcast_to(cum[tc - 1:tc, :], carry_ref.shape)


def _cum_logf(x, tc):
    bsz, t, n = x.shape
    return pl.pallas_call(
        _cum_kernel,
        grid=(bsz, t // tc),
        in_specs=[pl.BlockSpec((None, tc, n), lambda b, i: (b, i, 0))],
        out_specs=pl.BlockSpec((None, B_HEADS, tc, LANES), lambda b, i: (b, 0, i, 0)),
        out_shape=jax.ShapeDtypeStruct((bsz, B_HEADS, t, LANES), F32),
        scratch_shapes=[pltpu.VMEM((SUBLANES, LANES), F32)],
        compiler_params=_cparams(("parallel", "arbitrary")),
        name="logf_cumsum",
    )(x)


def _sum_keys(x):
    part = x.reshape(x.shape[0] // SUBLANES, SUBLANES, x.shape[1]).sum(axis=0)
    return jnp.sum(part, axis=0, keepdims=True)


def _max_keys(x):
    part = x.reshape(x.shape[0] // SUBLANES, SUBLANES, x.shape[1]).max(axis=0)
    return jnp.max(part, axis=0, keepdims=True)


def _dsa_kernel(aqt_ref, iqt_ref, smt_ref, kik_ref, avt_ref, o_ref, key_ref,
                *, tq, tk, n_keys, q_pos0, n_sel):
    i = pl.program_id(1)
    pos_first = q_pos0 + i * tq
    last_chunk = (pos_first + tq - 1) // CHUNK
    n_adm = jnp.minimum((last_chunk + 1) * CHUNK, n_keys)
    nkb = (n_adm + tk - 1) // tk
    n_pairs = (nkb + 1) // 2

    q_pos = pos_first + _lane_iota((1, tq))
    q_lim = jnp.minimum((q_pos // CHUNK + 1) * CHUNK, n_keys)
    key_row = _row_iota((tk, tq))

    def keys(kb):
        return kik_ref[pl.ds(pl.multiple_of(kb * tk, tk), tk), :]

    iq4 = jnp.concatenate([iqt_ref[hd * LANES:(hd + 1) * LANES, :] for hd in range(IDX_HEADS)], axis=1)
    smt = smt_ref[...]
    w_rows = [smt[8 + hd:9 + hd, :] for hd in range(IDX_HEADS)]

    def score_body(kb, carry):
        s4 = jnp.dot(keys(kb), iq4, preferred_element_type=F32)
        score = w_rows[0] * jnp.maximum(s4[:, 0:tq], 0.0)
        for hd in range(1, IDX_HEADS):
            score = score + w_rows[hd] * jnp.maximum(s4[:, hd * tq:(hd + 1) * tq], 0.0)
        score = jnp.where(key_row < q_lim - kb * tk, score, NEG_INF)
        key_ref[kb] = score.astype(KEY_DT)
        return carry

    lax.fori_loop(0, nkb, score_body, 0)

    @pl.when(nkb % 2 == 1)
    def _():
        key_ref[nkb] = jnp.full((tk, tq), NEG_INF, KEY_DT)

    one, zero = jnp.ones((), KEY_DT), jnp.zeros((), KEY_DT)
    packed_rows = 2 * SUBLANES

    def count(cand, strict):
        def hits(blk):
            h = jnp.where((blk > cand) if strict else (blk >= cand), one, zero)
            parts = [h[r * packed_rows:(r + 1) * packed_rows] for r in range(tk // packed_rows)]
            while len(parts) > 1:
                parts = [a + b for a, b in zip(parts[::2], parts[1::2])]
            return parts[0].astype(F32)

        def body(j, acc):
            return acc + hits(key_ref[2 * j]) + hits(key_ref[2 * j + 1])

        acc = lax.fori_loop(0, n_pairs, body, jnp.zeros((packed_rows, tq), F32))
        return jnp.sum(acc, axis=0, keepdims=True)

    def pattern_value(u):
        bits = jnp.where(u >= 0x8000, u & 0x7FFF, (~u) & 0xFFFF)
        return lax.bitcast_convert_type(lax.shift_left(bits, 16), F32).astype(KEY_DT)

    def bit_body(b, u):
        cand_u = u | lax.shift_left(jnp.int32(1), 15 - b)
        cnt = count(pattern_value(cand_u), False)
        return jnp.where(cnt >= n_sel, cand_u, u)

    u_thr = lax.fori_loop(0, 16, bit_body, jnp.zeros((1, tq), I32))
    thr_k = pattern_value(jnp.maximum(u_thr, 0x007F))
    need = n_sel - count(thr_k, True)
    thr = thr_k.astype(F32)

    aq6 = jnp.concatenate([aqt_ref[hd * LANES:(hd + 1) * LANES, :] for hd in range(A_HEADS)], axis=1)

    half = tk // 2
    lower = jnp.where(_lane_iota((half, half)) <= _row_iota((half, half)), 1.0, 0.0).astype(BF16)

    def attend():
        def body(kb, carry):
            eq_seen, ms, accs = carry
            blk = key_ref[kb].astype(F32)
            eq = blk == thr
            eq_f = jnp.where(eq, 1.0, 0.0)
            prefs = []
            for e in (eq_f[:half], eq_f[half:]):
                prefs.append(jnp.dot(lower, e.astype(BF16), preferred_element_type=F32) + eq_seen)
                eq_seen = eq_seen + _sum_keys(e)
            slack = jnp.where(blk >= thr, need - jnp.where(eq, jnp.concatenate(prefs, axis=0), 0.0), -1.0)
            bias = jnp.where(slack >= 0.0, jnp.where(jnp.abs(blk) < jnp.inf, 0.0, NEG_INF), NEG_INF)
            logits = jnp.dot(keys(kb), aq6, preferred_element_type=F32)
            vts = (avt_ref[2 * kb], avt_ref[2 * kb + 1])
            new_ms, new_accs = [], []
            for p in range(A_HEADS // 2):
                ps, alphas = [], []
                for hd in (2 * p, 2 * p + 1):
                    lg = logits[:, hd * tq:(hd + 1) * tq] + bias
                    m_old = ms[hd]
                    m_new = jnp.maximum(m_old, _max_keys(lg))
                    m_safe = jnp.where(m_new == NEG_INF, 0.0, m_new)
                    ps.append(jnp.exp2(lg - m_safe).astype(BF16))
                    alphas.append(jnp.exp2(m_old - m_safe))
                    new_ms.append(m_new)
                p2 = jnp.concatenate(ps, axis=1)
                pv = (jnp.dot(vts[0], p2[:tk // 2], preferred_element_type=F32)
                      + jnp.dot(vts[1], p2[tk // 2:], preferred_element_type=F32))
                new_accs.append(jnp.concatenate(alphas, axis=1) * accs[p] + pv)
            return eq_seen, tuple(new_ms), tuple(new_accs)

        init = (jnp.zeros((1, tq), F32),
                tuple(jnp.full((1, tq), NEG_INF, F32) for _ in range(A_HEADS)),
                tuple(jnp.zeros((V_ROWS, 2 * tq), F32) for _ in range(A_HEADS // 2)))
        return lax.fori_loop(0, nkb, body, init)[2]

    accs = attend()

    outs = []
    for p in range(A_HEADS // 2):
        o2 = accs[p][0:HEAD_DIM] / accs[p][HEAD_DIM:HEAD_DIM + 1]
        outs += [o2[:, 0:tq], o2[:, tq:2 * tq]]
    o_ref[...] = jnp.concatenate(outs, axis=0).T.astype(o_ref.dtype)


def _dsa(aqt, iqt, smt, kik, avt, *, tq, n_keys, q_pos0, n_sel):
    bsz, _, t_q = aqt.shape
    _, nblk, _, tkv = avt.shape
    assert nblk % 2 == 0
    tk = 2 * tkv
    kern = functools.partial(_dsa_kernel, tq=tq, tk=tk, n_keys=n_keys, q_pos0=q_pos0, n_sel=n_sel)
    return pl.pallas_call(
        kern,
        grid=(bsz, t_q // tq),
        in_specs=[pl.BlockSpec((None, A_HEADS * LANES, tq), lambda b, i: (b, 0, i)),
                  pl.BlockSpec((None, IDX_HEADS * LANES, tq), lambda b, i: (b, 0, i)),
                  pl.BlockSpec((None, 16, tq), lambda b, i: (b, 0, i)),
                  pl.BlockSpec((None, nblk * tkv, LANES), lambda b, i: (b, 0, 0)),
                  pl.BlockSpec((None, nblk, V_ROWS, tkv), lambda b, i: (b, 0, 0, 0))],
        out_specs=pl.BlockSpec((None, tq, 384), lambda b, i: (b, i, 0)),
        out_shape=jax.ShapeDtypeStruct((bsz, t_q, 384), BF16),
        scratch_shapes=[pltpu.VMEM((2 * ((nblk // 2 + 1) // 2), tk, tq), KEY_DT)],
        compiler_params=_cparams(("parallel", "arbitrary")),
        name="dsa_attention",
    )(aqt, iqt, smt, kik, avt)


def _fox_kernel(qt_ref, k_ref, vt_ref, ck_ref, o_ref, *, tq, tk, n_keys, q_pos0):
    i = pl.program_id(1)
    pos_first = q_pos0 + i * tq
    n_full = pos_first // tk
    nkb = (jnp.minimum(pos_first + tq, n_keys) + tk - 1) // tk
    q_pos = pos_first + _lane_iota((1, tq))
    key_row = _row_iota((tk, tq))
    qts = [qt_ref[hd * LANES:(hd + 1) * LANES, :] for hd in range(B_HEADS)]

    def step(kbs, state, masked):
        logits = []
        for hd in range(B_HEADS):
            for kb in kbs:
                rows = pl.ds(pl.multiple_of(kb * tk, tk), tk)
                ck = ck_ref[hd, rows, :]
                kblk = k_ref[rows, (hd // 2) * LANES:(hd // 2 + 1) * LANES]
                lg = (jnp.dot(kblk, qts[hd], preferred_element_type=F32)
                      - jnp.concatenate([ck] * (tq // LANES), axis=1))
                if masked:
                    lg = jnp.where(key_row <= q_pos - kb * tk, lg, NEG_INF)
                logits.append(lg)
        new = []
        for hd in range(B_HEADS):
            m_old, acc = state[hd]
            lgs = logits[hd * len(kbs):(hd + 1) * len(kbs)]
            m_new = m_old
            for lg in lgs:
                m_new = jnp.maximum(m_new, _max_keys(lg))
            m_safe = jnp.where(m_new == NEG_INF, 0.0, m_new) if masked else m_new
            acc = jnp.exp2(m_old - m_safe) * acc
            for kb, lg in zip(kbs, lgs):
                p = jnp.exp2(lg - m_safe).astype(BF16)
                acc = acc + jnp.dot(vt_ref[hd, kb], p, preferred_element_type=F32)
            new.append((m_new, acc))
        return tuple(new)

    init = tuple((jnp.full((1, tq), NEG_INF, F32), jnp.zeros((V_ROWS, tq), F32)) for _ in range(B_HEADS))
    state = lax.fori_loop(0, n_full // 2, lambda j, st: step((2 * j, 2 * j + 1), st, False), init)
    state = lax.fori_loop(2 * (n_full // 2), nkb, lambda kb, st: step((kb,), st, True), state)
    outs = [acc[0:HEAD_DIM] / acc[HEAD_DIM:HEAD_DIM + 1] for _, acc in state]
    o_ref[...] = jnp.concatenate(outs, axis=0).T.astype(o_ref.dtype)


def _fox(qt, k, vt, ck, *, tq, n_keys, q_pos0):
    bsz, _, t_q = qt.shape
    _, _, nblk, _, tk = vt.shape
    lpad = nblk * tk
    kern = functools.partial(_fox_kernel, tq=tq, tk=tk, n_keys=n_keys, q_pos0=q_pos0)
    return pl.pallas_call(
        kern,
        grid=(bsz, t_q // tq),
        in_specs=[pl.BlockSpec((None, B_HEADS * LANES, tq), lambda b, i: (b, 0, i)),
                  pl.BlockSpec((None, lpad, 384), lambda b, i: (b, 0, 0)),
                  pl.BlockSpec((None, B_HEADS, nblk, V_ROWS, tk), lambda b, i: (b, 0, 0, 0, 0)),
                  pl.BlockSpec((None, B_HEADS, lpad, LANES), lambda b, i: (b, 0, 0, 0))],
        out_specs=pl.BlockSpec((None, tq, 384), lambda b, i: (b, i, 0)),
        out_shape=jax.ShapeDtypeStruct((bsz, t_q, 384), BF16),
        compiler_params=_cparams(("parallel", "arbitrary")),
        name="fox_attention",
    )(qt, k, vt, ck)


def _pool_kernel(cur_ref, prev_ref, hist_ref, w_ref, s_ref, o_ref, ext, *, tc, start_pos):
    i = pl.program_id(1)
    cur = cur_ref[...]
    ext[0:16, :] = jnp.where(i == 0, hist_ref[...], prev_ref[tc - 16:, :])
    ext[16:, :] = cur
    pos = start_pos + i * tc + lax.broadcasted_iota(I32, (tc, POOL_WIDTH), 0)
    lane = _lane_iota((tc, POOL_WIDTH))
    run = cur
    pooled = jnp.zeros_like(cur)
    k = 1
    for g, w in enumerate(POOL_WINDOWS):
        while k < w:
            run = run + ext[16 - k:16 - k + tc, :]
            k += 1
        cnt = jnp.minimum(pos + 1, w).astype(F32)
        in_group = (lane >= g * POOL_GROUP_DIM) & (lane < (g + 1) * POOL_GROUP_DIM)
        pooled = jnp.where(in_group, run / cnt, pooled)
    z = (pooled - cur).astype(BF16)
    o_ref[...] = (jnp.dot(z, w_ref[...], preferred_element_type=F32) * s_ref[...]).astype(o_ref.dtype)


def _pool(cu, hist16, w_bd, scale, *, tc, start_pos):
    bsz, t, n = cu.shape
    kern = functools.partial(_pool_kernel, tc=tc, start_pos=start_pos)
    return pl.pallas_call(
        kern,
        grid=(bsz, t // tc),
        in_specs=[pl.BlockSpec((None, tc, n), lambda b, i: (b, i, 0)),
                  pl.BlockSpec((None, tc, n), lambda b, i: (b, jnp.maximum(i - 1, 0), 0)),
                  pl.BlockSpec((None, 16, n), lambda b, i: (b, 0, 0)),
                  _const_spec((n, n)), _const_spec((1, n))],
        out_specs=pl.BlockSpec((None, tc, n), lambda b, i: (b, i, 0)),
        out_shape=jax.ShapeDtypeStruct((bsz, t, n), BF16),
        scratch_shapes=[pltpu.VMEM((16 + tc, n), F32)],
        compiler_params=_cparams(("parallel", "arbitrary")),
        name="pool_mixer",
    )(cu, cu, hist16, w_bd, scale.reshape(1, n))


def _route(logits):
    lane = _lane_iota(logits.shape).astype(F32)
    lg = jnp.where(lane < N_EXPERTS, logits, NEG_INF)
    m1 = jnp.max(lg, axis=1, keepdims=True)
    i1 = jnp.min(jnp.where(lg == m1, lane, float(LANES)), axis=1, keepdims=True)
    hot1 = lane == i1
    lg2 = jnp.where(hot1, NEG_INF, lg)
    m2 = jnp.max(lg2, axis=1, keepdims=True)
    i2 = jnp.min(jnp.where(lg2 == m2, lane, float(LANES)), axis=1, keepdims=True)
    hot2 = lane == i2
    e2 = jnp.exp(m2 - m1)
    den = 1.0 + e2
    return jnp.where(hot1, 1.0 / den, 0.0) + jnp.where(hot2, e2 / den, 0.0)


def _merge_kernel(x_ref, oa_ref, ob_ref, oc_ref, gate_ref, g1_ref, sc2_ref, sh2_ref, g_ref,
                  wa_ref, wb_ref, wc_ref, wo_ref, *rest, moe):
    if moe:
        rw_ref, rb_ref, xo_ref, h_ref, gw_ref = rest
    else:
        xo_ref, h_ref = rest
    d = D_MODEL
    merged = (gate_ref[:, 0:d] * jnp.dot(oa_ref[...], wa_ref[...], preferred_element_type=F32)
              + gate_ref[:, d:2 * d] * jnp.dot(ob_ref[...], wb_ref[...], preferred_element_type=F32)
              + gate_ref[:, 2 * d:3 * d] * jnp.dot(oc_ref[...], wc_ref[...], preferred_element_type=F32))
    x = x_ref[...] + g1_ref[...] * jnp.dot(merged.astype(BF16), wo_ref[...], preferred_element_type=F32)
    xo_ref[...] = x
    ms = jnp.mean(x * x, axis=-1, keepdims=True)
    y = x * lax.rsqrt(ms + NORM_EPS) * g_ref[...]
    h = y * (1.0 + sc2_ref[...]) + sh2_ref[...]
    h_ref[...] = h.astype(BF16)
    if moe:
        h_hi = h.astype(BF16)
        h_lo = (h - h_hi.astype(F32)).astype(BF16)
        logits = (jnp.dot(h_hi, rw_ref[0], preferred_element_type=F32)
                  + jnp.dot(h_lo, rw_ref[0], preferred_element_type=F32)
                  + jnp.dot(h_hi, rw_ref[1], preferred_element_type=F32)) + rb_ref[...]
        gw_ref[...] = _route(logits)


def _merge(x, oa, ob, oc, gates, g1, sc2, sh2, g, wa, wb, wc, wo, router, tm):
    bsz, t, d = x.shape
    mrows = g1.shape[1]
    mblk = 1 if mrows == 1 else tm
    mod_spec = pl.BlockSpec((None, mblk, d), (lambda b, i: (b, 0, 0)) if mrows == 1 else (lambda b, i: (b, i, 0)))

    def tok(n):
        return pl.BlockSpec((None, tm, n), lambda b, i: (b, i, 0))

    in_specs = [tok(d), tok(384), tok(384), tok(256), tok(3 * d), mod_spec, mod_spec, mod_spec,
                _const_spec((1, d)), _const_spec(wa.shape), _const_spec(wb.shape), _const_spec(wc.shape),
                _const_spec(wo.shape)]
    args = [x, oa, ob, oc, gates, g1, sc2, sh2, g.reshape(1, d), wa, wb, wc, wo]
    out_specs = [tok(d), tok(d)]
    out_shape = [jax.ShapeDtypeStruct((bsz, t, d), F32), jax.ShapeDtypeStruct((bsz, t, d), BF16)]
    if router is not None:
        rw, rb = router
        in_specs += [_const_spec(rw.shape), _const_spec(rb.shape)]
        args += [rw, rb]
        out_specs.append(tok(LANES))
        out_shape.append(jax.ShapeDtypeStruct((bsz, t, LANES), F32))
    return pl.pallas_call(
        functools.partial(_merge_kernel, moe=router is not None),
        grid=(bsz, t // tm),
        in_specs=in_specs, out_specs=out_specs, out_shape=out_shape,
        compiler_params=_cparams(("parallel", "parallel")),
        name="merge_out",
    )(*args)


def _final_norm(x, gain):
    ms = jnp.mean(x * x, axis=-1, keepdims=True)
    return x * lax.rsqrt(ms + NORM_EPS) * gain


def _ffn_kernel(x_ref, h_ref, g2_ref, wg_ref, wu_ref, wd_ref, *rest, n_chunks, final):
    if final:
        fg_ref, o_ref = rest
    else:
        (o_ref,) = rest
    h = h_ref[...]
    tf = wg_ref.shape[1] // n_chunks
    acc = jnp.zeros(x_ref.shape, F32)
    for c in range(n_chunks):
        gt = jnp.dot(h, wg_ref[:, c * tf:(c + 1) * tf], preferred_element_type=F32)
        up = jnp.dot(h, wu_ref[:, c * tf:(c + 1) * tf], preferred_element_type=F32)
        act = (gt * jax.nn.sigmoid(gt) * up).astype(BF16)
        acc = acc + jnp.dot(act, wd_ref[c * tf:(c + 1) * tf, :], preferred_element_type=F32)
    x = x_ref[...] + g2_ref[...] * acc
    o_ref[...] = _final_norm(x, fg_ref[...]) if final else x


def _ffn(x, h, g2, wg, wu, wd, final_g, tm):
    bsz, t, d = x.shape
    mrows = g2.shape[1]
    mblk = 1 if mrows == 1 else tm
    mod_spec = pl.BlockSpec((None, mblk, d), (lambda b, i: (b, 0, 0)) if mrows == 1 else (lambda b, i: (b, i, 0)))
    tok = pl.BlockSpec((None, tm, d), lambda b, i: (b, i, 0))
    in_specs = [tok, tok, mod_spec, _const_spec(wg.shape), _const_spec(wu.shape), _const_spec(wd.shape)]
    args = [x, h, g2, wg, wu, wd]
    if final_g is not None:
        in_specs.append(_const_spec((1, d)))
        args.append(final_g.reshape(1, d))
    return pl.pallas_call(
        functools.partial(_ffn_kernel, n_chunks=2, final=final_g is not None),
        grid=(bsz, t // tm),
        in_specs=in_specs, out_specs=tok,
        out_shape=jax.ShapeDtypeStruct((bsz, t, d), F32),
        compiler_params=_cparams(("parallel", "parallel")),
        name="ffn_dense",
    )(*args)


def _moe_kernel(x_ref, h_ref, g2_ref, gw_ref, wg_ref, wu_ref, wd_ref, *rest, final):
    if final:
        fg_ref, o_ref, acc_ref = rest
    else:
        o_ref, acc_ref = rest
    e = pl.program_id(2)

    @pl.when(e == 0)
    def _():
        acc_ref[...] = jnp.zeros_like(acc_ref)

    h = h_ref[...]
    gt = jnp.dot(h, wg_ref[...], preferred_element_type=F32)
    up = jnp.dot(h, wu_ref[...], preferred_element_type=F32)
    act = (gt * jax.nn.sigmoid(gt) * up).astype(BF16)
    y = jnp.dot(act, wd_ref[...], preferred_element_type=F32)
    gw = gw_ref[...]
    ge = jnp.sum(jnp.where(_lane_iota(gw.shape) == e, gw, 0.0), axis=1, keepdims=True)
    acc_ref[...] += ge * y

    @pl.when(e == pl.num_programs(2) - 1)
    def _():
        x = x_ref[...] + g2_ref[...] * acc_ref[...]
        o_ref[...] = _final_norm(x, fg_ref[...]) if final else x


def _moe(x, h, g2, gw, wg, wu, wd, final_g, tm):
    bsz, t, d = x.shape
    n_e, _, dff = wg.shape
    mrows = g2.shape[1]
    mblk = 1 if mrows == 1 else tm
    mod_spec = pl.BlockSpec((None, mblk, d), (lambda b, i, e: (b, 0, 0)) if mrows == 1 else (lambda b, i, e: (b, i, 0)))
    tok = pl.BlockSpec((None, tm, d), lambda b, i, e: (b, i, 0))
    in_specs = [tok, tok, mod_spec, pl.BlockSpec((None, tm, LANES), lambda b, i, e: (b, i, 0)),
                pl.BlockSpec((None, d, dff), lambda b, i, e: (e, 0, 0)),
                pl.BlockSpec((None, d, dff), lambda b, i, e: (e, 0, 0)),
                pl.BlockSpec((None, dff, d), lambda b, i, e: (e, 0, 0))]
    args = [x, h, g2, gw, wg, wu, wd]
    if final_g is not None:
        in_specs.append(pl.BlockSpec((1, d), lambda b, i, e: (0, 0)))
        args.append(final_g.reshape(1, d))
    return pl.pallas_call(
        functools.partial(_moe_kernel, final=final_g is not None),
        grid=(bsz, t // tm, n_e),
        in_specs=in_specs, out_specs=tok,
        out_shape=jax.ShapeDtypeStruct((bsz, t, d), F32),
        scratch_shapes=[pltpu.VMEM((tm, d), F32)],
        compiler_params=_cparams(("parallel", "parallel", "arbitrary")),
        name="moe_dense",
    )(*args)


def _rope_tables(pos):
    inv = ROPE_THETA ** (-jnp.arange(HALF, dtype=F32) / HALF)
    ang = pos.astype(F32)[:, None] * inv[None, :]
    cos, sin = jnp.cos(ang), jnp.sin(ang)
    return jnp.tile(cos, (1, 4)), jnp.tile(jnp.concatenate([-sin, sin], axis=1), (1, 2))


def _pick_tile(n, pref):
    t = min(n, pref)
    while n % t:
        t //= 2
    return t


def _per_seq_cols(a, bsz, t, width):
    f = a.shape[1]
    a = jnp.moveaxis(a[0].reshape(f, bsz, t), 1, 0)
    return jnp.pad(a, ((0, 0), (0, 0), (0, width - t)))


def _value_blocks(past_vt, new_vt, bsz, t, lpad):
    lead = new_vt.shape[:-2]
    new_b = jnp.moveaxis(new_vt.reshape(*lead, V_ROWS, bsz, t), -2, 0)
    full = jnp.concatenate([past_vt, new_b], axis=-1)
    full = jnp.pad(full, [(0, 0)] * (full.ndim - 1) + [(0, lpad - full.shape[-1])])
    full = full.reshape(bsz, *lead, V_ROWS, lpad // KV_BLOCK, KV_BLOCK)
    return jnp.moveaxis(full, -2, -3)


def _with_ones_rows(vt):
    ones = jnp.ones(vt.shape[:-2] + (1, vt.shape[-1]), vt.dtype)
    zeros = jnp.zeros(vt.shape[:-2] + (V_ROWS - HEAD_DIM - 1, vt.shape[-1]), vt.dtype)
    return jnp.concatenate([vt, ones, zeros], axis=-2)


def _mixers(inp, past, n_past, lw, bsz, t):
    aqt, iqt, kik, avt, bqt, bk, bvt, cu, sm, smt = inp
    n_keys = n_past + t
    n_sel = min(TOPK_MAX, n_keys // 4)
    lpad = -(-n_keys // (2 * KV_BLOCK)) * 2 * KV_BLOCK
    if past is None:
        kik_all, avt_all, bk_all, bvt_all, logf_all = kik, avt, bk, bvt, sm
        hist16 = jnp.zeros((bsz, 16, POOL_WIDTH), F32)
        tq_a, tq_b, t_pad = _pick_tile(t, 256), _pick_tile(t, 256), t
    else:
        pa, pb, plf, pc = past
        pk, pv, pik = (pa[:, :, j].astype(BF16) for j in range(3))

        def join_rows(p, new):
            full = jnp.concatenate([p, new.reshape(bsz, t, new.shape[-1])], axis=1)
            return jnp.pad(full, ((0, 0), (0, lpad - n_keys), (0, 0)))

        kik_all = join_rows(jnp.concatenate([pk, pik], axis=-1), kik)
        avt_all = _value_blocks(_with_ones_rows(jnp.swapaxes(pv, 1, 2)), avt[0, 0], bsz, t, lpad)
        bk_all = join_rows(pb[:, :, 0].astype(BF16).reshape(bsz, n_past, 384), bk)
        pvt = jnp.transpose(pb[:, :, 1].astype(BF16), (0, 2, 3, 1))
        bvt_all = _value_blocks(_with_ones_rows(pvt), bvt[0, :, 0], bsz, t, lpad)
        logf_all = join_rows(jnp.pad(plf, ((0, 0), (0, 0), (0, LANES - B_HEADS))), sm)
        hist16 = jnp.pad(pc, ((0, 0), (1, 0), (0, 0)))
        tq_a = tq_b = t_pad = LANES
        aqt, iqt, bqt, smt = (_per_seq_cols(a, bsz, t, t_pad) for a in (aqt, iqt, bqt, smt))

    oa = _dsa(aqt, iqt, smt, kik_all, avt_all, tq=tq_a, n_keys=n_keys, q_pos0=n_past, n_sel=n_sel)
    ck = _cum_logf(logf_all, KV_BLOCK)
    ob = _fox(bqt, bk_all, bvt_all, ck, tq=tq_b, n_keys=n_keys, q_pos0=n_past)
    cu = cu.reshape(bsz, t, POOL_WIDTH)
    oc = _pool(cu, hist16, lw["pool_bd"], lw["pool_scale"], tc=_pick_tile(t, 256), start_pos=n_past)
    return oa[:, :t], ob[:, :t], oc


def _layer(x, mod, past, n_past, pos_tab, lw, layer, final_g, per_token):
    bsz, t, d = x.shape
    sh1, sc1, g1, sh2, sc2, g2 = mod
    if per_token:
        xt = x.reshape(1, bsz * t, d)
        sh1, sc1, g1, sh2, sc2, g2 = (jnp.broadcast_to(m, (bsz, t, d)).reshape(1, bsz * t, d) for m in mod)
        cos, sin = (jnp.tile(a, (bsz, 1)) for a in pos_tab)
    else:
        xt = x
        cos, sin = pos_tab
    tm = KV_BLOCK
    (aqt, iqt, na, kik, avt, nb, bqt, bk, bvt, cu, sm, smt, gates) = _in_proj(
        xt, sc1, sh1, lw["norm_mix_g"], lw["w_in"], lw["bf_bias"], cos, sin, tm)
    oa, ob, oc = _mixers((aqt, iqt, kik, avt, bqt, bk, bvt, cu, sm, smt), past, n_past, lw, bsz, t)

    def flat(a):
        return a.reshape(xt.shape[0], xt.shape[1], a.shape[-1])

    router = (lw["router_w"], lw["router_b"]) if layer % 2 else None
    res = _merge(xt, flat(oa), flat(ob), flat(oc), gates, g1, sc2, sh2, lw["norm_ffn_g"],
                 lw["w_br_a"], lw["w_br_b"], lw["w_br_c"], lw["w_out"], router, tm)
    tmf = _pick_tile(xt.shape[1], 512)
    if layer % 2 == 0:
        x_mid, h2 = res
        x_new = _ffn(x_mid, h2, g2, lw["ffn_wg"], lw["ffn_wu"], lw["ffn_wd"], final_g, tmf)
    else:
        x_mid, h2, gw = res
        x_new = _moe(x_mid, h2, g2, gw, lw["moe_wg"], lw["moe_wu"], lw["moe_wd"], final_g, tmf)
    new_a = na.reshape(bsz, t, 3, HEAD_DIM)
    new_b = nb.reshape(bsz, t, 2, B_HEADS, HEAD_DIM)
    new_logf = sm.reshape(bsz, t, LANES)[:, :, :B_HEADS]
    new_pool = cu.reshape(bsz, t, POOL_WIDTH)[:, t - POOL_HIST:, :]
    return x_new.reshape(bsz, t, d), (new_a, new_b, new_logf, new_pool)


def kernel(x_prompt, x_sample, cache_a_kvi, cache_b_kv, cache_b_logf, state_c_pool, c_prompt, c_sample,
           ada_w, ada_b, norm_mix_g, w_in, b_forget, pool_w, pool_scale, w_br_a, w_br_b, w_br_c, w_out,
           norm_ffn_g, ffn_w_gate, ffn_w_up, ffn_w_down, moe_router_w, moe_router_b, moe_w_gate,
           moe_w_up, moe_w_down, final_norm_g):
    depth = ada_w.shape[0]
    bp, tp, d = x_prompt.shape
    bs, ts, _ = x_sample.shape
    n_past = cache_a_kvi.shape[2]
    assert tp % KV_BLOCK == 0 and (bs * ts) % KV_BLOCK == 0 and ts <= LANES

    rows = -(-(bp + bs) // 8) * 8
    c_all = jnp.pad(jnp.concatenate([c_prompt, c_sample], axis=0), ((0, rows - bp - bs), (0, 0)))
    mod_all = _ada(c_all, ada_w, ada_b)

    tab_p = _rope_tables(jnp.arange(tp))
    tab_s = _rope_tables(n_past + jnp.arange(ts))

    xp, xs = x_prompt, x_sample
    outs_p, outs_s = [], []
    for layer in range(depth):
        j = layer // 2
        w_l, bias_l = _in_weights(w_in[layer], b_forget[layer])
        pw = pool_w[layer]
        pool_bd = jnp.zeros((POOL_WIDTH, POOL_WIDTH), F32)
        for g in range(len(POOL_WINDOWS)):
            sl = slice(g * POOL_GROUP_DIM, (g + 1) * POOL_GROUP_DIM)
            pool_bd = pool_bd.at[sl, sl].set(pw[g])
        lw = dict(w_in=w_l, bf_bias=bias_l, norm_mix_g=norm_mix_g[layer], norm_ffn_g=norm_ffn_g[layer],
                  pool_bd=pool_bd.astype(BF16), pool_scale=pool_scale[layer],
                  w_br_a=w_br_a[layer].astype(BF16), w_br_b=w_br_b[layer].astype(BF16),
                  w_br_c=w_br_c[layer].astype(BF16), w_out=w_out[layer].astype(BF16))
        if layer % 2 == 0:
            lw.update(ffn_wg=ffn_w_gate[j].astype(BF16), ffn_wu=ffn_w_up[j].astype(BF16),
                      ffn_wd=ffn_w_down[j].astype(BF16))
        else:
            rw = jnp.pad(moe_router_w[j], ((0, 0), (0, LANES - N_EXPERTS)))
            rw_hi = rw.astype(BF16)
            lw.update(router_w=jnp.stack([rw_hi, (rw - rw_hi.astype(F32)).astype(BF16)]),
                      router_b=jnp.pad(moe_router_b[j], (0, LANES - N_EXPERTS)).reshape(1, LANES),
                      moe_wg=moe_w_gate[j].astype(BF16), moe_wu=moe_w_up[j].astype(BF16),
                      moe_wd=moe_w_down[j].astype(BF16))
        final_g = final_norm_g if layer == depth - 1 else None
        mod_p = [m[:, None, :] for m in jnp.split(mod_all[layer, :bp], 6, axis=-1)]
        mod_s = [m[:, None, :] for m in jnp.split(mod_all[layer, bp:bp + bs], 6, axis=-1)]
        xp, new_p = _layer(xp, mod_p, None, 0, tab_p, lw, layer, final_g, per_token=False)
        past = (cache_a_kvi[layer], cache_b_kv[layer], cache_b_logf[layer], state_c_pool[layer])
        xs, new_s = _layer(xs, mod_s, past, n_past, tab_s, lw, layer, final_g, per_token=True)
        outs_p.append(new_p)
        outs_s.append(new_s)

    def stack(outs, k):
        return jnp.stack([o[k] for o in outs])

    return (xp, xs,
            stack(outs_p, 0), stack(outs_p, 1), stack(outs_p, 2), stack(outs_p, 3),
            stack(outs_s, 0), stack(outs_s, 1), stack(outs_s, 2), stack(outs_s, 3))
```

```python
import functools

import jax
import jax.numpy as jnp
import numpy as np
from jax import lax
from jax.experimental import pallas as pl
from jax.experimental.pallas import tpu as pltpu

F32 = jnp.float32
BF16 = jnp.bfloat16
I32 = jnp.int32

D_MODEL = 1024
CHUNK = 64
HEAD_DIM = 64
HALF = HEAD_DIM // 2
ROPE_THETA = 10000.0
NORM_EPS = 1e-6
A_HEADS = 6
IDX_HEADS = 4
TOPK_MAX = 256
B_HEADS = 6
POOL_WINDOWS = (2, 4, 8, 16)
POOL_GROUP_DIM = 64
POOL_WIDTH = 256
POOL_HIST = 15
N_EXPERTS = 8
LANES = 128
SUBLANES = 8
LOG2E = 1.4426950408889634
QK_SCALE = HEAD_DIM ** -0.5 * LOG2E
KV_BLOCK = 256
V_ROWS = HEAD_DIM + 16
VMEM_LIMIT = 56 * 1024 * 1024
NEG_INF = float("-inf")
KEY_DT = jnp.bfloat16

C_AQ, C_IQ, C_A, C_B, C_CU, C_SM, C_GATE, C_END = 0, 384, 640, 896, 2048, 2304, 2432, 5504


def _cparams(sem):
    return pltpu.CompilerParams(dimension_semantics=sem, vmem_limit_bytes=VMEM_LIMIT)


def _const_spec(shape):
    nd = len(shape)
    return pl.BlockSpec(shape, lambda *_: (0,) * nd, pipeline_mode=pl.Buffered(1))


def _lane_iota(shape):
    return lax.broadcasted_iota(I32, shape, len(shape) - 1)


def _row_iota(shape):
    return lax.broadcasted_iota(I32, shape, len(shape) - 2)


def _ada_kernel(c_ref, w_ref, b_ref, o_ref):
    c = c_ref[...]
    s = c * jax.nn.sigmoid(c)
    o_ref[...] = jnp.dot(s, w_ref[...], preferred_element_type=F32,
                         precision=lax.Precision.HIGHEST) + b_ref[...]


def _ada(c_all, ada_w, ada_b):
    depth, d, n = ada_w.shape
    rows = c_all.shape[0]
    tn = 1536
    return pl.pallas_call(
        _ada_kernel,
        grid=(depth, n // tn),
        in_specs=[pl.BlockSpec((rows, d), lambda l, j: (0, 0)),
                  pl.BlockSpec((None, d, tn), lambda l, j: (l, 0, j)),
                  pl.BlockSpec((None, 1, tn), lambda l, j: (l, 0, j))],
        out_specs=pl.BlockSpec((None, rows, tn), lambda l, j: (l, 0, j)),
        out_shape=jax.ShapeDtypeStruct((depth, rows, n), F32),
        compiler_params=_cparams(("arbitrary", "arbitrary")),
        name="ada_mod",
    )(c_all, ada_w, ada_b.reshape(depth, 1, n))


def _in_kernel(x_ref, sc_ref, sh_ref, g_ref, w_ref, bf_ref, cos_ref, sin_ref,
               aqt_ref, iqt_ref, nat_ref, kik_ref, avt_ref, nbt_ref, bqt_ref, bk_ref, bvt_ref,
               cu_ref, sm_ref, smt_ref, gate_ref):
    x = x_ref[...]
    ms = jnp.mean(x * x, axis=-1, keepdims=True)
    y = x * lax.rsqrt(ms + NORM_EPS) * g_ref[...]
    h = (y * (1.0 + sc_ref[...]) + sh_ref[...]).astype(BF16)
    tm = x.shape[0]

    def mm(a, b):
        return jnp.dot(h, w_ref[:, a:b], preferred_element_type=F32)

    cos = cos_ref[...]
    sin = sin_ref[...]
    lane = _lane_iota((tm, LANES))
    low = lane < HEAD_DIM
    first_half = (lane & HALF) == 0

    def rope(z):
        swapped = jnp.where(first_half, pltpu.roll(z, LANES - HALF, 1), pltpu.roll(z, HALF, 1))
        return z * cos + swapped * sin

    zeros64 = jnp.zeros((HEAD_DIM, tm), BF16)
    ones_rows = jnp.where(_row_iota((V_ROWS - HEAD_DIM, tm)) == 0, 1.0, 0.0).astype(BF16)

    def put_heads(ref, zt, p, slot_even, slot_odd):
        for hh, slot in ((0, slot_even), (1, slot_odd)):
            base = (2 * p + hh) * LANES
            ref[base + slot * HEAD_DIM:base + (slot + 1) * HEAD_DIM, :] = zt[hh * HEAD_DIM:(hh + 1) * HEAD_DIM]
            ref[base + (1 - slot) * HEAD_DIM:base + (2 - slot) * HEAD_DIM, :] = zeros64

    z = mm(C_AQ, C_IQ)
    for p in range(3):
        zt = (rope(z[:, p * LANES:(p + 1) * LANES]) * QK_SCALE).T.astype(BF16)
        put_heads(aqt_ref, zt, p, 0, 0)
    z = mm(C_IQ, C_A)
    for p in range(2):
        zt = rope(z[:, p * LANES:(p + 1) * LANES]).T.astype(BF16)
        put_heads(iqt_ref, zt, p, 1, 1)

    z = mm(C_A, C_B)
    kv = z[:, :LANES]
    r0 = jnp.where(low, rope(kv), kv)
    r1 = rope(z[:, LANES:])
    r0t = r0.T
    nat_ref[0:LANES, :] = r0t
    nat_ref[LANES:, :] = r1.T[0:HEAD_DIM, :]
    kik_ref[...] = jnp.where(low, r0, pltpu.roll(r1, HEAD_DIM, 1)).astype(BF16)
    avt_ref[0:HEAD_DIM, :] = r0t[HEAD_DIM:, :].astype(BF16)
    avt_ref[HEAD_DIM:, :] = ones_rows

    z = mm(C_B, C_CU)
    bk_ref[...] = z[:, 384:768].astype(BF16)
    for p in range(3):
        zt = (z[:, p * LANES:(p + 1) * LANES] * QK_SCALE).T.astype(BF16)
        put_heads(bqt_ref, zt, p, 0, 1)
        nbt_ref[p * LANES:(p + 1) * LANES, :] = z[:, 384 + p * LANES:384 + (p + 1) * LANES].T
        vt = z[:, 768 + p * LANES:768 + (p + 1) * LANES].T
        nbt_ref[384 + p * LANES:384 + (p + 1) * LANES, :] = vt
        vt = vt.astype(BF16)
        for hh in range(2):
            bvt_ref[2 * p + hh, 0:HEAD_DIM, :] = vt[hh * HEAD_DIM:(hh + 1) * HEAD_DIM]
            bvt_ref[2 * p + hh, HEAD_DIM:, :] = ones_rows

    cu_ref[...] = mm(C_CU, C_SM)

    z = mm(C_SM, C_GATE)
    t = z + bf_ref[...]
    logf = jnp.minimum(t, 0.0) - jnp.log1p(jnp.exp(-jnp.abs(t)))
    sm = jnp.where(lane < B_HEADS, logf, z)
    sm_ref[...] = sm
    smt_ref[...] = sm.T[0:16, :]

    for c in range(3):
        gate_ref[:, c * D_MODEL:(c + 1) * D_MODEL] = jax.nn.sigmoid(
            mm(C_GATE + c * D_MODEL, C_GATE + (c + 1) * D_MODEL))


def _in_weights(w_in_l, b_forget_l):
    d = w_in_l.shape[0]
    sizes = (384, 64, 64, 256, 4, 64, 384, 384, 384, 6, 256, 3072)
    o = np.concatenate([[0], np.cumsum(sizes)])
    w_in_l = w_in_l.astype(BF16)
    cols = [w_in_l[:, o[0]:o[1]],
            w_in_l[:, o[3]:o[4]],
            w_in_l[:, o[1]:o[3]], w_in_l[:, o[5]:o[6]], jnp.zeros((d, 64), BF16),
            w_in_l[:, o[6]:o[9]],
            w_in_l[:, o[10]:o[11]],
            w_in_l[:, o[9]:o[10]], jnp.zeros((d, 2), BF16), w_in_l[:, o[4]:o[5]], jnp.zeros((d, LANES - 12), BF16),
            w_in_l[:, o[11]:o[12]]]
    w = jnp.concatenate(cols, axis=1)
    assert w.shape[1] == C_END, w.shape
    bias = jnp.concatenate([b_forget_l, jnp.zeros((LANES - B_HEADS,), F32)]).reshape(1, LANES)
    return w, bias


def _in_proj(x, sc, sh, g, w, bias, cos, sin, tm):
    bsz, t, d = x.shape
    mrows = sc.shape[1]
    mblk = 1 if mrows == 1 else tm
    mod_spec = pl.BlockSpec((None, mblk, d), (lambda b, i: (b, 0, 0)) if mrows == 1 else (lambda b, i: (b, i, 0)))
    nblk = t // tm

    def rows(n, dt):
        return pl.BlockSpec((None, tm, n), lambda b, i: (b, i, 0)), jax.ShapeDtypeStruct((bsz, t, n), dt)

    def cols(n, dt):
        return pl.BlockSpec((None, n, tm), lambda b, i: (b, 0, i)), jax.ShapeDtypeStruct((bsz, n, t), dt)

    outs = [cols(A_HEADS * LANES, BF16), cols(IDX_HEADS * LANES, BF16), cols(192, F32), rows(LANES, BF16),
            (pl.BlockSpec((None, None, V_ROWS, tm), lambda b, i: (b, i, 0, 0)),
             jax.ShapeDtypeStruct((bsz, nblk, V_ROWS, tm), BF16)),
            cols(768, F32), cols(B_HEADS * LANES, BF16), rows(384, BF16),
            (pl.BlockSpec((None, B_HEADS, None, V_ROWS, tm), lambda b, i: (b, 0, i, 0, 0)),
             jax.ShapeDtypeStruct((bsz, B_HEADS, nblk, V_ROWS, tm), BF16)),
            rows(256, F32), rows(LANES, F32), cols(16, F32), rows(3 * D_MODEL, F32)]
    return pl.pallas_call(
        _in_kernel,
        grid=(bsz, nblk),
        in_specs=[pl.BlockSpec((None, tm, d), lambda b, i: (b, i, 0)), mod_spec, mod_spec,
                  _const_spec((1, d)), _const_spec(w.shape), _const_spec((1, LANES)),
                  pl.BlockSpec((tm, LANES), lambda b, i: (i, 0)),
                  pl.BlockSpec((tm, LANES), lambda b, i: (i, 0))],
        out_specs=[o[0] for o in outs],
        out_shape=[o[1] for o in outs],
        compiler_params=_cparams(("parallel", "parallel")),
        name="in_proj",
    )(x, sc, sh, g.reshape(1, d), w, bias, cos, sin)


def _cum_kernel(x_ref, o_ref, carry_ref):
    @pl.when(pl.program_id(1) == 0)
    def _():
        carry_ref[...] = jnp.zeros_like(carry_ref)

    tc = x_ref.shape[0]
    tri = jnp.where(_lane_iota((tc, tc)) <= _row_iota((tc, tc)), 1.0, 0.0)
    cum = jnp.dot(tri, x_ref[...], preferred_element_type=F32, precision=lax.Precision.HIGHEST) + carry_ref[0:1, :]
    for hd in range(B_HEADS):
        o_ref[hd] = jnp.broadcast_to(cum[:, hd:hd + 1] * LOG2E, (tc, LANES))
    carry_ref[...] = jnp.broadcast_to(cum[tc - 1:tc, :], carry_ref.shape)


def _cum_logf(x, tc):
    bsz, t, n = x.shape
    return pl.pallas_call(
        _cum_kernel,
        grid=(bsz, t // tc),
        in_specs=[pl.BlockSpec((None, tc, n), lambda b, i: (b, i, 0))],
        out_specs=pl.BlockSpec((None, B_HEADS, tc, LANES), lambda b, i: (b, 0, i, 0)),
        out_shape=jax.ShapeDtypeStruct((bsz, B_HEADS, t, LANES), F32),
        scratch_shapes=[pltpu.VMEM((SUBLANES, LANES), F32)],
        compiler_params=_cparams(("parallel", "arbitrary")),
        name="logf_cumsum",
    )(x)


def _sum_keys(x):
    part = x.reshape(x.shape[0] // SUBLANES, SUBLANES, x.shape[1]).sum(axis=0)
    return jnp.sum(part, axis=0, keepdims=True)


def _max_keys(x):
    part = x.reshape(x.shape[0] // SUBLANES, SUBLANES, x.shape[1]).max(axis=0)
    return jnp.max(part, axis=0, keepdims=True)


def _dsa_kernel(aqt_ref, iqt_ref, smt_ref, kik_ref, avt_ref, o_ref, key_ref,
                *, tq, tk, n_keys, q_pos0, n_sel):
    i = pl.program_id(1)
    pos_first = q_pos0 + i * tq
    last_chunk = (pos_first + tq - 1) // CHUNK
    n_adm = jnp.minimum((last_chunk + 1) * CHUNK, n_keys)
    nkb = (n_adm + tk - 1) // tk
    n_pairs = (nkb + 1) // 2

    q_pos = pos_first + _lane_iota((1, tq))
    q_lim = jnp.minimum((q_pos // CHUNK + 1) * CHUNK, n_keys)
    key_row = _row_iota((tk, tq))

    def keys(kb):
        return kik_ref[pl.ds(pl.multiple_of(kb * tk, tk), tk), :]

    iq4 = jnp.concatenate([iqt_ref[hd * LANES:(hd + 1) * LANES, :] for hd in range(IDX_HEADS)], axis=1)
    smt = smt_ref[...]
    w_rows = [smt[8 + hd:9 + hd, :] for hd in range(IDX_HEADS)]

    def score_body(kb, carry):
        s4 = jnp.dot(keys(kb), iq4, preferred_element_type=F32)
        score = w_rows[0] * jnp.maximum(s4[:, 0:tq], 0.0)
        for hd in range(1, IDX_HEADS):
            score = score + w_rows[hd] * jnp.maximum(s4[:, hd * tq:(hd + 1) * tq], 0.0)
        score = jnp.where(key_row < q_lim - kb * tk, score, NEG_INF)
        key_ref[kb] = score.astype(KEY_DT)
        return carry

    lax.fori_loop(0, nkb, score_body, 0)

    @pl.when(nkb % 2 == 1)
    def _():
        key_ref[nkb] = jnp.full((tk, tq), NEG_INF, KEY_DT)

    one, zero = jnp.ones((), KEY_DT), jnp.zeros((), KEY_DT)
    packed_rows = 2 * SUBLANES

    def count(cand, strict):
        def hits(blk):
            h = jnp.where((blk > cand) if strict else (blk >= cand), one, zero)
            parts = [h[r * packed_rows:(r + 1) * packed_rows] for r in range(tk // packed_rows)]
            while len(parts) > 1:
                parts = [a + b for a, b in zip(parts[::2], parts[1::2])]
            return parts[0].astype(F32)

        def body(j, acc):
            return acc + hits(key_ref[2 * j]) + hits(key_ref[2 * j + 1])

        acc = lax.fori_loop(0, n_pairs, body, jnp.zeros((packed_rows, tq), F32))
        return jnp.sum(acc, axis=0, keepdims=True)

    def pattern_value(u):
        bits = jnp.where(u >= 0x8000, u & 0x7FFF, (~u) & 0xFFFF)
        return lax.bitcast_convert_type(lax.shift_left(bits, 16), F32).astype(KEY_DT)

    def bit_body(b, u):
        cand_u = u | lax.shift_left(jnp.int32(1), 15 - b)
        cnt = count(pattern_value(cand_u), False)
        return jnp.where(cnt >= n_sel, cand_u, u)

    u_thr = lax.fori_loop(0, 16, bit_body, jnp.zeros((1, tq), I32))
    thr_k = pattern_value(jnp.maximum(u_thr, 0x007F))
    need = jnp.maximum(n_sel - count(thr_k, True), 0.0)
    thr = thr_k.astype(F32)

    aq6 = jnp.concatenate([aqt_ref[hd * LANES:(hd + 1) * LANES, :] for hd in range(A_HEADS)], axis=1)

    half = tk // 2
    lower = jnp.where(_lane_iota((half, half)) <= _row_iota((half, half)), 1.0, 0.0).astype(BF16)

    def attend():
        def body(kb, carry):
            eq_seen, ms, accs = carry
            blk = key_ref[kb].astype(F32)
            eq = blk == thr
            eq_f = jnp.where(eq, 1.0, 0.0)
            prefs = []
            for e in (eq_f[:half], eq_f[half:]):
                prefs.append(jnp.dot(lower, e.astype(BF16), preferred_element_type=F32) + eq_seen)
                eq_seen = eq_seen + _sum_keys(e)
            slack = jnp.where(blk >= thr, need - jnp.where(eq, jnp.concatenate(prefs, axis=0), 0.0), -1.0)
            bias = jnp.where(slack >= 0.0, jnp.where(jnp.abs(blk) < jnp.inf, 0.0, NEG_INF), NEG_INF)
            logits = jnp.dot(keys(kb), aq6, preferred_element_type=F32)
            vts = (avt_ref[2 * kb], avt_ref[2 * kb + 1])
            new_ms, new_accs = [], []
            for p in range(A_HEADS // 2):
                ps, alphas = [], []
                for hd in (2 * p, 2 * p + 1):
                    lg = logits[:, hd * tq:(hd + 1) * tq] + bias
                    m_old = ms[hd]
                    m_new = jnp.maximum(m_old, _max_keys(lg))
                    m_safe = jnp.where(m_new == NEG_INF, 0.0, m_new)
                    ps.append(jnp.exp2(lg - m_safe).astype(BF16))
                    alphas.append(jnp.exp2(m_old - m_safe))
                    new_ms.append(m_new)
                p2 = jnp.concatenate(ps, axis=1)
                pv = (jnp.dot(vts[0], p2[:tk // 2], preferred_element_type=F32)
                      + jnp.dot(vts[1], p2[tk // 2:], preferred_element_type=F32))
                new_accs.append(jnp.concatenate(alphas, axis=1) * accs[p] + pv)
            return eq_seen, tuple(new_ms), tuple(new_accs)

        init = (jnp.zeros((1, tq), F32),
                tuple(jnp.full((1, tq), NEG_INF, F32) for _ in range(A_HEADS)),
                tuple(jnp.zeros((V_ROWS, 2 * tq), F32) for _ in range(A_HEADS // 2)))
        return lax.fori_loop(0, nkb, body, init)[2]

    accs = attend()

    outs = []
    for p in range(A_HEADS // 2):
        o2 = accs[p][0:HEAD_DIM] / accs[p][HEAD_DIM:HEAD_DIM + 1]
        outs += [o2[:, 0:tq], o2[:, tq:2 * tq]]
    o_ref[...] = jnp.concatenate(outs, axis=0).T.astype(o_ref.dtype)


def _dsa(aqt, iqt, smt, kik, avt, *, tq, n_keys, q_pos0, n_sel):
    bsz, _, t_q = aqt.shape
    _, nblk, _, tkv = avt.shape
    assert nblk % 2 == 0
    tk = 2 * tkv
    kern = functools.partial(_dsa_kernel, tq=tq, tk=tk, n_keys=n_keys, q_pos0=q_pos0, n_sel=n_sel)
    return pl.pallas_call(
        kern,
        grid=(bsz, t_q // tq),
        in_specs=[pl.BlockSpec((None, A_HEADS * LANES, tq), lambda b, i: (b, 0, i)),
                  pl.BlockSpec((None, IDX_HEADS * LANES, tq), lambda b, i: (b, 0, i)),
                  pl.BlockSpec((None, 16, tq), lambda b, i: (b, 0, i)),
                  pl.BlockSpec((None, nblk * tkv, LANES), lambda b, i: (b, 0, 0)),
                  pl.BlockSpec((None, nblk, V_ROWS, tkv), lambda b, i: (b, 0, 0, 0))],
        out_specs=pl.BlockSpec((None, tq, 384), lambda b, i: (b, i, 0)),
        out_shape=jax.ShapeDtypeStruct((bsz, t_q, 384), BF16),
        scratch_shapes=[pltpu.VMEM((2 * ((nblk // 2 + 1) // 2), tk, tq), KEY_DT)],
        compiler_params=_cparams(("parallel", "arbitrary")),
        name="dsa_attention",
    )(aqt, iqt, smt, kik, avt)


def _fox_kernel(qt_ref, k_ref, vt_ref, ck_ref, o_ref, *, tq, tk, n_keys, q_pos0):
    i = pl.program_id(1)
    pos_first = q_pos0 + i * tq
    n_full = pos_first // tk
    nkb = (jnp.minimum(pos_first + tq, n_keys) + tk - 1) // tk
    q_pos = pos_first + _lane_iota((1, tq))
    key_row = _row_iota((tk, tq))
    qts = [qt_ref[hd * LANES:(hd + 1) * LANES, :] for hd in range(B_HEADS)]

    def step(kbs, state, masked):
        logits = []
        for hd in range(B_HEADS):
            for kb in kbs:
                rows = pl.ds(pl.multiple_of(kb * tk, tk), tk)
                ck = ck_ref[hd, rows, :]
                kblk = k_ref[rows, (hd // 2) * LANES:(hd // 2 + 1) * LANES]
                lg = (jnp.dot(kblk, qts[hd], preferred_element_type=F32)
                      - jnp.concatenate([ck] * (tq // LANES), axis=1))
                if masked:
                    lg = jnp.where(key_row <= q_pos - kb * tk, lg, NEG_INF)
                logits.append(lg)
        new = []
        for hd in range(B_HEADS):
            m_old, acc = state[hd]
            lgs = logits[hd * len(kbs):(hd + 1) * len(kbs)]
            m_new = m_old
            for lg in lgs:
                m_new = jnp.maximum(m_new, _max_keys(lg))
            m_safe = jnp.where(m_new == NEG_INF, 0.0, m_new) if masked else m_new
            acc = jnp.exp2(m_old - m_safe) * acc
            for kb, lg in zip(kbs, lgs):
                p = jnp.exp2(lg - m_safe).astype(BF16)
                acc = acc + jnp.dot(vt_ref[hd, kb], p, preferred_element_type=F32)
            new.append((m_new, acc))
        return tuple(new)

    init = tuple((jnp.full((1, tq), NEG_INF, F32), jnp.zeros((V_ROWS, tq), F32)) for _ in range(B_HEADS))
    state = lax.fori_loop(0, n_full // 2, lambda j, st: step((2 * j, 2 * j + 1), st, False), init)
    state = lax.fori_loop(2 * (n_full // 2), nkb, lambda kb, st: step((kb,), st, True), state)
    outs = [acc[0:HEAD_DIM] / acc[HEAD_DIM:HEAD_DIM + 1] for _, acc in state]
    o_ref[...] = jnp.concatenate(outs, axis=0).T.astype(o_ref.dtype)


def _fox(qt, k, vt, ck, *, tq, n_keys, q_pos0):
    bsz, _, t_q = qt.shape
    _, _, nblk, _, tk = vt.shape
    lpad = nblk * tk
    kern = functools.partial(_fox_kernel, tq=tq, tk=tk, n_keys=n_keys, q_pos0=q_pos0)
    return pl.pallas_call(
        kern,
        grid=(bsz, t_q // tq),
        in_specs=[pl.BlockSpec((None, B_HEADS * LANES, tq), lambda b, i: (b, 0, i)),
                  pl.BlockSpec((None, lpad, 384), lambda b, i: (b, 0, 0)),
                  pl.BlockSpec((None, B_HEADS, nblk, V_ROWS, tk), lambda b, i: (b, 0, 0, 0, 0)),
                  pl.BlockSpec((None, B_HEADS, lpad, LANES), lambda b, i: (b, 0, 0, 0))],
        out_specs=pl.BlockSpec((None, tq, 384), lambda b, i: (b, i, 0)),
        out_shape=jax.ShapeDtypeStruct((bsz, t_q, 384), BF16),
        compiler_params=_cparams(("parallel", "arbitrary")),
        name="fox_attention",
    )(qt, k, vt, ck)


def _pool_kernel(cur_ref, prev_ref, hist_ref, w_ref, s_ref, o_ref, ext, *, tc, start_pos):
    i = pl.program_id(1)
    cur = cur_ref[...]
    ext[0:16, :] = jnp.where(i == 0, hist_ref[...], prev_ref[tc - 16:, :])
    ext[16:, :] = cur
    pos = start_pos + i * tc + lax.broadcasted_iota(I32, (tc, POOL_WIDTH), 0)
    lane = _lane_iota((tc, POOL_WIDTH))
    run = cur
    pooled = jnp.zeros_like(cur)
    k = 1
    for g, w in enumerate(POOL_WINDOWS):
        while k < w:
            run = run + ext[16 - k:16 - k + tc, :]
            k += 1
        cnt = jnp.minimum(pos + 1, w).astype(F32)
        in_group = (lane >= g * POOL_GROUP_DIM) & (lane < (g + 1) * POOL_GROUP_DIM)
        pooled = jnp.where(in_group, run / cnt, pooled)
    z = (pooled - cur).astype(BF16)
    o_ref[...] = (jnp.dot(z, w_ref[...], preferred_element_type=F32) * s_ref[...]).astype(o_ref.dtype)


def _pool(cu, hist16, w_bd, scale, *, tc, start_pos):
    bsz, t, n = cu.shape
    kern = functools.partial(_pool_kernel, tc=tc, start_pos=start_pos)
    return pl.pallas_call(
        kern,
        grid=(bsz, t // tc),
        in_specs=[pl.BlockSpec((None, tc, n), lambda b, i: (b, i, 0)),
                  pl.BlockSpec((None, tc, n), lambda b, i: (b, jnp.maximum(i - 1, 0), 0)),
                  pl.BlockSpec((None, 16, n), lambda b, i: (b, 0, 0)),
                  _const_spec((n, n)), _const_spec((1, n))],
        out_specs=pl.BlockSpec((None, tc, n), lambda b, i: (b, i, 0)),
        out_shape=jax.ShapeDtypeStruct((bsz, t, n), BF16),
        scratch_shapes=[pltpu.VMEM((16 + tc, n), F32)],
        compiler_params=_cparams(("parallel", "arbitrary")),
        name="pool_mixer",
    )(cu, cu, hist16, w_bd, scale.reshape(1, n))


def _route(logits):
    lane = _lane_iota(logits.shape).astype(F32)
    lg = jnp.where(lane < N_EXPERTS, logits, NEG_INF)
    m1 = jnp.max(lg, axis=1, keepdims=True)
    i1 = jnp.min(jnp.where(lg == m1, lane, float(LANES)), axis=1, keepdims=True)
    hot1 = lane == i1
    lg2 = jnp.where(hot1, NEG_INF, lg)
    m2 = jnp.max(lg2, axis=1, keepdims=True)
    i2 = jnp.min(jnp.where(lg2 == m2, lane, float(LANES)), axis=1, keepdims=True)
    hot2 = lane == i2
    e2 = jnp.exp(m2 - m1)
    den = 1.0 + e2
    return jnp.where(hot1, 1.0 / den, 0.0) + jnp.where(hot2, e2 / den, 0.0)


def _merge_kernel(x_ref, oa_ref, ob_ref, oc_ref, gate_ref, g1_ref, sc2_ref, sh2_ref, g_ref,
                  wa_ref, wb_ref, wc_ref, wo_ref, *rest, moe):
    if moe:
        rw_ref, rb_ref, xo_ref, h_ref, gw_ref = rest
    else:
        xo_ref, h_ref = rest
    d = D_MODEL
    merged = (gate_ref[:, 0:d] * jnp.dot(oa_ref[...], wa_ref[...], preferred_element_type=F32)
              + gate_ref[:, d:2 * d] * jnp.dot(ob_ref[...], wb_ref[...], preferred_element_type=F32)
              + gate_ref[:, 2 * d:3 * d] * jnp.dot(oc_ref[...], wc_ref[...], preferred_element_type=F32))
    x = x_ref[...] + g1_ref[...] * jnp.dot(merged.astype(BF16), wo_ref[...], preferred_element_type=F32)
    xo_ref[...] = x
    ms = jnp.mean(x * x, axis=-1, keepdims=True)
    y = x * lax.rsqrt(ms + NORM_EPS) * g_ref[...]
    h = y * (1.0 + sc2_ref[...]) + sh2_ref[...]
    h_ref[...] = h.astype(BF16)
    if moe:
        h_hi = h.astype(BF16)
        h_lo = (h - h_hi.astype(F32)).astype(BF16)
        logits = (jnp.dot(h_hi, rw_ref[0], preferred_element_type=F32)
                  + jnp.dot(h_lo, rw_ref[0], preferred_element_type=F32)
                  + jnp.dot(h_hi, rw_ref[1], preferred_element_type=F32)) + rb_ref[...]
        gw_ref[...] = _route(logits)


def _merge(x, oa, ob, oc, gates, g1, sc2, sh2, g, wa, wb, wc, wo, router, tm):
    bsz, t, d = x.shape
    mrows = g1.shape[1]
    mblk = 1 if mrows == 1 else tm
    mod_spec = pl.BlockSpec((None, mblk, d), (lambda b, i: (b, 0, 0)) if mrows == 1 else (lambda b, i: (b, i, 0)))

    def tok(n):
        return pl.BlockSpec((None, tm, n), lambda b, i: (b, i, 0))

    in_specs = [tok(d), tok(384), tok(384), tok(256), tok(3 * d), mod_spec, mod_spec, mod_spec,
                _const_spec((1, d)), _const_spec(wa.shape), _const_spec(wb.shape), _const_spec(wc.shape),
                _const_spec(wo.shape)]
    args = [x, oa, ob, oc, gates, g1, sc2, sh2, g.reshape(1, d), wa, wb, wc, wo]
    out_specs = [tok(d), tok(d)]
    out_shape = [jax.ShapeDtypeStruct((bsz, t, d), F32), jax.ShapeDtypeStruct((bsz, t, d), BF16)]
    if router is not None:
        rw, rb = router
        in_specs += [_const_spec(rw.shape), _const_spec(rb.shape)]
        args += [rw, rb]
        out_specs.append(tok(LANES))
        out_shape.append(jax.ShapeDtypeStruct((bsz, t, LANES), F32))
    return pl.pallas_call(
        functools.partial(_merge_kernel, moe=router is not None),
        grid=(bsz, t // tm),
        in_specs=in_specs, out_specs=out_specs, out_shape=out_shape,
        compiler_params=_cparams(("parallel", "parallel")),
        name="merge_out",
    )(*args)


def _final_norm(x, gain):
    ms = jnp.mean(x * x, axis=-1, keepdims=True)
    return x * lax.rsqrt(ms + NORM_EPS) * gain


def _ffn_kernel(x_ref, h_ref, g2_ref, wg_ref, wu_ref, wd_ref, *rest, n_chunks, final):
    if final:
        fg_ref, o_ref = rest
    else:
        (o_ref,) = rest
    h = h_ref[...]
    tf = wg_ref.shape[1] // n_chunks
    acc = jnp.zeros(x_ref.shape, F32)
    for c in range(n_chunks):
        gt = jnp.dot(h, wg_ref[:, c * tf:(c + 1) * tf], preferred_element_type=F32)
        up = jnp.dot(h, wu_ref[:, c * tf:(c + 1) * tf], preferred_element_type=F32)
        act = (gt * jax.nn.sigmoid(gt) * up).astype(BF16)
        acc = acc + jnp.dot(act, wd_ref[c * tf:(c + 1) * tf, :], preferred_element_type=F32)
    x = x_ref[...] + g2_ref[...] * acc
    o_ref[...] = _final_norm(x, fg_ref[...]) if final else x


def _ffn(x, h, g2, wg, wu, wd, final_g, tm):
    bsz, t, d = x.shape
    mrows = g2.shape[1]
    mblk = 1 if mrows == 1 else tm
    mod_spec = pl.BlockSpec((None, mblk, d), (lambda b, i: (b, 0, 0)) if mrows == 1 else (lambda b, i: (b, i, 0)))
    tok = pl.BlockSpec((None, tm, d), lambda b, i: (b, i, 0))
    in_specs = [tok, tok, mod_spec, _const_spec(wg.shape), _const_spec(wu.shape), _const_spec(wd.shape)]
    args = [x, h, g2, wg, wu, wd]
    if final_g is not None:
        in_specs.append(_const_spec((1, d)))
        args.append(final_g.reshape(1, d))
    return pl.pallas_call(
        functools.partial(_ffn_kernel, n_chunks=2, final=final_g is not None),
        grid=(bsz, t // tm),
        in_specs=in_specs, out_specs=tok,
        out_shape=jax.ShapeDtypeStruct((bsz, t, d), F32),
        compiler_params=_cparams(("parallel", "parallel")),
        name="ffn_dense",
    )(*args)


def _moe_kernel(x_ref, h_ref, g2_ref, gw_ref, wg_ref, wu_ref, wd_ref, *rest, final):
    if final:
        fg_ref, o_ref, acc_ref = rest
    else:
        o_ref, acc_ref = rest
    e = pl.program_id(2)

    @pl.when(e == 0)
    def _():
        acc_ref[...] = jnp.zeros_like(acc_ref)

    h = h_ref[...]
    gt = jnp.dot(h, wg_ref[...], preferred_element_type=F32)
    up = jnp.dot(h, wu_ref[...], preferred_element_type=F32)
    act = (gt * jax.nn.sigmoid(gt) * up).astype(BF16)
    y = jnp.dot(act, wd_ref[...], preferred_element_type=F32)
    gw = gw_ref[...]
    ge = jnp.sum(jnp.where(_lane_iota(gw.shape) == e, gw, 0.0), axis=1, keepdims=True)
    acc_ref[...] += ge * y

    @pl.when(e == pl.num_programs(2) - 1)
    def _():
        x = x_ref[...] + g2_ref[...] * acc_ref[...]
        o_ref[...] = _final_norm(x, fg_ref[...]) if final else x


def _moe(x, h, g2, gw, wg, wu, wd, final_g, tm):
    bsz, t, d = x.shape
    n_e, _, dff = wg.shape
    mrows = g2.shape[1]
    mblk = 1 if mrows == 1 else tm
    mod_spec = pl.BlockSpec((None, mblk, d), (lambda b, i, e: (b, 0, 0)) if mrows == 1 else (lambda b, i, e: (b, i, 0)))
    tok = pl.BlockSpec((None, tm, d), lambda b, i, e: (b, i, 0))
    in_specs = [tok, tok, mod_spec, pl.BlockSpec((None, tm, LANES), lambda b, i, e: (b, i, 0)),
                pl.BlockSpec((None, d, dff), lambda b, i, e: (e, 0, 0)),
                pl.BlockSpec((None, d, dff), lambda b, i, e: (e, 0, 0)),
                pl.BlockSpec((None, dff, d), lambda b, i, e: (e, 0, 0))]
    args = [x, h, g2, gw, wg, wu, wd]
    if final_g is not None:
        in_specs.append(pl.BlockSpec((1, d), lambda b, i, e: (0, 0)))
        args.append(final_g.reshape(1, d))
    return pl.pallas_call(
        functools.partial(_moe_kernel, final=final_g is not None),
        grid=(bsz, t // tm, n_e),
        in_specs=in_specs, out_specs=tok,
        out_shape=jax.ShapeDtypeStruct((bsz, t, d), F32),
        scratch_shapes=[pltpu.VMEM((tm, d), F32)],
        compiler_params=_cparams(("parallel", "parallel", "arbitrary")),
        name="moe_dense",
    )(*args)


def _rope_tables(pos):
    inv = ROPE_THETA ** (-jnp.arange(HALF, dtype=F32) / HALF)
    ang = pos.astype(F32)[:, None] * inv[None, :]
    cos, sin = jnp.cos(ang), jnp.sin(ang)
    return jnp.tile(cos, (1, 4)), jnp.tile(jnp.concatenate([-sin, sin], axis=1), (1, 2))


def _pick_tile(n, pref):
    t = min(n, pref)
    while n % t:
        t //= 2
    return t


def _per_seq_cols(a, bsz, t, width):
    f = a.shape[1]
    a = jnp.moveaxis(a[0].reshape(f, bsz, t), 1, 0)
    return jnp.pad(a, ((0, 0), (0, 0), (0, width - t)))


def _value_blocks(past_vt, new_vt, bsz, t, lpad):
    lead = new_vt.shape[:-2]
    new_b = jnp.moveaxis(new_vt.reshape(*lead, V_ROWS, bsz, t), -2, 0)
    full = jnp.concatenate([past_vt, new_b], axis=-1)
    full = jnp.pad(full, [(0, 0)] * (full.ndim - 1) + [(0, lpad - full.shape[-1])])
    full = full.reshape(bsz, *lead, V_ROWS, lpad // KV_BLOCK, KV_BLOCK)
    return jnp.moveaxis(full, -2, -3)


def _with_ones_rows(vt):
    ones = jnp.ones(vt.shape[:-2] + (1, vt.shape[-1]), vt.dtype)
    zeros = jnp.zeros(vt.shape[:-2] + (V_ROWS - HEAD_DIM - 1, vt.shape[-1]), vt.dtype)
    return jnp.concatenate([vt, ones, zeros], axis=-2)


def _mixers(inp, past, n_past, lw, bsz, t):
    aqt, iqt, kik, avt, bqt, bk, bvt, cu, sm, smt = inp
    n_keys = n_past + t
    n_sel = min(TOPK_MAX, n_keys // 4)
    lpad = -(-n_keys // (2 * KV_BLOCK)) * 2 * KV_BLOCK
    if past is None:
        kik_all, avt_all, bk_all, bvt_all, logf_all = kik, avt, bk, bvt, sm
        hist16 = jnp.zeros((bsz, 16, POOL_WIDTH), F32)
        tq_a, tq_b, t_pad = _pick_tile(t, 256), _pick_tile(t, 256), t
    else:
        pa, pb, plf, pc = past
        pk, pv, pik = (pa[:, :, j].astype(BF16) for j in range(3))

        def join_rows(p, new):
            full = jnp.concatenate([p, new.reshape(bsz, t, new.shape[-1])], axis=1)
            return jnp.pad(full, ((0, 0), (0, lpad - n_keys), (0, 0)))

        kik_all = join_rows(jnp.concatenate([pk, pik], axis=-1), kik)
        avt_all = _value_blocks(_with_ones_rows(jnp.swapaxes(pv, 1, 2)), avt[0, 0], bsz, t, lpad)
        bk_all = join_rows(pb[:, :, 0].astype(BF16).reshape(bsz, n_past, 384), bk)
        pvt = jnp.transpose(pb[:, :, 1].astype(BF16), (0, 2, 3, 1))
        bvt_all = _value_blocks(_with_ones_rows(pvt), bvt[0, :, 0], bsz, t, lpad)
        logf_all = join_rows(jnp.pad(plf, ((0, 0), (0, 0), (0, LANES - B_HEADS))), sm)
        hist16 = jnp.pad(pc, ((0, 0), (1, 0), (0, 0)))
        tq_a = tq_b = t_pad = LANES
        aqt, iqt, bqt, smt = (_per_seq_cols(a, bsz, t, t_pad) for a in (aqt, iqt, bqt, smt))

    oa = _dsa(aqt, iqt, smt, kik_all, avt_all, tq=tq_a, n_keys=n_keys, q_pos0=n_past, n_sel=n_sel)
    ck = _cum_logf(logf_all, KV_BLOCK)
    ob = _fox(bqt, bk_all, bvt_all, ck, tq=tq_b, n_keys=n_keys, q_pos0=n_past)
    cu = cu.reshape(bsz, t, POOL_WIDTH)
    oc = _pool(cu, hist16, lw["pool_bd"], lw["pool_scale"], tc=_pick_tile(t, 256), start_pos=n_past)
    return oa[:, :t], ob[:, :t], oc


def _layer(x, mod, past, n_past, pos_tab, lw, layer, final_g, per_token):
    bsz, t, d = x.shape
    sh1, sc1, g1, sh2, sc2, g2 = mod
    if per_token:
        xt = x.reshape(1, bsz * t, d)
        sh1, sc1, g1, sh2, sc2, g2 = (jnp.broadcast_to(m, (bsz, t, d)).reshape(1, bsz * t, d) for m in mod)
        cos, sin = (jnp.tile(a, (bsz, 1)) for a in pos_tab)
    else:
        xt = x
        cos, sin = pos_tab
    tm = KV_BLOCK
    (aqt, iqt, nat, kik, avt, nbt, bqt, bk, bvt, cu, sm, smt, gates) = _in_proj(
        xt, sc1, sh1, lw["norm_mix_g"], lw["w_in"], lw["bf_bias"], cos, sin, tm)
    oa, ob, oc = _mixers((aqt, iqt, kik, avt, bqt, bk, bvt, cu, sm, smt), past, n_past, lw, bsz, t)

    def flat(a):
        return a.reshape(xt.shape[0], xt.shape[1], a.shape[-1])

    router = (lw["router_w"], lw["router_b"]) if layer % 2 else None
    res = _merge(xt, flat(oa), flat(ob), flat(oc), gates, g1, sc2, sh2, lw["norm_ffn_g"],
                 lw["w_br_a"], lw["w_br_b"], lw["w_br_c"], lw["w_out"], router, tm)
    tmf = _pick_tile(xt.shape[1], 512)
    if layer % 2 == 0:
        x_mid, h2 = res
        x_new = _ffn(x_mid, h2, g2, lw["ffn_wg"], lw["ffn_wu"], lw["ffn_wd"], final_g, tmf)
    else:
        x_mid, h2, gw = res
        x_new = _moe(x_mid, h2, g2, gw, lw["moe_wg"], lw["moe_wu"], lw["moe_wd"], final_g, tmf)
    def token_major(a, *feat):
        a = a.reshape(a.shape[0], *feat, -1, t) if per_token else a.reshape(a.shape[0], *feat, 1, t)
        a = jnp.moveaxis(a, (-2, -1), (1, 2))
        return a.reshape(bsz, t, *feat)

    new_a = token_major(nat, 3, HEAD_DIM)
    new_b = token_major(nbt, 2, B_HEADS, HEAD_DIM)
    new_logf = sm.reshape(bsz, t, LANES)[:, :, :B_HEADS]
    new_pool = cu.reshape(bsz, t, POOL_WIDTH)[:, t - POOL_HIST:, :]
    return x_new.reshape(bsz, t, d), (new_a, new_b, new_logf, new_pool)


def kernel(x_prompt, x_sample, cache_a_kvi, cache_b_kv, cache_b_logf, state_c_pool, c_prompt, c_sample,
           ada_w, ada_b, norm_mix_g, w_in, b_forget, pool_w, pool_scale, w_br_a, w_br_b, w_br_c, w_out,
           norm_ffn_g, ffn_w_gate, ffn_w_up, ffn_w_down, moe_router_w, moe_router_b, moe_w_gate,
           moe_w_up, moe_w_down, final_norm_g):
    depth = ada_w.shape[0]
    bp, tp, d = x_prompt.shape
    bs, ts, _ = x_sample.shape
    n_past = cache_a_kvi.shape[2]
    assert tp % KV_BLOCK == 0 and (bs * ts) % KV_BLOCK == 0 and ts <= LANES

    rows = -(-(bp + bs) // 8) * 8
    c_all = jnp.pad(jnp.concatenate([c_prompt, c_sample], axis=0), ((0, rows - bp - bs), (0, 0)))
    mod_all = _ada(c_all, ada_w, ada_b)

    tab_p = _rope_tables(jnp.arange(tp))
    tab_s = _rope_tables(n_past + jnp.arange(ts))

    xp, xs = x_prompt, x_sample
    outs_p, outs_s = [], []
    for layer in range(depth):
        j = layer // 2
        w_l, bias_l = _in_weights(w_in[layer], b_forget[layer])
        pw = pool_w[layer]
        pool_bd = jnp.zeros((POOL_WIDTH, POOL_WIDTH), F32)
        for g in range(len(POOL_WINDOWS)):
            sl = slice(g * POOL_GROUP_DIM, (g + 1) * POOL_GROUP_DIM)
            pool_bd = pool_bd.at[sl, sl].set(pw[g])
        lw = dict(w_in=w_l, bf_bias=bias_l, norm_mix_g=norm_mix_g[layer], norm_ffn_g=norm_ffn_g[layer],
                  pool_bd=pool_bd.astype(BF16), pool_scale=pool_scale[layer],
                  w_br_a=w_br_a[layer].astype(BF16), w_br_b=w_br_b[layer].astype(BF16),
                  w_br_c=w_br_c[layer].astype(BF16), w_out=w_out[layer].astype(BF16))
        if layer % 2 == 0:
            lw.update(ffn_wg=ffn_w_gate[j].astype(BF16), ffn_wu=ffn_w_up[j].astype(BF16),
                      ffn_wd=ffn_w_down[j].astype(BF16))
        else:
            rw = jnp.pad(moe_router_w[j], ((0, 0), (0, LANES - N_EXPERTS)))
            rw_hi = rw.astype(BF16)
            lw.update(router_w=jnp.stack([rw_hi, (rw - rw_hi.astype(F32)).astype(BF16)]),
                      router_b=jnp.pad(moe_router_b[j], (0, LANES - N_EXPERTS)).reshape(1, LANES),
                      moe_wg=moe_w_gate[j].astype(BF16), moe_wu=moe_w_up[j].astype(BF16),
                      moe_wd=moe_w_down[j].astype(BF16))
        final_g = final_norm_g if layer == depth - 1 else None
        mod_p = [m[:, None, :] for m in jnp.split(mod_all[layer, :bp], 6, axis=-1)]
        mod_s = [m[:, None, :] for m in jnp.split(mod_all[layer, bp:bp + bs], 6, axis=-1)]
        xp, new_p = _layer(xp, mod_p, None, 0, tab_p, lw, layer, final_g, per_token=False)
        past = (cache_a_kvi[layer], cache_b_kv[layer], cache_b_logf[layer], state_c_pool[layer])
        xs, new_s = _layer(xs, mod_s, past, n_past, tab_s, lw, layer, final_g, per_token=True)
        outs_p.append(new_p)
        outs_s.append(new_s)

    def stack(outs, k):
        return jnp.stack([o[k] for o in outs])

    return (xp, xs,
            stack(outs_p, 0), stack(outs_p, 1), stack(outs_p, 2), stack(outs_p, 3),
            stack(outs_s, 0), stack(outs_s, 1), stack(outs_s, 2), stack(outs_s, 3))
```

```python
import functools

import jax
import jax.numpy as jnp
import numpy as np
from jax import lax
from jax.experimental import pallas as pl
from jax.experimental.pallas import tpu as pltpu

F32 = jnp.float32
BF16 = jnp.bfloat16
I32 = jnp.int32

D_MODEL = 1024
CHUNK = 64
HEAD_DIM = 64
HALF = HEAD_DIM // 2
ROPE_THETA = 10000.0
NORM_EPS = 1e-6
A_HEADS = 6
IDX_HEADS = 4
TOPK_MAX = 256
B_HEADS = 6
POOL_WINDOWS = (2, 4, 8, 16)
POOL_GROUP_DIM = 64
POOL_WIDTH = 256
POOL_HIST = 15
N_EXPERTS = 8
LANES = 128
SUBLANES = 8
LOG2E = 1.4426950408889634
QK_SCALE = HEAD_DIM ** -0.5 * LOG2E
KV_BLOCK = 256
V_ROWS = HEAD_DIM + 16
MOE_CHUNK = 256
VMEM_LIMIT = 56 * 1024 * 1024
NEG_INF = float("-inf")
KEY_DT = jnp.bfloat16

C_AQ, C_IQ, C_A, C_B, C_CU, C_SM, C_GATE, C_END = 0, 384, 640, 896, 2048, 2304, 2432, 5504


def _cparams(sem):
    return pltpu.CompilerParams(dimension_semantics=sem, vmem_limit_bytes=VMEM_LIMIT)


def _const_spec(shape):
    nd = len(shape)
    return pl.BlockSpec(shape, lambda *_: (0,) * nd, pipeline_mode=pl.Buffered(1))


def _lane_iota(shape):
    return lax.broadcasted_iota(I32, shape, len(shape) - 1)


def _row_iota(shape):
    return lax.broadcasted_iota(I32, shape, len(shape) - 2)


def _ada_kernel(c_ref, w_ref, b_ref, o_ref):
    c = c_ref[...]
    s = c * jax.nn.sigmoid(c)
    o_ref[...] = jnp.dot(s, w_ref[...], preferred_element_type=F32,
                         precision=lax.Precision.HIGHEST) + b_ref[...]


def _ada(c_all, ada_w, ada_b):
    depth, d, n = ada_w.shape
    rows = c_all.shape[0]
    tn = 1536
    return pl.pallas_call(
        _ada_kernel,
        grid=(depth, n // tn),
        in_specs=[pl.BlockSpec((rows, d), lambda l, j: (0, 0)),
                  pl.BlockSpec((None, d, tn), lambda l, j: (l, 0, j)),
                  pl.BlockSpec((None, 1, tn), lambda l, j: (l, 0, j))],
        out_specs=pl.BlockSpec((None, rows, tn), lambda l, j: (l, 0, j)),
        out_shape=jax.ShapeDtypeStruct((depth, rows, n), F32),
        compiler_params=_cparams(("arbitrary", "arbitrary")),
        name="ada_mod",
    )(c_all, ada_w, ada_b.reshape(depth, 1, n))


def _in_kernel(x_ref, sc_ref, sh_ref, g_ref, w_ref, bf_ref, cos_ref, sin_ref,
               aqt_ref, iqt_ref, nat_ref, kik_ref, avt_ref, nbt_ref, bqt_ref, bk_ref, bvt_ref,
               cu_ref, sm_ref, smt_ref, gate_ref):
    x = x_ref[...]
    ms = jnp.mean(x * x, axis=-1, keepdims=True)
    y = x * lax.rsqrt(ms + NORM_EPS) * g_ref[...]
    h = (y * (1.0 + sc_ref[...]) + sh_ref[...]).astype(BF16)
    tm = x.shape[0]

    def mm(a, b):
        return jnp.dot(h, w_ref[:, a:b], preferred_element_type=F32)

    cos = cos_ref[...]
    sin = sin_ref[...]
    lane = _lane_iota((tm, LANES))
    low = lane < HEAD_DIM
    first_half = (lane & HALF) == 0

    def rope(z):
        swapped = jnp.where(first_half, pltpu.roll(z, LANES - HALF, 1), pltpu.roll(z, HALF, 1))
        return z * cos + swapped * sin

    zeros64 = jnp.zeros((HEAD_DIM, tm), BF16)
    ones_rows = jnp.where(_row_iota((V_ROWS - HEAD_DIM, tm)) == 0, 1.0, 0.0).astype(BF16)

    def put_heads(ref, zt, p, slot_even, slot_odd):
        for hh, slot in ((0, slot_even), (1, slot_odd)):
            base = (2 * p + hh) * LANES
            ref[base + slot * HEAD_DIM:base + (slot + 1) * HEAD_DIM, :] = zt[hh * HEAD_DIM:(hh + 1) * HEAD_DIM]
            ref[base + (1 - slot) * HEAD_DIM:base + (2 - slot) * HEAD_DIM, :] = zeros64

    z = mm(C_AQ, C_IQ)
    for p in range(3):
        zt = (rope(z[:, p * LANES:(p + 1) * LANES]) * QK_SCALE).T.astype(BF16)
        put_heads(aqt_ref, zt, p, 0, 0)
    z = mm(C_IQ, C_A)
    for p in range(2):
        zt = rope(z[:, p * LANES:(p + 1) * LANES]).T.astype(BF16)
        put_heads(iqt_ref, zt, p, 1, 1)

    z = mm(C_A, C_B)
    kv = z[:, :LANES]
    r0 = jnp.where(low, rope(kv), kv)
    r1 = rope(z[:, LANES:])
    r0t = r0.T
    nat_ref[0:LANES, :] = r0t
    nat_ref[LANES:, :] = r1.T[0:HEAD_DIM, :]
    kik_ref[...] = jnp.where(low, r0, pltpu.roll(r1, HEAD_DIM, 1)).astype(BF16)
    avt_ref[0:HEAD_DIM, :] = r0t[HEAD_DIM:, :].astype(BF16)
    avt_ref[HEAD_DIM:, :] = ones_rows

    z = mm(C_B, C_CU)
    bk_ref[...] = z[:, 384:768].astype(BF16)
    for p in range(3):
        zt = (z[:, p * LANES:(p + 1) * LANES] * QK_SCALE).T.astype(BF16)
        put_heads(bqt_ref, zt, p, 0, 1)
        nbt_ref[p * LANES:(p + 1) * LANES, :] = z[:, 384 + p * LANES:384 + (p + 1) * LANES].T
        vt = z[:, 768 + p * LANES:768 + (p + 1) * LANES].T
        nbt_ref[384 + p * LANES:384 + (p + 1) * LANES, :] = vt
        vt = vt.astype(BF16)
        for hh in range(2):
            bvt_ref[2 * p + hh, 0:HEAD_DIM, :] = vt[hh * HEAD_DIM:(hh + 1) * HEAD_DIM]
            bvt_ref[2 * p + hh, HEAD_DIM:, :] = ones_rows

    cu_ref[...] = mm(C_CU, C_SM)

    z = mm(C_SM, C_GATE)
    t = z + bf_ref[...]
    logf = jnp.minimum(t, 0.0) - jnp.log1p(jnp.exp(-jnp.abs(t)))
    sm = jnp.where(lane < B_HEADS, logf, z)
    sm_ref[...] = sm
    smt_ref[...] = sm.T[0:16, :]

    for c in range(3):
        gate_ref[:, c * D_MODEL:(c + 1) * D_MODEL] = jax.nn.sigmoid(
            mm(C_GATE + c * D_MODEL, C_GATE + (c + 1) * D_MODEL))


def _in_weights(w_in_l, b_forget_l):
    d = w_in_l.shape[0]
    sizes = (384, 64, 64, 256, 4, 64, 384, 384, 384, 6, 256, 3072)
    o = np.concatenate([[0], np.cumsum(sizes)])
    w_in_l = w_in_l.astype(BF16)
    cols = [w_in_l[:, o[0]:o[1]],
            w_in_l[:, o[3]:o[4]],
            w_in_l[:, o[1]:o[3]], w_in_l[:, o[5]:o[6]], jnp.zeros((d, 64), BF16),
            w_in_l[:, o[6]:o[9]],
            w_in_l[:, o[10]:o[11]],
            w_in_l[:, o[9]:o[10]], jnp.zeros((d, 2), BF16), w_in_l[:, o[4]:o[5]], jnp.zeros((d, LANES - 12), BF16),
            w_in_l[:, o[11]:o[12]]]
    w = jnp.concatenate(cols, axis=1)
    assert w.shape[1] == C_END, w.shape
    bias = jnp.concatenate([b_forget_l, jnp.zeros((LANES - B_HEADS,), F32)]).reshape(1, LANES)
    return w, bias


def _in_proj(x, sc, sh, g, w, bias, cos, sin, tm):
    bsz, t, d = x.shape
    mrows = sc.shape[1]
    mblk = 1 if mrows == 1 else tm
    mod_spec = pl.BlockSpec((None, mblk, d), (lambda b, i: (b, 0, 0)) if mrows == 1 else (lambda b, i: (b, i, 0)))
    nblk = t // tm

    def rows(n, dt):
        return pl.BlockSpec((None, tm, n), lambda b, i: (b, i, 0)), jax.ShapeDtypeStruct((bsz, t, n), dt)

    def cols(n, dt):
        return pl.BlockSpec((None, n, tm), lambda b, i: (b, 0, i)), jax.ShapeDtypeStruct((bsz, n, t), dt)

    outs = [cols(A_HEADS * LANES, BF16), cols(IDX_HEADS * LANES, BF16), cols(192, F32), rows(LANES, BF16),
            (pl.BlockSpec((None, None, V_ROWS, tm), lambda b, i: (b, i, 0, 0)),
             jax.ShapeDtypeStruct((bsz, nblk, V_ROWS, tm), BF16)),
            cols(768, F32), cols(B_HEADS * LANES, BF16), rows(384, BF16),
            (pl.BlockSpec((None, B_HEADS, None, V_ROWS, tm), lambda b, i: (b, 0, i, 0, 0)),
             jax.ShapeDtypeStruct((bsz, B_HEADS, nblk, V_ROWS, tm), BF16)),
            rows(256, F32), rows(LANES, F32), cols(16, F32), rows(3 * D_MODEL, F32)]
    return pl.pallas_call(
        _in_kernel,
        grid=(bsz, nblk),
        in_specs=[pl.BlockSpec((None, tm, d), lambda b, i: (b, i, 0)), mod_spec, mod_spec,
                  _const_spec((1, d)), _const_spec(w.shape), _const_spec((1, LANES)),
                  pl.BlockSpec((tm, LANES), lambda b, i: (i, 0)),
                  pl.BlockSpec((tm, LANES), lambda b, i: (i, 0))],
        out_specs=[o[0] for o in outs],
        out_shape=[o[1] for o in outs],
        compiler_params=_cparams(("parallel", "parallel")),
        name="in_proj",
    )(x, sc, sh, g.reshape(1, d), w, bias, cos, sin)


def _cum_kernel(x_ref, o_ref, carry_ref):
    @pl.when(pl.program_id(1) == 0)
    def _():
        carry_ref[...] = jnp.zeros_like(carry_ref)

    tc = x_ref.shape[0]
    tri = jnp.where(_lane_iota((tc, tc)) <= _row_iota((tc, tc)), 1.0, 0.0).astype(BF16)
    cum = carry_ref[0:1, :]
    rest = x_ref[...]
    for _ in range(3):
        piece = rest.astype(BF16)
        cum = cum + jnp.dot(tri, piece, preferred_element_type=F32)
        rest = rest - piece.astype(F32)
    for hd in range(B_HEADS):
        o_ref[hd] = jnp.broadcast_to(cum[:, hd:hd + 1] * LOG2E, (tc, LANES))
    carry_ref[...] = jnp.broadcast_to(cum[tc - 1:tc, :], carry_ref.shape)


def _cum_logf(x, tc):
    bsz, t, n = x.shape
    return pl.pallas_call(
        _cum_kernel,
        grid=(bsz, t // tc),
        in_specs=[pl.BlockSpec((None, tc, n), lambda b, i: (b, i, 0))],
        out_specs=pl.BlockSpec((None, B_HEADS, tc, LANES), lambda b, i: (b, 0, i, 0)),
        out_shape=jax.ShapeDtypeStruct((bsz, B_HEADS, t, LANES), F32),
        scratch_shapes=[pltpu.VMEM((SUBLANES, LANES), F32)],
        compiler_params=_cparams(("parallel", "arbitrary")),
        name="logf_cumsum",
    )(x)


def _sum_keys(x):
    part = x.reshape(x.shape[0] // SUBLANES, SUBLANES, x.shape[1]).sum(axis=0)
    return jnp.sum(part, axis=0, keepdims=True)


def _max_keys(x):
    part = x.reshape(x.shape[0] // SUBLANES, SUBLANES, x.shape[1]).max(axis=0)
    return jnp.max(part, axis=0, keepdims=True)


def _dsa_kernel(aqt_ref, iqt_ref, smt_ref, kik_ref, avt_ref, o_ref, key_ref,
                *, tq, tk, n_keys, q_pos0, n_sel):
    i = pl.program_id(1)
    pos_first = q_pos0 + i * tq
    last_chunk = (pos_first + tq - 1) // CHUNK
    n_adm = jnp.minimum((last_chunk + 1) * CHUNK, n_keys)
    nkb = (n_adm + tk - 1) // tk
    n_pairs = (nkb + 1) // 2

    q_pos = pos_first + _lane_iota((1, tq))
    q_lim = jnp.minimum((q_pos // CHUNK + 1) * CHUNK, n_keys)
    key_row = _row_iota((tk, tq))

    def keys(kb):
        return kik_ref[pl.ds(pl.multiple_of(kb * tk, tk), tk), :]

    iq4 = jnp.concatenate([iqt_ref[hd * LANES:(hd + 1) * LANES, :] for hd in range(IDX_HEADS)], axis=1)
    smt = smt_ref[...]
    w_rows = [smt[8 + hd:9 + hd, :] for hd in range(IDX_HEADS)]

    def score_body(kb, carry):
        s4 = jnp.dot(keys(kb), iq4, preferred_element_type=F32)
        score = w_rows[0] * jnp.maximum(s4[:, 0:tq], 0.0)
        for hd in range(1, IDX_HEADS):
            score = score + w_rows[hd] * jnp.maximum(s4[:, hd * tq:(hd + 1) * tq], 0.0)
        score = jnp.where(key_row < q_lim - kb * tk, score, NEG_INF)
        key_ref[kb] = score.astype(KEY_DT)
        return carry

    lax.fori_loop(0, nkb, score_body, 0)

    @pl.when(nkb % 2 == 1)
    def _():
        key_ref[nkb] = jnp.full((tk, tq), NEG_INF, KEY_DT)

    one, zero = jnp.ones((), KEY_DT), jnp.zeros((), KEY_DT)
    packed_rows = 2 * SUBLANES

    def count(cand, strict):
        def hits(blk):
            h = jnp.where((blk > cand) if strict else (blk >= cand), one, zero)
            parts = [h[r * packed_rows:(r + 1) * packed_rows] for r in range(tk // packed_rows)]
            while len(parts) > 1:
                parts = [a + b for a, b in zip(parts[::2], parts[1::2])]
            return parts[0].astype(F32)

        def body(j, acc):
            return acc + hits(key_ref[2 * j]) + hits(key_ref[2 * j + 1])

        acc = lax.fori_loop(0, n_pairs, body, jnp.zeros((packed_rows, tq), F32))
        return jnp.sum(acc, axis=0, keepdims=True)

    def pattern_value(u):
        bits = jnp.where(u >= 0x8000, u & 0x7FFF, (~u) & 0xFFFF)
        return lax.bitcast_convert_type(lax.shift_left(bits, 16), F32).astype(KEY_DT)

    def bit_body(b, u):
        cand_u = u | lax.shift_left(jnp.int32(1), 15 - b)
        cnt = count(pattern_value(cand_u), False)
        return jnp.where(cnt >= n_sel, cand_u, u)

    u_thr = lax.fori_loop(0, 16, bit_body, jnp.zeros((1, tq), I32))
    thr_k = pattern_value(jnp.maximum(u_thr, 0x007F))
    need = jnp.maximum(n_sel - count(thr_k, True), 0.0)
    thr = thr_k.astype(F32)

    aq6 = jnp.concatenate([aqt_ref[hd * LANES:(hd + 1) * LANES, :] for hd in range(A_HEADS)], axis=1)

    half = tk // 2
    lower = jnp.where(_lane_iota((half, half)) <= _row_iota((half, half)), 1.0, 0.0).astype(BF16)

    def attend():
        def body(kb, carry):
            eq_seen, ms, accs = carry
            blk = key_ref[kb].astype(F32)
            eq = blk == thr
            eq_f = jnp.where(eq, 1.0, 0.0)
            prefs = []
            for e in (eq_f[:half], eq_f[half:]):
                prefs.append(jnp.dot(lower, e.astype(BF16), preferred_element_type=F32) + eq_seen)
                eq_seen = eq_seen + _sum_keys(e)
            slack = jnp.where(blk >= thr, need - jnp.where(eq, jnp.concatenate(prefs, axis=0), 0.0), -1.0)
            bias = jnp.where(slack >= 0.0, jnp.where(jnp.abs(blk) < jnp.inf, 0.0, NEG_INF), NEG_INF)
            logits = jnp.dot(keys(kb), aq6, preferred_element_type=F32)
            vts = (avt_ref[2 * kb], avt_ref[2 * kb + 1])
            new_ms, new_accs = [], []
            for p in range(A_HEADS // 2):
                ps, alphas = [], []
                for hd in (2 * p, 2 * p + 1):
                    lg = logits[:, hd * tq:(hd + 1) * tq] + bias
                    m_old = ms[hd]
                    m_new = jnp.maximum(m_old, _max_keys(lg))
                    m_safe = jnp.where(m_new == NEG_INF, 0.0, m_new)
                    ps.append(jnp.exp2(lg - m_safe).astype(BF16))
                    alphas.append(jnp.exp2(m_old - m_safe))
                    new_ms.append(m_new)
                p2 = jnp.concatenate(ps, axis=1)
                pv = (jnp.dot(vts[0], p2[:tk // 2], preferred_element_type=F32)
                      + jnp.dot(vts[1], p2[tk // 2:], preferred_element_type=F32))
                new_accs.append(jnp.concatenate(alphas, axis=1) * accs[p] + pv)
            return eq_seen, tuple(new_ms), tuple(new_accs)

        init = (jnp.zeros((1, tq), F32),
                tuple(jnp.full((1, tq), NEG_INF, F32) for _ in range(A_HEADS)),
                tuple(jnp.zeros((V_ROWS, 2 * tq), F32) for _ in range(A_HEADS // 2)))
        return lax.fori_loop(0, nkb, body, init)[2]

    accs = attend()

    outs = []
    for p in range(A_HEADS // 2):
        o2 = accs[p][0:HEAD_DIM] / accs[p][HEAD_DIM:HEAD_DIM + 1]
        outs += [o2[:, 0:tq], o2[:, tq:2 * tq]]
    o_ref[...] = jnp.concatenate(outs, axis=0).T.astype(o_ref.dtype)


def _dsa(aqt, iqt, smt, kik, avt, *, tq, n_keys, q_pos0, n_sel):
    bsz, _, t_q = aqt.shape
    _, nblk, _, tkv = avt.shape
    assert nblk % 2 == 0
    tk = 2 * tkv
    kern = functools.partial(_dsa_kernel, tq=tq, tk=tk, n_keys=n_keys, q_pos0=q_pos0, n_sel=n_sel)
    return pl.pallas_call(
        kern,
        grid=(bsz, t_q // tq),
        in_specs=[pl.BlockSpec((None, A_HEADS * LANES, tq), lambda b, i: (b, 0, i)),
                  pl.BlockSpec((None, IDX_HEADS * LANES, tq), lambda b, i: (b, 0, i)),
                  pl.BlockSpec((None, 16, tq), lambda b, i: (b, 0, i)),
                  pl.BlockSpec((None, nblk * tkv, LANES), lambda b, i: (b, 0, 0)),
                  pl.BlockSpec((None, nblk, V_ROWS, tkv), lambda b, i: (b, 0, 0, 0))],
        out_specs=pl.BlockSpec((None, tq, 384), lambda b, i: (b, i, 0)),
        out_shape=jax.ShapeDtypeStruct((bsz, t_q, 384), BF16),
        scratch_shapes=[pltpu.VMEM((2 * ((nblk // 2 + 1) // 2), tk, tq), KEY_DT)],
        compiler_params=_cparams(("parallel", "arbitrary")),
        name="dsa_attention",
    )(aqt, iqt, smt, kik, avt)


def _fox_kernel(qt_ref, k_ref, vt_ref, ck_ref, o_ref, *, tq, tk, n_keys, q_pos0):
    i = pl.program_id(1)
    pos_first = q_pos0 + i * tq
    n_full = pos_first // tk
    nkb = (jnp.minimum(pos_first + tq, n_keys) + tk - 1) // tk
    q_pos = pos_first + _lane_iota((1, tq))
    key_row = _row_iota((tk, tq))
    qts = [qt_ref[hd * LANES:(hd + 1) * LANES, :] for hd in range(B_HEADS)]

    def step(kbs, state, masked):
        logits = []
        for hd in range(B_HEADS):
            for kb in kbs:
                rows = pl.ds(pl.multiple_of(kb * tk, tk), tk)
                ck = ck_ref[hd, rows, :]
                kblk = k_ref[rows, (hd // 2) * LANES:(hd // 2 + 1) * LANES]
                lg = (jnp.dot(kblk, qts[hd], preferred_element_type=F32)
                      - jnp.concatenate([ck] * (tq // LANES), axis=1))
                if masked:
                    lg = jnp.where(key_row <= q_pos - kb * tk, lg, NEG_INF)
                logits.append(lg)
        new = []
        for hd in range(B_HEADS):
            m_old, acc = state[hd]
            lgs = logits[hd * len(kbs):(hd + 1) * len(kbs)]
            m_new = m_old
            for lg in lgs:
                m_new = jnp.maximum(m_new, _max_keys(lg))
            m_safe = jnp.where(m_new == NEG_INF, 0.0, m_new) if masked else m_new
            acc = jnp.exp2(m_old - m_safe) * acc
            for kb, lg in zip(kbs, lgs):
                p = jnp.exp2(lg - m_safe).astype(BF16)
                acc = acc + jnp.dot(vt_ref[hd, kb], p, preferred_element_type=F32)
            new.append((m_new, acc))
        return tuple(new)

    init = tuple((jnp.full((1, tq), NEG_INF, F32), jnp.zeros((V_ROWS, tq), F32)) for _ in range(B_HEADS))
    state = lax.fori_loop(0, n_full // 2, lambda j, st: step((2 * j, 2 * j + 1), st, False), init)
    state = lax.fori_loop(2 * (n_full // 2), nkb, lambda kb, st: step((kb,), st, True), state)
    outs = [acc[0:HEAD_DIM] / acc[HEAD_DIM:HEAD_DIM + 1] for _, acc in state]
    o_ref[...] = jnp.concatenate(outs, axis=0).T.astype(o_ref.dtype)


def _fox(qt, k, vt, ck, *, tq, n_keys, q_pos0):
    bsz, _, t_q = qt.shape
    _, _, nblk, _, tk = vt.shape
    lpad = nblk * tk
    kern = functools.partial(_fox_kernel, tq=tq, tk=tk, n_keys=n_keys, q_pos0=q_pos0)
    return pl.pallas_call(
        kern,
        grid=(bsz, t_q // tq),
        in_specs=[pl.BlockSpec((None, B_HEADS * LANES, tq), lambda b, i: (b, 0, i)),
                  pl.BlockSpec((None, lpad, 384), lambda b, i: (b, 0, 0)),
                  pl.BlockSpec((None, B_HEADS, nblk, V_ROWS, tk), lambda b, i: (b, 0, 0, 0, 0)),
                  pl.BlockSpec((None, B_HEADS, lpad, LANES), lambda b, i: (b, 0, 0, 0))],
        out_specs=pl.BlockSpec((None, tq, 384), lambda b, i: (b, i, 0)),
        out_shape=jax.ShapeDtypeStruct((bsz, t_q, 384), BF16),
        compiler_params=_cparams(("parallel", "arbitrary")),
        name="fox_attention",
    )(qt, k, vt, ck)


def _pool_kernel(cur_ref, prev_ref, hist_ref, w_ref, s_ref, o_ref, ext, *, tc, start_pos):
    i = pl.program_id(1)
    cur = cur_ref[...]
    ext[0:16, :] = jnp.where(i == 0, hist_ref[...], prev_ref[tc - 16:, :])
    ext[16:, :] = cur
    pos = start_pos + i * tc + lax.broadcasted_iota(I32, (tc, POOL_WIDTH), 0)
    lane = _lane_iota((tc, POOL_WIDTH))
    run = cur
    pooled = jnp.zeros_like(cur)
    k = 1
    for g, w in enumerate(POOL_WINDOWS):
        while k < w:
            run = run + ext[16 - k:16 - k + tc, :]
            k += 1
        cnt = jnp.minimum(pos + 1, w).astype(F32)
        in_group = (lane >= g * POOL_GROUP_DIM) & (lane < (g + 1) * POOL_GROUP_DIM)
        pooled = jnp.where(in_group, run / cnt, pooled)
    z = (pooled - cur).astype(BF16)
    o_ref[...] = (jnp.dot(z, w_ref[...], preferred_element_type=F32) * s_ref[...]).astype(o_ref.dtype)


def _pool(cu, hist16, w_bd, scale, *, tc, start_pos):
    bsz, t, n = cu.shape
    kern = functools.partial(_pool_kernel, tc=tc, start_pos=start_pos)
    return pl.pallas_call(
        kern,
        grid=(bsz, t // tc),
        in_specs=[pl.BlockSpec((None, tc, n), lambda b, i: (b, i, 0)),
                  pl.BlockSpec((None, tc, n), lambda b, i: (b, jnp.maximum(i - 1, 0), 0)),
                  pl.BlockSpec((None, 16, n), lambda b, i: (b, 0, 0)),
                  _const_spec((n, n)), _const_spec((1, n))],
        out_specs=pl.BlockSpec((None, tc, n), lambda b, i: (b, i, 0)),
        out_shape=jax.ShapeDtypeStruct((bsz, t, n), BF16),
        scratch_shapes=[pltpu.VMEM((16 + tc, n), F32)],
        compiler_params=_cparams(("parallel", "arbitrary")),
        name="pool_mixer",
    )(cu, cu, hist16, w_bd, scale.reshape(1, n))


def _route(logits):
    lane = _lane_iota(logits.shape).astype(F32)
    lg = jnp.where(lane < N_EXPERTS, logits, NEG_INF)
    m1 = jnp.max(lg, axis=1, keepdims=True)
    i1 = jnp.min(jnp.where(lg == m1, lane, float(LANES)), axis=1, keepdims=True)
    hot1 = lane == i1
    lg2 = jnp.where(hot1, NEG_INF, lg)
    m2 = jnp.max(lg2, axis=1, keepdims=True)
    i2 = jnp.min(jnp.where(lg2 == m2, lane, float(LANES)), axis=1, keepdims=True)
    hot2 = lane == i2
    e2 = jnp.exp(m2 - m1)
    den = 1.0 + e2
    return jnp.where(hot1, 1.0 / den, 0.0) + jnp.where(hot2, e2 / den, 0.0)


def _merge_kernel(x_ref, oa_ref, ob_ref, oc_ref, gate_ref, g1_ref, sc2_ref, sh2_ref, g_ref,
                  wa_ref, wb_ref, wc_ref, wo_ref, *rest, moe):
    if moe:
        rw_ref, rb_ref, xo_ref, h_ref, gw_ref = rest
    else:
        xo_ref, h_ref = rest
    d = D_MODEL
    merged = (gate_ref[:, 0:d] * jnp.dot(oa_ref[...], wa_ref[...], preferred_element_type=F32)
              + gate_ref[:, d:2 * d] * jnp.dot(ob_ref[...], wb_ref[...], preferred_element_type=F32)
              + gate_ref[:, 2 * d:3 * d] * jnp.dot(oc_ref[...], wc_ref[...], preferred_element_type=F32))
    x = x_ref[...] + g1_ref[...] * jnp.dot(merged.astype(BF16), wo_ref[...], preferred_element_type=F32)
    xo_ref[...] = x
    ms = jnp.mean(x * x, axis=-1, keepdims=True)
    y = x * lax.rsqrt(ms + NORM_EPS) * g_ref[...]
    h = y * (1.0 + sc2_ref[...]) + sh2_ref[...]
    h_ref[...] = h.astype(BF16)
    if moe:
        h_hi = h.astype(BF16)
        h_lo = (h - h_hi.astype(F32)).astype(BF16)
        logits = (jnp.dot(h_hi, rw_ref[0], preferred_element_type=F32)
                  + jnp.dot(h_lo, rw_ref[0], preferred_element_type=F32)
                  + jnp.dot(h_hi, rw_ref[1], preferred_element_type=F32)) + rb_ref[...]
        gw_ref[...] = _route(logits)


def _merge(x, oa, ob, oc, gates, g1, sc2, sh2, g, wa, wb, wc, wo, router, tm):
    bsz, t, d = x.shape
    mrows = g1.shape[1]
    mblk = 1 if mrows == 1 else tm
    mod_spec = pl.BlockSpec((None, mblk, d), (lambda b, i: (b, 0, 0)) if mrows == 1 else (lambda b, i: (b, i, 0)))

    def tok(n):
        return pl.BlockSpec((None, tm, n), lambda b, i: (b, i, 0))

    in_specs = [tok(d), tok(384), tok(384), tok(256), tok(3 * d), mod_spec, mod_spec, mod_spec,
                _const_spec((1, d)), _const_spec(wa.shape), _const_spec(wb.shape), _const_spec(wc.shape),
                _const_spec(wo.shape)]
    args = [x, oa, ob, oc, gates, g1, sc2, sh2, g.reshape(1, d), wa, wb, wc, wo]
    out_specs = [tok(d), tok(d)]
    out_shape = [jax.ShapeDtypeStruct((bsz, t, d), F32), jax.ShapeDtypeStruct((bsz, t, d), BF16)]
    if router is not None:
        rw, rb = router
        in_specs += [_const_spec(rw.shape), _const_spec(rb.shape)]
        args += [rw, rb]
        out_specs.append(tok(LANES))
        out_shape.append(jax.ShapeDtypeStruct((bsz, t, LANES), F32))
    return pl.pallas_call(
        functools.partial(_merge_kernel, moe=router is not None),
        grid=(bsz, t // tm),
        in_specs=in_specs, out_specs=out_specs, out_shape=out_shape,
        compiler_params=_cparams(("parallel", "parallel")),
        name="merge_out",
    )(*args)


def _final_norm(x, gain):
    ms = jnp.mean(x * x, axis=-1, keepdims=True)
    return x * lax.rsqrt(ms + NORM_EPS) * gain


def _ffn_kernel(x_ref, h_ref, g2_ref, wg_ref, wu_ref, wd_ref, *rest, n_chunks, final):
    if final:
        fg_ref, o_ref = rest
    else:
        (o_ref,) = rest
    h = h_ref[...]
    tf = wg_ref.shape[1] // n_chunks
    acc = jnp.zeros(x_ref.shape, F32)
    for c in range(n_chunks):
        gt = jnp.dot(h, wg_ref[:, c * tf:(c + 1) * tf], preferred_element_type=F32)
        up = jnp.dot(h, wu_ref[:, c * tf:(c + 1) * tf], preferred_element_type=F32)
        act = (gt * jax.nn.sigmoid(gt) * up).astype(BF16)
        acc = acc + jnp.dot(act, wd_ref[c * tf:(c + 1) * tf, :], preferred_element_type=F32)
    x = x_ref[...] + g2_ref[...] * acc
    o_ref[...] = _final_norm(x, fg_ref[...]) if final else x


def _ffn(x, h, g2, wg, wu, wd, final_g, tm):
    bsz, t, d = x.shape
    mrows = g2.shape[1]
    mblk = 1 if mrows == 1 else tm
    mod_spec = pl.BlockSpec((None, mblk, d), (lambda b, i: (b, 0, 0)) if mrows == 1 else (lambda b, i: (b, i, 0)))
    tok = pl.BlockSpec((None, tm, d), lambda b, i: (b, i, 0))
    in_specs = [tok, tok, mod_spec, _const_spec(wg.shape), _const_spec(wu.shape), _const_spec(wd.shape)]
    args = [x, h, g2, wg, wu, wd]
    if final_g is not None:
        in_specs.append(_const_spec((1, d)))
        args.append(final_g.reshape(1, d))
    return pl.pallas_call(
        functools.partial(_ffn_kernel, n_chunks=2, final=final_g is not None),
        grid=(bsz, t // tm),
        in_specs=in_specs, out_specs=tok,
        out_shape=jax.ShapeDtypeStruct((bsz, t, d), F32),
        compiler_params=_cparams(("parallel", "parallel")),
        name="ffn_dense",
    )(*args)


def _moe_kernel(x_ref, h_ref, g2_ref, gw_ref, wg_ref, wu_ref, wd_ref, *rest, final):
    if final:
        fg_ref, o_ref, acc_ref, posc_ref, posr_ref = rest
    else:
        o_ref, acc_ref, posc_ref, posr_ref = rest
    e = pl.program_id(2)
    tm = h_ref.shape[0]
    n_slabs = tm // MOE_CHUNK

    @pl.when(e == 0)
    def _():
        acc_ref[...] = jnp.zeros_like(acc_ref)
        routed = gw_ref[...] != 0.0
        r_f = jnp.where(routed, 1.0, 0.0)
        r_b = r_f.astype(BF16)
        r_t = r_f.T
        r_tb = r_t.astype(BF16)
        tok_l = _lane_iota((MOE_CHUNK, tm))
        tok_r = _row_iota((MOE_CHUNK, tm))
        rank_r = jnp.zeros((LANES, tm), F32)
        for s in range(n_slabs):
            rows = slice(s * MOE_CHUNK, (s + 1) * MOE_CHUNK)
            earlier = jnp.where(tok_l < tok_r + s * MOE_CHUNK, 1.0, 0.0).astype(BF16)
            rank_c = jnp.dot(earlier, r_b, preferred_element_type=F32)
            posc_ref[rows, :] = jnp.where(routed[rows], rank_c, -1.0)
            later = jnp.where(tok_r + s * MOE_CHUNK < tok_l, 1.0, 0.0).astype(BF16)
            rank_r = rank_r + jnp.dot(r_tb[:, rows], later, preferred_element_type=F32)
        posr_ref[...] = jnp.where(r_t != 0.0, rank_r, -1.0)

    lane_e = _lane_iota((tm, LANES)) == e
    pos_c = jnp.sum(jnp.where(lane_e, posc_ref[...], 0.0), axis=1, keepdims=True)
    gate_c = jnp.sum(jnp.where(lane_e, gw_ref[...], 0.0), axis=1, keepdims=True)
    pos_r = posr_ref[pl.ds(e, 1), :]
    n_routed = jnp.max(pos_r) + 1.0
    n_chunks = (n_routed.astype(I32) + MOE_CHUNK - 1) // MOE_CHUNK
    slot_rows = _row_iota((MOE_CHUNK, tm)).astype(F32)
    slot_lanes = _lane_iota((MOE_CHUNK, MOE_CHUNK)).astype(F32)

    def chunk(c, carry):
        base = (c * MOE_CHUNK).astype(F32)
        pack = jnp.where(pos_r - base == slot_rows, 1.0, 0.0).astype(BF16)
        xc = jnp.dot(pack, h_ref[...], preferred_element_type=F32).astype(BF16)
        gt = jnp.dot(xc, wg_ref[...], preferred_element_type=F32)
        up = jnp.dot(xc, wu_ref[...], preferred_element_type=F32)
        act = (gt * jax.nn.sigmoid(gt) * up).astype(BF16)
        y = jnp.dot(act, wd_ref[...], preferred_element_type=F32).astype(BF16)
        for s in range(n_slabs):
            rows = slice(s * MOE_CHUNK, (s + 1) * MOE_CHUNK)
            unpack = jnp.where(pos_c[rows] - base == slot_lanes, 1.0, 0.0).astype(BF16)
            acc_ref[rows, :] += gate_c[rows] * jnp.dot(unpack, y, preferred_element_type=F32)
        return carry

    lax.fori_loop(0, n_chunks, chunk, 0)

    @pl.when(e == pl.num_programs(2) - 1)
    def _():
        x = x_ref[...] + g2_ref[...] * acc_ref[...]
        o_ref[...] = _final_norm(x, fg_ref[...]) if final else x


def _moe(x, h, g2, gw, wg, wu, wd, final_g, tm):
    bsz, t, d = x.shape
    n_e, _, dff = wg.shape
    mrows = g2.shape[1]
    mblk = 1 if mrows == 1 else tm
    mod_spec = pl.BlockSpec((None, mblk, d), (lambda b, i, e: (b, 0, 0)) if mrows == 1 else (lambda b, i, e: (b, i, 0)))
    tok = pl.BlockSpec((None, tm, d), lambda b, i, e: (b, i, 0))
    tok_once = pl.BlockSpec((None, tm, d), lambda b, i, e: (b, i, 0), pipeline_mode=pl.Buffered(1))
    in_specs = [tok_once, tok, mod_spec, pl.BlockSpec((None, tm, LANES), lambda b, i, e: (b, i, 0)),
                pl.BlockSpec((None, d, dff), lambda b, i, e: (e, 0, 0)),
                pl.BlockSpec((None, d, dff), lambda b, i, e: (e, 0, 0)),
                pl.BlockSpec((None, dff, d), lambda b, i, e: (e, 0, 0))]
    args = [x, h, g2, gw, wg, wu, wd]
    if final_g is not None:
        in_specs.append(pl.BlockSpec((1, d), lambda b, i, e: (0, 0)))
        args.append(final_g.reshape(1, d))
    return pl.pallas_call(
        functools.partial(_moe_kernel, final=final_g is not None),
        grid=(bsz, t // tm, n_e),
        in_specs=in_specs, out_specs=tok,
        out_shape=jax.ShapeDtypeStruct((bsz, t, d), F32),
        scratch_shapes=[pltpu.VMEM((tm, d), F32), pltpu.VMEM((tm, LANES), F32), pltpu.VMEM((LANES, tm), F32)],
        compiler_params=_cparams(("parallel", "parallel", "arbitrary")),
        name="moe_routed",
    )(*args)


def _rope_tables(pos):
    inv = ROPE_THETA ** (-jnp.arange(HALF, dtype=F32) / HALF)
    ang = pos.astype(F32)[:, None] * inv[None, :]
    cos, sin = jnp.cos(ang), jnp.sin(ang)
    return jnp.tile(cos, (1, 4)), jnp.tile(jnp.concatenate([-sin, sin], axis=1), (1, 2))


def _pick_tile(n, pref):
    t = min(n, pref)
    while n % t:
        t //= 2
    return t


def _per_seq_cols(a, bsz, t, width):
    f = a.shape[1]
    a = jnp.moveaxis(a[0].reshape(f, bsz, t), 1, 0)
    return jnp.pad(a, ((0, 0), (0, 0), (0, width - t)))


def _value_blocks(past_vt, new_vt, bsz, t, lpad):
    lead = new_vt.shape[:-2]
    new_b = jnp.moveaxis(new_vt.reshape(*lead, V_ROWS, bsz, t), -2, 0)
    full = jnp.concatenate([past_vt, new_b], axis=-1)
    full = jnp.pad(full, [(0, 0)] * (full.ndim - 1) + [(0, lpad - full.shape[-1])])
    full = full.reshape(bsz, *lead, V_ROWS, lpad // KV_BLOCK, KV_BLOCK)
    return jnp.moveaxis(full, -2, -3)


def _with_ones_rows(vt):
    ones = jnp.ones(vt.shape[:-2] + (1, vt.shape[-1]), vt.dtype)
    zeros = jnp.zeros(vt.shape[:-2] + (V_ROWS - HEAD_DIM - 1, vt.shape[-1]), vt.dtype)
    return jnp.concatenate([vt, ones, zeros], axis=-2)


def _mixers(inp, past, n_past, lw, bsz, t):
    aqt, iqt, kik, avt, bqt, bk, bvt, cu, sm, smt = inp
    n_keys = n_past + t
    n_sel = min(TOPK_MAX, n_keys // 4)
    lpad = -(-n_keys // (2 * KV_BLOCK)) * 2 * KV_BLOCK
    if past is None:
        kik_all, avt_all, bk_all, bvt_all, logf_all = kik, avt, bk, bvt, sm
        hist16 = jnp.zeros((bsz, 16, POOL_WIDTH), F32)
        tq_a, tq_b, t_pad = _pick_tile(t, 256), _pick_tile(t, 256), t
    else:
        pa, pb, plf, pc = past
        pk, pv, pik = (pa[:, :, j].astype(BF16) for j in range(3))

        def join_rows(p, new):
            full = jnp.concatenate([p, new.reshape(bsz, t, new.shape[-1])], axis=1)
            return jnp.pad(full, ((0, 0), (0, lpad - n_keys), (0, 0)))

        kik_all = join_rows(jnp.concatenate([pk, pik], axis=-1), kik)
        avt_all = _value_blocks(_with_ones_rows(jnp.swapaxes(pv, 1, 2)), avt[0, 0], bsz, t, lpad)
        bk_all = join_rows(pb[:, :, 0].astype(BF16).reshape(bsz, n_past, 384), bk)
        pvt = jnp.transpose(pb[:, :, 1].astype(BF16), (0, 2, 3, 1))
        bvt_all = _value_blocks(_with_ones_rows(pvt), bvt[0, :, 0], bsz, t, lpad)
        logf_all = join_rows(jnp.pad(plf, ((0, 0), (0, 0), (0, LANES - B_HEADS))), sm)
        hist16 = jnp.pad(pc, ((0, 0), (1, 0), (0, 0)))
        tq_a = tq_b = t_pad = LANES
        aqt, iqt, bqt, smt = (_per_seq_cols(a, bsz, t, t_pad) for a in (aqt, iqt, bqt, smt))

    oa = _dsa(aqt, iqt, smt, kik_all, avt_all, tq=tq_a, n_keys=n_keys, q_pos0=n_past, n_sel=n_sel)
    ck = _cum_logf(logf_all, KV_BLOCK)
    ob = _fox(bqt, bk_all, bvt_all, ck, tq=tq_b, n_keys=n_keys, q_pos0=n_past)
    cu = cu.reshape(bsz, t, POOL_WIDTH)
    oc = _pool(cu, hist16, lw["pool_bd"], lw["pool_scale"], tc=_pick_tile(t, 256), start_pos=n_past)
    return oa[:, :t], ob[:, :t], oc


def _layer(x, mod, past, n_past, pos_tab, lw, layer, final_g, per_token):
    bsz, t, d = x.shape
    sh1, sc1, g1, sh2, sc2, g2 = mod
    if per_token:
        xt = x.reshape(1, bsz * t, d)
        sh1, sc1, g1, sh2, sc2, g2 = (jnp.broadcast_to(m, (bsz, t, d)).reshape(1, bsz * t, d) for m in mod)
        cos, sin = (jnp.tile(a, (bsz, 1)) for a in pos_tab)
    else:
        xt = x
        cos, sin = pos_tab
    tm = KV_BLOCK
    (aqt, iqt, nat, kik, avt, nbt, bqt, bk, bvt, cu, sm, smt, gates) = _in_proj(
        xt, sc1, sh1, lw["norm_mix_g"], lw["w_in"], lw["bf_bias"], cos, sin, tm)
    oa, ob, oc = _mixers((aqt, iqt, kik, avt, bqt, bk, bvt, cu, sm, smt), past, n_past, lw, bsz, t)

    def flat(a):
        return a.reshape(xt.shape[0], xt.shape[1], a.shape[-1])

    router = (lw["router_w"], lw["router_b"]) if layer % 2 else None
    res = _merge(xt, flat(oa), flat(ob), flat(oc), gates, g1, sc2, sh2, lw["norm_ffn_g"],
                 lw["w_br_a"], lw["w_br_b"], lw["w_br_c"], lw["w_out"], router, tm)
    tmf = _pick_tile(xt.shape[1], 512)
    if layer % 2 == 0:
        x_mid, h2 = res
        x_new = _ffn(x_mid, h2, g2, lw["ffn_wg"], lw["ffn_wu"], lw["ffn_wd"], final_g, tmf)
    else:
        x_mid, h2, gw = res
        x_new = _moe(x_mid, h2, g2, gw, lw["moe_wg"], lw["moe_wu"], lw["moe_wd"], final_g,
                     _pick_tile(xt.shape[1], 1024))
    def token_major(a, *feat):
        a = a.reshape(a.shape[0], *feat, -1, t) if per_token else a.reshape(a.shape[0], *feat, 1, t)
        a = jnp.moveaxis(a, (-2, -1), (1, 2))
        return a.reshape(bsz, t, *feat)

    new_a = token_major(nat, 3, HEAD_DIM)
    new_b = token_major(nbt, 2, B_HEADS, HEAD_DIM)
    new_logf = sm.reshape(bsz, t, LANES)[:, :, :B_HEADS]
    new_pool = cu.reshape(bsz, t, POOL_WIDTH)[:, t - POOL_HIST:, :]
    return x_new.reshape(bsz, t, d), (new_a, new_b, new_logf, new_pool)


def kernel(x_prompt, x_sample, cache_a_kvi, cache_b_kv, cache_b_logf, state_c_pool, c_prompt, c_sample,
           ada_w, ada_b, norm_mix_g, w_in, b_forget, pool_w, pool_scale, w_br_a, w_br_b, w_br_c, w_out,
           norm_ffn_g, ffn_w_gate, ffn_w_up, ffn_w_down, moe_router_w, moe_router_b, moe_w_gate,
           moe_w_up, moe_w_down, final_norm_g):
    depth = ada_w.shape[0]
    bp, tp, d = x_prompt.shape
    bs, ts, _ = x_sample.shape
    n_past = cache_a_kvi.shape[2]
    assert tp % KV_BLOCK == 0 and (bs * ts) % KV_BLOCK == 0 and ts <= LANES

    rows = -(-(bp + bs) // 8) * 8
    c_all = jnp.pad(jnp.concatenate([c_prompt, c_sample], axis=0), ((0, rows - bp - bs), (0, 0)))
    mod_all = _ada(c_all, ada_w, ada_b)

    tab_p = _rope_tables(jnp.arange(tp))
    tab_s = _rope_tables(n_past + jnp.arange(ts))

    xp, xs = x_prompt, x_sample
    outs_p, outs_s = [], []
    for layer in range(depth):
        j = layer // 2
        w_l, bias_l = _in_weights(w_in[layer], b_forget[layer])
        pw = pool_w[layer]
        pool_bd = jnp.zeros((POOL_WIDTH, POOL_WIDTH), F32)
        for g in range(len(POOL_WINDOWS)):
            sl = slice(g * POOL_GROUP_DIM, (g + 1) * POOL_GROUP_DIM)
            pool_bd = pool_bd.at[sl, sl].set(pw[g])
        lw = dict(w_in=w_l, bf_bias=bias_l, norm_mix_g=norm_mix_g[layer], norm_ffn_g=norm_ffn_g[layer],
                  pool_bd=pool_bd.astype(BF16), pool_scale=pool_scale[layer],
                  w_br_a=w_br_a[layer].astype(BF16), w_br_b=w_br_b[layer].astype(BF16),
                  w_br_c=w_br_c[layer].astype(BF16), w_out=w_out[layer].astype(BF16))
        if layer % 2 == 0:
            lw.update(ffn_wg=ffn_w_gate[j].astype(BF16), ffn_wu=ffn_w_up[j].astype(BF16),
                      ffn_wd=ffn_w_down[j].astype(BF16))
        else:
            rw = jnp.pad(moe_router_w[j], ((0, 0), (0, LANES - N_EXPERTS)))
            rw_hi = rw.astype(BF16)
            lw.update(router_w=jnp.stack([rw_hi, (rw - rw_hi.astype(F32)).astype(BF16)]),
                      router_b=jnp.pad(moe_router_b[j], (0, LANES - N_EXPERTS)).reshape(1, LANES),
                      moe_wg=moe_w_gate[j].astype(BF16), moe_wu=moe_w_up[j].astype(BF16),
                      moe_wd=moe_w_down[j].astype(BF16))
        final_g = final_norm_g if layer == depth - 1 else None
        mod_p = [m[:, None, :] for m in jnp.split(mod_all[layer, :bp], 6, axis=-1)]
        mod_s = [m[:, None, :] for m in jnp.split(mod_all[layer, bp:bp + bs], 6, axis=-1)]
        xp, new_p = _layer(xp, mod_p, None, 0, tab_p, lw, layer, final_g, per_token=False)
        past = (cache_a_kvi[layer], cache_b_kv[layer], cache_b_logf[layer], state_c_pool[layer])
        xs, new_s = _layer(xs, mod_s, past, n_past, tab_s, lw, layer, final_g, per_token=True)
        outs_p.append(new_p)
        outs_s.append(new_s)

    def stack(outs, k):
        return jnp.stack([o[k] for o in outs])

    return (xp, xs,
            stack(outs_p, 0), stack(outs_p, 1), stack(outs_p, 2), stack(outs_p, 3),
            stack(outs_s, 0), stack(outs_s, 1), stack(outs_s, 2), stack(outs_s, 3))
```

```python
import functools

import jax
import jax.numpy as jnp
import numpy as np
from jax import lax
from jax.experimental import pallas as pl
from jax.experimental.pallas import tpu as pltpu

F32 = jnp.float32
BF16 = jnp.bfloat16
I32 = jnp.int32

D_MODEL = 1024
CHUNK = 64
HEAD_DIM = 64
HALF = HEAD_DIM // 2
ROPE_THETA = 10000.0
NORM_EPS = 1e-6
A_HEADS = 6
IDX_HEADS = 4
TOPK_MAX = 256
B_HEADS = 6
POOL_WINDOWS = (2, 4, 8, 16)
POOL_GROUP_DIM = 64
POOL_WIDTH = 256
POOL_HIST = 15
N_EXPERTS = 8
LANES = 128
SUBLANES = 8
LOG2E = 1.4426950408889634
QK_SCALE = HEAD_DIM ** -0.5 * LOG2E
KV_BLOCK = 256
V_ROWS = HEAD_DIM + 16
MOE_CHUNK = 256
VMEM_LIMIT = 56 * 1024 * 1024
NEG_INF = float("-inf")
KEY_DT = jnp.bfloat16

C_AQ, C_IQ, C_A, C_B, C_CU, C_SM, C_GATE, C_END = 0, 384, 640, 896, 2048, 2304, 2432, 5504


def _cparams(sem):
    return pltpu.CompilerParams(dimension_semantics=sem, vmem_limit_bytes=VMEM_LIMIT)


def _const_spec(shape):
    nd = len(shape)
    return pl.BlockSpec(shape, lambda *_: (0,) * nd, pipeline_mode=pl.Buffered(1))


def _lane_iota(shape):
    return lax.broadcasted_iota(I32, shape, len(shape) - 1)


def _row_iota(shape):
    return lax.broadcasted_iota(I32, shape, len(shape) - 2)


def _ada_kernel(c_ref, w_ref, b_ref, o_ref):
    c = c_ref[...]
    s = c * jax.nn.sigmoid(c)
    o_ref[...] = jnp.dot(s, w_ref[...], preferred_element_type=F32,
                         precision=lax.Precision.HIGHEST) + b_ref[...]


def _ada(c_all, ada_w, ada_b):
    depth, d, n = ada_w.shape
    rows = c_all.shape[0]
    tn = 1536
    return pl.pallas_call(
        _ada_kernel,
        grid=(depth, n // tn),
        in_specs=[pl.BlockSpec((rows, d), lambda l, j: (0, 0)),
                  pl.BlockSpec((None, d, tn), lambda l, j: (l, 0, j)),
                  pl.BlockSpec((None, 1, tn), lambda l, j: (l, 0, j))],
        out_specs=pl.BlockSpec((None, rows, tn), lambda l, j: (l, 0, j)),
        out_shape=jax.ShapeDtypeStruct((depth, rows, n), F32),
        compiler_params=_cparams(("arbitrary", "arbitrary")),
        name="ada_mod",
    )(c_all, ada_w, ada_b.reshape(depth, 1, n))


def _in_kernel(x_ref, sc_ref, sh_ref, g_ref, w_ref, bf_ref, cos_ref, sin_ref,
               aqt_ref, iqt_ref, nat_ref, kik_ref, avt_ref, nbt_ref, bqt_ref, bk_ref, bvt_ref,
               cu_ref, sm_ref, smt_ref, gate_ref):
    x = x_ref[...]
    ms = jnp.mean(x * x, axis=-1, keepdims=True)
    y = x * lax.rsqrt(ms + NORM_EPS) * g_ref[...]
    h = (y * (1.0 + sc_ref[...]) + sh_ref[...]).astype(BF16)
    tm = x.shape[0]

    def mm(a, b):
        return jnp.dot(h, w_ref[:, a:b], preferred_element_type=F32)

    cos = cos_ref[...]
    sin = sin_ref[...]
    lane = _lane_iota((tm, LANES))
    low = lane < HEAD_DIM
    first_half = (lane & HALF) == 0

    def rope(z):
        swapped = jnp.where(first_half, pltpu.roll(z, LANES - HALF, 1), pltpu.roll(z, HALF, 1))
        return z * cos + swapped * sin

    zeros64 = jnp.zeros((HEAD_DIM, tm), BF16)
    ones_rows = jnp.where(_row_iota((V_ROWS - HEAD_DIM, tm)) == 0, 1.0, 0.0).astype(BF16)

    def put_heads(ref, zt, p, slot_even, slot_odd):
        for hh, slot in ((0, slot_even), (1, slot_odd)):
            base = (2 * p + hh) * LANES
            ref[base + slot * HEAD_DIM:base + (slot + 1) * HEAD_DIM, :] = zt[hh * HEAD_DIM:(hh + 1) * HEAD_DIM]
            ref[base + (1 - slot) * HEAD_DIM:base + (2 - slot) * HEAD_DIM, :] = zeros64

    z = mm(C_AQ, C_IQ)
    for p in range(3):
        zt = (rope(z[:, p * LANES:(p + 1) * LANES]) * QK_SCALE).T.astype(BF16)
        put_heads(aqt_ref, zt, p, 0, 0)
    z = mm(C_IQ, C_A)
    for p in range(2):
        zt = rope(z[:, p * LANES:(p + 1) * LANES]).T.astype(BF16)
        put_heads(iqt_ref, zt, p, 1, 1)

    z = mm(C_A, C_B)
    kv = z[:, :LANES]
    r0 = jnp.where(low, rope(kv), kv)
    r1 = rope(z[:, LANES:])
    r0t = r0.T
    nat_ref[0:LANES, :] = r0t
    nat_ref[LANES:, :] = r1.T[0:HEAD_DIM, :]
    kik_ref[...] = jnp.where(low, r0, pltpu.roll(r1, HEAD_DIM, 1)).astype(BF16)
    avt_ref[0:HEAD_DIM, :] = r0t[HEAD_DIM:, :].astype(BF16)
    avt_ref[HEAD_DIM:, :] = ones_rows

    z = mm(C_B, C_CU)
    bk_ref[...] = z[:, 384:768].astype(BF16)
    for p in range(3):
        zt = (z[:, p * LANES:(p + 1) * LANES] * QK_SCALE).T.astype(BF16)
        put_heads(bqt_ref, zt, p, 0, 1)
        nbt_ref[p * LANES:(p + 1) * LANES, :] = z[:, 384 + p * LANES:384 + (p + 1) * LANES].T
        vt = z[:, 768 + p * LANES:768 + (p + 1) * LANES].T
        nbt_ref[384 + p * LANES:384 + (p + 1) * LANES, :] = vt
        vt = vt.astype(BF16)
        for hh in range(2):
            bvt_ref[2 * p + hh, 0:HEAD_DIM, :] = vt[hh * HEAD_DIM:(hh + 1) * HEAD_DIM]
            bvt_ref[2 * p + hh, HEAD_DIM:, :] = ones_rows

    cu_ref[...] = mm(C_CU, C_SM)

    z = mm(C_SM, C_GATE)
    t = z + bf_ref[...]
    logf = jnp.minimum(t, 0.0) - jnp.log1p(jnp.exp(-jnp.abs(t)))
    sm = jnp.where(lane < B_HEADS, logf, z)
    sm_ref[...] = sm
    smt_ref[...] = sm.T[0:16, :]

    for c in range(3):
        gate_ref[:, c * D_MODEL:(c + 1) * D_MODEL] = jax.nn.sigmoid(
            mm(C_GATE + c * D_MODEL, C_GATE + (c + 1) * D_MODEL))


def _in_weights(w_in_l, b_forget_l):
    d = w_in_l.shape[0]
    sizes = (384, 64, 64, 256, 4, 64, 384, 384, 384, 6, 256, 3072)
    o = np.concatenate([[0], np.cumsum(sizes)])
    w_in_l = w_in_l.astype(BF16)
    cols = [w_in_l[:, o[0]:o[1]],
            w_in_l[:, o[3]:o[4]],
            w_in_l[:, o[1]:o[3]], w_in_l[:, o[5]:o[6]], jnp.zeros((d, 64), BF16),
            w_in_l[:, o[6]:o[9]],
            w_in_l[:, o[10]:o[11]],
            w_in_l[:, o[9]:o[10]], jnp.zeros((d, 2), BF16), w_in_l[:, o[4]:o[5]], jnp.zeros((d, LANES - 12), BF16),
            w_in_l[:, o[11]:o[12]]]
    w = jnp.concatenate(cols, axis=1)
    assert w.shape[1] == C_END, w.shape
    bias = jnp.concatenate([b_forget_l, jnp.zeros((LANES - B_HEADS,), F32)]).reshape(1, LANES)
    return w, bias


def _in_proj(x, sc, sh, g, w, bias, cos, sin, tm):
    bsz, t, d = x.shape
    mrows = sc.shape[1]
    mblk = 1 if mrows == 1 else tm
    mod_spec = pl.BlockSpec((None, mblk, d), (lambda b, i: (b, 0, 0)) if mrows == 1 else (lambda b, i: (b, i, 0)))
    nblk = t // tm

    def rows(n, dt):
        return pl.BlockSpec((None, tm, n), lambda b, i: (b, i, 0)), jax.ShapeDtypeStruct((bsz, t, n), dt)

    def cols(n, dt):
        return pl.BlockSpec((None, n, tm), lambda b, i: (b, 0, i)), jax.ShapeDtypeStruct((bsz, n, t), dt)

    outs = [cols(A_HEADS * LANES, BF16), cols(IDX_HEADS * LANES, BF16), cols(192, F32), rows(LANES, BF16),
            (pl.BlockSpec((None, None, V_ROWS, tm), lambda b, i: (b, i, 0, 0)),
             jax.ShapeDtypeStruct((bsz, nblk, V_ROWS, tm), BF16)),
            cols(768, F32), cols(B_HEADS * LANES, BF16), rows(384, BF16),
            (pl.BlockSpec((None, B_HEADS, None, V_ROWS, tm), lambda b, i: (b, 0, i, 0, 0)),
             jax.ShapeDtypeStruct((bsz, B_HEADS, nblk, V_ROWS, tm), BF16)),
            rows(256, F32), rows(LANES, F32), cols(16, F32), rows(3 * D_MODEL, F32)]
    return pl.pallas_call(
        _in_kernel,
        grid=(bsz, nblk),
        in_specs=[pl.BlockSpec((None, tm, d), lambda b, i: (b, i, 0)), mod_spec, mod_spec,
                  _const_spec((1, d)), _const_spec(w.shape), _const_spec((1, LANES)),
                  pl.BlockSpec((tm, LANES), lambda b, i: (i, 0)),
                  pl.BlockSpec((tm, LANES), lambda b, i: (i, 0))],
        out_specs=[o[0] for o in outs],
        out_shape=[o[1] for o in outs],
        compiler_params=_cparams(("parallel", "parallel")),
        name="in_proj",
    )(x, sc, sh, g.reshape(1, d), w, bias, cos, sin)


def _cum_kernel(x_ref, o_ref, carry_ref):
    @pl.when(pl.program_id(1) == 0)
    def _():
        carry_ref[...] = jnp.zeros_like(carry_ref)

    tc = x_ref.shape[0]
    tri = jnp.where(_lane_iota((tc, tc)) <= _row_iota((tc, tc)), 1.0, 0.0).astype(BF16)
    cum = carry_ref[0:1, :]
    rest = x_ref[...]
    for _ in range(3):
        piece = rest.astype(BF16)
        cum = cum + jnp.dot(tri, piece, preferred_element_type=F32)
        rest = rest - piece.astype(F32)
    for hd in range(B_HEADS):
        o_ref[hd] = jnp.broadcast_to(cum[:, hd:hd + 1] * LOG2E, (tc, LANES))
    carry_ref[...] = jnp.broadcast_to(cum[tc - 1:tc, :], carry_ref.shape)


def _cum_logf(x, tc):
    bsz, t, n = x.shape
    return pl.pallas_call(
        _cum_kernel,
        grid=(bsz, t // tc),
        in_specs=[pl.BlockSpec((None, tc, n), lambda b, i: (b, i, 0))],
        out_specs=pl.BlockSpec((None, B_HEADS, tc, LANES), lambda b, i: (b, 0, i, 0)),
        out_shape=jax.ShapeDtypeStruct((bsz, B_HEADS, t, LANES), F32),
        scratch_shapes=[pltpu.VMEM((SUBLANES, LANES), F32)],
        compiler_params=_cparams(("parallel", "arbitrary")),
        name="logf_cumsum",
    )(x)


def _sum_keys(x):
    part = x.reshape(x.shape[0] // SUBLANES, SUBLANES, x.shape[1]).sum(axis=0)
    return jnp.sum(part, axis=0, keepdims=True)


def _max_keys(x):
    part = x.reshape(x.shape[0] // SUBLANES, SUBLANES, x.shape[1]).max(axis=0)
    return jnp.max(part, axis=0, keepdims=True)


def _dsa_kernel(aqt_ref, iqt_ref, smt_ref, kik_ref, avt_ref, o_ref, key_ref,
                *, tq, tk, n_keys, q_pos0, n_sel):
    i = pl.program_id(1)
    pos_first = q_pos0 + i * tq
    last_chunk = (pos_first + tq - 1) // CHUNK
    n_adm = jnp.minimum((last_chunk + 1) * CHUNK, n_keys)
    nkb = (n_adm + tk - 1) // tk
    n_pairs = (nkb + 1) // 2

    q_pos = pos_first + _lane_iota((1, tq))
    q_lim = jnp.minimum((q_pos // CHUNK + 1) * CHUNK, n_keys)
    key_row = _row_iota((tk, tq))

    def keys(kb):
        return kik_ref[pl.ds(pl.multiple_of(kb * tk, tk), tk), :]

    iq4 = jnp.concatenate([iqt_ref[hd * LANES:(hd + 1) * LANES, :] for hd in range(IDX_HEADS)], axis=1)
    smt = smt_ref[...]
    w_rows = [smt[8 + hd:9 + hd, :] for hd in range(IDX_HEADS)]

    def score_body(kb, carry):
        s4 = jnp.dot(keys(kb), iq4, preferred_element_type=F32)
        score = w_rows[0] * jnp.maximum(s4[:, 0:tq], 0.0)
        for hd in range(1, IDX_HEADS):
            score = score + w_rows[hd] * jnp.maximum(s4[:, hd * tq:(hd + 1) * tq], 0.0)
        score = jnp.where(key_row < q_lim - kb * tk, score, NEG_INF)
        key_ref[kb] = score.astype(KEY_DT)
        return carry

    lax.fori_loop(0, nkb, score_body, 0)

    @pl.when(nkb % 2 == 1)
    def _():
        key_ref[nkb] = jnp.full((tk, tq), NEG_INF, KEY_DT)

    one, zero = jnp.ones((), KEY_DT), jnp.zeros((), KEY_DT)
    packed_rows = 2 * SUBLANES

    def count(cand, strict):
        def hits(blk):
            h = jnp.where((blk > cand) if strict else (blk >= cand), one, zero)
            parts = [h[r * packed_rows:(r + 1) * packed_rows] for r in range(tk // packed_rows)]
            while len(parts) > 1:
                parts = [a + b for a, b in zip(parts[::2], parts[1::2])]
            return parts[0].astype(F32)

        def body(j, acc):
            return acc + hits(key_ref[2 * j]) + hits(key_ref[2 * j + 1])

        acc = lax.fori_loop(0, n_pairs, body, jnp.zeros((packed_rows, tq), F32))
        return jnp.sum(acc, axis=0, keepdims=True)

    def pattern_value(u):
        bits = jnp.where(u >= 0x8000, u & 0x7FFF, (~u) & 0xFFFF)
        return lax.bitcast_convert_type(lax.shift_left(bits, 16), F32).astype(KEY_DT)

    def bit_body(b, u):
        cand_u = u | lax.shift_left(jnp.int32(1), 15 - b)
        cnt = count(pattern_value(cand_u), False)
        return jnp.where(cnt >= n_sel, cand_u, u)

    u_thr = jnp.maximum(lax.fori_loop(0, 16, bit_body, jnp.zeros((1, tq), I32)), 0x007F)
    thr_16 = pattern_value(u_thr)

    def finer():
        lo = thr_16.astype(F32)
        hi = jnp.where(u_thr >= 0xFF80, jnp.inf, pattern_value(u_thr + 1).astype(F32))

        def halve(_, lo_hi):
            lo, hi = lo_hi
            cand = (0.5 * lo + 0.5 * hi).astype(KEY_DT)
            enough = count(cand, False) >= n_sel
            return jnp.where(enough, cand.astype(F32), lo), jnp.where(enough, hi, cand.astype(F32))

        return lax.fori_loop(0, 16, halve, (lo, hi))[0].astype(KEY_DT)

    thr_k = lax.cond(jnp.min(n_sel - count(thr_16, True)) <= 0.0, finer, lambda: thr_16)
    need = jnp.maximum(n_sel - count(thr_k, True), 0.0)
    thr = thr_k.astype(F32)

    aq6 = jnp.concatenate([aqt_ref[hd * LANES:(hd + 1) * LANES, :] for hd in range(A_HEADS)], axis=1)

    half = tk // 2
    lower = jnp.where(_lane_iota((half, half)) <= _row_iota((half, half)), 1.0, 0.0).astype(BF16)

    def attend():
        def body(kb, carry):
            eq_seen, ms, accs = carry
            blk = key_ref[kb].astype(F32)
            eq = blk == thr
            eq_f = jnp.where(eq, 1.0, 0.0)
            prefs = []
            for e in (eq_f[:half], eq_f[half:]):
                prefs.append(jnp.dot(lower, e.astype(BF16), preferred_element_type=F32) + eq_seen)
                eq_seen = eq_seen + _sum_keys(e)
            slack = jnp.where(blk >= thr, need - jnp.where(eq, jnp.concatenate(prefs, axis=0), 0.0), -1.0)
            bias = jnp.where(slack >= 0.0, jnp.where(jnp.abs(blk) < jnp.inf, 0.0, NEG_INF), NEG_INF)
            logits = jnp.dot(keys(kb), aq6, preferred_element_type=F32)
            vts = (avt_ref[2 * kb], avt_ref[2 * kb + 1])
            new_ms, new_accs = [], []
            for p in range(A_HEADS // 2):
                ps, alphas = [], []
                for hd in (2 * p, 2 * p + 1):
                    lg = logits[:, hd * tq:(hd + 1) * tq] + bias
                    m_old = ms[hd]
                    m_new = jnp.maximum(m_old, _max_keys(lg))
                    m_safe = jnp.where(m_new == NEG_INF, 0.0, m_new)
                    ps.append(jnp.exp2(lg - m_safe).astype(BF16))
                    alphas.append(jnp.exp2(m_old - m_safe))
                    new_ms.append(m_new)
                p2 = jnp.concatenate(ps, axis=1)
                pv = (jnp.dot(vts[0], p2[:tk // 2], preferred_element_type=F32)
                      + jnp.dot(vts[1], p2[tk // 2:], preferred_element_type=F32))
                new_accs.append(jnp.concatenate(alphas, axis=1) * accs[p] + pv)
            return eq_seen, tuple(new_ms), tuple(new_accs)

        init = (jnp.zeros((1, tq), F32),
                tuple(jnp.full((1, tq), NEG_INF, F32) for _ in range(A_HEADS)),
                tuple(jnp.zeros((V_ROWS, 2 * tq), F32) for _ in range(A_HEADS // 2)))
        return lax.fori_loop(0, nkb, body, init)[2]

    accs = attend()

    outs = []
    for p in range(A_HEADS // 2):
        o2 = accs[p][0:HEAD_DIM] / accs[p][HEAD_DIM:HEAD_DIM + 1]
        outs += [o2[:, 0:tq], o2[:, tq:2 * tq]]
    o_ref[...] = jnp.concatenate(outs, axis=0).T.astype(o_ref.dtype)


def _dsa(aqt, iqt, smt, kik, avt, *, tq, n_keys, q_pos0, n_sel):
    bsz, _, t_q = aqt.shape
    _, nblk, _, tkv = avt.shape
    assert nblk % 2 == 0
    tk = 2 * tkv
    kern = functools.partial(_dsa_kernel, tq=tq, tk=tk, n_keys=n_keys, q_pos0=q_pos0, n_sel=n_sel)
    return pl.pallas_call(
        kern,
        grid=(bsz, t_q // tq),
        in_specs=[pl.BlockSpec((None, A_HEADS * LANES, tq), lambda b, i: (b, 0, i)),
                  pl.BlockSpec((None, IDX_HEADS * LANES, tq), lambda b, i: (b, 0, i)),
                  pl.BlockSpec((None, 16, tq), lambda b, i: (b, 0, i)),
                  pl.BlockSpec((None, nblk * tkv, LANES), lambda b, i: (b, 0, 0)),
                  pl.BlockSpec((None, nblk, V_ROWS, tkv), lambda b, i: (b, 0, 0, 0))],
        out_specs=pl.BlockSpec((None, tq, 384), lambda b, i: (b, i, 0)),
        out_shape=jax.ShapeDtypeStruct((bsz, t_q, 384), BF16),
        scratch_shapes=[pltpu.VMEM((2 * ((nblk // 2 + 1) // 2), tk, tq), KEY_DT)],
        compiler_params=_cparams(("parallel", "arbitrary")),
        name="dsa_attention",
    )(aqt, iqt, smt, kik, avt)


def _fox_kernel(qt_ref, k_ref, vt_ref, ck_ref, o_ref, *, tq, tk, n_keys, q_pos0):
    i = pl.program_id(1)
    pos_first = q_pos0 + i * tq
    n_full = pos_first // tk
    nkb = (jnp.minimum(pos_first + tq, n_keys) + tk - 1) // tk
    q_pos = pos_first + _lane_iota((1, tq))
    key_row = _row_iota((tk, tq))
    qts = [qt_ref[hd * LANES:(hd + 1) * LANES, :] for hd in range(B_HEADS)]

    def step(kbs, state, masked):
        logits = []
        for hd in range(B_HEADS):
            for kb in kbs:
                rows = pl.ds(pl.multiple_of(kb * tk, tk), tk)
                ck = ck_ref[hd, rows, :]
                kblk = k_ref[rows, (hd // 2) * LANES:(hd // 2 + 1) * LANES]
                lg = (jnp.dot(kblk, qts[hd], preferred_element_type=F32)
                      - jnp.concatenate([ck] * (tq // LANES), axis=1))
                if masked:
                    lg = jnp.where(key_row <= q_pos - kb * tk, lg, NEG_INF)
                logits.append(lg)
        new = []
        for hd in range(B_HEADS):
            m_old, acc = state[hd]
            lgs = logits[hd * len(kbs):(hd + 1) * len(kbs)]
            m_new = m_old
            for lg in lgs:
                m_new = jnp.maximum(m_new, _max_keys(lg))
            m_safe = jnp.where(m_new == NEG_INF, 0.0, m_new) if masked else m_new
            acc = jnp.exp2(m_old - m_safe) * acc
            for kb, lg in zip(kbs, lgs):
                p = jnp.exp2(lg - m_safe).astype(BF16)
                acc = acc + jnp.dot(vt_ref[hd, kb], p, preferred_element_type=F32)
            new.append((m_new, acc))
        return tuple(new)

    init = tuple((jnp.full((1, tq), NEG_INF, F32), jnp.zeros((V_ROWS, tq), F32)) for _ in range(B_HEADS))
    state = lax.fori_loop(0, n_full // 2, lambda j, st: step((2 * j, 2 * j + 1), st, False), init)
    state = lax.fori_loop(2 * (n_full // 2), nkb, lambda kb, st: step((kb,), st, True), state)
    outs = [acc[0:HEAD_DIM] / acc[HEAD_DIM:HEAD_DIM + 1] for _, acc in state]
    o_ref[...] = jnp.concatenate(outs, axis=0).T.astype(o_ref.dtype)


def _fox(qt, k, vt, ck, *, tq, n_keys, q_pos0):
    bsz, _, t_q = qt.shape
    _, _, nblk, _, tk = vt.shape
    lpad = nblk * tk
    kern = functools.partial(_fox_kernel, tq=tq, tk=tk, n_keys=n_keys, q_pos0=q_pos0)
    return pl.pallas_call(
        kern,
        grid=(bsz, t_q // tq),
        in_specs=[pl.BlockSpec((None, B_HEADS * LANES, tq), lambda b, i: (b, 0, i)),
                  pl.BlockSpec((None, lpad, 384), lambda b, i: (b, 0, 0)),
                  pl.BlockSpec((None, B_HEADS, nblk, V_ROWS, tk), lambda b, i: (b, 0, 0, 0, 0)),
                  pl.BlockSpec((None, B_HEADS, lpad, LANES), lambda b, i: (b, 0, 0, 0))],
        out_specs=pl.BlockSpec((None, tq, 384), lambda b, i: (b, i, 0)),
        out_shape=jax.ShapeDtypeStruct((bsz, t_q, 384), BF16),
        compiler_params=_cparams(("parallel", "arbitrary")),
        name="fox_attention",
    )(qt, k, vt, ck)


def _pool_kernel(cur_ref, prev_ref, hist_ref, w_ref, s_ref, o_ref, ext, *, tc, start_pos):
    i = pl.program_id(1)
    cur = cur_ref[...]
    ext[0:16, :] = jnp.where(i == 0, hist_ref[...], prev_ref[tc - 16:, :])
    ext[16:, :] = cur
    pos = start_pos + i * tc + lax.broadcasted_iota(I32, (tc, POOL_WIDTH), 0)
    lane = _lane_iota((tc, POOL_WIDTH))
    run = cur
    pooled = jnp.zeros_like(cur)
    k = 1
    for g, w in enumerate(POOL_WINDOWS):
        while k < w:
            run = run + ext[16 - k:16 - k + tc, :]
            k += 1
        cnt = jnp.minimum(pos + 1, w).astype(F32)
        in_group = (lane >= g * POOL_GROUP_DIM) & (lane < (g + 1) * POOL_GROUP_DIM)
        pooled = jnp.where(in_group, run / cnt, pooled)
    z = (pooled - cur).astype(BF16)
    o_ref[...] = (jnp.dot(z, w_ref[...], preferred_element_type=F32) * s_ref[...]).astype(o_ref.dtype)


def _pool(cu, hist16, w_bd, scale, *, tc, start_pos):
    bsz, t, n = cu.shape
    kern = functools.partial(_pool_kernel, tc=tc, start_pos=start_pos)
    return pl.pallas_call(
        kern,
        grid=(bsz, t // tc),
        in_specs=[pl.BlockSpec((None, tc, n), lambda b, i: (b, i, 0)),
                  pl.BlockSpec((None, tc, n), lambda b, i: (b, jnp.maximum(i - 1, 0), 0)),
                  pl.BlockSpec((None, 16, n), lambda b, i: (b, 0, 0)),
                  _const_spec((n, n)), _const_spec((1, n))],
        out_specs=pl.BlockSpec((None, tc, n), lambda b, i: (b, i, 0)),
        out_shape=jax.ShapeDtypeStruct((bsz, t, n), BF16),
        scratch_shapes=[pltpu.VMEM((16 + tc, n), F32)],
        compiler_params=_cparams(("parallel", "arbitrary")),
        name="pool_mixer",
    )(cu, cu, hist16, w_bd, scale.reshape(1, n))


def _route(logits):
    lane = _lane_iota(logits.shape).astype(F32)
    lg = jnp.where(lane < N_EXPERTS, logits, NEG_INF)
    m1 = jnp.max(lg, axis=1, keepdims=True)
    i1 = jnp.min(jnp.where(lg == m1, lane, float(LANES)), axis=1, keepdims=True)
    hot1 = lane == i1
    lg2 = jnp.where(hot1, NEG_INF, lg)
    m2 = jnp.max(lg2, axis=1, keepdims=True)
    i2 = jnp.min(jnp.where(lg2 == m2, lane, float(LANES)), axis=1, keepdims=True)
    hot2 = lane == i2
    e2 = jnp.exp(m2 - m1)
    den = 1.0 + e2
    return jnp.where(hot1, 1.0 / den, 0.0) + jnp.where(hot2, e2 / den, 0.0)


def _merge_kernel(x_ref, oa_ref, ob_ref, oc_ref, gate_ref, g1_ref, sc2_ref, sh2_ref, g_ref,
                  wa_ref, wb_ref, wc_ref, wo_ref, *rest, moe):
    if moe:
        rw_ref, rb_ref, xo_ref, h_ref, gw_ref = rest
    else:
        xo_ref, h_ref = rest
    d = D_MODEL
    merged = (gate_ref[:, 0:d] * jnp.dot(oa_ref[...], wa_ref[...], preferred_element_type=F32)
              + gate_ref[:, d:2 * d] * jnp.dot(ob_ref[...], wb_ref[...], preferred_element_type=F32)
              + gate_ref[:, 2 * d:3 * d] * jnp.dot(oc_ref[...], wc_ref[...], preferred_element_type=F32))
    x = x_ref[...] + g1_ref[...] * jnp.dot(merged.astype(BF16), wo_ref[...], preferred_element_type=F32)
    xo_ref[...] = x
    ms = jnp.mean(x * x, axis=-1, keepdims=True)
    y = x * lax.rsqrt(ms + NORM_EPS) * g_ref[...]
    h = y * (1.0 + sc2_ref[...]) + sh2_ref[...]
    h_ref[...] = h.astype(BF16)
    if moe:
        h_hi = h.astype(BF16)
        h_lo = (h - h_hi.astype(F32)).astype(BF16)
        logits = (jnp.dot(h_hi, rw_ref[0], preferred_element_type=F32)
                  + jnp.dot(h_lo, rw_ref[0], preferred_element_type=F32)
                  + jnp.dot(h_hi, rw_ref[1], preferred_element_type=F32)) + rb_ref[...]
        gw_ref[...] = _route(logits)


def _merge(x, oa, ob, oc, gates, g1, sc2, sh2, g, wa, wb, wc, wo, router, tm):
    bsz, t, d = x.shape
    mrows = g1.shape[1]
    mblk = 1 if mrows == 1 else tm
    mod_spec = pl.BlockSpec((None, mblk, d), (lambda b, i: (b, 0, 0)) if mrows == 1 else (lambda b, i: (b, i, 0)))

    def tok(n):
        return pl.BlockSpec((None, tm, n), lambda b, i: (b, i, 0))

    in_specs = [tok(d), tok(384), tok(384), tok(256), tok(3 * d), mod_spec, mod_spec, mod_spec,
                _const_spec((1, d)), _const_spec(wa.shape), _const_spec(wb.shape), _const_spec(wc.shape),
                _const_spec(wo.shape)]
    args = [x, oa, ob, oc, gates, g1, sc2, sh2, g.reshape(1, d), wa, wb, wc, wo]
    out_specs = [tok(d), tok(d)]
    out_shape = [jax.ShapeDtypeStruct((bsz, t, d), F32), jax.ShapeDtypeStruct((bsz, t, d), BF16)]
    if router is not None:
        rw, rb = router
        in_specs += [_const_spec(rw.shape), _const_spec(rb.shape)]
        args += [rw, rb]
        out_specs.append(tok(LANES))
        out_shape.append(jax.ShapeDtypeStruct((bsz, t, LANES), F32))
    return pl.pallas_call(
        functools.partial(_merge_kernel, moe=router is not None),
        grid=(bsz, t // tm),
        in_specs=in_specs, out_specs=out_specs, out_shape=out_shape,
        compiler_params=_cparams(("parallel", "parallel")),
        name="merge_out",
    )(*args)


def _final_norm(x, gain):
    ms = jnp.mean(x * x, axis=-1, keepdims=True)
    return x * lax.rsqrt(ms + NORM_EPS) * gain


def _ffn_kernel(x_ref, h_ref, g2_ref, wg_ref, wu_ref, wd_ref, *rest, n_chunks, final):
    if final:
        fg_ref, o_ref = rest
    else:
        (o_ref,) = rest
    h = h_ref[...]
    tf = wg_ref.shape[1] // n_chunks
    acc = jnp.zeros(x_ref.shape, F32)
    for c in range(n_chunks):
        gt = jnp.dot(h, wg_ref[:, c * tf:(c + 1) * tf], preferred_element_type=F32)
        up = jnp.dot(h, wu_ref[:, c * tf:(c + 1) * tf], preferred_element_type=F32)
        act = (gt * jax.nn.sigmoid(gt) * up).astype(BF16)
        acc = acc + jnp.dot(act, wd_ref[c * tf:(c + 1) * tf, :], preferred_element_type=F32)
    x = x_ref[...] + g2_ref[...] * acc
    o_ref[...] = _final_norm(x, fg_ref[...]) if final else x


def _ffn(x, h, g2, wg, wu, wd, final_g, tm):
    bsz, t, d = x.shape
    mrows = g2.shape[1]
    mblk = 1 if mrows == 1 else tm
    mod_spec = pl.BlockSpec((None, mblk, d), (lambda b, i: (b, 0, 0)) if mrows == 1 else (lambda b, i: (b, i, 0)))
    tok = pl.BlockSpec((None, tm, d), lambda b, i: (b, i, 0))
    in_specs = [tok, tok, mod_spec, _const_spec(wg.shape), _const_spec(wu.shape), _const_spec(wd.shape)]
    args = [x, h, g2, wg, wu, wd]
    if final_g is not None:
        in_specs.append(_const_spec((1, d)))
        args.append(final_g.reshape(1, d))
    return pl.pallas_call(
        functools.partial(_ffn_kernel, n_chunks=2, final=final_g is not None),
        grid=(bsz, t // tm),
        in_specs=in_specs, out_specs=tok,
        out_shape=jax.ShapeDtypeStruct((bsz, t, d), F32),
        compiler_params=_cparams(("parallel", "parallel")),
        name="ffn_dense",
    )(*args)


def _moe_kernel(x_ref, h_ref, g2_ref, gw_ref, wg_ref, wu_ref, wd_ref, *rest, final):
    if final:
        fg_ref, o_ref, acc_ref, posc_ref, posr_ref = rest
    else:
        o_ref, acc_ref, posc_ref, posr_ref = rest
    e = pl.program_id(2)
    tm = h_ref.shape[0]
    n_slabs = tm // MOE_CHUNK

    @pl.when(e == 0)
    def _():
        acc_ref[...] = jnp.zeros_like(acc_ref)
        routed = gw_ref[...] != 0.0
        r_f = jnp.where(routed, 1.0, 0.0)
        r_b = r_f.astype(BF16)
        r_t = r_f.T
        r_tb = r_t.astype(BF16)
        tok_l = _lane_iota((MOE_CHUNK, tm))
        tok_r = _row_iota((MOE_CHUNK, tm))
        rank_r = jnp.zeros((LANES, tm), F32)
        for s in range(n_slabs):
            rows = slice(s * MOE_CHUNK, (s + 1) * MOE_CHUNK)
            earlier = jnp.where(tok_l < tok_r + s * MOE_CHUNK, 1.0, 0.0).astype(BF16)
            rank_c = jnp.dot(earlier, r_b, preferred_element_type=F32)
            posc_ref[rows, :] = jnp.where(routed[rows], rank_c, -1.0)
            later = jnp.where(tok_r + s * MOE_CHUNK < tok_l, 1.0, 0.0).astype(BF16)
            rank_r = rank_r + jnp.dot(r_tb[:, rows], later, preferred_element_type=F32)
        posr_ref[...] = jnp.where(r_t != 0.0, rank_r, -1.0)

    lane_e = _lane_iota((tm, LANES)) == e
    pos_c = jnp.sum(jnp.where(lane_e, posc_ref[...], 0.0), axis=1, keepdims=True)
    gate_c = jnp.sum(jnp.where(lane_e, gw_ref[...], 0.0), axis=1, keepdims=True)
    pos_r = posr_ref[pl.ds(e, 1), :]
    n_routed = (jnp.max(pos_r) + 1.0).astype(I32)

    def run_chunk(base, n_rows):
        slot_rows = _row_iota((n_rows, tm)).astype(F32)
        slot_lanes = _lane_iota((MOE_CHUNK, n_rows)).astype(F32)
        pack = jnp.where(pos_r - base == slot_rows, 1.0, 0.0).astype(BF16)
        xc = jnp.dot(pack, h_ref[...], preferred_element_type=F32).astype(BF16)
        gt = jnp.dot(xc, wg_ref[...], preferred_element_type=F32)
        up = jnp.dot(xc, wu_ref[...], preferred_element_type=F32)
        act = (gt * jax.nn.sigmoid(gt) * up).astype(BF16)
        y = jnp.dot(act, wd_ref[...], preferred_element_type=F32).astype(BF16)
        for s in range(n_slabs):
            rows = slice(s * MOE_CHUNK, (s + 1) * MOE_CHUNK)
            unpack = jnp.where(pos_c[rows] - base == slot_lanes, 1.0, 0.0).astype(BF16)
            acc_ref[rows, :] += gate_c[rows] * jnp.dot(unpack, y, preferred_element_type=F32)

    def first(c, carry):
        run_chunk(0.0, MOE_CHUNK)
        return carry

    def later(c, carry):
        run_chunk((MOE_CHUNK + c * (MOE_CHUNK // 2)).astype(F32), MOE_CHUNK // 2)
        return carry

    lax.fori_loop(0, jnp.minimum(n_routed, 1), first, 0)
    n_later = (jnp.maximum(n_routed - MOE_CHUNK, 0) + MOE_CHUNK // 2 - 1) // (MOE_CHUNK // 2)
    lax.fori_loop(0, n_later, later, 0)

    @pl.when(e == pl.num_programs(2) - 1)
    def _():
        x = x_ref[...] + g2_ref[...] * acc_ref[...]
        o_ref[...] = _final_norm(x, fg_ref[...]) if final else x


def _moe(x, h, g2, gw, wg, wu, wd, final_g, tm):
    bsz, t, d = x.shape
    n_e, _, dff = wg.shape
    mrows = g2.shape[1]
    mblk = 1 if mrows == 1 else tm
    mod_spec = pl.BlockSpec((None, mblk, d), (lambda b, i, e: (b, 0, 0)) if mrows == 1 else (lambda b, i, e: (b, i, 0)))
    tok = pl.BlockSpec((None, tm, d), lambda b, i, e: (b, i, 0))
    tok_once = pl.BlockSpec((None, tm, d), lambda b, i, e: (b, i, 0), pipeline_mode=pl.Buffered(1))
    in_specs = [tok_once, tok, mod_spec, pl.BlockSpec((None, tm, LANES), lambda b, i, e: (b, i, 0)),
                pl.BlockSpec((None, d, dff), lambda b, i, e: (e, 0, 0)),
                pl.BlockSpec((None, d, dff), lambda b, i, e: (e, 0, 0)),
                pl.BlockSpec((None, dff, d), lambda b, i, e: (e, 0, 0))]
    args = [x, h, g2, gw, wg, wu, wd]
    if final_g is not None:
        in_specs.append(pl.BlockSpec((1, d), lambda b, i, e: (0, 0)))
        args.append(final_g.reshape(1, d))
    return pl.pallas_call(
        functools.partial(_moe_kernel, final=final_g is not None),
        grid=(bsz, t // tm, n_e),
        in_specs=in_specs, out_specs=tok,
        out_shape=jax.ShapeDtypeStruct((bsz, t, d), F32),
        scratch_shapes=[pltpu.VMEM((tm, d), F32), pltpu.VMEM((tm, LANES), F32), pltpu.VMEM((LANES, tm), F32)],
        compiler_params=_cparams(("parallel", "parallel", "arbitrary")),
        name="moe_routed",
    )(*args)


def _rope_tables(pos):
    inv = ROPE_THETA ** (-jnp.arange(HALF, dtype=F32) / HALF)
    ang = pos.astype(F32)[:, None] * inv[None, :]
    cos, sin = jnp.cos(ang), jnp.sin(ang)
    return jnp.tile(cos, (1, 4)), jnp.tile(jnp.concatenate([-sin, sin], axis=1), (1, 2))


def _pick_tile(n, pref):
    t = min(n, pref)
    while n % t:
        t //= 2
    return t


def _per_seq_cols(a, bsz, t, width):
    f = a.shape[1]
    a = jnp.moveaxis(a[0].reshape(f, bsz, t), 1, 0)
    return jnp.pad(a, ((0, 0), (0, 0), (0, width - t)))


def _value_blocks(past_vt, new_vt, bsz, t, lpad):
    lead = new_vt.shape[:-2]
    new_b = jnp.moveaxis(new_vt.reshape(*lead, V_ROWS, bsz, t), -2, 0)
    full = jnp.concatenate([past_vt, new_b], axis=-1)
    full = jnp.pad(full, [(0, 0)] * (full.ndim - 1) + [(0, lpad - full.shape[-1])])
    full = full.reshape(bsz, *lead, V_ROWS, lpad // KV_BLOCK, KV_BLOCK)
    return jnp.moveaxis(full, -2, -3)


def _with_ones_rows(vt):
    ones = jnp.ones(vt.shape[:-2] + (1, vt.shape[-1]), vt.dtype)
    zeros = jnp.zeros(vt.shape[:-2] + (V_ROWS - HEAD_DIM - 1, vt.shape[-1]), vt.dtype)
    return jnp.concatenate([vt, ones, zeros], axis=-2)


def _mixers(inp, past, n_past, lw, bsz, t):
    aqt, iqt, kik, avt, bqt, bk, bvt, cu, sm, smt = inp
    n_keys = n_past + t
    n_sel = min(TOPK_MAX, n_keys // 4)
    lpad = -(-n_keys // (2 * KV_BLOCK)) * 2 * KV_BLOCK
    if past is None:
        kik_all, avt_all, bk_all, bvt_all, logf_all = kik, avt, bk, bvt, sm
        hist16 = jnp.zeros((bsz, 16, POOL_WIDTH), F32)
        tq_a, tq_b, t_pad = _pick_tile(t, 256), _pick_tile(t, 256), t
    else:
        pa, pb, plf, pc = past
        pk, pv, pik = (pa[:, :, j].astype(BF16) for j in range(3))

        def join_rows(p, new):
            full = jnp.concatenate([p, new.reshape(bsz, t, new.shape[-1])], axis=1)
            return jnp.pad(full, ((0, 0), (0, lpad - n_keys), (0, 0)))

        kik_all = join_rows(jnp.concatenate([pk, pik], axis=-1), kik)
        avt_all = _value_blocks(_with_ones_rows(jnp.swapaxes(pv, 1, 2)), avt[0, 0], bsz, t, lpad)
        bk_all = join_rows(pb[:, :, 0].astype(BF16).reshape(bsz, n_past, 384), bk)
        pvt = jnp.transpose(pb[:, :, 1].astype(BF16), (0, 2, 3, 1))
        bvt_all = _value_blocks(_with_ones_rows(pvt), bvt[0, :, 0], bsz, t, lpad)
        logf_all = join_rows(jnp.pad(plf, ((0, 0), (0, 0), (0, LANES - B_HEADS))), sm)
        hist16 = jnp.pad(pc, ((0, 0), (1, 0), (0, 0)))
        tq_a = tq_b = t_pad = LANES
        aqt, iqt, bqt, smt = (_per_seq_cols(a, bsz, t, t_pad) for a in (aqt, iqt, bqt, smt))

    oa = _dsa(aqt, iqt, smt, kik_all, avt_all, tq=tq_a, n_keys=n_keys, q_pos0=n_past, n_sel=n_sel)
    ck = _cum_logf(logf_all, KV_BLOCK)
    ob = _fox(bqt, bk_all, bvt_all, ck, tq=tq_b, n_keys=n_keys, q_pos0=n_past)
    cu = cu.reshape(bsz, t, POOL_WIDTH)
    oc = _pool(cu, hist16, lw["pool_bd"], lw["pool_scale"], tc=_pick_tile(t, 256), start_pos=n_past)
    return oa[:, :t], ob[:, :t], oc


def _layer(x, mod, past, n_past, pos_tab, lw, layer, final_g, per_token):
    bsz, t, d = x.shape
    sh1, sc1, g1, sh2, sc2, g2 = mod
    if per_token:
        xt = x.reshape(1, bsz * t, d)
        sh1, sc1, g1, sh2, sc2, g2 = (jnp.broadcast_to(m, (bsz, t, d)).reshape(1, bsz * t, d) for m in mod)
        cos, sin = (jnp.tile(a, (bsz, 1)) for a in pos_tab)
    else:
        xt = x
        cos, sin = pos_tab
    tm = KV_BLOCK
    (aqt, iqt, nat, kik, avt, nbt, bqt, bk, bvt, cu, sm, smt, gates) = _in_proj(
        xt, sc1, sh1, lw["norm_mix_g"], lw["w_in"], lw["bf_bias"], cos, sin, tm)
    oa, ob, oc = _mixers((aqt, iqt, kik, avt, bqt, bk, bvt, cu, sm, smt), past, n_past, lw, bsz, t)

    def flat(a):
        return a.reshape(xt.shape[0], xt.shape[1], a.shape[-1])

    router = (lw["router_w"], lw["router_b"]) if layer % 2 else None
    res = _merge(xt, flat(oa), flat(ob), flat(oc), gates, g1, sc2, sh2, lw["norm_ffn_g"],
                 lw["w_br_a"], lw["w_br_b"], lw["w_br_c"], lw["w_out"], router, tm)
    tmf = _pick_tile(xt.shape[1], 512)
    if layer % 2 == 0:
        x_mid, h2 = res
        x_new = _ffn(x_mid, h2, g2, lw["ffn_wg"], lw["ffn_wu"], lw["ffn_wd"], final_g, tmf)
    else:
        x_mid, h2, gw = res
        x_new = _moe(x_mid, h2, g2, gw, lw["moe_wg"], lw["moe_wu"], lw["moe_wd"], final_g,
                     _pick_tile(xt.shape[1], 1024))
    def token_major(a, *feat):
        a = a.reshape(a.shape[0], *feat, -1, t) if per_token else a.reshape(a.shape[0], *feat, 1, t)
        a = jnp.moveaxis(a, (-2, -1), (1, 2))
        return a.reshape(bsz, t, *feat)

    new_a = token_major(nat, 3, HEAD_DIM)
    new_b = token_major(nbt, 2, B_HEADS, HEAD_DIM)
    new_logf = sm.reshape(bsz, t, LANES)[:, :, :B_HEADS]
    new_pool = cu.reshape(bsz, t, POOL_WIDTH)[:, t - POOL_HIST:, :]
    return x_new.reshape(bsz, t, d), (new_a, new_b, new_logf, new_pool)


def kernel(x_prompt, x_sample, cache_a_kvi, cache_b_kv, cache_b_logf, state_c_pool, c_prompt, c_sample,
           ada_w, ada_b, norm_mix_g, w_in, b_forget, pool_w, pool_scale, w_br_a, w_br_b, w_br_c, w_out,
           norm_ffn_g, ffn_w_gate, ffn_w_up, ffn_w_down, moe_router_w, moe_router_b, moe_w_gate,
           moe_w_up, moe_w_down, final_norm_g):
    depth = ada_w.shape[0]
    bp, tp, d = x_prompt.shape
    bs, ts, _ = x_sample.shape
    n_past = cache_a_kvi.shape[2]
    assert tp % KV_BLOCK == 0 and (bs * ts) % KV_BLOCK == 0 and ts <= LANES

    rows = -(-(bp + bs) // 8) * 8
    c_all = jnp.pad(jnp.concatenate([c_prompt, c_sample], axis=0), ((0, rows - bp - bs), (0, 0)))
    mod_all = _ada(c_all, ada_w, ada_b)

    tab_p = _rope_tables(jnp.arange(tp))
    tab_s = _rope_tables(n_past + jnp.arange(ts))

    xp, xs = x_prompt, x_sample
    outs_p, outs_s = [], []
    for layer in range(depth):
        j = layer // 2
        w_l, bias_l = _in_weights(w_in[layer], b_forget[layer])
        pw = pool_w[layer]
        pool_bd = jnp.zeros((POOL_WIDTH, POOL_WIDTH), F32)
        for g in range(len(POOL_WINDOWS)):
            sl = slice(g * POOL_GROUP_DIM, (g + 1) * POOL_GROUP_DIM)
            pool_bd = pool_bd.at[sl, sl].set(pw[g])
        lw = dict(w_in=w_l, bf_bias=bias_l, norm_mix_g=norm_mix_g[layer], norm_ffn_g=norm_ffn_g[layer],
                  pool_bd=pool_bd.astype(BF16), pool_scale=pool_scale[layer],
                  w_br_a=w_br_a[layer].astype(BF16), w_br_b=w_br_b[layer].astype(BF16),
                  w_br_c=w_br_c[layer].astype(BF16), w_out=w_out[layer].astype(BF16))
        if layer % 2 == 0:
            lw.update(ffn_wg=ffn_w_gate[j].astype(BF16), ffn_wu=ffn_w_up[j].astype(BF16),
                      ffn_wd=ffn_w_down[j].astype(BF16))
        else:
            rw = jnp.pad(moe_router_w[j], ((0, 0), (0, LANES - N_EXPERTS)))
            rw_hi = rw.astype(BF16)
            lw.update(router_w=jnp.stack([rw_hi, (rw - rw_hi.astype(F32)).astype(BF16)]),
                      router_b=jnp.pad(moe_router_b[j], (0, LANES - N_EXPERTS)).reshape(1, LANES),
                      moe_wg=moe_w_gate[j].astype(BF16), moe_wu=moe_w_up[j].astype(BF16),
                      moe_wd=moe_w_down[j].astype(BF16))
        final_g = final_norm_g if layer == depth - 1 else None
        mod_p = [m[:, None, :] for m in jnp.split(mod_all[layer, :bp], 6, axis=-1)]
        mod_s = [m[:, None, :] for m in jnp.split(mod_all[layer, bp:bp + bs], 6, axis=-1)]
        xp, new_p = _layer(xp, mod_p, None, 0, tab_p, lw, layer, final_g, per_token=False)
        past = (cache_a_kvi[layer], cache_b_kv[layer], cache_b_logf[layer], state_c_pool[layer])
        xs, new_s = _layer(xs, mod_s, past, n_past, tab_s, lw, layer, final_g, per_token=True)
        outs_p.append(new_p)
        outs_s.append(new_s)

    def stack(outs, k):
        return jnp.stack([o[k] for o in outs])

    return (xp, xs,
            stack(outs_p, 0), stack(outs_p, 1), stack(outs_p, 2), stack(outs_p, 3),
            stack(outs_s, 0), stack(outs_s, 1), stack(outs_s, 2), stack(outs_s, 3))
```

```python
import functools

import jax
import jax.numpy as jnp
import numpy as np
from jax import lax
from jax.experimental import pallas as pl
from jax.experimental.pallas import tpu as pltpu

F32 = jnp.float32
BF16 = jnp.bfloat16
I32 = jnp.int32

D_MODEL = 1024
CHUNK = 64
HEAD_DIM = 64
HALF = HEAD_DIM // 2
ROPE_THETA = 10000.0
NORM_EPS = 1e-6
A_HEADS = 6
IDX_HEADS = 4
TOPK_MAX = 256
B_HEADS = 6
POOL_WINDOWS = (2, 4, 8, 16)
POOL_GROUP_DIM = 64
POOL_WIDTH = 256
POOL_HIST = 15
N_EXPERTS = 8
LANES = 128
SUBLANES = 8
LOG2E = 1.4426950408889634
QK_SCALE = HEAD_DIM ** -0.5 * LOG2E
KV_BLOCK = 256
V_ROWS = HEAD_DIM + 16
MOE_CHUNK = 256
VMEM_LIMIT = 56 * 1024 * 1024
NEG_INF = float("-inf")
KEY_DT = jnp.bfloat16

C_AQ, C_IQ, C_A, C_B, C_CU, C_SM, C_GATE, C_END = 0, 384, 640, 896, 2048, 2304, 2432, 5504


def _cparams(sem):
    return pltpu.CompilerParams(dimension_semantics=sem, vmem_limit_bytes=VMEM_LIMIT)


def _const_spec(shape):
    nd = len(shape)
    return pl.BlockSpec(shape, lambda *_: (0,) * nd, pipeline_mode=pl.Buffered(1))


def _lane_iota(shape):
    return lax.broadcasted_iota(I32, shape, len(shape) - 1)


def _row_iota(shape):
    return lax.broadcasted_iota(I32, shape, len(shape) - 2)


def _ada_kernel(c_ref, w_ref, b_ref, o_ref):
    c = c_ref[...]
    s = c * jax.nn.sigmoid(c)
    o_ref[...] = jnp.dot(s, w_ref[...], preferred_element_type=F32,
                         precision=lax.Precision.HIGHEST) + b_ref[...]


def _ada(c_all, ada_w, ada_b):
    depth, d, n = ada_w.shape
    rows = c_all.shape[0]
    tn = 1536
    return pl.pallas_call(
        _ada_kernel,
        grid=(depth, n // tn),
        in_specs=[pl.BlockSpec((rows, d), lambda l, j: (0, 0)),
                  pl.BlockSpec((None, d, tn), lambda l, j: (l, 0, j)),
                  pl.BlockSpec((None, 1, tn), lambda l, j: (l, 0, j))],
        out_specs=pl.BlockSpec((None, rows, tn), lambda l, j: (l, 0, j)),
        out_shape=jax.ShapeDtypeStruct((depth, rows, n), F32),
        compiler_params=_cparams(("arbitrary", "arbitrary")),
        name="ada_mod",
    )(c_all, ada_w, ada_b.reshape(depth, 1, n))


def _in_kernel(x_ref, sc_ref, sh_ref, g_ref, w_ref, bf_ref, cos_ref, sin_ref,
               aqt_ref, iqt_ref, nat_ref, kik_ref, avt_ref, nbt_ref, bqt_ref, bk_ref, bvt_ref,
               cu_ref, sm_ref, smt_ref, gate_ref):
    x = x_ref[...]
    ms = jnp.mean(x * x, axis=-1, keepdims=True)
    y = x * lax.rsqrt(ms + NORM_EPS) * g_ref[...]
    h = (y * (1.0 + sc_ref[...]) + sh_ref[...]).astype(BF16)
    tm = x.shape[0]

    def mm(a, b):
        return jnp.dot(h, w_ref[:, a:b], preferred_element_type=F32)

    cos = cos_ref[...]
    sin = sin_ref[...]
    lane = _lane_iota((tm, LANES))
    low = lane < HEAD_DIM
    first_half = (lane & HALF) == 0

    def rope(z):
        swapped = jnp.where(first_half, pltpu.roll(z, LANES - HALF, 1), pltpu.roll(z, HALF, 1))
        return z * cos + swapped * sin

    zeros64 = jnp.zeros((HEAD_DIM, tm), BF16)
    ones_rows = jnp.where(_row_iota((V_ROWS - HEAD_DIM, tm)) == 0, 1.0, 0.0).astype(BF16)

    def put_heads(ref, zt, p, slot_even, slot_odd):
        for hh, slot in ((0, slot_even), (1, slot_odd)):
            base = (2 * p + hh) * LANES
            ref[base + slot * HEAD_DIM:base + (slot + 1) * HEAD_DIM, :] = zt[hh * HEAD_DIM:(hh + 1) * HEAD_DIM]
            ref[base + (1 - slot) * HEAD_DIM:base + (2 - slot) * HEAD_DIM, :] = zeros64

    z = mm(C_AQ, C_IQ)
    for p in range(3):
        zt = (rope(z[:, p * LANES:(p + 1) * LANES]) * QK_SCALE).T.astype(BF16)
        put_heads(aqt_ref, zt, p, 0, 0)
    z = mm(C_IQ, C_A)
    for p in range(2):
        zt = rope(z[:, p * LANES:(p + 1) * LANES]).T.astype(BF16)
        put_heads(iqt_ref, zt, p, 1, 1)

    z = mm(C_A, C_B)
    kv = z[:, :LANES]
    r0 = jnp.where(low, rope(kv), kv)
    r1 = rope(z[:, LANES:])
    r0t = r0.T
    nat_ref[0:LANES, :] = r0t
    nat_ref[LANES:, :] = r1.T[0:HEAD_DIM, :]
    kik_ref[...] = jnp.where(low, r0, pltpu.roll(r1, HEAD_DIM, 1)).astype(BF16)
    avt_ref[0:HEAD_DIM, :] = r0t[HEAD_DIM:, :].astype(BF16)
    avt_ref[HEAD_DIM:, :] = ones_rows

    z = mm(C_B, C_CU)
    bk_ref[...] = z[:, 384:768].astype(BF16)
    for p in range(3):
        zt = (z[:, p * LANES:(p + 1) * LANES] * QK_SCALE).T.astype(BF16)
        put_heads(bqt_ref, zt, p, 0, 1)
        nbt_ref[p * LANES:(p + 1) * LANES, :] = z[:, 384 + p * LANES:384 + (p + 1) * LANES].T
        vt = z[:, 768 + p * LANES:768 + (p + 1) * LANES].T
        nbt_ref[384 + p * LANES:384 + (p + 1) * LANES, :] = vt
        vt = vt.astype(BF16)
        for hh in range(2):
            bvt_ref[2 * p + hh, 0:HEAD_DIM, :] = vt[hh * HEAD_DIM:(hh + 1) * HEAD_DIM]
            bvt_ref[2 * p + hh, HEAD_DIM:, :] = ones_rows

    cu_ref[...] = mm(C_CU, C_SM)

    z = mm(C_SM, C_GATE)
    t = z + bf_ref[...]
    logf = jnp.minimum(t, 0.0) - jnp.log1p(jnp.exp(-jnp.abs(t)))
    sm = jnp.where(lane < B_HEADS, logf, z)
    sm_ref[...] = sm
    smt_ref[...] = sm.T[0:16, :]

    for c in range(3):
        gate_ref[:, c * D_MODEL:(c + 1) * D_MODEL] = jax.nn.sigmoid(
            mm(C_GATE + c * D_MODEL, C_GATE + (c + 1) * D_MODEL))


def _in_weights(w_in_l, b_forget_l):
    d = w_in_l.shape[0]
    sizes = (384, 64, 64, 256, 4, 64, 384, 384, 384, 6, 256, 3072)
    o = np.concatenate([[0], np.cumsum(sizes)])
    w_in_l = w_in_l.astype(BF16)
    cols = [w_in_l[:, o[0]:o[1]],
            w_in_l[:, o[3]:o[4]],
            w_in_l[:, o[1]:o[3]], w_in_l[:, o[5]:o[6]], jnp.zeros((d, 64), BF16),
            w_in_l[:, o[6]:o[9]],
            w_in_l[:, o[10]:o[11]],
            w_in_l[:, o[9]:o[10]], jnp.zeros((d, 2), BF16), w_in_l[:, o[4]:o[5]], jnp.zeros((d, LANES - 12), BF16),
            w_in_l[:, o[11]:o[12]]]
    w = jnp.concatenate(cols, axis=1)
    assert w.shape[1] == C_END, w.shape
    bias = jnp.concatenate([b_forget_l, jnp.zeros((LANES - B_HEADS,), F32)]).reshape(1, LANES)
    return w, bias


def _in_proj(x, sc, sh, g, w, bias, cos, sin, tm):
    bsz, t, d = x.shape
    mrows = sc.shape[1]
    mblk = 1 if mrows == 1 else tm
    mod_spec = pl.BlockSpec((None, mblk, d), (lambda b, i: (b, 0, 0)) if mrows == 1 else (lambda b, i: (b, i, 0)))
    nblk = t // tm

    def rows(n, dt):
        return pl.BlockSpec((None, tm, n), lambda b, i: (b, i, 0)), jax.ShapeDtypeStruct((bsz, t, n), dt)

    def cols(n, dt):
        return pl.BlockSpec((None, n, tm), lambda b, i: (b, 0, i)), jax.ShapeDtypeStruct((bsz, n, t), dt)

    outs = [cols(A_HEADS * LANES, BF16), cols(IDX_HEADS * LANES, BF16), cols(192, F32), rows(LANES, BF16),
            (pl.BlockSpec((None, None, V_ROWS, tm), lambda b, i: (b, i, 0, 0)),
             jax.ShapeDtypeStruct((bsz, nblk, V_ROWS, tm), BF16)),
            cols(768, F32), cols(B_HEADS * LANES, BF16), rows(384, BF16),
            (pl.BlockSpec((None, B_HEADS, None, V_ROWS, tm), lambda b, i: (b, 0, i, 0, 0)),
             jax.ShapeDtypeStruct((bsz, B_HEADS, nblk, V_ROWS, tm), BF16)),
            rows(256, F32), rows(LANES, F32), cols(16, F32), rows(3 * D_MODEL, F32)]
    return pl.pallas_call(
        _in_kernel,
        grid=(bsz, nblk),
        in_specs=[pl.BlockSpec((None, tm, d), lambda b, i: (b, i, 0)), mod_spec, mod_spec,
                  _const_spec((1, d)), _const_spec(w.shape), _const_spec((1, LANES)),
                  pl.BlockSpec((tm, LANES), lambda b, i: (i, 0)),
                  pl.BlockSpec((tm, LANES), lambda b, i: (i, 0))],
        out_specs=[o[0] for o in outs],
        out_shape=[o[1] for o in outs],
        compiler_params=_cparams(("parallel", "parallel")),
        name="in_proj",
    )(x, sc, sh, g.reshape(1, d), w, bias, cos, sin)


def _cum_kernel(x_ref, o_ref, carry_ref):
    @pl.when(pl.program_id(1) == 0)
    def _():
        carry_ref[...] = jnp.zeros_like(carry_ref)

    tc = x_ref.shape[0]
    tri = jnp.where(_lane_iota((tc, tc)) <= _row_iota((tc, tc)), 1.0, 0.0).astype(BF16)
    cum = carry_ref[0:1, :]
    rest = x_ref[...]
    for _ in range(3):
        piece = rest.astype(BF16)
        cum = cum + jnp.dot(tri, piece, preferred_element_type=F32)
        rest = rest - piece.astype(F32)
    for hd in range(B_HEADS):
        o_ref[hd] = jnp.broadcast_to(cum[:, hd:hd + 1] * LOG2E, (tc, LANES))
    carry_ref[...] = jnp.broadcast_to(cum[tc - 1:tc, :], carry_ref.shape)


def _cum_logf(x, tc):
    bsz, t, n = x.shape
    return pl.pallas_call(
        _cum_kernel,
        grid=(bsz, t // tc),
        in_specs=[pl.BlockSpec((None, tc, n), lambda b, i: (b, i, 0))],
        out_specs=pl.BlockSpec((None, B_HEADS, tc, LANES), lambda b, i: (b, 0, i, 0)),
        out_shape=jax.ShapeDtypeStruct((bsz, B_HEADS, t, LANES), F32),
        scratch_shapes=[pltpu.VMEM((SUBLANES, LANES), F32)],
        compiler_params=_cparams(("parallel", "arbitrary")),
        name="logf_cumsum",
    )(x)


def _sum_keys(x):
    part = x.reshape(x.shape[0] // SUBLANES, SUBLANES, x.shape[1]).sum(axis=0)
    return jnp.sum(part, axis=0, keepdims=True)


def _max_keys(x):
    part = x.reshape(x.shape[0] // SUBLANES, SUBLANES, x.shape[1]).max(axis=0)
    return jnp.max(part, axis=0, keepdims=True)


def _dsa_kernel(aqt_ref, iqt_ref, smt_ref, kik_ref, avt_ref, o_ref, key_ref,
                *, tq, tk, n_keys, q_pos0, n_sel):
    i = pl.program_id(1)
    pos_first = q_pos0 + i * tq
    last_chunk = (pos_first + tq - 1) // CHUNK
    n_adm = jnp.minimum((last_chunk + 1) * CHUNK, n_keys)
    nkb = (n_adm + tk - 1) // tk
    n_pairs = (nkb + 1) // 2

    q_pos = pos_first + _lane_iota((1, tq))
    q_lim = jnp.minimum((q_pos // CHUNK + 1) * CHUNK, n_keys)
    key_row = _row_iota((tk, tq))

    def keys(kb):
        return kik_ref[pl.ds(pl.multiple_of(kb * tk, tk), tk), :]

    iq4 = jnp.concatenate([iqt_ref[hd * LANES:(hd + 1) * LANES, :] for hd in range(IDX_HEADS)], axis=1)
    smt = smt_ref[...]
    w_rows = [smt[8 + hd:9 + hd, :] for hd in range(IDX_HEADS)]

    def score_body(kb, carry):
        s4 = jnp.dot(keys(kb), iq4, preferred_element_type=F32)
        score = w_rows[0] * jnp.maximum(s4[:, 0:tq], 0.0)
        for hd in range(1, IDX_HEADS):
            score = score + w_rows[hd] * jnp.maximum(s4[:, hd * tq:(hd + 1) * tq], 0.0)
        score = jnp.where(key_row < q_lim - kb * tk, score, NEG_INF)
        key_ref[kb] = score.astype(KEY_DT)
        return carry

    lax.fori_loop(0, nkb, score_body, 0)

    @pl.when(nkb % 2 == 1)
    def _():
        key_ref[nkb] = jnp.full((tk, tq), NEG_INF, KEY_DT)

    one, zero = jnp.ones((), KEY_DT), jnp.zeros((), KEY_DT)
    packed_rows = 2 * SUBLANES

    def count(cand, strict):
        def hits(blk):
            h = jnp.where((blk > cand) if strict else (blk >= cand), one, zero)
            parts = [h[r * packed_rows:(r + 1) * packed_rows] for r in range(tk // packed_rows)]
            while len(parts) > 1:
                parts = [a + b for a, b in zip(parts[::2], parts[1::2])]
            return parts[0].astype(F32)

        def body(j, acc):
            return acc + hits(key_ref[2 * j]) + hits(key_ref[2 * j + 1])

        acc = lax.fori_loop(0, n_pairs, body, jnp.zeros((packed_rows, tq), F32))
        return jnp.sum(acc, axis=0, keepdims=True)

    def pattern_value(u):
        bits = jnp.where(u >= 0x8000, u & 0x7FFF, (~u) & 0xFFFF)
        return lax.bitcast_convert_type(lax.shift_left(bits, 16), F32).astype(KEY_DT)

    def bit_body(b, u):
        cand_u = u | lax.shift_left(jnp.int32(1), 15 - b)
        cnt = count(pattern_value(cand_u), False)
        return jnp.where(cnt >= n_sel, cand_u, u)

    u_thr = jnp.maximum(lax.fori_loop(0, 16, bit_body, jnp.zeros((1, tq), I32)), 0x007F)
    thr_16 = pattern_value(u_thr)

    def finer():
        lo = thr_16.astype(F32)
        hi = jnp.where(u_thr >= 0xFF80, jnp.inf, pattern_value(u_thr + 1).astype(F32))

        def halve(_, lo_hi):
            lo, hi = lo_hi
            cand = (0.5 * lo + 0.5 * hi).astype(KEY_DT)
            enough = count(cand, False) >= n_sel
            return jnp.where(enough, cand.astype(F32), lo), jnp.where(enough, hi, cand.astype(F32))

        thr_f = lax.fori_loop(0, 16, halve, (lo, hi))[0].astype(KEY_DT)
        return thr_f, count(thr_f, True)

    above_16 = count(thr_16, True)
    thr_k, above = lax.cond(jnp.min(n_sel - above_16) <= 0.0, finer, lambda: (thr_16, above_16))
    need = jnp.maximum(n_sel - above, 0.0)
    thr = thr_k.astype(F32)

    aq6 = jnp.concatenate([aqt_ref[hd * LANES:(hd + 1) * LANES, :] for hd in range(A_HEADS)], axis=1)

    half = tk // 2
    lower = jnp.where(_lane_iota((half, half)) <= _row_iota((half, half)), 1.0, 0.0).astype(BF16)

    def attend():
        def body(kb, carry):
            eq_seen, ms, accs = carry
            blk = key_ref[kb].astype(F32)
            eq = blk == thr
            eq_f = jnp.where(eq, 1.0, 0.0)
            prefs = []
            for e in (eq_f[:half], eq_f[half:]):
                prefs.append(jnp.dot(lower, e.astype(BF16), preferred_element_type=F32) + eq_seen)
                eq_seen = eq_seen + _sum_keys(e)
            slack = jnp.where(blk >= thr, need - jnp.where(eq, jnp.concatenate(prefs, axis=0), 0.0), -1.0)
            bias = jnp.where(slack >= 0.0, jnp.where(jnp.abs(blk) < jnp.inf, 0.0, NEG_INF), NEG_INF)
            logits = jnp.dot(keys(kb), aq6, preferred_element_type=F32)
            vts = (avt_ref[2 * kb], avt_ref[2 * kb + 1])
            new_ms, new_accs = [], []
            for p in range(A_HEADS // 2):
                ps, alphas = [], []
                for hd in (2 * p, 2 * p + 1):
                    lg = logits[:, hd * tq:(hd + 1) * tq] + bias
                    m_old = ms[hd]
                    m_new = jnp.maximum(m_old, _max_keys(lg))
                    m_safe = jnp.where(m_new == NEG_INF, 0.0, m_new)
                    ps.append(jnp.exp2(lg - m_safe).astype(BF16))
                    alphas.append(jnp.exp2(m_old - m_safe))
                    new_ms.append(m_new)
                p2 = jnp.concatenate(ps, axis=1)
                pv = (jnp.dot(vts[0], p2[:tk // 2], preferred_element_type=F32)
                      + jnp.dot(vts[1], p2[tk // 2:], preferred_element_type=F32))
                new_accs.append(jnp.concatenate(alphas, axis=1) * accs[p] + pv)
            return eq_seen, tuple(new_ms), tuple(new_accs)

        init = (jnp.zeros((1, tq), F32),
                tuple(jnp.full((1, tq), NEG_INF, F32) for _ in range(A_HEADS)),
                tuple(jnp.zeros((V_ROWS, 2 * tq), F32) for _ in range(A_HEADS // 2)))
        return lax.fori_loop(0, nkb, body, init)[2]

    accs = attend()

    outs = []
    for p in range(A_HEADS // 2):
        o2 = accs[p][0:HEAD_DIM] / accs[p][HEAD_DIM:HEAD_DIM + 1]
        outs += [o2[:, 0:tq], o2[:, tq:2 * tq]]
    o_ref[...] = jnp.concatenate(outs, axis=0).T.astype(o_ref.dtype)


def _dsa(aqt, iqt, smt, kik, avt, *, tq, n_keys, q_pos0, n_sel):
    bsz, _, t_q = aqt.shape
    _, nblk, _, tkv = avt.shape
    assert nblk % 2 == 0
    tk = 2 * tkv
    kern = functools.partial(_dsa_kernel, tq=tq, tk=tk, n_keys=n_keys, q_pos0=q_pos0, n_sel=n_sel)
    return pl.pallas_call(
        kern,
        grid=(bsz, t_q // tq),
        in_specs=[pl.BlockSpec((None, A_HEADS * LANES, tq), lambda b, i: (b, 0, i)),
                  pl.BlockSpec((None, IDX_HEADS * LANES, tq), lambda b, i: (b, 0, i)),
                  pl.BlockSpec((None, 16, tq), lambda b, i: (b, 0, i)),
                  pl.BlockSpec((None, nblk * tkv, LANES), lambda b, i: (b, 0, 0)),
                  pl.BlockSpec((None, nblk, V_ROWS, tkv), lambda b, i: (b, 0, 0, 0))],
        out_specs=pl.BlockSpec((None, tq, 384), lambda b, i: (b, i, 0)),
        out_shape=jax.ShapeDtypeStruct((bsz, t_q, 384), BF16),
        scratch_shapes=[pltpu.VMEM((2 * ((nblk // 2 + 1) // 2), tk, tq), KEY_DT)],
        compiler_params=_cparams(("parallel", "arbitrary")),
        name="dsa_attention",
    )(aqt, iqt, smt, kik, avt)


def _fox_kernel(qt_ref, k_ref, vt_ref, ck_ref, o_ref, *, tq, tk, n_keys, q_pos0):
    i = pl.program_id(1)
    pos_first = q_pos0 + i * tq
    n_full = pos_first // tk
    nkb = (jnp.minimum(pos_first + tq, n_keys) + tk - 1) // tk
    q_pos = pos_first + _lane_iota((1, tq))
    key_row = _row_iota((tk, tq))
    qts = [qt_ref[hd * LANES:(hd + 1) * LANES, :] for hd in range(B_HEADS)]

    def step(kbs, state, masked):
        logits = []
        for hd in range(B_HEADS):
            for kb in kbs:
                rows = pl.ds(pl.multiple_of(kb * tk, tk), tk)
                ck = ck_ref[hd, rows, :]
                kblk = k_ref[rows, (hd // 2) * LANES:(hd // 2 + 1) * LANES]
                lg = (jnp.dot(kblk, qts[hd], preferred_element_type=F32)
                      - jnp.concatenate([ck] * (tq // LANES), axis=1))
                if masked:
                    lg = jnp.where(key_row <= q_pos - kb * tk, lg, NEG_INF)
                logits.append(lg)
        new = []
        for hd in range(B_HEADS):
            m_old, acc = state[hd]
            lgs = logits[hd * len(kbs):(hd + 1) * len(kbs)]
            m_new = m_old
            for lg in lgs:
                m_new = jnp.maximum(m_new, _max_keys(lg))
            m_safe = jnp.where(m_new == NEG_INF, 0.0, m_new) if masked else m_new
            acc = jnp.exp2(m_old - m_safe) * acc
            for kb, lg in zip(kbs, lgs):
                p = jnp.exp2(lg - m_safe).astype(BF16)
                acc = acc + jnp.dot(vt_ref[hd, kb], p, preferred_element_type=F32)
            new.append((m_new, acc))
        return tuple(new)

    init = tuple((jnp.full((1, tq), NEG_INF, F32), jnp.zeros((V_ROWS, tq), F32)) for _ in range(B_HEADS))
    state = lax.fori_loop(0, n_full // 2, lambda j, st: step((2 * j, 2 * j + 1), st, False), init)
    state = lax.fori_loop(2 * (n_full // 2), nkb, lambda kb, st: step((kb,), st, True), state)
    outs = [acc[0:HEAD_DIM] / acc[HEAD_DIM:HEAD_DIM + 1] for _, acc in state]
    o_ref[...] = jnp.concatenate(outs, axis=0).T.astype(o_ref.dtype)


def _fox(qt, k, vt, ck, *, tq, n_keys, q_pos0):
    bsz, _, t_q = qt.shape
    _, _, nblk, _, tk = vt.shape
    lpad = nblk * tk
    kern = functools.partial(_fox_kernel, tq=tq, tk=tk, n_keys=n_keys, q_pos0=q_pos0)
    return pl.pallas_call(
        kern,
        grid=(bsz, t_q // tq),
        in_specs=[pl.BlockSpec((None, B_HEADS * LANES, tq), lambda b, i: (b, 0, i)),
                  pl.BlockSpec((None, lpad, 384), lambda b, i: (b, 0, 0)),
                  pl.BlockSpec((None, B_HEADS, nblk, V_ROWS, tk), lambda b, i: (b, 0, 0, 0, 0)),
                  pl.BlockSpec((None, B_HEADS, lpad, LANES), lambda b, i: (b, 0, 0, 0))],
        out_specs=pl.BlockSpec((None, tq, 384), lambda b, i: (b, i, 0)),
        out_shape=jax.ShapeDtypeStruct((bsz, t_q, 384), BF16),
        compiler_params=_cparams(("parallel", "arbitrary")),
        name="fox_attention",
    )(qt, k, vt, ck)


def _pool_kernel(cur_ref, prev_ref, hist_ref, w_ref, s_ref, o_ref, ext, *, tc, start_pos):
    i = pl.program_id(1)
    cur = cur_ref[...]
    ext[0:16, :] = jnp.where(i == 0, hist_ref[...], prev_ref[tc - 16:, :])
    ext[16:, :] = cur
    pos = start_pos + i * tc + lax.broadcasted_iota(I32, (tc, POOL_WIDTH), 0)
    lane = _lane_iota((tc, POOL_WIDTH))
    run = cur
    pooled = jnp.zeros_like(cur)
    k = 1
    for g, w in enumerate(POOL_WINDOWS):
        while k < w:
            run = run + ext[16 - k:16 - k + tc, :]
            k += 1
        cnt = jnp.minimum(pos + 1, w).astype(F32)
        in_group = (lane >= g * POOL_GROUP_DIM) & (lane < (g + 1) * POOL_GROUP_DIM)
        pooled = jnp.where(in_group, run / cnt, pooled)
    z = (pooled - cur).astype(BF16)
    o_ref[...] = (jnp.dot(z, w_ref[...], preferred_element_type=F32) * s_ref[...]).astype(o_ref.dtype)


def _pool(cu, hist16, w_bd, scale, *, tc, start_pos):
    bsz, t, n = cu.shape
    kern = functools.partial(_pool_kernel, tc=tc, start_pos=start_pos)
    return pl.pallas_call(
        kern,
        grid=(bsz, t // tc),
        in_specs=[pl.BlockSpec((None, tc, n), lambda b, i: (b, i, 0)),
                  pl.BlockSpec((None, tc, n), lambda b, i: (b, jnp.maximum(i - 1, 0), 0)),
                  pl.BlockSpec((None, 16, n), lambda b, i: (b, 0, 0)),
                  _const_spec((n, n)), _const_spec((1, n))],
        out_specs=pl.BlockSpec((None, tc, n), lambda b, i: (b, i, 0)),
        out_shape=jax.ShapeDtypeStruct((bsz, t, n), BF16),
        scratch_shapes=[pltpu.VMEM((16 + tc, n), F32)],
        compiler_params=_cparams(("parallel", "arbitrary")),
        name="pool_mixer",
    )(cu, cu, hist16, w_bd, scale.reshape(1, n))


def _route(logits):
    lane = _lane_iota(logits.shape).astype(F32)
    lg = jnp.where(lane < N_EXPERTS, logits, NEG_INF)
    m1 = jnp.max(lg, axis=1, keepdims=True)
    i1 = jnp.min(jnp.where(lg == m1, lane, float(LANES)), axis=1, keepdims=True)
    hot1 = lane == i1
    lg2 = jnp.where(hot1, NEG_INF, lg)
    m2 = jnp.max(lg2, axis=1, keepdims=True)
    i2 = jnp.min(jnp.where(lg2 == m2, lane, float(LANES)), axis=1, keepdims=True)
    hot2 = lane == i2
    e2 = jnp.exp(m2 - m1)
    den = 1.0 + e2
    return jnp.where(hot1, 1.0 / den, 0.0) + jnp.where(hot2, e2 / den, 0.0)


def _merge_kernel(x_ref, oa_ref, ob_ref, oc_ref, gate_ref, g1_ref, sc2_ref, sh2_ref, g_ref,
                  wa_ref, wb_ref, wc_ref, wo_ref, *rest, moe):
    if moe:
        rw_ref, rb_ref, xo_ref, h_ref, gw_ref = rest
    else:
        xo_ref, h_ref = rest
    d = D_MODEL
    merged = (gate_ref[:, 0:d] * jnp.dot(oa_ref[...], wa_ref[...], preferred_element_type=F32)
              + gate_ref[:, d:2 * d] * jnp.dot(ob_ref[...], wb_ref[...], preferred_element_type=F32)
              + gate_ref[:, 2 * d:3 * d] * jnp.dot(oc_ref[...], wc_ref[...], preferred_element_type=F32))
    x = x_ref[...] + g1_ref[...] * jnp.dot(merged.astype(BF16), wo_ref[...], preferred_element_type=F32)
    xo_ref[...] = x
    ms = jnp.mean(x * x, axis=-1, keepdims=True)
    y = x * lax.rsqrt(ms + NORM_EPS) * g_ref[...]
    h = y * (1.0 + sc2_ref[...]) + sh2_ref[...]
    h_ref[...] = h.astype(BF16)
    if moe:
        h_hi = h.astype(BF16)
        h_lo = (h - h_hi.astype(F32)).astype(BF16)
        logits = (jnp.dot(h_hi, rw_ref[0], preferred_element_type=F32)
                  + jnp.dot(h_lo, rw_ref[0], preferred_element_type=F32)
                  + jnp.dot(h_hi, rw_ref[1], preferred_element_type=F32)) + rb_ref[...]
        gw_ref[...] = _route(logits)


def _merge(x, oa, ob, oc, gates, g1, sc2, sh2, g, wa, wb, wc, wo, router, tm):
    bsz, t, d = x.shape
    mrows = g1.shape[1]
    mblk = 1 if mrows == 1 else tm
    mod_spec = pl.BlockSpec((None, mblk, d), (lambda b, i: (b, 0, 0)) if mrows == 1 else (lambda b, i: (b, i, 0)))

    def tok(n):
        return pl.BlockSpec((None, tm, n), lambda b, i: (b, i, 0))

    in_specs = [tok(d), tok(384), tok(384), tok(256), tok(3 * d), mod_spec, mod_spec, mod_spec,
                _const_spec((1, d)), _const_spec(wa.shape), _const_spec(wb.shape), _const_spec(wc.shape),
                _const_spec(wo.shape)]
    args = [x, oa, ob, oc, gates, g1, sc2, sh2, g.reshape(1, d), wa, wb, wc, wo]
    out_specs = [tok(d), tok(d)]
    out_shape = [jax.ShapeDtypeStruct((bsz, t, d), F32), jax.ShapeDtypeStruct((bsz, t, d), BF16)]
    if router is not None:
        rw, rb = router
        in_specs += [_const_spec(rw.shape), _const_spec(rb.shape)]
        args += [rw, rb]
        out_specs.append(tok(LANES))
        out_shape.append(jax.ShapeDtypeStruct((bsz, t, LANES), F32))
    return pl.pallas_call(
        functools.partial(_merge_kernel, moe=router is not None),
        grid=(bsz, t // tm),
        in_specs=in_specs, out_specs=out_specs, out_shape=out_shape,
        compiler_params=_cparams(("parallel", "parallel")),
        name="merge_out",
    )(*args)


def _final_norm(x, gain):
    ms = jnp.mean(x * x, axis=-1, keepdims=True)
    return x * lax.rsqrt(ms + NORM_EPS) * gain


def _ffn_kernel(x_ref, h_ref, g2_ref, wg_ref, wu_ref, wd_ref, *rest, n_chunks, final):
    if final:
        fg_ref, o_ref = rest
    else:
        (o_ref,) = rest
    h = h_ref[...]
    tf = wg_ref.shape[1] // n_chunks
    acc = jnp.zeros(x_ref.shape, F32)
    for c in range(n_chunks):
        gt = jnp.dot(h, wg_ref[:, c * tf:(c + 1) * tf], preferred_element_type=F32)
        up = jnp.dot(h, wu_ref[:, c * tf:(c + 1) * tf], preferred_element_type=F32)
        act = (gt * jax.nn.sigmoid(gt) * up).astype(BF16)
        acc = acc + jnp.dot(act, wd_ref[c * tf:(c + 1) * tf, :], preferred_element_type=F32)
    x = x_ref[...] + g2_ref[...] * acc
    o_ref[...] = _final_norm(x, fg_ref[...]) if final else x


def _ffn(x, h, g2, wg, wu, wd, final_g, tm):
    bsz, t, d = x.shape
    mrows = g2.shape[1]
    mblk = 1 if mrows == 1 else tm
    mod_spec = pl.BlockSpec((None, mblk, d), (lambda b, i: (b, 0, 0)) if mrows == 1 else (lambda b, i: (b, i, 0)))
    tok = pl.BlockSpec((None, tm, d), lambda b, i: (b, i, 0))
    in_specs = [tok, tok, mod_spec, _const_spec(wg.shape), _const_spec(wu.shape), _const_spec(wd.shape)]
    args = [x, h, g2, wg, wu, wd]
    if final_g is not None:
        in_specs.append(_const_spec((1, d)))
        args.append(final_g.reshape(1, d))
    return pl.pallas_call(
        functools.partial(_ffn_kernel, n_chunks=2, final=final_g is not None),
        grid=(bsz, t // tm),
        in_specs=in_specs, out_specs=tok,
        out_shape=jax.ShapeDtypeStruct((bsz, t, d), F32),
        compiler_params=_cparams(("parallel", "parallel")),
        name="ffn_dense",
    )(*args)


def _moe_kernel(x_ref, h_ref, g2_ref, gw_ref, wg_ref, wu_ref, wd_ref, *rest, final):
    if final:
        fg_ref, o_ref, acc_ref, posc_ref, posr_ref = rest
    else:
        o_ref, acc_ref, posc_ref, posr_ref = rest
    e = pl.program_id(2)
    tm = h_ref.shape[0]
    n_slabs = tm // MOE_CHUNK

    @pl.when(e == 0)
    def _():
        acc_ref[...] = jnp.zeros_like(acc_ref)
        routed = gw_ref[...] != 0.0
        r_f = jnp.where(routed, 1.0, 0.0)
        r_b = r_f.astype(BF16)
        r_t = r_f.T
        r_tb = r_t.astype(BF16)
        tok_l = _lane_iota((MOE_CHUNK, tm))
        tok_r = _row_iota((MOE_CHUNK, tm))
        rank_r = jnp.zeros((LANES, tm), F32)
        for s in range(n_slabs):
            rows = slice(s * MOE_CHUNK, (s + 1) * MOE_CHUNK)
            earlier = jnp.where(tok_l < tok_r + s * MOE_CHUNK, 1.0, 0.0).astype(BF16)
            rank_c = jnp.dot(earlier, r_b, preferred_element_type=F32)
            posc_ref[rows, :] = jnp.where(routed[rows], rank_c, -1.0)
            later = jnp.where(tok_r + s * MOE_CHUNK < tok_l, 1.0, 0.0).astype(BF16)
            rank_r = rank_r + jnp.dot(r_tb[:, rows], later, preferred_element_type=F32)
        posr_ref[...] = jnp.where(r_t != 0.0, rank_r, -1.0)

    lane_e = _lane_iota((tm, LANES)) == e
    pos_c = jnp.sum(jnp.where(lane_e, posc_ref[...], 0.0), axis=1, keepdims=True)
    gate_c = jnp.sum(jnp.where(lane_e, gw_ref[...], 0.0), axis=1, keepdims=True)
    pos_r = posr_ref[pl.ds(e, 1), :]
    n_routed = (jnp.max(pos_r) + 1.0).astype(I32)

    def run_chunk(base, n_rows):
        slot_rows = _row_iota((n_rows, tm)).astype(F32)
        slot_lanes = _lane_iota((MOE_CHUNK, n_rows)).astype(F32)
        pack = jnp.where(pos_r - base == slot_rows, 1.0, 0.0).astype(BF16)
        xc = jnp.dot(pack, h_ref[...], preferred_element_type=F32).astype(BF16)
        gt = jnp.dot(xc, wg_ref[...], preferred_element_type=F32)
        up = jnp.dot(xc, wu_ref[...], preferred_element_type=F32)
        act = (gt * jax.nn.sigmoid(gt) * up).astype(BF16)
        y = jnp.dot(act, wd_ref[...], preferred_element_type=F32).astype(BF16)
        for s in range(n_slabs):
            rows = slice(s * MOE_CHUNK, (s + 1) * MOE_CHUNK)
            unpack = jnp.where(pos_c[rows] - base == slot_lanes, 1.0, 0.0).astype(BF16)
            acc_ref[rows, :] += gate_c[rows] * jnp.dot(unpack, y, preferred_element_type=F32)

    def first(c, carry):
        run_chunk(0.0, MOE_CHUNK)
        return carry

    def later(c, carry):
        run_chunk((MOE_CHUNK + c * (MOE_CHUNK // 2)).astype(F32), MOE_CHUNK // 2)
        return carry

    lax.fori_loop(0, jnp.minimum(n_routed, 1), first, 0)
    n_later = (jnp.maximum(n_routed - MOE_CHUNK, 0) + MOE_CHUNK // 2 - 1) // (MOE_CHUNK // 2)
    lax.fori_loop(0, n_later, later, 0)

    @pl.when(e == pl.num_programs(2) - 1)
    def _():
        x = x_ref[...] + g2_ref[...] * acc_ref[...]
        o_ref[...] = _final_norm(x, fg_ref[...]) if final else x


def _moe(x, h, g2, gw, wg, wu, wd, final_g, tm):
    bsz, t, d = x.shape
    n_e, _, dff = wg.shape
    mrows = g2.shape[1]
    mblk = 1 if mrows == 1 else tm
    mod_spec = pl.BlockSpec((None, mblk, d), (lambda b, i, e: (b, 0, 0)) if mrows == 1 else (lambda b, i, e: (b, i, 0)))
    tok = pl.BlockSpec((None, tm, d), lambda b, i, e: (b, i, 0))
    tok_once = pl.BlockSpec((None, tm, d), lambda b, i, e: (b, i, 0), pipeline_mode=pl.Buffered(1))
    in_specs = [tok_once, tok, mod_spec, pl.BlockSpec((None, tm, LANES), lambda b, i, e: (b, i, 0)),
                pl.BlockSpec((None, d, dff), lambda b, i, e: (e, 0, 0)),
                pl.BlockSpec((None, d, dff), lambda b, i, e: (e, 0, 0)),
                pl.BlockSpec((None, dff, d), lambda b, i, e: (e, 0, 0))]
    args = [x, h, g2, gw, wg, wu, wd]
    if final_g is not None:
        in_specs.append(pl.BlockSpec((1, d), lambda b, i, e: (0, 0)))
        args.append(final_g.reshape(1, d))
    return pl.pallas_call(
        functools.partial(_moe_kernel, final=final_g is not None),
        grid=(bsz, t // tm, n_e),
        in_specs=in_specs, out_specs=tok,
        out_shape=jax.ShapeDtypeStruct((bsz, t, d), F32),
        scratch_shapes=[pltpu.VMEM((tm, d), F32), pltpu.VMEM((tm, LANES), F32), pltpu.VMEM((LANES, tm), F32)],
        compiler_params=_cparams(("parallel", "parallel", "arbitrary")),
        name="moe_routed",
    )(*args)


def _rope_tables(pos):
    inv = ROPE_THETA ** (-jnp.arange(HALF, dtype=F32) / HALF)
    ang = pos.astype(F32)[:, None] * inv[None, :]
    cos, sin = jnp.cos(ang), jnp.sin(ang)
    return jnp.tile(cos, (1, 4)), jnp.tile(jnp.concatenate([-sin, sin], axis=1), (1, 2))


def _pick_tile(n, pref):
    t = min(n, pref)
    while n % t:
        t //= 2
    return t


def _per_seq_cols(a, bsz, t, width):
    f = a.shape[1]
    a = jnp.moveaxis(a[0].reshape(f, bsz, t), 1, 0)
    return jnp.pad(a, ((0, 0), (0, 0), (0, width - t)))


def _value_blocks(past_vt, new_vt, bsz, t, lpad):
    lead = new_vt.shape[:-2]
    new_b = jnp.moveaxis(new_vt.reshape(*lead, V_ROWS, bsz, t), -2, 0)
    full = jnp.concatenate([past_vt, new_b], axis=-1)
    full = jnp.pad(full, [(0, 0)] * (full.ndim - 1) + [(0, lpad - full.shape[-1])])
    full = full.reshape(bsz, *lead, V_ROWS, lpad // KV_BLOCK, KV_BLOCK)
    return jnp.moveaxis(full, -2, -3)


def _with_ones_rows(vt):
    ones = jnp.ones(vt.shape[:-2] + (1, vt.shape[-1]), vt.dtype)
    zeros = jnp.zeros(vt.shape[:-2] + (V_ROWS - HEAD_DIM - 1, vt.shape[-1]), vt.dtype)
    return jnp.concatenate([vt, ones, zeros], axis=-2)


def _mixers(inp, past, n_past, lw, bsz, t):
    aqt, iqt, kik, avt, bqt, bk, bvt, cu, sm, smt = inp
    n_keys = n_past + t
    n_sel = min(TOPK_MAX, n_keys // 4)
    lpad = -(-n_keys // (2 * KV_BLOCK)) * 2 * KV_BLOCK
    if past is None:
        kik_all, avt_all, bk_all, bvt_all, logf_all = kik, avt, bk, bvt, sm
        hist16 = jnp.zeros((bsz, 16, POOL_WIDTH), F32)
        tq_a, tq_b, t_pad = _pick_tile(t, 256), _pick_tile(t, 256), t
    else:
        pa, pb, plf, pc = past
        pk, pv, pik = (pa[:, :, j].astype(BF16) for j in range(3))

        def join_rows(p, new):
            full = jnp.concatenate([p, new.reshape(bsz, t, new.shape[-1])], axis=1)
            return jnp.pad(full, ((0, 0), (0, lpad - n_keys), (0, 0)))

        kik_all = join_rows(jnp.concatenate([pk, pik], axis=-1), kik)
        avt_all = _value_blocks(_with_ones_rows(jnp.swapaxes(pv, 1, 2)), avt[0, 0], bsz, t, lpad)
        bk_all = join_rows(pb[:, :, 0].astype(BF16).reshape(bsz, n_past, 384), bk)
        pvt = jnp.transpose(pb[:, :, 1].astype(BF16), (0, 2, 3, 1))
        bvt_all = _value_blocks(_with_ones_rows(pvt), bvt[0, :, 0], bsz, t, lpad)
        logf_all = join_rows(jnp.pad(plf, ((0, 0), (0, 0), (0, LANES - B_HEADS))), sm)
        hist16 = jnp.pad(pc, ((0, 0), (1, 0), (0, 0)))
        tq_a = tq_b = t_pad = LANES
        aqt, iqt, bqt, smt = (_per_seq_cols(a, bsz, t, t_pad) for a in (aqt, iqt, bqt, smt))

    oa = _dsa(aqt, iqt, smt, kik_all, avt_all, tq=tq_a, n_keys=n_keys, q_pos0=n_past, n_sel=n_sel)
    ck = _cum_logf(logf_all, 2 * KV_BLOCK)
    ob = _fox(bqt, bk_all, bvt_all, ck, tq=tq_b, n_keys=n_keys, q_pos0=n_past)
    cu = cu.reshape(bsz, t, POOL_WIDTH)
    oc = _pool(cu, hist16, lw["pool_bd"], lw["pool_scale"], tc=_pick_tile(t, 1024), start_pos=n_past)
    return oa[:, :t], ob[:, :t], oc


def _layer(x, mod, past, n_past, pos_tab, lw, layer, final_g, per_token):
    bsz, t, d = x.shape
    sh1, sc1, g1, sh2, sc2, g2 = mod
    if per_token:
        xt = x.reshape(1, bsz * t, d)
        sh1, sc1, g1, sh2, sc2, g2 = (jnp.broadcast_to(m, (bsz, t, d)).reshape(1, bsz * t, d) for m in mod)
        cos, sin = (jnp.tile(a, (bsz, 1)) for a in pos_tab)
    else:
        xt = x
        cos, sin = pos_tab
    tm = KV_BLOCK
    (aqt, iqt, nat, kik, avt, nbt, bqt, bk, bvt, cu, sm, smt, gates) = _in_proj(
        xt, sc1, sh1, lw["norm_mix_g"], lw["w_in"], lw["bf_bias"], cos, sin, tm)
    oa, ob, oc = _mixers((aqt, iqt, kik, avt, bqt, bk, bvt, cu, sm, smt), past, n_past, lw, bsz, t)

    def flat(a):
        return a.reshape(xt.shape[0], xt.shape[1], a.shape[-1])

    router = (lw["router_w"], lw["router_b"]) if layer % 2 else None
    res = _merge(xt, flat(oa), flat(ob), flat(oc), gates, g1, sc2, sh2, lw["norm_ffn_g"],
                 lw["w_br_a"], lw["w_br_b"], lw["w_br_c"], lw["w_out"], router, _pick_tile(xt.shape[1], 512))
    tmf = _pick_tile(xt.shape[1], 512)
    if layer % 2 == 0:
        x_mid, h2 = res
        x_new = _ffn(x_mid, h2, g2, lw["ffn_wg"], lw["ffn_wu"], lw["ffn_wd"], final_g, tmf)
    else:
        x_mid, h2, gw = res
        x_new = _moe(x_mid, h2, g2, gw, lw["moe_wg"], lw["moe_wu"], lw["moe_wd"], final_g,
                     _pick_tile(xt.shape[1], 1024))
    def token_major(a, *feat):
        a = a.reshape(a.shape[0], *feat, -1, t) if per_token else a.reshape(a.shape[0], *feat, 1, t)
        a = jnp.moveaxis(a, (-2, -1), (1, 2))
        return a.reshape(bsz, t, *feat)

    new_a = token_major(nat, 3, HEAD_DIM)
    new_b = token_major(nbt, 2, B_HEADS, HEAD_DIM)
    new_logf = sm.reshape(bsz, t, LANES)[:, :, :B_HEADS]
    new_pool = cu.reshape(bsz, t, POOL_WIDTH)[:, t - POOL_HIST:, :]
    return x_new.reshape(bsz, t, d), (new_a, new_b, new_logf, new_pool)


def kernel(x_prompt, x_sample, cache_a_kvi, cache_b_kv, cache_b_logf, state_c_pool, c_prompt, c_sample,
           ada_w, ada_b, norm_mix_g, w_in, b_forget, pool_w, pool_scale, w_br_a, w_br_b, w_br_c, w_out,
           norm_ffn_g, ffn_w_gate, ffn_w_up, ffn_w_down, moe_router_w, moe_router_b, moe_w_gate,
           moe_w_up, moe_w_down, final_norm_g):
    depth = ada_w.shape[0]
    bp, tp, d = x_prompt.shape
    bs, ts, _ = x_sample.shape
    n_past = cache_a_kvi.shape[2]
    assert tp % KV_BLOCK == 0 and (bs * ts) % KV_BLOCK == 0 and ts <= LANES

    rows = -(-(bp + bs) // 8) * 8
    c_all = jnp.pad(jnp.concatenate([c_prompt, c_sample], axis=0), ((0, rows - bp - bs), (0, 0)))
    mod_all = _ada(c_all, ada_w, ada_b)

    tab_p = _rope_tables(jnp.arange(tp))
    tab_s = _rope_tables(n_past + jnp.arange(ts))

    xp, xs = x_prompt, x_sample
    outs_p, outs_s = [], []
    for layer in range(depth):
        j = layer // 2
        w_l, bias_l = _in_weights(w_in[layer], b_forget[layer])
        pw = pool_w[layer]
        pool_bd = jnp.zeros((POOL_WIDTH, POOL_WIDTH), F32)
        for g in range(len(POOL_WINDOWS)):
            sl = slice(g * POOL_GROUP_DIM, (g + 1) * POOL_GROUP_DIM)
            pool_bd = pool_bd.at[sl, sl].set(pw[g])
        lw = dict(w_in=w_l, bf_bias=bias_l, norm_mix_g=norm_mix_g[layer], norm_ffn_g=norm_ffn_g[layer],
                  pool_bd=pool_bd.astype(BF16), pool_scale=pool_scale[layer],
                  w_br_a=w_br_a[layer].astype(BF16), w_br_b=w_br_b[layer].astype(BF16),
                  w_br_c=w_br_c[layer].astype(BF16), w_out=w_out[layer].astype(BF16))
        if layer % 2 == 0:
            lw.update(ffn_wg=ffn_w_gate[j].astype(BF16), ffn_wu=ffn_w_up[j].astype(BF16),
                      ffn_wd=ffn_w_down[j].astype(BF16))
        else:
            rw = jnp.pad(moe_router_w[j], ((0, 0), (0, LANES - N_EXPERTS)))
            rw_hi = rw.astype(BF16)
            lw.update(router_w=jnp.stack([rw_hi, (rw - rw_hi.astype(F32)).astype(BF16)]),
                      router_b=jnp.pad(moe_router_b[j], (0, LANES - N_EXPERTS)).reshape(1, LANES),
                      moe_wg=moe_w_gate[j].astype(BF16), moe_wu=moe_w_up[j].astype(BF16),
                      moe_wd=moe_w_down[j].astype(BF16))
        final_g = final_norm_g if layer == depth - 1 else None
        mod_p = [m[:, None, :] for m in jnp.split(mod_all[layer, :bp], 6, axis=-1)]
        mod_s = [m[:, None, :] for m in jnp.split(mod_all[layer, bp:bp + bs], 6, axis=-1)]
        xp, new_p = _layer(xp, mod_p, None, 0, tab_p, lw, layer, final_g, per_token=False)
        past = (cache_a_kvi[layer], cache_b_kv[layer], cache_b_logf[layer], state_c_pool[layer])
        xs, new_s = _layer(xs, mod_s, past, n_past, tab_s, lw, layer, final_g, per_token=True)
        outs_p.append(new_p)
        outs_s.append(new_s)

    def stack(outs, k):
        return jnp.stack([o[k] for o in outs])

    return (xp, xs,
            stack(outs_p, 0), stack(outs_p, 1), stack(outs_p, 2), stack(outs_p, 3),
            stack(outs_s, 0), stack(outs_s, 1), stack(outs_s, 2), stack(outs_s, 3))
```

```python
import functools

import jax
import jax.numpy as jnp
import numpy as np
from jax import lax
from jax.experimental import pallas as pl
from jax.experimental.pallas import tpu as pltpu

F32 = jnp.float32
BF16 = jnp.bfloat16
I32 = jnp.int32

D_MODEL = 1024
CHUNK = 64
HEAD_DIM = 64
HALF = HEAD_DIM // 2
ROPE_THETA = 10000.0
NORM_EPS = 1e-6
A_HEADS = 6
IDX_HEADS = 4
TOPK_MAX = 256
B_HEADS = 6
POOL_WINDOWS = (2, 4, 8, 16)
POOL_GROUP_DIM = 64
POOL_WIDTH = 256
POOL_HIST = 15
N_EXPERTS = 8
LANES = 128
SUBLANES = 8
LOG2E = 1.4426950408889634
QK_SCALE = HEAD_DIM ** -0.5 * LOG2E
KV_BLOCK = 256
V_ROWS = HEAD_DIM + 16
MOE_CHUNK = 256
VMEM_LIMIT = 56 * 1024 * 1024
NEG_INF = float("-inf")
KEY_DT = jnp.bfloat16

C_AQ, C_IQ, C_A, C_B, C_CU, C_SM, C_GATE, C_END = 0, 384, 640, 896, 2048, 2304, 2432, 5504


def _cparams(sem):
    return pltpu.CompilerParams(dimension_semantics=sem, vmem_limit_bytes=VMEM_LIMIT)


def _const_spec(shape):
    nd = len(shape)
    return pl.BlockSpec(shape, lambda *_: (0,) * nd, pipeline_mode=pl.Buffered(1))


def _lane_iota(shape):
    return lax.broadcasted_iota(I32, shape, len(shape) - 1)


def _row_iota(shape):
    return lax.broadcasted_iota(I32, shape, len(shape) - 2)


def _ada_kernel(c_ref, w_ref, b_ref, o_ref):
    c = c_ref[...]
    s = c * jax.nn.sigmoid(c)
    o_ref[...] = jnp.dot(s, w_ref[...], preferred_element_type=F32,
                         precision=lax.Precision.HIGHEST) + b_ref[...]


def _ada(c_all, ada_w, ada_b):
    depth, d, n = ada_w.shape
    rows = c_all.shape[0]
    tn = 1536
    return pl.pallas_call(
        _ada_kernel,
        grid=(depth, n // tn),
        in_specs=[pl.BlockSpec((rows, d), lambda l, j: (0, 0)),
                  pl.BlockSpec((None, d, tn), lambda l, j: (l, 0, j)),
                  pl.BlockSpec((None, 1, tn), lambda l, j: (l, 0, j))],
        out_specs=pl.BlockSpec((None, rows, tn), lambda l, j: (l, 0, j)),
        out_shape=jax.ShapeDtypeStruct((depth, rows, n), F32),
        compiler_params=_cparams(("arbitrary", "arbitrary")),
        name="ada_mod",
    )(c_all, ada_w, ada_b.reshape(depth, 1, n))


def _in_kernel(x_ref, sc_ref, sh_ref, g_ref, w_ref, bf_ref, cos_ref, sin_ref,
               aqt_ref, iqt_ref, nat_ref, kik_ref, avt_ref, nbt_ref, bqt_ref, bk_ref, bvt_ref,
               cu_ref, sm_ref, smt_ref, gate_ref):
    x = x_ref[...]
    ms = jnp.mean(x * x, axis=-1, keepdims=True)
    y = x * lax.rsqrt(ms + NORM_EPS) * g_ref[...]
    h = (y * (1.0 + sc_ref[...]) + sh_ref[...]).astype(BF16)
    tm = x.shape[0]

    def mm(a, b):
        return jnp.dot(h, w_ref[:, a:b], preferred_element_type=F32)

    cos = cos_ref[...]
    sin = sin_ref[...]
    lane = _lane_iota((tm, LANES))
    low = lane < HEAD_DIM
    first_half = (lane & HALF) == 0

    def rope(z):
        swapped = jnp.where(first_half, pltpu.roll(z, LANES - HALF, 1), pltpu.roll(z, HALF, 1))
        return z * cos + swapped * sin

    zeros64 = jnp.zeros((HEAD_DIM, tm), BF16)
    ones_rows = jnp.where(_row_iota((V_ROWS - HEAD_DIM, tm)) == 0, 1.0, 0.0).astype(BF16)

    def put_heads(ref, zt, p, slot_even, slot_odd):
        for hh, slot in ((0, slot_even), (1, slot_odd)):
            base = (2 * p + hh) * LANES
            ref[base + slot * HEAD_DIM:base + (slot + 1) * HEAD_DIM, :] = zt[hh * HEAD_DIM:(hh + 1) * HEAD_DIM]
            ref[base + (1 - slot) * HEAD_DIM:base + (2 - slot) * HEAD_DIM, :] = zeros64

    z = mm(C_AQ, C_IQ)
    for p in range(3):
        zt = (rope(z[:, p * LANES:(p + 1) * LANES]) * QK_SCALE).T.astype(BF16)
        put_heads(aqt_ref, zt, p, 0, 0)
    z = mm(C_IQ, C_A)
    for p in range(2):
        zt = rope(z[:, p * LANES:(p + 1) * LANES]).T.astype(BF16)
        put_heads(iqt_ref, zt, p, 1, 1)

    z = mm(C_A, C_B)
    kv = z[:, :LANES]
    r0 = jnp.where(low, rope(kv), kv)
    r1 = rope(z[:, LANES:])
    r0t = r0.T
    nat_ref[0:LANES, :] = r0t
    nat_ref[LANES:, :] = r1.T[0:HEAD_DIM, :]
    kik_ref[...] = jnp.where(low, r0, pltpu.roll(r1, HEAD_DIM, 1)).astype(BF16)
    avt_ref[0:HEAD_DIM, :] = r0t[HEAD_DIM:, :].astype(BF16)
    avt_ref[HEAD_DIM:, :] = ones_rows

    z = mm(C_B, C_CU)
    bk_ref[...] = z[:, 384:768].astype(BF16)
    for p in range(3):
        zt = (z[:, p * LANES:(p + 1) * LANES] * QK_SCALE).T.astype(BF16)
        put_heads(bqt_ref, zt, p, 0, 1)
        nbt_ref[p * LANES:(p + 1) * LANES, :] = z[:, 384 + p * LANES:384 + (p + 1) * LANES].T
        vt = z[:, 768 + p * LANES:768 + (p + 1) * LANES].T
        nbt_ref[384 + p * LANES:384 + (p + 1) * LANES, :] = vt
        vt = vt.astype(BF16)
        for hh in range(2):
            bvt_ref[2 * p + hh, 0:HEAD_DIM, :] = vt[hh * HEAD_DIM:(hh + 1) * HEAD_DIM]
            bvt_ref[2 * p + hh, HEAD_DIM:, :] = ones_rows

    cu_ref[...] = mm(C_CU, C_SM)

    z = mm(C_SM, C_GATE)
    t = z + bf_ref[...]
    logf = jnp.minimum(t, 0.0) - jnp.log1p(jnp.exp(-jnp.abs(t)))
    sm = jnp.where(lane < B_HEADS, logf, z)
    sm_ref[...] = sm
    smt_ref[...] = sm.T[0:16, :]

    for c in range(3):
        gate_ref[:, c * D_MODEL:(c + 1) * D_MODEL] = jax.nn.sigmoid(
            mm(C_GATE + c * D_MODEL, C_GATE + (c + 1) * D_MODEL))


def _in_weights(w_in_l, b_forget_l):
    d = w_in_l.shape[0]
    sizes = (384, 64, 64, 256, 4, 64, 384, 384, 384, 6, 256, 3072)
    o = np.concatenate([[0], np.cumsum(sizes)])
    w_in_l = w_in_l.astype(BF16)
    cols = [w_in_l[:, o[0]:o[1]],
            w_in_l[:, o[3]:o[4]],
            w_in_l[:, o[1]:o[3]], w_in_l[:, o[5]:o[6]], jnp.zeros((d, 64), BF16),
            w_in_l[:, o[6]:o[9]],
            w_in_l[:, o[10]:o[11]],
            w_in_l[:, o[9]:o[10]], jnp.zeros((d, 2), BF16), w_in_l[:, o[4]:o[5]], jnp.zeros((d, LANES - 12), BF16),
            w_in_l[:, o[11]:o[12]]]
    w = jnp.concatenate(cols, axis=1)
    assert w.shape[1] == C_END, w.shape
    bias = jnp.concatenate([b_forget_l, jnp.zeros((LANES - B_HEADS,), F32)]).reshape(1, LANES)
    return w, bias


def _in_proj(x, sc, sh, g, w, bias, cos, sin, tm):
    bsz, t, d = x.shape
    mrows = sc.shape[1]
    mblk = 1 if mrows == 1 else tm
    mod_spec = pl.BlockSpec((None, mblk, d), (lambda b, i: (b, 0, 0)) if mrows == 1 else (lambda b, i: (b, i, 0)))
    nblk = t // tm

    def rows(n, dt):
        return pl.BlockSpec((None, tm, n), lambda b, i: (b, i, 0)), jax.ShapeDtypeStruct((bsz, t, n), dt)

    def cols(n, dt):
        return pl.BlockSpec((None, n, tm), lambda b, i: (b, 0, i)), jax.ShapeDtypeStruct((bsz, n, t), dt)

    outs = [cols(A_HEADS * LANES, BF16), cols(IDX_HEADS * LANES, BF16), cols(192, F32), rows(LANES, BF16),
            (pl.BlockSpec((None, None, V_ROWS, tm), lambda b, i: (b, i, 0, 0)),
             jax.ShapeDtypeStruct((bsz, nblk, V_ROWS, tm), BF16)),
            cols(768, F32), cols(B_HEADS * LANES, BF16), rows(384, BF16),
            (pl.BlockSpec((None, B_HEADS, None, V_ROWS, tm), lambda b, i: (b, 0, i, 0, 0)),
             jax.ShapeDtypeStruct((bsz, B_HEADS, nblk, V_ROWS, tm), BF16)),
            rows(256, F32), rows(LANES, F32), cols(16, F32), rows(3 * D_MODEL, F32)]
    return pl.pallas_call(
        _in_kernel,
        grid=(bsz, nblk),
        in_specs=[pl.BlockSpec((None, tm, d), lambda b, i: (b, i, 0)), mod_spec, mod_spec,
                  _const_spec((1, d)), _const_spec(w.shape), _const_spec((1, LANES)),
                  pl.BlockSpec((tm, LANES), lambda b, i: (i, 0)),
                  pl.BlockSpec((tm, LANES), lambda b, i: (i, 0))],
        out_specs=[o[0] for o in outs],
        out_shape=[o[1] for o in outs],
        compiler_params=_cparams(("parallel", "parallel")),
        name="in_proj",
    )(x, sc, sh, g.reshape(1, d), w, bias, cos, sin)


def _cum_kernel(x_ref, o_ref, carry_ref):
    @pl.when(pl.program_id(1) == 0)
    def _():
        carry_ref[...] = jnp.zeros_like(carry_ref)

    tc = x_ref.shape[0]
    tri = jnp.where(_lane_iota((tc, tc)) <= _row_iota((tc, tc)), 1.0, 0.0).astype(BF16)
    cum = carry_ref[0:1, :]
    rest = x_ref[...]
    for _ in range(3):
        piece = rest.astype(BF16)
        cum = cum + jnp.dot(tri, piece, preferred_element_type=F32)
        rest = rest - piece.astype(F32)
    carry_ref[...] = jnp.broadcast_to(cum[tc - 1:tc, :], carry_ref.shape)
    pieces = []
    rest = cum * LOG2E
    for _ in range(3):
        piece = rest.astype(BF16)
        pieces.append(piece.astype(F32))
        rest = rest - pieces[-1]
    lane = _lane_iota((tc, LANES))
    for p in range(B_HEADS // 2):
        slab = jnp.zeros((tc, LANES), F32)
        for hh in range(2):
            for j, piece in enumerate(pieces):
                dst, src = 3 * hh + j, 2 * p + hh
                slab = jnp.where(lane == dst, pltpu.roll(piece, (dst - src) % LANES, 1), slab)
        o_ref[p] = slab.astype(BF16)


def _cum_logf(x, tc):
    bsz, t, n = x.shape
    return pl.pallas_call(
        _cum_kernel,
        grid=(bsz, t // tc),
        in_specs=[pl.BlockSpec((None, tc, n), lambda b, i: (b, i, 0))],
        out_specs=pl.BlockSpec((None, B_HEADS // 2, tc, LANES), lambda b, i: (b, 0, i, 0)),
        out_shape=jax.ShapeDtypeStruct((bsz, B_HEADS // 2, t, LANES), BF16),
        scratch_shapes=[pltpu.VMEM((SUBLANES, LANES), F32)],
        compiler_params=_cparams(("parallel", "arbitrary")),
        name="logf_cumsum",
    )(x)


def _sum_keys(x):
    part = x.reshape(x.shape[0] // SUBLANES, SUBLANES, x.shape[1]).sum(axis=0)
    return jnp.sum(part, axis=0, keepdims=True)


def _max_keys(x):
    part = x.reshape(x.shape[0] // SUBLANES, SUBLANES, x.shape[1]).max(axis=0)
    return jnp.max(part, axis=0, keepdims=True)


def _dsa_kernel(aqt_ref, iqt_ref, smt_ref, kik_ref, avt_ref, o_ref, key_ref,
                *, tq, tk, n_keys, q_pos0, n_sel):
    i = pl.program_id(1)
    pos_first = q_pos0 + i * tq
    last_chunk = (pos_first + tq - 1) // CHUNK
    n_adm = jnp.minimum((last_chunk + 1) * CHUNK, n_keys)
    nkb = (n_adm + tk - 1) // tk
    n_pairs = (nkb + 1) // 2

    q_pos = pos_first + _lane_iota((1, tq))
    q_lim = jnp.minimum((q_pos // CHUNK + 1) * CHUNK, n_keys)
    key_row = _row_iota((tk, tq))

    def keys(kb):
        return kik_ref[pl.ds(pl.multiple_of(kb * tk, tk), tk), :]

    iq4 = jnp.concatenate([iqt_ref[hd * LANES:(hd + 1) * LANES, :] for hd in range(IDX_HEADS)], axis=1)
    smt = smt_ref[...]
    w_rows = [smt[8 + hd:9 + hd, :] for hd in range(IDX_HEADS)]

    def score_body(kb, carry):
        s4 = jnp.dot(keys(kb), iq4, preferred_element_type=F32)
        score = w_rows[0] * jnp.maximum(s4[:, 0:tq], 0.0)
        for hd in range(1, IDX_HEADS):
            score = score + w_rows[hd] * jnp.maximum(s4[:, hd * tq:(hd + 1) * tq], 0.0)
        score = jnp.where(key_row < q_lim - kb * tk, score, NEG_INF)
        key_ref[kb] = score.astype(KEY_DT)
        return carry

    lax.fori_loop(0, nkb, score_body, 0)

    @pl.when(nkb % 2 == 1)
    def _():
        key_ref[nkb] = jnp.full((tk, tq), NEG_INF, KEY_DT)

    one, zero = jnp.ones((), KEY_DT), jnp.zeros((), KEY_DT)
    packed_rows = 2 * SUBLANES

    def count(cand, strict):
        def hits(blk):
            h = jnp.where((blk > cand) if strict else (blk >= cand), one, zero)
            parts = [h[r * packed_rows:(r + 1) * packed_rows] for r in range(tk // packed_rows)]
            while len(parts) > 1:
                parts = [a + b for a, b in zip(parts[::2], parts[1::2])]
            return parts[0].astype(F32)

        def body(j, acc):
            return acc + hits(key_ref[2 * j]) + hits(key_ref[2 * j + 1])

        acc = lax.fori_loop(0, n_pairs, body, jnp.zeros((packed_rows, tq), F32))
        return jnp.sum(acc, axis=0, keepdims=True)

    def pattern_value(u):
        bits = jnp.where(u >= 0x8000, u & 0x7FFF, (~u) & 0xFFFF)
        return lax.bitcast_convert_type(lax.shift_left(bits, 16), F32).astype(KEY_DT)

    def bit_body(b, u):
        cand_u = u | lax.shift_left(jnp.int32(1), 15 - b)
        cnt = count(pattern_value(cand_u), False)
        return jnp.where(cnt >= n_sel, cand_u, u)

    u_thr = jnp.maximum(lax.fori_loop(0, 16, bit_body, jnp.zeros((1, tq), I32)), 0x007F)
    thr_16 = pattern_value(u_thr)

    def finer():
        lo = thr_16.astype(F32)
        hi = jnp.where(u_thr >= 0xFF80, jnp.inf, pattern_value(u_thr + 1).astype(F32))

        def halve(_, lo_hi):
            lo, hi = lo_hi
            cand = (0.5 * lo + 0.5 * hi).astype(KEY_DT)
            enough = count(cand, False) >= n_sel
            return jnp.where(enough, cand.astype(F32), lo), jnp.where(enough, hi, cand.astype(F32))

        thr_f = lax.fori_loop(0, 16, halve, (lo, hi))[0].astype(KEY_DT)
        return thr_f, count(thr_f, True)

    above_16 = count(thr_16, True)
    thr_k, above = lax.cond(jnp.min(n_sel - above_16) <= 0.0, finer, lambda: (thr_16, above_16))
    need = jnp.maximum(n_sel - above, 0.0)
    thr = thr_k.astype(F32)

    aq6 = jnp.concatenate([aqt_ref[hd * LANES:(hd + 1) * LANES, :] for hd in range(A_HEADS)], axis=1)

    half = tk // 2
    lower = jnp.where(_lane_iota((half, half)) <= _row_iota((half, half)), 1.0, 0.0).astype(BF16)

    def attend():
        def body(kb, carry):
            eq_seen, ms, accs = carry
            blk = key_ref[kb].astype(F32)
            eq = blk == thr
            eq_f = jnp.where(eq, 1.0, 0.0)
            prefs = []
            for e in (eq_f[:half], eq_f[half:]):
                prefs.append(jnp.dot(lower, e.astype(BF16), preferred_element_type=F32) + eq_seen)
                eq_seen = eq_seen + _sum_keys(e)
            slack = jnp.where(blk >= thr, need - jnp.where(eq, jnp.concatenate(prefs, axis=0), 0.0), -1.0)
            bias = jnp.where(slack >= 0.0, jnp.where(jnp.abs(blk) < jnp.inf, 0.0, NEG_INF), NEG_INF)
            logits = jnp.dot(keys(kb), aq6, preferred_element_type=F32)
            vts = (avt_ref[2 * kb], avt_ref[2 * kb + 1])
            new_ms, new_accs = [], []
            for p in range(A_HEADS // 2):
                ps, alphas = [], []
                for hd in (2 * p, 2 * p + 1):
                    lg = logits[:, hd * tq:(hd + 1) * tq] + bias
                    m_old = ms[hd]
                    m_new = jnp.maximum(m_old, _max_keys(lg))
                    m_safe = jnp.where(m_new == NEG_INF, 0.0, m_new)
                    ps.append(jnp.exp2(lg - m_safe).astype(BF16))
                    alphas.append(jnp.exp2(m_old - m_safe))
                    new_ms.append(m_new)
                p2 = jnp.concatenate(ps, axis=1)
                pv = (jnp.dot(vts[0], p2[:tk // 2], preferred_element_type=F32)
                      + jnp.dot(vts[1], p2[tk // 2:], preferred_element_type=F32))
                new_accs.append(jnp.concatenate(alphas, axis=1) * accs[p] + pv)
            return eq_seen, tuple(new_ms), tuple(new_accs)

        init = (jnp.zeros((1, tq), F32),
                tuple(jnp.full((1, tq), NEG_INF, F32) for _ in range(A_HEADS)),
                tuple(jnp.zeros((V_ROWS, 2 * tq), F32) for _ in range(A_HEADS // 2)))
        return lax.fori_loop(0, nkb, body, init)[2]

    accs = attend()

    outs = []
    for p in range(A_HEADS // 2):
        o2 = accs[p][0:HEAD_DIM] / accs[p][HEAD_DIM:HEAD_DIM + 1]
        outs += [o2[:, 0:tq], o2[:, tq:2 * tq]]
    o_ref[...] = jnp.concatenate(outs, axis=0).T.astype(o_ref.dtype)


def _dsa(aqt, iqt, smt, kik, avt, *, tq, n_keys, q_pos0, n_sel):
    bsz, _, t_q = aqt.shape
    _, nblk, _, tkv = avt.shape
    assert nblk % 2 == 0
    tk = 2 * tkv
    kern = functools.partial(_dsa_kernel, tq=tq, tk=tk, n_keys=n_keys, q_pos0=q_pos0, n_sel=n_sel)
    return pl.pallas_call(
        kern,
        grid=(bsz, t_q // tq),
        in_specs=[pl.BlockSpec((None, A_HEADS * LANES, tq), lambda b, i: (b, 0, i)),
                  pl.BlockSpec((None, IDX_HEADS * LANES, tq), lambda b, i: (b, 0, i)),
                  pl.BlockSpec((None, 16, tq), lambda b, i: (b, 0, i)),
                  pl.BlockSpec((None, nblk * tkv, LANES), lambda b, i: (b, 0, 0)),
                  pl.BlockSpec((None, nblk, V_ROWS, tkv), lambda b, i: (b, 0, 0, 0))],
        out_specs=pl.BlockSpec((None, tq, 384), lambda b, i: (b, i, 0)),
        out_shape=jax.ShapeDtypeStruct((bsz, t_q, 384), BF16),
        scratch_shapes=[pltpu.VMEM((2 * ((nblk // 2 + 1) // 2), tk, tq), KEY_DT)],
        compiler_params=_cparams(("parallel", "arbitrary")),
        name="dsa_attention",
    )(aqt, iqt, smt, kik, avt)


def _fox_kernel(qt_ref, k_ref, vt_ref, ck_ref, o_ref, *, tq, tk, n_keys, q_pos0):
    i = pl.program_id(1)
    pos_first = q_pos0 + i * tq
    n_full = pos_first // tk
    nkb = (jnp.minimum(pos_first + tq, n_keys) + tk - 1) // tk
    q_pos = pos_first + _lane_iota((1, tq))
    key_row = _row_iota((tk, tq))
    piece_row = _row_iota((LANES, tq))
    qts = []
    for hd in range(B_HEADS):
        minus = jnp.where((piece_row >= 3 * (hd % 2)) & (piece_row < 3 * (hd % 2) + 3), -1.0, 0.0).astype(BF16)
        qts.append(jnp.concatenate([qt_ref[hd * LANES:(hd + 1) * LANES, :], minus], axis=0))

    def step(kbs, state, masked):
        logits = []
        for hd in range(B_HEADS):
            for kb in kbs:
                rows = pl.ds(pl.multiple_of(kb * tk, tk), tk)
                kblk = jnp.concatenate([k_ref[rows, (hd // 2) * LANES:(hd // 2 + 1) * LANES],
                                        ck_ref[hd // 2, rows, :]], axis=1)
                lg = jnp.dot(kblk, qts[hd], preferred_element_type=F32)
                if masked:
                    lg = jnp.where(key_row <= q_pos - kb * tk, lg, NEG_INF)
                logits.append(lg)
        new = []
        for hd in range(B_HEADS):
            m_old, acc = state[hd]
            lgs = logits[hd * len(kbs):(hd + 1) * len(kbs)]
            m_new = m_old
            for lg in lgs:
                m_new = jnp.maximum(m_new, _max_keys(lg))
            m_safe = jnp.where(m_new == NEG_INF, 0.0, m_new) if masked else m_new
            acc = jnp.exp2(m_old - m_safe) * acc
            for kb, lg in zip(kbs, lgs):
                p = jnp.exp2(lg - m_safe).astype(BF16)
                acc = acc + jnp.dot(vt_ref[hd, kb], p, preferred_element_type=F32)
            new.append((m_new, acc))
        return tuple(new)

    init = tuple((jnp.full((1, tq), NEG_INF, F32), jnp.zeros((V_ROWS, tq), F32)) for _ in range(B_HEADS))
    state = lax.fori_loop(0, n_full // 2, lambda j, st: step((2 * j, 2 * j + 1), st, False), init)
    state = lax.fori_loop(2 * (n_full // 2), nkb, lambda kb, st: step((kb,), st, True), state)
    outs = [acc[0:HEAD_DIM] / acc[HEAD_DIM:HEAD_DIM + 1] for _, acc in state]
    o_ref[...] = jnp.concatenate(outs, axis=0).T.astype(o_ref.dtype)


def _fox(qt, k, vt, ck, *, tq, n_keys, q_pos0):
    bsz, _, t_q = qt.shape
    _, _, nblk, _, tk = vt.shape
    lpad = nblk * tk
    kern = functools.partial(_fox_kernel, tq=tq, tk=tk, n_keys=n_keys, q_pos0=q_pos0)
    return pl.pallas_call(
        kern,
        grid=(bsz, t_q // tq),
        in_specs=[pl.BlockSpec((None, B_HEADS * LANES, tq), lambda b, i: (b, 0, i)),
                  pl.BlockSpec((None, lpad, 384), lambda b, i: (b, 0, 0)),
                  pl.BlockSpec((None, B_HEADS, nblk, V_ROWS, tk), lambda b, i: (b, 0, 0, 0, 0)),
                  pl.BlockSpec((None, B_HEADS // 2, lpad, LANES), lambda b, i: (b, 0, 0, 0))],
        out_specs=pl.BlockSpec((None, tq, 384), lambda b, i: (b, i, 0)),
        out_shape=jax.ShapeDtypeStruct((bsz, t_q, 384), BF16),
        compiler_params=_cparams(("parallel", "arbitrary")),
        name="fox_attention",
    )(qt, k, vt, ck)


def _pool_kernel(cur_ref, prev_ref, hist_ref, w_ref, s_ref, o_ref, ext, *, tc, start_pos):
    i = pl.program_id(1)
    cur = cur_ref[...]
    ext[0:16, :] = jnp.where(i == 0, hist_ref[...], prev_ref[tc - 16:, :])
    ext[16:, :] = cur
    pos = start_pos + i * tc + lax.broadcasted_iota(I32, (tc, POOL_WIDTH), 0)
    lane = _lane_iota((tc, POOL_WIDTH))
    run = cur
    pooled = jnp.zeros_like(cur)
    k = 1
    for g, w in enumerate(POOL_WINDOWS):
        while k < w:
            run = run + ext[16 - k:16 - k + tc, :]
            k += 1
        cnt = jnp.minimum(pos + 1, w).astype(F32)
        in_group = (lane >= g * POOL_GROUP_DIM) & (lane < (g + 1) * POOL_GROUP_DIM)
        pooled = jnp.where(in_group, run / cnt, pooled)
    z = (pooled - cur).astype(BF16)
    o_ref[...] = (jnp.dot(z, w_ref[...], preferred_element_type=F32) * s_ref[...]).astype(o_ref.dtype)


def _pool(cu, hist16, w_bd, scale, *, tc, start_pos):
    bsz, t, n = cu.shape
    kern = functools.partial(_pool_kernel, tc=tc, start_pos=start_pos)
    return pl.pallas_call(
        kern,
        grid=(bsz, t // tc),
        in_specs=[pl.BlockSpec((None, tc, n), lambda b, i: (b, i, 0)),
                  pl.BlockSpec((None, tc, n), lambda b, i: (b, jnp.maximum(i - 1, 0), 0)),
                  pl.BlockSpec((None, 16, n), lambda b, i: (b, 0, 0)),
                  _const_spec((n, n)), _const_spec((1, n))],
        out_specs=pl.BlockSpec((None, tc, n), lambda b, i: (b, i, 0)),
        out_shape=jax.ShapeDtypeStruct((bsz, t, n), BF16),
        scratch_shapes=[pltpu.VMEM((16 + tc, n), F32)],
        compiler_params=_cparams(("parallel", "arbitrary")),
        name="pool_mixer",
    )(cu, cu, hist16, w_bd, scale.reshape(1, n))


def _route(logits):
    lane = _lane_iota(logits.shape).astype(F32)
    lg = jnp.where(lane < N_EXPERTS, logits, NEG_INF)
    m1 = jnp.max(lg, axis=1, keepdims=True)
    i1 = jnp.min(jnp.where(lg == m1, lane, float(LANES)), axis=1, keepdims=True)
    hot1 = lane == i1
    lg2 = jnp.where(hot1, NEG_INF, lg)
    m2 = jnp.max(lg2, axis=1, keepdims=True)
    i2 = jnp.min(jnp.where(lg2 == m2, lane, float(LANES)), axis=1, keepdims=True)
    hot2 = lane == i2
    e2 = jnp.exp(m2 - m1)
    den = 1.0 + e2
    return jnp.where(hot1, 1.0 / den, 0.0) + jnp.where(hot2, e2 / den, 0.0)


def _merge_kernel(x_ref, oa_ref, ob_ref, oc_ref, gate_ref, g1_ref, sc2_ref, sh2_ref, g_ref,
                  wa_ref, wb_ref, wc_ref, wo_ref, *rest, moe):
    if moe:
        rw_ref, rb_ref, xo_ref, h_ref, gw_ref = rest
    else:
        xo_ref, h_ref = rest
    d = D_MODEL
    merged = (gate_ref[:, 0:d] * jnp.dot(oa_ref[...], wa_ref[...], preferred_element_type=F32)
              + gate_ref[:, d:2 * d] * jnp.dot(ob_ref[...], wb_ref[...], preferred_element_type=F32)
              + gate_ref[:, 2 * d:3 * d] * jnp.dot(oc_ref[...], wc_ref[...], preferred_element_type=F32))
    x = x_ref[...] + g1_ref[...] * jnp.dot(merged.astype(BF16), wo_ref[...], preferred_element_type=F32)
    xo_ref[...] = x
    ms = jnp.mean(x * x, axis=-1, keepdims=True)
    y = x * lax.rsqrt(ms + NORM_EPS) * g_ref[...]
    h = y * (1.0 + sc2_ref[...]) + sh2_ref[...]
    h_ref[...] = h.astype(BF16)
    if moe:
        h_hi = h.astype(BF16)
        h_lo = (h - h_hi.astype(F32)).astype(BF16)
        logits = (jnp.dot(h_hi, rw_ref[0], preferred_element_type=F32)
                  + jnp.dot(h_lo, rw_ref[0], preferred_element_type=F32)
                  + jnp.dot(h_hi, rw_ref[1], preferred_element_type=F32)) + rb_ref[...]
        gw_ref[...] = _route(logits)


def _merge(x, oa, ob, oc, gates, g1, sc2, sh2, g, wa, wb, wc, wo, router, tm):
    bsz, t, d = x.shape
    mrows = g1.shape[1]
    mblk = 1 if mrows == 1 else tm
    mod_spec = pl.BlockSpec((None, mblk, d), (lambda b, i: (b, 0, 0)) if mrows == 1 else (lambda b, i: (b, i, 0)))

    def tok(n):
        return pl.BlockSpec((None, tm, n), lambda b, i: (b, i, 0))

    in_specs = [tok(d), tok(384), tok(384), tok(256), tok(3 * d), mod_spec, mod_spec, mod_spec,
                _const_spec((1, d)), _const_spec(wa.shape), _const_spec(wb.shape), _const_spec(wc.shape),
                _const_spec(wo.shape)]
    args = [x, oa, ob, oc, gates, g1, sc2, sh2, g.reshape(1, d), wa, wb, wc, wo]
    out_specs = [tok(d), tok(d)]
    out_shape = [jax.ShapeDtypeStruct((bsz, t, d), F32), jax.ShapeDtypeStruct((bsz, t, d), BF16)]
    if router is not None:
        rw, rb = router
        in_specs += [_const_spec(rw.shape), _const_spec(rb.shape)]
        args += [rw, rb]
        out_specs.append(tok(LANES))
        out_shape.append(jax.ShapeDtypeStruct((bsz, t, LANES), F32))
    return pl.pallas_call(
        functools.partial(_merge_kernel, moe=router is not None),
        grid=(bsz, t // tm),
        in_specs=in_specs, out_specs=out_specs, out_shape=out_shape,
        compiler_params=_cparams(("parallel", "parallel")),
        name="merge_out",
    )(*args)


def _final_norm(x, gain):
    ms = jnp.mean(x * x, axis=-1, keepdims=True)
    return x * lax.rsqrt(ms + NORM_EPS) * gain


def _ffn_kernel(x_ref, h_ref, g2_ref, wg_ref, wu_ref, wd_ref, *rest, n_chunks, final):
    if final:
        fg_ref, o_ref = rest
    else:
        (o_ref,) = rest
    h = h_ref[...]
    tf = wg_ref.shape[1] // n_chunks
    acc = jnp.zeros(x_ref.shape, F32)
    for c in range(n_chunks):
        gt = jnp.dot(h, wg_ref[:, c * tf:(c + 1) * tf], preferred_element_type=F32)
        up = jnp.dot(h, wu_ref[:, c * tf:(c + 1) * tf], preferred_element_type=F32)
        act = (gt * jax.nn.sigmoid(gt) * up).astype(BF16)
        acc = acc + jnp.dot(act, wd_ref[c * tf:(c + 1) * tf, :], preferred_element_type=F32)
    x = x_ref[...] + g2_ref[...] * acc
    o_ref[...] = _final_norm(x, fg_ref[...]) if final else x


def _ffn(x, h, g2, wg, wu, wd, final_g, tm):
    bsz, t, d = x.shape
    mrows = g2.shape[1]
    mblk = 1 if mrows == 1 else tm
    mod_spec = pl.BlockSpec((None, mblk, d), (lambda b, i: (b, 0, 0)) if mrows == 1 else (lambda b, i: (b, i, 0)))
    tok = pl.BlockSpec((None, tm, d), lambda b, i: (b, i, 0))
    in_specs = [tok, tok, mod_spec, _const_spec(wg.shape), _const_spec(wu.shape), _const_spec(wd.shape)]
    args = [x, h, g2, wg, wu, wd]
    if final_g is not None:
        in_specs.append(_const_spec((1, d)))
        args.append(final_g.reshape(1, d))
    return pl.pallas_call(
        functools.partial(_ffn_kernel, n_chunks=2, final=final_g is not None),
        grid=(bsz, t // tm),
        in_specs=in_specs, out_specs=tok,
        out_shape=jax.ShapeDtypeStruct((bsz, t, d), F32),
        compiler_params=_cparams(("parallel", "parallel")),
        name="ffn_dense",
    )(*args)


def _moe_kernel(x_ref, h_ref, g2_ref, gw_ref, wg_ref, wu_ref, wd_ref, *rest, final):
    if final:
        fg_ref, o_ref, acc_ref, posc_ref, posr_ref = rest
    else:
        o_ref, acc_ref, posc_ref, posr_ref = rest
    e = pl.program_id(2)
    tm = h_ref.shape[0]
    n_slabs = tm // MOE_CHUNK

    @pl.when(e == 0)
    def _():
        acc_ref[...] = jnp.zeros_like(acc_ref)
        routed = gw_ref[...] != 0.0
        r_f = jnp.where(routed, 1.0, 0.0)
        r_b = r_f.astype(BF16)
        r_t = r_f.T
        r_tb = r_t.astype(BF16)
        tok_l = _lane_iota((MOE_CHUNK, tm))
        tok_r = _row_iota((MOE_CHUNK, tm))
        rank_r = jnp.zeros((LANES, tm), F32)
        for s in range(n_slabs):
            rows = slice(s * MOE_CHUNK, (s + 1) * MOE_CHUNK)
            earlier = jnp.where(tok_l < tok_r + s * MOE_CHUNK, 1.0, 0.0).astype(BF16)
            rank_c = jnp.dot(earlier, r_b, preferred_element_type=F32)
            posc_ref[rows, :] = jnp.where(routed[rows], rank_c, -1.0)
            later = jnp.where(tok_r + s * MOE_CHUNK < tok_l, 1.0, 0.0).astype(BF16)
            rank_r = rank_r + jnp.dot(r_tb[:, rows], later, preferred_element_type=F32)
        posr_ref[...] = jnp.where(r_t != 0.0, rank_r, -1.0)

    lane_e = _lane_iota((tm, LANES)) == e
    pos_c = jnp.sum(jnp.where(lane_e, posc_ref[...], 0.0), axis=1, keepdims=True)
    gate_c = jnp.sum(jnp.where(lane_e, gw_ref[...], 0.0), axis=1, keepdims=True)
    pos_r = posr_ref[pl.ds(e, 1), :]
    n_routed = (jnp.max(pos_r) + 1.0).astype(I32)

    def run_chunk(base, n_rows):
        slot_rows = _row_iota((n_rows, tm)).astype(F32)
        slot_lanes = _lane_iota((MOE_CHUNK, n_rows)).astype(F32)
        pack = jnp.where(pos_r - base == slot_rows, 1.0, 0.0).astype(BF16)
        xc = jnp.dot(pack, h_ref[...], preferred_element_type=F32).astype(BF16)
        gt = jnp.dot(xc, wg_ref[...], preferred_element_type=F32)
        up = jnp.dot(xc, wu_ref[...], preferred_element_type=F32)
        act = (gt * jax.nn.sigmoid(gt) * up).astype(BF16)
        y = jnp.dot(act, wd_ref[...], preferred_element_type=F32).astype(BF16)
        for s in range(n_slabs):
            rows = slice(s * MOE_CHUNK, (s + 1) * MOE_CHUNK)
            unpack = jnp.where(pos_c[rows] - base == slot_lanes, 1.0, 0.0).astype(BF16)
            acc_ref[rows, :] += gate_c[rows] * jnp.dot(unpack, y, preferred_element_type=F32)

    def first(c, carry):
        run_chunk(0.0, MOE_CHUNK)
        return carry

    def later(c, carry):
        run_chunk((MOE_CHUNK + c * (MOE_CHUNK // 2)).astype(F32), MOE_CHUNK // 2)
        return carry

    lax.fori_loop(0, jnp.minimum(n_routed, 1), first, 0)
    n_later = (jnp.maximum(n_routed - MOE_CHUNK, 0) + MOE_CHUNK // 2 - 1) // (MOE_CHUNK // 2)
    lax.fori_loop(0, n_later, later, 0)

    @pl.when(e == pl.num_programs(2) - 1)
    def _():
        x = x_ref[...] + g2_ref[...] * acc_ref[...]
        o_ref[...] = _final_norm(x, fg_ref[...]) if final else x


def _moe(x, h, g2, gw, wg, wu, wd, final_g, tm):
    bsz, t, d = x.shape
    n_e, _, dff = wg.shape
    mrows = g2.shape[1]
    mblk = 1 if mrows == 1 else tm
    mod_spec = pl.BlockSpec((None, mblk, d), (lambda b, i, e: (b, 0, 0)) if mrows == 1 else (lambda b, i, e: (b, i, 0)))
    tok = pl.BlockSpec((None, tm, d), lambda b, i, e: (b, i, 0))
    tok_once = pl.BlockSpec((None, tm, d), lambda b, i, e: (b, i, 0), pipeline_mode=pl.Buffered(1))
    in_specs = [tok_once, tok, mod_spec, pl.BlockSpec((None, tm, LANES), lambda b, i, e: (b, i, 0)),
                pl.BlockSpec((None, d, dff), lambda b, i, e: (e, 0, 0)),
                pl.BlockSpec((None, d, dff), lambda b, i, e: (e, 0, 0)),
                pl.BlockSpec((None, dff, d), lambda b, i, e: (e, 0, 0))]
    args = [x, h, g2, gw, wg, wu, wd]
    if final_g is not None:
        in_specs.append(pl.BlockSpec((1, d), lambda b, i, e: (0, 0)))
        args.append(final_g.reshape(1, d))
    return pl.pallas_call(
        functools.partial(_moe_kernel, final=final_g is not None),
        grid=(bsz, t // tm, n_e),
        in_specs=in_specs, out_specs=tok,
        out_shape=jax.ShapeDtypeStruct((bsz, t, d), F32),
        scratch_shapes=[pltpu.VMEM((tm, d), F32), pltpu.VMEM((tm, LANES), F32), pltpu.VMEM((LANES, tm), F32)],
        compiler_params=_cparams(("parallel", "parallel", "arbitrary")),
        name="moe_routed",
    )(*args)


def _rope_tables(pos):
    inv = ROPE_THETA ** (-jnp.arange(HALF, dtype=F32) / HALF)
    ang = pos.astype(F32)[:, None] * inv[None, :]
    cos, sin = jnp.cos(ang), jnp.sin(ang)
    return jnp.tile(cos, (1, 4)), jnp.tile(jnp.concatenate([-sin, sin], axis=1), (1, 2))


def _pick_tile(n, pref):
    t = min(n, pref)
    while n % t:
        t //= 2
    return t


def _per_seq_cols(a, bsz, t, width):
    f = a.shape[1]
    a = jnp.moveaxis(a[0].reshape(f, bsz, t), 1, 0)
    return jnp.pad(a, ((0, 0), (0, 0), (0, width - t)))


def _value_blocks(past_vt, new_vt, bsz, t, lpad):
    lead = new_vt.shape[:-2]
    new_b = jnp.moveaxis(new_vt.reshape(*lead, V_ROWS, bsz, t), -2, 0)
    full = jnp.concatenate([past_vt, new_b], axis=-1)
    full = jnp.pad(full, [(0, 0)] * (full.ndim - 1) + [(0, lpad - full.shape[-1])])
    full = full.reshape(bsz, *lead, V_ROWS, lpad // KV_BLOCK, KV_BLOCK)
    return jnp.moveaxis(full, -2, -3)


def _with_ones_rows(vt):
    ones = jnp.ones(vt.shape[:-2] + (1, vt.shape[-1]), vt.dtype)
    zeros = jnp.zeros(vt.shape[:-2] + (V_ROWS - HEAD_DIM - 1, vt.shape[-1]), vt.dtype)
    return jnp.concatenate([vt, ones, zeros], axis=-2)


def _mixers(inp, past, n_past, lw, bsz, t):
    aqt, iqt, kik, avt, bqt, bk, bvt, cu, sm, smt = inp
    n_keys = n_past + t
    n_sel = min(TOPK_MAX, n_keys // 4)
    lpad = -(-n_keys // (2 * KV_BLOCK)) * 2 * KV_BLOCK
    if past is None:
        kik_all, avt_all, bk_all, bvt_all, logf_all = kik, avt, bk, bvt, sm
        hist16 = jnp.zeros((bsz, 16, POOL_WIDTH), F32)
        tq_a, tq_b, t_pad = _pick_tile(t, 256), _pick_tile(t, 256), t
    else:
        pa, pb, plf, pc = past
        pk, pv, pik = (pa[:, :, j].astype(BF16) for j in range(3))

        def join_rows(p, new):
            full = jnp.concatenate([p, new.reshape(bsz, t, new.shape[-1])], axis=1)
            return jnp.pad(full, ((0, 0), (0, lpad - n_keys), (0, 0)))

        kik_all = join_rows(jnp.concatenate([pk, pik], axis=-1), kik)
        avt_all = _value_blocks(_with_ones_rows(jnp.swapaxes(pv, 1, 2)), avt[0, 0], bsz, t, lpad)
        bk_all = join_rows(pb[:, :, 0].astype(BF16).reshape(bsz, n_past, 384), bk)
        pvt = jnp.transpose(pb[:, :, 1].astype(BF16), (0, 2, 3, 1))
        bvt_all = _value_blocks(_with_ones_rows(pvt), bvt[0, :, 0], bsz, t, lpad)
        logf_all = join_rows(jnp.pad(plf, ((0, 0), (0, 0), (0, LANES - B_HEADS))), sm)
        hist16 = jnp.pad(pc, ((0, 0), (1, 0), (0, 0)))
        tq_a = tq_b = t_pad = LANES
        aqt, iqt, bqt, smt = (_per_seq_cols(a, bsz, t, t_pad) for a in (aqt, iqt, bqt, smt))

    oa = _dsa(aqt, iqt, smt, kik_all, avt_all, tq=tq_a, n_keys=n_keys, q_pos0=n_past, n_sel=n_sel)
    ck = _cum_logf(logf_all, 2 * KV_BLOCK)
    ob = _fox(bqt, bk_all, bvt_all, ck, tq=tq_b, n_keys=n_keys, q_pos0=n_past)
    cu = cu.reshape(bsz, t, POOL_WIDTH)
    oc = _pool(cu, hist16, lw["pool_bd"], lw["pool_scale"], tc=_pick_tile(t, 1024), start_pos=n_past)
    return oa[:, :t], ob[:, :t], oc


def _layer(x, mod, past, n_past, pos_tab, lw, layer, final_g, per_token):
    bsz, t, d = x.shape
    sh1, sc1, g1, sh2, sc2, g2 = mod
    if per_token:
        xt = x.reshape(1, bsz * t, d)
        sh1, sc1, g1, sh2, sc2, g2 = (jnp.broadcast_to(m, (bsz, t, d)).reshape(1, bsz * t, d) for m in mod)
        cos, sin = (jnp.tile(a, (bsz, 1)) for a in pos_tab)
    else:
        xt = x
        cos, sin = pos_tab
    tm = KV_BLOCK
    (aqt, iqt, nat, kik, avt, nbt, bqt, bk, bvt, cu, sm, smt, gates) = _in_proj(
        xt, sc1, sh1, lw["norm_mix_g"], lw["w_in"], lw["bf_bias"], cos, sin, tm)
    oa, ob, oc = _mixers((aqt, iqt, kik, avt, bqt, bk, bvt, cu, sm, smt), past, n_past, lw, bsz, t)

    def flat(a):
        return a.reshape(xt.shape[0], xt.shape[1], a.shape[-1])

    router = (lw["router_w"], lw["router_b"]) if layer % 2 else None
    res = _merge(xt, flat(oa), flat(ob), flat(oc), gates, g1, sc2, sh2, lw["norm_ffn_g"],
                 lw["w_br_a"], lw["w_br_b"], lw["w_br_c"], lw["w_out"], router, _pick_tile(xt.shape[1], 512))
    tmf = _pick_tile(xt.shape[1], 512)
    if layer % 2 == 0:
        x_mid, h2 = res
        x_new = _ffn(x_mid, h2, g2, lw["ffn_wg"], lw["ffn_wu"], lw["ffn_wd"], final_g, tmf)
    else:
        x_mid, h2, gw = res
        x_new = _moe(x_mid, h2, g2, gw, lw["moe_wg"], lw["moe_wu"], lw["moe_wd"], final_g,
                     _pick_tile(xt.shape[1], 1024))
    def token_major(a, *feat):
        a = a.reshape(a.shape[0], *feat, -1, t) if per_token else a.reshape(a.shape[0], *feat, 1, t)
        a = jnp.moveaxis(a, (-2, -1), (1, 2))
        return a.reshape(bsz, t, *feat)

    new_a = token_major(nat, 3, HEAD_DIM)
    new_b = token_major(nbt, 2, B_HEADS, HEAD_DIM)
    new_logf = sm.reshape(bsz, t, LANES)[:, :, :B_HEADS]
    new_pool = cu.reshape(bsz, t, POOL_WIDTH)[:, t - POOL_HIST:, :]
    return x_new.reshape(bsz, t, d), (new_a, new_b, new_logf, new_pool)


def kernel(x_prompt, x_sample, cache_a_kvi, cache_b_kv, cache_b_logf, state_c_pool, c_prompt, c_sample,
           ada_w, ada_b, norm_mix_g, w_in, b_forget, pool_w, pool_scale, w_br_a, w_br_b, w_br_c, w_out,
           norm_ffn_g, ffn_w_gate, ffn_w_up, ffn_w_down, moe_router_w, moe_router_b, moe_w_gate,
           moe_w_up, moe_w_down, final_norm_g):
    depth = ada_w.shape[0]
    bp, tp, d = x_prompt.shape
    bs, ts, _ = x_sample.shape
    n_past = cache_a_kvi.shape[2]
    assert tp % KV_BLOCK == 0 and (bs * ts) % KV_BLOCK == 0 and ts <= LANES

    rows = -(-(bp + bs) // 8) * 8
    c_all = jnp.pad(jnp.concatenate([c_prompt, c_sample], axis=0), ((0, rows - bp - bs), (0, 0)))
    mod_all = _ada(c_all, ada_w, ada_b)

    tab_p = _rope_tables(jnp.arange(tp))
    tab_s = _rope_tables(n_past + jnp.arange(ts))

    xp, xs = x_prompt, x_sample
    outs_p, outs_s = [], []
    for layer in range(depth):
        j = layer // 2
        w_l, bias_l = _in_weights(w_in[layer], b_forget[layer])
        pw = pool_w[layer]
        pool_bd = jnp.zeros((POOL_WIDTH, POOL_WIDTH), F32)
        for g in range(len(POOL_WINDOWS)):
            sl = slice(g * POOL_GROUP_DIM, (g + 1) * POOL_GROUP_DIM)
            pool_bd = pool_bd.at[sl, sl].set(pw[g])
        lw = dict(w_in=w_l, bf_bias=bias_l, norm_mix_g=norm_mix_g[layer], norm_ffn_g=norm_ffn_g[layer],
                  pool_bd=pool_bd.astype(BF16), pool_scale=pool_scale[layer],
                  w_br_a=w_br_a[layer].astype(BF16), w_br_b=w_br_b[layer].astype(BF16),
                  w_br_c=w_br_c[layer].astype(BF16), w_out=w_out[layer].astype(BF16))
        if layer % 2 == 0:
            lw.update(ffn_wg=ffn_w_gate[j].astype(BF16), ffn_wu=ffn_w_up[j].astype(BF16),
                      ffn_wd=ffn_w_down[j].astype(BF16))
        else:
            rw = jnp.pad(moe_router_w[j], ((0, 0), (0, LANES - N_EXPERTS)))
            rw_hi = rw.astype(BF16)
            lw.update(router_w=jnp.stack([rw_hi, (rw - rw_hi.astype(F32)).astype(BF16)]),
                      router_b=jnp.pad(moe_router_b[j], (0, LANES - N_EXPERTS)).reshape(1, LANES),
                      moe_wg=moe_w_gate[j].astype(BF16), moe_wu=moe_w_up[j].astype(BF16),
                      moe_wd=moe_w_down[j].astype(BF16))
        final_g = final_norm_g if layer == depth - 1 else None
        mod_p = [m[:, None, :] for m in jnp.split(mod_all[layer, :bp], 6, axis=-1)]
        mod_s = [m[:, None, :] for m in jnp.split(mod_all[layer, bp:bp + bs], 6, axis=-1)]
        xp, new_p = _layer(xp, mod_p, None, 0, tab_p, lw, layer, final_g, per_token=False)
        past = (cache_a_kvi[layer], cache_b_kv[layer], cache_b_logf[layer], state_c_pool[layer])
        xs, new_s = _layer(xs, mod_s, past, n_past, tab_s, lw, layer, final_g, per_token=True)
        outs_p.append(new_p)
        outs_s.append(new_s)

    def stack(outs, k):
        return jnp.stack([o[k] for o in outs])

    return (xp, xs,
            stack(outs_p, 0), stack(outs_p, 1), stack(outs_p, 2), stack(outs_p, 3),
            stack(outs_s, 0), stack(outs_s, 1), stack(outs_s, 2), stack(outs_s, 3))
```

```python
import functools

import jax
import jax.numpy as jnp
import numpy as np
from jax import lax
from jax.experimental import pallas as pl
from jax.experimental.pallas import tpu as pltpu

F32 = jnp.float32
BF16 = jnp.bfloat16
I32 = jnp.int32

D_MODEL = 1024
CHUNK = 64
HEAD_DIM = 64
HALF = HEAD_DIM // 2
ROPE_THETA = 10000.0
NORM_EPS = 1e-6
A_HEADS = 6
IDX_HEADS = 4
TOPK_MAX = 256
B_HEADS = 6
POOL_WINDOWS = (2, 4, 8, 16)
POOL_GROUP_DIM = 64
POOL_WIDTH = 256
POOL_HIST = 15
N_EXPERTS = 8
LANES = 128
SUBLANES = 8
LOG2E = 1.4426950408889634
QK_SCALE = HEAD_DIM ** -0.5 * LOG2E
KV_BLOCK = 256
V_ROWS = HEAD_DIM + 16
MOE_CHUNK = 256
VMEM_LIMIT = 56 * 1024 * 1024
NEG_INF = float("-inf")
KEY_DT = jnp.bfloat16

C_AQ, C_IQ, C_A, C_B, C_CU, C_SM, C_GATE, C_END = 0, 384, 640, 896, 2048, 2304, 2432, 5504


def _cparams(sem):
    return pltpu.CompilerParams(dimension_semantics=sem, vmem_limit_bytes=VMEM_LIMIT)


def _const_spec(shape):
    nd = len(shape)
    return pl.BlockSpec(shape, lambda *_: (0,) * nd, pipeline_mode=pl.Buffered(1))


def _lane_iota(shape):
    return lax.broadcasted_iota(I32, shape, len(shape) - 1)


def _row_iota(shape):
    return lax.broadcasted_iota(I32, shape, len(shape) - 2)


def _ada_kernel(c_ref, w_ref, b_ref, o_ref):
    c = c_ref[...]
    s = c * jax.nn.sigmoid(c)
    o_ref[...] = jnp.dot(s, w_ref[...], preferred_element_type=F32,
                         precision=lax.Precision.HIGHEST) + b_ref[...]


def _ada(c_all, ada_w, ada_b):
    depth, d, n = ada_w.shape
    rows = c_all.shape[0]
    tn = 1536
    return pl.pallas_call(
        _ada_kernel,
        grid=(depth, n // tn),
        in_specs=[pl.BlockSpec((rows, d), lambda l, j: (0, 0)),
                  pl.BlockSpec((None, d, tn), lambda l, j: (l, 0, j)),
                  pl.BlockSpec((None, 1, tn), lambda l, j: (l, 0, j))],
        out_specs=pl.BlockSpec((None, rows, tn), lambda l, j: (l, 0, j)),
        out_shape=jax.ShapeDtypeStruct((depth, rows, n), F32),
        compiler_params=_cparams(("arbitrary", "arbitrary")),
        name="ada_mod",
    )(c_all, ada_w, ada_b.reshape(depth, 1, n))


def _in_kernel(x_ref, sc_ref, sh_ref, g_ref, w_ref, bf_ref, cos_ref, sin_ref,
               aqt_ref, iqt_ref, nat_ref, kik_ref, avt_ref, nbt_ref, bqt_ref, bk_ref, bvt_ref,
               cu_ref, sm_ref, smt_ref, gate_ref):
    x = x_ref[...]
    ms = jnp.mean(x * x, axis=-1, keepdims=True)
    y = x * lax.rsqrt(ms + NORM_EPS) * g_ref[...]
    h = (y * (1.0 + sc_ref[...]) + sh_ref[...]).astype(BF16)
    tm = x.shape[0]

    def mm(a, b):
        return jnp.dot(h, w_ref[:, a:b], preferred_element_type=F32)

    cos = cos_ref[...]
    sin = sin_ref[...]
    lane = _lane_iota((tm, LANES))
    low = lane < HEAD_DIM
    first_half = (lane & HALF) == 0

    def rope(z):
        swapped = jnp.where(first_half, pltpu.roll(z, LANES - HALF, 1), pltpu.roll(z, HALF, 1))
        return z * cos + swapped * sin

    zeros64 = jnp.zeros((HEAD_DIM, tm), BF16)
    ones_rows = jnp.where(_row_iota((V_ROWS - HEAD_DIM, tm)) == 0, 1.0, 0.0).astype(BF16)

    def put_heads(ref, zt, p, slot_even, slot_odd):
        for hh, slot in ((0, slot_even), (1, slot_odd)):
            base = (2 * p + hh) * LANES
            ref[base + slot * HEAD_DIM:base + (slot + 1) * HEAD_DIM, :] = zt[hh * HEAD_DIM:(hh + 1) * HEAD_DIM]
            ref[base + (1 - slot) * HEAD_DIM:base + (2 - slot) * HEAD_DIM, :] = zeros64

    z = mm(C_AQ, C_IQ)
    for p in range(3):
        zt = (rope(z[:, p * LANES:(p + 1) * LANES]) * QK_SCALE).T.astype(BF16)
        put_heads(aqt_ref, zt, p, 0, 0)
    z = mm(C_IQ, C_A)
    for p in range(2):
        zt = rope(z[:, p * LANES:(p + 1) * LANES]).T.astype(BF16)
        put_heads(iqt_ref, zt, p, 1, 1)

    z = mm(C_A, C_B)
    kv = z[:, :LANES]
    r0 = jnp.where(low, rope(kv), kv)
    r1 = rope(z[:, LANES:])
    r0t = r0.T
    nat_ref[0:LANES, :] = r0t
    nat_ref[LANES:, :] = r1.T[0:HEAD_DIM, :]
    kik_ref[...] = jnp.where(low, r0, pltpu.roll(r1, HEAD_DIM, 1)).astype(BF16)
    avt_ref[0:HEAD_DIM, :] = r0t[HEAD_DIM:, :].astype(BF16)
    avt_ref[HEAD_DIM:, :] = ones_rows

    z = mm(C_B, C_CU)
    bk_ref[...] = z[:, 384:768].astype(BF16)
    for p in range(3):
        zt = (z[:, p * LANES:(p + 1) * LANES] * QK_SCALE).T.astype(BF16)
        put_heads(bqt_ref, zt, p, 0, 1)
        nbt_ref[p * LANES:(p + 1) * LANES, :] = z[:, 384 + p * LANES:384 + (p + 1) * LANES].T
        vt = z[:, 768 + p * LANES:768 + (p + 1) * LANES].T
        nbt_ref[384 + p * LANES:384 + (p + 1) * LANES, :] = vt
        vt = vt.astype(BF16)
        for hh in range(2):
            bvt_ref[2 * p + hh, 0:HEAD_DIM, :] = vt[hh * HEAD_DIM:(hh + 1) * HEAD_DIM]
            bvt_ref[2 * p + hh, HEAD_DIM:, :] = ones_rows

    cu_ref[...] = mm(C_CU, C_SM)

    z = mm(C_SM, C_GATE)
    t = z + bf_ref[...]
    logf = jnp.minimum(t, 0.0) - jnp.log1p(jnp.exp(-jnp.abs(t)))
    sm = jnp.where(lane < B_HEADS, logf, z)
    sm_ref[...] = sm
    smt_ref[...] = sm.T[0:16, :]

    for c in range(3):
        gate_ref[:, c * D_MODEL:(c + 1) * D_MODEL] = jax.nn.sigmoid(
            mm(C_GATE + c * D_MODEL, C_GATE + (c + 1) * D_MODEL)).astype(BF16)


def _in_weights(w_in_l, b_forget_l):
    d = w_in_l.shape[0]
    sizes = (384, 64, 64, 256, 4, 64, 384, 384, 384, 6, 256, 3072)
    o = np.concatenate([[0], np.cumsum(sizes)])
    w_in_l = w_in_l.astype(BF16)
    cols = [w_in_l[:, o[0]:o[1]],
            w_in_l[:, o[3]:o[4]],
            w_in_l[:, o[1]:o[3]], w_in_l[:, o[5]:o[6]], jnp.zeros((d, 64), BF16),
            w_in_l[:, o[6]:o[9]],
            w_in_l[:, o[10]:o[11]],
            w_in_l[:, o[9]:o[10]], jnp.zeros((d, 2), BF16), w_in_l[:, o[4]:o[5]], jnp.zeros((d, LANES - 12), BF16),
            w_in_l[:, o[11]:o[12]]]
    w = jnp.concatenate(cols, axis=1)
    assert w.shape[1] == C_END, w.shape
    bias = jnp.concatenate([b_forget_l, jnp.zeros((LANES - B_HEADS,), F32)]).reshape(1, LANES)
    return w, bias


def _in_proj(x, sc, sh, g, w, bias, cos, sin, tm):
    bsz, t, d = x.shape
    mrows = sc.shape[1]
    mblk = 1 if mrows == 1 else tm
    mod_spec = pl.BlockSpec((None, mblk, d), (lambda b, i: (b, 0, 0)) if mrows == 1 else (lambda b, i: (b, i, 0)))
    nblk = t // tm

    def rows(n, dt):
        return pl.BlockSpec((None, tm, n), lambda b, i: (b, i, 0)), jax.ShapeDtypeStruct((bsz, t, n), dt)

    def cols(n, dt):
        return pl.BlockSpec((None, n, tm), lambda b, i: (b, 0, i)), jax.ShapeDtypeStruct((bsz, n, t), dt)

    outs = [cols(A_HEADS * LANES, BF16), cols(IDX_HEADS * LANES, BF16), cols(192, F32), rows(LANES, BF16),
            (pl.BlockSpec((None, None, V_ROWS, tm), lambda b, i: (b, i, 0, 0)),
             jax.ShapeDtypeStruct((bsz, nblk, V_ROWS, tm), BF16)),
            cols(768, F32), cols(B_HEADS * LANES, BF16), rows(384, BF16),
            (pl.BlockSpec((None, B_HEADS, None, V_ROWS, tm), lambda b, i: (b, 0, i, 0, 0)),
             jax.ShapeDtypeStruct((bsz, B_HEADS, nblk, V_ROWS, tm), BF16)),
            rows(256, F32), rows(LANES, F32), cols(16, F32), rows(3 * D_MODEL, BF16)]
    return pl.pallas_call(
        _in_kernel,
        grid=(bsz, nblk),
        in_specs=[pl.BlockSpec((None, tm, d), lambda b, i: (b, i, 0)), mod_spec, mod_spec,
                  _const_spec((1, d)), _const_spec(w.shape), _const_spec((1, LANES)),
                  pl.BlockSpec((tm, LANES), lambda b, i: (i, 0)),
                  pl.BlockSpec((tm, LANES), lambda b, i: (i, 0))],
        out_specs=[o[0] for o in outs],
        out_shape=[o[1] for o in outs],
        compiler_params=_cparams(("parallel", "parallel")),
        name="in_proj",
    )(x, sc, sh, g.reshape(1, d), w, bias, cos, sin)


def _cum_kernel(x_ref, o_ref, carry_ref):
    @pl.when(pl.program_id(1) == 0)
    def _():
        carry_ref[...] = jnp.zeros_like(carry_ref)

    tc = x_ref.shape[0]
    tri = jnp.where(_lane_iota((tc, tc)) <= _row_iota((tc, tc)), 1.0, 0.0).astype(BF16)
    cum = carry_ref[0:1, :]
    rest = x_ref[...]
    for _ in range(3):
        piece = rest.astype(BF16)
        cum = cum + jnp.dot(tri, piece, preferred_element_type=F32)
        rest = rest - piece.astype(F32)
    carry_ref[...] = jnp.broadcast_to(cum[tc - 1:tc, :], carry_ref.shape)
    pieces = []
    rest = cum * LOG2E
    for _ in range(3):
        piece = rest.astype(BF16)
        pieces.append(piece.astype(F32))
        rest = rest - pieces[-1]
    lane = _lane_iota((tc, LANES))
    for p in range(B_HEADS // 2):
        slab = jnp.zeros((tc, LANES), F32)
        for hh in range(2):
            for j, piece in enumerate(pieces):
                dst, src = 3 * hh + j, 2 * p + hh
                slab = jnp.where(lane == dst, pltpu.roll(piece, (dst - src) % LANES, 1), slab)
        o_ref[p] = slab.astype(BF16)


def _cum_logf(x, tc):
    bsz, t, n = x.shape
    return pl.pallas_call(
        _cum_kernel,
        grid=(bsz, t // tc),
        in_specs=[pl.BlockSpec((None, tc, n), lambda b, i: (b, i, 0))],
        out_specs=pl.BlockSpec((None, B_HEADS // 2, tc, LANES), lambda b, i: (b, 0, i, 0)),
        out_shape=jax.ShapeDtypeStruct((bsz, B_HEADS // 2, t, LANES), BF16),
        scratch_shapes=[pltpu.VMEM((SUBLANES, LANES), F32)],
        compiler_params=_cparams(("parallel", "arbitrary")),
        name="logf_cumsum",
    )(x)


def _sum_keys(x):
    part = x.reshape(x.shape[0] // SUBLANES, SUBLANES, x.shape[1]).sum(axis=0)
    return jnp.sum(part, axis=0, keepdims=True)


def _max_keys(x):
    part = x.reshape(x.shape[0] // SUBLANES, SUBLANES, x.shape[1]).max(axis=0)
    return jnp.max(part, axis=0, keepdims=True)


def _dsa_kernel(aqt_ref, iqt_ref, smt_ref, kik_ref, avt_ref, o_ref, key_ref,
                *, tq, tk, n_keys, q_pos0, n_sel):
    i = pl.program_id(1)
    pos_first = q_pos0 + i * tq
    last_chunk = (pos_first + tq - 1) // CHUNK
    n_adm = jnp.minimum((last_chunk + 1) * CHUNK, n_keys)
    nkb = (n_adm + tk - 1) // tk
    n_pairs = (nkb + 1) // 2

    q_pos = pos_first + _lane_iota((1, tq))
    q_lim = jnp.minimum((q_pos // CHUNK + 1) * CHUNK, n_keys)
    key_row = _row_iota((tk, tq))

    def keys(kb):
        return kik_ref[pl.ds(pl.multiple_of(kb * tk, tk), tk), :]

    iq4 = jnp.concatenate([iqt_ref[hd * LANES:(hd + 1) * LANES, :] for hd in range(IDX_HEADS)], axis=1)
    smt = smt_ref[...]
    w_rows = [smt[8 + hd:9 + hd, :] for hd in range(IDX_HEADS)]

    def score_body(kb, carry):
        s4 = jnp.dot(keys(kb), iq4, preferred_element_type=F32)
        score = w_rows[0] * jnp.maximum(s4[:, 0:tq], 0.0)
        for hd in range(1, IDX_HEADS):
            score = score + w_rows[hd] * jnp.maximum(s4[:, hd * tq:(hd + 1) * tq], 0.0)
        score = jnp.where(key_row < q_lim - kb * tk, score, NEG_INF)
        key_ref[kb] = score.astype(KEY_DT)
        return carry

    lax.fori_loop(0, nkb, score_body, 0)

    @pl.when(nkb % 2 == 1)
    def _():
        key_ref[nkb] = jnp.full((tk, tq), NEG_INF, KEY_DT)

    one, zero = jnp.ones((), KEY_DT), jnp.zeros((), KEY_DT)
    packed_rows = 2 * SUBLANES

    def count(cand, strict):
        def hits(blk):
            h = jnp.where((blk > cand) if strict else (blk >= cand), one, zero)
            parts = [h[r * packed_rows:(r + 1) * packed_rows] for r in range(tk // packed_rows)]
            while len(parts) > 1:
                parts = [a + b for a, b in zip(parts[::2], parts[1::2])]
            return parts[0].astype(F32)

        def body(j, acc):
            return acc + hits(key_ref[2 * j]) + hits(key_ref[2 * j + 1])

        acc = lax.fori_loop(0, n_pairs, body, jnp.zeros((packed_rows, tq), F32))
        return jnp.sum(acc, axis=0, keepdims=True)

    def pattern_value(u):
        bits = jnp.where(u >= 0x8000, u & 0x7FFF, (~u) & 0xFFFF)
        return lax.bitcast_convert_type(lax.shift_left(bits, 16), F32).astype(KEY_DT)

    def bit_body(b, u):
        cand_u = u | lax.shift_left(jnp.int32(1), 15 - b)
        cnt = count(pattern_value(cand_u), False)
        return jnp.where(cnt >= n_sel, cand_u, u)

    u_thr = jnp.maximum(lax.fori_loop(0, 16, bit_body, jnp.zeros((1, tq), I32)), 0x007F)
    thr_16 = pattern_value(u_thr)

    def finer():
        lo = thr_16.astype(F32)
        hi = jnp.where(u_thr >= 0xFF80, jnp.inf, pattern_value(u_thr + 1).astype(F32))

        def halve(_, lo_hi):
            lo, hi = lo_hi
            cand = (0.5 * lo + 0.5 * hi).astype(KEY_DT)
            enough = count(cand, False) >= n_sel
            return jnp.where(enough, cand.astype(F32), lo), jnp.where(enough, hi, cand.astype(F32))

        thr_f = lax.fori_loop(0, 16, halve, (lo, hi))[0].astype(KEY_DT)
        return thr_f, count(thr_f, True)

    above_16 = count(thr_16, True)
    thr_k, above = lax.cond(jnp.min(n_sel - above_16) <= 0.0, finer, lambda: (thr_16, above_16))
    need = jnp.maximum(n_sel - above, 0.0)
    thr = thr_k.astype(F32)

    aq6 = jnp.concatenate([aqt_ref[hd * LANES:(hd + 1) * LANES, :] for hd in range(A_HEADS)], axis=1)

    half = tk // 2
    lower = jnp.where(_lane_iota((half, half)) <= _row_iota((half, half)), 1.0, 0.0).astype(BF16)

    def attend():
        def body(kb, carry):
            eq_seen, ms, accs = carry
            blk = key_ref[kb].astype(F32)
            eq = blk == thr
            eq_f = jnp.where(eq, 1.0, 0.0)
            prefs = []
            for e in (eq_f[:half], eq_f[half:]):
                prefs.append(jnp.dot(lower, e.astype(BF16), preferred_element_type=F32) + eq_seen)
                eq_seen = eq_seen + _sum_keys(e)
            slack = jnp.where(blk >= thr, need - jnp.where(eq, jnp.concatenate(prefs, axis=0), 0.0), -1.0)
            bias = jnp.where(slack >= 0.0, jnp.where(jnp.abs(blk) < jnp.inf, 0.0, NEG_INF), NEG_INF)
            logits = jnp.dot(keys(kb), aq6, preferred_element_type=F32)
            vts = (avt_ref[2 * kb], avt_ref[2 * kb + 1])
            new_ms, new_accs = [], []
            for p in range(A_HEADS // 2):
                ps, alphas = [], []
                for hd in (2 * p, 2 * p + 1):
                    lg = logits[:, hd * tq:(hd + 1) * tq] + bias
                    m_old = ms[hd]
                    m_new = jnp.maximum(m_old, _max_keys(lg))
                    m_safe = jnp.where(m_new == NEG_INF, 0.0, m_new)
                    ps.append(jnp.exp2(lg - m_safe).astype(BF16))
                    alphas.append(jnp.exp2(m_old - m_safe))
                    new_ms.append(m_new)
                p2 = jnp.concatenate(ps, axis=1)
                pv = (jnp.dot(vts[0], p2[:tk // 2], preferred_element_type=F32)
                      + jnp.dot(vts[1], p2[tk // 2:], preferred_element_type=F32))
                new_accs.append(jnp.concatenate(alphas, axis=1) * accs[p] + pv)
            return eq_seen, tuple(new_ms), tuple(new_accs)

        init = (jnp.zeros((1, tq), F32),
                tuple(jnp.full((1, tq), NEG_INF, F32) for _ in range(A_HEADS)),
                tuple(jnp.zeros((V_ROWS, 2 * tq), F32) for _ in range(A_HEADS // 2)))
        return lax.fori_loop(0, nkb, body, init)[2]

    accs = attend()

    outs = []
    for p in range(A_HEADS // 2):
        o2 = accs[p][0:HEAD_DIM] / accs[p][HEAD_DIM:HEAD_DIM + 1]
        outs += [o2[:, 0:tq], o2[:, tq:2 * tq]]
    o_ref[...] = jnp.concatenate(outs, axis=0).T.astype(o_ref.dtype)


def _dsa(aqt, iqt, smt, kik, avt, *, tq, n_keys, q_pos0, n_sel):
    bsz, _, t_q = aqt.shape
    _, nblk, _, tkv = avt.shape
    assert nblk % 2 == 0
    tk = 2 * tkv
    kern = functools.partial(_dsa_kernel, tq=tq, tk=tk, n_keys=n_keys, q_pos0=q_pos0, n_sel=n_sel)
    return pl.pallas_call(
        kern,
        grid=(bsz, t_q // tq),
        in_specs=[pl.BlockSpec((None, A_HEADS * LANES, tq), lambda b, i: (b, 0, i)),
                  pl.BlockSpec((None, IDX_HEADS * LANES, tq), lambda b, i: (b, 0, i)),
                  pl.BlockSpec((None, 16, tq), lambda b, i: (b, 0, i)),
                  pl.BlockSpec((None, nblk * tkv, LANES), lambda b, i: (b, 0, 0)),
                  pl.BlockSpec((None, nblk, V_ROWS, tkv), lambda b, i: (b, 0, 0, 0))],
        out_specs=pl.BlockSpec((None, tq, 384), lambda b, i: (b, i, 0)),
        out_shape=jax.ShapeDtypeStruct((bsz, t_q, 384), BF16),
        scratch_shapes=[pltpu.VMEM((2 * ((nblk // 2 + 1) // 2), tk, tq), KEY_DT)],
        compiler_params=_cparams(("parallel", "arbitrary")),
        name="dsa_attention",
    )(aqt, iqt, smt, kik, avt)


def _fox_kernel(qt_ref, k_ref, vt_ref, ck_ref, o_ref, *, tq, tk, n_keys, q_pos0):
    i = pl.program_id(1)
    pos_first = q_pos0 + i * tq
    n_full = pos_first // tk
    nkb = (jnp.minimum(pos_first + tq, n_keys) + tk - 1) // tk
    q_pos = pos_first + _lane_iota((1, tq))
    key_row = _row_iota((tk, tq))
    piece_row = _row_iota((LANES, tq))
    qts = []
    for hd in range(B_HEADS):
        minus = jnp.where((piece_row >= 3 * (hd % 2)) & (piece_row < 3 * (hd % 2) + 3), -1.0, 0.0).astype(BF16)
        qts.append(jnp.concatenate([qt_ref[hd * LANES:(hd + 1) * LANES, :], minus], axis=0))

    def step(kbs, state, masked):
        logits = []
        for hd in range(B_HEADS):
            for kb in kbs:
                rows = pl.ds(pl.multiple_of(kb * tk, tk), tk)
                kblk = jnp.concatenate([k_ref[rows, (hd // 2) * LANES:(hd // 2 + 1) * LANES],
                                        ck_ref[hd // 2, rows, :]], axis=1)
                lg = jnp.dot(kblk, qts[hd], preferred_element_type=F32)
                if masked:
                    lg = jnp.where(key_row <= q_pos - kb * tk, lg, NEG_INF)
                logits.append(lg)
        new = []
        for hd in range(B_HEADS):
            m_old, acc = state[hd]
            lgs = logits[hd * len(kbs):(hd + 1) * len(kbs)]
            m_new = m_old
            for lg in lgs:
                m_new = jnp.maximum(m_new, _max_keys(lg))
            m_safe = jnp.where(m_new == NEG_INF, 0.0, m_new) if masked else m_new
            acc = jnp.exp2(m_old - m_safe) * acc
            for kb, lg in zip(kbs, lgs):
                p = jnp.exp2(lg - m_safe).astype(BF16)
                acc = acc + jnp.dot(vt_ref[hd, kb], p, preferred_element_type=F32)
            new.append((m_new, acc))
        return tuple(new)

    init = tuple((jnp.full((1, tq), NEG_INF, F32), jnp.zeros((V_ROWS, tq), F32)) for _ in range(B_HEADS))
    state = lax.fori_loop(0, n_full // 2, lambda j, st: step((2 * j, 2 * j + 1), st, False), init)
    state = lax.fori_loop(2 * (n_full // 2), nkb, lambda kb, st: step((kb,), st, True), state)
    outs = [acc[0:HEAD_DIM] / acc[HEAD_DIM:HEAD_DIM + 1] for _, acc in state]
    o_ref[...] = jnp.concatenate(outs, axis=0).T.astype(o_ref.dtype)


def _fox(qt, k, vt, ck, *, tq, n_keys, q_pos0):
    bsz, _, t_q = qt.shape
    _, _, nblk, _, tk = vt.shape
    lpad = nblk * tk
    kern = functools.partial(_fox_kernel, tq=tq, tk=tk, n_keys=n_keys, q_pos0=q_pos0)
    return pl.pallas_call(
        kern,
        grid=(bsz, t_q // tq),
        in_specs=[pl.BlockSpec((None, B_HEADS * LANES, tq), lambda b, i: (b, 0, i)),
                  pl.BlockSpec((None, lpad, 384), lambda b, i: (b, 0, 0)),
                  pl.BlockSpec((None, B_HEADS, nblk, V_ROWS, tk), lambda b, i: (b, 0, 0, 0, 0)),
                  pl.BlockSpec((None, B_HEADS // 2, lpad, LANES), lambda b, i: (b, 0, 0, 0))],
        out_specs=pl.BlockSpec((None, tq, 384), lambda b, i: (b, i, 0)),
        out_shape=jax.ShapeDtypeStruct((bsz, t_q, 384), BF16),
        compiler_params=_cparams(("parallel", "arbitrary")),
        name="fox_attention",
    )(qt, k, vt, ck)


def _pool_kernel(cur_ref, prev_ref, hist_ref, w_ref, s_ref, o_ref, ext, *, tc, start_pos):
    i = pl.program_id(1)
    cur = cur_ref[...]
    ext[0:16, :] = jnp.where(i == 0, hist_ref[...], prev_ref[tc - 16:, :])
    ext[16:, :] = cur
    pos = start_pos + i * tc + lax.broadcasted_iota(I32, (tc, POOL_WIDTH), 0)
    lane = _lane_iota((tc, POOL_WIDTH))
    run = cur
    pooled = jnp.zeros_like(cur)
    k = 1
    for g, w in enumerate(POOL_WINDOWS):
        while k < w:
            run = run + ext[16 - k:16 - k + tc, :]
            k += 1
        cnt = jnp.minimum(pos + 1, w).astype(F32)
        in_group = (lane >= g * POOL_GROUP_DIM) & (lane < (g + 1) * POOL_GROUP_DIM)
        pooled = jnp.where(in_group, run / cnt, pooled)
    z = (pooled - cur).astype(BF16)
    o_ref[...] = (jnp.dot(z, w_ref[...], preferred_element_type=F32) * s_ref[...]).astype(o_ref.dtype)


def _pool(cu, hist16, w_bd, scale, *, tc, start_pos):
    bsz, t, n = cu.shape
    kern = functools.partial(_pool_kernel, tc=tc, start_pos=start_pos)
    return pl.pallas_call(
        kern,
        grid=(bsz, t // tc),
        in_specs=[pl.BlockSpec((None, tc, n), lambda b, i: (b, i, 0)),
                  pl.BlockSpec((None, tc, n), lambda b, i: (b, jnp.maximum(i - 1, 0), 0)),
                  pl.BlockSpec((None, 16, n), lambda b, i: (b, 0, 0)),
                  _const_spec((n, n)), _const_spec((1, n))],
        out_specs=pl.BlockSpec((None, tc, n), lambda b, i: (b, i, 0)),
        out_shape=jax.ShapeDtypeStruct((bsz, t, n), BF16),
        scratch_shapes=[pltpu.VMEM((16 + tc, n), F32)],
        compiler_params=_cparams(("parallel", "arbitrary")),
        name="pool_mixer",
    )(cu, cu, hist16, w_bd, scale.reshape(1, n))


def _route(logits):
    lane = _lane_iota(logits.shape).astype(F32)
    lg = jnp.where(lane < N_EXPERTS, logits, NEG_INF)
    m1 = jnp.max(lg, axis=1, keepdims=True)
    i1 = jnp.min(jnp.where(lg == m1, lane, float(LANES)), axis=1, keepdims=True)
    hot1 = lane == i1
    lg2 = jnp.where(hot1, NEG_INF, lg)
    m2 = jnp.max(lg2, axis=1, keepdims=True)
    i2 = jnp.min(jnp.where(lg2 == m2, lane, float(LANES)), axis=1, keepdims=True)
    hot2 = lane == i2
    e2 = jnp.exp(m2 - m1)
    den = 1.0 + e2
    return jnp.where(hot1, 1.0 / den, 0.0) + jnp.where(hot2, e2 / den, 0.0)


def _merge_kernel(x_ref, oa_ref, ob_ref, oc_ref, gate_ref, g1_ref, sc2_ref, sh2_ref, g_ref,
                  wa_ref, wb_ref, wc_ref, wo_ref, *rest, moe):
    if moe:
        rw_ref, rb_ref, xo_ref, h_ref, gw_ref = rest
    else:
        xo_ref, h_ref = rest
    d = D_MODEL
    merged = (gate_ref[:, 0:d] * jnp.dot(oa_ref[...], wa_ref[...], preferred_element_type=F32)
              + gate_ref[:, d:2 * d] * jnp.dot(ob_ref[...], wb_ref[...], preferred_element_type=F32)
              + gate_ref[:, 2 * d:3 * d] * jnp.dot(oc_ref[...], wc_ref[...], preferred_element_type=F32))
    x = x_ref[...] + g1_ref[...] * jnp.dot(merged.astype(BF16), wo_ref[...], preferred_element_type=F32)
    xo_ref[...] = x
    ms = jnp.mean(x * x, axis=-1, keepdims=True)
    y = x * lax.rsqrt(ms + NORM_EPS) * g_ref[...]
    h = y * (1.0 + sc2_ref[...]) + sh2_ref[...]
    h_ref[...] = h.astype(BF16)
    if moe:
        h_hi = h.astype(BF16)
        h_lo = (h - h_hi.astype(F32)).astype(BF16)
        logits = (jnp.dot(h_hi, rw_ref[0], preferred_element_type=F32)
                  + jnp.dot(h_lo, rw_ref[0], preferred_element_type=F32)
                  + jnp.dot(h_hi, rw_ref[1], preferred_element_type=F32)) + rb_ref[...]
        gw_ref[...] = _route(logits)


def _merge(x, oa, ob, oc, gates, g1, sc2, sh2, g, wa, wb, wc, wo, router, tm):
    bsz, t, d = x.shape
    mrows = g1.shape[1]
    mblk = 1 if mrows == 1 else tm
    mod_spec = pl.BlockSpec((None, mblk, d), (lambda b, i: (b, 0, 0)) if mrows == 1 else (lambda b, i: (b, i, 0)))

    def tok(n):
        return pl.BlockSpec((None, tm, n), lambda b, i: (b, i, 0))

    in_specs = [tok(d), tok(384), tok(384), tok(256), tok(3 * d), mod_spec, mod_spec, mod_spec,
                _const_spec((1, d)), _const_spec(wa.shape), _const_spec(wb.shape), _const_spec(wc.shape),
                _const_spec(wo.shape)]
    args = [x, oa, ob, oc, gates, g1, sc2, sh2, g.reshape(1, d), wa, wb, wc, wo]
    out_specs = [tok(d), tok(d)]
    out_shape = [jax.ShapeDtypeStruct((bsz, t, d), F32), jax.ShapeDtypeStruct((bsz, t, d), BF16)]
    if router is not None:
        rw, rb = router
        in_specs += [_const_spec(rw.shape), _const_spec(rb.shape)]
        args += [rw, rb]
        out_specs.append(tok(LANES))
        out_shape.append(jax.ShapeDtypeStruct((bsz, t, LANES), F32))
    return pl.pallas_call(
        functools.partial(_merge_kernel, moe=router is not None),
        grid=(bsz, t // tm),
        in_specs=in_specs, out_specs=out_specs, out_shape=out_shape,
        compiler_params=_cparams(("parallel", "parallel")),
        name="merge_out",
    )(*args)


def _final_norm(x, gain):
    ms = jnp.mean(x * x, axis=-1, keepdims=True)
    return x * lax.rsqrt(ms + NORM_EPS) * gain


def _ffn_kernel(x_ref, h_ref, g2_ref, wg_ref, wu_ref, wd_ref, *rest, n_chunks, final):
    if final:
        fg_ref, o_ref = rest
    else:
        (o_ref,) = rest
    h = h_ref[...]
    tf = wg_ref.shape[1] // n_chunks
    acc = jnp.zeros(x_ref.shape, F32)
    for c in range(n_chunks):
        gt = jnp.dot(h, wg_ref[:, c * tf:(c + 1) * tf], preferred_element_type=F32)
        up = jnp.dot(h, wu_ref[:, c * tf:(c + 1) * tf], preferred_element_type=F32)
        act = (gt * jax.nn.sigmoid(gt) * up).astype(BF16)
        acc = acc + jnp.dot(act, wd_ref[c * tf:(c + 1) * tf, :], preferred_element_type=F32)
    x = x_ref[...] + g2_ref[...] * acc
    o_ref[...] = _final_norm(x, fg_ref[...]) if final else x


def _ffn(x, h, g2, wg, wu, wd, final_g, tm):
    bsz, t, d = x.shape
    mrows = g2.shape[1]
    mblk = 1 if mrows == 1 else tm
    mod_spec = pl.BlockSpec((None, mblk, d), (lambda b, i: (b, 0, 0)) if mrows == 1 else (lambda b, i: (b, i, 0)))
    tok = pl.BlockSpec((None, tm, d), lambda b, i: (b, i, 0))
    in_specs = [tok, tok, mod_spec, _const_spec(wg.shape), _const_spec(wu.shape), _const_spec(wd.shape)]
    args = [x, h, g2, wg, wu, wd]
    if final_g is not None:
        in_specs.append(_const_spec((1, d)))
        args.append(final_g.reshape(1, d))
    return pl.pallas_call(
        functools.partial(_ffn_kernel, n_chunks=2, final=final_g is not None),
        grid=(bsz, t // tm),
        in_specs=in_specs, out_specs=tok,
        out_shape=jax.ShapeDtypeStruct((bsz, t, d), F32),
        compiler_params=_cparams(("parallel", "parallel")),
        name="ffn_dense",
    )(*args)


def _moe_kernel(x_ref, h_ref, g2_ref, gw_ref, wg_ref, wu_ref, wd_ref, *rest, final):
    if final:
        fg_ref, o_ref, acc_ref, posc_ref, posr_ref = rest
    else:
        o_ref, acc_ref, posc_ref, posr_ref = rest
    e = pl.program_id(2)
    tm = h_ref.shape[0]
    n_slabs = tm // MOE_CHUNK

    @pl.when(e == 0)
    def _():
        acc_ref[...] = jnp.zeros_like(acc_ref)
        routed = gw_ref[...] != 0.0
        r_f = jnp.where(routed, 1.0, 0.0)
        r_b = r_f.astype(BF16)
        r_t = r_f.T
        r_tb = r_t.astype(BF16)
        tok_l = _lane_iota((MOE_CHUNK, tm))
        tok_r = _row_iota((MOE_CHUNK, tm))
        rank_r = jnp.zeros((LANES, tm), F32)
        for s in range(n_slabs):
            rows = slice(s * MOE_CHUNK, (s + 1) * MOE_CHUNK)
            earlier = jnp.where(tok_l < tok_r + s * MOE_CHUNK, 1.0, 0.0).astype(BF16)
            rank_c = jnp.dot(earlier, r_b, preferred_element_type=F32)
            posc_ref[rows, :] = jnp.where(routed[rows], rank_c, -1.0)
            later = jnp.where(tok_r + s * MOE_CHUNK < tok_l, 1.0, 0.0).astype(BF16)
            rank_r = rank_r + jnp.dot(r_tb[:, rows], later, preferred_element_type=F32)
        posr_ref[...] = jnp.where(r_t != 0.0, rank_r, -1.0)

    lane_e = _lane_iota((tm, LANES)) == e
    pos_c = jnp.sum(jnp.where(lane_e, posc_ref[...], 0.0), axis=1, keepdims=True)
    gate_c = jnp.sum(jnp.where(lane_e, gw_ref[...], 0.0), axis=1, keepdims=True)
    pos_r = posr_ref[pl.ds(e, 1), :]
    n_routed = (jnp.max(pos_r) + 1.0).astype(I32)

    def run_chunk(base, n_rows):
        slot_rows = _row_iota((n_rows, tm)).astype(F32)
        slot_lanes = _lane_iota((MOE_CHUNK, n_rows)).astype(F32)
        pack = jnp.where(pos_r - base == slot_rows, 1.0, 0.0).astype(BF16)
        xc = jnp.dot(pack, h_ref[...], preferred_element_type=F32).astype(BF16)
        gt = jnp.dot(xc, wg_ref[...], preferred_element_type=F32)
        up = jnp.dot(xc, wu_ref[...], preferred_element_type=F32)
        act = (gt * jax.nn.sigmoid(gt) * up).astype(BF16)
        y = jnp.dot(act, wd_ref[...], preferred_element_type=F32).astype(BF16)
        for s in range(n_slabs):
            rows = slice(s * MOE_CHUNK, (s + 1) * MOE_CHUNK)
            unpack = jnp.where(pos_c[rows] - base == slot_lanes, 1.0, 0.0).astype(BF16)
            acc_ref[rows, :] += gate_c[rows] * jnp.dot(unpack, y, preferred_element_type=F32)

    def first(c, carry):
        run_chunk(0.0, MOE_CHUNK)
        return carry

    def later(c, carry):
        run_chunk((MOE_CHUNK + c * (MOE_CHUNK // 2)).astype(F32), MOE_CHUNK // 2)
        return carry

    lax.fori_loop(0, jnp.minimum(n_routed, 1), first, 0)
    n_later = (jnp.maximum(n_routed - MOE_CHUNK, 0) + MOE_CHUNK // 2 - 1) // (MOE_CHUNK // 2)
    lax.fori_loop(0, n_later, later, 0)

    @pl.when(e == pl.num_programs(2) - 1)
    def _():
        x = x_ref[...] + g2_ref[...] * acc_ref[...]
        o_ref[...] = _final_norm(x, fg_ref[...]) if final else x


def _moe(x, h, g2, gw, wg, wu, wd, final_g, tm):
    bsz, t, d = x.shape
    n_e, _, dff = wg.shape
    mrows = g2.shape[1]
    mblk = 1 if mrows == 1 else tm
    mod_spec = pl.BlockSpec((None, mblk, d), (lambda b, i, e: (b, 0, 0)) if mrows == 1 else (lambda b, i, e: (b, i, 0)))
    tok = pl.BlockSpec((None, tm, d), lambda b, i, e: (b, i, 0))
    tok_once = pl.BlockSpec((None, tm, d), lambda b, i, e: (b, i, 0), pipeline_mode=pl.Buffered(1))
    in_specs = [tok_once, tok, mod_spec, pl.BlockSpec((None, tm, LANES), lambda b, i, e: (b, i, 0)),
                pl.BlockSpec((None, d, dff), lambda b, i, e: (e, 0, 0)),
                pl.BlockSpec((None, d, dff), lambda b, i, e: (e, 0, 0)),
                pl.BlockSpec((None, dff, d), lambda b, i, e: (e, 0, 0))]
    args = [x, h, g2, gw, wg, wu, wd]
    if final_g is not None:
        in_specs.append(pl.BlockSpec((1, d), lambda b, i, e: (0, 0)))
        args.append(final_g.reshape(1, d))
    return pl.pallas_call(
        functools.partial(_moe_kernel, final=final_g is not None),
        grid=(bsz, t // tm, n_e),
        in_specs=in_specs, out_specs=tok,
        out_shape=jax.ShapeDtypeStruct((bsz, t, d), F32),
        scratch_shapes=[pltpu.VMEM((tm, d), F32), pltpu.VMEM((tm, LANES), F32), pltpu.VMEM((LANES, tm), F32)],
        compiler_params=_cparams(("parallel", "parallel", "arbitrary")),
        name="moe_routed",
    )(*args)


def _rope_tables(pos):
    inv = ROPE_THETA ** (-jnp.arange(HALF, dtype=F32) / HALF)
    ang = pos.astype(F32)[:, None] * inv[None, :]
    cos, sin = jnp.cos(ang), jnp.sin(ang)
    return jnp.tile(cos, (1, 4)), jnp.tile(jnp.concatenate([-sin, sin], axis=1), (1, 2))


def _pick_tile(n, pref):
    t = min(n, pref)
    while n % t:
        t //= 2
    return t


def _per_seq_cols(a, bsz, t, width):
    f = a.shape[1]
    a = jnp.moveaxis(a[0].reshape(f, bsz, t), 1, 0)
    return jnp.pad(a, ((0, 0), (0, 0), (0, width - t)))


def _value_blocks(past_vt, new_vt, bsz, t, lpad):
    lead = new_vt.shape[:-2]
    new_b = jnp.moveaxis(new_vt.reshape(*lead, V_ROWS, bsz, t), -2, 0)
    full = jnp.concatenate([past_vt, new_b], axis=-1)
    full = jnp.pad(full, [(0, 0)] * (full.ndim - 1) + [(0, lpad - full.shape[-1])])
    full = full.reshape(bsz, *lead, V_ROWS, lpad // KV_BLOCK, KV_BLOCK)
    return jnp.moveaxis(full, -2, -3)


def _with_ones_rows(vt):
    ones = jnp.ones(vt.shape[:-2] + (1, vt.shape[-1]), vt.dtype)
    zeros = jnp.zeros(vt.shape[:-2] + (V_ROWS - HEAD_DIM - 1, vt.shape[-1]), vt.dtype)
    return jnp.concatenate([vt, ones, zeros], axis=-2)


def _mixers(inp, past, n_past, lw, bsz, t):
    aqt, iqt, kik, avt, bqt, bk, bvt, cu, sm, smt = inp
    n_keys = n_past + t
    n_sel = min(TOPK_MAX, n_keys // 4)
    lpad = -(-n_keys // (2 * KV_BLOCK)) * 2 * KV_BLOCK
    if past is None:
        kik_all, avt_all, bk_all, bvt_all, logf_all = kik, avt, bk, bvt, sm
        hist16 = jnp.zeros((bsz, 16, POOL_WIDTH), F32)
        tq_a, tq_b, t_pad = _pick_tile(t, 256), _pick_tile(t, 256), t
    else:
        pa, pb, plf, pc = past
        pk, pv, pik = (pa[:, :, j].astype(BF16) for j in range(3))

        def join_rows(p, new):
            full = jnp.concatenate([p, new.reshape(bsz, t, new.shape[-1])], axis=1)
            return jnp.pad(full, ((0, 0), (0, lpad - n_keys), (0, 0)))

        kik_all = join_rows(jnp.concatenate([pk, pik], axis=-1), kik)
        avt_all = _value_blocks(_with_ones_rows(jnp.swapaxes(pv, 1, 2)), avt[0, 0], bsz, t, lpad)
        bk_all = join_rows(pb[:, :, 0].astype(BF16).reshape(bsz, n_past, 384), bk)
        pvt = jnp.transpose(pb[:, :, 1].astype(BF16), (0, 2, 3, 1))
        bvt_all = _value_blocks(_with_ones_rows(pvt), bvt[0, :, 0], bsz, t, lpad)
        logf_all = join_rows(jnp.pad(plf, ((0, 0), (0, 0), (0, LANES - B_HEADS))), sm)
        hist16 = jnp.pad(pc, ((0, 0), (1, 0), (0, 0)))
        tq_a = tq_b = t_pad = LANES
        aqt, iqt, bqt, smt = (_per_seq_cols(a, bsz, t, t_pad) for a in (aqt, iqt, bqt, smt))

    oa = _dsa(aqt, iqt, smt, kik_all, avt_all, tq=tq_a, n_keys=n_keys, q_pos0=n_past, n_sel=n_sel)
    ck = _cum_logf(logf_all, 2 * KV_BLOCK)
    ob = _fox(bqt, bk_all, bvt_all, ck, tq=tq_b, n_keys=n_keys, q_pos0=n_past)
    cu = cu.reshape(bsz, t, POOL_WIDTH)
    oc = _pool(cu, hist16, lw["pool_bd"], lw["pool_scale"], tc=_pick_tile(t, 1024), start_pos=n_past)
    return oa[:, :t], ob[:, :t], oc


def _layer(x, mod, past, n_past, pos_tab, lw, layer, final_g, per_token):
    bsz, t, d = x.shape
    sh1, sc1, g1, sh2, sc2, g2 = mod
    if per_token:
        xt = x.reshape(1, bsz * t, d)
        sh1, sc1, g1, sh2, sc2, g2 = (jnp.broadcast_to(m, (bsz, t, d)).reshape(1, bsz * t, d) for m in mod)
        cos, sin = (jnp.tile(a, (bsz, 1)) for a in pos_tab)
    else:
        xt = x
        cos, sin = pos_tab
    tm = KV_BLOCK
    (aqt, iqt, nat, kik, avt, nbt, bqt, bk, bvt, cu, sm, smt, gates) = _in_proj(
        xt, sc1, sh1, lw["norm_mix_g"], lw["w_in"], lw["bf_bias"], cos, sin, tm)
    oa, ob, oc = _mixers((aqt, iqt, kik, avt, bqt, bk, bvt, cu, sm, smt), past, n_past, lw, bsz, t)

    def flat(a):
        return a.reshape(xt.shape[0], xt.shape[1], a.shape[-1])

    router = (lw["router_w"], lw["router_b"]) if layer % 2 else None
    res = _merge(xt, flat(oa), flat(ob), flat(oc), gates, g1, sc2, sh2, lw["norm_ffn_g"],
                 lw["w_br_a"], lw["w_br_b"], lw["w_br_c"], lw["w_out"], router, _pick_tile(xt.shape[1], 512))
    tmf = _pick_tile(xt.shape[1], 512)
    if layer % 2 == 0:
        x_mid, h2 = res
        x_new = _ffn(x_mid, h2, g2, lw["ffn_wg"], lw["ffn_wu"], lw["ffn_wd"], final_g, tmf)
    else:
        x_mid, h2, gw = res
        x_new = _moe(x_mid, h2, g2, gw, lw["moe_wg"], lw["moe_wu"], lw["moe_wd"], final_g,
                     _pick_tile(xt.shape[1], 1024))
    def token_major(a, *feat):
        a = a.reshape(a.shape[0], *feat, -1, t) if per_token else a.reshape(a.shape[0], *feat, 1, t)
        a = jnp.moveaxis(a, (-2, -1), (1, 2))
        return a.reshape(bsz, t, *feat)

    new_a = token_major(nat, 3, HEAD_DIM)
    new_b = token_major(nbt, 2, B_HEADS, HEAD_DIM)
    new_logf = sm.reshape(bsz, t, LANES)[:, :, :B_HEADS]
    new_pool = cu.reshape(bsz, t, POOL_WIDTH)[:, t - POOL_HIST:, :]
    return x_new.reshape(bsz, t, d), (new_a, new_b, new_logf, new_pool)


def kernel(x_prompt, x_sample, cache_a_kvi, cache_b_kv, cache_b_logf, state_c_pool, c_prompt, c_sample,
           ada_w, ada_b, norm_mix_g, w_in, b_forget, pool_w, pool_scale, w_br_a, w_br_b, w_br_c, w_out,
           norm_ffn_g, ffn_w_gate, ffn_w_up, ffn_w_down, moe_router_w, moe_router_b, moe_w_gate,
           moe_w_up, moe_w_down, final_norm_g):
    depth = ada_w.shape[0]
    bp, tp, d = x_prompt.shape
    bs, ts, _ = x_sample.shape
    n_past = cache_a_kvi.shape[2]
    assert tp % KV_BLOCK == 0 and (bs * ts) % KV_BLOCK == 0 and ts <= LANES

    rows = -(-(bp + bs) // 8) * 8
    c_all = jnp.pad(jnp.concatenate([c_prompt, c_sample], axis=0), ((0, rows - bp - bs), (0, 0)))
    mod_all = _ada(c_all, ada_w, ada_b)

    tab_p = _rope_tables(jnp.arange(tp))
    tab_s = _rope_tables(n_past + jnp.arange(ts))

    xp, xs = x_prompt, x_sample
    outs_p, outs_s = [], []
    for layer in range(depth):
        j = layer // 2
        w_l, bias_l = _in_weights(w_in[layer], b_forget[layer])
        pw = pool_w[layer]
        pool_bd = jnp.zeros((POOL_WIDTH, POOL_WIDTH), F32)
        for g in range(len(POOL_WINDOWS)):
            sl = slice(g * POOL_GROUP_DIM, (g + 1) * POOL_GROUP_DIM)
            pool_bd = pool_bd.at[sl, sl].set(pw[g])
        lw = dict(w_in=w_l, bf_bias=bias_l, norm_mix_g=norm_mix_g[layer], norm_ffn_g=norm_ffn_g[layer],
                  pool_bd=pool_bd.astype(BF16), pool_scale=pool_scale[layer],
                  w_br_a=w_br_a[layer].astype(BF16), w_br_b=w_br_b[layer].astype(BF16),
                  w_br_c=w_br_c[layer].astype(BF16), w_out=w_out[layer].astype(BF16))
        if layer % 2 == 0:
            lw.update(ffn_wg=ffn_w_gate[j].astype(BF16), ffn_wu=ffn_w_up[j].astype(BF16),
                      ffn_wd=ffn_w_down[j].astype(BF16))
        else:
            rw = jnp.pad(moe_router_w[j], ((0, 0), (0, LANES - N_EXPERTS)))
            rw_hi = rw.astype(BF16)
            lw.update(router_w=jnp.stack([rw_hi, (rw - rw_hi.astype(F32)).astype(BF16)]),
                      router_b=jnp.pad(moe_router_b[j], (0, LANES - N_EXPERTS)).reshape(1, LANES),
                      moe_wg=moe_w_gate[j].astype(BF16), moe_wu=moe_w_up[j].astype(BF16),
                      moe_wd=moe_w_down[j].astype(BF16))
        final_g = final_norm_g if layer == depth - 1 else None
        mod_p = [m[:, None, :] for m in jnp.split(mod_all[layer, :bp], 6, axis=-1)]
        mod_s = [m[:, None, :] for m in jnp.split(mod_all[layer, bp:bp + bs], 6, axis=-1)]
        xp, new_p = _layer(xp, mod_p, None, 0, tab_p, lw, layer, final_g, per_token=False)
        past = (cache_a_kvi[layer], cache_b_kv[layer], cache_b_logf[layer], state_c_pool[layer])
        xs, new_s = _layer(xs, mod_s, past, n_past, tab_s, lw, layer, final_g, per_token=True)
        outs_p.append(new_p)
        outs_s.append(new_s)

    def stack(outs, k):
        return jnp.stack([o[k] for o in outs])

    return (xp, xs,
            stack(outs_p, 0), stack(outs_p, 1), stack(outs_p, 2), stack(outs_p, 3),
            stack(outs_s, 0), stack(outs_s, 1), stack(outs_s, 2), stack(outs_s, 3))
```

```python
import functools

import jax
import jax.numpy as jnp
import numpy as np
from jax import lax
from jax.experimental import pallas as pl
from jax.experimental.pallas import tpu as pltpu

F32 = jnp.float32
BF16 = jnp.bfloat16
I32 = jnp.int32

D_MODEL = 1024
CHUNK = 64
HEAD_DIM = 64
HALF = HEAD_DIM // 2
ROPE_THETA = 10000.0
NORM_EPS = 1e-6
A_HEADS = 6
IDX_HEADS = 4
TOPK_MAX = 256
B_HEADS = 6
POOL_WINDOWS = (2, 4, 8, 16)
POOL_GROUP_DIM = 64
POOL_WIDTH = 256
POOL_HIST = 15
N_EXPERTS = 8
LANES = 128
SUBLANES = 8
LOG2E = 1.4426950408889634
QK_SCALE = HEAD_DIM ** -0.5 * LOG2E
KV_BLOCK = 256
V_ROWS = HEAD_DIM + 16
MOE_CHUNK = 256
VMEM_LIMIT = 56 * 1024 * 1024
NEG_INF = float("-inf")
KEY_DT = jnp.bfloat16
RANK_ZERO = 0x8000
RANK_NEG_INF = 0x007F
RANK_POS_INF = 0xFF80
IW_LANE = 8

C_AQ, C_IQ, C_A, C_B, C_CU, C_SM, C_GATE, C_END = 0, 384, 640, 896, 2048, 2304, 2432, 5504


def _cparams(sem):
    return pltpu.CompilerParams(dimension_semantics=sem, vmem_limit_bytes=VMEM_LIMIT)


def _const_spec(shape):
    nd = len(shape)
    return pl.BlockSpec(shape, lambda *_: (0,) * nd, pipeline_mode=pl.Buffered(1))


def _lane_iota(shape):
    return lax.broadcasted_iota(I32, shape, len(shape) - 1)


def _row_iota(shape):
    return lax.broadcasted_iota(I32, shape, len(shape) - 2)


def _ada_kernel(c_ref, w_ref, b_ref, o_ref):
    c = c_ref[...]
    s = c * jax.nn.sigmoid(c)
    o_ref[...] = jnp.dot(s, w_ref[...], preferred_element_type=F32,
                         precision=lax.Precision.HIGHEST) + b_ref[...]


def _ada(c_all, ada_w, ada_b):
    depth, d, n = ada_w.shape
    rows = c_all.shape[0]
    tn = 1536
    return pl.pallas_call(
        _ada_kernel,
        grid=(depth, n // tn),
        in_specs=[pl.BlockSpec((rows, d), lambda l, j: (0, 0)),
                  pl.BlockSpec((None, d, tn), lambda l, j: (l, 0, j)),
                  pl.BlockSpec((None, 1, tn), lambda l, j: (l, 0, j))],
        out_specs=pl.BlockSpec((None, rows, tn), lambda l, j: (l, 0, j)),
        out_shape=jax.ShapeDtypeStruct((depth, rows, n), F32),
        compiler_params=_cparams(("arbitrary", "arbitrary")),
        name="ada_mod",
    )(c_all, ada_w, ada_b.reshape(depth, 1, n))


def _in_kernel(x_ref, sc_ref, sh_ref, g_ref, w_ref, bf_ref, cos_ref, sin_ref,
               aqt_ref, iqt_ref, nat_ref, kik_ref, avt_ref, nbt_ref, bqt_ref, bk_ref, bvt_ref,
               cu_ref, sm_ref, smt_ref, gate_ref):
    x = x_ref[...]
    ms = jnp.mean(x * x, axis=-1, keepdims=True)
    y = x * lax.rsqrt(ms + NORM_EPS) * g_ref[...]
    h = (y * (1.0 + sc_ref[...]) + sh_ref[...]).astype(BF16)
    tm = x.shape[0]

    def mm(a, b):
        return jnp.dot(h, w_ref[:, a:b], preferred_element_type=F32)

    cos = cos_ref[...]
    sin = sin_ref[...]
    lane = _lane_iota((tm, LANES))
    low = lane < HEAD_DIM
    first_half = (lane & HALF) == 0

    def rope(z):
        swapped = jnp.where(first_half, pltpu.roll(z, LANES - HALF, 1), pltpu.roll(z, HALF, 1))
        return z * cos + swapped * sin

    zeros64 = jnp.zeros((HEAD_DIM, tm), BF16)
    ones_rows = jnp.where(_row_iota((V_ROWS - HEAD_DIM, tm)) == 0, 1.0, 0.0).astype(BF16)

    def put_heads(ref, zt, p, slot_even, slot_odd):
        for hh, slot in ((0, slot_even), (1, slot_odd)):
            base = (2 * p + hh) * LANES
            ref[base + slot * HEAD_DIM:base + (slot + 1) * HEAD_DIM, :] = zt[hh * HEAD_DIM:(hh + 1) * HEAD_DIM]
            ref[base + (1 - slot) * HEAD_DIM:base + (2 - slot) * HEAD_DIM, :] = zeros64

    z = mm(C_AQ, C_IQ)
    for p in range(3):
        zt = (rope(z[:, p * LANES:(p + 1) * LANES]) * QK_SCALE).T.astype(BF16)
        put_heads(aqt_ref, zt, p, 0, 0)
    z = mm(C_IQ, C_A)
    for p in range(2):
        zt = rope(z[:, p * LANES:(p + 1) * LANES]).T.astype(BF16)
        put_heads(iqt_ref, zt, p, 1, 1)

    z = mm(C_A, C_B)
    kv = z[:, :LANES]
    r0 = jnp.where(low, rope(kv), kv)
    r1 = rope(z[:, LANES:])
    r0t = r0.T
    nat_ref[0:LANES, :] = r0t
    nat_ref[LANES:, :] = r1.T[0:HEAD_DIM, :]
    kik_ref[...] = jnp.where(low, r0, pltpu.roll(r1, HEAD_DIM, 1)).astype(BF16)
    avt_ref[0:HEAD_DIM, :] = r0t[HEAD_DIM:, :].astype(BF16)
    avt_ref[HEAD_DIM:, :] = ones_rows

    z = mm(C_B, C_CU)
    bk_ref[...] = z[:, 384:768].astype(BF16)
    for p in range(3):
        zt = (z[:, p * LANES:(p + 1) * LANES] * QK_SCALE).T.astype(BF16)
        put_heads(bqt_ref, zt, p, 0, 1)
        nbt_ref[p * LANES:(p + 1) * LANES, :] = z[:, 384 + p * LANES:384 + (p + 1) * LANES].T
        vt = z[:, 768 + p * LANES:768 + (p + 1) * LANES].T
        nbt_ref[384 + p * LANES:384 + (p + 1) * LANES, :] = vt
        vt = vt.astype(BF16)
        for hh in range(2):
            bvt_ref[2 * p + hh, 0:HEAD_DIM, :] = vt[hh * HEAD_DIM:(hh + 1) * HEAD_DIM]
            bvt_ref[2 * p + hh, HEAD_DIM:, :] = ones_rows

    cu_ref[...] = mm(C_CU, C_SM)

    z = mm(C_SM, C_GATE)
    t = z + bf_ref[...]
    logf = jnp.minimum(t, 0.0) - jnp.log1p(jnp.exp(-jnp.abs(t)))
    sm = jnp.where(lane < B_HEADS, logf, z)
    sm_ref[...] = sm
    smt_ref[...] = sm.T[0:16, :]

    for c in range(3):
        gate_ref[:, c * D_MODEL:(c + 1) * D_MODEL] = jax.nn.sigmoid(
            mm(C_GATE + c * D_MODEL, C_GATE + (c + 1) * D_MODEL)).astype(BF16)


def _in_weights(w_in_l, b_forget_l):
    d = w_in_l.shape[0]
    sizes = (384, 64, 64, 256, 4, 64, 384, 384, 384, 6, 256, 3072)
    o = np.concatenate([[0], np.cumsum(sizes)])
    w_in_l = w_in_l.astype(BF16)
    cols = [w_in_l[:, o[0]:o[1]],
            w_in_l[:, o[3]:o[4]],
            w_in_l[:, o[1]:o[3]], w_in_l[:, o[5]:o[6]], jnp.zeros((d, 64), BF16),
            w_in_l[:, o[6]:o[9]],
            w_in_l[:, o[10]:o[11]],
            w_in_l[:, o[9]:o[10]], jnp.zeros((d, IW_LANE - B_HEADS), BF16), w_in_l[:, o[4]:o[5]],
            jnp.zeros((d, LANES - IW_LANE - IDX_HEADS), BF16),
            w_in_l[:, o[11]:o[12]]]
    w = jnp.concatenate(cols, axis=1)
    assert w.shape[1] == C_END, w.shape
    bias = jnp.concatenate([b_forget_l, jnp.zeros((LANES - B_HEADS,), F32)]).reshape(1, LANES)
    return w, bias


def _in_proj(x, sc, sh, g, w, bias, cos, sin, tm):
    bsz, t, d = x.shape
    mrows = sc.shape[1]
    mblk = 1 if mrows == 1 else tm
    mod_spec = pl.BlockSpec((None, mblk, d), (lambda b, i: (b, 0, 0)) if mrows == 1 else (lambda b, i: (b, i, 0)))
    nblk = t // tm

    def rows(n, dt):
        return pl.BlockSpec((None, tm, n), lambda b, i: (b, i, 0)), jax.ShapeDtypeStruct((bsz, t, n), dt)

    def cols(n, dt):
        return pl.BlockSpec((None, n, tm), lambda b, i: (b, 0, i)), jax.ShapeDtypeStruct((bsz, n, t), dt)

    outs = [cols(A_HEADS * LANES, BF16), cols(IDX_HEADS * LANES, BF16), cols(192, F32), rows(LANES, BF16),
            (pl.BlockSpec((None, None, V_ROWS, tm), lambda b, i: (b, i, 0, 0)),
             jax.ShapeDtypeStruct((bsz, nblk, V_ROWS, tm), BF16)),
            cols(768, F32), cols(B_HEADS * LANES, BF16), rows(384, BF16),
            (pl.BlockSpec((None, B_HEADS, None, V_ROWS, tm), lambda b, i: (b, 0, i, 0, 0)),
             jax.ShapeDtypeStruct((bsz, B_HEADS, nblk, V_ROWS, tm), BF16)),
            rows(256, F32), rows(LANES, F32), cols(16, F32), rows(3 * D_MODEL, BF16)]
    return pl.pallas_call(
        _in_kernel,
        grid=(bsz, nblk),
        in_specs=[pl.BlockSpec((None, tm, d), lambda b, i: (b, i, 0)), mod_spec, mod_spec,
                  _const_spec((1, d)), _const_spec(w.shape), _const_spec((1, LANES)),
                  pl.BlockSpec((tm, LANES), lambda b, i: (i, 0)),
                  pl.BlockSpec((tm, LANES), lambda b, i: (i, 0))],
        out_specs=[o[0] for o in outs],
        out_shape=[o[1] for o in outs],
        compiler_params=_cparams(("parallel", "parallel")),
        name="in_proj",
    )(x, sc, sh, g.reshape(1, d), w, bias, cos, sin)


def _cum_kernel(x_ref, o_ref, carry_ref):
    @pl.when(pl.program_id(1) == 0)
    def _():
        carry_ref[...] = jnp.zeros_like(carry_ref)

    tc = x_ref.shape[0]
    tri = jnp.where(_lane_iota((tc, tc)) <= _row_iota((tc, tc)), 1.0, 0.0).astype(BF16)
    cum = carry_ref[0:1, :]
    rest = x_ref[...]
    for _ in range(3):
        piece = rest.astype(BF16)
        cum = cum + jnp.dot(tri, piece, preferred_element_type=F32)
        rest = rest - piece.astype(F32)
    carry_ref[...] = jnp.broadcast_to(cum[tc - 1:tc, :], carry_ref.shape)
    pieces = []
    rest = cum * LOG2E
    for _ in range(3):
        piece = rest.astype(BF16)
        pieces.append(piece.astype(F32))
        rest = rest - pieces[-1]
    lane = _lane_iota((tc, LANES))
    for p in range(B_HEADS // 2):
        slab = jnp.zeros((tc, LANES), F32)
        for hh in range(2):
            for j, piece in enumerate(pieces):
                dst, src = 3 * hh + j, 2 * p + hh
                slab = jnp.where(lane == dst, pltpu.roll(piece, (dst - src) % LANES, 1), slab)
        o_ref[p] = slab.astype(BF16)


def _cum_logf(x, tc):
    bsz, t, n = x.shape
    return pl.pallas_call(
        _cum_kernel,
        grid=(bsz, t // tc),
        in_specs=[pl.BlockSpec((None, tc, n), lambda b, i: (b, i, 0))],
        out_specs=pl.BlockSpec((None, B_HEADS // 2, tc, LANES), lambda b, i: (b, 0, i, 0)),
        out_shape=jax.ShapeDtypeStruct((bsz, B_HEADS // 2, t, LANES), BF16),
        scratch_shapes=[pltpu.VMEM((SUBLANES, LANES), F32)],
        compiler_params=_cparams(("parallel", "arbitrary")),
        name="logf_cumsum",
    )(x)


def _sum_keys(x):
    part = x.reshape(x.shape[0] // SUBLANES, SUBLANES, x.shape[1]).sum(axis=0)
    return jnp.sum(part, axis=0, keepdims=True)


def _max_keys(x):
    part = x.reshape(x.shape[0] // SUBLANES, SUBLANES, x.shape[1]).max(axis=0)
    return jnp.max(part, axis=0, keepdims=True)


def _dsa_kernel(aqt_ref, iqt_ref, smt_ref, kik_ref, avt_ref, o_ref, key_ref,
                *, tq, tk, n_keys, q_pos0, n_sel):
    i = pl.program_id(1)
    pos_first = q_pos0 + i * tq
    last_chunk = (pos_first + tq - 1) // CHUNK
    n_adm = jnp.minimum((last_chunk + 1) * CHUNK, n_keys)
    nkb = (n_adm + tk - 1) // tk

    q_pos = pos_first + _lane_iota((1, tq))
    q_lim = jnp.minimum((q_pos // CHUNK + 1) * CHUNK, n_keys)
    key_row = _row_iota((tk, tq))

    def keys(kb):
        return kik_ref[pl.ds(pl.multiple_of(kb * tk, tk), tk), :]

    iq4 = jnp.concatenate([iqt_ref[hd * LANES:(hd + 1) * LANES, :] for hd in range(IDX_HEADS)], axis=1)
    smt = smt_ref[...]
    w_rows = [smt[IW_LANE + hd:IW_LANE + hd + 1, :] for hd in range(IDX_HEADS)]

    def score_body(kb, carry):
        s4 = jnp.dot(keys(kb), iq4, preferred_element_type=F32)
        score = w_rows[0] * jnp.maximum(s4[:, 0:tq], 0.0)
        for hd in range(1, IDX_HEADS):
            score = score + w_rows[hd] * jnp.maximum(s4[:, hd * tq:(hd + 1) * tq], 0.0)
        score = jnp.where(key_row < q_lim - kb * tk, score, NEG_INF)
        key_ref[kb] = score.astype(KEY_DT)
        return carry

    lax.fori_loop(0, nkb, score_body, 0)

    one, zero = jnp.ones((), KEY_DT), jnp.zeros((), KEY_DT)
    packed_rows = 2 * SUBLANES

    def count(cand, strict):
        def hits(blk):
            h = jnp.where((blk > cand) if strict else (blk >= cand), one, zero)
            parts = [h[r * packed_rows:(r + 1) * packed_rows] for r in range(tk // packed_rows)]
            while len(parts) > 1:
                parts = [a + b for a, b in zip(parts[::2], parts[1::2])]
            return parts[0].astype(F32)

        def pair(j, acc):
            return acc + hits(key_ref[2 * j]) + hits(key_ref[2 * j + 1])

        acc = lax.fori_loop(0, nkb // 2, pair, jnp.zeros((packed_rows, tq), F32))
        acc = lax.fori_loop(2 * (nkb // 2), nkb, lambda kb, a: a + hits(key_ref[kb]), acc)
        return jnp.sum(acc, axis=0, keepdims=True)

    def pattern_value(u):
        bits = jnp.where(u >= RANK_ZERO, u - RANK_ZERO, (~u) & 0xFFFF)
        return lax.bitcast_convert_type(lax.shift_left(bits, 16), F32).astype(KEY_DT)

    def bit_body(b, u):
        cand_u = u | lax.shift_left(jnp.int32(1), 15 - b)
        cnt = count(pattern_value(cand_u), False)
        return jnp.where(cnt >= n_sel, cand_u, u)

    u_thr = jnp.maximum(lax.fori_loop(0, 16, bit_body, jnp.zeros((1, tq), I32)), RANK_NEG_INF)
    thr_16 = pattern_value(u_thr)

    def finer():
        lo = thr_16.astype(F32)
        hi = jnp.where(u_thr >= RANK_POS_INF, jnp.inf, pattern_value(u_thr + 1).astype(F32))

        def halve(_, lo_hi):
            lo, hi = lo_hi
            cand = (0.5 * lo + 0.5 * hi).astype(KEY_DT)
            enough = count(cand, False) >= n_sel
            return jnp.where(enough, cand.astype(F32), lo), jnp.where(enough, hi, cand.astype(F32))

        thr_f = lax.fori_loop(0, 16, halve, (lo, hi))[0].astype(KEY_DT)
        return thr_f, count(thr_f, True)

    above_16 = count(thr_16, True)
    thr_k, above = lax.cond(jnp.min(n_sel - above_16) <= 0.0, finer, lambda: (thr_16, above_16))
    need = jnp.maximum(n_sel - above, 0.0)
    thr = thr_k.astype(F32)

    aq6 = jnp.concatenate([aqt_ref[hd * LANES:(hd + 1) * LANES, :] for hd in range(A_HEADS)], axis=1)

    half = tk // 2
    lower = jnp.where(_lane_iota((half, half)) <= _row_iota((half, half)), 1.0, 0.0).astype(BF16)

    def attend():
        def body(kb, carry):
            eq_seen, ms, accs = carry
            blk = key_ref[kb].astype(F32)
            eq = blk == thr
            eq_f = jnp.where(eq, 1.0, 0.0)
            prefs = []
            for e in (eq_f[:half], eq_f[half:]):
                prefs.append(jnp.dot(lower, e.astype(BF16), preferred_element_type=F32) + eq_seen)
                eq_seen = eq_seen + _sum_keys(e)
            slack = jnp.where(blk >= thr, need - jnp.where(eq, jnp.concatenate(prefs, axis=0), 0.0), -1.0)
            bias = jnp.where(slack >= 0.0, jnp.where(jnp.abs(blk) < jnp.inf, 0.0, NEG_INF), NEG_INF)
            logits = jnp.dot(keys(kb), aq6, preferred_element_type=F32)
            vts = (avt_ref[2 * kb], avt_ref[2 * kb + 1])
            new_ms, new_accs = [], []
            for p in range(A_HEADS // 2):
                ps, alphas = [], []
                for hd in (2 * p, 2 * p + 1):
                    lg = logits[:, hd * tq:(hd + 1) * tq] + bias
                    m_old = ms[hd]
                    m_new = jnp.maximum(m_old, _max_keys(lg))
                    m_safe = jnp.where(m_new == NEG_INF, 0.0, m_new)
                    ps.append(jnp.exp2(lg - m_safe).astype(BF16))
                    alphas.append(jnp.exp2(m_old - m_safe))
                    new_ms.append(m_new)
                p2 = jnp.concatenate(ps, axis=1)
                pv = (jnp.dot(vts[0], p2[:tk // 2], preferred_element_type=F32)
                      + jnp.dot(vts[1], p2[tk // 2:], preferred_element_type=F32))
                new_accs.append(jnp.concatenate(alphas, axis=1) * accs[p] + pv)
            return eq_seen, tuple(new_ms), tuple(new_accs)

        init = (jnp.zeros((1, tq), F32),
                tuple(jnp.full((1, tq), NEG_INF, F32) for _ in range(A_HEADS)),
                tuple(jnp.zeros((V_ROWS, 2 * tq), F32) for _ in range(A_HEADS // 2)))
        return lax.fori_loop(0, nkb, body, init)[2]

    accs = attend()

    outs = []
    for p in range(A_HEADS // 2):
        o2 = accs[p][0:HEAD_DIM] / accs[p][HEAD_DIM:HEAD_DIM + 1]
        outs += [o2[:, 0:tq], o2[:, tq:2 * tq]]
    o_ref[...] = jnp.concatenate(outs, axis=0).T.astype(o_ref.dtype)


def _dsa(aqt, iqt, smt, kik, avt, *, tq, n_keys, q_pos0, n_sel):
    bsz, _, t_q = aqt.shape
    _, nblk, _, tkv = avt.shape
    assert nblk % 2 == 0
    tk = 2 * tkv
    kern = functools.partial(_dsa_kernel, tq=tq, tk=tk, n_keys=n_keys, q_pos0=q_pos0, n_sel=n_sel)
    return pl.pallas_call(
        kern,
        grid=(bsz, t_q // tq),
        in_specs=[pl.BlockSpec((None, A_HEADS * LANES, tq), lambda b, i: (b, 0, i)),
                  pl.BlockSpec((None, IDX_HEADS * LANES, tq), lambda b, i: (b, 0, i)),
                  pl.BlockSpec((None, 16, tq), lambda b, i: (b, 0, i)),
                  pl.BlockSpec((None, nblk * tkv, LANES), lambda b, i: (b, 0, 0)),
                  pl.BlockSpec((None, nblk, V_ROWS, tkv), lambda b, i: (b, 0, 0, 0))],
        out_specs=pl.BlockSpec((None, tq, 384), lambda b, i: (b, i, 0)),
        out_shape=jax.ShapeDtypeStruct((bsz, t_q, 384), BF16),
        scratch_shapes=[pltpu.VMEM((nblk // 2, tk, tq), KEY_DT)],
        compiler_params=_cparams(("parallel", "arbitrary")),
        name="dsa_attention",
    )(aqt, iqt, smt, kik, avt)


def _fox_kernel(qt_ref, k_ref, vt_ref, ck_ref, o_ref, *, tq, tk, n_keys, q_pos0):
    i = pl.program_id(1)
    pos_first = q_pos0 + i * tq
    n_full = pos_first // tk
    nkb = (jnp.minimum(pos_first + tq, n_keys) + tk - 1) // tk
    q_pos = pos_first + _lane_iota((1, tq))
    key_row = _row_iota((tk, tq))
    piece_row = _row_iota((LANES, tq))
    qts = []
    for hd in range(B_HEADS):
        minus = jnp.where((piece_row >= 3 * (hd % 2)) & (piece_row < 3 * (hd % 2) + 3), -1.0, 0.0).astype(BF16)
        qts.append(jnp.concatenate([qt_ref[hd * LANES:(hd + 1) * LANES, :], minus], axis=0))

    def step(kbs, state, masked):
        logits = []
        for hd in range(B_HEADS):
            for kb in kbs:
                rows = pl.ds(pl.multiple_of(kb * tk, tk), tk)
                kblk = jnp.concatenate([k_ref[rows, (hd // 2) * LANES:(hd // 2 + 1) * LANES],
                                        ck_ref[hd // 2, rows, :]], axis=1)
                lg = jnp.dot(kblk, qts[hd], preferred_element_type=F32)
                if masked:
                    lg = jnp.where(key_row <= q_pos - kb * tk, lg, NEG_INF)
                logits.append(lg)
        new = []
        for hd in range(B_HEADS):
            m_old, acc = state[hd]
            lgs = logits[hd * len(kbs):(hd + 1) * len(kbs)]
            m_new = m_old
            for lg in lgs:
                m_new = jnp.maximum(m_new, _max_keys(lg))
            m_safe = jnp.where(m_new == NEG_INF, 0.0, m_new) if masked else m_new
            acc = jnp.exp2(m_old - m_safe) * acc
            for kb, lg in zip(kbs, lgs):
                p = jnp.exp2(lg - m_safe).astype(BF16)
                acc = acc + jnp.dot(vt_ref[hd, kb], p, preferred_element_type=F32)
            new.append((m_new, acc))
        return tuple(new)

    init = tuple((jnp.full((1, tq), NEG_INF, F32), jnp.zeros((V_ROWS, tq), F32)) for _ in range(B_HEADS))
    state = lax.fori_loop(0, n_full // 2, lambda j, st: step((2 * j, 2 * j + 1), st, False), init)
    state = lax.fori_loop(2 * (n_full // 2), nkb, lambda kb, st: step((kb,), st, True), state)
    outs = [acc[0:HEAD_DIM] / acc[HEAD_DIM:HEAD_DIM + 1] for _, acc in state]
    o_ref[...] = jnp.concatenate(outs, axis=0).T.astype(o_ref.dtype)


def _fox(qt, k, vt, ck, *, tq, n_keys, q_pos0):
    bsz, _, t_q = qt.shape
    _, _, nblk, _, tk = vt.shape
    lpad = nblk * tk
    kern = functools.partial(_fox_kernel, tq=tq, tk=tk, n_keys=n_keys, q_pos0=q_pos0)
    return pl.pallas_call(
        kern,
        grid=(bsz, t_q // tq),
        in_specs=[pl.BlockSpec((None, B_HEADS * LANES, tq), lambda b, i: (b, 0, i)),
                  pl.BlockSpec((None, lpad, 384), lambda b, i: (b, 0, 0)),
                  pl.BlockSpec((None, B_HEADS, nblk, V_ROWS, tk), lambda b, i: (b, 0, 0, 0, 0)),
                  pl.BlockSpec((None, B_HEADS // 2, lpad, LANES), lambda b, i: (b, 0, 0, 0))],
        out_specs=pl.BlockSpec((None, tq, 384), lambda b, i: (b, i, 0)),
        out_shape=jax.ShapeDtypeStruct((bsz, t_q, 384), BF16),
        compiler_params=_cparams(("parallel", "arbitrary")),
        name="fox_attention",
    )(qt, k, vt, ck)


def _pool_kernel(cur_ref, prev_ref, hist_ref, w_ref, s_ref, o_ref, ext, *, tc, start_pos):
    i = pl.program_id(1)
    cur = cur_ref[...]
    ext[0:16, :] = jnp.where(i == 0, hist_ref[...], prev_ref[tc - 16:, :])
    ext[16:, :] = cur
    pos = start_pos + i * tc + lax.broadcasted_iota(I32, (tc, POOL_WIDTH), 0)
    lane = _lane_iota((tc, POOL_WIDTH))
    run = cur
    pooled = jnp.zeros_like(cur)
    k = 1
    for g, w in enumerate(POOL_WINDOWS):
        while k < w:
            run = run + ext[16 - k:16 - k + tc, :]
            k += 1
        cnt = jnp.minimum(pos + 1, w).astype(F32)
        in_group = (lane >= g * POOL_GROUP_DIM) & (lane < (g + 1) * POOL_GROUP_DIM)
        pooled = jnp.where(in_group, run / cnt, pooled)
    z = (pooled - cur).astype(BF16)
    o_ref[...] = (jnp.dot(z, w_ref[...], preferred_element_type=F32) * s_ref[...]).astype(o_ref.dtype)


def _pool(cu, hist16, w_bd, scale, *, tc, start_pos):
    bsz, t, n = cu.shape
    kern = functools.partial(_pool_kernel, tc=tc, start_pos=start_pos)
    return pl.pallas_call(
        kern,
        grid=(bsz, t // tc),
        in_specs=[pl.BlockSpec((None, tc, n), lambda b, i: (b, i, 0)),
                  pl.BlockSpec((None, tc, n), lambda b, i: (b, jnp.maximum(i - 1, 0), 0)),
                  pl.BlockSpec((None, 16, n), lambda b, i: (b, 0, 0)),
                  _const_spec((n, n)), _const_spec((1, n))],
        out_specs=pl.BlockSpec((None, tc, n), lambda b, i: (b, i, 0)),
        out_shape=jax.ShapeDtypeStruct((bsz, t, n), BF16),
        scratch_shapes=[pltpu.VMEM((16 + tc, n), F32)],
        compiler_params=_cparams(("parallel", "arbitrary")),
        name="pool_mixer",
    )(cu, cu, hist16, w_bd, scale.reshape(1, n))


def _route(logits):
    lane = _lane_iota(logits.shape).astype(F32)
    lg = jnp.where(lane < N_EXPERTS, logits, NEG_INF)
    m1 = jnp.max(lg, axis=1, keepdims=True)
    i1 = jnp.min(jnp.where(lg == m1, lane, float(LANES)), axis=1, keepdims=True)
    hot1 = lane == i1
    lg2 = jnp.where(hot1, NEG_INF, lg)
    m2 = jnp.max(lg2, axis=1, keepdims=True)
    i2 = jnp.min(jnp.where(lg2 == m2, lane, float(LANES)), axis=1, keepdims=True)
    hot2 = lane == i2
    e2 = jnp.exp(m2 - m1)
    den = 1.0 + e2
    return jnp.where(hot1, 1.0 / den, 0.0) + jnp.where(hot2, e2 / den, 0.0)


def _merge_kernel(x_ref, oa_ref, ob_ref, oc_ref, gate_ref, g1_ref, sc2_ref, sh2_ref, g_ref,
                  wa_ref, wb_ref, wc_ref, wo_ref, *rest, moe):
    if moe:
        rw_ref, rb_ref, xo_ref, h_ref, gw_ref = rest
    else:
        xo_ref, h_ref = rest
    d = D_MODEL
    merged = (gate_ref[:, 0:d] * jnp.dot(oa_ref[...], wa_ref[...], preferred_element_type=F32)
              + gate_ref[:, d:2 * d] * jnp.dot(ob_ref[...], wb_ref[...], preferred_element_type=F32)
              + gate_ref[:, 2 * d:3 * d] * jnp.dot(oc_ref[...], wc_ref[...], preferred_element_type=F32))
    x = x_ref[...] + g1_ref[...] * jnp.dot(merged.astype(BF16), wo_ref[...], preferred_element_type=F32)
    xo_ref[...] = x
    ms = jnp.mean(x * x, axis=-1, keepdims=True)
    y = x * lax.rsqrt(ms + NORM_EPS) * g_ref[...]
    h = y * (1.0 + sc2_ref[...]) + sh2_ref[...]
    h_ref[...] = h.astype(BF16)
    if moe:
        h_hi = h.astype(BF16)
        h_lo = (h - h_hi.astype(F32)).astype(BF16)
        logits = (jnp.dot(h_hi, rw_ref[0], preferred_element_type=F32)
                  + jnp.dot(h_lo, rw_ref[0], preferred_element_type=F32)
                  + jnp.dot(h_hi, rw_ref[1], preferred_element_type=F32)) + rb_ref[...]
        gw_ref[...] = _route(logits)


def _merge(x, oa, ob, oc, gates, g1, sc2, sh2, g, wa, wb, wc, wo, router, tm):
    bsz, t, d = x.shape
    mrows = g1.shape[1]
    mblk = 1 if mrows == 1 else tm
    mod_spec = pl.BlockSpec((None, mblk, d), (lambda b, i: (b, 0, 0)) if mrows == 1 else (lambda b, i: (b, i, 0)))

    def tok(n):
        return pl.BlockSpec((None, tm, n), lambda b, i: (b, i, 0))

    in_specs = [tok(d), tok(384), tok(384), tok(256), tok(3 * d), mod_spec, mod_spec, mod_spec,
                _const_spec((1, d)), _const_spec(wa.shape), _const_spec(wb.shape), _const_spec(wc.shape),
                _const_spec(wo.shape)]
    args = [x, oa, ob, oc, gates, g1, sc2, sh2, g.reshape(1, d), wa, wb, wc, wo]
    out_specs = [tok(d), tok(d)]
    out_shape = [jax.ShapeDtypeStruct((bsz, t, d), F32), jax.ShapeDtypeStruct((bsz, t, d), BF16)]
    if router is not None:
        rw, rb = router
        in_specs += [_const_spec(rw.shape), _const_spec(rb.shape)]
        args += [rw, rb]
        out_specs.append(tok(LANES))
        out_shape.append(jax.ShapeDtypeStruct((bsz, t, LANES), F32))
    return pl.pallas_call(
        functools.partial(_merge_kernel, moe=router is not None),
        grid=(bsz, t // tm),
        in_specs=in_specs, out_specs=out_specs, out_shape=out_shape,
        compiler_params=_cparams(("parallel", "parallel")),
        name="merge_out",
    )(*args)


def _final_norm(x, gain):
    ms = jnp.mean(x * x, axis=-1, keepdims=True)
    return x * lax.rsqrt(ms + NORM_EPS) * gain


def _ffn_kernel(x_ref, h_ref, g2_ref, wg_ref, wu_ref, wd_ref, *rest, n_chunks, final):
    if final:
        fg_ref, o_ref = rest
    else:
        (o_ref,) = rest
    h = h_ref[...]
    tf = wg_ref.shape[1] // n_chunks
    acc = jnp.zeros(x_ref.shape, F32)
    for c in range(n_chunks):
        gt = jnp.dot(h, wg_ref[:, c * tf:(c + 1) * tf], preferred_element_type=F32)
        up = jnp.dot(h, wu_ref[:, c * tf:(c + 1) * tf], preferred_element_type=F32)
        act = (gt * jax.nn.sigmoid(gt) * up).astype(BF16)
        acc = acc + jnp.dot(act, wd_ref[c * tf:(c + 1) * tf, :], preferred_element_type=F32)
    x = x_ref[...] + g2_ref[...] * acc
    o_ref[...] = _final_norm(x, fg_ref[...]) if final else x


def _ffn(x, h, g2, wg, wu, wd, final_g, tm):
    bsz, t, d = x.shape
    mrows = g2.shape[1]
    mblk = 1 if mrows == 1 else tm
    mod_spec = pl.BlockSpec((None, mblk, d), (lambda b, i: (b, 0, 0)) if mrows == 1 else (lambda b, i: (b, i, 0)))
    tok = pl.BlockSpec((None, tm, d), lambda b, i: (b, i, 0))
    in_specs = [tok, tok, mod_spec, _const_spec(wg.shape), _const_spec(wu.shape), _const_spec(wd.shape)]
    args = [x, h, g2, wg, wu, wd]
    if final_g is not None:
        in_specs.append(_const_spec((1, d)))
        args.append(final_g.reshape(1, d))
    return pl.pallas_call(
        functools.partial(_ffn_kernel, n_chunks=2, final=final_g is not None),
        grid=(bsz, t // tm),
        in_specs=in_specs, out_specs=tok,
        out_shape=jax.ShapeDtypeStruct((bsz, t, d), F32),
        compiler_params=_cparams(("parallel", "parallel")),
        name="ffn_dense",
    )(*args)


def _moe_kernel(x_ref, h_ref, g2_ref, gw_ref, wg_ref, wu_ref, wd_ref, *rest, final):
    if final:
        fg_ref, o_ref, acc_ref, posc_ref, posr_ref = rest
    else:
        o_ref, acc_ref, posc_ref, posr_ref = rest
    e = pl.program_id(2)
    tm = h_ref.shape[0]
    n_slabs = tm // MOE_CHUNK

    @pl.when(e == 0)
    def _():
        acc_ref[...] = jnp.zeros_like(acc_ref)
        routed = gw_ref[...] != 0.0
        r_f = jnp.where(routed, 1.0, 0.0)
        r_b = r_f.astype(BF16)
        r_t = r_f.T
        r_tb = r_t.astype(BF16)
        tok_l = _lane_iota((MOE_CHUNK, tm))
        tok_r = _row_iota((MOE_CHUNK, tm))
        rank_r = jnp.zeros((LANES, tm), F32)
        for s in range(n_slabs):
            rows = slice(s * MOE_CHUNK, (s + 1) * MOE_CHUNK)
            earlier = jnp.where(tok_l < tok_r + s * MOE_CHUNK, 1.0, 0.0).astype(BF16)
            rank_c = jnp.dot(earlier, r_b, preferred_element_type=F32)
            posc_ref[rows, :] = jnp.where(routed[rows], rank_c, -1.0)
            later = jnp.where(tok_r + s * MOE_CHUNK < tok_l, 1.0, 0.0).astype(BF16)
            rank_r = rank_r + jnp.dot(r_tb[:, rows], later, preferred_element_type=F32)
        posr_ref[...] = jnp.where(r_t != 0.0, rank_r, -1.0)

    lane_e = _lane_iota((tm, LANES)) == e
    pos_c = jnp.sum(jnp.where(lane_e, posc_ref[...], 0.0), axis=1, keepdims=True)
    gate_c = jnp.sum(jnp.where(lane_e, gw_ref[...], 0.0), axis=1, keepdims=True)
    pos_r = posr_ref[pl.ds(e, 1), :]
    n_routed = (jnp.max(pos_r) + 1.0).astype(I32)

    def run_chunk(base, n_rows):
        slot_rows = _row_iota((n_rows, tm)).astype(F32)
        slot_lanes = _lane_iota((MOE_CHUNK, n_rows)).astype(F32)
        pack = jnp.where(pos_r - base == slot_rows, 1.0, 0.0).astype(BF16)
        xc = jnp.dot(pack, h_ref[...], preferred_element_type=F32).astype(BF16)
        gt = jnp.dot(xc, wg_ref[...], preferred_element_type=F32)
        up = jnp.dot(xc, wu_ref[...], preferred_element_type=F32)
        act = (gt * jax.nn.sigmoid(gt) * up).astype(BF16)
        y = jnp.dot(act, wd_ref[...], preferred_element_type=F32).astype(BF16)
        for s in range(n_slabs):
            rows = slice(s * MOE_CHUNK, (s + 1) * MOE_CHUNK)
            unpack = jnp.where(pos_c[rows] - base == slot_lanes, 1.0, 0.0).astype(BF16)
            acc_ref[rows, :] += gate_c[rows] * jnp.dot(unpack, y, preferred_element_type=F32)

    def first(c, carry):
        run_chunk(0.0, MOE_CHUNK)
        return carry

    def later(c, carry):
        run_chunk((MOE_CHUNK + c * (MOE_CHUNK // 2)).astype(F32), MOE_CHUNK // 2)
        return carry

    lax.fori_loop(0, jnp.minimum(n_routed, 1), first, 0)
    n_later = (jnp.maximum(n_routed - MOE_CHUNK, 0) + MOE_CHUNK // 2 - 1) // (MOE_CHUNK // 2)
    lax.fori_loop(0, n_later, later, 0)

    @pl.when(e == pl.num_programs(2) - 1)
    def _():
        x = x_ref[...] + g2_ref[...] * acc_ref[...]
        o_ref[...] = _final_norm(x, fg_ref[...]) if final else x


def _moe(x, h, g2, gw, wg, wu, wd, final_g, tm):
    bsz, t, d = x.shape
    n_e, _, dff = wg.shape
    mrows = g2.shape[1]
    mblk = 1 if mrows == 1 else tm
    mod_spec = pl.BlockSpec((None, mblk, d), (lambda b, i, e: (b, 0, 0)) if mrows == 1 else (lambda b, i, e: (b, i, 0)))
    tok = pl.BlockSpec((None, tm, d), lambda b, i, e: (b, i, 0))
    tok_once = pl.BlockSpec((None, tm, d), lambda b, i, e: (b, i, 0), pipeline_mode=pl.Buffered(1))
    in_specs = [tok_once, tok, mod_spec, pl.BlockSpec((None, tm, LANES), lambda b, i, e: (b, i, 0)),
                pl.BlockSpec((None, d, dff), lambda b, i, e: (e, 0, 0)),
                pl.BlockSpec((None, d, dff), lambda b, i, e: (e, 0, 0)),
                pl.BlockSpec((None, dff, d), lambda b, i, e: (e, 0, 0))]
    args = [x, h, g2, gw, wg, wu, wd]
    if final_g is not None:
        in_specs.append(pl.BlockSpec((1, d), lambda b, i, e: (0, 0)))
        args.append(final_g.reshape(1, d))
    return pl.pallas_call(
        functools.partial(_moe_kernel, final=final_g is not None),
        grid=(bsz, t // tm, n_e),
        in_specs=in_specs, out_specs=tok,
        out_shape=jax.ShapeDtypeStruct((bsz, t, d), F32),
        scratch_shapes=[pltpu.VMEM((tm, d), F32), pltpu.VMEM((tm, LANES), F32), pltpu.VMEM((LANES, tm), F32)],
        compiler_params=_cparams(("parallel", "parallel", "arbitrary")),
        name="moe_routed",
    )(*args)


def _rope_tables(pos):
    inv = ROPE_THETA ** (-jnp.arange(HALF, dtype=F32) / HALF)
    ang = pos.astype(F32)[:, None] * inv[None, :]
    cos, sin = jnp.cos(ang), jnp.sin(ang)
    return jnp.tile(cos, (1, 4)), jnp.tile(jnp.concatenate([-sin, sin], axis=1), (1, 2))


def _pick_tile(n, pref):
    t = min(n, pref)
    while n % t:
        t //= 2
    return t


def _per_seq_cols(a, bsz, t, width):
    f = a.shape[1]
    a = jnp.moveaxis(a[0].reshape(f, bsz, t), 1, 0)
    return jnp.pad(a, ((0, 0), (0, 0), (0, width - t)))


def _value_blocks(past_vt, new_vt, bsz, t, lpad):
    lead = new_vt.shape[:-2]
    new_b = jnp.moveaxis(new_vt.reshape(*lead, V_ROWS, bsz, t), -2, 0)
    full = jnp.concatenate([past_vt, new_b], axis=-1)
    full = jnp.pad(full, [(0, 0)] * (full.ndim - 1) + [(0, lpad - full.shape[-1])])
    full = full.reshape(bsz, *lead, V_ROWS, lpad // KV_BLOCK, KV_BLOCK)
    return jnp.moveaxis(full, -2, -3)


def _with_ones_rows(vt):
    ones = jnp.ones(vt.shape[:-2] + (1, vt.shape[-1]), vt.dtype)
    zeros = jnp.zeros(vt.shape[:-2] + (V_ROWS - HEAD_DIM - 1, vt.shape[-1]), vt.dtype)
    return jnp.concatenate([vt, ones, zeros], axis=-2)


def _mixers(inp, past, n_past, lw, bsz, t):
    aqt, iqt, kik, avt, bqt, bk, bvt, cu, sm, smt = inp
    n_keys = n_past + t
    n_sel = min(TOPK_MAX, n_keys // 4)
    lpad = -(-n_keys // (2 * KV_BLOCK)) * 2 * KV_BLOCK
    if past is None:
        kik_all, avt_all, bk_all, bvt_all, logf_all = kik, avt, bk, bvt, sm
        hist16 = jnp.zeros((bsz, 16, POOL_WIDTH), F32)
        tq_a, tq_b, t_pad = _pick_tile(t, 256), _pick_tile(t, 256), t
    else:
        pa, pb, plf, pc = past
        pk, pv, pik = (pa[:, :, j].astype(BF16) for j in range(3))

        def join_rows(p, new):
            full = jnp.concatenate([p, new.reshape(bsz, t, new.shape[-1])], axis=1)
            return jnp.pad(full, ((0, 0), (0, lpad - n_keys), (0, 0)))

        kik_all = join_rows(jnp.concatenate([pk, pik], axis=-1), kik)
        avt_all = _value_blocks(_with_ones_rows(jnp.swapaxes(pv, 1, 2)), avt[0, 0], bsz, t, lpad)
        bk_all = join_rows(pb[:, :, 0].astype(BF16).reshape(bsz, n_past, 384), bk)
        pvt = jnp.transpose(pb[:, :, 1].astype(BF16), (0, 2, 3, 1))
        bvt_all = _value_blocks(_with_ones_rows(pvt), bvt[0, :, 0], bsz, t, lpad)
        logf_all = join_rows(jnp.pad(plf, ((0, 0), (0, 0), (0, LANES - B_HEADS))), sm)
        hist16 = jnp.pad(pc, ((0, 0), (1, 0), (0, 0)))
        tq_a = tq_b = t_pad = LANES
        aqt, iqt, bqt, smt = (_per_seq_cols(a, bsz, t, t_pad) for a in (aqt, iqt, bqt, smt))

    oa = _dsa(aqt, iqt, smt, kik_all, avt_all, tq=tq_a, n_keys=n_keys, q_pos0=n_past, n_sel=n_sel)
    ck = _cum_logf(logf_all, 2 * KV_BLOCK)
    ob = _fox(bqt, bk_all, bvt_all, ck, tq=tq_b, n_keys=n_keys, q_pos0=n_past)
    cu = cu.reshape(bsz, t, POOL_WIDTH)
    oc = _pool(cu, hist16, lw["pool_bd"], lw["pool_scale"], tc=_pick_tile(t, 1024), start_pos=n_past)
    return oa[:, :t], ob[:, :t], oc


def _layer(x, mod, past, n_past, pos_tab, lw, layer, final_g, per_token):
    bsz, t, d = x.shape
    sh1, sc1, g1, sh2, sc2, g2 = mod
    if per_token:
        xt = x.reshape(1, bsz * t, d)
        sh1, sc1, g1, sh2, sc2, g2 = (jnp.broadcast_to(m, (bsz, t, d)).reshape(1, bsz * t, d) for m in mod)
        cos, sin = (jnp.tile(a, (bsz, 1)) for a in pos_tab)
    else:
        xt = x
        cos, sin = pos_tab
    tm = KV_BLOCK
    (aqt, iqt, nat, kik, avt, nbt, bqt, bk, bvt, cu, sm, smt, gates) = _in_proj(
        xt, sc1, sh1, lw["norm_mix_g"], lw["w_in"], lw["bf_bias"], cos, sin, tm)
    oa, ob, oc = _mixers((aqt, iqt, kik, avt, bqt, bk, bvt, cu, sm, smt), past, n_past, lw, bsz, t)

    def flat(a):
        return a.reshape(xt.shape[0], xt.shape[1], a.shape[-1])

    router = (lw["router_w"], lw["router_b"]) if layer % 2 else None
    res = _merge(xt, flat(oa), flat(ob), flat(oc), gates, g1, sc2, sh2, lw["norm_ffn_g"],
                 lw["w_br_a"], lw["w_br_b"], lw["w_br_c"], lw["w_out"], router, _pick_tile(xt.shape[1], 512))
    tmf = _pick_tile(xt.shape[1], 512)
    if layer % 2 == 0:
        x_mid, h2 = res
        x_new = _ffn(x_mid, h2, g2, lw["ffn_wg"], lw["ffn_wu"], lw["ffn_wd"], final_g, tmf)
    else:
        x_mid, h2, gw = res
        x_new = _moe(x_mid, h2, g2, gw, lw["moe_wg"], lw["moe_wu"], lw["moe_wd"], final_g,
                     _pick_tile(xt.shape[1], 1024))
    def token_major(a, *feat):
        a = a.reshape(a.shape[0], *feat, -1, t) if per_token else a.reshape(a.shape[0], *feat, 1, t)
        a = jnp.moveaxis(a, (-2, -1), (1, 2))
        return a.reshape(bsz, t, *feat)

    new_a = token_major(nat, 3, HEAD_DIM)
    new_b = token_major(nbt, 2, B_HEADS, HEAD_DIM)
    new_logf = sm.reshape(bsz, t, LANES)[:, :, :B_HEADS]
    new_pool = cu.reshape(bsz, t, POOL_WIDTH)[:, t - POOL_HIST:, :]
    return x_new.reshape(bsz, t, d), (new_a, new_b, new_logf, new_pool)


def kernel(x_prompt, x_sample, cache_a_kvi, cache_b_kv, cache_b_logf, state_c_pool, c_prompt, c_sample,
           ada_w, ada_b, norm_mix_g, w_in, b_forget, pool_w, pool_scale, w_br_a, w_br_b, w_br_c, w_out,
           norm_ffn_g, ffn_w_gate, ffn_w_up, ffn_w_down, moe_router_w, moe_router_b, moe_w_gate,
           moe_w_up, moe_w_down, final_norm_g):
    depth = ada_w.shape[0]
    bp, tp, d = x_prompt.shape
    bs, ts, _ = x_sample.shape
    n_past = cache_a_kvi.shape[2]
    assert tp % KV_BLOCK == 0 and (bs * ts) % KV_BLOCK == 0 and ts <= LANES

    rows = -(-(bp + bs) // 8) * 8
    c_all = jnp.pad(jnp.concatenate([c_prompt, c_sample], axis=0), ((0, rows - bp - bs), (0, 0)))
    mod_all = _ada(c_all, ada_w, ada_b)

    tab_p = _rope_tables(jnp.arange(tp))
    tab_s = _rope_tables(n_past + jnp.arange(ts))

    xp, xs = x_prompt, x_sample
    outs_p, outs_s = [], []
    for layer in range(depth):
        j = layer // 2
        w_l, bias_l = _in_weights(w_in[layer], b_forget[layer])
        pw = pool_w[layer]
        pool_bd = jnp.zeros((POOL_WIDTH, POOL_WIDTH), F32)
        for g in range(len(POOL_WINDOWS)):
            sl = slice(g * POOL_GROUP_DIM, (g + 1) * POOL_GROUP_DIM)
            pool_bd = pool_bd.at[sl, sl].set(pw[g])
        lw = dict(w_in=w_l, bf_bias=bias_l, norm_mix_g=norm_mix_g[layer], norm_ffn_g=norm_ffn_g[layer],
                  pool_bd=pool_bd.astype(BF16), pool_scale=pool_scale[layer],
                  w_br_a=w_br_a[layer].astype(BF16), w_br_b=w_br_b[layer].astype(BF16),
                  w_br_c=w_br_c[layer].astype(BF16), w_out=w_out[layer].astype(BF16))
        if layer % 2 == 0:
            lw.update(ffn_wg=ffn_w_gate[j].astype(BF16), ffn_wu=ffn_w_up[j].astype(BF16),
                      ffn_wd=ffn_w_down[j].astype(BF16))
        else:
            rw = jnp.pad(moe_router_w[j], ((0, 0), (0, LANES - N_EXPERTS)))
            rw_hi = rw.astype(BF16)
            lw.update(router_w=jnp.stack([rw_hi, (rw - rw_hi.astype(F32)).astype(BF16)]),
                      router_b=jnp.pad(moe_router_b[j], (0, LANES - N_EXPERTS)).reshape(1, LANES),
                      moe_wg=moe_w_gate[j].astype(BF16), moe_wu=moe_w_up[j].astype(BF16),
                      moe_wd=moe_w_down[j].astype(BF16))
        final_g = final_norm_g if layer == depth - 1 else None
        mod_p = [m[:, None, :] for m in jnp.split(mod_all[layer, :bp], 6, axis=-1)]
        mod_s = [m[:, None, :] for m in jnp.split(mod_all[layer, bp:bp + bs], 6, axis=-1)]
        xp, new_p = _layer(xp, mod_p, None, 0, tab_p, lw, layer, final_g, per_token=False)
        past = (cache_a_kvi[layer], cache_b_kv[layer], cache_b_logf[layer], state_c_pool[layer])
        xs, new_s = _layer(xs, mod_s, past, n_past, tab_s, lw, layer, final_g, per_token=True)
        outs_p.append(new_p)
        outs_s.append(new_s)

    def stack(outs, k):
        return jnp.stack([o[k] for o in outs])

    return (xp, xs,
            stack(outs_p, 0), stack(outs_p, 1), stack(outs_p, 2), stack(outs_p, 3),
            stack(outs_s, 0), stack(outs_s, 1), stack(outs_s, 2), stack(outs_s, 3))
```

```python
import functools

import jax
import jax.numpy as jnp
import numpy as np
from jax import lax
from jax.experimental import pallas as pl
from jax.experimental.pallas import tpu as pltpu

F32 = jnp.float32
BF16 = jnp.bfloat16
I32 = jnp.int32

D_MODEL = 1024
CHUNK = 64
HEAD_DIM = 64
HALF = HEAD_DIM // 2
ROPE_THETA = 10000.0
NORM_EPS = 1e-6
A_HEADS = 6
IDX_HEADS = 4
TOPK_MAX = 256
B_HEADS = 6
POOL_WINDOWS = (2, 4, 8, 16)
POOL_GROUP_DIM = 64
POOL_WIDTH = 256
POOL_HIST = 15
N_EXPERTS = 8
LANES = 128
SUBLANES = 8
LOG2E = 1.4426950408889634
QK_SCALE = HEAD_DIM ** -0.5 * LOG2E
KV_BLOCK = 256
V_ROWS = HEAD_DIM + 16
MOE_CHUNK = 256
VMEM_LIMIT = 56 * 1024 * 1024
NEG_INF = float("-inf")
KEY_DT = jnp.bfloat16
RANK_ZERO = 0x8000
RANK_NEG_INF = 0x007F
RANK_POS_INF = 0xFF80
IW_LANE = 8

C_AQ, C_IQ, C_A, C_B, C_CU, C_SM, C_GATE, C_END = 0, 384, 640, 896, 2048, 2304, 2432, 5504


def _cparams(sem):
    return pltpu.CompilerParams(dimension_semantics=sem, vmem_limit_bytes=VMEM_LIMIT)


def _const_spec(shape):
    nd = len(shape)
    return pl.BlockSpec(shape, lambda *_: (0,) * nd, pipeline_mode=pl.Buffered(1))


def _lane_iota(shape):
    return lax.broadcasted_iota(I32, shape, len(shape) - 1)


def _row_iota(shape):
    return lax.broadcasted_iota(I32, shape, len(shape) - 2)


def _ada_kernel(c_ref, w_ref, b_ref, o_ref):
    c = c_ref[...]
    s = c * jax.nn.sigmoid(c)
    o_ref[...] = jnp.dot(s, w_ref[...], preferred_element_type=F32,
                         precision=lax.Precision.HIGHEST) + b_ref[...]


def _ada(c_all, ada_w, ada_b):
    depth, d, n = ada_w.shape
    rows = c_all.shape[0]
    tn = 1536
    return pl.pallas_call(
        _ada_kernel,
        grid=(depth, n // tn),
        in_specs=[pl.BlockSpec((rows, d), lambda l, j: (0, 0)),
                  pl.BlockSpec((None, d, tn), lambda l, j: (l, 0, j)),
                  pl.BlockSpec((None, 1, tn), lambda l, j: (l, 0, j))],
        out_specs=pl.BlockSpec((None, rows, tn), lambda l, j: (l, 0, j)),
        out_shape=jax.ShapeDtypeStruct((depth, rows, n), F32),
        compiler_params=_cparams(("arbitrary", "arbitrary")),
        name="ada_mod",
    )(c_all, ada_w, ada_b.reshape(depth, 1, n))


def _in_kernel(x_ref, sc_ref, sh_ref, g_ref, w_ref, bf_ref, cos_ref, sin_ref,
               aqt_ref, iqt_ref, nat_ref, kik_ref, avt_ref, nbt_ref, bqt_ref, bk_ref, bvt_ref,
               cu_ref, sm_ref, smt_ref, gate_ref):
    x = x_ref[...]
    ms = jnp.mean(x * x, axis=-1, keepdims=True)
    y = x * lax.rsqrt(ms + NORM_EPS) * g_ref[...]
    h = (y * (1.0 + sc_ref[...]) + sh_ref[...]).astype(BF16)
    tm = x.shape[0]

    def mm(a, b):
        return jnp.dot(h, w_ref[:, a:b], preferred_element_type=F32)

    cos = cos_ref[...]
    sin = sin_ref[...]
    lane = _lane_iota((tm, LANES))
    low = lane < HEAD_DIM
    first_half = (lane & HALF) == 0

    def rope(z):
        swapped = jnp.where(first_half, pltpu.roll(z, LANES - HALF, 1), pltpu.roll(z, HALF, 1))
        return z * cos + swapped * sin

    zeros64 = jnp.zeros((HEAD_DIM, tm), BF16)
    ones_rows = jnp.where(_row_iota((V_ROWS - HEAD_DIM, KV_BLOCK)) == 0, 1.0, 0.0).astype(BF16)

    def put_values(ref, vt):
        for j in range(tm // KV_BLOCK):
            ref[j, 0:HEAD_DIM, :] = vt[:, j * KV_BLOCK:(j + 1) * KV_BLOCK]
            ref[j, HEAD_DIM:, :] = ones_rows

    def put_heads(ref, zt, p, slot_even, slot_odd):
        for hh, slot in ((0, slot_even), (1, slot_odd)):
            base = (2 * p + hh) * LANES
            ref[base + slot * HEAD_DIM:base + (slot + 1) * HEAD_DIM, :] = zt[hh * HEAD_DIM:(hh + 1) * HEAD_DIM]
            ref[base + (1 - slot) * HEAD_DIM:base + (2 - slot) * HEAD_DIM, :] = zeros64

    z = mm(C_AQ, C_IQ)
    for p in range(3):
        zt = (rope(z[:, p * LANES:(p + 1) * LANES]) * QK_SCALE).T.astype(BF16)
        put_heads(aqt_ref, zt, p, 0, 0)
    z = mm(C_IQ, C_A)
    for p in range(2):
        zt = rope(z[:, p * LANES:(p + 1) * LANES]).T.astype(BF16)
        put_heads(iqt_ref, zt, p, 1, 1)

    z = mm(C_A, C_B)
    kv = z[:, :LANES]
    r0 = jnp.where(low, rope(kv), kv)
    r1 = rope(z[:, LANES:])
    r0t = r0.T
    nat_ref[0:LANES, :] = r0t
    nat_ref[LANES:, :] = r1.T[0:HEAD_DIM, :]
    kik_ref[...] = jnp.where(low, r0, pltpu.roll(r1, HEAD_DIM, 1)).astype(BF16)
    put_values(avt_ref, r0t[HEAD_DIM:, :].astype(BF16))

    z = mm(C_B, C_CU)
    bk_ref[...] = z[:, 384:768].astype(BF16)
    for p in range(3):
        zt = (z[:, p * LANES:(p + 1) * LANES] * QK_SCALE).T.astype(BF16)
        put_heads(bqt_ref, zt, p, 0, 1)
        nbt_ref[p * LANES:(p + 1) * LANES, :] = z[:, 384 + p * LANES:384 + (p + 1) * LANES].T
        vt = z[:, 768 + p * LANES:768 + (p + 1) * LANES].T
        nbt_ref[384 + p * LANES:384 + (p + 1) * LANES, :] = vt
        vt = vt.astype(BF16)
        for hh in range(2):
            put_values(bvt_ref.at[2 * p + hh], vt[hh * HEAD_DIM:(hh + 1) * HEAD_DIM])

    cu_ref[...] = mm(C_CU, C_SM)

    z = mm(C_SM, C_GATE)
    t = z + bf_ref[...]
    logf = jnp.minimum(t, 0.0) - jnp.log1p(jnp.exp(-jnp.abs(t)))
    sm = jnp.where(lane < B_HEADS, logf, z)
    sm_ref[...] = sm
    smt_ref[...] = sm.T[0:16, :]

    for c in range(3):
        gate_ref[:, c * D_MODEL:(c + 1) * D_MODEL] = jax.nn.sigmoid(
            mm(C_GATE + c * D_MODEL, C_GATE + (c + 1) * D_MODEL)).astype(BF16)


def _in_weights(w_in_l, b_forget_l):
    d = w_in_l.shape[0]
    sizes = (384, 64, 64, 256, 4, 64, 384, 384, 384, 6, 256, 3072)
    o = np.concatenate([[0], np.cumsum(sizes)])
    w_in_l = w_in_l.astype(BF16)
    cols = [w_in_l[:, o[0]:o[1]],
            w_in_l[:, o[3]:o[4]],
            w_in_l[:, o[1]:o[3]], w_in_l[:, o[5]:o[6]], jnp.zeros((d, 64), BF16),
            w_in_l[:, o[6]:o[9]],
            w_in_l[:, o[10]:o[11]],
            w_in_l[:, o[9]:o[10]], jnp.zeros((d, IW_LANE - B_HEADS), BF16), w_in_l[:, o[4]:o[5]],
            jnp.zeros((d, LANES - IW_LANE - IDX_HEADS), BF16),
            w_in_l[:, o[11]:o[12]]]
    w = jnp.concatenate(cols, axis=1)
    assert w.shape[1] == C_END, w.shape
    bias = jnp.concatenate([b_forget_l, jnp.zeros((LANES - B_HEADS,), F32)]).reshape(1, LANES)
    return w, bias


def _in_proj(x, sc, sh, g, w, bias, cos, sin, tm):
    bsz, t, d = x.shape
    mrows = sc.shape[1]
    mblk = 1 if mrows == 1 else tm
    mod_spec = pl.BlockSpec((None, mblk, d), (lambda b, i: (b, 0, 0)) if mrows == 1 else (lambda b, i: (b, i, 0)))
    nblk = t // tm
    nsub = tm // KV_BLOCK

    def rows(n, dt):
        return pl.BlockSpec((None, tm, n), lambda b, i: (b, i, 0)), jax.ShapeDtypeStruct((bsz, t, n), dt)

    def cols(n, dt):
        return pl.BlockSpec((None, n, tm), lambda b, i: (b, 0, i)), jax.ShapeDtypeStruct((bsz, n, t), dt)

    outs = [cols(A_HEADS * LANES, BF16), cols(IDX_HEADS * LANES, BF16), cols(192, F32), rows(LANES, BF16),
            (pl.BlockSpec((None, nsub, V_ROWS, KV_BLOCK), lambda b, i: (b, i, 0, 0)),
             jax.ShapeDtypeStruct((bsz, nblk * nsub, V_ROWS, KV_BLOCK), BF16)),
            cols(768, F32), cols(B_HEADS * LANES, BF16), rows(384, BF16),
            (pl.BlockSpec((None, B_HEADS, nsub, V_ROWS, KV_BLOCK), lambda b, i: (b, 0, i, 0, 0)),
             jax.ShapeDtypeStruct((bsz, B_HEADS, nblk * nsub, V_ROWS, KV_BLOCK), BF16)),
            rows(256, F32), rows(LANES, F32), cols(16, F32), rows(3 * D_MODEL, BF16)]
    return pl.pallas_call(
        _in_kernel,
        grid=(bsz, nblk),
        in_specs=[pl.BlockSpec((None, tm, d), lambda b, i: (b, i, 0)), mod_spec, mod_spec,
                  _const_spec((1, d)), _const_spec(w.shape), _const_spec((1, LANES)),
                  pl.BlockSpec((tm, LANES), lambda b, i: (i, 0)),
                  pl.BlockSpec((tm, LANES), lambda b, i: (i, 0))],
        out_specs=[o[0] for o in outs],
        out_shape=[o[1] for o in outs],
        compiler_params=_cparams(("parallel", "parallel")),
        name="in_proj",
    )(x, sc, sh, g.reshape(1, d), w, bias, cos, sin)


def _cum_kernel(x_ref, o_ref, carry_ref):
    @pl.when(pl.program_id(1) == 0)
    def _():
        carry_ref[...] = jnp.zeros_like(carry_ref)

    tc = x_ref.shape[0]
    tri = jnp.where(_lane_iota((tc, tc)) <= _row_iota((tc, tc)), 1.0, 0.0).astype(BF16)
    cum = carry_ref[0:1, :]
    rest = x_ref[...]
    for _ in range(3):
        piece = rest.astype(BF16)
        cum = cum + jnp.dot(tri, piece, preferred_element_type=F32)
        rest = rest - piece.astype(F32)
    carry_ref[...] = jnp.broadcast_to(cum[tc - 1:tc, :], carry_ref.shape)
    pieces = []
    rest = cum * LOG2E
    for _ in range(3):
        piece = rest.astype(BF16)
        pieces.append(piece.astype(F32))
        rest = rest - pieces[-1]
    lane = _lane_iota((tc, LANES))
    for p in range(B_HEADS // 2):
        slab = jnp.zeros((tc, LANES), F32)
        for hh in range(2):
            for j, piece in enumerate(pieces):
                dst, src = 3 * hh + j, 2 * p + hh
                slab = jnp.where(lane == dst, pltpu.roll(piece, (dst - src) % LANES, 1), slab)
        o_ref[p] = slab.astype(BF16)


def _cum_logf(x, tc):
    bsz, t, n = x.shape
    return pl.pallas_call(
        _cum_kernel,
        grid=(bsz, t // tc),
        in_specs=[pl.BlockSpec((None, tc, n), lambda b, i: (b, i, 0))],
        out_specs=pl.BlockSpec((None, B_HEADS // 2, tc, LANES), lambda b, i: (b, 0, i, 0)),
        out_shape=jax.ShapeDtypeStruct((bsz, B_HEADS // 2, t, LANES), BF16),
        scratch_shapes=[pltpu.VMEM((SUBLANES, LANES), F32)],
        compiler_params=_cparams(("parallel", "arbitrary")),
        name="logf_cumsum",
    )(x)


def _sum_keys(x):
    part = x.reshape(x.shape[0] // SUBLANES, SUBLANES, x.shape[1]).sum(axis=0)
    return jnp.sum(part, axis=0, keepdims=True)


def _max_keys(x):
    part = x.reshape(x.shape[0] // SUBLANES, SUBLANES, x.shape[1]).max(axis=0)
    return jnp.max(part, axis=0, keepdims=True)


def _dsa_kernel(aqt_ref, iqt_ref, smt_ref, kik_ref, avt_ref, o_ref, key_ref,
                *, tq, tk, n_keys, q_pos0, n_sel):
    i = pl.program_id(1)
    pos_first = q_pos0 + i * tq
    last_chunk = (pos_first + tq - 1) // CHUNK
    n_adm = jnp.minimum((last_chunk + 1) * CHUNK, n_keys)
    nkb = (n_adm + tk - 1) // tk

    q_pos = pos_first + _lane_iota((1, tq))
    q_lim = jnp.minimum((q_pos // CHUNK + 1) * CHUNK, n_keys)
    key_row = _row_iota((tk, tq))

    def keys(kb):
        return kik_ref[pl.ds(pl.multiple_of(kb * tk, tk), tk), :]

    iq4 = jnp.concatenate([iqt_ref[hd * LANES:(hd + 1) * LANES, :] for hd in range(IDX_HEADS)], axis=1)
    smt = smt_ref[...]
    w_rows = [smt[IW_LANE + hd:IW_LANE + hd + 1, :] for hd in range(IDX_HEADS)]

    def score_body(kb, carry):
        s4 = jnp.dot(keys(kb), iq4, preferred_element_type=F32)
        score = w_rows[0] * jnp.maximum(s4[:, 0:tq], 0.0)
        for hd in range(1, IDX_HEADS):
            score = score + w_rows[hd] * jnp.maximum(s4[:, hd * tq:(hd + 1) * tq], 0.0)
        score = jnp.where(key_row < q_lim - kb * tk, score, NEG_INF)
        key_ref[kb] = score.astype(KEY_DT)
        return carry

    lax.fori_loop(0, nkb, score_body, 0)

    one, zero = jnp.ones((), KEY_DT), jnp.zeros((), KEY_DT)
    packed_rows = 2 * SUBLANES

    def count(cand, strict):
        def hits(blk):
            h = jnp.where((blk > cand) if strict else (blk >= cand), one, zero)
            parts = [h[r * packed_rows:(r + 1) * packed_rows] for r in range(tk // packed_rows)]
            while len(parts) > 1:
                parts = [a + b for a, b in zip(parts[::2], parts[1::2])]
            return parts[0].astype(F32)

        def pair(j, acc):
            return acc + hits(key_ref[2 * j]) + hits(key_ref[2 * j + 1])

        acc = lax.fori_loop(0, nkb // 2, pair, jnp.zeros((packed_rows, tq), F32))
        acc = lax.fori_loop(2 * (nkb // 2), nkb, lambda kb, a: a + hits(key_ref[kb]), acc)
        return jnp.sum(acc, axis=0, keepdims=True)

    def pattern_value(u):
        bits = jnp.where(u >= RANK_ZERO, u - RANK_ZERO, (~u) & 0xFFFF)
        return lax.bitcast_convert_type(lax.shift_left(bits, 16), F32).astype(KEY_DT)

    def bit_body(b, u):
        cand_u = u | lax.shift_left(jnp.int32(1), 15 - b)
        cnt = count(pattern_value(cand_u), False)
        return jnp.where(cnt >= n_sel, cand_u, u)

    u_thr = jnp.maximum(lax.fori_loop(0, 16, bit_body, jnp.zeros((1, tq), I32)), RANK_NEG_INF)
    thr_16 = pattern_value(u_thr)

    def finer():
        lo = thr_16.astype(F32)
        hi = jnp.where(u_thr >= RANK_POS_INF, jnp.inf, pattern_value(u_thr + 1).astype(F32))

        def halve(_, lo_hi):
            lo, hi = lo_hi
            cand = (0.5 * lo + 0.5 * hi).astype(KEY_DT)
            enough = count(cand, False) >= n_sel
            return jnp.where(enough, cand.astype(F32), lo), jnp.where(enough, hi, cand.astype(F32))

        thr_f = lax.fori_loop(0, 16, halve, (lo, hi))[0].astype(KEY_DT)
        return thr_f, count(thr_f, True)

    above_16 = count(thr_16, True)
    thr_k, above = lax.cond(jnp.min(n_sel - above_16) <= 0.0, finer, lambda: (thr_16, above_16))
    need = jnp.maximum(n_sel - above, 0.0)
    thr = thr_k.astype(F32)

    aq6 = jnp.concatenate([aqt_ref[hd * LANES:(hd + 1) * LANES, :] for hd in range(A_HEADS)], axis=1)

    half = tk // 2
    lower = jnp.where(_lane_iota((half, half)) <= _row_iota((half, half)), 1.0, 0.0).astype(BF16)

    def attend():
        def body(kb, carry):
            eq_seen, ms, accs = carry
            blk = key_ref[kb].astype(F32)
            eq = blk == thr
            eq_f = jnp.where(eq, 1.0, 0.0)
            prefs = []
            for e in (eq_f[:half], eq_f[half:]):
                prefs.append(jnp.dot(lower, e.astype(BF16), preferred_element_type=F32) + eq_seen)
                eq_seen = eq_seen + _sum_keys(e)
            slack = jnp.where(blk >= thr, need - jnp.where(eq, jnp.concatenate(prefs, axis=0), 0.0), -1.0)
            bias = jnp.where(slack >= 0.0, jnp.where(jnp.abs(blk) < jnp.inf, 0.0, NEG_INF), NEG_INF)
            logits = jnp.dot(keys(kb), aq6, preferred_element_type=F32)
            vts = (avt_ref[2 * kb], avt_ref[2 * kb + 1])
            new_ms, new_accs = [], []
            for p in range(A_HEADS // 2):
                ps, alphas = [], []
                for hd in (2 * p, 2 * p + 1):
                    lg = logits[:, hd * tq:(hd + 1) * tq] + bias
                    m_old = ms[hd]
                    m_new = jnp.maximum(m_old, _max_keys(lg))
                    m_safe = jnp.where(m_new == NEG_INF, 0.0, m_new)
                    ps.append(jnp.exp2(lg - m_safe).astype(BF16))
                    alphas.append(jnp.exp2(m_old - m_safe))
                    new_ms.append(m_new)
                p2 = jnp.concatenate(ps, axis=1)
                pv = (jnp.dot(vts[0], p2[:tk // 2], preferred_element_type=F32)
                      + jnp.dot(vts[1], p2[tk // 2:], preferred_element_type=F32))
                new_accs.append(jnp.concatenate(alphas, axis=1) * accs[p] + pv)
            return eq_seen, tuple(new_ms), tuple(new_accs)

        init = (jnp.zeros((1, tq), F32),
                tuple(jnp.full((1, tq), NEG_INF, F32) for _ in range(A_HEADS)),
                tuple(jnp.zeros((V_ROWS, 2 * tq), F32) for _ in range(A_HEADS // 2)))
        return lax.fori_loop(0, nkb, body, init)[2]

    accs = attend()

    outs = []
    for p in range(A_HEADS // 2):
        o2 = accs[p][0:HEAD_DIM] / accs[p][HEAD_DIM:HEAD_DIM + 1]
        outs += [o2[:, 0:tq], o2[:, tq:2 * tq]]
    o_ref[...] = jnp.concatenate(outs, axis=0).T.astype(o_ref.dtype)


def _dsa(aqt, iqt, smt, kik, avt, *, tq, n_keys, q_pos0, n_sel):
    bsz, _, t_q = aqt.shape
    _, nblk, _, tkv = avt.shape
    assert nblk % 2 == 0
    tk = 2 * tkv
    kern = functools.partial(_dsa_kernel, tq=tq, tk=tk, n_keys=n_keys, q_pos0=q_pos0, n_sel=n_sel)
    return pl.pallas_call(
        kern,
        grid=(bsz, t_q // tq),
        in_specs=[pl.BlockSpec((None, A_HEADS * LANES, tq), lambda b, i: (b, 0, i)),
                  pl.BlockSpec((None, IDX_HEADS * LANES, tq), lambda b, i: (b, 0, i)),
                  pl.BlockSpec((None, 16, tq), lambda b, i: (b, 0, i)),
                  pl.BlockSpec((None, nblk * tkv, LANES), lambda b, i: (b, 0, 0)),
                  pl.BlockSpec((None, nblk, V_ROWS, tkv), lambda b, i: (b, 0, 0, 0))],
        out_specs=pl.BlockSpec((None, tq, 384), lambda b, i: (b, i, 0)),
        out_shape=jax.ShapeDtypeStruct((bsz, t_q, 384), BF16),
        scratch_shapes=[pltpu.VMEM((nblk // 2, tk, tq), KEY_DT)],
        compiler_params=_cparams(("parallel", "arbitrary")),
        name="dsa_attention",
    )(aqt, iqt, smt, kik, avt)


def _fox_kernel(qt_ref, k_ref, vt_ref, ck_ref, o_ref, *, tq, tk, n_keys, q_pos0):
    i = pl.program_id(1)
    pos_first = q_pos0 + i * tq
    n_full = pos_first // tk
    nkb = (jnp.minimum(pos_first + tq, n_keys) + tk - 1) // tk
    q_pos = pos_first + _lane_iota((1, tq))
    key_row = _row_iota((tk, tq))
    piece_row = _row_iota((LANES, tq))
    qts = []
    for hd in range(B_HEADS):
        minus = jnp.where((piece_row >= 3 * (hd % 2)) & (piece_row < 3 * (hd % 2) + 3), -1.0, 0.0).astype(BF16)
        qts.append(jnp.concatenate([qt_ref[hd * LANES:(hd + 1) * LANES, :], minus], axis=0))

    def step(kbs, state, masked):
        logits = []
        for hd in range(B_HEADS):
            for kb in kbs:
                rows = pl.ds(pl.multiple_of(kb * tk, tk), tk)
                kblk = jnp.concatenate([k_ref[rows, (hd // 2) * LANES:(hd // 2 + 1) * LANES],
                                        ck_ref[hd // 2, rows, :]], axis=1)
                lg = jnp.dot(kblk, qts[hd], preferred_element_type=F32)
                if masked:
                    lg = jnp.where(key_row <= q_pos - kb * tk, lg, NEG_INF)
                logits.append(lg)
        new = []
        for hd in range(B_HEADS):
            m_old, acc = state[hd]
            lgs = logits[hd * len(kbs):(hd + 1) * len(kbs)]
            m_new = m_old
            for lg in lgs:
                m_new = jnp.maximum(m_new, _max_keys(lg))
            m_safe = jnp.where(m_new == NEG_INF, 0.0, m_new) if masked else m_new
            acc = jnp.exp2(m_old - m_safe) * acc
            for kb, lg in zip(kbs, lgs):
                p = jnp.exp2(lg - m_safe).astype(BF16)
                acc = acc + jnp.dot(vt_ref[hd, kb], p, preferred_element_type=F32)
            new.append((m_new, acc))
        return tuple(new)

    init = tuple((jnp.full((1, tq), NEG_INF, F32), jnp.zeros((V_ROWS, tq), F32)) for _ in range(B_HEADS))
    state = lax.fori_loop(0, n_full // 2, lambda j, st: step((2 * j, 2 * j + 1), st, False), init)
    state = lax.fori_loop(2 * (n_full // 2), nkb, lambda kb, st: step((kb,), st, True), state)
    outs = [acc[0:HEAD_DIM] / acc[HEAD_DIM:HEAD_DIM + 1] for _, acc in state]
    o_ref[...] = jnp.concatenate(outs, axis=0).T.astype(o_ref.dtype)


def _fox(qt, k, vt, ck, *, tq, n_keys, q_pos0):
    bsz, _, t_q = qt.shape
    _, _, nblk, _, tk = vt.shape
    lpad = nblk * tk
    kern = functools.partial(_fox_kernel, tq=tq, tk=tk, n_keys=n_keys, q_pos0=q_pos0)
    return pl.pallas_call(
        kern,
        grid=(bsz, t_q // tq),
        in_specs=[pl.BlockSpec((None, B_HEADS * LANES, tq), lambda b, i: (b, 0, i)),
                  pl.BlockSpec((None, lpad, 384), lambda b, i: (b, 0, 0)),
                  pl.BlockSpec((None, B_HEADS, nblk, V_ROWS, tk), lambda b, i: (b, 0, 0, 0, 0)),
                  pl.BlockSpec((None, B_HEADS // 2, lpad, LANES), lambda b, i: (b, 0, 0, 0))],
        out_specs=pl.BlockSpec((None, tq, 384), lambda b, i: (b, i, 0)),
        out_shape=jax.ShapeDtypeStruct((bsz, t_q, 384), BF16),
        compiler_params=_cparams(("parallel", "arbitrary")),
        name="fox_attention",
    )(qt, k, vt, ck)


def _pool_kernel(cur_ref, prev_ref, hist_ref, w_ref, s_ref, o_ref, ext, *, tc, start_pos):
    i = pl.program_id(1)
    cur = cur_ref[...]
    ext[0:16, :] = jnp.where(i == 0, hist_ref[...], prev_ref[tc - 16:, :])
    ext[16:, :] = cur
    pos = start_pos + i * tc + lax.broadcasted_iota(I32, (tc, POOL_WIDTH), 0)
    lane = _lane_iota((tc, POOL_WIDTH))
    run = cur
    pooled = jnp.zeros_like(cur)
    k = 1
    for g, w in enumerate(POOL_WINDOWS):
        while k < w:
            run = run + ext[16 - k:16 - k + tc, :]
            k += 1
        cnt = jnp.minimum(pos + 1, w).astype(F32)
        in_group = (lane >= g * POOL_GROUP_DIM) & (lane < (g + 1) * POOL_GROUP_DIM)
        pooled = jnp.where(in_group, run / cnt, pooled)
    z = (pooled - cur).astype(BF16)
    o_ref[...] = (jnp.dot(z, w_ref[...], preferred_element_type=F32) * s_ref[...]).astype(o_ref.dtype)


def _pool(cu, hist16, w_bd, scale, *, tc, start_pos):
    bsz, t, n = cu.shape
    kern = functools.partial(_pool_kernel, tc=tc, start_pos=start_pos)
    return pl.pallas_call(
        kern,
        grid=(bsz, t // tc),
        in_specs=[pl.BlockSpec((None, tc, n), lambda b, i: (b, i, 0)),
                  pl.BlockSpec((None, tc, n), lambda b, i: (b, jnp.maximum(i - 1, 0), 0)),
                  pl.BlockSpec((None, 16, n), lambda b, i: (b, 0, 0)),
                  _const_spec((n, n)), _const_spec((1, n))],
        out_specs=pl.BlockSpec((None, tc, n), lambda b, i: (b, i, 0)),
        out_shape=jax.ShapeDtypeStruct((bsz, t, n), BF16),
        scratch_shapes=[pltpu.VMEM((16 + tc, n), F32)],
        compiler_params=_cparams(("parallel", "arbitrary")),
        name="pool_mixer",
    )(cu, cu, hist16, w_bd, scale.reshape(1, n))


def _route(logits):
    lane = _lane_iota(logits.shape).astype(F32)
    lg = jnp.where(lane < N_EXPERTS, logits, NEG_INF)
    m1 = jnp.max(lg, axis=1, keepdims=True)
    i1 = jnp.min(jnp.where(lg == m1, lane, float(LANES)), axis=1, keepdims=True)
    hot1 = lane == i1
    lg2 = jnp.where(hot1, NEG_INF, lg)
    m2 = jnp.max(lg2, axis=1, keepdims=True)
    i2 = jnp.min(jnp.where(lg2 == m2, lane, float(LANES)), axis=1, keepdims=True)
    hot2 = lane == i2
    e2 = jnp.exp(m2 - m1)
    den = 1.0 + e2
    return jnp.where(hot1, 1.0 / den, 0.0) + jnp.where(hot2, e2 / den, 0.0)


def _merge_kernel(x_ref, oa_ref, ob_ref, oc_ref, gate_ref, g1_ref, sc2_ref, sh2_ref, g_ref,
                  wa_ref, wb_ref, wc_ref, wo_ref, *rest, moe):
    if moe:
        rw_ref, rb_ref, xo_ref, h_ref, gw_ref = rest
    else:
        xo_ref, h_ref = rest
    d = D_MODEL
    merged = (gate_ref[:, 0:d] * jnp.dot(oa_ref[...], wa_ref[...], preferred_element_type=F32)
              + gate_ref[:, d:2 * d] * jnp.dot(ob_ref[...], wb_ref[...], preferred_element_type=F32)
              + gate_ref[:, 2 * d:3 * d] * jnp.dot(oc_ref[...], wc_ref[...], preferred_element_type=F32))
    x = x_ref[...] + g1_ref[...] * jnp.dot(merged.astype(BF16), wo_ref[...], preferred_element_type=F32)
    xo_ref[...] = x
    ms = jnp.mean(x * x, axis=-1, keepdims=True)
    y = x * lax.rsqrt(ms + NORM_EPS) * g_ref[...]
    h = y * (1.0 + sc2_ref[...]) + sh2_ref[...]
    h_ref[...] = h.astype(BF16)
    if moe:
        h_hi = h.astype(BF16)
        h_lo = (h - h_hi.astype(F32)).astype(BF16)
        logits = (jnp.dot(h_hi, rw_ref[0], preferred_element_type=F32)
                  + jnp.dot(h_lo, rw_ref[0], preferred_element_type=F32)
                  + jnp.dot(h_hi, rw_ref[1], preferred_element_type=F32)) + rb_ref[...]
        gw_ref[...] = _route(logits)


def _merge(x, oa, ob, oc, gates, g1, sc2, sh2, g, wa, wb, wc, wo, router, tm):
    bsz, t, d = x.shape
    mrows = g1.shape[1]
    mblk = 1 if mrows == 1 else tm
    mod_spec = pl.BlockSpec((None, mblk, d), (lambda b, i: (b, 0, 0)) if mrows == 1 else (lambda b, i: (b, i, 0)))

    def tok(n):
        return pl.BlockSpec((None, tm, n), lambda b, i: (b, i, 0))

    in_specs = [tok(d), tok(384), tok(384), tok(256), tok(3 * d), mod_spec, mod_spec, mod_spec,
                _const_spec((1, d)), _const_spec(wa.shape), _const_spec(wb.shape), _const_spec(wc.shape),
                _const_spec(wo.shape)]
    args = [x, oa, ob, oc, gates, g1, sc2, sh2, g.reshape(1, d), wa, wb, wc, wo]
    out_specs = [tok(d), tok(d)]
    out_shape = [jax.ShapeDtypeStruct((bsz, t, d), F32), jax.ShapeDtypeStruct((bsz, t, d), BF16)]
    if router is not None:
        rw, rb = router
        in_specs += [_const_spec(rw.shape), _const_spec(rb.shape)]
        args += [rw, rb]
        out_specs.append(tok(LANES))
        out_shape.append(jax.ShapeDtypeStruct((bsz, t, LANES), F32))
    return pl.pallas_call(
        functools.partial(_merge_kernel, moe=router is not None),
        grid=(bsz, t // tm),
        in_specs=in_specs, out_specs=out_specs, out_shape=out_shape,
        compiler_params=_cparams(("parallel", "parallel")),
        name="merge_out",
    )(*args)


def _final_norm(x, gain):
    ms = jnp.mean(x * x, axis=-1, keepdims=True)
    return x * lax.rsqrt(ms + NORM_EPS) * gain


def _ffn_kernel(x_ref, h_ref, g2_ref, wg_ref, wu_ref, wd_ref, *rest, n_chunks, final):
    if final:
        fg_ref, o_ref = rest
    else:
        (o_ref,) = rest
    h = h_ref[...]
    tf = wg_ref.shape[1] // n_chunks
    acc = jnp.zeros(x_ref.shape, F32)
    for c in range(n_chunks):
        gt = jnp.dot(h, wg_ref[:, c * tf:(c + 1) * tf], preferred_element_type=F32)
        up = jnp.dot(h, wu_ref[:, c * tf:(c + 1) * tf], preferred_element_type=F32)
        act = (gt * jax.nn.sigmoid(gt) * up).astype(BF16)
        acc = acc + jnp.dot(act, wd_ref[c * tf:(c + 1) * tf, :], preferred_element_type=F32)
    x = x_ref[...] + g2_ref[...] * acc
    o_ref[...] = _final_norm(x, fg_ref[...]) if final else x


def _ffn(x, h, g2, wg, wu, wd, final_g, tm):
    bsz, t, d = x.shape
    mrows = g2.shape[1]
    mblk = 1 if mrows == 1 else tm
    mod_spec = pl.BlockSpec((None, mblk, d), (lambda b, i: (b, 0, 0)) if mrows == 1 else (lambda b, i: (b, i, 0)))
    tok = pl.BlockSpec((None, tm, d), lambda b, i: (b, i, 0))
    in_specs = [tok, tok, mod_spec, _const_spec(wg.shape), _const_spec(wu.shape), _const_spec(wd.shape)]
    args = [x, h, g2, wg, wu, wd]
    if final_g is not None:
        in_specs.append(_const_spec((1, d)))
        args.append(final_g.reshape(1, d))
    return pl.pallas_call(
        functools.partial(_ffn_kernel, n_chunks=2, final=final_g is not None),
        grid=(bsz, t // tm),
        in_specs=in_specs, out_specs=tok,
        out_shape=jax.ShapeDtypeStruct((bsz, t, d), F32),
        compiler_params=_cparams(("parallel", "parallel")),
        name="ffn_dense",
    )(*args)


def _moe_kernel(x_ref, h_ref, g2_ref, gw_ref, wg_ref, wu_ref, wd_ref, *rest, final):
    if final:
        fg_ref, o_ref, acc_ref, posc_ref, posr_ref = rest
    else:
        o_ref, acc_ref, posc_ref, posr_ref = rest
    e = pl.program_id(2)
    tm = h_ref.shape[0]
    n_slabs = tm // MOE_CHUNK

    @pl.when(e == 0)
    def _():
        acc_ref[...] = jnp.zeros_like(acc_ref)
        routed = gw_ref[...] != 0.0
        r_f = jnp.where(routed, 1.0, 0.0)
        r_b = r_f.astype(BF16)
        r_t = r_f.T
        r_tb = r_t.astype(BF16)
        tok_l = _lane_iota((MOE_CHUNK, tm))
        tok_r = _row_iota((MOE_CHUNK, tm))
        rank_r = jnp.zeros((LANES, tm), F32)
        for s in range(n_slabs):
            rows = slice(s * MOE_CHUNK, (s + 1) * MOE_CHUNK)
            earlier = jnp.where(tok_l < tok_r + s * MOE_CHUNK, 1.0, 0.0).astype(BF16)
            rank_c = jnp.dot(earlier, r_b, preferred_element_type=F32)
            posc_ref[rows, :] = jnp.where(routed[rows], rank_c, -1.0)
            later = jnp.where(tok_r + s * MOE_CHUNK < tok_l, 1.0, 0.0).astype(BF16)
            rank_r = rank_r + jnp.dot(r_tb[:, rows], later, preferred_element_type=F32)
        posr_ref[...] = jnp.where(r_t != 0.0, rank_r, -1.0)

    lane_e = _lane_iota((tm, LANES)) == e
    pos_c = jnp.sum(jnp.where(lane_e, posc_ref[...], 0.0), axis=1, keepdims=True)
    gate_c = jnp.sum(jnp.where(lane_e, gw_ref[...], 0.0), axis=1, keepdims=True)
    pos_r = posr_ref[pl.ds(e, 1), :]
    n_routed = (jnp.max(pos_r) + 1.0).astype(I32)

    def run_chunk(base, n_rows):
        slot_rows = _row_iota((n_rows, tm)).astype(F32)
        slot_lanes = _lane_iota((MOE_CHUNK, n_rows)).astype(F32)
        pack = jnp.where(pos_r - base == slot_rows, 1.0, 0.0).astype(BF16)
        xc = jnp.dot(pack, h_ref[...], preferred_element_type=F32).astype(BF16)
        gt = jnp.dot(xc, wg_ref[...], preferred_element_type=F32)
        up = jnp.dot(xc, wu_ref[...], preferred_element_type=F32)
        act = (gt * jax.nn.sigmoid(gt) * up).astype(BF16)
        y = jnp.dot(act, wd_ref[...], preferred_element_type=F32).astype(BF16)
        for s in range(n_slabs):
            rows = slice(s * MOE_CHUNK, (s + 1) * MOE_CHUNK)
            unpack = jnp.where(pos_c[rows] - base == slot_lanes, 1.0, 0.0).astype(BF16)
            acc_ref[rows, :] += gate_c[rows] * jnp.dot(unpack, y, preferred_element_type=F32)

    def first(c, carry):
        run_chunk(0.0, MOE_CHUNK)
        return carry

    def later(c, carry):
        run_chunk((MOE_CHUNK + c * (MOE_CHUNK // 2)).astype(F32), MOE_CHUNK // 2)
        return carry

    lax.fori_loop(0, jnp.minimum(n_routed, 1), first, 0)
    n_later = (jnp.maximum(n_routed - MOE_CHUNK, 0) + MOE_CHUNK // 2 - 1) // (MOE_CHUNK // 2)
    lax.fori_loop(0, n_later, later, 0)

    @pl.when(e == pl.num_programs(2) - 1)
    def _():
        x = x_ref[...] + g2_ref[...] * acc_ref[...]
        o_ref[...] = _final_norm(x, fg_ref[...]) if final else x


def _moe(x, h, g2, gw, wg, wu, wd, final_g, tm):
    bsz, t, d = x.shape
    n_e, _, dff = wg.shape
    mrows = g2.shape[1]
    mblk = 1 if mrows == 1 else tm
    mod_spec = pl.BlockSpec((None, mblk, d), (lambda b, i, e: (b, 0, 0)) if mrows == 1 else (lambda b, i, e: (b, i, 0)))
    tok = pl.BlockSpec((None, tm, d), lambda b, i, e: (b, i, 0))
    tok_once = pl.BlockSpec((None, tm, d), lambda b, i, e: (b, i, 0), pipeline_mode=pl.Buffered(1))
    in_specs = [tok_once, tok, mod_spec, pl.BlockSpec((None, tm, LANES), lambda b, i, e: (b, i, 0)),
                pl.BlockSpec((None, d, dff), lambda b, i, e: (e, 0, 0)),
                pl.BlockSpec((None, d, dff), lambda b, i, e: (e, 0, 0)),
                pl.BlockSpec((None, dff, d), lambda b, i, e: (e, 0, 0))]
    args = [x, h, g2, gw, wg, wu, wd]
    if final_g is not None:
        in_specs.append(pl.BlockSpec((1, d), lambda b, i, e: (0, 0)))
        args.append(final_g.reshape(1, d))
    return pl.pallas_call(
        functools.partial(_moe_kernel, final=final_g is not None),
        grid=(bsz, t // tm, n_e),
        in_specs=in_specs, out_specs=tok,
        out_shape=jax.ShapeDtypeStruct((bsz, t, d), F32),
        scratch_shapes=[pltpu.VMEM((tm, d), F32), pltpu.VMEM((tm, LANES), F32), pltpu.VMEM((LANES, tm), F32)],
        compiler_params=_cparams(("parallel", "parallel", "arbitrary")),
        name="moe_routed",
    )(*args)


def _rope_tables(pos):
    inv = ROPE_THETA ** (-jnp.arange(HALF, dtype=F32) / HALF)
    ang = pos.astype(F32)[:, None] * inv[None, :]
    cos, sin = jnp.cos(ang), jnp.sin(ang)
    return jnp.tile(cos, (1, 4)), jnp.tile(jnp.concatenate([-sin, sin], axis=1), (1, 2))


def _pick_tile(n, pref):
    t = min(n, pref)
    while n % t:
        t //= 2
    return t


def _per_seq_cols(a, bsz, t, width):
    f = a.shape[1]
    a = jnp.moveaxis(a[0].reshape(f, bsz, t), 1, 0)
    return jnp.pad(a, ((0, 0), (0, 0), (0, width - t)))


def _value_blocks(past_vt, new_vt, bsz, t, lpad):
    lead = new_vt.shape[:-2]
    new_b = jnp.moveaxis(new_vt.reshape(*lead, V_ROWS, bsz, t), -2, 0)
    full = jnp.concatenate([past_vt, new_b], axis=-1)
    full = jnp.pad(full, [(0, 0)] * (full.ndim - 1) + [(0, lpad - full.shape[-1])])
    full = full.reshape(bsz, *lead, V_ROWS, lpad // KV_BLOCK, KV_BLOCK)
    return jnp.moveaxis(full, -2, -3)


def _with_ones_rows(vt):
    ones = jnp.ones(vt.shape[:-2] + (1, vt.shape[-1]), vt.dtype)
    zeros = jnp.zeros(vt.shape[:-2] + (V_ROWS - HEAD_DIM - 1, vt.shape[-1]), vt.dtype)
    return jnp.concatenate([vt, ones, zeros], axis=-2)


def _mixers(inp, past, n_past, lw, bsz, t):
    aqt, iqt, kik, avt, bqt, bk, bvt, cu, sm, smt = inp
    n_keys = n_past + t
    n_sel = min(TOPK_MAX, n_keys // 4)
    lpad = -(-n_keys // (2 * KV_BLOCK)) * 2 * KV_BLOCK
    if past is None:
        kik_all, avt_all, bk_all, bvt_all, logf_all = kik, avt, bk, bvt, sm
        hist16 = jnp.zeros((bsz, 16, POOL_WIDTH), F32)
        tq_a, tq_b, t_pad = _pick_tile(t, 256), _pick_tile(t, 256), t
    else:
        pa, pb, plf, pc = past
        pk, pv, pik = (pa[:, :, j].astype(BF16) for j in range(3))

        def join_rows(p, new):
            full = jnp.concatenate([p, new.reshape(bsz, t, new.shape[-1])], axis=1)
            return jnp.pad(full, ((0, 0), (0, lpad - n_keys), (0, 0)))

        kik_all = join_rows(jnp.concatenate([pk, pik], axis=-1), kik)
        avt_all = _value_blocks(_with_ones_rows(jnp.swapaxes(pv, 1, 2)), avt[0, 0], bsz, t, lpad)
        bk_all = join_rows(pb[:, :, 0].astype(BF16).reshape(bsz, n_past, 384), bk)
        pvt = jnp.transpose(pb[:, :, 1].astype(BF16), (0, 2, 3, 1))
        bvt_all = _value_blocks(_with_ones_rows(pvt), bvt[0, :, 0], bsz, t, lpad)
        logf_all = join_rows(jnp.pad(plf, ((0, 0), (0, 0), (0, LANES - B_HEADS))), sm)
        hist16 = jnp.pad(pc, ((0, 0), (1, 0), (0, 0)))
        tq_a = tq_b = t_pad = LANES
        aqt, iqt, bqt, smt = (_per_seq_cols(a, bsz, t, t_pad) for a in (aqt, iqt, bqt, smt))

    oa = _dsa(aqt, iqt, smt, kik_all, avt_all, tq=tq_a, n_keys=n_keys, q_pos0=n_past, n_sel=n_sel)
    ck = _cum_logf(logf_all, 2 * KV_BLOCK)
    ob = _fox(bqt, bk_all, bvt_all, ck, tq=tq_b, n_keys=n_keys, q_pos0=n_past)
    cu = cu.reshape(bsz, t, POOL_WIDTH)
    oc = _pool(cu, hist16, lw["pool_bd"], lw["pool_scale"], tc=_pick_tile(t, 1024), start_pos=n_past)
    return oa[:, :t], ob[:, :t], oc


def _layer(x, mod, past, n_past, pos_tab, lw, layer, final_g, per_token):
    bsz, t, d = x.shape
    sh1, sc1, g1, sh2, sc2, g2 = mod
    if per_token:
        xt = x.reshape(1, bsz * t, d)
        sh1, sc1, g1, sh2, sc2, g2 = (jnp.broadcast_to(m, (bsz, t, d)).reshape(1, bsz * t, d) for m in mod)
        cos, sin = (jnp.tile(a, (bsz, 1)) for a in pos_tab)
    else:
        xt = x
        cos, sin = pos_tab
    tm = _pick_tile(xt.shape[1], 2 * KV_BLOCK)
    (aqt, iqt, nat, kik, avt, nbt, bqt, bk, bvt, cu, sm, smt, gates) = _in_proj(
        xt, sc1, sh1, lw["norm_mix_g"], lw["w_in"], lw["bf_bias"], cos, sin, tm)
    oa, ob, oc = _mixers((aqt, iqt, kik, avt, bqt, bk, bvt, cu, sm, smt), past, n_past, lw, bsz, t)

    def flat(a):
        return a.reshape(xt.shape[0], xt.shape[1], a.shape[-1])

    router = (lw["router_w"], lw["router_b"]) if layer % 2 else None
    res = _merge(xt, flat(oa), flat(ob), flat(oc), gates, g1, sc2, sh2, lw["norm_ffn_g"],
                 lw["w_br_a"], lw["w_br_b"], lw["w_br_c"], lw["w_out"], router, _pick_tile(xt.shape[1], 512))
    tmf = _pick_tile(xt.shape[1], 512)
    if layer % 2 == 0:
        x_mid, h2 = res
        x_new = _ffn(x_mid, h2, g2, lw["ffn_wg"], lw["ffn_wu"], lw["ffn_wd"], final_g, tmf)
    else:
        x_mid, h2, gw = res
        x_new = _moe(x_mid, h2, g2, gw, lw["moe_wg"], lw["moe_wu"], lw["moe_wd"], final_g,
                     _pick_tile(xt.shape[1], 1024))
    def token_major(a, *feat):
        a = a.reshape(a.shape[0], *feat, -1, t) if per_token else a.reshape(a.shape[0], *feat, 1, t)
        a = jnp.moveaxis(a, (-2, -1), (1, 2))
        return a.reshape(bsz, t, *feat)

    new_a = token_major(nat, 3, HEAD_DIM)
    new_b = token_major(nbt, 2, B_HEADS, HEAD_DIM)
    new_logf = sm.reshape(bsz, t, LANES)[:, :, :B_HEADS]
    new_pool = cu.reshape(bsz, t, POOL_WIDTH)[:, t - POOL_HIST:, :]
    return x_new.reshape(bsz, t, d), (new_a, new_b, new_logf, new_pool)


def kernel(x_prompt, x_sample, cache_a_kvi, cache_b_kv, cache_b_logf, state_c_pool, c_prompt, c_sample,
           ada_w, ada_b, norm_mix_g, w_in, b_forget, pool_w, pool_scale, w_br_a, w_br_b, w_br_c, w_out,
           norm_ffn_g, ffn_w_gate, ffn_w_up, ffn_w_down, moe_router_w, moe_router_b, moe_w_gate,
           moe_w_up, moe_w_down, final_norm_g):
    depth = ada_w.shape[0]
    bp, tp, d = x_prompt.shape
    bs, ts, _ = x_sample.shape
    n_past = cache_a_kvi.shape[2]
    assert tp % KV_BLOCK == 0 and (bs * ts) % KV_BLOCK == 0 and ts <= LANES

    rows = -(-(bp + bs) // 8) * 8
    c_all = jnp.pad(jnp.concatenate([c_prompt, c_sample], axis=0), ((0, rows - bp - bs), (0, 0)))
    mod_all = _ada(c_all, ada_w, ada_b)

    tab_p = _rope_tables(jnp.arange(tp))
    tab_s = _rope_tables(n_past + jnp.arange(ts))

    xp, xs = x_prompt, x_sample
    outs_p, outs_s = [], []
    for layer in range(depth):
        j = layer // 2
        w_l, bias_l = _in_weights(w_in[layer], b_forget[layer])
        pw = pool_w[layer]
        pool_bd = jnp.zeros((POOL_WIDTH, POOL_WIDTH), F32)
        for g in range(len(POOL_WINDOWS)):
            sl = slice(g * POOL_GROUP_DIM, (g + 1) * POOL_GROUP_DIM)
            pool_bd = pool_bd.at[sl, sl].set(pw[g])
        lw = dict(w_in=w_l, bf_bias=bias_l, norm_mix_g=norm_mix_g[layer], norm_ffn_g=norm_ffn_g[layer],
                  pool_bd=pool_bd.astype(BF16), pool_scale=pool_scale[layer],
                  w_br_a=w_br_a[layer].astype(BF16), w_br_b=w_br_b[layer].astype(BF16),
                  w_br_c=w_br_c[layer].astype(BF16), w_out=w_out[layer].astype(BF16))
        if layer % 2 == 0:
            lw.update(ffn_wg=ffn_w_gate[j].astype(BF16), ffn_wu=ffn_w_up[j].astype(BF16),
                      ffn_wd=ffn_w_down[j].astype(BF16))
        else:
            rw = jnp.pad(moe_router_w[j], ((0, 0), (0, LANES - N_EXPERTS)))
            rw_hi = rw.astype(BF16)
            lw.update(router_w=jnp.stack([rw_hi, (rw - rw_hi.astype(F32)).astype(BF16)]),
                      router_b=jnp.pad(moe_router_b[j], (0, LANES - N_EXPERTS)).reshape(1, LANES),
                      moe_wg=moe_w_gate[j].astype(BF16), moe_wu=moe_w_up[j].astype(BF16),
                      moe_wd=moe_w_down[j].astype(BF16))
        final_g = final_norm_g if layer == depth - 1 else None
        mod_p = [m[:, None, :] for m in jnp.split(mod_all[layer, :bp], 6, axis=-1)]
        mod_s = [m[:, None, :] for m in jnp.split(mod_all[layer, bp:bp + bs], 6, axis=-1)]
        xp, new_p = _layer(xp, mod_p, None, 0, tab_p, lw, layer, final_g, per_token=False)
        past = (cache_a_kvi[layer], cache_b_kv[layer], cache_b_logf[layer], state_c_pool[layer])
        xs, new_s = _layer(xs, mod_s, past, n_past, tab_s, lw, layer, final_g, per_token=True)
        outs_p.append(new_p)
        outs_s.append(new_s)

    def stack(outs, k):
        return jnp.stack([o[k] for o in outs])

    return (xp, xs,
            stack(outs_p, 0), stack(outs_p, 1), stack(outs_p, 2), stack(outs_p, 3),
            stack(outs_s, 0), stack(outs_s, 1), stack(outs_s, 2), stack(outs_s, 3))
```

```python
import functools

import jax
import jax.numpy as jnp
import numpy as np
from jax import lax
from jax.experimental import pallas as pl
from jax.experimental.pallas import tpu as pltpu

F32 = jnp.float32
BF16 = jnp.bfloat16
I32 = jnp.int32

D_MODEL = 1024
CHUNK = 64
HEAD_DIM = 64
HALF = HEAD_DIM // 2
ROPE_THETA = 10000.0
NORM_EPS = 1e-6
A_HEADS = 6
IDX_HEADS = 4
TOPK_MAX = 256
B_HEADS = 6
POOL_WINDOWS = (2, 4, 8, 16)
POOL_GROUP_DIM = 64
POOL_WIDTH = 256
POOL_HIST = 15
N_EXPERTS = 8
LANES = 128
SUBLANES = 8
LOG2E = 1.4426950408889634
QK_SCALE = HEAD_DIM ** -0.5 * LOG2E
KV_BLOCK = 256
V_ROWS = HEAD_DIM + 16
MOE_CHUNK = 256
VMEM_LIMIT = 56 * 1024 * 1024
NEG_INF = float("-inf")
KEY_DT = jnp.bfloat16
RANK_ZERO = 0x8000
RANK_NEG_INF = 0x007F
RANK_POS_INF = 0xFF80
IW_LANE = 8

C_AQ, C_IQ, C_A, C_B, C_CU, C_SM, C_GATE, C_END = 0, 384, 640, 896, 2048, 2304, 2432, 5504


def _cparams(sem):
    return pltpu.CompilerParams(dimension_semantics=sem, vmem_limit_bytes=VMEM_LIMIT)


def _const_spec(shape):
    nd = len(shape)
    return pl.BlockSpec(shape, lambda *_: (0,) * nd, pipeline_mode=pl.Buffered(1))


def _lane_iota(shape):
    return lax.broadcasted_iota(I32, shape, len(shape) - 1)


def _row_iota(shape):
    return lax.broadcasted_iota(I32, shape, len(shape) - 2)


def _ada_kernel(c_ref, w_ref, b_ref, o_ref):
    c = c_ref[...]
    s = c * jax.nn.sigmoid(c)
    o_ref[...] = jnp.dot(s, w_ref[...], preferred_element_type=F32,
                         precision=lax.Precision.HIGHEST) + b_ref[...]


def _ada(c_all, ada_w, ada_b):
    depth, d, n = ada_w.shape
    rows = c_all.shape[0]
    tn = 1536
    return pl.pallas_call(
        _ada_kernel,
        grid=(depth, n // tn),
        in_specs=[pl.BlockSpec((rows, d), lambda l, j: (0, 0)),
                  pl.BlockSpec((None, d, tn), lambda l, j: (l, 0, j)),
                  pl.BlockSpec((None, 1, tn), lambda l, j: (l, 0, j))],
        out_specs=pl.BlockSpec((None, rows, tn), lambda l, j: (l, 0, j)),
        out_shape=jax.ShapeDtypeStruct((depth, rows, n), F32),
        compiler_params=_cparams(("arbitrary", "arbitrary")),
        name="ada_mod",
    )(c_all, ada_w, ada_b.reshape(depth, 1, n))


def _in_kernel(x_ref, sc_ref, sh_ref, g_ref, w_ref, bf_ref, cos_ref, sin_ref,
               aqt_ref, iqt_ref, nat_ref, kik_ref, avt_ref, nbt_ref, bqt_ref, bk_ref, bvt_ref,
               cu_ref, sm_ref, smt_ref, gate_ref):
    x = x_ref[...]
    ms = jnp.mean(x * x, axis=-1, keepdims=True)
    y = x * lax.rsqrt(ms + NORM_EPS) * g_ref[...]
    h = (y * (1.0 + sc_ref[...]) + sh_ref[...]).astype(BF16)
    tm = x.shape[0]

    def mm(a, b):
        return jnp.dot(h, w_ref[:, a:b], preferred_element_type=F32)

    cos = cos_ref[...]
    sin = sin_ref[...]
    lane = _lane_iota((tm, LANES))
    low = lane < HEAD_DIM
    first_half = (lane & HALF) == 0

    def rope(z):
        swapped = jnp.where(first_half, pltpu.roll(z, LANES - HALF, 1), pltpu.roll(z, HALF, 1))
        return z * cos + swapped * sin

    zeros64 = jnp.zeros((HEAD_DIM, tm), BF16)
    ones_rows = jnp.where(_row_iota((V_ROWS - HEAD_DIM, tm)) == 0, 1.0, 0.0).astype(BF16)

    def put_heads(ref, zt, p, slot_even, slot_odd):
        for hh, slot in ((0, slot_even), (1, slot_odd)):
            base = (2 * p + hh) * LANES
            ref[base + slot * HEAD_DIM:base + (slot + 1) * HEAD_DIM, :] = zt[hh * HEAD_DIM:(hh + 1) * HEAD_DIM]
            ref[base + (1 - slot) * HEAD_DIM:base + (2 - slot) * HEAD_DIM, :] = zeros64

    z = mm(C_AQ, C_IQ)
    for p in range(3):
        zt = (rope(z[:, p * LANES:(p + 1) * LANES]) * QK_SCALE).T.astype(BF16)
        put_heads(aqt_ref, zt, p, 0, 0)
    z = mm(C_IQ, C_A)
    for p in range(2):
        zt = rope(z[:, p * LANES:(p + 1) * LANES]).T.astype(BF16)
        put_heads(iqt_ref, zt, p, 1, 1)

    z = mm(C_A, C_B)
    kv = z[:, :LANES]
    r0 = jnp.where(low, rope(kv), kv)
    r1 = rope(z[:, LANES:])
    r0t = r0.T
    nat_ref[0:LANES, :] = r0t
    nat_ref[LANES:, :] = r1.T[0:HEAD_DIM, :]
    kik_ref[...] = jnp.where(low, r0, pltpu.roll(r1, HEAD_DIM, 1)).astype(BF16)
    avt_ref[0:HEAD_DIM, :] = r0t[HEAD_DIM:, :].astype(BF16)
    avt_ref[HEAD_DIM:, :] = ones_rows

    z = mm(C_B, C_CU)
    bk_ref[...] = z[:, 384:768].astype(BF16)
    for p in range(3):
        zt = (z[:, p * LANES:(p + 1) * LANES] * QK_SCALE).T.astype(BF16)
        put_heads(bqt_ref, zt, p, 0, 1)
        nbt_ref[p * LANES:(p + 1) * LANES, :] = z[:, 384 + p * LANES:384 + (p + 1) * LANES].T
        vt = z[:, 768 + p * LANES:768 + (p + 1) * LANES].T
        nbt_ref[384 + p * LANES:384 + (p + 1) * LANES, :] = vt
        vt = vt.astype(BF16)
        for hh in range(2):
            bvt_ref[2 * p + hh, 0:HEAD_DIM, :] = vt[hh * HEAD_DIM:(hh + 1) * HEAD_DIM]
            bvt_ref[2 * p + hh, HEAD_DIM:, :] = ones_rows

    cu_ref[...] = mm(C_CU, C_SM)

    z = mm(C_SM, C_GATE)
    t = z + bf_ref[...]
    logf = jnp.minimum(t, 0.0) - jnp.log1p(jnp.exp(-jnp.abs(t)))
    sm = jnp.where(lane < B_HEADS, logf, z)
    sm_ref[...] = sm
    smt_ref[...] = sm.T[0:16, :]

    for c in range(3):
        gate_ref[:, c * D_MODEL:(c + 1) * D_MODEL] = jax.nn.sigmoid(
            mm(C_GATE + c * D_MODEL, C_GATE + (c + 1) * D_MODEL)).astype(BF16)


def _in_weights(w_in_l, b_forget_l):
    d = w_in_l.shape[0]
    sizes = (384, 64, 64, 256, 4, 64, 384, 384, 384, 6, 256, 3072)
    o = np.concatenate([[0], np.cumsum(sizes)])
    w_in_l = w_in_l.astype(BF16)
    cols = [w_in_l[:, o[0]:o[1]],
            w_in_l[:, o[3]:o[4]],
            w_in_l[:, o[1]:o[3]], w_in_l[:, o[5]:o[6]], jnp.zeros((d, 64), BF16),
            w_in_l[:, o[6]:o[9]],
            w_in_l[:, o[10]:o[11]],
            w_in_l[:, o[9]:o[10]], jnp.zeros((d, IW_LANE - B_HEADS), BF16), w_in_l[:, o[4]:o[5]],
            jnp.zeros((d, LANES - IW_LANE - IDX_HEADS), BF16),
            w_in_l[:, o[11]:o[12]]]
    w = jnp.concatenate(cols, axis=1)
    assert w.shape[1] == C_END, w.shape
    bias = jnp.concatenate([b_forget_l, jnp.zeros((LANES - B_HEADS,), F32)]).reshape(1, LANES)
    return w, bias


def _in_proj(x, sc, sh, g, w, bias, cos, sin, tm):
    bsz, t, d = x.shape
    mrows = sc.shape[1]
    mblk = 1 if mrows == 1 else tm
    mod_spec = pl.BlockSpec((None, mblk, d), (lambda b, i: (b, 0, 0)) if mrows == 1 else (lambda b, i: (b, i, 0)))
    nblk = t // tm

    def rows(n, dt):
        return pl.BlockSpec((None, tm, n), lambda b, i: (b, i, 0)), jax.ShapeDtypeStruct((bsz, t, n), dt)

    def cols(n, dt):
        return pl.BlockSpec((None, n, tm), lambda b, i: (b, 0, i)), jax.ShapeDtypeStruct((bsz, n, t), dt)

    outs = [cols(A_HEADS * LANES, BF16), cols(IDX_HEADS * LANES, BF16), cols(192, F32), rows(LANES, BF16),
            (pl.BlockSpec((None, None, V_ROWS, tm), lambda b, i: (b, i, 0, 0)),
             jax.ShapeDtypeStruct((bsz, nblk, V_ROWS, tm), BF16)),
            cols(768, F32), cols(B_HEADS * LANES, BF16), rows(384, BF16),
            (pl.BlockSpec((None, B_HEADS, None, V_ROWS, tm), lambda b, i: (b, 0, i, 0, 0)),
             jax.ShapeDtypeStruct((bsz, B_HEADS, nblk, V_ROWS, tm), BF16)),
            rows(256, F32), rows(LANES, F32), cols(16, F32), rows(3 * D_MODEL, BF16)]
    return pl.pallas_call(
        _in_kernel,
        grid=(bsz, nblk),
        in_specs=[pl.BlockSpec((None, tm, d), lambda b, i: (b, i, 0)), mod_spec, mod_spec,
                  _const_spec((1, d)), _const_spec(w.shape), _const_spec((1, LANES)),
                  pl.BlockSpec((tm, LANES), lambda b, i: (i, 0)),
                  pl.BlockSpec((tm, LANES), lambda b, i: (i, 0))],
        out_specs=[o[0] for o in outs],
        out_shape=[o[1] for o in outs],
        compiler_params=_cparams(("parallel", "parallel")),
        name="in_proj",
    )(x, sc, sh, g.reshape(1, d), w, bias, cos, sin)


def _cum_kernel(x_ref, o_ref, carry_ref):
    @pl.when(pl.program_id(1) == 0)
    def _():
        carry_ref[...] = jnp.zeros_like(carry_ref)

    tc = x_ref.shape[0]
    tri = jnp.where(_lane_iota((tc, tc)) <= _row_iota((tc, tc)), 1.0, 0.0).astype(BF16)
    cum = carry_ref[0:1, :]
    rest = x_ref[...]
    for _ in range(3):
        piece = rest.astype(BF16)
        cum = cum + jnp.dot(tri, piece, preferred_element_type=F32)
        rest = rest - piece.astype(F32)
    carry_ref[...] = jnp.broadcast_to(cum[tc - 1:tc, :], carry_ref.shape)
    pieces = []
    rest = cum * LOG2E
    for _ in range(3):
        piece = rest.astype(BF16)
        pieces.append(piece.astype(F32))
        rest = rest - pieces[-1]
    lane = _lane_iota((tc, LANES))
    for p in range(B_HEADS // 2):
        slab = jnp.zeros((tc, LANES), F32)
        for hh in range(2):
            for j, piece in enumerate(pieces):
                dst, src = 3 * hh + j, 2 * p + hh
                slab = jnp.where(lane == dst, pltpu.roll(piece, (dst - src) % LANES, 1), slab)
        o_ref[p] = slab.astype(BF16)


def _cum_logf(x, tc):
    bsz, t, n = x.shape
    return pl.pallas_call(
        _cum_kernel,
        grid=(bsz, t // tc),
        in_specs=[pl.BlockSpec((None, tc, n), lambda b, i: (b, i, 0))],
        out_specs=pl.BlockSpec((None, B_HEADS // 2, tc, LANES), lambda b, i: (b, 0, i, 0)),
        out_shape=jax.ShapeDtypeStruct((bsz, B_HEADS // 2, t, LANES), BF16),
        scratch_shapes=[pltpu.VMEM((SUBLANES, LANES), F32)],
        compiler_params=_cparams(("parallel", "arbitrary")),
        name="logf_cumsum",
    )(x)


def _sum_keys(x):
    part = x.reshape(x.shape[0] // SUBLANES, SUBLANES, x.shape[1]).sum(axis=0)
    return jnp.sum(part, axis=0, keepdims=True)


def _max_keys(x):
    part = x.reshape(x.shape[0] // SUBLANES, SUBLANES, x.shape[1]).max(axis=0)
    return jnp.max(part, axis=0, keepdims=True)


def _dsa_kernel(aqt_ref, iqt_ref, smt_ref, kik_ref, avt_ref, o_ref, key_ref,
                *, tq, tk, n_keys, q_pos0, n_sel):
    i = pl.program_id(1)
    pos_first = q_pos0 + i * tq
    last_chunk = (pos_first + tq - 1) // CHUNK
    n_adm = jnp.minimum((last_chunk + 1) * CHUNK, n_keys)
    nkb = (n_adm + tk - 1) // tk

    q_pos = pos_first + _lane_iota((1, tq))
    q_lim = jnp.minimum((q_pos // CHUNK + 1) * CHUNK, n_keys)
    key_row = _row_iota((tk, tq))

    def keys(kb):
        return kik_ref[pl.ds(pl.multiple_of(kb * tk, tk), tk), :]

    iq4 = jnp.concatenate([iqt_ref[hd * LANES:(hd + 1) * LANES, :] for hd in range(IDX_HEADS)], axis=1)
    smt = smt_ref[...]
    w_rows = [smt[IW_LANE + hd:IW_LANE + hd + 1, :] for hd in range(IDX_HEADS)]

    def score_body(kb, carry):
        s4 = jnp.dot(keys(kb), iq4, preferred_element_type=F32)
        score = w_rows[0] * jnp.maximum(s4[:, 0:tq], 0.0)
        for hd in range(1, IDX_HEADS):
            score = score + w_rows[hd] * jnp.maximum(s4[:, hd * tq:(hd + 1) * tq], 0.0)
        score = jnp.where(key_row < q_lim - kb * tk, score, NEG_INF)
        key_ref[kb] = score.astype(KEY_DT)
        return carry

    lax.fori_loop(0, nkb, score_body, 0)

    one, zero = jnp.ones((), KEY_DT), jnp.zeros((), KEY_DT)
    packed_rows = 2 * SUBLANES

    def count(cand, strict):
        def hits(blk):
            h = jnp.where((blk > cand) if strict else (blk >= cand), one, zero)
            parts = [h[r * packed_rows:(r + 1) * packed_rows] for r in range(tk // packed_rows)]
            while len(parts) > 1:
                parts = [a + b for a, b in zip(parts[::2], parts[1::2])]
            return parts[0].astype(F32)

        def pair(j, acc):
            return acc + hits(key_ref[2 * j]) + hits(key_ref[2 * j + 1])

        acc = lax.fori_loop(0, nkb // 2, pair, jnp.zeros((packed_rows, tq), F32))
        acc = lax.fori_loop(2 * (nkb // 2), nkb, lambda kb, a: a + hits(key_ref[kb]), acc)
        return jnp.sum(acc, axis=0, keepdims=True)

    def pattern_value(u):
        bits = jnp.where(u >= RANK_ZERO, u - RANK_ZERO, (~u) & 0xFFFF)
        return lax.bitcast_convert_type(lax.shift_left(bits, 16), F32).astype(KEY_DT)

    def bit_body(b, u):
        cand_u = u | lax.shift_left(jnp.int32(1), 15 - b)
        cnt = count(pattern_value(cand_u), False)
        return jnp.where(cnt >= n_sel, cand_u, u)

    u_thr = jnp.maximum(lax.fori_loop(0, 16, bit_body, jnp.zeros((1, tq), I32)), RANK_NEG_INF)
    thr_16 = pattern_value(u_thr)

    def finer():
        lo = thr_16.astype(F32)
        hi = jnp.where(u_thr >= RANK_POS_INF, jnp.inf, pattern_value(u_thr + 1).astype(F32))

        def halve(_, lo_hi):
            lo, hi = lo_hi
            cand = (0.5 * lo + 0.5 * hi).astype(KEY_DT)
            enough = count(cand, False) >= n_sel
            return jnp.where(enough, cand.astype(F32), lo), jnp.where(enough, hi, cand.astype(F32))

        thr_f = lax.fori_loop(0, 16, halve, (lo, hi))[0].astype(KEY_DT)
        return thr_f, count(thr_f, True)

    above_16 = count(thr_16, True)
    thr_k, above = lax.cond(jnp.min(n_sel - above_16) <= 0.0, finer, lambda: (thr_16, above_16))
    need = jnp.maximum(n_sel - above, 0.0)
    thr = thr_k.astype(F32)

    aq6 = jnp.concatenate([aqt_ref[hd * LANES:(hd + 1) * LANES, :] for hd in range(A_HEADS)], axis=1)

    half = tk // 2
    lower = jnp.where(_lane_iota((half, half)) <= _row_iota((half, half)), 1.0, 0.0).astype(BF16)

    def attend():
        def body(kb, carry):
            eq_seen, ms, accs = carry
            blk = key_ref[kb].astype(F32)
            eq = blk == thr
            eq_f = jnp.where(eq, 1.0, 0.0)
            prefs = []
            for e in (eq_f[:half], eq_f[half:]):
                prefs.append(jnp.dot(lower, e.astype(BF16), preferred_element_type=F32) + eq_seen)
                eq_seen = eq_seen + _sum_keys(e)
            slack = jnp.where(blk >= thr, need - jnp.where(eq, jnp.concatenate(prefs, axis=0), 0.0), -1.0)
            bias = jnp.where(slack >= 0.0, jnp.where(jnp.abs(blk) < jnp.inf, 0.0, NEG_INF), NEG_INF)
            logits = jnp.dot(keys(kb), aq6, preferred_element_type=F32)
            vts = (avt_ref[2 * kb], avt_ref[2 * kb + 1])
            new_ms, new_accs = [], []
            for p in range(A_HEADS // 2):
                ps, alphas = [], []
                for hd in (2 * p, 2 * p + 1):
                    lg = logits[:, hd * tq:(hd + 1) * tq] + bias
                    m_old = ms[hd]
                    m_new = jnp.maximum(m_old, _max_keys(lg))
                    m_safe = jnp.where(m_new == NEG_INF, 0.0, m_new)
                    ps.append(jnp.exp2(lg - m_safe).astype(BF16))
                    alphas.append(jnp.exp2(m_old - m_safe))
                    new_ms.append(m_new)
                p2 = jnp.concatenate(ps, axis=1)
                pv = (jnp.dot(vts[0], p2[:tk // 2], preferred_element_type=F32)
                      + jnp.dot(vts[1], p2[tk // 2:], preferred_element_type=F32))
                new_accs.append(jnp.concatenate(alphas, axis=1) * accs[p] + pv)
            return eq_seen, tuple(new_ms), tuple(new_accs)

        init = (jnp.zeros((1, tq), F32),
                tuple(jnp.full((1, tq), NEG_INF, F32) for _ in range(A_HEADS)),
                tuple(jnp.zeros((V_ROWS, 2 * tq), F32) for _ in range(A_HEADS // 2)))
        return lax.fori_loop(0, nkb, body, init)[2]

    accs = attend()

    outs = []
    for p in range(A_HEADS // 2):
        o2 = accs[p][0:HEAD_DIM] / accs[p][HEAD_DIM:HEAD_DIM + 1]
        outs += [o2[:, 0:tq], o2[:, tq:2 * tq]]
    o_ref[...] = jnp.concatenate(outs, axis=0).T.astype(o_ref.dtype)


def _dsa(aqt, iqt, smt, kik, avt, *, tq, n_keys, q_pos0, n_sel):
    bsz, _, t_q = aqt.shape
    _, nblk, _, tkv = avt.shape
    assert nblk % 2 == 0
    tk = 2 * tkv
    kern = functools.partial(_dsa_kernel, tq=tq, tk=tk, n_keys=n_keys, q_pos0=q_pos0, n_sel=n_sel)
    return pl.pallas_call(
        kern,
        grid=(bsz, t_q // tq),
        in_specs=[pl.BlockSpec((None, A_HEADS * LANES, tq), lambda b, i: (b, 0, i)),
                  pl.BlockSpec((None, IDX_HEADS * LANES, tq), lambda b, i: (b, 0, i)),
                  pl.BlockSpec((None, 16, tq), lambda b, i: (b, 0, i)),
                  pl.BlockSpec((None, nblk * tkv, LANES), lambda b, i: (b, 0, 0)),
                  pl.BlockSpec((None, nblk, V_ROWS, tkv), lambda b, i: (b, 0, 0, 0))],
        out_specs=pl.BlockSpec((None, tq, 384), lambda b, i: (b, i, 0)),
        out_shape=jax.ShapeDtypeStruct((bsz, t_q, 384), BF16),
        scratch_shapes=[pltpu.VMEM((nblk // 2, tk, tq), KEY_DT)],
        compiler_params=_cparams(("parallel", "arbitrary")),
        name="dsa_attention",
    )(aqt, iqt, smt, kik, avt)


def _fox_kernel(qt_ref, k_ref, vt_ref, ck_ref, o_ref, *, tq, tk, n_keys, q_pos0):
    i = pl.program_id(1)
    pos_first = q_pos0 + i * tq
    n_full = pos_first // tk
    nkb = (jnp.minimum(pos_first + tq, n_keys) + tk - 1) // tk
    q_pos = pos_first + _lane_iota((1, tq))
    key_row = _row_iota((tk, tq))
    piece_row = _row_iota((LANES, tq))
    qts = []
    for hd in range(B_HEADS):
        minus = jnp.where((piece_row >= 3 * (hd % 2)) & (piece_row < 3 * (hd % 2) + 3), -1.0, 0.0).astype(BF16)
        qts.append(jnp.concatenate([qt_ref[hd * LANES:(hd + 1) * LANES, :], minus], axis=0))

    def step(kbs, state, masked):
        logits = []
        for hd in range(B_HEADS):
            for kb in kbs:
                rows = pl.ds(pl.multiple_of(kb * tk, tk), tk)
                kblk = jnp.concatenate([k_ref[rows, (hd // 2) * LANES:(hd // 2 + 1) * LANES],
                                        ck_ref[hd // 2, rows, :]], axis=1)
                lg = jnp.dot(kblk, qts[hd], preferred_element_type=F32)
                if masked:
                    lg = jnp.where(key_row <= q_pos - kb * tk, lg, NEG_INF)
                logits.append(lg)
        new = []
        for hd in range(B_HEADS):
            m_old, acc = state[hd]
            lgs = logits[hd * len(kbs):(hd + 1) * len(kbs)]
            m_new = m_old
            for lg in lgs:
                m_new = jnp.maximum(m_new, _max_keys(lg))
            m_safe = jnp.where(m_new == NEG_INF, 0.0, m_new) if masked else m_new
            acc = jnp.exp2(m_old - m_safe) * acc
            for kb, lg in zip(kbs, lgs):
                p = jnp.exp2(lg - m_safe).astype(BF16)
                acc = acc + jnp.dot(vt_ref[hd, kb], p, preferred_element_type=F32)
            new.append((m_new, acc))
        return tuple(new)

    init = tuple((jnp.full((1, tq), NEG_INF, F32), jnp.zeros((V_ROWS, tq), F32)) for _ in range(B_HEADS))
    state = lax.fori_loop(0, n_full // 2, lambda j, st: step((2 * j, 2 * j + 1), st, False), init)
    state = lax.fori_loop(2 * (n_full // 2), nkb, lambda kb, st: step((kb,), st, True), state)
    outs = [acc[0:HEAD_DIM] / acc[HEAD_DIM:HEAD_DIM + 1] for _, acc in state]
    o_ref[...] = jnp.concatenate(outs, axis=0).T.astype(o_ref.dtype)


def _fox(qt, k, vt, ck, *, tq, n_keys, q_pos0):
    bsz, _, t_q = qt.shape
    _, _, nblk, _, tk = vt.shape
    lpad = nblk * tk
    kern = functools.partial(_fox_kernel, tq=tq, tk=tk, n_keys=n_keys, q_pos0=q_pos0)
    return pl.pallas_call(
        kern,
        grid=(bsz, t_q // tq),
        in_specs=[pl.BlockSpec((None, B_HEADS * LANES, tq), lambda b, i: (b, 0, i)),
                  pl.BlockSpec((None, lpad, 384), lambda b, i: (b, 0, 0)),
                  pl.BlockSpec((None, B_HEADS, nblk, V_ROWS, tk), lambda b, i: (b, 0, 0, 0, 0)),
                  pl.BlockSpec((None, B_HEADS // 2, lpad, LANES), lambda b, i: (b, 0, 0, 0))],
        out_specs=pl.BlockSpec((None, tq, 384), lambda b, i: (b, i, 0)),
        out_shape=jax.ShapeDtypeStruct((bsz, t_q, 384), BF16),
        compiler_params=_cparams(("parallel", "arbitrary")),
        name="fox_attention",
    )(qt, k, vt, ck)


def _pool_kernel(cur_ref, prev_ref, hist_ref, w_ref, s_ref, o_ref, ext, *, tc, start_pos):
    i = pl.program_id(1)
    cur = cur_ref[...]
    ext[0:16, :] = jnp.where(i == 0, hist_ref[...], prev_ref[tc - 16:, :])
    ext[16:, :] = cur
    pos = start_pos + i * tc + lax.broadcasted_iota(I32, (tc, POOL_WIDTH), 0)
    lane = _lane_iota((tc, POOL_WIDTH))
    run = cur
    pooled = jnp.zeros_like(cur)
    k = 1
    for g, w in enumerate(POOL_WINDOWS):
        while k < w:
            run = run + ext[16 - k:16 - k + tc, :]
            k += 1
        cnt = jnp.minimum(pos + 1, w).astype(F32)
        in_group = (lane >= g * POOL_GROUP_DIM) & (lane < (g + 1) * POOL_GROUP_DIM)
        pooled = jnp.where(in_group, run / cnt, pooled)
    z = (pooled - cur).astype(BF16)
    o_ref[...] = (jnp.dot(z, w_ref[...], preferred_element_type=F32) * s_ref[...]).astype(o_ref.dtype)


def _pool(cu, hist16, w_bd, scale, *, tc, start_pos):
    bsz, t, n = cu.shape
    kern = functools.partial(_pool_kernel, tc=tc, start_pos=start_pos)
    return pl.pallas_call(
        kern,
        grid=(bsz, t // tc),
        in_specs=[pl.BlockSpec((None, tc, n), lambda b, i: (b, i, 0)),
                  pl.BlockSpec((None, tc, n), lambda b, i: (b, jnp.maximum(i - 1, 0), 0)),
                  pl.BlockSpec((None, 16, n), lambda b, i: (b, 0, 0)),
                  _const_spec((n, n)), _const_spec((1, n))],
        out_specs=pl.BlockSpec((None, tc, n), lambda b, i: (b, i, 0)),
        out_shape=jax.ShapeDtypeStruct((bsz, t, n), BF16),
        scratch_shapes=[pltpu.VMEM((16 + tc, n), F32)],
        compiler_params=_cparams(("parallel", "arbitrary")),
        name="pool_mixer",
    )(cu, cu, hist16, w_bd, scale.reshape(1, n))


def _route(logits):
    lane = _lane_iota(logits.shape).astype(F32)
    lg = jnp.where(lane < N_EXPERTS, logits, NEG_INF)
    m1 = jnp.max(lg, axis=1, keepdims=True)
    i1 = jnp.min(jnp.where(lg == m1, lane, float(LANES)), axis=1, keepdims=True)
    hot1 = lane == i1
    lg2 = jnp.where(hot1, NEG_INF, lg)
    m2 = jnp.max(lg2, axis=1, keepdims=True)
    i2 = jnp.min(jnp.where(lg2 == m2, lane, float(LANES)), axis=1, keepdims=True)
    hot2 = lane == i2
    e2 = jnp.exp(m2 - m1)
    den = 1.0 + e2
    return jnp.where(hot1, 1.0 / den, 0.0) + jnp.where(hot2, e2 / den, 0.0)


def _merge_kernel(x_ref, oa_ref, ob_ref, oc_ref, gate_ref, g1_ref, sc2_ref, sh2_ref, g_ref,
                  wa_ref, wb_ref, wc_ref, wo_ref, *rest, moe):
    if moe:
        rw_ref, rb_ref, xo_ref, h_ref, gw_ref = rest
    else:
        xo_ref, h_ref = rest
    d = D_MODEL
    merged = (gate_ref[:, 0:d] * jnp.dot(oa_ref[...], wa_ref[...], preferred_element_type=F32)
              + gate_ref[:, d:2 * d] * jnp.dot(ob_ref[...], wb_ref[...], preferred_element_type=F32)
              + gate_ref[:, 2 * d:3 * d] * jnp.dot(oc_ref[...], wc_ref[...], preferred_element_type=F32))
    x = x_ref[...] + g1_ref[...] * jnp.dot(merged.astype(BF16), wo_ref[...], preferred_element_type=F32)
    xo_ref[...] = x
    ms = jnp.mean(x * x, axis=-1, keepdims=True)
    y = x * lax.rsqrt(ms + NORM_EPS) * g_ref[...]
    h = y * (1.0 + sc2_ref[...]) + sh2_ref[...]
    h_ref[...] = h.astype(BF16)
    if moe:
        h_hi = h.astype(BF16)
        h_lo = (h - h_hi.astype(F32)).astype(BF16)
        logits = (jnp.dot(h_hi, rw_ref[0], preferred_element_type=F32)
                  + jnp.dot(h_lo, rw_ref[0], preferred_element_type=F32)
                  + jnp.dot(h_hi, rw_ref[1], preferred_element_type=F32)) + rb_ref[...]
        gw_ref[...] = _route(logits)


def _merge(x, oa, ob, oc, gates, g1, sc2, sh2, g, wa, wb, wc, wo, router, tm):
    bsz, t, d = x.shape
    mrows = g1.shape[1]
    mblk = 1 if mrows == 1 else tm
    mod_spec = pl.BlockSpec((None, mblk, d), (lambda b, i: (b, 0, 0)) if mrows == 1 else (lambda b, i: (b, i, 0)))

    def tok(n):
        return pl.BlockSpec((None, tm, n), lambda b, i: (b, i, 0))

    in_specs = [tok(d), tok(384), tok(384), tok(256), tok(3 * d), mod_spec, mod_spec, mod_spec,
                _const_spec((1, d)), _const_spec(wa.shape), _const_spec(wb.shape), _const_spec(wc.shape),
                _const_spec(wo.shape)]
    args = [x, oa, ob, oc, gates, g1, sc2, sh2, g.reshape(1, d), wa, wb, wc, wo]
    out_specs = [tok(d), tok(d)]
    out_shape = [jax.ShapeDtypeStruct((bsz, t, d), F32), jax.ShapeDtypeStruct((bsz, t, d), BF16)]
    if router is not None:
        rw, rb = router
        in_specs += [_const_spec(rw.shape), _const_spec(rb.shape)]
        args += [rw, rb]
        out_specs.append(tok(LANES))
        out_shape.append(jax.ShapeDtypeStruct((bsz, t, LANES), F32))
    return pl.pallas_call(
        functools.partial(_merge_kernel, moe=router is not None),
        grid=(bsz, t // tm),
        in_specs=in_specs, out_specs=out_specs, out_shape=out_shape,
        compiler_params=_cparams(("parallel", "parallel")),
        name="merge_out",
    )(*args)


def _final_norm(x, gain):
    ms = jnp.mean(x * x, axis=-1, keepdims=True)
    return x * lax.rsqrt(ms + NORM_EPS) * gain


def _ffn_kernel(x_ref, h_ref, g2_ref, wg_ref, wu_ref, wd_ref, *rest, n_chunks, final):
    if final:
        fg_ref, o_ref = rest
    else:
        (o_ref,) = rest
    h = h_ref[...]
    tf = wg_ref.shape[1] // n_chunks
    acc = jnp.zeros(x_ref.shape, F32)
    for c in range(n_chunks):
        gt = jnp.dot(h, wg_ref[:, c * tf:(c + 1) * tf], preferred_element_type=F32)
        up = jnp.dot(h, wu_ref[:, c * tf:(c + 1) * tf], preferred_element_type=F32)
        act = (gt * jax.nn.sigmoid(gt) * up).astype(BF16)
        acc = acc + jnp.dot(act, wd_ref[c * tf:(c + 1) * tf, :], preferred_element_type=F32)
    x = x_ref[...] + g2_ref[...] * acc
    o_ref[...] = _final_norm(x, fg_ref[...]) if final else x


def _ffn(x, h, g2, wg, wu, wd, final_g, tm):
    bsz, t, d = x.shape
    mrows = g2.shape[1]
    mblk = 1 if mrows == 1 else tm
    mod_spec = pl.BlockSpec((None, mblk, d), (lambda b, i: (b, 0, 0)) if mrows == 1 else (lambda b, i: (b, i, 0)))
    tok = pl.BlockSpec((None, tm, d), lambda b, i: (b, i, 0))
    in_specs = [tok, tok, mod_spec, _const_spec(wg.shape), _const_spec(wu.shape), _const_spec(wd.shape)]
    args = [x, h, g2, wg, wu, wd]
    if final_g is not None:
        in_specs.append(_const_spec((1, d)))
        args.append(final_g.reshape(1, d))
    return pl.pallas_call(
        functools.partial(_ffn_kernel, n_chunks=2, final=final_g is not None),
        grid=(bsz, t // tm),
        in_specs=in_specs, out_specs=tok,
        out_shape=jax.ShapeDtypeStruct((bsz, t, d), F32),
        compiler_params=_cparams(("parallel", "parallel")),
        name="ffn_dense",
    )(*args)


def _moe_kernel(x_ref, h_ref, g2_ref, gw_ref, wg_ref, wu_ref, wd_ref, *rest, final):
    if final:
        fg_ref, o_ref, acc_ref, posc_ref, posr_ref = rest
    else:
        o_ref, acc_ref, posc_ref, posr_ref = rest
    e = pl.program_id(2)
    tm = h_ref.shape[0]
    n_slabs = tm // MOE_CHUNK

    @pl.when(e == 0)
    def _():
        acc_ref[...] = jnp.zeros_like(acc_ref)
        routed = gw_ref[...] != 0.0
        r_f = jnp.where(routed, 1.0, 0.0)
        r_b = r_f.astype(BF16)
        r_t = r_f.T
        r_tb = r_t.astype(BF16)
        tok_l = _lane_iota((MOE_CHUNK, tm))
        tok_r = _row_iota((MOE_CHUNK, tm))
        rank_r = jnp.zeros((LANES, tm), F32)
        for s in range(n_slabs):
            rows = slice(s * MOE_CHUNK, (s + 1) * MOE_CHUNK)
            earlier = jnp.where(tok_l < tok_r + s * MOE_CHUNK, 1.0, 0.0).astype(BF16)
            rank_c = jnp.dot(earlier, r_b, preferred_element_type=F32)
            posc_ref[rows, :] = jnp.where(routed[rows], rank_c, -1.0)
            later = jnp.where(tok_r + s * MOE_CHUNK < tok_l, 1.0, 0.0).astype(BF16)
            rank_r = rank_r + jnp.dot(r_tb[:, rows], later, preferred_element_type=F32)
        posr_ref[...] = jnp.where(r_t != 0.0, rank_r, -1.0)

    lane_e = _lane_iota((tm, LANES)) == e
    pos_c = jnp.sum(jnp.where(lane_e, posc_ref[...], 0.0), axis=1, keepdims=True)
    gate_c = jnp.sum(jnp.where(lane_e, gw_ref[...], 0.0), axis=1, keepdims=True)
    pos_r = posr_ref[pl.ds(e, 1), :]
    n_routed = (jnp.max(pos_r) + 1.0).astype(I32)

    def run_chunk(base, n_rows):
        slot_rows = _row_iota((n_rows, tm)).astype(F32)
        slot_lanes = _lane_iota((MOE_CHUNK, n_rows)).astype(F32)
        pack = jnp.where(pos_r - base == slot_rows, 1.0, 0.0).astype(BF16)
        xc = jnp.dot(pack, h_ref[...], preferred_element_type=F32).astype(BF16)
        gt = jnp.dot(xc, wg_ref[...], preferred_element_type=F32)
        up = jnp.dot(xc, wu_ref[...], preferred_element_type=F32)
        act = (gt * jax.nn.sigmoid(gt) * up).astype(BF16)
        y = jnp.dot(act, wd_ref[...], preferred_element_type=F32).astype(BF16)
        for s in range(n_slabs):
            rows = slice(s * MOE_CHUNK, (s + 1) * MOE_CHUNK)
            unpack = jnp.where(pos_c[rows] - base == slot_lanes, 1.0, 0.0).astype(BF16)
            acc_ref[rows, :] += gate_c[rows] * jnp.dot(unpack, y, preferred_element_type=F32)

    def first(c, carry):
        run_chunk(0.0, MOE_CHUNK)
        return carry

    def later(c, carry):
        run_chunk((MOE_CHUNK + c * (MOE_CHUNK // 2)).astype(F32), MOE_CHUNK // 2)
        return carry

    lax.fori_loop(0, jnp.minimum(n_routed, 1), first, 0)
    n_later = (jnp.maximum(n_routed - MOE_CHUNK, 0) + MOE_CHUNK // 2 - 1) // (MOE_CHUNK // 2)
    lax.fori_loop(0, n_later, later, 0)

    @pl.when(e == pl.num_programs(2) - 1)
    def _():
        x = x_ref[...] + g2_ref[...] * acc_ref[...]
        o_ref[...] = _final_norm(x, fg_ref[...]) if final else x


def _moe(x, h, g2, gw, wg, wu, wd, final_g, tm):
    bsz, t, d = x.shape
    n_e, _, dff = wg.shape
    mrows = g2.shape[1]
    mblk = 1 if mrows == 1 else tm
    mod_spec = pl.BlockSpec((None, mblk, d), (lambda b, i, e: (b, 0, 0)) if mrows == 1 else (lambda b, i, e: (b, i, 0)))
    tok = pl.BlockSpec((None, tm, d), lambda b, i, e: (b, i, 0))
    tok_once = pl.BlockSpec((None, tm, d), lambda b, i, e: (b, i, 0), pipeline_mode=pl.Buffered(1))
    in_specs = [tok_once, tok, mod_spec, pl.BlockSpec((None, tm, LANES), lambda b, i, e: (b, i, 0)),
                pl.BlockSpec((None, d, dff), lambda b, i, e: (e, 0, 0)),
                pl.BlockSpec((None, d, dff), lambda b, i, e: (e, 0, 0)),
                pl.BlockSpec((None, dff, d), lambda b, i, e: (e, 0, 0))]
    args = [x, h, g2, gw, wg, wu, wd]
    if final_g is not None:
        in_specs.append(pl.BlockSpec((1, d), lambda b, i, e: (0, 0)))
        args.append(final_g.reshape(1, d))
    return pl.pallas_call(
        functools.partial(_moe_kernel, final=final_g is not None),
        grid=(bsz, t // tm, n_e),
        in_specs=in_specs, out_specs=tok,
        out_shape=jax.ShapeDtypeStruct((bsz, t, d), F32),
        scratch_shapes=[pltpu.VMEM((tm, d), F32), pltpu.VMEM((tm, LANES), F32), pltpu.VMEM((LANES, tm), F32)],
        compiler_params=_cparams(("parallel", "parallel", "arbitrary")),
        name="moe_routed",
    )(*args)


def _rope_tables(pos):
    inv = ROPE_THETA ** (-jnp.arange(HALF, dtype=F32) / HALF)
    ang = pos.astype(F32)[:, None] * inv[None, :]
    cos, sin = jnp.cos(ang), jnp.sin(ang)
    return jnp.tile(cos, (1, 4)), jnp.tile(jnp.concatenate([-sin, sin], axis=1), (1, 2))


def _pick_tile(n, pref):
    t = min(n, pref)
    while n % t:
        t //= 2
    return t


def _per_seq_cols(a, bsz, t, width):
    f = a.shape[1]
    a = jnp.moveaxis(a[0].reshape(f, bsz, t), 1, 0)
    return jnp.pad(a, ((0, 0), (0, 0), (0, width - t)))


def _value_blocks(past_vt, new_vt, bsz, t, lpad):
    lead = new_vt.shape[:-2]
    new_b = jnp.moveaxis(new_vt.reshape(*lead, V_ROWS, bsz, t), -2, 0)
    full = jnp.concatenate([past_vt, new_b], axis=-1)
    full = jnp.pad(full, [(0, 0)] * (full.ndim - 1) + [(0, lpad - full.shape[-1])])
    full = full.reshape(bsz, *lead, V_ROWS, lpad // KV_BLOCK, KV_BLOCK)
    return jnp.moveaxis(full, -2, -3)


def _with_ones_rows(vt):
    ones = jnp.ones(vt.shape[:-2] + (1, vt.shape[-1]), vt.dtype)
    zeros = jnp.zeros(vt.shape[:-2] + (V_ROWS - HEAD_DIM - 1, vt.shape[-1]), vt.dtype)
    return jnp.concatenate([vt, ones, zeros], axis=-2)


def _mixers(inp, past, n_past, lw, bsz, t):
    aqt, iqt, kik, avt, bqt, bk, bvt, cu, sm, smt = inp
    n_keys = n_past + t
    n_sel = min(TOPK_MAX, n_keys // 4)
    lpad = -(-n_keys // (2 * KV_BLOCK)) * 2 * KV_BLOCK
    if past is None:
        kik_all, avt_all, bk_all, bvt_all, logf_all = kik, avt, bk, bvt, sm
        hist16 = jnp.zeros((bsz, 16, POOL_WIDTH), F32)
        tq_a, tq_b, t_pad = _pick_tile(t, 512), _pick_tile(t, 256), t
    else:
        pa, pb, plf, pc = past
        pk, pv, pik = (pa[:, :, j].astype(BF16) for j in range(3))

        def join_rows(p, new):
            full = jnp.concatenate([p, new.reshape(bsz, t, new.shape[-1])], axis=1)
            return jnp.pad(full, ((0, 0), (0, lpad - n_keys), (0, 0)))

        kik_all = join_rows(jnp.concatenate([pk, pik], axis=-1), kik)
        avt_all = _value_blocks(_with_ones_rows(jnp.swapaxes(pv, 1, 2)), avt[0, 0], bsz, t, lpad)
        bk_all = join_rows(pb[:, :, 0].astype(BF16).reshape(bsz, n_past, 384), bk)
        pvt = jnp.transpose(pb[:, :, 1].astype(BF16), (0, 2, 3, 1))
        bvt_all = _value_blocks(_with_ones_rows(pvt), bvt[0, :, 0], bsz, t, lpad)
        logf_all = join_rows(jnp.pad(plf, ((0, 0), (0, 0), (0, LANES - B_HEADS))), sm)
        hist16 = jnp.pad(pc, ((0, 0), (1, 0), (0, 0)))
        tq_a = tq_b = t_pad = LANES
        aqt, iqt, bqt, smt = (_per_seq_cols(a, bsz, t, t_pad) for a in (aqt, iqt, bqt, smt))

    oa = _dsa(aqt, iqt, smt, kik_all, avt_all, tq=tq_a, n_keys=n_keys, q_pos0=n_past, n_sel=n_sel)
    ck = _cum_logf(logf_all, 2 * KV_BLOCK)
    ob = _fox(bqt, bk_all, bvt_all, ck, tq=tq_b, n_keys=n_keys, q_pos0=n_past)
    cu = cu.reshape(bsz, t, POOL_WIDTH)
    oc = _pool(cu, hist16, lw["pool_bd"], lw["pool_scale"], tc=_pick_tile(t, 1024), start_pos=n_past)
    return oa[:, :t], ob[:, :t], oc


def _layer(x, mod, past, n_past, pos_tab, lw, layer, final_g, per_token):
    bsz, t, d = x.shape
    sh1, sc1, g1, sh2, sc2, g2 = mod
    if per_token:
        xt = x.reshape(1, bsz * t, d)
        sh1, sc1, g1, sh2, sc2, g2 = (jnp.broadcast_to(m, (bsz, t, d)).reshape(1, bsz * t, d) for m in mod)
        cos, sin = (jnp.tile(a, (bsz, 1)) for a in pos_tab)
    else:
        xt = x
        cos, sin = pos_tab
    tm = KV_BLOCK
    (aqt, iqt, nat, kik, avt, nbt, bqt, bk, bvt, cu, sm, smt, gates) = _in_proj(
        xt, sc1, sh1, lw["norm_mix_g"], lw["w_in"], lw["bf_bias"], cos, sin, tm)
    oa, ob, oc = _mixers((aqt, iqt, kik, avt, bqt, bk, bvt, cu, sm, smt), past, n_past, lw, bsz, t)

    def flat(a):
        return a.reshape(xt.shape[0], xt.shape[1], a.shape[-1])

    router = (lw["router_w"], lw["router_b"]) if layer % 2 else None
    res = _merge(xt, flat(oa), flat(ob), flat(oc), gates, g1, sc2, sh2, lw["norm_ffn_g"],
                 lw["w_br_a"], lw["w_br_b"], lw["w_br_c"], lw["w_out"], router, _pick_tile(xt.shape[1], 512))
    tmf = _pick_tile(xt.shape[1], 512)
    if layer % 2 == 0:
        x_mid, h2 = res
        x_new = _ffn(x_mid, h2, g2, lw["ffn_wg"], lw["ffn_wu"], lw["ffn_wd"], final_g, tmf)
    else:
        x_mid, h2, gw = res
        x_new = _moe(x_mid, h2, g2, gw, lw["moe_wg"], lw["moe_wu"], lw["moe_wd"], final_g,
                     _pick_tile(xt.shape[1], 1024))
    def token_major(a, *feat):
        a = a.reshape(a.shape[0], *feat, -1, t) if per_token else a.reshape(a.shape[0], *feat, 1, t)
        a = jnp.moveaxis(a, (-2, -1), (1, 2))
        return a.reshape(bsz, t, *feat)

    new_a = token_major(nat, 3, HEAD_DIM)
    new_b = token_major(nbt, 2, B_HEADS, HEAD_DIM)
    new_logf = sm.reshape(bsz, t, LANES)[:, :, :B_HEADS]
    new_pool = cu.reshape(bsz, t, POOL_WIDTH)[:, t - POOL_HIST:, :]
    return x_new.reshape(bsz, t, d), (new_a, new_b, new_logf, new_pool)


def kernel(x_prompt, x_sample, cache_a_kvi, cache_b_kv, cache_b_logf, state_c_pool, c_prompt, c_sample,
           ada_w, ada_b, norm_mix_g, w_in, b_forget, pool_w, pool_scale, w_br_a, w_br_b, w_br_c, w_out,
           norm_ffn_g, ffn_w_gate, ffn_w_up, ffn_w_down, moe_router_w, moe_router_b, moe_w_gate,
           moe_w_up, moe_w_down, final_norm_g):
    depth = ada_w.shape[0]
    bp, tp, d = x_prompt.shape
    bs, ts, _ = x_sample.shape
    n_past = cache_a_kvi.shape[2]
    assert tp % KV_BLOCK == 0 and (bs * ts) % KV_BLOCK == 0 and ts <= LANES

    rows = -(-(bp + bs) // 8) * 8
    c_all = jnp.pad(jnp.concatenate([c_prompt, c_sample], axis=0), ((0, rows - bp - bs), (0, 0)))
    mod_all = _ada(c_all, ada_w, ada_b)

    tab_p = _rope_tables(jnp.arange(tp))
    tab_s = _rope_tables(n_past + jnp.arange(ts))

    xp, xs = x_prompt, x_sample
    outs_p, outs_s = [], []
    for layer in range(depth):
        j = layer // 2
        w_l, bias_l = _in_weights(w_in[layer], b_forget[layer])
        pw = pool_w[layer]
        pool_bd = jnp.zeros((POOL_WIDTH, POOL_WIDTH), F32)
        for g in range(len(POOL_WINDOWS)):
            sl = slice(g * POOL_GROUP_DIM, (g + 1) * POOL_GROUP_DIM)
            pool_bd = pool_bd.at[sl, sl].set(pw[g])
        lw = dict(w_in=w_l, bf_bias=bias_l, norm_mix_g=norm_mix_g[layer], norm_ffn_g=norm_ffn_g[layer],
                  pool_bd=pool_bd.astype(BF16), pool_scale=pool_scale[layer],
                  w_br_a=w_br_a[layer].astype(BF16), w_br_b=w_br_b[layer].astype(BF16),
                  w_br_c=w_br_c[layer].astype(BF16), w_out=w_out[layer].astype(BF16))
        if layer % 2 == 0:
            lw.update(ffn_wg=ffn_w_gate[j].astype(BF16), ffn_wu=ffn_w_up[j].astype(BF16),
                      ffn_wd=ffn_w_down[j].astype(BF16))
        else:
            rw = jnp.pad(moe_router_w[j], ((0, 0), (0, LANES - N_EXPERTS)))
            rw_hi = rw.astype(BF16)
            lw.update(router_w=jnp.stack([rw_hi, (rw - rw_hi.astype(F32)).astype(BF16)]),
                      router_b=jnp.pad(moe_router_b[j], (0, LANES - N_EXPERTS)).reshape(1, LANES),
                      moe_wg=moe_w_gate[j].astype(BF16), moe_wu=moe_w_up[j].astype(BF16),
                      moe_wd=moe_w_down[j].astype(BF16))
        final_g = final_norm_g if layer == depth - 1 else None
        mod_p = [m[:, None, :] for m in jnp.split(mod_all[layer, :bp], 6, axis=-1)]
        mod_s = [m[:, None, :] for m in jnp.split(mod_all[layer, bp:bp + bs], 6, axis=-1)]
        xp, new_p = _layer(xp, mod_p, None, 0, tab_p, lw, layer, final_g, per_token=False)
        past = (cache_a_kvi[layer], cache_b_kv[layer], cache_b_logf[layer], state_c_pool[layer])
        xs, new_s = _layer(xs, mod_s, past, n_past, tab_s, lw, layer, final_g, per_token=True)
        outs_p.append(new_p)
        outs_s.append(new_s)

    def stack(outs, k):
        return jnp.stack([o[k] for o in outs])

    return (xp, xs,
            stack(outs_p, 0), stack(outs_p, 1), stack(outs_p, 2), stack(outs_p, 3),
            stack(outs_s, 0), stack(outs_s, 1), stack(outs_s, 2), stack(outs_s, 3))
```

```python
import functools

import jax
import jax.numpy as jnp
import numpy as np
from jax import lax
from jax.experimental import pallas as pl
from jax.experimental.pallas import tpu as pltpu

F32 = jnp.float32
BF16 = jnp.bfloat16
I32 = jnp.int32

D_MODEL = 1024
CHUNK = 64
HEAD_DIM = 64
HALF = HEAD_DIM // 2
ROPE_THETA = 10000.0
NORM_EPS = 1e-6
A_HEADS = 6
IDX_HEADS = 4
TOPK_MAX = 256
B_HEADS = 6
POOL_WINDOWS = (2, 4, 8, 16)
POOL_GROUP_DIM = 64
POOL_WIDTH = 256
POOL_HIST = 15
N_EXPERTS = 8
LANES = 128
SUBLANES = 8
LOG2E = 1.4426950408889634
QK_SCALE = HEAD_DIM ** -0.5 * LOG2E
KV_BLOCK = 256
V_ROWS = HEAD_DIM + 16
MOE_CHUNK = 256
VMEM_LIMIT = 56 * 1024 * 1024
NEG_INF = float("-inf")
KEY_DT = jnp.bfloat16
RANK_ZERO = 0x8000
RANK_NEG_INF = 0x007F
RANK_POS_INF = 0xFF80
IW_LANE = 8

C_AQ, C_IQ, C_A, C_B, C_CU, C_SM, C_GATE, C_END = 0, 384, 640, 896, 2048, 2304, 2432, 5504


def _cparams(sem):
    return pltpu.CompilerParams(dimension_semantics=sem, vmem_limit_bytes=VMEM_LIMIT)


def _const_spec(shape):
    nd = len(shape)
    return pl.BlockSpec(shape, lambda *_: (0,) * nd, pipeline_mode=pl.Buffered(1))


def _lane_iota(shape):
    return lax.broadcasted_iota(I32, shape, len(shape) - 1)


def _row_iota(shape):
    return lax.broadcasted_iota(I32, shape, len(shape) - 2)


def _ada_kernel(c_ref, w_ref, b_ref, o_ref):
    c = c_ref[...]
    s = c * jax.nn.sigmoid(c)
    o_ref[...] = jnp.dot(s, w_ref[...], preferred_element_type=F32,
                         precision=lax.Precision.HIGHEST) + b_ref[...]


def _ada(c_all, ada_w, ada_b):
    depth, d, n = ada_w.shape
    rows = c_all.shape[0]
    tn = 1536
    return pl.pallas_call(
        _ada_kernel,
        grid=(depth, n // tn),
        in_specs=[pl.BlockSpec((rows, d), lambda l, j: (0, 0)),
                  pl.BlockSpec((None, d, tn), lambda l, j: (l, 0, j)),
                  pl.BlockSpec((None, 1, tn), lambda l, j: (l, 0, j))],
        out_specs=pl.BlockSpec((None, rows, tn), lambda l, j: (l, 0, j)),
        out_shape=jax.ShapeDtypeStruct((depth, rows, n), F32),
        compiler_params=_cparams(("arbitrary", "arbitrary")),
        name="ada_mod",
    )(c_all, ada_w, ada_b.reshape(depth, 1, n))


def _in_kernel(x_ref, sc_ref, sh_ref, g_ref, w_ref, bf_ref, cos_ref, sin_ref,
               aqt_ref, iqt_ref, nat_ref, kik_ref, avt_ref, nbt_ref, bqt_ref, bk_ref, bvt_ref,
               cu_ref, sm_ref, smt_ref, gate_ref):
    x = x_ref[...]
    ms = jnp.mean(x * x, axis=-1, keepdims=True)
    y = x * lax.rsqrt(ms + NORM_EPS) * g_ref[...]
    h = (y * (1.0 + sc_ref[...]) + sh_ref[...]).astype(BF16)
    tm = x.shape[0]

    def mm(a, b):
        return jnp.dot(h, w_ref[:, a:b], preferred_element_type=F32)

    cos = cos_ref[...]
    sin = sin_ref[...]
    lane = _lane_iota((tm, LANES))
    low = lane < HEAD_DIM
    first_half = (lane & HALF) == 0

    def rope(z):
        swapped = jnp.where(first_half, pltpu.roll(z, LANES - HALF, 1), pltpu.roll(z, HALF, 1))
        return z * cos + swapped * sin

    zeros64 = jnp.zeros((HEAD_DIM, tm), BF16)
    ones_rows = jnp.where(_row_iota((V_ROWS - HEAD_DIM, tm)) == 0, 1.0, 0.0).astype(BF16)

    def put_heads(ref, zt, p, slot_even, slot_odd):
        for hh, slot in ((0, slot_even), (1, slot_odd)):
            base = (2 * p + hh) * LANES
            ref[base + slot * HEAD_DIM:base + (slot + 1) * HEAD_DIM, :] = zt[hh * HEAD_DIM:(hh + 1) * HEAD_DIM]
            ref[base + (1 - slot) * HEAD_DIM:base + (2 - slot) * HEAD_DIM, :] = zeros64

    z = mm(C_AQ, C_IQ)
    for p in range(3):
        zt = (rope(z[:, p * LANES:(p + 1) * LANES]) * QK_SCALE).T.astype(BF16)
        put_heads(aqt_ref, zt, p, 0, 0)
    z = mm(C_IQ, C_A)
    for p in range(2):
        zt = rope(z[:, p * LANES:(p + 1) * LANES]).T.astype(BF16)
        put_heads(iqt_ref, zt, p, 1, 1)

    z = mm(C_A, C_B)
    kv = z[:, :LANES]
    r0 = jnp.where(low, rope(kv), kv)
    r1 = rope(z[:, LANES:])
    r0t = r0.T
    nat_ref[0:LANES, :] = r0t
    nat_ref[LANES:, :] = r1.T[0:HEAD_DIM, :]
    kik_ref[...] = jnp.where(low, r0, pltpu.roll(r1, HEAD_DIM, 1)).astype(BF16)
    avt_ref[0:HEAD_DIM, :] = r0t[HEAD_DIM:, :].astype(BF16)
    avt_ref[HEAD_DIM:, :] = ones_rows

    z = mm(C_B, C_CU)
    bk_ref[...] = z[:, 384:768].astype(BF16)
    for p in range(3):
        zt = (z[:, p * LANES:(p + 1) * LANES] * QK_SCALE).T.astype(BF16)
        put_heads(bqt_ref, zt, p, 0, 1)
        nbt_ref[p * LANES:(p + 1) * LANES, :] = z[:, 384 + p * LANES:384 + (p + 1) * LANES].T
        vt = z[:, 768 + p * LANES:768 + (p + 1) * LANES].T
        nbt_ref[384 + p * LANES:384 + (p + 1) * LANES, :] = vt
        vt = vt.astype(BF16)
        for hh in range(2):
            bvt_ref[2 * p + hh, 0:HEAD_DIM, :] = vt[hh * HEAD_DIM:(hh + 1) * HEAD_DIM]
            bvt_ref[2 * p + hh, HEAD_DIM:, :] = ones_rows

    cu_ref[...] = mm(C_CU, C_SM)

    z = mm(C_SM, C_GATE)
    t = z + bf_ref[...]
    logf = jnp.minimum(t, 0.0) - jnp.log1p(jnp.exp(-jnp.abs(t)))
    sm = jnp.where(lane < B_HEADS, logf, z)
    sm_ref[...] = sm
    smt_ref[...] = sm.T[0:16, :]

    for c in range(3):
        gate_ref[:, c * D_MODEL:(c + 1) * D_MODEL] = jax.nn.sigmoid(
            mm(C_GATE + c * D_MODEL, C_GATE + (c + 1) * D_MODEL)).astype(BF16)


def _in_weights(w_in_l, b_forget_l):
    d = w_in_l.shape[0]
    sizes = (384, 64, 64, 256, 4, 64, 384, 384, 384, 6, 256, 3072)
    o = np.concatenate([[0], np.cumsum(sizes)])
    w_in_l = w_in_l.astype(BF16)
    cols = [w_in_l[:, o[0]:o[1]],
            w_in_l[:, o[3]:o[4]],
            w_in_l[:, o[1]:o[3]], w_in_l[:, o[5]:o[6]], jnp.zeros((d, 64), BF16),
            w_in_l[:, o[6]:o[9]],
            w_in_l[:, o[10]:o[11]],
            w_in_l[:, o[9]:o[10]], jnp.zeros((d, IW_LANE - B_HEADS), BF16), w_in_l[:, o[4]:o[5]],
            jnp.zeros((d, LANES - IW_LANE - IDX_HEADS), BF16),
            w_in_l[:, o[11]:o[12]]]
    w = jnp.concatenate(cols, axis=1)
    assert w.shape[1] == C_END, w.shape
    bias = jnp.concatenate([b_forget_l, jnp.zeros((LANES - B_HEADS,), F32)]).reshape(1, LANES)
    return w, bias


def _in_proj(x, sc, sh, g, w, bias, cos, sin, tm):
    bsz, t, d = x.shape
    mrows = sc.shape[1]
    mblk = 1 if mrows == 1 else tm
    mod_spec = pl.BlockSpec((None, mblk, d), (lambda b, i: (b, 0, 0)) if mrows == 1 else (lambda b, i: (b, i, 0)))
    nblk = t // tm

    def rows(n, dt):
        return pl.BlockSpec((None, tm, n), lambda b, i: (b, i, 0)), jax.ShapeDtypeStruct((bsz, t, n), dt)

    def cols(n, dt):
        return pl.BlockSpec((None, n, tm), lambda b, i: (b, 0, i)), jax.ShapeDtypeStruct((bsz, n, t), dt)

    outs = [cols(A_HEADS * LANES, BF16), cols(IDX_HEADS * LANES, BF16), cols(192, F32), rows(LANES, BF16),
            (pl.BlockSpec((None, None, V_ROWS, tm), lambda b, i: (b, i, 0, 0)),
             jax.ShapeDtypeStruct((bsz, nblk, V_ROWS, tm), BF16)),
            cols(768, F32), cols(B_HEADS * LANES, BF16), rows(384, BF16),
            (pl.BlockSpec((None, B_HEADS, None, V_ROWS, tm), lambda b, i: (b, 0, i, 0, 0)),
             jax.ShapeDtypeStruct((bsz, B_HEADS, nblk, V_ROWS, tm), BF16)),
            rows(256, F32), rows(LANES, F32), cols(16, F32), rows(3 * D_MODEL, BF16)]
    return pl.pallas_call(
        _in_kernel,
        grid=(bsz, nblk),
        in_specs=[pl.BlockSpec((None, tm, d), lambda b, i: (b, i, 0)), mod_spec, mod_spec,
                  _const_spec((1, d)), _const_spec(w.shape), _const_spec((1, LANES)),
                  pl.BlockSpec((tm, LANES), lambda b, i: (i, 0)),
                  pl.BlockSpec((tm, LANES), lambda b, i: (i, 0))],
        out_specs=[o[0] for o in outs],
        out_shape=[o[1] for o in outs],
        compiler_params=_cparams(("parallel", "parallel")),
        name="in_proj",
    )(x, sc, sh, g.reshape(1, d), w, bias, cos, sin)


def _cum_kernel(x_ref, o_ref, carry_ref):
    @pl.when(pl.program_id(1) == 0)
    def _():
        carry_ref[...] = jnp.zeros_like(carry_ref)

    tc = x_ref.shape[0]
    tri = jnp.where(_lane_iota((tc, tc)) <= _row_iota((tc, tc)), 1.0, 0.0).astype(BF16)
    cum = carry_ref[0:1, :]
    rest = x_ref[...]
    for _ in range(3):
        piece = rest.astype(BF16)
        cum = cum + jnp.dot(tri, piece, preferred_element_type=F32)
        rest = rest - piece.astype(F32)
    carry_ref[...] = jnp.broadcast_to(cum[tc - 1:tc, :], carry_ref.shape)
    pieces = []
    rest = cum * LOG2E
    for _ in range(3):
        piece = rest.astype(BF16)
        pieces.append(piece.astype(F32))
        rest = rest - pieces[-1]
    lane = _lane_iota((tc, LANES))
    for p in range(B_HEADS // 2):
        slab = jnp.zeros((tc, LANES), F32)
        for hh in range(2):
            for j, piece in enumerate(pieces):
                dst, src = 3 * hh + j, 2 * p + hh
                slab = jnp.where(lane == dst, pltpu.roll(piece, (dst - src) % LANES, 1), slab)
        o_ref[p] = slab.astype(BF16)


def _cum_logf(x, tc):
    bsz, t, n = x.shape
    return pl.pallas_call(
        _cum_kernel,
        grid=(bsz, t // tc),
        in_specs=[pl.BlockSpec((None, tc, n), lambda b, i: (b, i, 0))],
        out_specs=pl.BlockSpec((None, B_HEADS // 2, tc, LANES), lambda b, i: (b, 0, i, 0)),
        out_shape=jax.ShapeDtypeStruct((bsz, B_HEADS // 2, t, LANES), BF16),
        scratch_shapes=[pltpu.VMEM((SUBLANES, LANES), F32)],
        compiler_params=_cparams(("parallel", "arbitrary")),
        name="logf_cumsum",
    )(x)


def _sum_keys(x):
    part = x.reshape(x.shape[0] // SUBLANES, SUBLANES, x.shape[1]).sum(axis=0)
    return jnp.sum(part, axis=0, keepdims=True)


def _max_keys(x):
    part = x.reshape(x.shape[0] // SUBLANES, SUBLANES, x.shape[1]).max(axis=0)
    return jnp.max(part, axis=0, keepdims=True)


def _dsa_kernel(aqt_ref, iqt_ref, smt_ref, kik_ref, avt_ref, o_ref, key_ref,
                *, tq, tk, n_keys, q_pos0, n_sel):
    i = pl.program_id(1)
    pos_first = q_pos0 + i * tq
    last_chunk = (pos_first + tq - 1) // CHUNK
    n_adm = jnp.minimum((last_chunk + 1) * CHUNK, n_keys)
    nkb = (n_adm + tk - 1) // tk

    q_pos = pos_first + _lane_iota((1, tq))
    q_lim = jnp.minimum((q_pos // CHUNK + 1) * CHUNK, n_keys)
    key_row = _row_iota((tk, tq))

    def keys(kb):
        return kik_ref[pl.ds(pl.multiple_of(kb * tk, tk), tk), :]

    iq4 = jnp.concatenate([iqt_ref[hd * LANES:(hd + 1) * LANES, :] for hd in range(IDX_HEADS)], axis=1)
    smt = smt_ref[...]
    w_rows = [smt[IW_LANE + hd:IW_LANE + hd + 1, :] for hd in range(IDX_HEADS)]

    def score_body(kb, carry):
        s4 = jnp.dot(keys(kb), iq4, preferred_element_type=F32)
        score = w_rows[0] * jnp.maximum(s4[:, 0:tq], 0.0)
        for hd in range(1, IDX_HEADS):
            score = score + w_rows[hd] * jnp.maximum(s4[:, hd * tq:(hd + 1) * tq], 0.0)
        score = jnp.where(key_row < q_lim - kb * tk, score, NEG_INF)
        key_ref[kb] = score.astype(KEY_DT)
        return carry

    lax.fori_loop(0, nkb, score_body, 0)

    one, zero = jnp.ones((), KEY_DT), jnp.zeros((), KEY_DT)
    packed_rows = 2 * SUBLANES

    def count(cand, strict):
        def hits(blk):
            cols = []
            for c0 in range(0, tq, 2 * LANES):
                c1 = min(c0 + 2 * LANES, tq)
                h = jnp.where((blk[:, c0:c1] > cand[:, c0:c1]) if strict else (blk[:, c0:c1] >= cand[:, c0:c1]),
                              one, zero)
                parts = [h[r * packed_rows:(r + 1) * packed_rows] for r in range(tk // packed_rows)]
                while len(parts) > 1:
                    parts = [a + b for a, b in zip(parts[::2], parts[1::2])]
                cols.append(parts[0].astype(F32))
            return cols[0] if len(cols) == 1 else jnp.concatenate(cols, axis=1)

        def pair(j, acc):
            return acc + hits(key_ref[2 * j]) + hits(key_ref[2 * j + 1])

        acc = lax.fori_loop(0, nkb // 2, pair, jnp.zeros((packed_rows, tq), F32))
        acc = lax.fori_loop(2 * (nkb // 2), nkb, lambda kb, a: a + hits(key_ref[kb]), acc)
        return jnp.sum(acc, axis=0, keepdims=True)

    def pattern_value(u):
        bits = jnp.where(u >= RANK_ZERO, u - RANK_ZERO, (~u) & 0xFFFF)
        return lax.bitcast_convert_type(lax.shift_left(bits, 16), F32).astype(KEY_DT)

    def bit_body(b, u):
        cand_u = u | lax.shift_left(jnp.int32(1), 15 - b)
        cnt = count(pattern_value(cand_u), False)
        return jnp.where(cnt >= n_sel, cand_u, u)

    u_thr = jnp.maximum(lax.fori_loop(0, 16, bit_body, jnp.zeros((1, tq), I32)), RANK_NEG_INF)
    thr_16 = pattern_value(u_thr)

    def finer():
        lo = thr_16.astype(F32)
        hi = jnp.where(u_thr >= RANK_POS_INF, jnp.inf, pattern_value(u_thr + 1).astype(F32))

        def halve(_, lo_hi):
            lo, hi = lo_hi
            cand = (0.5 * lo + 0.5 * hi).astype(KEY_DT)
            enough = count(cand, False) >= n_sel
            return jnp.where(enough, cand.astype(F32), lo), jnp.where(enough, hi, cand.astype(F32))

        thr_f = lax.fori_loop(0, 16, halve, (lo, hi))[0].astype(KEY_DT)
        return thr_f, count(thr_f, True)

    above_16 = count(thr_16, True)
    thr_k, above = lax.cond(jnp.min(n_sel - above_16) <= 0.0, finer, lambda: (thr_16, above_16))
    need = jnp.maximum(n_sel - above, 0.0)
    thr = thr_k.astype(F32)

    aq6 = jnp.concatenate([aqt_ref[hd * LANES:(hd + 1) * LANES, :] for hd in range(A_HEADS)], axis=1)

    half = tk // 2
    lower = jnp.where(_lane_iota((half, half)) <= _row_iota((half, half)), 1.0, 0.0).astype(BF16)

    def attend():
        def body(kb, carry):
            eq_seen, ms, accs = carry
            blk = key_ref[kb].astype(F32)
            eq = blk == thr
            eq_f = jnp.where(eq, 1.0, 0.0)
            prefs = []
            for e in (eq_f[:half], eq_f[half:]):
                prefs.append(jnp.dot(lower, e.astype(BF16), preferred_element_type=F32) + eq_seen)
                eq_seen = eq_seen + _sum_keys(e)
            slack = jnp.where(blk >= thr, need - jnp.where(eq, jnp.concatenate(prefs, axis=0), 0.0), -1.0)
            bias = jnp.where(slack >= 0.0, jnp.where(jnp.abs(blk) < jnp.inf, 0.0, NEG_INF), NEG_INF)
            logits = jnp.dot(keys(kb), aq6, preferred_element_type=F32)
            vts = (avt_ref[2 * kb], avt_ref[2 * kb + 1])
            new_ms, new_accs = [], []
            for p in range(A_HEADS // 2):
                ps, alphas = [], []
                for hd in (2 * p, 2 * p + 1):
                    lg = logits[:, hd * tq:(hd + 1) * tq] + bias
                    m_old = ms[hd]
                    m_new = jnp.maximum(m_old, _max_keys(lg))
                    m_safe = jnp.where(m_new == NEG_INF, 0.0, m_new)
                    ps.append(jnp.exp2(lg - m_safe).astype(BF16))
                    alphas.append(jnp.exp2(m_old - m_safe))
                    new_ms.append(m_new)
                p2 = jnp.concatenate(ps, axis=1)
                pv = (jnp.dot(vts[0], p2[:tk // 2], preferred_element_type=F32)
                      + jnp.dot(vts[1], p2[tk // 2:], preferred_element_type=F32))
                new_accs.append(jnp.concatenate(alphas, axis=1) * accs[p] + pv)
            return eq_seen, tuple(new_ms), tuple(new_accs)

        init = (jnp.zeros((1, tq), F32),
                tuple(jnp.full((1, tq), NEG_INF, F32) for _ in range(A_HEADS)),
                tuple(jnp.zeros((V_ROWS, 2 * tq), F32) for _ in range(A_HEADS // 2)))
        return lax.fori_loop(0, nkb, body, init)[2]

    accs = attend()

    outs = []
    for p in range(A_HEADS // 2):
        o2 = accs[p][0:HEAD_DIM] / accs[p][HEAD_DIM:HEAD_DIM + 1]
        outs += [o2[:, 0:tq], o2[:, tq:2 * tq]]
    o_ref[...] = jnp.concatenate(outs, axis=0).T.astype(o_ref.dtype)


def _dsa(aqt, iqt, smt, kik, avt, *, tq, n_keys, q_pos0, n_sel):
    bsz, _, t_q = aqt.shape
    _, nblk, _, tkv = avt.shape
    assert nblk % 2 == 0
    tk = 2 * tkv
    kern = functools.partial(_dsa_kernel, tq=tq, tk=tk, n_keys=n_keys, q_pos0=q_pos0, n_sel=n_sel)
    return pl.pallas_call(
        kern,
        grid=(bsz, t_q // tq),
        in_specs=[pl.BlockSpec((None, A_HEADS * LANES, tq), lambda b, i: (b, 0, i)),
                  pl.BlockSpec((None, IDX_HEADS * LANES, tq), lambda b, i: (b, 0, i)),
                  pl.BlockSpec((None, 16, tq), lambda b, i: (b, 0, i)),
                  pl.BlockSpec((None, nblk * tkv, LANES), lambda b, i: (b, 0, 0)),
                  pl.BlockSpec((None, nblk, V_ROWS, tkv), lambda b, i: (b, 0, 0, 0))],
        out_specs=pl.BlockSpec((None, tq, 384), lambda b, i: (b, i, 0)),
        out_shape=jax.ShapeDtypeStruct((bsz, t_q, 384), BF16),
        scratch_shapes=[pltpu.VMEM((nblk // 2, tk, tq), KEY_DT)],
        compiler_params=_cparams(("parallel", "arbitrary")),
        name="dsa_attention",
    )(aqt, iqt, smt, kik, avt)


def _fox_kernel(qt_ref, k_ref, vt_ref, ck_ref, o_ref, *, tq, tk, n_keys, q_pos0):
    i = pl.program_id(1)
    pos_first = q_pos0 + i * tq
    n_full = pos_first // tk
    nkb = (jnp.minimum(pos_first + tq, n_keys) + tk - 1) // tk
    q_pos = pos_first + _lane_iota((1, tq))
    key_row = _row_iota((tk, tq))
    piece_row = _row_iota((LANES, tq))
    qts = []
    for hd in range(B_HEADS):
        minus = jnp.where((piece_row >= 3 * (hd % 2)) & (piece_row < 3 * (hd % 2) + 3), -1.0, 0.0).astype(BF16)
        qts.append(jnp.concatenate([qt_ref[hd * LANES:(hd + 1) * LANES, :], minus], axis=0))

    def step(kbs, state, masked):
        logits = []
        for hd in range(B_HEADS):
            for kb in kbs:
                rows = pl.ds(pl.multiple_of(kb * tk, tk), tk)
                kblk = jnp.concatenate([k_ref[rows, (hd // 2) * LANES:(hd // 2 + 1) * LANES],
                                        ck_ref[hd // 2, rows, :]], axis=1)
                lg = jnp.dot(kblk, qts[hd], preferred_element_type=F32)
                if masked:
                    lg = jnp.where(key_row <= q_pos - kb * tk, lg, NEG_INF)
                logits.append(lg)
        new = []
        for hd in range(B_HEADS):
            m_old, acc = state[hd]
            lgs = logits[hd * len(kbs):(hd + 1) * len(kbs)]
            m_new = m_old
            for lg in lgs:
                m_new = jnp.maximum(m_new, _max_keys(lg))
            m_safe = jnp.where(m_new == NEG_INF, 0.0, m_new) if masked else m_new
            acc = jnp.exp2(m_old - m_safe) * acc
            for kb, lg in zip(kbs, lgs):
                p = jnp.exp2(lg - m_safe).astype(BF16)
                acc = acc + jnp.dot(vt_ref[hd, kb], p, preferred_element_type=F32)
            new.append((m_new, acc))
        return tuple(new)

    init = tuple((jnp.full((1, tq), NEG_INF, F32), jnp.zeros((V_ROWS, tq), F32)) for _ in range(B_HEADS))
    state = lax.fori_loop(0, n_full // 2, lambda j, st: step((2 * j, 2 * j + 1), st, False), init)
    state = lax.fori_loop(2 * (n_full // 2), nkb, lambda kb, st: step((kb,), st, True), state)
    outs = [acc[0:HEAD_DIM] / acc[HEAD_DIM:HEAD_DIM + 1] for _, acc in state]
    o_ref[...] = jnp.concatenate(outs, axis=0).T.astype(o_ref.dtype)


def _fox(qt, k, vt, ck, *, tq, n_keys, q_pos0):
    bsz, _, t_q = qt.shape
    _, _, nblk, _, tk = vt.shape
    lpad = nblk * tk
    kern = functools.partial(_fox_kernel, tq=tq, tk=tk, n_keys=n_keys, q_pos0=q_pos0)
    return pl.pallas_call(
        kern,
        grid=(bsz, t_q // tq),
        in_specs=[pl.BlockSpec((None, B_HEADS * LANES, tq), lambda b, i: (b, 0, i)),
                  pl.BlockSpec((None, lpad, 384), lambda b, i: (b, 0, 0)),
                  pl.BlockSpec((None, B_HEADS, nblk, V_ROWS, tk), lambda b, i: (b, 0, 0, 0, 0)),
                  pl.BlockSpec((None, B_HEADS // 2, lpad, LANES), lambda b, i: (b, 0, 0, 0))],
        out_specs=pl.BlockSpec((None, tq, 384), lambda b, i: (b, i, 0)),
        out_shape=jax.ShapeDtypeStruct((bsz, t_q, 384), BF16),
        compiler_params=_cparams(("parallel", "arbitrary")),
        name="fox_attention",
    )(qt, k, vt, ck)


def _pool_kernel(cur_ref, prev_ref, hist_ref, w_ref, s_ref, o_ref, ext, *, tc, start_pos):
    i = pl.program_id(1)
    cur = cur_ref[...]
    ext[0:16, :] = jnp.where(i == 0, hist_ref[...], prev_ref[tc - 16:, :])
    ext[16:, :] = cur
    pos = start_pos + i * tc + lax.broadcasted_iota(I32, (tc, POOL_WIDTH), 0)
    lane = _lane_iota((tc, POOL_WIDTH))
    run = cur
    pooled = jnp.zeros_like(cur)
    k = 1
    for g, w in enumerate(POOL_WINDOWS):
        while k < w:
            run = run + ext[16 - k:16 - k + tc, :]
            k += 1
        cnt = jnp.minimum(pos + 1, w).astype(F32)
        in_group = (lane >= g * POOL_GROUP_DIM) & (lane < (g + 1) * POOL_GROUP_DIM)
        pooled = jnp.where(in_group, run / cnt, pooled)
    z = (pooled - cur).astype(BF16)
    o_ref[...] = (jnp.dot(z, w_ref[...], preferred_element_type=F32) * s_ref[...]).astype(o_ref.dtype)


def _pool(cu, hist16, w_bd, scale, *, tc, start_pos):
    bsz, t, n = cu.shape
    kern = functools.partial(_pool_kernel, tc=tc, start_pos=start_pos)
    return pl.pallas_call(
        kern,
        grid=(bsz, t // tc),
        in_specs=[pl.BlockSpec((None, tc, n), lambda b, i: (b, i, 0)),
                  pl.BlockSpec((None, tc, n), lambda b, i: (b, jnp.maximum(i - 1, 0), 0)),
                  pl.BlockSpec((None, 16, n), lambda b, i: (b, 0, 0)),
                  _const_spec((n, n)), _const_spec((1, n))],
        out_specs=pl.BlockSpec((None, tc, n), lambda b, i: (b, i, 0)),
        out_shape=jax.ShapeDtypeStruct((bsz, t, n), BF16),
        scratch_shapes=[pltpu.VMEM((16 + tc, n), F32)],
        compiler_params=_cparams(("parallel", "arbitrary")),
        name="pool_mixer",
    )(cu, cu, hist16, w_bd, scale.reshape(1, n))


def _route(logits):
    lane = _lane_iota(logits.shape).astype(F32)
    lg = jnp.where(lane < N_EXPERTS, logits, NEG_INF)
    m1 = jnp.max(lg, axis=1, keepdims=True)
    i1 = jnp.min(jnp.where(lg == m1, lane, float(LANES)), axis=1, keepdims=True)
    hot1 = lane == i1
    lg2 = jnp.where(hot1, NEG_INF, lg)
    m2 = jnp.max(lg2, axis=1, keepdims=True)
    i2 = jnp.min(jnp.where(lg2 == m2, lane, float(LANES)), axis=1, keepdims=True)
    hot2 = lane == i2
    e2 = jnp.exp(m2 - m1)
    den = 1.0 + e2
    return jnp.where(hot1, 1.0 / den, 0.0) + jnp.where(hot2, e2 / den, 0.0)


def _merge_kernel(x_ref, oa_ref, ob_ref, oc_ref, gate_ref, g1_ref, sc2_ref, sh2_ref, g_ref,
                  wa_ref, wb_ref, wc_ref, wo_ref, *rest, moe):
    if moe:
        rw_ref, rb_ref, xo_ref, h_ref, gw_ref = rest
    else:
        xo_ref, h_ref = rest
    d = D_MODEL
    merged = (gate_ref[:, 0:d] * jnp.dot(oa_ref[...], wa_ref[...], preferred_element_type=F32)
              + gate_ref[:, d:2 * d] * jnp.dot(ob_ref[...], wb_ref[...], preferred_element_type=F32)
              + gate_ref[:, 2 * d:3 * d] * jnp.dot(oc_ref[...], wc_ref[...], preferred_element_type=F32))
    x = x_ref[...] + g1_ref[...] * jnp.dot(merged.astype(BF16), wo_ref[...], preferred_element_type=F32)
    xo_ref[...] = x
    ms = jnp.mean(x * x, axis=-1, keepdims=True)
    y = x * lax.rsqrt(ms + NORM_EPS) * g_ref[...]
    h = y * (1.0 + sc2_ref[...]) + sh2_ref[...]
    h_ref[...] = h.astype(BF16)
    if moe:
        h_hi = h.astype(BF16)
        h_lo = (h - h_hi.astype(F32)).astype(BF16)
        logits = (jnp.dot(h_hi, rw_ref[0], preferred_element_type=F32)
                  + jnp.dot(h_lo, rw_ref[0], preferred_element_type=F32)
                  + jnp.dot(h_hi, rw_ref[1], preferred_element_type=F32)) + rb_ref[...]
        gw_ref[...] = _route(logits)


def _merge(x, oa, ob, oc, gates, g1, sc2, sh2, g, wa, wb, wc, wo, router, tm):
    bsz, t, d = x.shape
    mrows = g1.shape[1]
    mblk = 1 if mrows == 1 else tm
    mod_spec = pl.BlockSpec((None, mblk, d), (lambda b, i: (b, 0, 0)) if mrows == 1 else (lambda b, i: (b, i, 0)))

    def tok(n):
        return pl.BlockSpec((None, tm, n), lambda b, i: (b, i, 0))

    in_specs = [tok(d), tok(384), tok(384), tok(256), tok(3 * d), mod_spec, mod_spec, mod_spec,
                _const_spec((1, d)), _const_spec(wa.shape), _const_spec(wb.shape), _const_spec(wc.shape),
                _const_spec(wo.shape)]
    args = [x, oa, ob, oc, gates, g1, sc2, sh2, g.reshape(1, d), wa, wb, wc, wo]
    out_specs = [tok(d), tok(d)]
    out_shape = [jax.ShapeDtypeStruct((bsz, t, d), F32), jax.ShapeDtypeStruct((bsz, t, d), BF16)]
    if router is not None:
        rw, rb = router
        in_specs += [_const_spec(rw.shape), _const_spec(rb.shape)]
        args += [rw, rb]
        out_specs.append(tok(LANES))
        out_shape.append(jax.ShapeDtypeStruct((bsz, t, LANES), F32))
    return pl.pallas_call(
        functools.partial(_merge_kernel, moe=router is not None),
        grid=(bsz, t // tm),
        in_specs=in_specs, out_specs=out_specs, out_shape=out_shape,
        compiler_params=_cparams(("parallel", "parallel")),
        name="merge_out",
    )(*args)


def _final_norm(x, gain):
    ms = jnp.mean(x * x, axis=-1, keepdims=True)
    return x * lax.rsqrt(ms + NORM_EPS) * gain


def _ffn_kernel(x_ref, h_ref, g2_ref, wg_ref, wu_ref, wd_ref, *rest, n_chunks, final):
    if final:
        fg_ref, o_ref = rest
    else:
        (o_ref,) = rest
    h = h_ref[...]
    tf = wg_ref.shape[1] // n_chunks
    acc = jnp.zeros(x_ref.shape, F32)
    for c in range(n_chunks):
        gt = jnp.dot(h, wg_ref[:, c * tf:(c + 1) * tf], preferred_element_type=F32)
        up = jnp.dot(h, wu_ref[:, c * tf:(c + 1) * tf], preferred_element_type=F32)
        act = (gt * jax.nn.sigmoid(gt) * up).astype(BF16)
        acc = acc + jnp.dot(act, wd_ref[c * tf:(c + 1) * tf, :], preferred_element_type=F32)
    x = x_ref[...] + g2_ref[...] * acc
    o_ref[...] = _final_norm(x, fg_ref[...]) if final else x


def _ffn(x, h, g2, wg, wu, wd, final_g, tm):
    bsz, t, d = x.shape
    mrows = g2.shape[1]
    mblk = 1 if mrows == 1 else tm
    mod_spec = pl.BlockSpec((None, mblk, d), (lambda b, i: (b, 0, 0)) if mrows == 1 else (lambda b, i: (b, i, 0)))
    tok = pl.BlockSpec((None, tm, d), lambda b, i: (b, i, 0))
    in_specs = [tok, tok, mod_spec, _const_spec(wg.shape), _const_spec(wu.shape), _const_spec(wd.shape)]
    args = [x, h, g2, wg, wu, wd]
    if final_g is not None:
        in_specs.append(_const_spec((1, d)))
        args.append(final_g.reshape(1, d))
    return pl.pallas_call(
        functools.partial(_ffn_kernel, n_chunks=2, final=final_g is not None),
        grid=(bsz, t // tm),
        in_specs=in_specs, out_specs=tok,
        out_shape=jax.ShapeDtypeStruct((bsz, t, d), F32),
        compiler_params=_cparams(("parallel", "parallel")),
        name="ffn_dense",
    )(*args)


def _moe_kernel(x_ref, h_ref, g2_ref, gw_ref, wg_ref, wu_ref, wd_ref, *rest, final):
    if final:
        fg_ref, o_ref, acc_ref, posc_ref, posr_ref = rest
    else:
        o_ref, acc_ref, posc_ref, posr_ref = rest
    e = pl.program_id(2)
    tm = h_ref.shape[0]
    n_slabs = tm // MOE_CHUNK

    @pl.when(e == 0)
    def _():
        acc_ref[...] = jnp.zeros_like(acc_ref)
        routed = gw_ref[...] != 0.0
        r_f = jnp.where(routed, 1.0, 0.0)
        r_b = r_f.astype(BF16)
        r_t = r_f.T
        r_tb = r_t.astype(BF16)
        tok_l = _lane_iota((MOE_CHUNK, tm))
        tok_r = _row_iota((MOE_CHUNK, tm))
        rank_r = jnp.zeros((LANES, tm), F32)
        for s in range(n_slabs):
            rows = slice(s * MOE_CHUNK, (s + 1) * MOE_CHUNK)
            earlier = jnp.where(tok_l < tok_r + s * MOE_CHUNK, 1.0, 0.0).astype(BF16)
            rank_c = jnp.dot(earlier, r_b, preferred_element_type=F32)
            posc_ref[rows, :] = jnp.where(routed[rows], rank_c, -1.0)
            later = jnp.where(tok_r + s * MOE_CHUNK < tok_l, 1.0, 0.0).astype(BF16)
            rank_r = rank_r + jnp.dot(r_tb[:, rows], later, preferred_element_type=F32)
        posr_ref[...] = jnp.where(r_t != 0.0, rank_r, -1.0)

    lane_e = _lane_iota((tm, LANES)) == e
    pos_c = jnp.sum(jnp.where(lane_e, posc_ref[...], 0.0), axis=1, keepdims=True)
    gate_c = jnp.sum(jnp.where(lane_e, gw_ref[...], 0.0), axis=1, keepdims=True)
    pos_r = posr_ref[pl.ds(e, 1), :]
    n_routed = (jnp.max(pos_r) + 1.0).astype(I32)

    def run_chunk(base, n_rows):
        slot_rows = _row_iota((n_rows, tm)).astype(F32)
        slot_lanes = _lane_iota((MOE_CHUNK, n_rows)).astype(F32)
        pack = jnp.where(pos_r - base == slot_rows, 1.0, 0.0).astype(BF16)
        xc = jnp.dot(pack, h_ref[...], preferred_element_type=F32).astype(BF16)
        gt = jnp.dot(xc, wg_ref[...], preferred_element_type=F32)
        up = jnp.dot(xc, wu_ref[...], preferred_element_type=F32)
        act = (gt * jax.nn.sigmoid(gt) * up).astype(BF16)
        y = jnp.dot(act, wd_ref[...], preferred_element_type=F32).astype(BF16)
        for s in range(n_slabs):
            rows = slice(s * MOE_CHUNK, (s + 1) * MOE_CHUNK)
            unpack = jnp.where(pos_c[rows] - base == slot_lanes, 1.0, 0.0).astype(BF16)
            acc_ref[rows, :] += gate_c[rows] * jnp.dot(unpack, y, preferred_element_type=F32)

    def first(c, carry):
        run_chunk(0.0, MOE_CHUNK)
        return carry

    def later(c, carry):
        run_chunk((MOE_CHUNK + c * (MOE_CHUNK // 2)).astype(F32), MOE_CHUNK // 2)
        return carry

    lax.fori_loop(0, jnp.minimum(n_routed, 1), first, 0)
    n_later = (jnp.maximum(n_routed - MOE_CHUNK, 0) + MOE_CHUNK // 2 - 1) // (MOE_CHUNK // 2)
    lax.fori_loop(0, n_later, later, 0)

    @pl.when(e == pl.num_programs(2) - 1)
    def _():
        x = x_ref[...] + g2_ref[...] * acc_ref[...]
        o_ref[...] = _final_norm(x, fg_ref[...]) if final else x


def _moe(x, h, g2, gw, wg, wu, wd, final_g, tm):
    bsz, t, d = x.shape
    n_e, _, dff = wg.shape
    mrows = g2.shape[1]
    mblk = 1 if mrows == 1 else tm
    mod_spec = pl.BlockSpec((None, mblk, d), (lambda b, i, e: (b, 0, 0)) if mrows == 1 else (lambda b, i, e: (b, i, 0)))
    tok = pl.BlockSpec((None, tm, d), lambda b, i, e: (b, i, 0))
    tok_once = pl.BlockSpec((None, tm, d), lambda b, i, e: (b, i, 0), pipeline_mode=pl.Buffered(1))
    in_specs = [tok_once, tok, mod_spec, pl.BlockSpec((None, tm, LANES), lambda b, i, e: (b, i, 0)),
                pl.BlockSpec((None, d, dff), lambda b, i, e: (e, 0, 0)),
                pl.BlockSpec((None, d, dff), lambda b, i, e: (e, 0, 0)),
                pl.BlockSpec((None, dff, d), lambda b, i, e: (e, 0, 0))]
    args = [x, h, g2, gw, wg, wu, wd]
    if final_g is not None:
        in_specs.append(pl.BlockSpec((1, d), lambda b, i, e: (0, 0)))
        args.append(final_g.reshape(1, d))
    return pl.pallas_call(
        functools.partial(_moe_kernel, final=final_g is not None),
        grid=(bsz, t // tm, n_e),
        in_specs=in_specs, out_specs=tok,
        out_shape=jax.ShapeDtypeStruct((bsz, t, d), F32),
        scratch_shapes=[pltpu.VMEM((tm, d), F32), pltpu.VMEM((tm, LANES), F32), pltpu.VMEM((LANES, tm), F32)],
        compiler_params=_cparams(("parallel", "parallel", "arbitrary")),
        name="moe_routed",
    )(*args)


def _rope_tables(pos):
    inv = ROPE_THETA ** (-jnp.arange(HALF, dtype=F32) / HALF)
    ang = pos.astype(F32)[:, None] * inv[None, :]
    cos, sin = jnp.cos(ang), jnp.sin(ang)
    return jnp.tile(cos, (1, 4)), jnp.tile(jnp.concatenate([-sin, sin], axis=1), (1, 2))


def _pick_tile(n, pref):
    t = min(n, pref)
    while n % t:
        t //= 2
    return t


def _per_seq_cols(a, bsz, t, width):
    f = a.shape[1]
    a = jnp.moveaxis(a[0].reshape(f, bsz, t), 1, 0)
    return jnp.pad(a, ((0, 0), (0, 0), (0, width - t)))


def _value_blocks(past_vt, new_vt, bsz, t, lpad):
    lead = new_vt.shape[:-2]
    new_b = jnp.moveaxis(new_vt.reshape(*lead, V_ROWS, bsz, t), -2, 0)
    full = jnp.concatenate([past_vt, new_b], axis=-1)
    full = jnp.pad(full, [(0, 0)] * (full.ndim - 1) + [(0, lpad - full.shape[-1])])
    full = full.reshape(bsz, *lead, V_ROWS, lpad // KV_BLOCK, KV_BLOCK)
    return jnp.moveaxis(full, -2, -3)


def _with_ones_rows(vt):
    ones = jnp.ones(vt.shape[:-2] + (1, vt.shape[-1]), vt.dtype)
    zeros = jnp.zeros(vt.shape[:-2] + (V_ROWS - HEAD_DIM - 1, vt.shape[-1]), vt.dtype)
    return jnp.concatenate([vt, ones, zeros], axis=-2)


def _mixers(inp, past, n_past, lw, bsz, t):
    aqt, iqt, kik, avt, bqt, bk, bvt, cu, sm, smt = inp
    n_keys = n_past + t
    n_sel = min(TOPK_MAX, n_keys // 4)
    lpad = -(-n_keys // (2 * KV_BLOCK)) * 2 * KV_BLOCK
    if past is None:
        kik_all, avt_all, bk_all, bvt_all, logf_all = kik, avt, bk, bvt, sm
        hist16 = jnp.zeros((bsz, 16, POOL_WIDTH), F32)
        tq_a, tq_b, t_pad = _pick_tile(t, 512), _pick_tile(t, 256), t
    else:
        pa, pb, plf, pc = past
        pk, pv, pik = (pa[:, :, j].astype(BF16) for j in range(3))

        def join_rows(p, new):
            full = jnp.concatenate([p, new.reshape(bsz, t, new.shape[-1])], axis=1)
            return jnp.pad(full, ((0, 0), (0, lpad - n_keys), (0, 0)))

        kik_all = join_rows(jnp.concatenate([pk, pik], axis=-1), kik)
        avt_all = _value_blocks(_with_ones_rows(jnp.swapaxes(pv, 1, 2)), avt[0, 0], bsz, t, lpad)
        bk_all = join_rows(pb[:, :, 0].astype(BF16).reshape(bsz, n_past, 384), bk)
        pvt = jnp.transpose(pb[:, :, 1].astype(BF16), (0, 2, 3, 1))
        bvt_all = _value_blocks(_with_ones_rows(pvt), bvt[0, :, 0], bsz, t, lpad)
        logf_all = join_rows(jnp.pad(plf, ((0, 0), (0, 0), (0, LANES - B_HEADS))), sm)
        hist16 = jnp.pad(pc, ((0, 0), (1, 0), (0, 0)))
        tq_a = tq_b = t_pad = LANES
        aqt, iqt, bqt, smt = (_per_seq_cols(a, bsz, t, t_pad) for a in (aqt, iqt, bqt, smt))

    oa = _dsa(aqt, iqt, smt, kik_all, avt_all, tq=tq_a, n_keys=n_keys, q_pos0=n_past, n_sel=n_sel)
    ck = _cum_logf(logf_all, 2 * KV_BLOCK)
    ob = _fox(bqt, bk_all, bvt_all, ck, tq=tq_b, n_keys=n_keys, q_pos0=n_past)
    cu = cu.reshape(bsz, t, POOL_WIDTH)
    oc = _pool(cu, hist16, lw["pool_bd"], lw["pool_scale"], tc=_pick_tile(t, 1024), start_pos=n_past)
    return oa[:, :t], ob[:, :t], oc


def _layer(x, mod, past, n_past, pos_tab, lw, layer, final_g, per_token):
    bsz, t, d = x.shape
    sh1, sc1, g1, sh2, sc2, g2 = mod
    if per_token:
        xt = x.reshape(1, bsz * t, d)
        sh1, sc1, g1, sh2, sc2, g2 = (jnp.broadcast_to(m, (bsz, t, d)).reshape(1, bsz * t, d) for m in mod)
        cos, sin = (jnp.tile(a, (bsz, 1)) for a in pos_tab)
    else:
        xt = x
        cos, sin = pos_tab
    tm = KV_BLOCK
    (aqt, iqt, nat, kik, avt, nbt, bqt, bk, bvt, cu, sm, smt, gates) = _in_proj(
        xt, sc1, sh1, lw["norm_mix_g"], lw["w_in"], lw["bf_bias"], cos, sin, tm)
    oa, ob, oc = _mixers((aqt, iqt, kik, avt, bqt, bk, bvt, cu, sm, smt), past, n_past, lw, bsz, t)

    def flat(a):
        return a.reshape(xt.shape[0], xt.shape[1], a.shape[-1])

    router = (lw["router_w"], lw["router_b"]) if layer % 2 else None
    res = _merge(xt, flat(oa), flat(ob), flat(oc), gates, g1, sc2, sh2, lw["norm_ffn_g"],
                 lw["w_br_a"], lw["w_br_b"], lw["w_br_c"], lw["w_out"], router, _pick_tile(xt.shape[1], 512))
    tmf = _pick_tile(xt.shape[1], 512)
    if layer % 2 == 0:
        x_mid, h2 = res
        x_new = _ffn(x_mid, h2, g2, lw["ffn_wg"], lw["ffn_wu"], lw["ffn_wd"], final_g, tmf)
    else:
        x_mid, h2, gw = res
        x_new = _moe(x_mid, h2, g2, gw, lw["moe_wg"], lw["moe_wu"], lw["moe_wd"], final_g,
                     _pick_tile(xt.shape[1], 1024))
    def token_major(a, *feat):
        a = a.reshape(a.shape[0], *feat, -1, t) if per_token else a.reshape(a.shape[0], *feat, 1, t)
        a = jnp.moveaxis(a, (-2, -1), (1, 2))
        return a.reshape(bsz, t, *feat)

    new_a = token_major(nat, 3, HEAD_DIM)
    new_b = token_major(nbt, 2, B_HEADS, HEAD_DIM)
    new_logf = sm.reshape(bsz, t, LANES)[:, :, :B_HEADS]
    new_pool = cu.reshape(bsz, t, POOL_WIDTH)[:, t - POOL_HIST:, :]
    return x_new.reshape(bsz, t, d), (new_a, new_b, new_logf, new_pool)


def kernel(x_prompt, x_sample, cache_a_kvi, cache_b_kv, cache_b_logf, state_c_pool, c_prompt, c_sample,
           ada_w, ada_b, norm_mix_g, w_in, b_forget, pool_w, pool_scale, w_br_a, w_br_b, w_br_c, w_out,
           norm_ffn_g, ffn_w_gate, ffn_w_up, ffn_w_down, moe_router_w, moe_router_b, moe_w_gate,
           moe_w_up, moe_w_down, final_norm_g):
    depth = ada_w.shape[0]
    bp, tp, d = x_prompt.shape
    bs, ts, _ = x_sample.shape
    n_past = cache_a_kvi.shape[2]
    assert tp % KV_BLOCK == 0 and (bs * ts) % KV_BLOCK == 0 and ts <= LANES

    rows = -(-(bp + bs) // 8) * 8
    c_all = jnp.pad(jnp.concatenate([c_prompt, c_sample], axis=0), ((0, rows - bp - bs), (0, 0)))
    mod_all = _ada(c_all, ada_w, ada_b)

    tab_p = _rope_tables(jnp.arange(tp))
    tab_s = _rope_tables(n_past + jnp.arange(ts))

    xp, xs = x_prompt, x_sample
    outs_p, outs_s = [], []
    for layer in range(depth):
        j = layer // 2
        w_l, bias_l = _in_weights(w_in[layer], b_forget[layer])
        pw = pool_w[layer]
        pool_bd = jnp.zeros((POOL_WIDTH, POOL_WIDTH), F32)
        for g in range(len(POOL_WINDOWS)):
            sl = slice(g * POOL_GROUP_DIM, (g + 1) * POOL_GROUP_DIM)
            pool_bd = pool_bd.at[sl, sl].set(pw[g])
        lw = dict(w_in=w_l, bf_bias=bias_l, norm_mix_g=norm_mix_g[layer], norm_ffn_g=norm_ffn_g[layer],
                  pool_bd=pool_bd.astype(BF16), pool_scale=pool_scale[layer],
                  w_br_a=w_br_a[layer].astype(BF16), w_br_b=w_br_b[layer].astype(BF16),
                  w_br_c=w_br_c[layer].astype(BF16), w_out=w_out[layer].astype(BF16))
        if layer % 2 == 0:
            lw.update(ffn_wg=ffn_w_gate[j].astype(BF16), ffn_wu=ffn_w_up[j].astype(BF16),
                      ffn_wd=ffn_w_down[j].astype(BF16))
        else:
            rw = jnp.pad(moe_router_w[j], ((0, 0), (0, LANES - N_EXPERTS)))
            rw_hi = rw.astype(BF16)
            lw.update(router_w=jnp.stack([rw_hi, (rw - rw_hi.astype(F32)).astype(BF16)]),
                      router_b=jnp.pad(moe_router_b[j], (0, LANES - N_EXPERTS)).reshape(1, LANES),
                      moe_wg=moe_w_gate[j].astype(BF16), moe_wu=moe_w_up[j].astype(BF16),
                      moe_wd=moe_w_down[j].astype(BF16))
        final_g = final_norm_g if layer == depth - 1 else None
        mod_p = [m[:, None, :] for m in jnp.split(mod_all[layer, :bp], 6, axis=-1)]
        mod_s = [m[:, None, :] for m in jnp.split(mod_all[layer, bp:bp + bs], 6, axis=-1)]
        xp, new_p = _layer(xp, mod_p, None, 0, tab_p, lw, layer, final_g, per_token=False)
        past = (cache_a_kvi[layer], cache_b_kv[layer], cache_b_logf[layer], state_c_pool[layer])
        xs, new_s = _layer(xs, mod_s, past, n_past, tab_s, lw, layer, final_g, per_token=True)
        outs_p.append(new_p)
        outs_s.append(new_s)

    def stack(outs, k):
        return jnp.stack([o[k] for o in outs])

    return (xp, xs,
            stack(outs_p, 0), stack(outs_p, 1), stack(outs_p, 2), stack(outs_p, 3),
            stack(outs_s, 0), stack(outs_s, 1), stack(outs_s, 2), stack(outs_s, 3))
```

```python
import functools

import jax
import jax.numpy as jnp
import numpy as np
from jax import lax
from jax.experimental import pallas as pl
from jax.experimental.pallas import tpu as pltpu

F32 = jnp.float32
BF16 = jnp.bfloat16
I32 = jnp.int32

D_MODEL = 1024
CHUNK = 64
HEAD_DIM = 64
HALF = HEAD_DIM // 2
ROPE_THETA = 10000.0
NORM_EPS = 1e-6
A_HEADS = 6
IDX_HEADS = 4
TOPK_MAX = 256
B_HEADS = 6
POOL_WINDOWS = (2, 4, 8, 16)
POOL_GROUP_DIM = 64
POOL_WIDTH = 256
POOL_HIST = 15
N_EXPERTS = 8
LANES = 128
SUBLANES = 8
LOG2E = 1.4426950408889634
QK_SCALE = HEAD_DIM ** -0.5 * LOG2E
KV_BLOCK = 256
V_ROWS = HEAD_DIM + 16
MOE_CHUNK = 256
VMEM_LIMIT = 56 * 1024 * 1024
NEG_INF = float("-inf")
KEY_DT = jnp.bfloat16
RANK_ZERO = 0x8000
RANK_NEG_INF = 0x007F
RANK_POS_INF = 0xFF80
IW_LANE = 8

C_AQ, C_IQ, C_A, C_B, C_CU, C_SM, C_GATE, C_END = 0, 384, 640, 896, 2048, 2304, 2432, 5504


def _cparams(sem):
    return pltpu.CompilerParams(dimension_semantics=sem, vmem_limit_bytes=VMEM_LIMIT)


def _const_spec(shape):
    nd = len(shape)
    return pl.BlockSpec(shape, lambda *_: (0,) * nd, pipeline_mode=pl.Buffered(1))


def _lane_iota(shape):
    return lax.broadcasted_iota(I32, shape, len(shape) - 1)


def _row_iota(shape):
    return lax.broadcasted_iota(I32, shape, len(shape) - 2)


def _ada_kernel(c_ref, w_ref, b_ref, o_ref):
    c = c_ref[...]
    s = c * jax.nn.sigmoid(c)
    o_ref[...] = jnp.dot(s, w_ref[...], preferred_element_type=F32,
                         precision=lax.Precision.HIGHEST) + b_ref[...]


def _ada(c_all, ada_w, ada_b):
    depth, d, n = ada_w.shape
    rows = c_all.shape[0]
    tn = 1536
    return pl.pallas_call(
        _ada_kernel,
        grid=(depth, n // tn),
        in_specs=[pl.BlockSpec((rows, d), lambda l, j: (0, 0)),
                  pl.BlockSpec((None, d, tn), lambda l, j: (l, 0, j)),
                  pl.BlockSpec((None, 1, tn), lambda l, j: (l, 0, j))],
        out_specs=pl.BlockSpec((None, rows, tn), lambda l, j: (l, 0, j)),
        out_shape=jax.ShapeDtypeStruct((depth, rows, n), F32),
        compiler_params=_cparams(("arbitrary", "arbitrary")),
        name="ada_mod",
    )(c_all, ada_w, ada_b.reshape(depth, 1, n))


def _in_kernel(x_ref, sc_ref, sh_ref, g_ref, w_ref, bf_ref, cos_ref, sin_ref,
               aqt_ref, iqt_ref, nat_ref, kik_ref, avt_ref, nbt_ref, bqt_ref, bk_ref, bvt_ref,
               cu_ref, sm_ref, smt_ref, gate_ref):
    x = x_ref[...]
    ms = jnp.mean(x * x, axis=-1, keepdims=True)
    y = x * lax.rsqrt(ms + NORM_EPS) * g_ref[...]
    h = (y * (1.0 + sc_ref[...]) + sh_ref[...]).astype(BF16)
    tm = x.shape[0]

    def mm(a, b):
        return jnp.dot(h, w_ref[:, a:b], preferred_element_type=F32)

    cos = cos_ref[...]
    sin = sin_ref[...]
    lane = _lane_iota((tm, LANES))
    low = lane < HEAD_DIM
    first_half = (lane & HALF) == 0

    def rope(z):
        swapped = jnp.where(first_half, pltpu.roll(z, LANES - HALF, 1), pltpu.roll(z, HALF, 1))
        return z * cos + swapped * sin

    zeros64 = jnp.zeros((HEAD_DIM, tm), BF16)
    ones_rows = jnp.where(_row_iota((V_ROWS - HEAD_DIM, tm)) == 0, 1.0, 0.0).astype(BF16)

    def put_heads(ref, zt, p, slot_even, slot_odd):
        for hh, slot in ((0, slot_even), (1, slot_odd)):
            base = (2 * p + hh) * LANES
            ref[base + slot * HEAD_DIM:base + (slot + 1) * HEAD_DIM, :] = zt[hh * HEAD_DIM:(hh + 1) * HEAD_DIM]
            ref[base + (1 - slot) * HEAD_DIM:base + (2 - slot) * HEAD_DIM, :] = zeros64

    z = mm(C_AQ, C_IQ)
    for p in range(3):
        zt = (rope(z[:, p * LANES:(p + 1) * LANES]) * QK_SCALE).T.astype(BF16)
        put_heads(aqt_ref, zt, p, 0, 0)
    z = mm(C_IQ, C_A)
    for p in range(2):
        zt = rope(z[:, p * LANES:(p + 1) * LANES]).T.astype(BF16)
        put_heads(iqt_ref, zt, p, 1, 1)

    z = mm(C_A, C_B)
    kv = z[:, :LANES]
    r0 = jnp.where(low, rope(kv), kv)
    r1 = rope(z[:, LANES:])
    r0t = r0.T
    nat_ref[0:LANES, :] = r0t
    nat_ref[LANES:, :] = r1.T[0:HEAD_DIM, :]
    kik_ref[...] = jnp.where(low, r0, pltpu.roll(r1, HEAD_DIM, 1)).astype(BF16)
    avt_ref[0:HEAD_DIM, :] = r0t[HEAD_DIM:, :].astype(BF16)
    avt_ref[HEAD_DIM:, :] = ones_rows

    z = mm(C_B, C_CU)
    bk_ref[...] = z[:, 384:768].astype(BF16)
    for p in range(3):
        zt = (z[:, p * LANES:(p + 1) * LANES] * QK_SCALE).T.astype(BF16)
        put_heads(bqt_ref, zt, p, 0, 1)
        nbt_ref[p * LANES:(p + 1) * LANES, :] = z[:, 384 + p * LANES:384 + (p + 1) * LANES].T
        vt = z[:, 768 + p * LANES:768 + (p + 1) * LANES].T
        nbt_ref[384 + p * LANES:384 + (p + 1) * LANES, :] = vt
        vt = vt.astype(BF16)
        for hh in range(2):
            bvt_ref[2 * p + hh, 0:HEAD_DIM, :] = vt[hh * HEAD_DIM:(hh + 1) * HEAD_DIM]
            bvt_ref[2 * p + hh, HEAD_DIM:, :] = ones_rows

    cu_ref[...] = mm(C_CU, C_SM)

    z = mm(C_SM, C_GATE)
    t = z + bf_ref[...]
    logf = jnp.minimum(t, 0.0) - jnp.log1p(jnp.exp(-jnp.abs(t)))
    sm = jnp.where(lane < B_HEADS, logf, z)
    sm_ref[...] = sm
    smt_ref[...] = sm.T[0:16, :]

    for c in range(3):
        gate_ref[:, c * D_MODEL:(c + 1) * D_MODEL] = jax.nn.sigmoid(
            mm(C_GATE + c * D_MODEL, C_GATE + (c + 1) * D_MODEL)).astype(BF16)


def _in_weights(w_in_l, b_forget_l):
    d = w_in_l.shape[0]
    sizes = (384, 64, 64, 256, 4, 64, 384, 384, 384, 6, 256, 3072)
    o = np.concatenate([[0], np.cumsum(sizes)])
    w_in_l = w_in_l.astype(BF16)
    cols = [w_in_l[:, o[0]:o[1]],
            w_in_l[:, o[3]:o[4]],
            w_in_l[:, o[1]:o[3]], w_in_l[:, o[5]:o[6]], jnp.zeros((d, 64), BF16),
            w_in_l[:, o[6]:o[9]],
            w_in_l[:, o[10]:o[11]],
            w_in_l[:, o[9]:o[10]], jnp.zeros((d, IW_LANE - B_HEADS), BF16), w_in_l[:, o[4]:o[5]],
            jnp.zeros((d, LANES - IW_LANE - IDX_HEADS), BF16),
            w_in_l[:, o[11]:o[12]]]
    w = jnp.concatenate(cols, axis=1)
    assert w.shape[1] == C_END, w.shape
    bias = jnp.concatenate([b_forget_l, jnp.zeros((LANES - B_HEADS,), F32)]).reshape(1, LANES)
    return w, bias


def _in_proj(x, sc, sh, g, w, bias, cos, sin, tm):
    bsz, t, d = x.shape
    mrows = sc.shape[1]
    mblk = 1 if mrows == 1 else tm
    mod_spec = pl.BlockSpec((None, mblk, d), (lambda b, i: (b, 0, 0)) if mrows == 1 else (lambda b, i: (b, i, 0)))
    nblk = t // tm

    def rows(n, dt):
        return pl.BlockSpec((None, tm, n), lambda b, i: (b, i, 0)), jax.ShapeDtypeStruct((bsz, t, n), dt)

    def cols(n, dt):
        return pl.BlockSpec((None, n, tm), lambda b, i: (b, 0, i)), jax.ShapeDtypeStruct((bsz, n, t), dt)

    outs = [cols(A_HEADS * LANES, BF16), cols(IDX_HEADS * LANES, BF16), cols(192, F32), rows(LANES, BF16),
            (pl.BlockSpec((None, None, V_ROWS, tm), lambda b, i: (b, i, 0, 0)),
             jax.ShapeDtypeStruct((bsz, nblk, V_ROWS, tm), BF16)),
            cols(768, F32), cols(B_HEADS * LANES, BF16), rows(384, BF16),
            (pl.BlockSpec((None, B_HEADS, None, V_ROWS, tm), lambda b, i: (b, 0, i, 0, 0)),
             jax.ShapeDtypeStruct((bsz, B_HEADS, nblk, V_ROWS, tm), BF16)),
            rows(256, F32), rows(LANES, F32), cols(16, F32), rows(3 * D_MODEL, BF16)]
    return pl.pallas_call(
        _in_kernel,
        grid=(bsz, nblk),
        in_specs=[pl.BlockSpec((None, tm, d), lambda b, i: (b, i, 0)), mod_spec, mod_spec,
                  _const_spec((1, d)), _const_spec(w.shape), _const_spec((1, LANES)),
                  pl.BlockSpec((tm, LANES), lambda b, i: (i, 0)),
                  pl.BlockSpec((tm, LANES), lambda b, i: (i, 0))],
        out_specs=[o[0] for o in outs],
        out_shape=[o[1] for o in outs],
        compiler_params=_cparams(("parallel", "parallel")),
        name="in_proj",
    )(x, sc, sh, g.reshape(1, d), w, bias, cos, sin)


def _cum_kernel(x_ref, o_ref, carry_ref):
    @pl.when(pl.program_id(1) == 0)
    def _():
        carry_ref[...] = jnp.zeros_like(carry_ref)

    tc = x_ref.shape[0]
    tri = jnp.where(_lane_iota((tc, tc)) <= _row_iota((tc, tc)), 1.0, 0.0).astype(BF16)
    cum = carry_ref[0:1, :]
    rest = x_ref[...]
    for _ in range(3):
        piece = rest.astype(BF16)
        cum = cum + jnp.dot(tri, piece, preferred_element_type=F32)
        rest = rest - piece.astype(F32)
    carry_ref[...] = jnp.broadcast_to(cum[tc - 1:tc, :], carry_ref.shape)
    pieces = []
    rest = cum * LOG2E
    for _ in range(3):
        piece = rest.astype(BF16)
        pieces.append(piece.astype(F32))
        rest = rest - pieces[-1]
    lane = _lane_iota((tc, LANES))
    for p in range(B_HEADS // 2):
        slab = jnp.zeros((tc, LANES), F32)
        for hh in range(2):
            for j, piece in enumerate(pieces):
                dst, src = 3 * hh + j, 2 * p + hh
                slab = jnp.where(lane == dst, pltpu.roll(piece, (dst - src) % LANES, 1), slab)
        o_ref[p] = slab.astype(BF16)


def _cum_logf(x, tc):
    bsz, t, n = x.shape
    return pl.pallas_call(
        _cum_kernel,
        grid=(bsz, t // tc),
        in_specs=[pl.BlockSpec((None, tc, n), lambda b, i: (b, i, 0))],
        out_specs=pl.BlockSpec((None, B_HEADS // 2, tc, LANES), lambda b, i: (b, 0, i, 0)),
        out_shape=jax.ShapeDtypeStruct((bsz, B_HEADS // 2, t, LANES), BF16),
        scratch_shapes=[pltpu.VMEM((SUBLANES, LANES), F32)],
        compiler_params=_cparams(("parallel", "arbitrary")),
        name="logf_cumsum",
    )(x)


def _sum_keys(x):
    part = x.reshape(x.shape[0] // SUBLANES, SUBLANES, x.shape[1]).sum(axis=0)
    return jnp.sum(part, axis=0, keepdims=True)


def _max_keys(x):
    part = x.reshape(x.shape[0] // SUBLANES, SUBLANES, x.shape[1]).max(axis=0)
    return jnp.max(part, axis=0, keepdims=True)


def _dsa_kernel(aqt_ref, iqt_ref, smt_ref, kik_ref, avt_ref, o_ref, key_ref,
                *, tq, tk, n_keys, q_pos0, n_sel):
    i = pl.program_id(1)
    pos_first = q_pos0 + i * tq
    last_chunk = (pos_first + tq - 1) // CHUNK
    n_adm = jnp.minimum((last_chunk + 1) * CHUNK, n_keys)
    nkb = (n_adm + tk - 1) // tk

    q_pos = pos_first + _lane_iota((1, tq))
    q_lim = jnp.minimum((q_pos // CHUNK + 1) * CHUNK, n_keys)
    key_row = _row_iota((tk, tq))

    def keys(kb):
        return kik_ref[pl.ds(pl.multiple_of(kb * tk, tk), tk), :]

    iq4 = jnp.concatenate([iqt_ref[hd * LANES:(hd + 1) * LANES, :] for hd in range(IDX_HEADS)], axis=1)
    smt = smt_ref[...]
    w_rows = [smt[IW_LANE + hd:IW_LANE + hd + 1, :] for hd in range(IDX_HEADS)]

    def score_body(kb, carry):
        s4 = jnp.dot(keys(kb), iq4, preferred_element_type=F32)
        score = w_rows[0] * jnp.maximum(s4[:, 0:tq], 0.0)
        for hd in range(1, IDX_HEADS):
            score = score + w_rows[hd] * jnp.maximum(s4[:, hd * tq:(hd + 1) * tq], 0.0)
        score = jnp.where(key_row < q_lim - kb * tk, score, NEG_INF)
        key_ref[kb] = score.astype(KEY_DT)
        return carry

    lax.fori_loop(0, nkb, score_body, 0)

    one, zero = jnp.ones((), KEY_DT), jnp.zeros((), KEY_DT)
    packed_rows = 2 * SUBLANES

    def count(cand, strict):
        def hits(blk):
            cols = []
            for c0 in range(0, tq, 2 * LANES):
                c1 = min(c0 + 2 * LANES, tq)
                h = jnp.where((blk[:, c0:c1] > cand[:, c0:c1]) if strict else (blk[:, c0:c1] >= cand[:, c0:c1]),
                              one, zero)
                parts = [h[r * packed_rows:(r + 1) * packed_rows] for r in range(tk // packed_rows)]
                while len(parts) > 1:
                    parts = [a + b for a, b in zip(parts[::2], parts[1::2])]
                cols.append(parts[0].astype(F32))
            return cols[0] if len(cols) == 1 else jnp.concatenate(cols, axis=1)

        def pair(j, acc):
            return acc + hits(key_ref[2 * j]) + hits(key_ref[2 * j + 1])

        acc = lax.fori_loop(0, nkb // 2, pair, jnp.zeros((packed_rows, tq), F32))
        acc = lax.fori_loop(2 * (nkb // 2), nkb, lambda kb, a: a + hits(key_ref[kb]), acc)
        return jnp.sum(acc, axis=0, keepdims=True)

    def pattern_value(u):
        bits = jnp.where(u >= RANK_ZERO, u - RANK_ZERO, (~u) & 0xFFFF)
        return lax.bitcast_convert_type(lax.shift_left(bits, 16), F32).astype(KEY_DT)

    def bit_body(b, u):
        cand_u = u | lax.shift_left(jnp.int32(1), 15 - b)
        cnt = count(pattern_value(cand_u), False)
        return jnp.where(cnt >= n_sel, cand_u, u)

    u_thr = jnp.maximum(lax.fori_loop(0, 16, bit_body, jnp.zeros((1, tq), I32)), RANK_NEG_INF)
    thr_16 = pattern_value(u_thr)

    def finer():
        lo = thr_16.astype(F32)
        hi = jnp.where(u_thr >= RANK_POS_INF, jnp.inf, pattern_value(u_thr + 1).astype(F32))

        def halve(_, lo_hi):
            lo, hi = lo_hi
            cand = (0.5 * lo + 0.5 * hi).astype(KEY_DT)
            enough = count(cand, False) >= n_sel
            return jnp.where(enough, cand.astype(F32), lo), jnp.where(enough, hi, cand.astype(F32))

        thr_f = lax.fori_loop(0, 16, halve, (lo, hi))[0].astype(KEY_DT)
        return thr_f, count(thr_f, True)

    above_16 = count(thr_16, True)
    thr_k, above = lax.cond(jnp.min(n_sel - above_16) <= 0.0, finer, lambda: (thr_16, above_16))
    need = jnp.maximum(n_sel - above, 0.0)
    thr = thr_k.astype(F32)

    aq6 = jnp.concatenate([aqt_ref[hd * LANES:(hd + 1) * LANES, :] for hd in range(A_HEADS)], axis=1)

    half = tk // 2
    lower = jnp.where(_lane_iota((half, half)) <= _row_iota((half, half)), 1.0, 0.0).astype(BF16)

    def attend():
        def body(kb, carry):
            eq_seen, ms, accs = carry
            blk = key_ref[kb].astype(F32)
            eq = blk == thr
            eq_f = jnp.where(eq, 1.0, 0.0)
            prefs = []
            for e in (eq_f[:half], eq_f[half:]):
                prefs.append(jnp.dot(lower, e.astype(BF16), preferred_element_type=F32) + eq_seen)
                eq_seen = eq_seen + _sum_keys(e)
            slack = jnp.where(blk >= thr, need - jnp.where(eq, jnp.concatenate(prefs, axis=0), 0.0), -1.0)
            bias = jnp.where(slack >= 0.0, jnp.where(jnp.abs(blk) < jnp.inf, 0.0, NEG_INF), NEG_INF)
            logits = jnp.dot(keys(kb), aq6, preferred_element_type=F32)
            vts = (avt_ref[2 * kb], avt_ref[2 * kb + 1])
            new_ms, new_accs = [], []
            for p in range(A_HEADS // 2):
                ps, alphas = [], []
                for hd in (2 * p, 2 * p + 1):
                    lg = logits[:, hd * tq:(hd + 1) * tq] + bias
                    m_old = ms[hd]
                    m_new = jnp.maximum(m_old, _max_keys(lg))
                    m_safe = jnp.where(m_new == NEG_INF, 0.0, m_new)
                    ps.append(jnp.exp2(lg - m_safe).astype(BF16))
                    alphas.append(jnp.exp2(m_old - m_safe))
                    new_ms.append(m_new)
                p2 = jnp.concatenate(ps, axis=1)
                pv = (jnp.dot(vts[0], p2[:tk // 2], preferred_element_type=F32)
                      + jnp.dot(vts[1], p2[tk // 2:], preferred_element_type=F32))
                new_accs.append(jnp.concatenate(alphas, axis=1) * accs[p] + pv)
            return eq_seen, tuple(new_ms), tuple(new_accs)

        init = (jnp.zeros((1, tq), F32),
                tuple(jnp.full((1, tq), NEG_INF, F32) for _ in range(A_HEADS)),
                tuple(jnp.zeros((V_ROWS, 2 * tq), F32) for _ in range(A_HEADS // 2)))
        return lax.fori_loop(0, nkb, body, init)[2]

    accs = attend()

    outs = []
    for p in range(A_HEADS // 2):
        o2 = accs[p][0:HEAD_DIM] / accs[p][HEAD_DIM:HEAD_DIM + 1]
        outs += [o2[:, 0:tq], o2[:, tq:2 * tq]]
    o_ref[...] = jnp.concatenate(outs, axis=0).T.astype(o_ref.dtype)


def _dsa(aqt, iqt, smt, kik, avt, *, tq, n_keys, q_pos0, n_sel):
    bsz, _, t_q = aqt.shape
    _, nblk, _, tkv = avt.shape
    assert nblk % 2 == 0
    tk = 2 * tkv
    kern = functools.partial(_dsa_kernel, tq=tq, tk=tk, n_keys=n_keys, q_pos0=q_pos0, n_sel=n_sel)
    return pl.pallas_call(
        kern,
        grid=(bsz, t_q // tq),
        in_specs=[pl.BlockSpec((None, A_HEADS * LANES, tq), lambda b, i: (b, 0, i)),
                  pl.BlockSpec((None, IDX_HEADS * LANES, tq), lambda b, i: (b, 0, i)),
                  pl.BlockSpec((None, 16, tq), lambda b, i: (b, 0, i)),
                  pl.BlockSpec((None, nblk * tkv, LANES), lambda b, i: (b, 0, 0)),
                  pl.BlockSpec((None, nblk, V_ROWS, tkv), lambda b, i: (b, 0, 0, 0))],
        out_specs=pl.BlockSpec((None, tq, 384), lambda b, i: (b, i, 0)),
        out_shape=jax.ShapeDtypeStruct((bsz, t_q, 384), BF16),
        scratch_shapes=[pltpu.VMEM((nblk // 2, tk, tq), KEY_DT)],
        compiler_params=_cparams(("parallel", "arbitrary")),
        name="dsa_attention",
    )(aqt, iqt, smt, kik, avt)


def _fox_kernel(qt_ref, k_ref, vt_ref, ck_ref, o_ref, *, tq, tk, n_keys, q_pos0):
    i = pl.program_id(1)
    pos_first = q_pos0 + i * tq
    n_full = pos_first // tk
    nkb = (jnp.minimum(pos_first + tq, n_keys) + tk - 1) // tk
    q_pos = pos_first + _lane_iota((1, tq))
    key_row = _row_iota((tk, tq))
    piece_row = _row_iota((LANES, tq))
    qts = []
    for hd in range(B_HEADS):
        minus = jnp.where((piece_row >= 3 * (hd % 2)) & (piece_row < 3 * (hd % 2) + 3), -1.0, 0.0).astype(BF16)
        qts.append(jnp.concatenate([qt_ref[hd * LANES:(hd + 1) * LANES, :], minus], axis=0))

    def step(kbs, state, masked):
        logits = []
        for hd in range(B_HEADS):
            for kb in kbs:
                rows = pl.ds(pl.multiple_of(kb * tk, tk), tk)
                kblk = jnp.concatenate([k_ref[rows, (hd // 2) * LANES:(hd // 2 + 1) * LANES],
                                        ck_ref[hd // 2, rows, :]], axis=1)
                lg = jnp.dot(kblk, qts[hd], preferred_element_type=F32)
                if masked:
                    lg = jnp.where(key_row <= q_pos - kb * tk, lg, NEG_INF)
                logits.append(lg)
        new = []
        for hd in range(B_HEADS):
            m_old, acc = state[hd]
            lgs = logits[hd * len(kbs):(hd + 1) * len(kbs)]
            m_new = m_old
            for lg in lgs:
                m_new = jnp.maximum(m_new, _max_keys(lg))
            m_safe = jnp.where(m_new == NEG_INF, 0.0, m_new) if masked else m_new
            acc = jnp.exp2(m_old - m_safe) * acc
            for kb, lg in zip(kbs, lgs):
                p = jnp.exp2(lg - m_safe).astype(BF16)
                acc = acc + jnp.dot(vt_ref[hd, kb], p, preferred_element_type=F32)
            new.append((m_new, acc))
        return tuple(new)

    init = tuple((jnp.full((1, tq), NEG_INF, F32), jnp.zeros((V_ROWS, tq), F32)) for _ in range(B_HEADS))
    state = lax.fori_loop(0, n_full // 2, lambda j, st: step((2 * j, 2 * j + 1), st, False), init)
    state = lax.fori_loop(2 * (n_full // 2), nkb, lambda kb, st: step((kb,), st, True), state)
    outs = [acc[0:HEAD_DIM] / acc[HEAD_DIM:HEAD_DIM + 1] for _, acc in state]
    o_ref[...] = jnp.concatenate(outs, axis=0).T.astype(o_ref.dtype)


def _fox(qt, k, vt, ck, *, tq, n_keys, q_pos0):
    bsz, _, t_q = qt.shape
    _, _, nblk, _, tk = vt.shape
    lpad = nblk * tk
    kern = functools.partial(_fox_kernel, tq=tq, tk=tk, n_keys=n_keys, q_pos0=q_pos0)
    return pl.pallas_call(
        kern,
        grid=(bsz, t_q // tq),
        in_specs=[pl.BlockSpec((None, B_HEADS * LANES, tq), lambda b, i: (b, 0, i)),
                  pl.BlockSpec((None, lpad, 384), lambda b, i: (b, 0, 0)),
                  pl.BlockSpec((None, B_HEADS, nblk, V_ROWS, tk), lambda b, i: (b, 0, 0, 0, 0)),
                  pl.BlockSpec((None, B_HEADS // 2, lpad, LANES), lambda b, i: (b, 0, 0, 0))],
        out_specs=pl.BlockSpec((None, tq, 384), lambda b, i: (b, i, 0)),
        out_shape=jax.ShapeDtypeStruct((bsz, t_q, 384), BF16),
        compiler_params=_cparams(("parallel", "arbitrary")),
        name="fox_attention",
    )(qt, k, vt, ck)


def _pool_kernel(cur_ref, prev_ref, hist_ref, w_ref, s_ref, o_ref, ext, *, tc, start_pos):
    i = pl.program_id(1)
    cur = cur_ref[...]
    ext[0:16, :] = jnp.where(i == 0, hist_ref[...], prev_ref[tc - 16:, :])
    ext[16:, :] = cur
    pos = start_pos + i * tc + lax.broadcasted_iota(I32, (tc, POOL_WIDTH), 0)
    lane = _lane_iota((tc, POOL_WIDTH))
    run = cur
    pooled = jnp.zeros_like(cur)
    k = 1
    for g, w in enumerate(POOL_WINDOWS):
        while k < w:
            run = run + ext[16 - k:16 - k + tc, :]
            k += 1
        cnt = jnp.minimum(pos + 1, w).astype(F32)
        in_group = (lane >= g * POOL_GROUP_DIM) & (lane < (g + 1) * POOL_GROUP_DIM)
        pooled = jnp.where(in_group, run / cnt, pooled)
    z = (pooled - cur).astype(BF16)
    o_ref[...] = (jnp.dot(z, w_ref[...], preferred_element_type=F32) * s_ref[...]).astype(o_ref.dtype)


def _pool(cu, hist16, w_bd, scale, *, tc, start_pos):
    bsz, t, n = cu.shape
    kern = functools.partial(_pool_kernel, tc=tc, start_pos=start_pos)
    return pl.pallas_call(
        kern,
        grid=(bsz, t // tc),
        in_specs=[pl.BlockSpec((None, tc, n), lambda b, i: (b, i, 0)),
                  pl.BlockSpec((None, tc, n), lambda b, i: (b, jnp.maximum(i - 1, 0), 0)),
                  pl.BlockSpec((None, 16, n), lambda b, i: (b, 0, 0)),
                  _const_spec((n, n)), _const_spec((1, n))],
        out_specs=pl.BlockSpec((None, tc, n), lambda b, i: (b, i, 0)),
        out_shape=jax.ShapeDtypeStruct((bsz, t, n), BF16),
        scratch_shapes=[pltpu.VMEM((16 + tc, n), F32)],
        compiler_params=_cparams(("parallel", "arbitrary")),
        name="pool_mixer",
    )(cu, cu, hist16, w_bd, scale.reshape(1, n))


def _route(logits):
    lane = _lane_iota(logits.shape).astype(F32)
    lg = jnp.where(lane < N_EXPERTS, logits, NEG_INF)
    m1 = jnp.max(lg, axis=1, keepdims=True)
    i1 = jnp.min(jnp.where(lg == m1, lane, float(LANES)), axis=1, keepdims=True)
    hot1 = lane == i1
    lg2 = jnp.where(hot1, NEG_INF, lg)
    m2 = jnp.max(lg2, axis=1, keepdims=True)
    i2 = jnp.min(jnp.where(lg2 == m2, lane, float(LANES)), axis=1, keepdims=True)
    hot2 = lane == i2
    e2 = jnp.exp(m2 - m1)
    den = 1.0 + e2
    return jnp.where(hot1, 1.0 / den, 0.0) + jnp.where(hot2, e2 / den, 0.0)


def _final_norm(x, gain):
    ms = jnp.mean(x * x, axis=-1, keepdims=True)
    return x * lax.rsqrt(ms + NORM_EPS) * gain


def _merge_kernel(x_ref, oa_ref, ob_ref, oc_ref, gate_ref, g1_ref, sc2_ref, sh2_ref, g_ref,
                  wa_ref, wb_ref, wc_ref, wo_ref, *rest, moe, final):
    if moe:
        rw_ref, rb_ref, xo_ref, h_ref, gw_ref = rest
    elif final:
        g2_ref, wg_ref, wu_ref, wd_ref, fg_ref, xo_ref = rest
    else:
        g2_ref, wg_ref, wu_ref, wd_ref, xo_ref = rest
    d = D_MODEL
    merged = (gate_ref[:, 0:d] * jnp.dot(oa_ref[...], wa_ref[...], preferred_element_type=F32)
              + gate_ref[:, d:2 * d] * jnp.dot(ob_ref[...], wb_ref[...], preferred_element_type=F32)
              + gate_ref[:, 2 * d:3 * d] * jnp.dot(oc_ref[...], wc_ref[...], preferred_element_type=F32))
    x = x_ref[...] + g1_ref[...] * jnp.dot(merged.astype(BF16), wo_ref[...], preferred_element_type=F32)
    ms = jnp.mean(x * x, axis=-1, keepdims=True)
    y = x * lax.rsqrt(ms + NORM_EPS) * g_ref[...]
    h = y * (1.0 + sc2_ref[...]) + sh2_ref[...]
    if not moe:
        hb = h.astype(BF16)
        n_chunks = 2
        tf = wg_ref.shape[1] // n_chunks
        acc = jnp.zeros(x.shape, F32)
        for c in range(n_chunks):
            gt = jnp.dot(hb, wg_ref[:, c * tf:(c + 1) * tf], preferred_element_type=F32)
            up = jnp.dot(hb, wu_ref[:, c * tf:(c + 1) * tf], preferred_element_type=F32)
            act = (gt * jax.nn.sigmoid(gt) * up).astype(BF16)
            acc = acc + jnp.dot(act, wd_ref[c * tf:(c + 1) * tf, :], preferred_element_type=F32)
        x = x + g2_ref[...] * acc
        xo_ref[...] = _final_norm(x, fg_ref[...]) if final else x
    else:
        xo_ref[...] = x
        h_ref[...] = h.astype(BF16)
        h_hi = h.astype(BF16)
        h_lo = (h - h_hi.astype(F32)).astype(BF16)
        logits = (jnp.dot(h_hi, rw_ref[0], preferred_element_type=F32)
                  + jnp.dot(h_lo, rw_ref[0], preferred_element_type=F32)
                  + jnp.dot(h_hi, rw_ref[1], preferred_element_type=F32)) + rb_ref[...]
        gw_ref[...] = _route(logits)


def _merge(x, oa, ob, oc, gates, g1, sc2, sh2, g, wa, wb, wc, wo, router, ffn, tm):
    assert (router is None) != (ffn is None)
    bsz, t, d = x.shape
    mrows = g1.shape[1]
    mblk = 1 if mrows == 1 else tm
    mod_spec = pl.BlockSpec((None, mblk, d), (lambda b, i: (b, 0, 0)) if mrows == 1 else (lambda b, i: (b, i, 0)))

    def tok(n):
        return pl.BlockSpec((None, tm, n), lambda b, i: (b, i, 0))

    in_specs = [tok(d), tok(384), tok(384), tok(256), tok(3 * d), mod_spec, mod_spec, mod_spec,
                _const_spec((1, d)), _const_spec(wa.shape), _const_spec(wb.shape), _const_spec(wc.shape),
                _const_spec(wo.shape)]
    args = [x, oa, ob, oc, gates, g1, sc2, sh2, g.reshape(1, d), wa, wb, wc, wo]
    if router is not None:
        rw, rb = router
        in_specs += [_const_spec(rw.shape), _const_spec(rb.shape)]
        args += [rw, rb]
        out_specs = [tok(d), tok(d), tok(LANES)]
        out_shape = [jax.ShapeDtypeStruct((bsz, t, d), F32), jax.ShapeDtypeStruct((bsz, t, d), BF16),
                     jax.ShapeDtypeStruct((bsz, t, LANES), F32)]
    else:
        g2, wg, wu, wd, final_g = ffn
        in_specs += [mod_spec, _const_spec(wg.shape), _const_spec(wu.shape), _const_spec(wd.shape)]
        args += [g2, wg, wu, wd]
        if final_g is not None:
            in_specs.append(_const_spec((1, d)))
            args.append(final_g.reshape(1, d))
        out_specs = [tok(d)]
        out_shape = [jax.ShapeDtypeStruct((bsz, t, d), F32)]
    return pl.pallas_call(
        functools.partial(_merge_kernel, moe=router is not None, final=ffn is not None and ffn[4] is not None),
        grid=(bsz, t // tm),
        in_specs=in_specs, out_specs=out_specs, out_shape=out_shape,
        compiler_params=_cparams(("parallel", "parallel")),
        name="merge_out" if router is not None else "merge_ffn",
    )(*args)


def _moe_kernel(x_ref, h_ref, g2_ref, gw_ref, wg_ref, wu_ref, wd_ref, *rest, final):
    if final:
        fg_ref, o_ref, acc_ref, posc_ref, posr_ref = rest
    else:
        o_ref, acc_ref, posc_ref, posr_ref = rest
    e = pl.program_id(2)
    tm = h_ref.shape[0]
    n_slabs = tm // MOE_CHUNK

    @pl.when(e == 0)
    def _():
        acc_ref[...] = jnp.zeros_like(acc_ref)
        routed = gw_ref[...] != 0.0
        r_f = jnp.where(routed, 1.0, 0.0)
        r_b = r_f.astype(BF16)
        r_t = r_f.T
        r_tb = r_t.astype(BF16)
        tok_l = _lane_iota((MOE_CHUNK, tm))
        tok_r = _row_iota((MOE_CHUNK, tm))
        rank_r = jnp.zeros((LANES, tm), F32)
        for s in range(n_slabs):
            rows = slice(s * MOE_CHUNK, (s + 1) * MOE_CHUNK)
            earlier = jnp.where(tok_l < tok_r + s * MOE_CHUNK, 1.0, 0.0).astype(BF16)
            rank_c = jnp.dot(earlier, r_b, preferred_element_type=F32)
            posc_ref[rows, :] = jnp.where(routed[rows], rank_c, -1.0)
            later = jnp.where(tok_r + s * MOE_CHUNK < tok_l, 1.0, 0.0).astype(BF16)
            rank_r = rank_r + jnp.dot(r_tb[:, rows], later, preferred_element_type=F32)
        posr_ref[...] = jnp.where(r_t != 0.0, rank_r, -1.0)

    lane_e = _lane_iota((tm, LANES)) == e
    pos_c = jnp.sum(jnp.where(lane_e, posc_ref[...], 0.0), axis=1, keepdims=True)
    gate_c = jnp.sum(jnp.where(lane_e, gw_ref[...], 0.0), axis=1, keepdims=True)
    pos_r = posr_ref[pl.ds(e, 1), :]
    n_routed = (jnp.max(pos_r) + 1.0).astype(I32)

    def run_chunk(base, n_rows):
        slot_rows = _row_iota((n_rows, tm)).astype(F32)
        slot_lanes = _lane_iota((MOE_CHUNK, n_rows)).astype(F32)
        pack = jnp.where(pos_r - base == slot_rows, 1.0, 0.0).astype(BF16)
        xc = jnp.dot(pack, h_ref[...], preferred_element_type=F32).astype(BF16)
        gt = jnp.dot(xc, wg_ref[...], preferred_element_type=F32)
        up = jnp.dot(xc, wu_ref[...], preferred_element_type=F32)
        act = (gt * jax.nn.sigmoid(gt) * up).astype(BF16)
        y = jnp.dot(act, wd_ref[...], preferred_element_type=F32).astype(BF16)
        for s in range(n_slabs):
            rows = slice(s * MOE_CHUNK, (s + 1) * MOE_CHUNK)
            unpack = jnp.where(pos_c[rows] - base == slot_lanes, 1.0, 0.0).astype(BF16)
            acc_ref[rows, :] += gate_c[rows] * jnp.dot(unpack, y, preferred_element_type=F32)

    def first(c, carry):
        run_chunk(0.0, MOE_CHUNK)
        return carry

    def later(c, carry):
        run_chunk((MOE_CHUNK + c * (MOE_CHUNK // 2)).astype(F32), MOE_CHUNK // 2)
        return carry

    lax.fori_loop(0, jnp.minimum(n_routed, 1), first, 0)
    n_later = (jnp.maximum(n_routed - MOE_CHUNK, 0) + MOE_CHUNK // 2 - 1) // (MOE_CHUNK // 2)
    lax.fori_loop(0, n_later, later, 0)

    @pl.when(e == pl.num_programs(2) - 1)
    def _():
        x = x_ref[...] + g2_ref[...] * acc_ref[...]
        o_ref[...] = _final_norm(x, fg_ref[...]) if final else x


def _moe(x, h, g2, gw, wg, wu, wd, final_g, tm):
    bsz, t, d = x.shape
    n_e, _, dff = wg.shape
    mrows = g2.shape[1]
    mblk = 1 if mrows == 1 else tm
    mod_spec = pl.BlockSpec((None, mblk, d), (lambda b, i, e: (b, 0, 0)) if mrows == 1 else (lambda b, i, e: (b, i, 0)))
    tok = pl.BlockSpec((None, tm, d), lambda b, i, e: (b, i, 0))
    tok_once = pl.BlockSpec((None, tm, d), lambda b, i, e: (b, i, 0), pipeline_mode=pl.Buffered(1))
    in_specs = [tok_once, tok, mod_spec, pl.BlockSpec((None, tm, LANES), lambda b, i, e: (b, i, 0)),
                pl.BlockSpec((None, d, dff), lambda b, i, e: (e, 0, 0)),
                pl.BlockSpec((None, d, dff), lambda b, i, e: (e, 0, 0)),
                pl.BlockSpec((None, dff, d), lambda b, i, e: (e, 0, 0))]
    args = [x, h, g2, gw, wg, wu, wd]
    if final_g is not None:
        in_specs.append(pl.BlockSpec((1, d), lambda b, i, e: (0, 0)))
        args.append(final_g.reshape(1, d))
    return pl.pallas_call(
        functools.partial(_moe_kernel, final=final_g is not None),
        grid=(bsz, t // tm, n_e),
        in_specs=in_specs, out_specs=tok,
        out_shape=jax.ShapeDtypeStruct((bsz, t, d), F32),
        scratch_shapes=[pltpu.VMEM((tm, d), F32), pltpu.VMEM((tm, LANES), F32), pltpu.VMEM((LANES, tm), F32)],
        compiler_params=_cparams(("parallel", "parallel", "arbitrary")),
        name="moe_routed",
    )(*args)


def _rope_tables(pos):
    inv = ROPE_THETA ** (-jnp.arange(HALF, dtype=F32) / HALF)
    ang = pos.astype(F32)[:, None] * inv[None, :]
    cos, sin = jnp.cos(ang), jnp.sin(ang)
    return jnp.tile(cos, (1, 4)), jnp.tile(jnp.concatenate([-sin, sin], axis=1), (1, 2))


def _pick_tile(n, pref):
    t = min(n, pref)
    while n % t:
        t //= 2
    return t


def _per_seq_cols(a, bsz, t, width):
    f = a.shape[1]
    a = jnp.moveaxis(a[0].reshape(f, bsz, t), 1, 0)
    return jnp.pad(a, ((0, 0), (0, 0), (0, width - t)))


def _value_blocks(past_vt, new_vt, bsz, t, lpad):
    lead = new_vt.shape[:-2]
    new_b = jnp.moveaxis(new_vt.reshape(*lead, V_ROWS, bsz, t), -2, 0)
    full = jnp.concatenate([past_vt, new_b], axis=-1)
    full = jnp.pad(full, [(0, 0)] * (full.ndim - 1) + [(0, lpad - full.shape[-1])])
    full = full.reshape(bsz, *lead, V_ROWS, lpad // KV_BLOCK, KV_BLOCK)
    return jnp.moveaxis(full, -2, -3)


def _with_ones_rows(vt):
    ones = jnp.ones(vt.shape[:-2] + (1, vt.shape[-1]), vt.dtype)
    zeros = jnp.zeros(vt.shape[:-2] + (V_ROWS - HEAD_DIM - 1, vt.shape[-1]), vt.dtype)
    return jnp.concatenate([vt, ones, zeros], axis=-2)


def _mixers(inp, past, n_past, lw, bsz, t):
    aqt, iqt, kik, avt, bqt, bk, bvt, cu, sm, smt = inp
    n_keys = n_past + t
    n_sel = min(TOPK_MAX, n_keys // 4)
    lpad = -(-n_keys // (2 * KV_BLOCK)) * 2 * KV_BLOCK
    if past is None:
        kik_all, avt_all, bk_all, bvt_all, logf_all = kik, avt, bk, bvt, sm
        hist16 = jnp.zeros((bsz, 16, POOL_WIDTH), F32)
        tq_a, tq_b, t_pad = _pick_tile(t, 512), _pick_tile(t, 256), t
    else:
        pa, pb, plf, pc = past
        pk, pv, pik = (pa[:, :, j].astype(BF16) for j in range(3))

        def join_rows(p, new):
            full = jnp.concatenate([p, new.reshape(bsz, t, new.shape[-1])], axis=1)
            return jnp.pad(full, ((0, 0), (0, lpad - n_keys), (0, 0)))

        kik_all = join_rows(jnp.concatenate([pk, pik], axis=-1), kik)
        avt_all = _value_blocks(_with_ones_rows(jnp.swapaxes(pv, 1, 2)), avt[0, 0], bsz, t, lpad)
        bk_all = join_rows(pb[:, :, 0].astype(BF16).reshape(bsz, n_past, 384), bk)
        pvt = jnp.transpose(pb[:, :, 1].astype(BF16), (0, 2, 3, 1))
        bvt_all = _value_blocks(_with_ones_rows(pvt), bvt[0, :, 0], bsz, t, lpad)
        logf_all = join_rows(jnp.pad(plf, ((0, 0), (0, 0), (0, LANES - B_HEADS))), sm)
        hist16 = jnp.pad(pc, ((0, 0), (1, 0), (0, 0)))
        tq_a = tq_b = t_pad = LANES
        aqt, iqt, bqt, smt = (_per_seq_cols(a, bsz, t, t_pad) for a in (aqt, iqt, bqt, smt))

    oa = _dsa(aqt, iqt, smt, kik_all, avt_all, tq=tq_a, n_keys=n_keys, q_pos0=n_past, n_sel=n_sel)
    ck = _cum_logf(logf_all, 2 * KV_BLOCK)
    ob = _fox(bqt, bk_all, bvt_all, ck, tq=tq_b, n_keys=n_keys, q_pos0=n_past)
    cu = cu.reshape(bsz, t, POOL_WIDTH)
    oc = _pool(cu, hist16, lw["pool_bd"], lw["pool_scale"], tc=_pick_tile(t, 1024), start_pos=n_past)
    return oa[:, :t], ob[:, :t], oc


def _layer(x, mod, past, n_past, pos_tab, lw, layer, final_g, per_token):
    bsz, t, d = x.shape
    sh1, sc1, g1, sh2, sc2, g2 = mod
    if per_token:
        xt = x.reshape(1, bsz * t, d)
        sh1, sc1, g1, sh2, sc2, g2 = (jnp.broadcast_to(m, (bsz, t, d)).reshape(1, bsz * t, d) for m in mod)
        cos, sin = (jnp.tile(a, (bsz, 1)) for a in pos_tab)
    else:
        xt = x
        cos, sin = pos_tab
    tm = KV_BLOCK
    (aqt, iqt, nat, kik, avt, nbt, bqt, bk, bvt, cu, sm, smt, gates) = _in_proj(
        xt, sc1, sh1, lw["norm_mix_g"], lw["w_in"], lw["bf_bias"], cos, sin, tm)
    oa, ob, oc = _mixers((aqt, iqt, kik, avt, bqt, bk, bvt, cu, sm, smt), past, n_past, lw, bsz, t)

    def flat(a):
        return a.reshape(xt.shape[0], xt.shape[1], a.shape[-1])

    router = (lw["router_w"], lw["router_b"]) if layer % 2 else None
    ffn = None if layer % 2 else (g2, lw["ffn_wg"], lw["ffn_wu"], lw["ffn_wd"], final_g)
    res = _merge(xt, flat(oa), flat(ob), flat(oc), gates, g1, sc2, sh2, lw["norm_ffn_g"],
                 lw["w_br_a"], lw["w_br_b"], lw["w_br_c"], lw["w_out"], router, ffn, _pick_tile(xt.shape[1], 512))
    if layer % 2 == 0:
        (x_new,) = res
    else:
        x_mid, h2, gw = res
        x_new = _moe(x_mid, h2, g2, gw, lw["moe_wg"], lw["moe_wu"], lw["moe_wd"], final_g,
                     _pick_tile(xt.shape[1], 1024))

    def token_major(a, *feat):
        a = a.reshape(a.shape[0], *feat, -1, t) if per_token else a.reshape(a.shape[0], *feat, 1, t)
        a = jnp.moveaxis(a, (-2, -1), (1, 2))
        return a.reshape(bsz, t, *feat)

    new_a = token_major(nat, 3, HEAD_DIM)
    new_b = token_major(nbt, 2, B_HEADS, HEAD_DIM)
    new_logf = sm.reshape(bsz, t, LANES)[:, :, :B_HEADS]
    new_pool = cu.reshape(bsz, t, POOL_WIDTH)[:, t - POOL_HIST:, :]
    return x_new.reshape(bsz, t, d), (new_a, new_b, new_logf, new_pool)


def kernel(x_prompt, x_sample, cache_a_kvi, cache_b_kv, cache_b_logf, state_c_pool, c_prompt, c_sample,
           ada_w, ada_b, norm_mix_g, w_in, b_forget, pool_w, pool_scale, w_br_a, w_br_b, w_br_c, w_out,
           norm_ffn_g, ffn_w_gate, ffn_w_up, ffn_w_down, moe_router_w, moe_router_b, moe_w_gate,
           moe_w_up, moe_w_down, final_norm_g):
    depth = ada_w.shape[0]
    bp, tp, d = x_prompt.shape
    bs, ts, _ = x_sample.shape
    n_past = cache_a_kvi.shape[2]
    assert tp % KV_BLOCK == 0 and (bs * ts) % KV_BLOCK == 0 and ts <= LANES

    rows = -(-(bp + bs) // 8) * 8
    c_all = jnp.pad(jnp.concatenate([c_prompt, c_sample], axis=0), ((0, rows - bp - bs), (0, 0)))
    mod_all = _ada(c_all, ada_w, ada_b)

    tab_p = _rope_tables(jnp.arange(tp))
    tab_s = _rope_tables(n_past + jnp.arange(ts))

    xp, xs = x_prompt, x_sample
    outs_p, outs_s = [], []
    for layer in range(depth):
        j = layer // 2
        w_l, bias_l = _in_weights(w_in[layer], b_forget[layer])
        pw = pool_w[layer]
        pool_bd = jnp.zeros((POOL_WIDTH, POOL_WIDTH), F32)
        for g in range(len(POOL_WINDOWS)):
            sl = slice(g * POOL_GROUP_DIM, (g + 1) * POOL_GROUP_DIM)
            pool_bd = pool_bd.at[sl, sl].set(pw[g])
        lw = dict(w_in=w_l, bf_bias=bias_l, norm_mix_g=norm_mix_g[layer], norm_ffn_g=norm_ffn_g[layer],
                  pool_bd=pool_bd.astype(BF16), pool_scale=pool_scale[layer],
                  w_br_a=w_br_a[layer].astype(BF16), w_br_b=w_br_b[layer].astype(BF16),
                  w_br_c=w_br_c[layer].astype(BF16), w_out=w_out[layer].astype(BF16))
        if layer % 2 == 0:
            lw.update(ffn_wg=ffn_w_gate[j].astype(BF16), ffn_wu=ffn_w_up[j].astype(BF16),
                      ffn_wd=ffn_w_down[j].astype(BF16))
        else:
            rw = jnp.pad(moe_router_w[j], ((0, 0), (0, LANES - N_EXPERTS)))
            rw_hi = rw.astype(BF16)
            lw.update(router_w=jnp.stack([rw_hi, (rw - rw_hi.astype(F32)).astype(BF16)]),
                      router_b=jnp.pad(moe_router_b[j], (0, LANES - N_EXPERTS)).reshape(1, LANES),
                      moe_wg=moe_w_gate[j].astype(BF16), moe_wu=moe_w_up[j].astype(BF16),
                      moe_wd=moe_w_down[j].astype(BF16))
        final_g = final_norm_g if layer == depth - 1 else None
        mod_p = [m[:, None, :] for m in jnp.split(mod_all[layer, :bp], 6, axis=-1)]
        mod_s = [m[:, None, :] for m in jnp.split(mod_all[layer, bp:bp + bs], 6, axis=-1)]
        xp, new_p = _layer(xp, mod_p, None, 0, tab_p, lw, layer, final_g, per_token=False)
        past = (cache_a_kvi[layer], cache_b_kv[layer], cache_b_logf[layer], state_c_pool[layer])
        xs, new_s = _layer(xs, mod_s, past, n_past, tab_s, lw, layer, final_g, per_token=True)
        outs_p.append(new_p)
        outs_s.append(new_s)

    def stack(outs, k):
        return jnp.stack([o[k] for o in outs])

    return (xp, xs,
            stack(outs_p, 0), stack(outs_p, 1), stack(outs_p, 2), stack(outs_p, 3),
            stack(outs_s, 0), stack(outs_s, 1), stack(outs_s, 2), stack(outs_s, 3))
```

```python
import functools

import jax
import jax.numpy as jnp
import numpy as np
from jax import lax
from jax.experimental import pallas as pl
from jax.experimental.pallas import tpu as pltpu

F32 = jnp.float32
BF16 = jnp.bfloat16
I32 = jnp.int32

D_MODEL = 1024
CHUNK = 64
HEAD_DIM = 64
HALF = HEAD_DIM // 2
ROPE_THETA = 10000.0
NORM_EPS = 1e-6
A_HEADS = 6
IDX_HEADS = 4
TOPK_MAX = 256
B_HEADS = 6
POOL_WINDOWS = (2, 4, 8, 16)
POOL_GROUP_DIM = 64
POOL_WIDTH = 256
POOL_HIST = 15
N_EXPERTS = 8
LANES = 128
SUBLANES = 8
LOG2E = 1.4426950408889634
QK_SCALE = HEAD_DIM ** -0.5 * LOG2E
KV_BLOCK = 256
V_ROWS = HEAD_DIM + 16
MOE_CHUNK = 256
VMEM_LIMIT = 56 * 1024 * 1024
NEG_INF = float("-inf")
KEY_DT = jnp.bfloat16
RANK_ZERO = 0x8000
RANK_NEG_INF = 0x007F
RANK_POS_INF = 0xFF80
IW_LANE = 8

C_AQ, C_IQ, C_A, C_B, C_CU, C_SM, C_GATE, C_END = 0, 384, 640, 896, 2048, 2304, 2432, 5504


def _cparams(sem):
    return pltpu.CompilerParams(dimension_semantics=sem, vmem_limit_bytes=VMEM_LIMIT)


def _const_spec(shape):
    nd = len(shape)
    return pl.BlockSpec(shape, lambda *_: (0,) * nd, pipeline_mode=pl.Buffered(1))


def _lane_iota(shape):
    return lax.broadcasted_iota(I32, shape, len(shape) - 1)


def _row_iota(shape):
    return lax.broadcasted_iota(I32, shape, len(shape) - 2)


def _ada_kernel(c_ref, w_ref, b_ref, o_ref):
    c = c_ref[...]
    s = c * jax.nn.sigmoid(c)
    o_ref[...] = jnp.dot(s, w_ref[...], preferred_element_type=F32,
                         precision=lax.Precision.HIGHEST) + b_ref[...]


def _ada(c_all, ada_w, ada_b):
    depth, d, n = ada_w.shape
    rows = c_all.shape[0]
    tn = 1536
    return pl.pallas_call(
        _ada_kernel,
        grid=(depth, n // tn),
        in_specs=[pl.BlockSpec((rows, d), lambda l, j: (0, 0)),
                  pl.BlockSpec((None, d, tn), lambda l, j: (l, 0, j)),
                  pl.BlockSpec((None, 1, tn), lambda l, j: (l, 0, j))],
        out_specs=pl.BlockSpec((None, rows, tn), lambda l, j: (l, 0, j)),
        out_shape=jax.ShapeDtypeStruct((depth, rows, n), F32),
        compiler_params=_cparams(("arbitrary", "arbitrary")),
        name="ada_mod",
    )(c_all, ada_w, ada_b.reshape(depth, 1, n))


def _in_kernel(x_ref, sc_ref, sh_ref, g_ref, w_ref, bf_ref, cos_ref, sin_ref,
               aqt_ref, iqt_ref, nat_ref, kik_ref, avt_ref, nbt_ref, bqt_ref, bk_ref, bvt_ref,
               cu_ref, sm_ref, smt_ref, gate_ref):
    x = x_ref[...]
    ms = jnp.mean(x * x, axis=-1, keepdims=True)
    y = x * lax.rsqrt(ms + NORM_EPS) * g_ref[...]
    h = (y * (1.0 + sc_ref[...]) + sh_ref[...]).astype(BF16)
    tm = x.shape[0]

    def mm(a, b):
        return jnp.dot(h, w_ref[:, a:b], preferred_element_type=F32)

    cos = cos_ref[...]
    sin = sin_ref[...]
    lane = _lane_iota((tm, LANES))
    low = lane < HEAD_DIM
    first_half = (lane & HALF) == 0

    def rope(z):
        swapped = jnp.where(first_half, pltpu.roll(z, LANES - HALF, 1), pltpu.roll(z, HALF, 1))
        return z * cos + swapped * sin

    zeros64 = jnp.zeros((HEAD_DIM, tm), BF16)
    ones_rows = jnp.where(_row_iota((V_ROWS - HEAD_DIM, tm)) == 0, 1.0, 0.0).astype(BF16)

    def put_heads(ref, zt, p, slot_even, slot_odd):
        for hh, slot in ((0, slot_even), (1, slot_odd)):
            base = (2 * p + hh) * LANES
            ref[base + slot * HEAD_DIM:base + (slot + 1) * HEAD_DIM, :] = zt[hh * HEAD_DIM:(hh + 1) * HEAD_DIM]
            ref[base + (1 - slot) * HEAD_DIM:base + (2 - slot) * HEAD_DIM, :] = zeros64

    z = mm(C_AQ, C_IQ)
    for p in range(3):
        zt = (rope(z[:, p * LANES:(p + 1) * LANES]) * QK_SCALE).T.astype(BF16)
        put_heads(aqt_ref, zt, p, 0, 0)
    z = mm(C_IQ, C_A)
    for p in range(2):
        zt = rope(z[:, p * LANES:(p + 1) * LANES]).T.astype(BF16)
        put_heads(iqt_ref, zt, p, 1, 1)

    z = mm(C_A, C_B)
    kv = z[:, :LANES]
    r0 = jnp.where(low, rope(kv), kv)
    r1 = rope(z[:, LANES:])
    r0t = r0.T
    nat_ref[0:LANES, :] = r0t
    nat_ref[LANES:, :] = r1.T[0:HEAD_DIM, :]
    kik_ref[...] = jnp.where(low, r0, pltpu.roll(r1, HEAD_DIM, 1)).astype(BF16)
    avt_ref[0:HEAD_DIM, :] = r0t[HEAD_DIM:, :].astype(BF16)
    avt_ref[HEAD_DIM:, :] = ones_rows

    z = mm(C_B, C_CU)
    bk_ref[...] = z[:, 384:768].astype(BF16)
    for p in range(3):
        zt = (z[:, p * LANES:(p + 1) * LANES] * QK_SCALE).T.astype(BF16)
        put_heads(bqt_ref, zt, p, 0, 1)
        nbt_ref[p * LANES:(p + 1) * LANES, :] = z[:, 384 + p * LANES:384 + (p + 1) * LANES].T
        vt = z[:, 768 + p * LANES:768 + (p + 1) * LANES].T
        nbt_ref[384 + p * LANES:384 + (p + 1) * LANES, :] = vt
        vt = vt.astype(BF16)
        for hh in range(2):
            bvt_ref[2 * p + hh, 0:HEAD_DIM, :] = vt[hh * HEAD_DIM:(hh + 1) * HEAD_DIM]
            bvt_ref[2 * p + hh, HEAD_DIM:, :] = ones_rows

    cu_ref[...] = mm(C_CU, C_SM)

    z = mm(C_SM, C_GATE)
    t = z + bf_ref[...]
    logf = jnp.minimum(t, 0.0) - jnp.log1p(jnp.exp(-jnp.abs(t)))
    sm = jnp.where(lane < B_HEADS, logf, z)
    sm_ref[...] = sm
    smt_ref[...] = sm.T[0:16, :]

    for c in range(3):
        gate_ref[:, c * D_MODEL:(c + 1) * D_MODEL] = jax.nn.sigmoid(
            mm(C_GATE + c * D_MODEL, C_GATE + (c + 1) * D_MODEL)).astype(BF16)


def _in_weights(w_in_l, b_forget_l):
    d = w_in_l.shape[0]
    sizes = (384, 64, 64, 256, 4, 64, 384, 384, 384, 6, 256, 3072)
    o = np.concatenate([[0], np.cumsum(sizes)])
    w_in_l = w_in_l.astype(BF16)
    cols = [w_in_l[:, o[0]:o[1]],
            w_in_l[:, o[3]:o[4]],
            w_in_l[:, o[1]:o[3]], w_in_l[:, o[5]:o[6]], jnp.zeros((d, 64), BF16),
            w_in_l[:, o[6]:o[9]],
            w_in_l[:, o[10]:o[11]],
            w_in_l[:, o[9]:o[10]], jnp.zeros((d, IW_LANE - B_HEADS), BF16), w_in_l[:, o[4]:o[5]],
            jnp.zeros((d, LANES - IW_LANE - IDX_HEADS), BF16),
            w_in_l[:, o[11]:o[12]]]
    w = jnp.concatenate(cols, axis=1)
    assert w.shape[1] == C_END, w.shape
    bias = jnp.concatenate([b_forget_l, jnp.zeros((LANES - B_HEADS,), F32)]).reshape(1, LANES)
    return w, bias


def _in_proj(x, sc, sh, g, w, bias, cos, sin, tm):
    bsz, t, d = x.shape
    mrows = sc.shape[1]
    mblk = 1 if mrows == 1 else tm
    mod_spec = pl.BlockSpec((None, mblk, d), (lambda b, i: (b, 0, 0)) if mrows == 1 else (lambda b, i: (b, i, 0)))
    nblk = t // tm

    def rows(n, dt):
        return pl.BlockSpec((None, tm, n), lambda b, i: (b, i, 0)), jax.ShapeDtypeStruct((bsz, t, n), dt)

    def cols(n, dt):
        return pl.BlockSpec((None, n, tm), lambda b, i: (b, 0, i)), jax.ShapeDtypeStruct((bsz, n, t), dt)

    outs = [cols(A_HEADS * LANES, BF16), cols(IDX_HEADS * LANES, BF16), cols(192, F32), rows(LANES, BF16),
            (pl.BlockSpec((None, None, V_ROWS, tm), lambda b, i: (b, i, 0, 0)),
             jax.ShapeDtypeStruct((bsz, nblk, V_ROWS, tm), BF16)),
            cols(768, F32), cols(B_HEADS * LANES, BF16), rows(384, BF16),
            (pl.BlockSpec((None, B_HEADS, None, V_ROWS, tm), lambda b, i: (b, 0, i, 0, 0)),
             jax.ShapeDtypeStruct((bsz, B_HEADS, nblk, V_ROWS, tm), BF16)),
            rows(256, F32), rows(LANES, F32), cols(16, F32), rows(3 * D_MODEL, BF16)]
    return pl.pallas_call(
        _in_kernel,
        grid=(bsz, nblk),
        in_specs=[pl.BlockSpec((None, tm, d), lambda b, i: (b, i, 0)), mod_spec, mod_spec,
                  _const_spec((1, d)), _const_spec(w.shape), _const_spec((1, LANES)),
                  pl.BlockSpec((tm, LANES), lambda b, i: (i, 0)),
                  pl.BlockSpec((tm, LANES), lambda b, i: (i, 0))],
        out_specs=[o[0] for o in outs],
        out_shape=[o[1] for o in outs],
        compiler_params=_cparams(("parallel", "parallel")),
        name="in_proj",
    )(x, sc, sh, g.reshape(1, d), w, bias, cos, sin)


def _cum_kernel(x_ref, o_ref, carry_ref):
    @pl.when(pl.program_id(1) == 0)
    def _():
        carry_ref[...] = jnp.zeros_like(carry_ref)

    tc = x_ref.shape[0]
    tri = jnp.where(_lane_iota((tc, tc)) <= _row_iota((tc, tc)), 1.0, 0.0).astype(BF16)
    cum = carry_ref[0:1, :]
    rest = x_ref[...]
    for _ in range(3):
        piece = rest.astype(BF16)
        cum = cum + jnp.dot(tri, piece, preferred_element_type=F32)
        rest = rest - piece.astype(F32)
    carry_ref[...] = jnp.broadcast_to(cum[tc - 1:tc, :], carry_ref.shape)
    pieces = []
    rest = cum * LOG2E
    for _ in range(3):
        piece = rest.astype(BF16)
        pieces.append(piece.astype(F32))
        rest = rest - pieces[-1]
    lane = _lane_iota((tc, LANES))
    for p in range(B_HEADS // 2):
        slab = jnp.zeros((tc, LANES), F32)
        for hh in range(2):
            for j, piece in enumerate(pieces):
                dst, src = 3 * hh + j, 2 * p + hh
                slab = jnp.where(lane == dst, pltpu.roll(piece, (dst - src) % LANES, 1), slab)
        o_ref[p] = slab.astype(BF16)


def _cum_logf(x, tc):
    bsz, t, n = x.shape
    return pl.pallas_call(
        _cum_kernel,
        grid=(bsz, t // tc),
        in_specs=[pl.BlockSpec((None, tc, n), lambda b, i: (b, i, 0))],
        out_specs=pl.BlockSpec((None, B_HEADS // 2, tc, LANES), lambda b, i: (b, 0, i, 0)),
        out_shape=jax.ShapeDtypeStruct((bsz, B_HEADS // 2, t, LANES), BF16),
        scratch_shapes=[pltpu.VMEM((SUBLANES, LANES), F32)],
        compiler_params=_cparams(("parallel", "arbitrary")),
        name="logf_cumsum",
    )(x)


def _sum_keys(x):
    part = x.reshape(x.shape[0] // SUBLANES, SUBLANES, x.shape[1]).sum(axis=0)
    return jnp.sum(part, axis=0, keepdims=True)


def _max_keys(x):
    part = x.reshape(x.shape[0] // SUBLANES, SUBLANES, x.shape[1]).max(axis=0)
    return jnp.max(part, axis=0, keepdims=True)


def _dsa_kernel(aqt_ref, iqt_ref, smt_ref, kik_ref, avt_ref, o_ref, key_ref,
                *, tq, tk, n_keys, q_pos0, n_sel):
    i = pl.program_id(1)
    pos_first = q_pos0 + i * tq
    last_chunk = (pos_first + tq - 1) // CHUNK
    n_adm = jnp.minimum((last_chunk + 1) * CHUNK, n_keys)
    nkb = (n_adm + tk - 1) // tk

    q_pos = pos_first + _lane_iota((1, tq))
    q_lim = jnp.minimum((q_pos // CHUNK + 1) * CHUNK, n_keys)
    key_row = _row_iota((tk, tq))

    def keys(kb):
        return kik_ref[pl.ds(pl.multiple_of(kb * tk, tk), tk), :]

    iq4 = jnp.concatenate([iqt_ref[hd * LANES:(hd + 1) * LANES, :] for hd in range(IDX_HEADS)], axis=1)
    smt = smt_ref[...]
    w_rows = [smt[IW_LANE + hd:IW_LANE + hd + 1, :] for hd in range(IDX_HEADS)]

    def score_body(kb, carry):
        s4 = jnp.dot(keys(kb), iq4, preferred_element_type=F32)
        score = w_rows[0] * jnp.maximum(s4[:, 0:tq], 0.0)
        for hd in range(1, IDX_HEADS):
            score = score + w_rows[hd] * jnp.maximum(s4[:, hd * tq:(hd + 1) * tq], 0.0)
        score = jnp.where(key_row < q_lim - kb * tk, score, NEG_INF)
        key_ref[kb] = score.astype(KEY_DT)
        return carry

    lax.fori_loop(0, nkb, score_body, 0)

    one, zero = jnp.ones((), KEY_DT), jnp.zeros((), KEY_DT)
    packed_rows = 2 * SUBLANES

    def count(cand, strict):
        def hits(blk):
            cols = []
            for c0 in range(0, tq, 2 * LANES):
                c1 = min(c0 + 2 * LANES, tq)
                h = jnp.where((blk[:, c0:c1] > cand[:, c0:c1]) if strict else (blk[:, c0:c1] >= cand[:, c0:c1]),
                              one, zero)
                parts = [h[r * packed_rows:(r + 1) * packed_rows] for r in range(tk // packed_rows)]
                while len(parts) > 1:
                    parts = [a + b for a, b in zip(parts[::2], parts[1::2])]
                cols.append(parts[0].astype(F32))
            return cols[0] if len(cols) == 1 else jnp.concatenate(cols, axis=1)

        def pair(j, acc):
            return acc + hits(key_ref[2 * j]) + hits(key_ref[2 * j + 1])

        acc = lax.fori_loop(0, nkb // 2, pair, jnp.zeros((packed_rows, tq), F32))
        acc = lax.fori_loop(2 * (nkb // 2), nkb, lambda kb, a: a + hits(key_ref[kb]), acc)
        return jnp.sum(acc, axis=0, keepdims=True)

    def pattern_value(u):
        bits = jnp.where(u >= RANK_ZERO, u - RANK_ZERO, (~u) & 0xFFFF)
        return lax.bitcast_convert_type(lax.shift_left(bits, 16), F32).astype(KEY_DT)

    def bit_body(b, u):
        cand_u = u | lax.shift_left(jnp.int32(1), 15 - b)
        cnt = count(pattern_value(cand_u), False)
        return jnp.where(cnt >= n_sel, cand_u, u)

    u_thr = jnp.maximum(lax.fori_loop(0, 16, bit_body, jnp.zeros((1, tq), I32)), RANK_NEG_INF)
    thr_16 = pattern_value(u_thr)

    def finer():
        lo = thr_16.astype(F32)
        hi = jnp.where(u_thr >= RANK_POS_INF, jnp.inf, pattern_value(u_thr + 1).astype(F32))

        def halve(_, lo_hi):
            lo, hi = lo_hi
            cand = (0.5 * lo + 0.5 * hi).astype(KEY_DT)
            enough = count(cand, False) >= n_sel
            return jnp.where(enough, cand.astype(F32), lo), jnp.where(enough, hi, cand.astype(F32))

        thr_f = lax.fori_loop(0, 16, halve, (lo, hi))[0].astype(KEY_DT)
        return thr_f, count(thr_f, True)

    above_16 = count(thr_16, True)
    thr_k, above = lax.cond(jnp.min(n_sel - above_16) <= 0.0, finer, lambda: (thr_16, above_16))
    need = jnp.maximum(n_sel - above, 0.0)
    thr = thr_k.astype(F32)

    aq6 = jnp.concatenate([aqt_ref[hd * LANES:(hd + 1) * LANES, :] for hd in range(A_HEADS)], axis=1)

    half = tk // 2
    lower = jnp.where(_lane_iota((half, half)) <= _row_iota((half, half)), 1.0, 0.0).astype(BF16)

    def attend():
        def body(kb, carry):
            eq_seen, ms, accs = carry
            blk = key_ref[kb].astype(F32)
            eq = blk == thr
            eq_f = jnp.where(eq, 1.0, 0.0)
            prefs = []
            for e in (eq_f[:half], eq_f[half:]):
                prefs.append(jnp.dot(lower, e.astype(BF16), preferred_element_type=F32) + eq_seen)
                eq_seen = eq_seen + _sum_keys(e)
            slack = jnp.where(blk >= thr, need - jnp.where(eq, jnp.concatenate(prefs, axis=0), 0.0), -1.0)
            bias = jnp.where(slack >= 0.0, jnp.where(jnp.abs(blk) < jnp.inf, 0.0, NEG_INF), NEG_INF)
            logits = jnp.dot(keys(kb), aq6, preferred_element_type=F32)
            vts = (avt_ref[2 * kb], avt_ref[2 * kb + 1])
            new_ms, new_accs = [], []
            for p in range(A_HEADS // 2):
                ps, alphas = [], []
                for hd in (2 * p, 2 * p + 1):
                    lg = logits[:, hd * tq:(hd + 1) * tq] + bias
                    m_old = ms[hd]
                    m_new = jnp.maximum(m_old, _max_keys(lg))
                    m_safe = jnp.where(m_new == NEG_INF, 0.0, m_new)
                    ps.append(jnp.exp2(lg - m_safe).astype(BF16))
                    alphas.append(jnp.exp2(m_old - m_safe))
                    new_ms.append(m_new)
                p2 = jnp.concatenate(ps, axis=1)
                pv = (jnp.dot(vts[0], p2[:tk // 2], preferred_element_type=F32)
                      + jnp.dot(vts[1], p2[tk // 2:], preferred_element_type=F32))
                new_accs.append(jnp.concatenate(alphas, axis=1) * accs[p] + pv)
            return eq_seen, tuple(new_ms), tuple(new_accs)

        init = (jnp.zeros((1, tq), F32),
                tuple(jnp.full((1, tq), NEG_INF, F32) for _ in range(A_HEADS)),
                tuple(jnp.zeros((V_ROWS, 2 * tq), F32) for _ in range(A_HEADS // 2)))
        return lax.fori_loop(0, nkb, body, init)[2]

    accs = attend()

    outs = []
    for p in range(A_HEADS // 2):
        o2 = accs[p][0:HEAD_DIM] / accs[p][HEAD_DIM:HEAD_DIM + 1]
        outs += [o2[:, 0:tq], o2[:, tq:2 * tq]]
    o_ref[...] = jnp.concatenate(outs, axis=0).T.astype(o_ref.dtype)


def _dsa(aqt, iqt, smt, kik, avt, *, tq, n_keys, q_pos0, n_sel):
    bsz, _, t_q = aqt.shape
    _, nblk, _, tkv = avt.shape
    assert nblk % 2 == 0
    tk = 2 * tkv
    kern = functools.partial(_dsa_kernel, tq=tq, tk=tk, n_keys=n_keys, q_pos0=q_pos0, n_sel=n_sel)
    return pl.pallas_call(
        kern,
        grid=(bsz, t_q // tq),
        in_specs=[pl.BlockSpec((None, A_HEADS * LANES, tq), lambda b, i: (b, 0, i)),
                  pl.BlockSpec((None, IDX_HEADS * LANES, tq), lambda b, i: (b, 0, i)),
                  pl.BlockSpec((None, 16, tq), lambda b, i: (b, 0, i)),
                  pl.BlockSpec((None, nblk * tkv, LANES), lambda b, i: (b, 0, 0)),
                  pl.BlockSpec((None, nblk, V_ROWS, tkv), lambda b, i: (b, 0, 0, 0))],
        out_specs=pl.BlockSpec((None, tq, 384), lambda b, i: (b, i, 0)),
        out_shape=jax.ShapeDtypeStruct((bsz, t_q, 384), BF16),
        scratch_shapes=[pltpu.VMEM((nblk // 2, tk, tq), KEY_DT)],
        compiler_params=_cparams(("parallel", "arbitrary")),
        name="dsa_attention",
    )(aqt, iqt, smt, kik, avt)


def _fox_kernel(qt_ref, k_ref, vt_ref, ck_ref, o_ref, *, tq, tk, n_keys, q_pos0):
    i = pl.program_id(1)
    pos_first = q_pos0 + i * tq
    n_full = pos_first // tk
    nkb = (jnp.minimum(pos_first + tq, n_keys) + tk - 1) // tk
    q_pos = pos_first + _lane_iota((1, tq))
    key_row = _row_iota((tk, tq))
    piece_row = _row_iota((LANES, tq))
    qts = []
    for hd in range(B_HEADS):
        minus = jnp.where((piece_row >= 3 * (hd % 2)) & (piece_row < 3 * (hd % 2) + 3), -1.0, 0.0).astype(BF16)
        qts.append(jnp.concatenate([qt_ref[hd * LANES:(hd + 1) * LANES, :], minus], axis=0))

    def step(kbs, state, masked):
        logits = []
        for hd in range(B_HEADS):
            for kb in kbs:
                rows = pl.ds(pl.multiple_of(kb * tk, tk), tk)
                kblk = jnp.concatenate([k_ref[rows, (hd // 2) * LANES:(hd // 2 + 1) * LANES],
                                        ck_ref[hd // 2, rows, :]], axis=1)
                lg = jnp.dot(kblk, qts[hd], preferred_element_type=F32)
                if masked:
                    lg = jnp.where(key_row <= q_pos - kb * tk, lg, NEG_INF)
                logits.append(lg)
        new = []
        for hd in range(B_HEADS):
            m_old, acc = state[hd]
            lgs = logits[hd * len(kbs):(hd + 1) * len(kbs)]
            m_new = m_old
            for lg in lgs:
                m_new = jnp.maximum(m_new, _max_keys(lg))
            m_safe = jnp.where(m_new == NEG_INF, 0.0, m_new) if masked else m_new
            acc = jnp.exp2(m_old - m_safe) * acc
            for kb, lg in zip(kbs, lgs):
                p = jnp.exp2(lg - m_safe).astype(BF16)
                acc = acc + jnp.dot(vt_ref[hd, kb], p, preferred_element_type=F32)
            new.append((m_new, acc))
        return tuple(new)

    init = tuple((jnp.full((1, tq), NEG_INF, F32), jnp.zeros((V_ROWS, tq), F32)) for _ in range(B_HEADS))
    state = lax.fori_loop(0, n_full // 2, lambda j, st: step((2 * j, 2 * j + 1), st, False), init)
    state = lax.fori_loop(2 * (n_full // 2), nkb, lambda kb, st: step((kb,), st, True), state)
    outs = [acc[0:HEAD_DIM] / acc[HEAD_DIM:HEAD_DIM + 1] for _, acc in state]
    o_ref[...] = jnp.concatenate(outs, axis=0).T.astype(o_ref.dtype)


def _fox(qt, k, vt, ck, *, tq, n_keys, q_pos0):
    bsz, _, t_q = qt.shape
    _, _, nblk, _, tk = vt.shape
    lpad = nblk * tk
    kern = functools.partial(_fox_kernel, tq=tq, tk=tk, n_keys=n_keys, q_pos0=q_pos0)
    return pl.pallas_call(
        kern,
        grid=(bsz, t_q // tq),
        in_specs=[pl.BlockSpec((None, B_HEADS * LANES, tq), lambda b, i: (b, 0, i)),
                  pl.BlockSpec((None, lpad, 384), lambda b, i: (b, 0, 0)),
                  pl.BlockSpec((None, B_HEADS, nblk, V_ROWS, tk), lambda b, i: (b, 0, 0, 0, 0)),
                  pl.BlockSpec((None, B_HEADS // 2, lpad, LANES), lambda b, i: (b, 0, 0, 0))],
        out_specs=pl.BlockSpec((None, tq, 384), lambda b, i: (b, i, 0)),
        out_shape=jax.ShapeDtypeStruct((bsz, t_q, 384), BF16),
        compiler_params=_cparams(("parallel", "arbitrary")),
        name="fox_attention",
    )(qt, k, vt, ck)


def _pool_kernel(cur_ref, prev_ref, hist_ref, w_ref, s_ref, o_ref, ext, *, tc, start_pos):
    i = pl.program_id(1)
    cur = cur_ref[...]
    ext[0:16, :] = jnp.where(i == 0, hist_ref[...], prev_ref[tc - 16:, :])
    ext[16:, :] = cur
    pos = start_pos + i * tc + lax.broadcasted_iota(I32, (tc, POOL_WIDTH), 0)
    lane = _lane_iota((tc, POOL_WIDTH))
    run = cur
    pooled = jnp.zeros_like(cur)
    k = 1
    for g, w in enumerate(POOL_WINDOWS):
        while k < w:
            run = run + ext[16 - k:16 - k + tc, :]
            k += 1
        cnt = jnp.minimum(pos + 1, w).astype(F32)
        in_group = (lane >= g * POOL_GROUP_DIM) & (lane < (g + 1) * POOL_GROUP_DIM)
        pooled = jnp.where(in_group, run / cnt, pooled)
    z = (pooled - cur).astype(BF16)
    o_ref[...] = (jnp.dot(z, w_ref[...], preferred_element_type=F32) * s_ref[...]).astype(o_ref.dtype)


def _pool(cu, hist16, w_bd, scale, *, tc, start_pos):
    bsz, t, n = cu.shape
    kern = functools.partial(_pool_kernel, tc=tc, start_pos=start_pos)
    return pl.pallas_call(
        kern,
        grid=(bsz, t // tc),
        in_specs=[pl.BlockSpec((None, tc, n), lambda b, i: (b, i, 0)),
                  pl.BlockSpec((None, tc, n), lambda b, i: (b, jnp.maximum(i - 1, 0), 0)),
                  pl.BlockSpec((None, 16, n), lambda b, i: (b, 0, 0)),
                  _const_spec((n, n)), _const_spec((1, n))],
        out_specs=pl.BlockSpec((None, tc, n), lambda b, i: (b, i, 0)),
        out_shape=jax.ShapeDtypeStruct((bsz, t, n), BF16),
        scratch_shapes=[pltpu.VMEM((16 + tc, n), F32)],
        compiler_params=_cparams(("parallel", "arbitrary")),
        name="pool_mixer",
    )(cu, cu, hist16, w_bd, scale.reshape(1, n))


def _route(logits):
    lane = _lane_iota(logits.shape).astype(F32)
    lg = jnp.where(lane < N_EXPERTS, logits, NEG_INF)
    m1 = jnp.max(lg, axis=1, keepdims=True)
    i1 = jnp.min(jnp.where(lg == m1, lane, float(LANES)), axis=1, keepdims=True)
    hot1 = lane == i1
    lg2 = jnp.where(hot1, NEG_INF, lg)
    m2 = jnp.max(lg2, axis=1, keepdims=True)
    i2 = jnp.min(jnp.where(lg2 == m2, lane, float(LANES)), axis=1, keepdims=True)
    hot2 = lane == i2
    e2 = jnp.exp(m2 - m1)
    den = 1.0 + e2
    return jnp.where(hot1, 1.0 / den, 0.0) + jnp.where(hot2, e2 / den, 0.0)


def _final_norm(x, gain):
    ms = jnp.mean(x * x, axis=-1, keepdims=True)
    return x * lax.rsqrt(ms + NORM_EPS) * gain


def _merge_kernel(x_ref, oa_ref, ob_ref, oc_ref, gate_ref, g1_ref, sc2_ref, sh2_ref, g_ref,
                  wa_ref, wb_ref, wc_ref, wo_ref, *rest, moe, final):
    if moe:
        rw_ref, rb_ref, xo_ref, h_ref, gw_ref = rest
    elif final:
        g2_ref, wg_ref, wu_ref, wd_ref, fg_ref, xo_ref = rest
    else:
        g2_ref, wg_ref, wu_ref, wd_ref, xo_ref = rest
    d = D_MODEL
    merged = (gate_ref[:, 0:d] * jnp.dot(oa_ref[...], wa_ref[...], preferred_element_type=F32)
              + gate_ref[:, d:2 * d] * jnp.dot(ob_ref[...], wb_ref[...], preferred_element_type=F32)
              + gate_ref[:, 2 * d:3 * d] * jnp.dot(oc_ref[...], wc_ref[...], preferred_element_type=F32))
    x = x_ref[...] + g1_ref[...] * jnp.dot(merged.astype(BF16), wo_ref[...], preferred_element_type=F32)
    ms = jnp.mean(x * x, axis=-1, keepdims=True)
    y = x * lax.rsqrt(ms + NORM_EPS) * g_ref[...]
    h = y * (1.0 + sc2_ref[...]) + sh2_ref[...]
    if not moe:
        hb = h.astype(BF16)
        n_chunks = 1
        tf = wg_ref.shape[1] // n_chunks
        acc = jnp.zeros(x.shape, F32)
        for c in range(n_chunks):
            gt = jnp.dot(hb, wg_ref[:, c * tf:(c + 1) * tf], preferred_element_type=F32)
            up = jnp.dot(hb, wu_ref[:, c * tf:(c + 1) * tf], preferred_element_type=F32)
            act = (gt * jax.nn.sigmoid(gt) * up).astype(BF16)
            acc = acc + jnp.dot(act, wd_ref[c * tf:(c + 1) * tf, :], preferred_element_type=F32)
        x = x + g2_ref[...] * acc
        xo_ref[...] = _final_norm(x, fg_ref[...]) if final else x
    else:
        xo_ref[...] = x
        h_ref[...] = h.astype(BF16)
        h_hi = h.astype(BF16)
        h_lo = (h - h_hi.astype(F32)).astype(BF16)
        logits = (jnp.dot(h_hi, rw_ref[0], preferred_element_type=F32)
                  + jnp.dot(h_lo, rw_ref[0], preferred_element_type=F32)
                  + jnp.dot(h_hi, rw_ref[1], preferred_element_type=F32)) + rb_ref[...]
        gw_ref[...] = _route(logits)


def _merge(x, oa, ob, oc, gates, g1, sc2, sh2, g, wa, wb, wc, wo, router, ffn, tm):
    assert (router is None) != (ffn is None)
    bsz, t, d = x.shape
    mrows = g1.shape[1]
    mblk = 1 if mrows == 1 else tm
    mod_spec = pl.BlockSpec((None, mblk, d), (lambda b, i: (b, 0, 0)) if mrows == 1 else (lambda b, i: (b, i, 0)))

    def tok(n):
        return pl.BlockSpec((None, tm, n), lambda b, i: (b, i, 0))

    in_specs = [tok(d), tok(384), tok(384), tok(256), tok(3 * d), mod_spec, mod_spec, mod_spec,
                _const_spec((1, d)), _const_spec(wa.shape), _const_spec(wb.shape), _const_spec(wc.shape),
                _const_spec(wo.shape)]
    args = [x, oa, ob, oc, gates, g1, sc2, sh2, g.reshape(1, d), wa, wb, wc, wo]
    if router is not None:
        rw, rb = router
        in_specs += [_const_spec(rw.shape), _const_spec(rb.shape)]
        args += [rw, rb]
        out_specs = [tok(d), tok(d), tok(LANES)]
        out_shape = [jax.ShapeDtypeStruct((bsz, t, d), F32), jax.ShapeDtypeStruct((bsz, t, d), BF16),
                     jax.ShapeDtypeStruct((bsz, t, LANES), F32)]
    else:
        g2, wg, wu, wd, final_g = ffn
        in_specs += [mod_spec, _const_spec(wg.shape), _const_spec(wu.shape), _const_spec(wd.shape)]
        args += [g2, wg, wu, wd]
        if final_g is not None:
            in_specs.append(_const_spec((1, d)))
            args.append(final_g.reshape(1, d))
        out_specs = [tok(d)]
        out_shape = [jax.ShapeDtypeStruct((bsz, t, d), F32)]
    return pl.pallas_call(
        functools.partial(_merge_kernel, moe=router is not None, final=ffn is not None and ffn[4] is not None),
        grid=(bsz, t // tm),
        in_specs=in_specs, out_specs=out_specs, out_shape=out_shape,
        compiler_params=_cparams(("parallel", "parallel")),
        name="merge_out" if router is not None else "merge_ffn",
    )(*args)


def _moe_kernel(x_ref, h_ref, g2_ref, gw_ref, wg_ref, wu_ref, wd_ref, *rest, final):
    if final:
        fg_ref, o_ref, acc_ref, posc_ref, posr_ref = rest
    else:
        o_ref, acc_ref, posc_ref, posr_ref = rest
    e = pl.program_id(2)
    tm = h_ref.shape[0]
    n_slabs = tm // MOE_CHUNK

    @pl.when(e == 0)
    def _():
        acc_ref[...] = jnp.zeros_like(acc_ref)
        routed = gw_ref[...] != 0.0
        r_f = jnp.where(routed, 1.0, 0.0)
        r_b = r_f.astype(BF16)
        r_t = r_f.T
        r_tb = r_t.astype(BF16)
        tok_l = _lane_iota((MOE_CHUNK, tm))
        tok_r = _row_iota((MOE_CHUNK, tm))
        rank_r = jnp.zeros((LANES, tm), F32)
        for s in range(n_slabs):
            rows = slice(s * MOE_CHUNK, (s + 1) * MOE_CHUNK)
            earlier = jnp.where(tok_l < tok_r + s * MOE_CHUNK, 1.0, 0.0).astype(BF16)
            rank_c = jnp.dot(earlier, r_b, preferred_element_type=F32)
            posc_ref[rows, :] = jnp.where(routed[rows], rank_c, -1.0)
            later = jnp.where(tok_r + s * MOE_CHUNK < tok_l, 1.0, 0.0).astype(BF16)
            rank_r = rank_r + jnp.dot(r_tb[:, rows], later, preferred_element_type=F32)
        posr_ref[...] = jnp.where(r_t != 0.0, rank_r, -1.0)

    lane_e = _lane_iota((tm, LANES)) == e
    pos_c = jnp.sum(jnp.where(lane_e, posc_ref[...], 0.0), axis=1, keepdims=True)
    gate_c = jnp.sum(jnp.where(lane_e, gw_ref[...], 0.0), axis=1, keepdims=True)
    pos_r = posr_ref[pl.ds(e, 1), :]
    n_routed = (jnp.max(pos_r) + 1.0).astype(I32)

    def run_chunk(base, n_rows):
        slot_rows = _row_iota((n_rows, tm)).astype(F32)
        slot_lanes = _lane_iota((MOE_CHUNK, n_rows)).astype(F32)
        pack = jnp.where(pos_r - base == slot_rows, 1.0, 0.0).astype(BF16)
        xc = jnp.dot(pack, h_ref[...], preferred_element_type=F32).astype(BF16)
        gt = jnp.dot(xc, wg_ref[...], preferred_element_type=F32)
        up = jnp.dot(xc, wu_ref[...], preferred_element_type=F32)
        act = (gt * jax.nn.sigmoid(gt) * up).astype(BF16)
        y = jnp.dot(act, wd_ref[...], preferred_element_type=F32).astype(BF16)
        for s in range(n_slabs):
            rows = slice(s * MOE_CHUNK, (s + 1) * MOE_CHUNK)
            unpack = jnp.where(pos_c[rows] - base == slot_lanes, 1.0, 0.0).astype(BF16)
            acc_ref[rows, :] += gate_c[rows] * jnp.dot(unpack, y, preferred_element_type=F32)

    def first(c, carry):
        run_chunk(0.0, MOE_CHUNK)
        return carry

    def later(c, carry):
        run_chunk((MOE_CHUNK + c * (MOE_CHUNK // 2)).astype(F32), MOE_CHUNK // 2)
        return carry

    lax.fori_loop(0, jnp.minimum(n_routed, 1), first, 0)
    n_later = (jnp.maximum(n_routed - MOE_CHUNK, 0) + MOE_CHUNK // 2 - 1) // (MOE_CHUNK // 2)
    lax.fori_loop(0, n_later, later, 0)

    @pl.when(e == pl.num_programs(2) - 1)
    def _():
        x = x_ref[...] + g2_ref[...] * acc_ref[...]
        o_ref[...] = _final_norm(x, fg_ref[...]) if final else x


def _moe(x, h, g2, gw, wg, wu, wd, final_g, tm):
    bsz, t, d = x.shape
    n_e, _, dff = wg.shape
    mrows = g2.shape[1]
    mblk = 1 if mrows == 1 else tm
    mod_spec = pl.BlockSpec((None, mblk, d), (lambda b, i, e: (b, 0, 0)) if mrows == 1 else (lambda b, i, e: (b, i, 0)))
    tok = pl.BlockSpec((None, tm, d), lambda b, i, e: (b, i, 0))
    tok_once = pl.BlockSpec((None, tm, d), lambda b, i, e: (b, i, 0), pipeline_mode=pl.Buffered(1))
    in_specs = [tok_once, tok, mod_spec, pl.BlockSpec((None, tm, LANES), lambda b, i, e: (b, i, 0)),
                pl.BlockSpec((None, d, dff), lambda b, i, e: (e, 0, 0)),
                pl.BlockSpec((None, d, dff), lambda b, i, e: (e, 0, 0)),
                pl.BlockSpec((None, dff, d), lambda b, i, e: (e, 0, 0))]
    args = [x, h, g2, gw, wg, wu, wd]
    if final_g is not None:
        in_specs.append(pl.BlockSpec((1, d), lambda b, i, e: (0, 0)))
        args.append(final_g.reshape(1, d))
    return pl.pallas_call(
        functools.partial(_moe_kernel, final=final_g is not None),
        grid=(bsz, t // tm, n_e),
        in_specs=in_specs, out_specs=tok,
        out_shape=jax.ShapeDtypeStruct((bsz, t, d), F32),
        scratch_shapes=[pltpu.VMEM((tm, d), F32), pltpu.VMEM((tm, LANES), F32), pltpu.VMEM((LANES, tm), F32)],
        compiler_params=_cparams(("parallel", "parallel", "arbitrary")),
        name="moe_routed",
    )(*args)


def _rope_tables(pos):
    inv = ROPE_THETA ** (-jnp.arange(HALF, dtype=F32) / HALF)
    ang = pos.astype(F32)[:, None] * inv[None, :]
    cos, sin = jnp.cos(ang), jnp.sin(ang)
    return jnp.tile(cos, (1, 4)), jnp.tile(jnp.concatenate([-sin, sin], axis=1), (1, 2))


def _pick_tile(n, pref):
    t = min(n, pref)
    while n % t:
        t //= 2
    return t


def _per_seq_cols(a, bsz, t, width):
    f = a.shape[1]
    a = jnp.moveaxis(a[0].reshape(f, bsz, t), 1, 0)
    return jnp.pad(a, ((0, 0), (0, 0), (0, width - t)))


def _value_blocks(past_vt, new_vt, bsz, t, lpad):
    lead = new_vt.shape[:-2]
    new_b = jnp.moveaxis(new_vt.reshape(*lead, V_ROWS, bsz, t), -2, 0)
    full = jnp.concatenate([past_vt, new_b], axis=-1)
    full = jnp.pad(full, [(0, 0)] * (full.ndim - 1) + [(0, lpad - full.shape[-1])])
    full = full.reshape(bsz, *lead, V_ROWS, lpad // KV_BLOCK, KV_BLOCK)
    return jnp.moveaxis(full, -2, -3)


def _with_ones_rows(vt):
    ones = jnp.ones(vt.shape[:-2] + (1, vt.shape[-1]), vt.dtype)
    zeros = jnp.zeros(vt.shape[:-2] + (V_ROWS - HEAD_DIM - 1, vt.shape[-1]), vt.dtype)
    return jnp.concatenate([vt, ones, zeros], axis=-2)


def _mixers(inp, past, n_past, lw, bsz, t):
    aqt, iqt, kik, avt, bqt, bk, bvt, cu, sm, smt = inp
    n_keys = n_past + t
    n_sel = min(TOPK_MAX, n_keys // 4)
    lpad = -(-n_keys // (2 * KV_BLOCK)) * 2 * KV_BLOCK
    if past is None:
        kik_all, avt_all, bk_all, bvt_all, logf_all = kik, avt, bk, bvt, sm
        hist16 = jnp.zeros((bsz, 16, POOL_WIDTH), F32)
        tq_a, tq_b, t_pad = _pick_tile(t, 512), _pick_tile(t, 256), t
    else:
        pa, pb, plf, pc = past
        pk, pv, pik = (pa[:, :, j].astype(BF16) for j in range(3))

        def join_rows(p, new):
            full = jnp.concatenate([p, new.reshape(bsz, t, new.shape[-1])], axis=1)
            return jnp.pad(full, ((0, 0), (0, lpad - n_keys), (0, 0)))

        kik_all = join_rows(jnp.concatenate([pk, pik], axis=-1), kik)
        avt_all = _value_blocks(_with_ones_rows(jnp.swapaxes(pv, 1, 2)), avt[0, 0], bsz, t, lpad)
        bk_all = join_rows(pb[:, :, 0].astype(BF16).reshape(bsz, n_past, 384), bk)
        pvt = jnp.transpose(pb[:, :, 1].astype(BF16), (0, 2, 3, 1))
        bvt_all = _value_blocks(_with_ones_rows(pvt), bvt[0, :, 0], bsz, t, lpad)
        logf_all = join_rows(jnp.pad(plf, ((0, 0), (0, 0), (0, LANES - B_HEADS))), sm)
        hist16 = jnp.pad(pc, ((0, 0), (1, 0), (0, 0)))
        tq_a = tq_b = t_pad = LANES
        aqt, iqt, bqt, smt = (_per_seq_cols(a, bsz, t, t_pad) for a in (aqt, iqt, bqt, smt))

    oa = _dsa(aqt, iqt, smt, kik_all, avt_all, tq=tq_a, n_keys=n_keys, q_pos0=n_past, n_sel=n_sel)
    ck = _cum_logf(logf_all, 2 * KV_BLOCK)
    ob = _fox(bqt, bk_all, bvt_all, ck, tq=tq_b, n_keys=n_keys, q_pos0=n_past)
    cu = cu.reshape(bsz, t, POOL_WIDTH)
    oc = _pool(cu, hist16, lw["pool_bd"], lw["pool_scale"], tc=_pick_tile(t, 1024), start_pos=n_past)
    return oa[:, :t], ob[:, :t], oc


def _layer(x, mod, past, n_past, pos_tab, lw, layer, final_g, per_token):
    bsz, t, d = x.shape
    sh1, sc1, g1, sh2, sc2, g2 = mod
    if per_token:
        xt = x.reshape(1, bsz * t, d)
        sh1, sc1, g1, sh2, sc2, g2 = (jnp.broadcast_to(m, (bsz, t, d)).reshape(1, bsz * t, d) for m in mod)
        cos, sin = (jnp.tile(a, (bsz, 1)) for a in pos_tab)
    else:
        xt = x
        cos, sin = pos_tab
    tm = KV_BLOCK
    (aqt, iqt, nat, kik, avt, nbt, bqt, bk, bvt, cu, sm, smt, gates) = _in_proj(
        xt, sc1, sh1, lw["norm_mix_g"], lw["w_in"], lw["bf_bias"], cos, sin, tm)
    oa, ob, oc = _mixers((aqt, iqt, kik, avt, bqt, bk, bvt, cu, sm, smt), past, n_past, lw, bsz, t)

    def flat(a):
        return a.reshape(xt.shape[0], xt.shape[1], a.shape[-1])

    router = (lw["router_w"], lw["router_b"]) if layer % 2 else None
    ffn = None if layer % 2 else (g2, lw["ffn_wg"], lw["ffn_wu"], lw["ffn_wd"], final_g)
    res = _merge(xt, flat(oa), flat(ob), flat(oc), gates, g1, sc2, sh2, lw["norm_ffn_g"],
                 lw["w_br_a"], lw["w_br_b"], lw["w_br_c"], lw["w_out"], router, ffn, _pick_tile(xt.shape[1], 512))
    if layer % 2 == 0:
        (x_new,) = res
    else:
        x_mid, h2, gw = res
        x_new = _moe(x_mid, h2, g2, gw, lw["moe_wg"], lw["moe_wu"], lw["moe_wd"], final_g,
                     _pick_tile(xt.shape[1], 1024))

    def token_major(a, *feat):
        a = a.reshape(a.shape[0], *feat, -1, t) if per_token else a.reshape(a.shape[0], *feat, 1, t)
        a = jnp.moveaxis(a, (-2, -1), (1, 2))
        return a.reshape(bsz, t, *feat)

    new_a = token_major(nat, 3, HEAD_DIM)
    new_b = token_major(nbt, 2, B_HEADS, HEAD_DIM)
    new_logf = sm.reshape(bsz, t, LANES)[:, :, :B_HEADS]
    new_pool = cu.reshape(bsz, t, POOL_WIDTH)[:, t - POOL_HIST:, :]
    return x_new.reshape(bsz, t, d), (new_a, new_b, new_logf, new_pool)


def kernel(x_prompt, x_sample, cache_a_kvi, cache_b_kv, cache_b_logf, state_c_pool, c_prompt, c_sample,
           ada_w, ada_b, norm_mix_g, w_in, b_forget, pool_w, pool_scale, w_br_a, w_br_b, w_br_c, w_out,
           norm_ffn_g, ffn_w_gate, ffn_w_up, ffn_w_down, moe_router_w, moe_router_b, moe_w_gate,
           moe_w_up, moe_w_down, final_norm_g):
    depth = ada_w.shape[0]
    bp, tp, d = x_prompt.shape
    bs, ts, _ = x_sample.shape
    n_past = cache_a_kvi.shape[2]
    assert tp % KV_BLOCK == 0 and (bs * ts) % KV_BLOCK == 0 and ts <= LANES

    rows = -(-(bp + bs) // 8) * 8
    c_all = jnp.pad(jnp.concatenate([c_prompt, c_sample], axis=0), ((0, rows - bp - bs), (0, 0)))
    mod_all = _ada(c_all, ada_w, ada_b)

    tab_p = _rope_tables(jnp.arange(tp))
    tab_s = _rope_tables(n_past + jnp.arange(ts))

    xp, xs = x_prompt, x_sample
    outs_p, outs_s = [], []
    for layer in range(depth):
        j = layer // 2
        w_l, bias_l = _in_weights(w_in[layer], b_forget[layer])
        pw = pool_w[layer]
        pool_bd = jnp.zeros((POOL_WIDTH, POOL_WIDTH), F32)
        for g in range(len(POOL_WINDOWS)):
            sl = slice(g * POOL_GROUP_DIM, (g + 1) * POOL_GROUP_DIM)
            pool_bd = pool_bd.at[sl, sl].set(pw[g])
        lw = dict(w_in=w_l, bf_bias=bias_l, norm_mix_g=norm_mix_g[layer], norm_ffn_g=norm_ffn_g[layer],
                  pool_bd=pool_bd.astype(BF16), pool_scale=pool_scale[layer],
                  w_br_a=w_br_a[layer].astype(BF16), w_br_b=w_br_b[layer].astype(BF16),
                  w_br_c=w_br_c[layer].astype(BF16), w_out=w_out[layer].astype(BF16))
        if layer % 2 == 0:
            lw.update(ffn_wg=ffn_w_gate[j].astype(BF16), ffn_wu=ffn_w_up[j].astype(BF16),
                      ffn_wd=ffn_w_down[j].astype(BF16))
        else:
            rw = jnp.pad(moe_router_w[j], ((0, 0), (0, LANES - N_EXPERTS)))
            rw_hi = rw.astype(BF16)
            lw.update(router_w=jnp.stack([rw_hi, (rw - rw_hi.astype(F32)).astype(BF16)]),
                      router_b=jnp.pad(moe_router_b[j], (0, LANES - N_EXPERTS)).reshape(1, LANES),
                      moe_wg=moe_w_gate[j].astype(BF16), moe_wu=moe_w_up[j].astype(BF16),
                      moe_wd=moe_w_down[j].astype(BF16))
        final_g = final_norm_g if layer == depth - 1 else None
        mod_p = [m[:, None, :] for m in jnp.split(mod_all[layer, :bp], 6, axis=-1)]
        mod_s = [m[:, None, :] for m in jnp.split(mod_all[layer, bp:bp + bs], 6, axis=-1)]
        xp, new_p = _layer(xp, mod_p, None, 0, tab_p, lw, layer, final_g, per_token=False)
        past = (cache_a_kvi[layer], cache_b_kv[layer], cache_b_logf[layer], state_c_pool[layer])
        xs, new_s = _layer(xs, mod_s, past, n_past, tab_s, lw, layer, final_g, per_token=True)
        outs_p.append(new_p)
        outs_s.append(new_s)

    def stack(outs, k):
        return jnp.stack([o[k] for o in outs])

    return (xp, xs,
            stack(outs_p, 0), stack(outs_p, 1), stack(outs_p, 2), stack(outs_p, 3),
            stack(outs_s, 0), stack(outs_s, 1), stack(outs_s, 2), stack(outs_s, 3))
```

```python
import functools

import jax
import jax.numpy as jnp
import numpy as np
from jax import lax
from jax.experimental import pallas as pl
from jax.experimental.pallas import tpu as pltpu

F32 = jnp.float32
BF16 = jnp.bfloat16
I32 = jnp.int32

D_MODEL = 1024
CHUNK = 64
HEAD_DIM = 64
HALF = HEAD_DIM // 2
ROPE_THETA = 10000.0
NORM_EPS = 1e-6
A_HEADS = 6
IDX_HEADS = 4
TOPK_MAX = 256
B_HEADS = 6
POOL_WINDOWS = (2, 4, 8, 16)
POOL_GROUP_DIM = 64
POOL_WIDTH = 256
POOL_HIST = 15
N_EXPERTS = 8
LANES = 128
SUBLANES = 8
LOG2E = 1.4426950408889634
QK_SCALE = HEAD_DIM ** -0.5 * LOG2E
KV_BLOCK = 256
V_ROWS = HEAD_DIM + 16
MOE_CHUNK = 256
VMEM_LIMIT = 56 * 1024 * 1024
NEG_INF = float("-inf")
KEY_DT = jnp.bfloat16
RANK_ZERO = 0x8000
RANK_NEG_INF = 0x007F
RANK_POS_INF = 0xFF80
IW_LANE = 8

C_AQ, C_IQ, C_A, C_B, C_CU, C_SM, C_GATE, C_END = 0, 384, 640, 896, 2048, 2304, 2432, 5504


def _cparams(sem):
    return pltpu.CompilerParams(dimension_semantics=sem, vmem_limit_bytes=VMEM_LIMIT)


def _const_spec(shape):
    nd = len(shape)
    return pl.BlockSpec(shape, lambda *_: (0,) * nd, pipeline_mode=pl.Buffered(1))


def _lane_iota(shape):
    return lax.broadcasted_iota(I32, shape, len(shape) - 1)


def _row_iota(shape):
    return lax.broadcasted_iota(I32, shape, len(shape) - 2)


def _ada_kernel(c_ref, w_ref, b_ref, o_ref):
    c = c_ref[...]
    s = c * jax.nn.sigmoid(c)
    o_ref[...] = jnp.dot(s, w_ref[...], preferred_element_type=F32,
                         precision=lax.Precision.HIGHEST) + b_ref[...]


def _ada(c_all, ada_w, ada_b):
    depth, d, n = ada_w.shape
    rows = c_all.shape[0]
    tn = 1536
    return pl.pallas_call(
        _ada_kernel,
        grid=(depth, n // tn),
        in_specs=[pl.BlockSpec((rows, d), lambda l, j: (0, 0)),
                  pl.BlockSpec((None, d, tn), lambda l, j: (l, 0, j)),
                  pl.BlockSpec((None, 1, tn), lambda l, j: (l, 0, j))],
        out_specs=pl.BlockSpec((None, rows, tn), lambda l, j: (l, 0, j)),
        out_shape=jax.ShapeDtypeStruct((depth, rows, n), F32),
        compiler_params=_cparams(("arbitrary", "arbitrary")),
        name="ada_mod",
    )(c_all, ada_w, ada_b.reshape(depth, 1, n))


def _decay_pieces(x, first, o_ref, carry_ref):
    @pl.when(first)
    def _():
        carry_ref[...] = jnp.zeros_like(carry_ref)

    tc = x.shape[0]
    tri = jnp.where(_lane_iota((tc, tc)) <= _row_iota((tc, tc)), 1.0, 0.0).astype(BF16)
    cum = carry_ref[0:1, :]
    rest = x
    for _ in range(3):
        piece = rest.astype(BF16)
        cum = cum + jnp.dot(tri, piece, preferred_element_type=F32)
        rest = rest - piece.astype(F32)
    carry_ref[...] = jnp.broadcast_to(cum[tc - 1:tc, :], carry_ref.shape)
    pieces = []
    rest = cum * LOG2E
    for _ in range(3):
        piece = rest.astype(BF16)
        pieces.append(piece.astype(F32))
        rest = rest - pieces[-1]
    lane = _lane_iota((tc, LANES))
    for p in range(B_HEADS // 2):
        slab = jnp.zeros((tc, LANES), F32)
        for hh in range(2):
            for j, piece in enumerate(pieces):
                dst, src = 3 * hh + j, 2 * p + hh
                slab = jnp.where(lane == dst, pltpu.roll(piece, (dst - src) % LANES, 1), slab)
        o_ref[p] = slab.astype(BF16)


def _in_kernel(x_ref, sc_ref, sh_ref, g_ref, w_ref, bf_ref, cos_ref, sin_ref,
               aqt_ref, iqt_ref, nat_ref, kik_ref, avt_ref, nbt_ref, bqt_ref, bk_ref, bvt_ref,
               cu_ref, sm_ref, smt_ref, gate_ref, *decay):
    x = x_ref[...]
    ms = jnp.mean(x * x, axis=-1, keepdims=True)
    y = x * lax.rsqrt(ms + NORM_EPS) * g_ref[...]
    h = (y * (1.0 + sc_ref[...]) + sh_ref[...]).astype(BF16)
    tm = x.shape[0]

    def mm(a, b):
        return jnp.dot(h, w_ref[:, a:b], preferred_element_type=F32)

    cos = cos_ref[...]
    sin = sin_ref[...]
    lane = _lane_iota((tm, LANES))
    low = lane < HEAD_DIM
    first_half = (lane & HALF) == 0

    def rope(z):
        swapped = jnp.where(first_half, pltpu.roll(z, LANES - HALF, 1), pltpu.roll(z, HALF, 1))
        return z * cos + swapped * sin

    zeros64 = jnp.zeros((HEAD_DIM, tm), BF16)
    ones_rows = jnp.where(_row_iota((V_ROWS - HEAD_DIM, tm)) == 0, 1.0, 0.0).astype(BF16)

    def put_heads(ref, zt, p, slot_even, slot_odd):
        for hh, slot in ((0, slot_even), (1, slot_odd)):
            base = (2 * p + hh) * LANES
            ref[base + slot * HEAD_DIM:base + (slot + 1) * HEAD_DIM, :] = zt[hh * HEAD_DIM:(hh + 1) * HEAD_DIM]
            ref[base + (1 - slot) * HEAD_DIM:base + (2 - slot) * HEAD_DIM, :] = zeros64

    z = mm(C_AQ, C_IQ)
    for p in range(3):
        zt = (rope(z[:, p * LANES:(p + 1) * LANES]) * QK_SCALE).T.astype(BF16)
        put_heads(aqt_ref, zt, p, 0, 0)
    z = mm(C_IQ, C_A)
    for p in range(2):
        zt = rope(z[:, p * LANES:(p + 1) * LANES]).T.astype(BF16)
        put_heads(iqt_ref, zt, p, 1, 1)

    z = mm(C_A, C_B)
    kv = z[:, :LANES]
    r0 = jnp.where(low, rope(kv), kv)
    r1 = rope(z[:, LANES:])
    r0t = r0.T
    nat_ref[0:LANES, :] = r0t
    nat_ref[LANES:, :] = r1.T[0:HEAD_DIM, :]
    kik_ref[...] = jnp.where(low, r0, pltpu.roll(r1, HEAD_DIM, 1)).astype(BF16)
    avt_ref[0:HEAD_DIM, :] = r0t[HEAD_DIM:, :].astype(BF16)
    avt_ref[HEAD_DIM:, :] = ones_rows

    z = mm(C_B, C_CU)
    bk_ref[...] = z[:, 384:768].astype(BF16)
    for p in range(3):
        zt = (z[:, p * LANES:(p + 1) * LANES] * QK_SCALE).T.astype(BF16)
        put_heads(bqt_ref, zt, p, 0, 1)
        nbt_ref[p * LANES:(p + 1) * LANES, :] = z[:, 384 + p * LANES:384 + (p + 1) * LANES].T
        vt = z[:, 768 + p * LANES:768 + (p + 1) * LANES].T
        nbt_ref[384 + p * LANES:384 + (p + 1) * LANES, :] = vt
        vt = vt.astype(BF16)
        for hh in range(2):
            bvt_ref[2 * p + hh, 0:HEAD_DIM, :] = vt[hh * HEAD_DIM:(hh + 1) * HEAD_DIM]
            bvt_ref[2 * p + hh, HEAD_DIM:, :] = ones_rows

    cu_ref[...] = mm(C_CU, C_SM)

    z = mm(C_SM, C_GATE)
    t = z + bf_ref[...]
    logf = jnp.minimum(t, 0.0) - jnp.log1p(jnp.exp(-jnp.abs(t)))
    sm = jnp.where(lane < B_HEADS, logf, z)
    sm_ref[...] = sm
    smt_ref[...] = sm.T[0:16, :]
    if decay:
        ck_ref, carry_ref = decay
        _decay_pieces(sm, pl.program_id(1) == 0, ck_ref, carry_ref)

    for c in range(3):
        gate_ref[:, c * D_MODEL:(c + 1) * D_MODEL] = jax.nn.sigmoid(
            mm(C_GATE + c * D_MODEL, C_GATE + (c + 1) * D_MODEL)).astype(BF16)


def _in_weights(w_in_l, b_forget_l):
    d = w_in_l.shape[0]
    sizes = (384, 64, 64, 256, 4, 64, 384, 384, 384, 6, 256, 3072)
    o = np.concatenate([[0], np.cumsum(sizes)])
    w_in_l = w_in_l.astype(BF16)
    cols = [w_in_l[:, o[0]:o[1]],
            w_in_l[:, o[3]:o[4]],
            w_in_l[:, o[1]:o[3]], w_in_l[:, o[5]:o[6]], jnp.zeros((d, 64), BF16),
            w_in_l[:, o[6]:o[9]],
            w_in_l[:, o[10]:o[11]],
            w_in_l[:, o[9]:o[10]], jnp.zeros((d, IW_LANE - B_HEADS), BF16), w_in_l[:, o[4]:o[5]],
            jnp.zeros((d, LANES - IW_LANE - IDX_HEADS), BF16),
            w_in_l[:, o[11]:o[12]]]
    w = jnp.concatenate(cols, axis=1)
    assert w.shape[1] == C_END, w.shape
    bias = jnp.concatenate([b_forget_l, jnp.zeros((LANES - B_HEADS,), F32)]).reshape(1, LANES)
    return w, bias


def _in_proj(x, sc, sh, g, w, bias, cos, sin, tm, with_decay):
    bsz, t, d = x.shape
    mrows = sc.shape[1]
    mblk = 1 if mrows == 1 else tm
    mod_spec = pl.BlockSpec((None, mblk, d), (lambda b, i: (b, 0, 0)) if mrows == 1 else (lambda b, i: (b, i, 0)))
    nblk = t // tm

    def rows(n, dt):
        return pl.BlockSpec((None, tm, n), lambda b, i: (b, i, 0)), jax.ShapeDtypeStruct((bsz, t, n), dt)

    def cols(n, dt):
        return pl.BlockSpec((None, n, tm), lambda b, i: (b, 0, i)), jax.ShapeDtypeStruct((bsz, n, t), dt)

    outs = [cols(A_HEADS * LANES, BF16), cols(IDX_HEADS * LANES, BF16), cols(192, F32), rows(LANES, BF16),
            (pl.BlockSpec((None, None, V_ROWS, tm), lambda b, i: (b, i, 0, 0)),
             jax.ShapeDtypeStruct((bsz, nblk, V_ROWS, tm), BF16)),
            cols(768, F32), cols(B_HEADS * LANES, BF16), rows(384, BF16),
            (pl.BlockSpec((None, B_HEADS, None, V_ROWS, tm), lambda b, i: (b, 0, i, 0, 0)),
             jax.ShapeDtypeStruct((bsz, B_HEADS, nblk, V_ROWS, tm), BF16)),
            rows(256, F32), rows(LANES, F32), cols(16, F32), rows(3 * D_MODEL, BF16)]
    scratch = []
    if with_decay:
        outs.append((pl.BlockSpec((None, B_HEADS // 2, tm, LANES), lambda b, i: (b, 0, i, 0)),
                     jax.ShapeDtypeStruct((bsz, B_HEADS // 2, t, LANES), BF16)))
        scratch.append(pltpu.VMEM((SUBLANES, LANES), F32))
    return pl.pallas_call(
        _in_kernel,
        grid=(bsz, nblk),
        scratch_shapes=scratch,
        in_specs=[pl.BlockSpec((None, tm, d), lambda b, i: (b, i, 0)), mod_spec, mod_spec,
                  _const_spec((1, d)), _const_spec(w.shape), _const_spec((1, LANES)),
                  pl.BlockSpec((tm, LANES), lambda b, i: (i, 0)),
                  pl.BlockSpec((tm, LANES), lambda b, i: (i, 0))],
        out_specs=[o[0] for o in outs],
        out_shape=[o[1] for o in outs],
        compiler_params=_cparams(("parallel", "arbitrary" if with_decay else "parallel")),
        name="in_proj",
    )(x, sc, sh, g.reshape(1, d), w, bias, cos, sin)


def _cum_kernel(x_ref, o_ref, carry_ref):
    _decay_pieces(x_ref[...], pl.program_id(1) == 0, o_ref, carry_ref)


def _cum_logf(x, tc):
    bsz, t, n = x.shape
    return pl.pallas_call(
        _cum_kernel,
        grid=(bsz, t // tc),
        in_specs=[pl.BlockSpec((None, tc, n), lambda b, i: (b, i, 0))],
        out_specs=pl.BlockSpec((None, B_HEADS // 2, tc, LANES), lambda b, i: (b, 0, i, 0)),
        out_shape=jax.ShapeDtypeStruct((bsz, B_HEADS // 2, t, LANES), BF16),
        scratch_shapes=[pltpu.VMEM((SUBLANES, LANES), F32)],
        compiler_params=_cparams(("parallel", "arbitrary")),
        name="logf_cumsum",
    )(x)


def _sum_keys(x):
    part = x.reshape(x.shape[0] // SUBLANES, SUBLANES, x.shape[1]).sum(axis=0)
    return jnp.sum(part, axis=0, keepdims=True)


def _max_keys(x):
    part = x.reshape(x.shape[0] // SUBLANES, SUBLANES, x.shape[1]).max(axis=0)
    return jnp.max(part, axis=0, keepdims=True)


def _dsa_kernel(aqt_ref, iqt_ref, smt_ref, kik_ref, avt_ref, o_ref, key_ref,
                *, tq, tk, n_keys, q_pos0, n_sel):
    i = pl.program_id(1)
    pos_first = q_pos0 + i * tq
    last_chunk = (pos_first + tq - 1) // CHUNK
    n_adm = jnp.minimum((last_chunk + 1) * CHUNK, n_keys)
    nkb = (n_adm + tk - 1) // tk

    q_pos = pos_first + _lane_iota((1, tq))
    q_lim = jnp.minimum((q_pos // CHUNK + 1) * CHUNK, n_keys)
    key_row = _row_iota((tk, tq))

    def keys(kb):
        return kik_ref[pl.ds(pl.multiple_of(kb * tk, tk), tk), :]

    iq4 = jnp.concatenate([iqt_ref[hd * LANES:(hd + 1) * LANES, :] for hd in range(IDX_HEADS)], axis=1)
    smt = smt_ref[...]
    w_rows = [smt[IW_LANE + hd:IW_LANE + hd + 1, :] for hd in range(IDX_HEADS)]

    def score_body(kb, carry):
        s4 = jnp.dot(keys(kb), iq4, preferred_element_type=F32)
        score = w_rows[0] * jnp.maximum(s4[:, 0:tq], 0.0)
        for hd in range(1, IDX_HEADS):
            score = score + w_rows[hd] * jnp.maximum(s4[:, hd * tq:(hd + 1) * tq], 0.0)
        score = jnp.where(key_row < q_lim - kb * tk, score, NEG_INF)
        key_ref[kb] = score.astype(KEY_DT)
        return carry

    lax.fori_loop(0, nkb, score_body, 0)

    one, zero = jnp.ones((), KEY_DT), jnp.zeros((), KEY_DT)
    packed_rows = 2 * SUBLANES

    def count(cand, strict):
        def hits(blk):
            cols = []
            for c0 in range(0, tq, 2 * LANES):
                c1 = min(c0 + 2 * LANES, tq)
                h = jnp.where((blk[:, c0:c1] > cand[:, c0:c1]) if strict else (blk[:, c0:c1] >= cand[:, c0:c1]),
                              one, zero)
                parts = [h[r * packed_rows:(r + 1) * packed_rows] for r in range(tk // packed_rows)]
                while len(parts) > 1:
                    parts = [a + b for a, b in zip(parts[::2], parts[1::2])]
                cols.append(parts[0].astype(F32))
            return cols[0] if len(cols) == 1 else jnp.concatenate(cols, axis=1)

        def pair(j, acc):
            return acc + hits(key_ref[2 * j]) + hits(key_ref[2 * j + 1])

        acc = lax.fori_loop(0, nkb // 2, pair, jnp.zeros((packed_rows, tq), F32))
        acc = lax.fori_loop(2 * (nkb // 2), nkb, lambda kb, a: a + hits(key_ref[kb]), acc)
        return jnp.sum(acc, axis=0, keepdims=True)

    def pattern_value(u):
        bits = jnp.where(u >= RANK_ZERO, u - RANK_ZERO, (~u) & 0xFFFF)
        return lax.bitcast_convert_type(lax.shift_left(bits, 16), F32).astype(KEY_DT)

    def bit_body(b, u):
        cand_u = u | lax.shift_left(jnp.int32(1), 15 - b)
        cnt = count(pattern_value(cand_u), False)
        return jnp.where(cnt >= n_sel, cand_u, u)

    u_thr = jnp.maximum(lax.fori_loop(0, 16, bit_body, jnp.zeros((1, tq), I32)), RANK_NEG_INF)
    thr_16 = pattern_value(u_thr)

    def finer():
        lo = thr_16.astype(F32)
        hi = jnp.where(u_thr >= RANK_POS_INF, jnp.inf, pattern_value(u_thr + 1).astype(F32))

        def halve(_, lo_hi):
            lo, hi = lo_hi
            cand = (0.5 * lo + 0.5 * hi).astype(KEY_DT)
            enough = count(cand, False) >= n_sel
            return jnp.where(enough, cand.astype(F32), lo), jnp.where(enough, hi, cand.astype(F32))

        thr_f = lax.fori_loop(0, 16, halve, (lo, hi))[0].astype(KEY_DT)
        return thr_f, count(thr_f, True)

    above_16 = count(thr_16, True)
    thr_k, above = lax.cond(jnp.min(n_sel - above_16) <= 0.0, finer, lambda: (thr_16, above_16))
    need = jnp.maximum(n_sel - above, 0.0)
    thr = thr_k.astype(F32)

    aq6 = jnp.concatenate([aqt_ref[hd * LANES:(hd + 1) * LANES, :] for hd in range(A_HEADS)], axis=1)

    half = tk // 2
    lower = jnp.where(_lane_iota((half, half)) <= _row_iota((half, half)), 1.0, 0.0).astype(BF16)

    def attend():
        def body(kb, carry):
            eq_seen, ms, accs = carry
            blk = key_ref[kb].astype(F32)
            eq = blk == thr
            eq_f = jnp.where(eq, 1.0, 0.0)
            prefs = []
            for e in (eq_f[:half], eq_f[half:]):
                prefs.append(jnp.dot(lower, e.astype(BF16), preferred_element_type=F32) + eq_seen)
                eq_seen = eq_seen + _sum_keys(e)
            slack = jnp.where(blk >= thr, need - jnp.where(eq, jnp.concatenate(prefs, axis=0), 0.0), -1.0)
            bias = jnp.where(slack >= 0.0, jnp.where(jnp.abs(blk) < jnp.inf, 0.0, NEG_INF), NEG_INF)
            logits = jnp.dot(keys(kb), aq6, preferred_element_type=F32)
            vts = (avt_ref[2 * kb], avt_ref[2 * kb + 1])
            new_ms, new_accs = [], []
            for p in range(A_HEADS // 2):
                ps, alphas = [], []
                for hd in (2 * p, 2 * p + 1):
                    lg = logits[:, hd * tq:(hd + 1) * tq] + bias
                    m_old = ms[hd]
                    m_new = jnp.maximum(m_old, _max_keys(lg))
                    m_safe = jnp.where(m_new == NEG_INF, 0.0, m_new)
                    ps.append(jnp.exp2(lg - m_safe).astype(BF16))
                    alphas.append(jnp.exp2(m_old - m_safe))
                    new_ms.append(m_new)
                p2 = jnp.concatenate(ps, axis=1)
                pv = (jnp.dot(vts[0], p2[:tk // 2], preferred_element_type=F32)
                      + jnp.dot(vts[1], p2[tk // 2:], preferred_element_type=F32))
                new_accs.append(jnp.concatenate(alphas, axis=1) * accs[p] + pv)
            return eq_seen, tuple(new_ms), tuple(new_accs)

        init = (jnp.zeros((1, tq), F32),
                tuple(jnp.full((1, tq), NEG_INF, F32) for _ in range(A_HEADS)),
                tuple(jnp.zeros((V_ROWS, 2 * tq), F32) for _ in range(A_HEADS // 2)))
        return lax.fori_loop(0, nkb, body, init)[2]

    accs = attend()

    outs = []
    for p in range(A_HEADS // 2):
        o2 = accs[p][0:HEAD_DIM] / accs[p][HEAD_DIM:HEAD_DIM + 1]
        outs += [o2[:, 0:tq], o2[:, tq:2 * tq]]
    o_ref[...] = jnp.concatenate(outs, axis=0).T.astype(o_ref.dtype)


def _dsa(aqt, iqt, smt, kik, avt, *, tq, n_keys, q_pos0, n_sel):
    bsz, _, t_q = aqt.shape
    _, nblk, _, tkv = avt.shape
    assert nblk % 2 == 0
    tk = 2 * tkv
    kern = functools.partial(_dsa_kernel, tq=tq, tk=tk, n_keys=n_keys, q_pos0=q_pos0, n_sel=n_sel)
    return pl.pallas_call(
        kern,
        grid=(bsz, t_q // tq),
        in_specs=[pl.BlockSpec((None, A_HEADS * LANES, tq), lambda b, i: (b, 0, i)),
                  pl.BlockSpec((None, IDX_HEADS * LANES, tq), lambda b, i: (b, 0, i)),
                  pl.BlockSpec((None, 16, tq), lambda b, i: (b, 0, i)),
                  pl.BlockSpec((None, nblk * tkv, LANES), lambda b, i: (b, 0, 0)),
                  pl.BlockSpec((None, nblk, V_ROWS, tkv), lambda b, i: (b, 0, 0, 0))],
        out_specs=pl.BlockSpec((None, tq, 384), lambda b, i: (b, i, 0)),
        out_shape=jax.ShapeDtypeStruct((bsz, t_q, 384), BF16),
        scratch_shapes=[pltpu.VMEM((nblk // 2, tk, tq), KEY_DT)],
        compiler_params=_cparams(("parallel", "arbitrary")),
        name="dsa_attention",
    )(aqt, iqt, smt, kik, avt)


def _fox_kernel(qt_ref, k_ref, vt_ref, ck_ref, o_ref, *, tq, tk, n_keys, q_pos0):
    i = pl.program_id(1)
    pos_first = q_pos0 + i * tq
    n_full = pos_first // tk
    nkb = (jnp.minimum(pos_first + tq, n_keys) + tk - 1) // tk
    q_pos = pos_first + _lane_iota((1, tq))
    key_row = _row_iota((tk, tq))
    piece_row = _row_iota((LANES, tq))
    qts = []
    for hd in range(B_HEADS):
        minus = jnp.where((piece_row >= 3 * (hd % 2)) & (piece_row < 3 * (hd % 2) + 3), -1.0, 0.0).astype(BF16)
        qts.append(jnp.concatenate([qt_ref[hd * LANES:(hd + 1) * LANES, :], minus], axis=0))

    def step(kbs, state, masked):
        logits = []
        for hd in range(B_HEADS):
            for kb in kbs:
                rows = pl.ds(pl.multiple_of(kb * tk, tk), tk)
                kblk = jnp.concatenate([k_ref[rows, (hd // 2) * LANES:(hd // 2 + 1) * LANES],
                                        ck_ref[hd // 2, rows, :]], axis=1)
                lg = jnp.dot(kblk, qts[hd], preferred_element_type=F32)
                if masked:
                    lg = jnp.where(key_row <= q_pos - kb * tk, lg, NEG_INF)
                logits.append(lg)
        new = []
        for hd in range(B_HEADS):
            m_old, acc = state[hd]
            lgs = logits[hd * len(kbs):(hd + 1) * len(kbs)]
            m_new = m_old
            for lg in lgs:
                m_new = jnp.maximum(m_new, _max_keys(lg))
            m_safe = jnp.where(m_new == NEG_INF, 0.0, m_new) if masked else m_new
            acc = jnp.exp2(m_old - m_safe) * acc
            for kb, lg in zip(kbs, lgs):
                p = jnp.exp2(lg - m_safe).astype(BF16)
                acc = acc + jnp.dot(vt_ref[hd, kb], p, preferred_element_type=F32)
            new.append((m_new, acc))
        return tuple(new)

    init = tuple((jnp.full((1, tq), NEG_INF, F32), jnp.zeros((V_ROWS, tq), F32)) for _ in range(B_HEADS))
    state = lax.fori_loop(0, n_full // 2, lambda j, st: step((2 * j, 2 * j + 1), st, False), init)
    state = lax.fori_loop(2 * (n_full // 2), nkb, lambda kb, st: step((kb,), st, True), state)
    outs = [acc[0:HEAD_DIM] / acc[HEAD_DIM:HEAD_DIM + 1] for _, acc in state]
    o_ref[...] = jnp.concatenate(outs, axis=0).T.astype(o_ref.dtype)


def _fox(qt, k, vt, ck, *, tq, n_keys, q_pos0):
    bsz, _, t_q = qt.shape
    _, _, nblk, _, tk = vt.shape
    lpad = nblk * tk
    kern = functools.partial(_fox_kernel, tq=tq, tk=tk, n_keys=n_keys, q_pos0=q_pos0)
    return pl.pallas_call(
        kern,
        grid=(bsz, t_q // tq),
        in_specs=[pl.BlockSpec((None, B_HEADS * LANES, tq), lambda b, i: (b, 0, i)),
                  pl.BlockSpec((None, lpad, 384), lambda b, i: (b, 0, 0)),
                  pl.BlockSpec((None, B_HEADS, nblk, V_ROWS, tk), lambda b, i: (b, 0, 0, 0, 0)),
                  pl.BlockSpec((None, B_HEADS // 2, lpad, LANES), lambda b, i: (b, 0, 0, 0))],
        out_specs=pl.BlockSpec((None, tq, 384), lambda b, i: (b, i, 0)),
        out_shape=jax.ShapeDtypeStruct((bsz, t_q, 384), BF16),
        compiler_params=_cparams(("parallel", "arbitrary")),
        name="fox_attention",
    )(qt, k, vt, ck)


def _pool_kernel(cur_ref, prev_ref, hist_ref, w_ref, s_ref, o_ref, ext, *, tc, start_pos):
    i = pl.program_id(1)
    cur = cur_ref[...]
    ext[0:16, :] = jnp.where(i == 0, hist_ref[...], prev_ref[tc - 16:, :])
    ext[16:, :] = cur
    pos = start_pos + i * tc + lax.broadcasted_iota(I32, (tc, POOL_WIDTH), 0)
    lane = _lane_iota((tc, POOL_WIDTH))
    run = cur
    pooled = jnp.zeros_like(cur)
    k = 1
    for g, w in enumerate(POOL_WINDOWS):
        while k < w:
            run = run + ext[16 - k:16 - k + tc, :]
            k += 1
        cnt = jnp.minimum(pos + 1, w).astype(F32)
        in_group = (lane >= g * POOL_GROUP_DIM) & (lane < (g + 1) * POOL_GROUP_DIM)
        pooled = jnp.where(in_group, run / cnt, pooled)
    z = (pooled - cur).astype(BF16)
    o_ref[...] = (jnp.dot(z, w_ref[...], preferred_element_type=F32) * s_ref[...]).astype(o_ref.dtype)


def _pool(cu, hist16, w_bd, scale, *, tc, start_pos):
    bsz, t, n = cu.shape
    kern = functools.partial(_pool_kernel, tc=tc, start_pos=start_pos)
    return pl.pallas_call(
        kern,
        grid=(bsz, t // tc),
        in_specs=[pl.BlockSpec((None, tc, n), lambda b, i: (b, i, 0)),
                  pl.BlockSpec((None, tc, n), lambda b, i: (b, jnp.maximum(i - 1, 0), 0)),
                  pl.BlockSpec((None, 16, n), lambda b, i: (b, 0, 0)),
                  _const_spec((n, n)), _const_spec((1, n))],
        out_specs=pl.BlockSpec((None, tc, n), lambda b, i: (b, i, 0)),
        out_shape=jax.ShapeDtypeStruct((bsz, t, n), BF16),
        scratch_shapes=[pltpu.VMEM((16 + tc, n), F32)],
        compiler_params=_cparams(("parallel", "arbitrary")),
        name="pool_mixer",
    )(cu, cu, hist16, w_bd, scale.reshape(1, n))


def _route(logits):
    lane = _lane_iota(logits.shape).astype(F32)
    lg = jnp.where(lane < N_EXPERTS, logits, NEG_INF)
    m1 = jnp.max(lg, axis=1, keepdims=True)
    i1 = jnp.min(jnp.where(lg == m1, lane, float(LANES)), axis=1, keepdims=True)
    hot1 = lane == i1
    lg2 = jnp.where(hot1, NEG_INF, lg)
    m2 = jnp.max(lg2, axis=1, keepdims=True)
    i2 = jnp.min(jnp.where(lg2 == m2, lane, float(LANES)), axis=1, keepdims=True)
    hot2 = lane == i2
    e2 = jnp.exp(m2 - m1)
    den = 1.0 + e2
    return jnp.where(hot1, 1.0 / den, 0.0) + jnp.where(hot2, e2 / den, 0.0)


def _final_norm(x, gain):
    ms = jnp.mean(x * x, axis=-1, keepdims=True)
    return x * lax.rsqrt(ms + NORM_EPS) * gain


def _merge_kernel(x_ref, oa_ref, ob_ref, oc_ref, gate_ref, g1_ref, sc2_ref, sh2_ref, g_ref,
                  wa_ref, wb_ref, wc_ref, wo_ref, *rest, moe, final):
    if moe:
        rw_ref, rb_ref, xo_ref, h_ref, gw_ref = rest
    elif final:
        g2_ref, wg_ref, wu_ref, wd_ref, fg_ref, xo_ref = rest
    else:
        g2_ref, wg_ref, wu_ref, wd_ref, xo_ref = rest
    d = D_MODEL
    merged = (gate_ref[:, 0:d] * jnp.dot(oa_ref[...], wa_ref[...], preferred_element_type=F32)
              + gate_ref[:, d:2 * d] * jnp.dot(ob_ref[...], wb_ref[...], preferred_element_type=F32)
              + gate_ref[:, 2 * d:3 * d] * jnp.dot(oc_ref[...], wc_ref[...], preferred_element_type=F32))
    x = x_ref[...] + g1_ref[...] * jnp.dot(merged.astype(BF16), wo_ref[...], preferred_element_type=F32)
    ms = jnp.mean(x * x, axis=-1, keepdims=True)
    y = x * lax.rsqrt(ms + NORM_EPS) * g_ref[...]
    h = y * (1.0 + sc2_ref[...]) + sh2_ref[...]
    if not moe:
        hb = h.astype(BF16)
        n_chunks = 2
        tf = wg_ref.shape[1] // n_chunks
        acc = jnp.zeros(x.shape, F32)
        for c in range(n_chunks):
            gt = jnp.dot(hb, wg_ref[:, c * tf:(c + 1) * tf], preferred_element_type=F32)
            up = jnp.dot(hb, wu_ref[:, c * tf:(c + 1) * tf], preferred_element_type=F32)
            act = (gt * jax.nn.sigmoid(gt) * up).astype(BF16)
            acc = acc + jnp.dot(act, wd_ref[c * tf:(c + 1) * tf, :], preferred_element_type=F32)
        x = x + g2_ref[...] * acc
        xo_ref[...] = _final_norm(x, fg_ref[...]) if final else x
    else:
        xo_ref[...] = x
        h_ref[...] = h.astype(BF16)
        h_hi = h.astype(BF16)
        h_lo = (h - h_hi.astype(F32)).astype(BF16)
        logits = (jnp.dot(h_hi, rw_ref[0], preferred_element_type=F32)
                  + jnp.dot(h_lo, rw_ref[0], preferred_element_type=F32)
                  + jnp.dot(h_hi, rw_ref[1], preferred_element_type=F32)) + rb_ref[...]
        gw_ref[...] = _route(logits)


def _merge(x, oa, ob, oc, gates, g1, sc2, sh2, g, wa, wb, wc, wo, router, ffn, tm):
    assert (router is None) != (ffn is None)
    bsz, t, d = x.shape
    mrows = g1.shape[1]
    mblk = 1 if mrows == 1 else tm
    mod_spec = pl.BlockSpec((None, mblk, d), (lambda b, i: (b, 0, 0)) if mrows == 1 else (lambda b, i: (b, i, 0)))

    def tok(n):
        return pl.BlockSpec((None, tm, n), lambda b, i: (b, i, 0))

    in_specs = [tok(d), tok(384), tok(384), tok(256), tok(3 * d), mod_spec, mod_spec, mod_spec,
                _const_spec((1, d)), _const_spec(wa.shape), _const_spec(wb.shape), _const_spec(wc.shape),
                _const_spec(wo.shape)]
    args = [x, oa, ob, oc, gates, g1, sc2, sh2, g.reshape(1, d), wa, wb, wc, wo]
    if router is not None:
        rw, rb = router
        in_specs += [_const_spec(rw.shape), _const_spec(rb.shape)]
        args += [rw, rb]
        out_specs = [tok(d), tok(d), tok(LANES)]
        out_shape = [jax.ShapeDtypeStruct((bsz, t, d), F32), jax.ShapeDtypeStruct((bsz, t, d), BF16),
                     jax.ShapeDtypeStruct((bsz, t, LANES), F32)]
    else:
        g2, wg, wu, wd, final_g = ffn
        in_specs += [mod_spec, _const_spec(wg.shape), _const_spec(wu.shape), _const_spec(wd.shape)]
        args += [g2, wg, wu, wd]
        if final_g is not None:
            in_specs.append(_const_spec((1, d)))
            args.append(final_g.reshape(1, d))
        out_specs = [tok(d)]
        out_shape = [jax.ShapeDtypeStruct((bsz, t, d), F32)]
    return pl.pallas_call(
        functools.partial(_merge_kernel, moe=router is not None, final=ffn is not None and ffn[4] is not None),
        grid=(bsz, t // tm),
        in_specs=in_specs, out_specs=out_specs, out_shape=out_shape,
        compiler_params=_cparams(("parallel", "parallel")),
        name="merge_out" if router is not None else "merge_ffn",
    )(*args)


def _moe_kernel(x_ref, h_ref, g2_ref, gw_ref, wg_ref, wu_ref, wd_ref, *rest, final):
    if final:
        fg_ref, o_ref, acc_ref, posc_ref, posr_ref = rest
    else:
        o_ref, acc_ref, posc_ref, posr_ref = rest
    e = pl.program_id(2)
    tm = h_ref.shape[0]
    n_slabs = tm // MOE_CHUNK

    @pl.when(e == 0)
    def _():
        acc_ref[...] = jnp.zeros_like(acc_ref)
        routed = gw_ref[...] != 0.0
        r_f = jnp.where(routed, 1.0, 0.0)
        r_b = r_f.astype(BF16)
        r_t = r_f.T
        r_tb = r_t.astype(BF16)
        tok_l = _lane_iota((MOE_CHUNK, tm))
        tok_r = _row_iota((MOE_CHUNK, tm))
        rank_r = jnp.zeros((LANES, tm), F32)
        for s in range(n_slabs):
            rows = slice(s * MOE_CHUNK, (s + 1) * MOE_CHUNK)
            earlier = jnp.where(tok_l < tok_r + s * MOE_CHUNK, 1.0, 0.0).astype(BF16)
            rank_c = jnp.dot(earlier, r_b, preferred_element_type=F32)
            posc_ref[rows, :] = jnp.where(routed[rows], rank_c, -1.0)
            later = jnp.where(tok_r + s * MOE_CHUNK < tok_l, 1.0, 0.0).astype(BF16)
            rank_r = rank_r + jnp.dot(r_tb[:, rows], later, preferred_element_type=F32)
        posr_ref[...] = jnp.where(r_t != 0.0, rank_r, -1.0)

    lane_e = _lane_iota((tm, LANES)) == e
    pos_c = jnp.sum(jnp.where(lane_e, posc_ref[...], 0.0), axis=1, keepdims=True)
    gate_c = jnp.sum(jnp.where(lane_e, gw_ref[...], 0.0), axis=1, keepdims=True)
    pos_r = posr_ref[pl.ds(e, 1), :]
    n_routed = (jnp.max(pos_r) + 1.0).astype(I32)

    def run_chunk(base, n_rows):
        slot_rows = _row_iota((n_rows, tm)).astype(F32)
        slot_lanes = _lane_iota((MOE_CHUNK, n_rows)).astype(F32)
        pack = jnp.where(pos_r - base == slot_rows, 1.0, 0.0).astype(BF16)
        xc = jnp.dot(pack, h_ref[...], preferred_element_type=F32).astype(BF16)
        gt = jnp.dot(xc, wg_ref[...], preferred_element_type=F32)
        up = jnp.dot(xc, wu_ref[...], preferred_element_type=F32)
        act = (gt * jax.nn.sigmoid(gt) * up).astype(BF16)
        y = jnp.dot(act, wd_ref[...], preferred_element_type=F32).astype(BF16)
        for s in range(n_slabs):
            rows = slice(s * MOE_CHUNK, (s + 1) * MOE_CHUNK)
            unpack = jnp.where(pos_c[rows] - base == slot_lanes, 1.0, 0.0).astype(BF16)
            acc_ref[rows, :] += gate_c[rows] * jnp.dot(unpack, y, preferred_element_type=F32)

    def first(c, carry):
        run_chunk(0.0, MOE_CHUNK)
        return carry

    def later(c, carry):
        run_chunk((MOE_CHUNK + c * (MOE_CHUNK // 2)).astype(F32), MOE_CHUNK // 2)
        return carry

    lax.fori_loop(0, jnp.minimum(n_routed, 1), first, 0)
    n_later = (jnp.maximum(n_routed - MOE_CHUNK, 0) + MOE_CHUNK // 2 - 1) // (MOE_CHUNK // 2)
    lax.fori_loop(0, n_later, later, 0)

    @pl.when(e == pl.num_programs(2) - 1)
    def _():
        x = x_ref[...] + g2_ref[...] * acc_ref[...]
        o_ref[...] = _final_norm(x, fg_ref[...]) if final else x


def _moe(x, h, g2, gw, wg, wu, wd, final_g, tm):
    bsz, t, d = x.shape
    n_e, _, dff = wg.shape
    mrows = g2.shape[1]
    mblk = 1 if mrows == 1 else tm
    mod_spec = pl.BlockSpec((None, mblk, d), (lambda b, i, e: (b, 0, 0)) if mrows == 1 else (lambda b, i, e: (b, i, 0)))
    tok = pl.BlockSpec((None, tm, d), lambda b, i, e: (b, i, 0))
    tok_once = pl.BlockSpec((None, tm, d), lambda b, i, e: (b, i, 0), pipeline_mode=pl.Buffered(1))
    in_specs = [tok_once, tok, mod_spec, pl.BlockSpec((None, tm, LANES), lambda b, i, e: (b, i, 0)),
                pl.BlockSpec((None, d, dff), lambda b, i, e: (e, 0, 0)),
                pl.BlockSpec((None, d, dff), lambda b, i, e: (e, 0, 0)),
                pl.BlockSpec((None, dff, d), lambda b, i, e: (e, 0, 0))]
    args = [x, h, g2, gw, wg, wu, wd]
    if final_g is not None:
        in_specs.append(pl.BlockSpec((1, d), lambda b, i, e: (0, 0)))
        args.append(final_g.reshape(1, d))
    return pl.pallas_call(
        functools.partial(_moe_kernel, final=final_g is not None),
        grid=(bsz, t // tm, n_e),
        in_specs=in_specs, out_specs=tok,
        out_shape=jax.ShapeDtypeStruct((bsz, t, d), F32),
        scratch_shapes=[pltpu.VMEM((tm, d), F32), pltpu.VMEM((tm, LANES), F32), pltpu.VMEM((LANES, tm), F32)],
        compiler_params=_cparams(("parallel", "parallel", "arbitrary")),
        name="moe_routed",
    )(*args)


def _rope_tables(pos):
    inv = ROPE_THETA ** (-jnp.arange(HALF, dtype=F32) / HALF)
    ang = pos.astype(F32)[:, None] * inv[None, :]
    cos, sin = jnp.cos(ang), jnp.sin(ang)
    return jnp.tile(cos, (1, 4)), jnp.tile(jnp.concatenate([-sin, sin], axis=1), (1, 2))


def _pick_tile(n, pref):
    t = min(n, pref)
    while n % t:
        t //= 2
    return t


def _per_seq_cols(a, bsz, t, width):
    f = a.shape[1]
    a = jnp.moveaxis(a[0].reshape(f, bsz, t), 1, 0)
    return jnp.pad(a, ((0, 0), (0, 0), (0, width - t)))


def _value_blocks(past_vt, new_vt, bsz, t, lpad):
    lead = new_vt.shape[:-2]
    new_b = jnp.moveaxis(new_vt.reshape(*lead, V_ROWS, bsz, t), -2, 0)
    full = jnp.concatenate([past_vt, new_b], axis=-1)
    full = jnp.pad(full, [(0, 0)] * (full.ndim - 1) + [(0, lpad - full.shape[-1])])
    full = full.reshape(bsz, *lead, V_ROWS, lpad // KV_BLOCK, KV_BLOCK)
    return jnp.moveaxis(full, -2, -3)


def _with_ones_rows(vt):
    ones = jnp.ones(vt.shape[:-2] + (1, vt.shape[-1]), vt.dtype)
    zeros = jnp.zeros(vt.shape[:-2] + (V_ROWS - HEAD_DIM - 1, vt.shape[-1]), vt.dtype)
    return jnp.concatenate([vt, ones, zeros], axis=-2)


def _mixers(inp, past, n_past, lw, bsz, t):
    aqt, iqt, kik, avt, bqt, bk, bvt, cu, sm, smt, *ck = inp
    n_keys = n_past + t
    n_sel = min(TOPK_MAX, n_keys // 4)
    lpad = -(-n_keys // (2 * KV_BLOCK)) * 2 * KV_BLOCK
    if past is None:
        kik_all, avt_all, bk_all, bvt_all, logf_all = kik, avt, bk, bvt, sm
        hist16 = jnp.zeros((bsz, 16, POOL_WIDTH), F32)
        tq_a, tq_b, t_pad = _pick_tile(t, 512), _pick_tile(t, 256), t
    else:
        pa, pb, plf, pc = past
        pk, pv, pik = (pa[:, :, j].astype(BF16) for j in range(3))

        def join_rows(p, new):
            full = jnp.concatenate([p, new.reshape(bsz, t, new.shape[-1])], axis=1)
            return jnp.pad(full, ((0, 0), (0, lpad - n_keys), (0, 0)))

        kik_all = join_rows(jnp.concatenate([pk, pik], axis=-1), kik)
        avt_all = _value_blocks(_with_ones_rows(jnp.swapaxes(pv, 1, 2)), avt[0, 0], bsz, t, lpad)
        bk_all = join_rows(pb[:, :, 0].astype(BF16).reshape(bsz, n_past, 384), bk)
        pvt = jnp.transpose(pb[:, :, 1].astype(BF16), (0, 2, 3, 1))
        bvt_all = _value_blocks(_with_ones_rows(pvt), bvt[0, :, 0], bsz, t, lpad)
        logf_all = join_rows(jnp.pad(plf, ((0, 0), (0, 0), (0, LANES - B_HEADS))), sm)
        hist16 = jnp.pad(pc, ((0, 0), (1, 0), (0, 0)))
        tq_a = tq_b = t_pad = LANES
        aqt, iqt, bqt, smt = (_per_seq_cols(a, bsz, t, t_pad) for a in (aqt, iqt, bqt, smt))

    oa = _dsa(aqt, iqt, smt, kik_all, avt_all, tq=tq_a, n_keys=n_keys, q_pos0=n_past, n_sel=n_sel)
    ck = ck[0] if ck else _cum_logf(logf_all, 2 * KV_BLOCK)
    ob = _fox(bqt, bk_all, bvt_all, ck, tq=tq_b, n_keys=n_keys, q_pos0=n_past)
    cu = cu.reshape(bsz, t, POOL_WIDTH)
    oc = _pool(cu, hist16, lw["pool_bd"], lw["pool_scale"], tc=_pick_tile(t, 1024), start_pos=n_past)
    return oa[:, :t], ob[:, :t], oc


def _layer(x, mod, past, n_past, pos_tab, lw, layer, final_g, per_token):
    bsz, t, d = x.shape
    sh1, sc1, g1, sh2, sc2, g2 = mod
    if per_token:
        xt = x.reshape(1, bsz * t, d)
        sh1, sc1, g1, sh2, sc2, g2 = (jnp.broadcast_to(m, (bsz, t, d)).reshape(1, bsz * t, d) for m in mod)
        cos, sin = (jnp.tile(a, (bsz, 1)) for a in pos_tab)
    else:
        xt = x
        cos, sin = pos_tab
    tm = KV_BLOCK
    (aqt, iqt, nat, kik, avt, nbt, bqt, bk, bvt, cu, sm, smt, gates, *ck) = _in_proj(
        xt, sc1, sh1, lw["norm_mix_g"], lw["w_in"], lw["bf_bias"], cos, sin, tm, with_decay=past is None)
    oa, ob, oc = _mixers((aqt, iqt, kik, avt, bqt, bk, bvt, cu, sm, smt, *ck), past, n_past, lw, bsz, t)

    def flat(a):
        return a.reshape(xt.shape[0], xt.shape[1], a.shape[-1])

    router = (lw["router_w"], lw["router_b"]) if layer % 2 else None
    ffn = None if layer % 2 else (g2, lw["ffn_wg"], lw["ffn_wu"], lw["ffn_wd"], final_g)
    res = _merge(xt, flat(oa), flat(ob), flat(oc), gates, g1, sc2, sh2, lw["norm_ffn_g"],
                 lw["w_br_a"], lw["w_br_b"], lw["w_br_c"], lw["w_out"], router, ffn, _pick_tile(xt.shape[1], 512))
    if layer % 2 == 0:
        (x_new,) = res
    else:
        x_mid, h2, gw = res
        x_new = _moe(x_mid, h2, g2, gw, lw["moe_wg"], lw["moe_wu"], lw["moe_wd"], final_g,
                     _pick_tile(xt.shape[1], 1024))

    def token_major(a, *feat):
        a = a.reshape(a.shape[0], *feat, -1, t) if per_token else a.reshape(a.shape[0], *feat, 1, t)
        a = jnp.moveaxis(a, (-2, -1), (1, 2))
        return a.reshape(bsz, t, *feat)

    new_a = token_major(nat, 3, HEAD_DIM)
    new_b = token_major(nbt, 2, B_HEADS, HEAD_DIM)
    new_logf = sm.reshape(bsz, t, LANES)[:, :, :B_HEADS]
    new_pool = cu.reshape(bsz, t, POOL_WIDTH)[:, t - POOL_HIST:, :]
    return x_new.reshape(bsz, t, d), (new_a, new_b, new_logf, new_pool)


def kernel(x_prompt, x_sample, cache_a_kvi, cache_b_kv, cache_b_logf, state_c_pool, c_prompt, c_sample,
           ada_w, ada_b, norm_mix_g, w_in, b_forget, pool_w, pool_scale, w_br_a, w_br_b, w_br_c, w_out,
           norm_ffn_g, ffn_w_gate, ffn_w_up, ffn_w_down, moe_router_w, moe_router_b, moe_w_gate,
           moe_w_up, moe_w_down, final_norm_g):
    depth = ada_w.shape[0]
    bp, tp, d = x_prompt.shape
    bs, ts, _ = x_sample.shape
    n_past = cache_a_kvi.shape[2]
    assert tp % KV_BLOCK == 0 and (bs * ts) % KV_BLOCK == 0 and ts <= LANES

    rows = -(-(bp + bs) // 8) * 8
    c_all = jnp.pad(jnp.concatenate([c_prompt, c_sample], axis=0), ((0, rows - bp - bs), (0, 0)))
    mod_all = _ada(c_all, ada_w, ada_b)

    tab_p = _rope_tables(jnp.arange(tp))
    tab_s = _rope_tables(n_past + jnp.arange(ts))

    xp, xs = x_prompt, x_sample
    outs_p, outs_s = [], []
    for layer in range(depth):
        j = layer // 2
        w_l, bias_l = _in_weights(w_in[layer], b_forget[layer])
        pw = pool_w[layer]
        pool_bd = jnp.zeros((POOL_WIDTH, POOL_WIDTH), F32)
        for g in range(len(POOL_WINDOWS)):
            sl = slice(g * POOL_GROUP_DIM, (g + 1) * POOL_GROUP_DIM)
            pool_bd = pool_bd.at[sl, sl].set(pw[g])
        lw = dict(w_in=w_l, bf_bias=bias_l, norm_mix_g=norm_mix_g[layer], norm_ffn_g=norm_ffn_g[layer],
                  pool_bd=pool_bd.astype(BF16), pool_scale=pool_scale[layer],
                  w_br_a=w_br_a[layer].astype(BF16), w_br_b=w_br_b[layer].astype(BF16),
                  w_br_c=w_br_c[layer].astype(BF16), w_out=w_out[layer].astype(BF16))
        if layer % 2 == 0:
            lw.update(ffn_wg=ffn_w_gate[j].astype(BF16), ffn_wu=ffn_w_up[j].astype(BF16),
                      ffn_wd=ffn_w_down[j].astype(BF16))
        else:
            rw = jnp.pad(moe_router_w[j], ((0, 0), (0, LANES - N_EXPERTS)))
            rw_hi = rw.astype(BF16)
            lw.update(router_w=jnp.stack([rw_hi, (rw - rw_hi.astype(F32)).astype(BF16)]),
                      router_b=jnp.pad(moe_router_b[j], (0, LANES - N_EXPERTS)).reshape(1, LANES),
                      moe_wg=moe_w_gate[j].astype(BF16), moe_wu=moe_w_up[j].astype(BF16),
                      moe_wd=moe_w_down[j].astype(BF16))
        final_g = final_norm_g if layer == depth - 1 else None
        mod_p = [m[:, None, :] for m in jnp.split(mod_all[layer, :bp], 6, axis=-1)]
        mod_s = [m[:, None, :] for m in jnp.split(mod_all[layer, bp:bp + bs], 6, axis=-1)]
        xp, new_p = _layer(xp, mod_p, None, 0, tab_p, lw, layer, final_g, per_token=False)
        past = (cache_a_kvi[layer], cache_b_kv[layer], cache_b_logf[layer], state_c_pool[layer])
        xs, new_s = _layer(xs, mod_s, past, n_past, tab_s, lw, layer, final_g, per_token=True)
        outs_p.append(new_p)
        outs_s.append(new_s)

    def stack(outs, k):
        return jnp.stack([o[k] for o in outs])

    return (xp, xs,
            stack(outs_p, 0), stack(outs_p, 1), stack(outs_p, 2), stack(outs_p, 3),
            stack(outs_s, 0), stack(outs_s, 1), stack(outs_s, 2), stack(outs_s, 3))
```

```python
import functools

import jax
import jax.numpy as jnp
import numpy as np
from jax import lax
from jax.experimental import pallas as pl
from jax.experimental.pallas import tpu as pltpu

F32 = jnp.float32
BF16 = jnp.bfloat16
I32 = jnp.int32

D_MODEL = 1024
CHUNK = 64
HEAD_DIM = 64
HALF = HEAD_DIM // 2
ROPE_THETA = 10000.0
NORM_EPS = 1e-6
A_HEADS = 6
IDX_HEADS = 4
TOPK_MAX = 256
B_HEADS = 6
POOL_WINDOWS = (2, 4, 8, 16)
POOL_GROUP_DIM = 64
POOL_WIDTH = 256
POOL_HIST = 15
N_EXPERTS = 8
LANES = 128
SUBLANES = 8
LOG2E = 1.4426950408889634
QK_SCALE = HEAD_DIM ** -0.5 * LOG2E
KV_BLOCK = 256
V_ROWS = HEAD_DIM + 16
MOE_CHUNK = 256
VMEM_LIMIT = 56 * 1024 * 1024
NEG_INF = float("-inf")
KEY_DT = jnp.bfloat16
RANK_ZERO = 0x8000
RANK_NEG_INF = 0x007F
RANK_POS_INF = 0xFF80
IW_LANE = 8

C_AQ, C_IQ, C_A, C_B, C_CU, C_SM, C_GATE, C_END = 0, 384, 640, 896, 2048, 2304, 2432, 5504


def _cparams(sem):
    return pltpu.CompilerParams(dimension_semantics=sem, vmem_limit_bytes=VMEM_LIMIT)


def _const_spec(shape):
    nd = len(shape)
    return pl.BlockSpec(shape, lambda *_: (0,) * nd, pipeline_mode=pl.Buffered(1))


def _lane_iota(shape):
    return lax.broadcasted_iota(I32, shape, len(shape) - 1)


def _row_iota(shape):
    return lax.broadcasted_iota(I32, shape, len(shape) - 2)


def _ada_kernel(c_ref, w_ref, b_ref, o_ref):
    c = c_ref[...]
    s = c * jax.nn.sigmoid(c)
    w = w_ref[...]
    s_hi, w_hi = s.astype(BF16), w.astype(BF16)
    s_lo, w_lo = (s - s_hi.astype(F32)).astype(BF16), (w - w_hi.astype(F32)).astype(BF16)
    o_ref[...] = (jnp.dot(s_hi, w_hi, preferred_element_type=F32) + jnp.dot(s_lo, w_hi, preferred_element_type=F32)
                  + jnp.dot(s_hi, w_lo, preferred_element_type=F32)) + b_ref[...]


def _ada(c_all, ada_w, ada_b):
    depth, d, n = ada_w.shape
    rows = c_all.shape[0]
    tn = 1536
    return pl.pallas_call(
        _ada_kernel,
        grid=(depth, n // tn),
        in_specs=[pl.BlockSpec((rows, d), lambda l, j: (0, 0)),
                  pl.BlockSpec((None, d, tn), lambda l, j: (l, 0, j)),
                  pl.BlockSpec((None, 1, tn), lambda l, j: (l, 0, j))],
        out_specs=pl.BlockSpec((None, rows, tn), lambda l, j: (l, 0, j)),
        out_shape=jax.ShapeDtypeStruct((depth, rows, n), F32),
        compiler_params=_cparams(("arbitrary", "arbitrary")),
        name="ada_mod",
    )(c_all, ada_w, ada_b.reshape(depth, 1, n))


def _decay_pieces(x, first, o_ref, carry_ref):
    @pl.when(first)
    def _():
        carry_ref[...] = jnp.zeros_like(carry_ref)

    tc = x.shape[0]
    tri = jnp.where(_lane_iota((tc, tc)) <= _row_iota((tc, tc)), 1.0, 0.0).astype(BF16)
    cum = carry_ref[0:1, :]
    rest = x
    for _ in range(3):
        piece = rest.astype(BF16)
        cum = cum + jnp.dot(tri, piece, preferred_element_type=F32)
        rest = rest - piece.astype(F32)
    carry_ref[...] = jnp.broadcast_to(cum[tc - 1:tc, :], carry_ref.shape)
    pieces = []
    rest = cum * LOG2E
    for _ in range(3):
        piece = rest.astype(BF16)
        pieces.append(piece.astype(F32))
        rest = rest - pieces[-1]
    lane = _lane_iota((tc, LANES))
    for p in range(B_HEADS // 2):
        slab = jnp.zeros((tc, LANES), F32)
        for hh in range(2):
            for j, piece in enumerate(pieces):
                dst, src = 3 * hh + j, 2 * p + hh
                slab = jnp.where(lane == dst, pltpu.roll(piece, (dst - src) % LANES, 1), slab)
        o_ref[p] = slab.astype(BF16)


def _in_kernel(x_ref, sc_ref, sh_ref, g_ref, w_ref, bf_ref, cos_ref, sin_ref,
               aqt_ref, iqt_ref, nat_ref, kik_ref, avt_ref, nbt_ref, bqt_ref, bk_ref, bvt_ref,
               cu_ref, sm_ref, smt_ref, gate_ref, *decay):
    x = x_ref[...]
    ms = jnp.mean(x * x, axis=-1, keepdims=True)
    y = x * lax.rsqrt(ms + NORM_EPS) * g_ref[...]
    h = (y * (1.0 + sc_ref[...]) + sh_ref[...]).astype(BF16)
    tm = x.shape[0]

    def mm(a, b):
        return jnp.dot(h, w_ref[:, a:b], preferred_element_type=F32)

    cos = cos_ref[...]
    sin = sin_ref[...]
    lane = _lane_iota((tm, LANES))
    low = lane < HEAD_DIM
    first_half = (lane & HALF) == 0

    def rope(z):
        swapped = jnp.where(first_half, pltpu.roll(z, LANES - HALF, 1), pltpu.roll(z, HALF, 1))
        return z * cos + swapped * sin

    zeros64 = jnp.zeros((HEAD_DIM, tm), BF16)
    ones_rows = jnp.where(_row_iota((V_ROWS - HEAD_DIM, tm)) == 0, 1.0, 0.0).astype(BF16)

    def put_heads(ref, zt, p, slot_even, slot_odd):
        for hh, slot in ((0, slot_even), (1, slot_odd)):
            base = (2 * p + hh) * LANES
            ref[base + slot * HEAD_DIM:base + (slot + 1) * HEAD_DIM, :] = zt[hh * HEAD_DIM:(hh + 1) * HEAD_DIM]
            ref[base + (1 - slot) * HEAD_DIM:base + (2 - slot) * HEAD_DIM, :] = zeros64

    z = mm(C_AQ, C_IQ)
    for p in range(3):
        zt = (rope(z[:, p * LANES:(p + 1) * LANES]) * QK_SCALE).T.astype(BF16)
        put_heads(aqt_ref, zt, p, 0, 0)
    z = mm(C_IQ, C_A)
    for p in range(2):
        zt = rope(z[:, p * LANES:(p + 1) * LANES]).T.astype(BF16)
        put_heads(iqt_ref, zt, p, 1, 1)

    z = mm(C_A, C_B)
    kv = z[:, :LANES]
    r0 = jnp.where(low, rope(kv), kv)
    r1 = rope(z[:, LANES:])
    r0t = r0.T
    nat_ref[0:LANES, :] = r0t
    nat_ref[LANES:, :] = r1.T[0:HEAD_DIM, :]
    kik_ref[...] = jnp.where(low, r0, pltpu.roll(r1, HEAD_DIM, 1)).astype(BF16)
    avt_ref[0:HEAD_DIM, :] = r0t[HEAD_DIM:, :].astype(BF16)
    avt_ref[HEAD_DIM:, :] = ones_rows

    z = mm(C_B, C_CU)
    bk_ref[...] = z[:, 384:768].astype(BF16)
    for p in range(3):
        zt = (z[:, p * LANES:(p + 1) * LANES] * QK_SCALE).T.astype(BF16)
        put_heads(bqt_ref, zt, p, 0, 1)
        nbt_ref[p * LANES:(p + 1) * LANES, :] = z[:, 384 + p * LANES:384 + (p + 1) * LANES].T
        vt = z[:, 768 + p * LANES:768 + (p + 1) * LANES].T
        nbt_ref[384 + p * LANES:384 + (p + 1) * LANES, :] = vt
        vt = vt.astype(BF16)
        for hh in range(2):
            bvt_ref[2 * p + hh, 0:HEAD_DIM, :] = vt[hh * HEAD_DIM:(hh + 1) * HEAD_DIM]
            bvt_ref[2 * p + hh, HEAD_DIM:, :] = ones_rows

    cu_ref[...] = mm(C_CU, C_SM)

    z = mm(C_SM, C_GATE)
    t = z + bf_ref[...]
    logf = jnp.minimum(t, 0.0) - jnp.log1p(jnp.exp(-jnp.abs(t)))
    sm = jnp.where(lane < B_HEADS, logf, z)
    sm_ref[...] = sm
    smt_ref[...] = sm.T[0:16, :]
    if decay:
        ck_ref, carry_ref = decay
        _decay_pieces(sm, pl.program_id(1) == 0, ck_ref, carry_ref)

    for c in range(3):
        gate_ref[:, c * D_MODEL:(c + 1) * D_MODEL] = jax.nn.sigmoid(
            mm(C_GATE + c * D_MODEL, C_GATE + (c + 1) * D_MODEL)).astype(BF16)


def _in_weights(w_in_l, b_forget_l):
    d = w_in_l.shape[0]
    sizes = (384, 64, 64, 256, 4, 64, 384, 384, 384, 6, 256, 3072)
    o = np.concatenate([[0], np.cumsum(sizes)])
    w_in_l = w_in_l.astype(BF16)
    cols = [w_in_l[:, o[0]:o[1]],
            w_in_l[:, o[3]:o[4]],
            w_in_l[:, o[1]:o[3]], w_in_l[:, o[5]:o[6]], jnp.zeros((d, 64), BF16),
            w_in_l[:, o[6]:o[9]],
            w_in_l[:, o[10]:o[11]],
            w_in_l[:, o[9]:o[10]], jnp.zeros((d, IW_LANE - B_HEADS), BF16), w_in_l[:, o[4]:o[5]],
            jnp.zeros((d, LANES - IW_LANE - IDX_HEADS), BF16),
            w_in_l[:, o[11]:o[12]]]
    w = jnp.concatenate(cols, axis=1)
    assert w.shape[1] == C_END, w.shape
    bias = jnp.concatenate([b_forget_l, jnp.zeros((LANES - B_HEADS,), F32)]).reshape(1, LANES)
    return w, bias


def _in_proj(x, sc, sh, g, w, bias, cos, sin, tm, with_decay):
    bsz, t, d = x.shape
    mrows = sc.shape[1]
    mblk = 1 if mrows == 1 else tm
    mod_spec = pl.BlockSpec((None, mblk, d), (lambda b, i: (b, 0, 0)) if mrows == 1 else (lambda b, i: (b, i, 0)))
    nblk = t // tm

    def rows(n, dt):
        return pl.BlockSpec((None, tm, n), lambda b, i: (b, i, 0)), jax.ShapeDtypeStruct((bsz, t, n), dt)

    def cols(n, dt):
        return pl.BlockSpec((None, n, tm), lambda b, i: (b, 0, i)), jax.ShapeDtypeStruct((bsz, n, t), dt)

    outs = [cols(A_HEADS * LANES, BF16), cols(IDX_HEADS * LANES, BF16), cols(192, F32), rows(LANES, BF16),
            (pl.BlockSpec((None, None, V_ROWS, tm), lambda b, i: (b, i, 0, 0)),
             jax.ShapeDtypeStruct((bsz, nblk, V_ROWS, tm), BF16)),
            cols(768, F32), cols(B_HEADS * LANES, BF16), rows(384, BF16),
            (pl.BlockSpec((None, B_HEADS, None, V_ROWS, tm), lambda b, i: (b, 0, i, 0, 0)),
             jax.ShapeDtypeStruct((bsz, B_HEADS, nblk, V_ROWS, tm), BF16)),
            rows(256, F32), rows(LANES, F32), cols(16, F32), rows(3 * D_MODEL, BF16)]
    scratch = []
    if with_decay:
        outs.append((pl.BlockSpec((None, B_HEADS // 2, tm, LANES), lambda b, i: (b, 0, i, 0)),
                     jax.ShapeDtypeStruct((bsz, B_HEADS // 2, t, LANES), BF16)))
        scratch.append(pltpu.VMEM((SUBLANES, LANES), F32))
    return pl.pallas_call(
        _in_kernel,
        grid=(bsz, nblk),
        scratch_shapes=scratch,
        in_specs=[pl.BlockSpec((None, tm, d), lambda b, i: (b, i, 0)), mod_spec, mod_spec,
                  _const_spec((1, d)), _const_spec(w.shape), _const_spec((1, LANES)),
                  pl.BlockSpec((tm, LANES), lambda b, i: (i, 0)),
                  pl.BlockSpec((tm, LANES), lambda b, i: (i, 0))],
        out_specs=[o[0] for o in outs],
        out_shape=[o[1] for o in outs],
        compiler_params=_cparams(("parallel", "arbitrary" if with_decay else "parallel")),
        name="in_proj",
    )(x, sc, sh, g.reshape(1, d), w, bias, cos, sin)


def _cum_kernel(x_ref, o_ref, carry_ref):
    _decay_pieces(x_ref[...], pl.program_id(1) == 0, o_ref, carry_ref)


def _cum_logf(x, tc):
    bsz, t, n = x.shape
    return pl.pallas_call(
        _cum_kernel,
        grid=(bsz, t // tc),
        in_specs=[pl.BlockSpec((None, tc, n), lambda b, i: (b, i, 0))],
        out_specs=pl.BlockSpec((None, B_HEADS // 2, tc, LANES), lambda b, i: (b, 0, i, 0)),
        out_shape=jax.ShapeDtypeStruct((bsz, B_HEADS // 2, t, LANES), BF16),
        scratch_shapes=[pltpu.VMEM((SUBLANES, LANES), F32)],
        compiler_params=_cparams(("parallel", "arbitrary")),
        name="logf_cumsum",
    )(x)


def _sum_keys(x):
    part = x.reshape(x.shape[0] // SUBLANES, SUBLANES, x.shape[1]).sum(axis=0)
    return jnp.sum(part, axis=0, keepdims=True)


def _max_keys(x):
    part = x.reshape(x.shape[0] // SUBLANES, SUBLANES, x.shape[1]).max(axis=0)
    return jnp.max(part, axis=0, keepdims=True)


def _dsa_kernel(aqt_ref, iqt_ref, smt_ref, kik_ref, avt_ref, o_ref, key_ref,
                *, tq, tk, n_keys, q_pos0, n_sel):
    i = pl.program_id(1)
    pos_first = q_pos0 + i * tq
    last_chunk = (pos_first + tq - 1) // CHUNK
    n_adm = jnp.minimum((last_chunk + 1) * CHUNK, n_keys)
    nkb = (n_adm + tk - 1) // tk

    q_pos = pos_first + _lane_iota((1, tq))
    q_lim = jnp.minimum((q_pos // CHUNK + 1) * CHUNK, n_keys)
    key_row = _row_iota((tk, tq))

    def keys(kb):
        return kik_ref[pl.ds(pl.multiple_of(kb * tk, tk), tk), :]

    iq4 = jnp.concatenate([iqt_ref[hd * LANES:(hd + 1) * LANES, :] for hd in range(IDX_HEADS)], axis=1)
    smt = smt_ref[...]
    w_rows = [smt[IW_LANE + hd:IW_LANE + hd + 1, :] for hd in range(IDX_HEADS)]

    def score_body(kb, carry):
        s4 = jnp.dot(keys(kb), iq4, preferred_element_type=F32)
        score = w_rows[0] * jnp.maximum(s4[:, 0:tq], 0.0)
        for hd in range(1, IDX_HEADS):
            score = score + w_rows[hd] * jnp.maximum(s4[:, hd * tq:(hd + 1) * tq], 0.0)
        score = jnp.where(key_row < q_lim - kb * tk, score, NEG_INF)
        key_ref[kb] = score.astype(KEY_DT)
        return carry

    lax.fori_loop(0, nkb, score_body, 0)

    one, zero = jnp.ones((), KEY_DT), jnp.zeros((), KEY_DT)
    packed_rows = 2 * SUBLANES

    def count(cand, strict):
        def hits(blk):
            cols = []
            for c0 in range(0, tq, 2 * LANES):
                c1 = min(c0 + 2 * LANES, tq)
                h = jnp.where((blk[:, c0:c1] > cand[:, c0:c1]) if strict else (blk[:, c0:c1] >= cand[:, c0:c1]),
                              one, zero)
                parts = [h[r * packed_rows:(r + 1) * packed_rows] for r in range(tk // packed_rows)]
                while len(parts) > 1:
                    parts = [a + b for a, b in zip(parts[::2], parts[1::2])]
                cols.append(parts[0].astype(F32))
            return cols[0] if len(cols) == 1 else jnp.concatenate(cols, axis=1)

        def pair(j, acc):
            return acc + hits(key_ref[2 * j]) + hits(key_ref[2 * j + 1])

        acc = lax.fori_loop(0, nkb // 2, pair, jnp.zeros((packed_rows, tq), F32))
        acc = lax.fori_loop(2 * (nkb // 2), nkb, lambda kb, a: a + hits(key_ref[kb]), acc)
        return jnp.sum(acc, axis=0, keepdims=True)

    def pattern_value(u):
        bits = jnp.where(u >= RANK_ZERO, u - RANK_ZERO, (~u) & 0xFFFF)
        return lax.bitcast_convert_type(lax.shift_left(bits, 16), F32).astype(KEY_DT)

    def bit_body(b, u):
        cand_u = u | lax.shift_left(jnp.int32(1), 15 - b)
        cnt = count(pattern_value(cand_u), False)
        return jnp.where(cnt >= n_sel, cand_u, u)

    u_thr = jnp.maximum(lax.fori_loop(0, 16, bit_body, jnp.zeros((1, tq), I32)), RANK_NEG_INF)
    thr_16 = pattern_value(u_thr)

    def finer():
        lo = thr_16.astype(F32)
        hi = jnp.where(u_thr >= RANK_POS_INF, jnp.inf, pattern_value(u_thr + 1).astype(F32))

        def halve(_, lo_hi):
            lo, hi = lo_hi
            cand = (0.5 * lo + 0.5 * hi).astype(KEY_DT)
            enough = count(cand, False) >= n_sel
            return jnp.where(enough, cand.astype(F32), lo), jnp.where(enough, hi, cand.astype(F32))

        thr_f = lax.fori_loop(0, 16, halve, (lo, hi))[0].astype(KEY_DT)
        return thr_f, count(thr_f, True)

    above_16 = count(thr_16, True)
    thr_k, above = lax.cond(jnp.min(n_sel - above_16) <= 0.0, finer, lambda: (thr_16, above_16))
    need = jnp.maximum(n_sel - above, 0.0)
    thr = thr_k.astype(F32)

    aq6 = jnp.concatenate([aqt_ref[hd * LANES:(hd + 1) * LANES, :] for hd in range(A_HEADS)], axis=1)

    half = tk // 2
    lower = jnp.where(_lane_iota((half, half)) <= _row_iota((half, half)), 1.0, 0.0).astype(BF16)

    def attend():
        def body(kb, carry):
            eq_seen, ms, accs = carry
            blk = key_ref[kb].astype(F32)
            eq = blk == thr
            eq_f = jnp.where(eq, 1.0, 0.0)
            prefs = []
            for e in (eq_f[:half], eq_f[half:]):
                prefs.append(jnp.dot(lower, e.astype(BF16), preferred_element_type=F32) + eq_seen)
                eq_seen = eq_seen + _sum_keys(e)
            slack = jnp.where(blk >= thr, need - jnp.where(eq, jnp.concatenate(prefs, axis=0), 0.0), -1.0)
            bias = jnp.where(slack >= 0.0, jnp.where(jnp.abs(blk) < jnp.inf, 0.0, NEG_INF), NEG_INF)
            logits = jnp.dot(keys(kb), aq6, preferred_element_type=F32)
            vts = (avt_ref[2 * kb], avt_ref[2 * kb + 1])
            new_ms, new_accs = [], []
            for p in range(A_HEADS // 2):
                ps, alphas = [], []
                for hd in (2 * p, 2 * p + 1):
                    lg = logits[:, hd * tq:(hd + 1) * tq] + bias
                    m_old = ms[hd]
                    m_new = jnp.maximum(m_old, _max_keys(lg))
                    m_safe = jnp.where(m_new == NEG_INF, 0.0, m_new)
                    ps.append(jnp.exp2(lg - m_safe).astype(BF16))
                    alphas.append(jnp.exp2(m_old - m_safe))
                    new_ms.append(m_new)
                p2 = jnp.concatenate(ps, axis=1)
                pv = (jnp.dot(vts[0], p2[:tk // 2], preferred_element_type=F32)
                      + jnp.dot(vts[1], p2[tk // 2:], preferred_element_type=F32))
                new_accs.append(jnp.concatenate(alphas, axis=1) * accs[p] + pv)
            return eq_seen, tuple(new_ms), tuple(new_accs)

        init = (jnp.zeros((1, tq), F32),
                tuple(jnp.full((1, tq), NEG_INF, F32) for _ in range(A_HEADS)),
                tuple(jnp.zeros((V_ROWS, 2 * tq), F32) for _ in range(A_HEADS // 2)))
        return lax.fori_loop(0, nkb, body, init)[2]

    accs = attend()

    outs = []
    for p in range(A_HEADS // 2):
        o2 = accs[p][0:HEAD_DIM] / accs[p][HEAD_DIM:HEAD_DIM + 1]
        outs += [o2[:, 0:tq], o2[:, tq:2 * tq]]
    o_ref[...] = jnp.concatenate(outs, axis=0).T.astype(o_ref.dtype)


def _dsa(aqt, iqt, smt, kik, avt, *, tq, n_keys, q_pos0, n_sel):
    bsz, _, t_q = aqt.shape
    _, nblk, _, tkv = avt.shape
    assert nblk % 2 == 0
    tk = 2 * tkv
    kern = functools.partial(_dsa_kernel, tq=tq, tk=tk, n_keys=n_keys, q_pos0=q_pos0, n_sel=n_sel)
    return pl.pallas_call(
        kern,
        grid=(bsz, t_q // tq),
        in_specs=[pl.BlockSpec((None, A_HEADS * LANES, tq), lambda b, i: (b, 0, i)),
                  pl.BlockSpec((None, IDX_HEADS * LANES, tq), lambda b, i: (b, 0, i)),
                  pl.BlockSpec((None, 16, tq), lambda b, i: (b, 0, i)),
                  pl.BlockSpec((None, nblk * tkv, LANES), lambda b, i: (b, 0, 0)),
                  pl.BlockSpec((None, nblk, V_ROWS, tkv), lambda b, i: (b, 0, 0, 0))],
        out_specs=pl.BlockSpec((None, tq, 384), lambda b, i: (b, i, 0)),
        out_shape=jax.ShapeDtypeStruct((bsz, t_q, 384), BF16),
        scratch_shapes=[pltpu.VMEM((nblk // 2, tk, tq), KEY_DT)],
        compiler_params=_cparams(("parallel", "arbitrary")),
        name="dsa_attention",
    )(aqt, iqt, smt, kik, avt)


def _fox_kernel(qt_ref, k_ref, vt_ref, ck_ref, o_ref, *, tq, tk, n_keys, q_pos0):
    i = pl.program_id(1)
    pos_first = q_pos0 + i * tq
    n_full = pos_first // tk
    nkb = (jnp.minimum(pos_first + tq, n_keys) + tk - 1) // tk
    q_pos = pos_first + _lane_iota((1, tq))
    key_row = _row_iota((tk, tq))
    piece_row = _row_iota((LANES, tq))
    qts = []
    for hd in range(B_HEADS):
        minus = jnp.where((piece_row >= 3 * (hd % 2)) & (piece_row < 3 * (hd % 2) + 3), -1.0, 0.0).astype(BF16)
        qts.append(jnp.concatenate([qt_ref[hd * LANES:(hd + 1) * LANES, :], minus], axis=0))

    def step(kbs, state, masked):
        logits = []
        for hd in range(B_HEADS):
            for kb in kbs:
                rows = pl.ds(pl.multiple_of(kb * tk, tk), tk)
                kblk = jnp.concatenate([k_ref[rows, (hd // 2) * LANES:(hd // 2 + 1) * LANES],
                                        ck_ref[hd // 2, rows, :]], axis=1)
                lg = jnp.dot(kblk, qts[hd], preferred_element_type=F32)
                if masked:
                    lg = jnp.where(key_row <= q_pos - kb * tk, lg, NEG_INF)
                logits.append(lg)
        new = []
        for hd in range(B_HEADS):
            m_old, acc = state[hd]
            lgs = logits[hd * len(kbs):(hd + 1) * len(kbs)]
            m_new = m_old
            for lg in lgs:
                m_new = jnp.maximum(m_new, _max_keys(lg))
            m_safe = jnp.where(m_new == NEG_INF, 0.0, m_new) if masked else m_new
            acc = jnp.exp2(m_old - m_safe) * acc
            for kb, lg in zip(kbs, lgs):
                p = jnp.exp2(lg - m_safe).astype(BF16)
                acc = acc + jnp.dot(vt_ref[hd, kb], p, preferred_element_type=F32)
            new.append((m_new, acc))
        return tuple(new)

    init = tuple((jnp.full((1, tq), NEG_INF, F32), jnp.zeros((V_ROWS, tq), F32)) for _ in range(B_HEADS))
    state = lax.fori_loop(0, n_full // 2, lambda j, st: step((2 * j, 2 * j + 1), st, False), init)
    state = lax.fori_loop(2 * (n_full // 2), nkb, lambda kb, st: step((kb,), st, True), state)
    outs = [acc[0:HEAD_DIM] / acc[HEAD_DIM:HEAD_DIM + 1] for _, acc in state]
    o_ref[...] = jnp.concatenate(outs, axis=0).T.astype(o_ref.dtype)


def _fox(qt, k, vt, ck, *, tq, n_keys, q_pos0):
    bsz, _, t_q = qt.shape
    _, _, nblk, _, tk = vt.shape
    lpad = nblk * tk
    kern = functools.partial(_fox_kernel, tq=tq, tk=tk, n_keys=n_keys, q_pos0=q_pos0)
    return pl.pallas_call(
        kern,
        grid=(bsz, t_q // tq),
        in_specs=[pl.BlockSpec((None, B_HEADS * LANES, tq), lambda b, i: (b, 0, i)),
                  pl.BlockSpec((None, lpad, 384), lambda b, i: (b, 0, 0)),
                  pl.BlockSpec((None, B_HEADS, nblk, V_ROWS, tk), lambda b, i: (b, 0, 0, 0, 0)),
                  pl.BlockSpec((None, B_HEADS // 2, lpad, LANES), lambda b, i: (b, 0, 0, 0))],
        out_specs=pl.BlockSpec((None, tq, 384), lambda b, i: (b, i, 0)),
        out_shape=jax.ShapeDtypeStruct((bsz, t_q, 384), BF16),
        compiler_params=_cparams(("parallel", "arbitrary")),
        name="fox_attention",
    )(qt, k, vt, ck)


def _pool_kernel(cur_ref, prev_ref, hist_ref, w_ref, s_ref, o_ref, ext, *, tc, start_pos):
    i = pl.program_id(1)
    cur = cur_ref[...]
    ext[0:16, :] = jnp.where(i == 0, hist_ref[...], prev_ref[tc - 16:, :])
    ext[16:, :] = cur
    pos = start_pos + i * tc + lax.broadcasted_iota(I32, (tc, POOL_WIDTH), 0)
    lane = _lane_iota((tc, POOL_WIDTH))
    run = cur
    pooled = jnp.zeros_like(cur)
    k = 1
    for g, w in enumerate(POOL_WINDOWS):
        while k < w:
            run = run + ext[16 - k:16 - k + tc, :]
            k += 1
        cnt = jnp.minimum(pos + 1, w).astype(F32)
        in_group = (lane >= g * POOL_GROUP_DIM) & (lane < (g + 1) * POOL_GROUP_DIM)
        pooled = jnp.where(in_group, run / cnt, pooled)
    z = (pooled - cur).astype(BF16)
    o_ref[...] = (jnp.dot(z, w_ref[...], preferred_element_type=F32) * s_ref[...]).astype(o_ref.dtype)


def _pool(cu, hist16, w_bd, scale, *, tc, start_pos):
    bsz, t, n = cu.shape
    kern = functools.partial(_pool_kernel, tc=tc, start_pos=start_pos)
    return pl.pallas_call(
        kern,
        grid=(bsz, t // tc),
        in_specs=[pl.BlockSpec((None, tc, n), lambda b, i: (b, i, 0)),
                  pl.BlockSpec((None, tc, n), lambda b, i: (b, jnp.maximum(i - 1, 0), 0)),
                  pl.BlockSpec((None, 16, n), lambda b, i: (b, 0, 0)),
                  _const_spec((n, n)), _const_spec((1, n))],
        out_specs=pl.BlockSpec((None, tc, n), lambda b, i: (b, i, 0)),
        out_shape=jax.ShapeDtypeStruct((bsz, t, n), BF16),
        scratch_shapes=[pltpu.VMEM((16 + tc, n), F32)],
        compiler_params=_cparams(("parallel", "arbitrary")),
        name="pool_mixer",
    )(cu, cu, hist16, w_bd, scale.reshape(1, n))


def _route(logits):
    lane = _lane_iota(logits.shape).astype(F32)
    lg = jnp.where(lane < N_EXPERTS, logits, NEG_INF)
    m1 = jnp.max(lg, axis=1, keepdims=True)
    i1 = jnp.min(jnp.where(lg == m1, lane, float(LANES)), axis=1, keepdims=True)
    hot1 = lane == i1
    lg2 = jnp.where(hot1, NEG_INF, lg)
    m2 = jnp.max(lg2, axis=1, keepdims=True)
    i2 = jnp.min(jnp.where(lg2 == m2, lane, float(LANES)), axis=1, keepdims=True)
    hot2 = lane == i2
    e2 = jnp.exp(m2 - m1)
    den = 1.0 + e2
    return jnp.where(hot1, 1.0 / den, 0.0) + jnp.where(hot2, e2 / den, 0.0)


def _final_norm(x, gain):
    ms = jnp.mean(x * x, axis=-1, keepdims=True)
    return x * lax.rsqrt(ms + NORM_EPS) * gain


def _merge_kernel(x_ref, oa_ref, ob_ref, oc_ref, gate_ref, g1_ref, sc2_ref, sh2_ref, g_ref,
                  wa_ref, wb_ref, wc_ref, wo_ref, *rest, moe, final):
    if moe:
        rw_ref, rb_ref, xo_ref, h_ref, gw_ref = rest
    elif final:
        g2_ref, wg_ref, wu_ref, wd_ref, fg_ref, xo_ref = rest
    else:
        g2_ref, wg_ref, wu_ref, wd_ref, xo_ref = rest
    d = D_MODEL
    merged = (gate_ref[:, 0:d] * jnp.dot(oa_ref[...], wa_ref[...], preferred_element_type=F32)
              + gate_ref[:, d:2 * d] * jnp.dot(ob_ref[...], wb_ref[...], preferred_element_type=F32)
              + gate_ref[:, 2 * d:3 * d] * jnp.dot(oc_ref[...], wc_ref[...], preferred_element_type=F32))
    x = x_ref[...] + g1_ref[...] * jnp.dot(merged.astype(BF16), wo_ref[...], preferred_element_type=F32)
    ms = jnp.mean(x * x, axis=-1, keepdims=True)
    y = x * lax.rsqrt(ms + NORM_EPS) * g_ref[...]
    h = y * (1.0 + sc2_ref[...]) + sh2_ref[...]
    if not moe:
        hb = h.astype(BF16)
        n_chunks = 2
        tf = wg_ref.shape[1] // n_chunks
        acc = jnp.zeros(x.shape, F32)
        for c in range(n_chunks):
            gt = jnp.dot(hb, wg_ref[:, c * tf:(c + 1) * tf], preferred_element_type=F32)
            up = jnp.dot(hb, wu_ref[:, c * tf:(c + 1) * tf], preferred_element_type=F32)
            act = (gt * jax.nn.sigmoid(gt) * up).astype(BF16)
            acc = acc + jnp.dot(act, wd_ref[c * tf:(c + 1) * tf, :], preferred_element_type=F32)
        x = x + g2_ref[...] * acc
        xo_ref[...] = _final_norm(x, fg_ref[...]) if final else x
    else:
        xo_ref[...] = x
        h_ref[...] = h.astype(BF16)
        h_hi = h.astype(BF16)
        h_lo = (h - h_hi.astype(F32)).astype(BF16)
        logits = (jnp.dot(h_hi, rw_ref[0], preferred_element_type=F32)
                  + jnp.dot(h_lo, rw_ref[0], preferred_element_type=F32)
                  + jnp.dot(h_hi, rw_ref[1], preferred_element_type=F32)) + rb_ref[...]
        gw_ref[...] = _route(logits)


def _merge(x, oa, ob, oc, gates, g1, sc2, sh2, g, wa, wb, wc, wo, router, ffn, tm):
    assert (router is None) != (ffn is None)
    bsz, t, d = x.shape
    mrows = g1.shape[1]
    mblk = 1 if mrows == 1 else tm
    mod_spec = pl.BlockSpec((None, mblk, d), (lambda b, i: (b, 0, 0)) if mrows == 1 else (lambda b, i: (b, i, 0)))

    def tok(n):
        return pl.BlockSpec((None, tm, n), lambda b, i: (b, i, 0))

    in_specs = [tok(d), tok(384), tok(384), tok(256), tok(3 * d), mod_spec, mod_spec, mod_spec,
                _const_spec((1, d)), _const_spec(wa.shape), _const_spec(wb.shape), _const_spec(wc.shape),
                _const_spec(wo.shape)]
    args = [x, oa, ob, oc, gates, g1, sc2, sh2, g.reshape(1, d), wa, wb, wc, wo]
    if router is not None:
        rw, rb = router
        in_specs += [_const_spec(rw.shape), _const_spec(rb.shape)]
        args += [rw, rb]
        out_specs = [tok(d), tok(d), tok(LANES)]
        out_shape = [jax.ShapeDtypeStruct((bsz, t, d), F32), jax.ShapeDtypeStruct((bsz, t, d), BF16),
                     jax.ShapeDtypeStruct((bsz, t, LANES), F32)]
    else:
        g2, wg, wu, wd, final_g = ffn
        in_specs += [mod_spec, _const_spec(wg.shape), _const_spec(wu.shape), _const_spec(wd.shape)]
        args += [g2, wg, wu, wd]
        if final_g is not None:
            in_specs.append(_const_spec((1, d)))
            args.append(final_g.reshape(1, d))
        out_specs = [tok(d)]
        out_shape = [jax.ShapeDtypeStruct((bsz, t, d), F32)]
    return pl.pallas_call(
        functools.partial(_merge_kernel, moe=router is not None, final=ffn is not None and ffn[4] is not None),
        grid=(bsz, t // tm),
        in_specs=in_specs, out_specs=out_specs, out_shape=out_shape,
        compiler_params=_cparams(("parallel", "parallel")),
        name="merge_out" if router is not None else "merge_ffn",
    )(*args)


def _moe_kernel(x_ref, h_ref, g2_ref, gw_ref, wg_ref, wu_ref, wd_ref, *rest, final):
    if final:
        fg_ref, o_ref, acc_ref, posc_ref, posr_ref = rest
    else:
        o_ref, acc_ref, posc_ref, posr_ref = rest
    e = pl.program_id(2)
    tm = h_ref.shape[0]
    n_slabs = tm // MOE_CHUNK

    @pl.when(e == 0)
    def _():
        acc_ref[...] = jnp.zeros_like(acc_ref)
        routed = gw_ref[...] != 0.0
        r_f = jnp.where(routed, 1.0, 0.0)
        r_b = r_f.astype(BF16)
        r_t = r_f.T
        r_tb = r_t.astype(BF16)
        tok_l = _lane_iota((MOE_CHUNK, tm))
        tok_r = _row_iota((MOE_CHUNK, tm))
        rank_r = jnp.zeros((LANES, tm), F32)
        for s in range(n_slabs):
            rows = slice(s * MOE_CHUNK, (s + 1) * MOE_CHUNK)
            earlier = jnp.where(tok_l < tok_r + s * MOE_CHUNK, 1.0, 0.0).astype(BF16)
            rank_c = jnp.dot(earlier, r_b, preferred_element_type=F32)
            posc_ref[rows, :] = jnp.where(routed[rows], rank_c, -1.0)
            later = jnp.where(tok_r + s * MOE_CHUNK < tok_l, 1.0, 0.0).astype(BF16)
            rank_r = rank_r + jnp.dot(r_tb[:, rows], later, preferred_element_type=F32)
        posr_ref[...] = jnp.where(r_t != 0.0, rank_r, -1.0)

    lane_e = _lane_iota((tm, LANES)) == e
    pos_c = jnp.sum(jnp.where(lane_e, posc_ref[...], 0.0), axis=1, keepdims=True)
    gate_c = jnp.sum(jnp.where(lane_e, gw_ref[...], 0.0), axis=1, keepdims=True)
    pos_r = posr_ref[pl.ds(e, 1), :]
    n_routed = (jnp.max(pos_r) + 1.0).astype(I32)

    def run_chunk(base, n_rows):
        slot_rows = _row_iota((n_rows, tm)).astype(F32)
        slot_lanes = _lane_iota((MOE_CHUNK, n_rows)).astype(F32)
        pack = jnp.where(pos_r - base == slot_rows, 1.0, 0.0).astype(BF16)
        xc = jnp.dot(pack, h_ref[...], preferred_element_type=F32).astype(BF16)
        gt = jnp.dot(xc, wg_ref[...], preferred_element_type=F32)
        up = jnp.dot(xc, wu_ref[...], preferred_element_type=F32)
        act = (gt * jax.nn.sigmoid(gt) * up).astype(BF16)
        y = jnp.dot(act, wd_ref[...], preferred_element_type=F32).astype(BF16)
        for s in range(n_slabs):
            rows = slice(s * MOE_CHUNK, (s + 1) * MOE_CHUNK)
            unpack = jnp.where(pos_c[rows] - base == slot_lanes, 1.0, 0.0).astype(BF16)
            acc_ref[rows, :] += gate_c[rows] * jnp.dot(unpack, y, preferred_element_type=F32)

    def first(c, carry):
        run_chunk(0.0, MOE_CHUNK)
        return carry

    def later(c, carry):
        run_chunk((MOE_CHUNK + c * (MOE_CHUNK // 2)).astype(F32), MOE_CHUNK // 2)
        return carry

    lax.fori_loop(0, jnp.minimum(n_routed, 1), first, 0)
    n_later = (jnp.maximum(n_routed - MOE_CHUNK, 0) + MOE_CHUNK // 2 - 1) // (MOE_CHUNK // 2)
    lax.fori_loop(0, n_later, later, 0)

    @pl.when(e == pl.num_programs(2) - 1)
    def _():
        x = x_ref[...] + g2_ref[...] * acc_ref[...]
        o_ref[...] = _final_norm(x, fg_ref[...]) if final else x


def _moe(x, h, g2, gw, wg, wu, wd, final_g, tm):
    bsz, t, d = x.shape
    n_e, _, dff = wg.shape
    mrows = g2.shape[1]
    mblk = 1 if mrows == 1 else tm
    mod_spec = pl.BlockSpec((None, mblk, d), (lambda b, i, e: (b, 0, 0)) if mrows == 1 else (lambda b, i, e: (b, i, 0)))
    tok = pl.BlockSpec((None, tm, d), lambda b, i, e: (b, i, 0))
    tok_once = pl.BlockSpec((None, tm, d), lambda b, i, e: (b, i, 0), pipeline_mode=pl.Buffered(1))
    in_specs = [tok_once, tok, mod_spec, pl.BlockSpec((None, tm, LANES), lambda b, i, e: (b, i, 0)),
                pl.BlockSpec((None, d, dff), lambda b, i, e: (e, 0, 0)),
                pl.BlockSpec((None, d, dff), lambda b, i, e: (e, 0, 0)),
                pl.BlockSpec((None, dff, d), lambda b, i, e: (e, 0, 0))]
    args = [x, h, g2, gw, wg, wu, wd]
    if final_g is not None:
        in_specs.append(pl.BlockSpec((1, d), lambda b, i, e: (0, 0)))
        args.append(final_g.reshape(1, d))
    return pl.pallas_call(
        functools.partial(_moe_kernel, final=final_g is not None),
        grid=(bsz, t // tm, n_e),
        in_specs=in_specs, out_specs=tok,
        out_shape=jax.ShapeDtypeStruct((bsz, t, d), F32),
        scratch_shapes=[pltpu.VMEM((tm, d), F32), pltpu.VMEM((tm, LANES), F32), pltpu.VMEM((LANES, tm), F32)],
        compiler_params=_cparams(("parallel", "parallel", "arbitrary")),
        name="moe_routed",
    )(*args)


def _rope_tables(pos):
    inv = ROPE_THETA ** (-jnp.arange(HALF, dtype=F32) / HALF)
    ang = pos.astype(F32)[:, None] * inv[None, :]
    cos, sin = jnp.cos(ang), jnp.sin(ang)
    return jnp.tile(cos, (1, 4)), jnp.tile(jnp.concatenate([-sin, sin], axis=1), (1, 2))


def _pick_tile(n, pref):
    t = min(n, pref)
    while n % t:
        t //= 2
    return t


def _per_seq_cols(a, bsz, t, width):
    f = a.shape[1]
    a = jnp.moveaxis(a[0].reshape(f, bsz, t), 1, 0)
    return jnp.pad(a, ((0, 0), (0, 0), (0, width - t)))


def _value_blocks(past_vt, new_vt, bsz, t, lpad):
    lead = new_vt.shape[:-2]
    new_b = jnp.moveaxis(new_vt.reshape(*lead, V_ROWS, bsz, t), -2, 0)
    full = jnp.concatenate([past_vt, new_b], axis=-1)
    full = jnp.pad(full, [(0, 0)] * (full.ndim - 1) + [(0, lpad - full.shape[-1])])
    full = full.reshape(bsz, *lead, V_ROWS, lpad // KV_BLOCK, KV_BLOCK)
    return jnp.moveaxis(full, -2, -3)


def _with_ones_rows(vt):
    ones = jnp.ones(vt.shape[:-2] + (1, vt.shape[-1]), vt.dtype)
    zeros = jnp.zeros(vt.shape[:-2] + (V_ROWS - HEAD_DIM - 1, vt.shape[-1]), vt.dtype)
    return jnp.concatenate([vt, ones, zeros], axis=-2)


def _mixers(inp, past, n_past, lw, bsz, t):
    aqt, iqt, kik, avt, bqt, bk, bvt, cu, sm, smt, *ck = inp
    n_keys = n_past + t
    n_sel = min(TOPK_MAX, n_keys // 4)
    lpad = -(-n_keys // (2 * KV_BLOCK)) * 2 * KV_BLOCK
    if past is None:
        kik_all, avt_all, bk_all, bvt_all, logf_all = kik, avt, bk, bvt, sm
        hist16 = jnp.zeros((bsz, 16, POOL_WIDTH), F32)
        tq_a, tq_b, t_pad = _pick_tile(t, 512), _pick_tile(t, 256), t
    else:
        pa, pb, plf, pc = past
        pk, pv, pik = (pa[:, :, j].astype(BF16) for j in range(3))

        def join_rows(p, new):
            full = jnp.concatenate([p, new.reshape(bsz, t, new.shape[-1])], axis=1)
            return jnp.pad(full, ((0, 0), (0, lpad - n_keys), (0, 0)))

        kik_all = join_rows(jnp.concatenate([pk, pik], axis=-1), kik)
        avt_all = _value_blocks(_with_ones_rows(jnp.swapaxes(pv, 1, 2)), avt[0, 0], bsz, t, lpad)
        bk_all = join_rows(pb[:, :, 0].astype(BF16).reshape(bsz, n_past, 384), bk)
        pvt = jnp.transpose(pb[:, :, 1].astype(BF16), (0, 2, 3, 1))
        bvt_all = _value_blocks(_with_ones_rows(pvt), bvt[0, :, 0], bsz, t, lpad)
        logf_all = join_rows(jnp.pad(plf, ((0, 0), (0, 0), (0, LANES - B_HEADS))), sm)
        hist16 = jnp.pad(pc, ((0, 0), (1, 0), (0, 0)))
        tq_a = tq_b = t_pad = LANES
        aqt, iqt, bqt, smt = (_per_seq_cols(a, bsz, t, t_pad) for a in (aqt, iqt, bqt, smt))

    oa = _dsa(aqt, iqt, smt, kik_all, avt_all, tq=tq_a, n_keys=n_keys, q_pos0=n_past, n_sel=n_sel)
    ck = ck[0] if ck else _cum_logf(logf_all, 2 * KV_BLOCK)
    ob = _fox(bqt, bk_all, bvt_all, ck, tq=tq_b, n_keys=n_keys, q_pos0=n_past)
    cu = cu.reshape(bsz, t, POOL_WIDTH)
    oc = _pool(cu, hist16, lw["pool_bd"], lw["pool_scale"], tc=_pick_tile(t, 1024), start_pos=n_past)
    return oa[:, :t], ob[:, :t], oc


def _layer(x, mod, past, n_past, pos_tab, lw, layer, final_g, per_token):
    bsz, t, d = x.shape
    sh1, sc1, g1, sh2, sc2, g2 = mod
    if per_token:
        xt = x.reshape(1, bsz * t, d)
        sh1, sc1, g1, sh2, sc2, g2 = (jnp.broadcast_to(m, (bsz, t, d)).reshape(1, bsz * t, d) for m in mod)
        cos, sin = (jnp.tile(a, (bsz, 1)) for a in pos_tab)
    else:
        xt = x
        cos, sin = pos_tab
    tm = KV_BLOCK
    (aqt, iqt, nat, kik, avt, nbt, bqt, bk, bvt, cu, sm, smt, gates, *ck) = _in_proj(
        xt, sc1, sh1, lw["norm_mix_g"], lw["w_in"], lw["bf_bias"], cos, sin, tm, with_decay=past is None)
    oa, ob, oc = _mixers((aqt, iqt, kik, avt, bqt, bk, bvt, cu, sm, smt, *ck), past, n_past, lw, bsz, t)

    def flat(a):
        return a.reshape(xt.shape[0], xt.shape[1], a.shape[-1])

    router = (lw["router_w"], lw["router_b"]) if layer % 2 else None
    ffn = None if layer % 2 else (g2, lw["ffn_wg"], lw["ffn_wu"], lw["ffn_wd"], final_g)
    res = _merge(xt, flat(oa), flat(ob), flat(oc), gates, g1, sc2, sh2, lw["norm_ffn_g"],
                 lw["w_br_a"], lw["w_br_b"], lw["w_br_c"], lw["w_out"], router, ffn, _pick_tile(xt.shape[1], 512))
    if layer % 2 == 0:
        (x_new,) = res
    else:
        x_mid, h2, gw = res
        x_new = _moe(x_mid, h2, g2, gw, lw["moe_wg"], lw["moe_wu"], lw["moe_wd"], final_g,
                     _pick_tile(xt.shape[1], 1024))

    def token_major(a, *feat):
        a = a.reshape(a.shape[0], *feat, -1, t) if per_token else a.reshape(a.shape[0], *feat, 1, t)
        a = jnp.moveaxis(a, (-2, -1), (1, 2))
        return a.reshape(bsz, t, *feat)

    new_a = token_major(nat, 3, HEAD_DIM)
    new_b = token_major(nbt, 2, B_HEADS, HEAD_DIM)
    new_logf = sm.reshape(bsz, t, LANES)[:, :, :B_HEADS]
    new_pool = cu.reshape(bsz, t, POOL_WIDTH)[:, t - POOL_HIST:, :]
    return x_new.reshape(bsz, t, d), (new_a, new_b, new_logf, new_pool)


def kernel(x_prompt, x_sample, cache_a_kvi, cache_b_kv, cache_b_logf, state_c_pool, c_prompt, c_sample,
           ada_w, ada_b, norm_mix_g, w_in, b_forget, pool_w, pool_scale, w_br_a, w_br_b, w_br_c, w_out,
           norm_ffn_g, ffn_w_gate, ffn_w_up, ffn_w_down, moe_router_w, moe_router_b, moe_w_gate,
           moe_w_up, moe_w_down, final_norm_g):
    depth = ada_w.shape[0]
    bp, tp, d = x_prompt.shape
    bs, ts, _ = x_sample.shape
    n_past = cache_a_kvi.shape[2]
    assert tp % KV_BLOCK == 0 and (bs * ts) % KV_BLOCK == 0 and ts <= LANES

    rows = -(-(bp + bs) // 8) * 8
    c_all = jnp.pad(jnp.concatenate([c_prompt, c_sample], axis=0), ((0, rows - bp - bs), (0, 0)))
    mod_all = _ada(c_all, ada_w, ada_b)

    tab_p = _rope_tables(jnp.arange(tp))
    tab_s = _rope_tables(n_past + jnp.arange(ts))

    xp, xs = x_prompt, x_sample
    outs_p, outs_s = [], []
    for layer in range(depth):
        j = layer // 2
        w_l, bias_l = _in_weights(w_in[layer], b_forget[layer])
        pw = pool_w[layer]
        pool_bd = jnp.zeros((POOL_WIDTH, POOL_WIDTH), F32)
        for g in range(len(POOL_WINDOWS)):
            sl = slice(g * POOL_GROUP_DIM, (g + 1) * POOL_GROUP_DIM)
            pool_bd = pool_bd.at[sl, sl].set(pw[g])
        lw = dict(w_in=w_l, bf_bias=bias_l, norm_mix_g=norm_mix_g[layer], norm_ffn_g=norm_ffn_g[layer],
                  pool_bd=pool_bd.astype(BF16), pool_scale=pool_scale[layer],
                  w_br_a=w_br_a[layer].astype(BF16), w_br_b=w_br_b[layer].astype(BF16),
                  w_br_c=w_br_c[layer].astype(BF16), w_out=w_out[layer].astype(BF16))
        if layer % 2 == 0:
            lw.update(ffn_wg=ffn_w_gate[j].astype(BF16), ffn_wu=ffn_w_up[j].astype(BF16),
                      ffn_wd=ffn_w_down[j].astype(BF16))
        else:
            rw = jnp.pad(moe_router_w[j], ((0, 0), (0, LANES - N_EXPERTS)))
            rw_hi = rw.astype(BF16)
            lw.update(router_w=jnp.stack([rw_hi, (rw - rw_hi.astype(F32)).astype(BF16)]),
                      router_b=jnp.pad(moe_router_b[j], (0, LANES - N_EXPERTS)).reshape(1, LANES),
                      moe_wg=moe_w_gate[j].astype(BF16), moe_wu=moe_w_up[j].astype(BF16),
                      moe_wd=moe_w_down[j].astype(BF16))
        final_g = final_norm_g if layer == depth - 1 else None
        mod_p = [m[:, None, :] for m in jnp.split(mod_all[layer, :bp], 6, axis=-1)]
        mod_s = [m[:, None, :] for m in jnp.split(mod_all[layer, bp:bp + bs], 6, axis=-1)]
        xp, new_p = _layer(xp, mod_p, None, 0, tab_p, lw, layer, final_g, per_token=False)
        past = (cache_a_kvi[layer], cache_b_kv[layer], cache_b_logf[layer], state_c_pool[layer])
        xs, new_s = _layer(xs, mod_s, past, n_past, tab_s, lw, layer, final_g, per_token=True)
        outs_p.append(new_p)
        outs_s.append(new_s)

    def stack(outs, k):
        return jnp.stack([o[k] for o in outs])

    return (xp, xs,
            stack(outs_p, 0), stack(outs_p, 1), stack(outs_p, 2), stack(outs_p, 3),
            stack(outs_s, 0), stack(outs_s, 1), stack(outs_s, 2), stack(outs_s, 3))
```
